```python
import jax, jax.numpy as jnp
from jax import lax
import numpy as np

D_MODEL = 1024
BATCH = 8
SEQ = 4096
DEPTH = 1

DN_HEADS = 8
DN_DK = 128
DN_DV = 128
DN_CONV = 4
DN_CHUNK = 64
DIL_GROUPS = ((128, 1), (512, 4), (2048, 16))
DIL_HEADS = 4
DIL_DH = 128
ATT_BLOCK = 128
NORM_EPS = 1e-6

N_DIL = len(DIL_GROUPS)
DN_QK_W = DN_HEADS * DN_DK
DN_V_W = DN_HEADS * DN_DV
DIL_W = DIL_HEADS * DIL_DH
PROJ_SIZES = (DN_QK_W, DN_QK_W, DN_V_W, DN_V_W, DN_HEADS, DN_HEADS,
              N_DIL * DIL_W, N_DIL * DIL_W, N_DIL * DIL_W, DIL_W, D_MODEL, D_MODEL)
PROJ_W = sum(PROJ_SIZES)

kernel_name = "hybrid_deltanet_dilated_alibi_block"


def _rmsnorm(x, w):
    xf = x.astype(jnp.float32)
    y = xf * lax.rsqrt(jnp.mean(xf * xf, axis=-1, keepdims=True) + NORM_EPS)
    return (y * w.astype(jnp.float32)).astype(x.dtype)


def _l2norm(x):
    return x * lax.rsqrt(jnp.sum(x * x, axis=-1, keepdims=True) + NORM_EPS)


def _split_cols(t, sizes):
    out, start = [], 0
    for s in sizes:
        out.append(t[..., start:start + s])
        start += s
    return out


def _causal_conv(u, w):
    K, C = w.shape
    return lax.conv_general_dilated(
        u, w[:, None, :].astype(u.dtype), window_strides=(1,), padding=[(K - 1, 0)],
        dimension_numbers=('NWC', 'WIO', 'NWC'), feature_group_count=C)


def _alibi_slopes(n):
    return 2.0 ** (-8.0 * jnp.arange(1, n + 1, dtype=jnp.float32) / n)


def _gated_delta_rule(q, k, v, beta, g):
    Bn, Sn, H, dk = q.shape
    dv = v.shape[-1]
    C = DN_CHUNK
    N = Sn // C

    def chunk(t):
        t = t.reshape((Bn, N, C, H) + t.shape[3:])
        return jnp.moveaxis(t, 3, 1)

    q = chunk(q) * (dk ** -0.5)
    k, v, beta, g = chunk(k), chunk(v), chunk(beta), chunk(g)
    gc = jnp.cumsum(g, axis=-1)
    causal = jnp.tril(jnp.ones((C, C), dtype=bool))
    strict = jnp.tril(jnp.ones((C, C), dtype=bool), -1)
    gamma = jnp.exp(jnp.where(causal, gc[..., :, None] - gc[..., None, :], -jnp.inf))

    kb = k * beta[..., None]
    a = jnp.einsum('bhnid,bhnjd->bhnij', kb, k) * gamma
    m = jnp.where(strict, a, 0.0) + jnp.eye(C, dtype=a.dtype)
    rhs = jnp.concatenate([v * beta[..., None], kb * jnp.exp(gc)[..., None]], axis=-1)
    sol = lax.linalg.triangular_solve(m, rhs, left_side=True, lower=True, unit_diagonal=True)
    u, w = sol[..., :dv], sol[..., dv:]

    aqk = jnp.einsum('bhnid,bhnjd->bhnij', q, k) * gamma
    qd = q * jnp.exp(gc)[..., None]
    kd = k * jnp.exp(gc[..., -1:] - gc)[..., None]
    dlast = jnp.exp(gc[..., -1])

    def step(state, xs):
        u_n, w_n, aqk_n, qd_n, kd_n, dl_n = xs
        v_new = u_n - jnp.einsum('bhck,bhkv->bhcv', w_n, state)
        o = jnp.einsum('bhck,bhkv->bhcv', qd_n, state) + jnp.einsum('bhij,bhjv->bhiv', aqk_n, v_new)
        state = state * dl_n[..., None, None] + jnp.einsum('bhck,bhcv->bhkv', kd_n, v_new)
        return state, o

    xs = (jnp.moveaxis(u, 2, 0), jnp.moveaxis(w, 2, 0), jnp.moveaxis(aqk, 2, 0),
          jnp.moveaxis(qd, 2, 0), jnp.moveaxis(kd, 2, 0), jnp.moveaxis(dlast, 2, 0))
    s0 = jnp.zeros((Bn, H, dk, dv), jnp.float32)
    _, o = lax.scan(step, s0, xs)
    o = jnp.moveaxis(o, 0, 2)
    return jnp.moveaxis(o, 1, 3).reshape(Bn, Sn, H, dv)


def _dilated_group(q, k, v, window, dilation, slopes):
    Bn, Sn, H, dh = q.shape
    L = Sn // dilation
    span = window // dilation
    nb = -(-L // ATT_BLOCK)
    n_prev = -(-span // ATT_BLOCK)
    Lp = nb * ATT_BLOCK
    KW = (n_prev + 1) * ATT_BLOCK

    def sub(t):
        return jnp.swapaxes(t.reshape(Bn, L, dilation, H, dh), 1, 2)

    qb = jnp.pad(sub(q), ((0, 0), (0, 0), (0, Lp - L), (0, 0), (0, 0)))
    qb = qb.reshape(Bn, dilation, nb, ATT_BLOCK, H, dh)

    def windows(t):
        t = jnp.pad(sub(t), ((0, 0), (0, 0), (n_prev * ATT_BLOCK, Lp - L), (0, 0), (0, 0)))
        t = t.reshape(Bn, dilation, nb + n_prev, ATT_BLOCK, H, dh)
        return jnp.concatenate([t[:, :, j:j + nb] for j in range(n_prev + 1)], axis=3)

    kw, vw = windows(k), windows(v)
    lq = (jnp.arange(nb)[:, None, None] * ATT_BLOCK + jnp.arange(ATT_BLOCK)[None, :, None])
    dist = n_prev * ATT_BLOCK + jnp.arange(ATT_BLOCK)[:, None] - jnp.arange(KW)[None, :]
    valid = (dist >= 0) & (dist <= span) & (lq - dist >= 0)
    alibi = slopes[:, None, None] * (dist * dilation).astype(jnp.float32)[None]

    s = jnp.einsum('bdnqhe,bdnkhe->bdnhqk', qb, kw).astype(jnp.float32) * (dh ** -0.5) - alibi
    s = jnp.where(valid[None, None, :, None], s, -jnp.inf)
    mx = jnp.max(s, axis=-1)
    p = jnp.exp(s - mx[..., None])
    den = jnp.sum(p, axis=-1)
    num = jnp.einsum('bdnhqk,bdnkhe->bdnqhe', p, vw.astype(jnp.float32))

    def back(t):
        t = t.reshape((Bn, dilation, Lp) + t.shape[4:])[:, :, :L]
        return jnp.swapaxes(t, 1, 2).reshape((Bn, Sn) + t.shape[3:])

    return back(num), back(jnp.swapaxes(den, 3, 4)), back(jnp.swapaxes(mx, 3, 4))


def _dilated_attention(q, k, v):
    Bn, Sn, _ = q.shape
    q = q.reshape(Bn, Sn, N_DIL, DIL_HEADS, DIL_DH)
    k = k.reshape(Bn, Sn, N_DIL, DIL_HEADS, DIL_DH)
    v = v.reshape(Bn, Sn, N_DIL, DIL_HEADS, DIL_DH)
    slopes = _alibi_slopes(N_DIL * DIL_HEADS).reshape(N_DIL, DIL_HEADS)
    parts = [_dilated_group(q[:, :, i], k[:, :, i], v[:, :, i], win, dil, slopes[i])
             for i, (win, dil) in enumerate(DIL_GROUPS)]
    m_all = parts[0][2]
    for _, _, mx in parts[1:]:
        m_all = jnp.maximum(m_all, mx)
    num = 0.0
    den = 0.0
    for nm, dn, mx in parts:
        sc = jnp.exp(mx - m_all)
        num = num + nm * sc[..., None]
        den = den + dn * sc
    return (num / den[..., None]).reshape(Bn, Sn, DIL_W)


def _fwd_setup_inputs(seed: int = 0) -> dict:
    key = jax.random.key(seed)
    ks = jax.random.split(key, 12)
    f32 = jnp.float32
    x = jax.random.normal(ks[0], (BATCH, SEQ, D_MODEL), f32)
    norm_w = 1.0 + 0.01 * jax.random.normal(ks[1], (DEPTH, D_MODEL), f32)
    w_in = jax.random.normal(ks[2], (DEPTH, D_MODEL, PROJ_W), f32) * D_MODEL ** -0.5
    conv_w = jax.random.normal(ks[3], (DEPTH, DN_CONV, 2 * DN_QK_W + DN_V_W), f32) * DN_CONV ** -0.5
    a_log = jnp.log(jax.random.uniform(ks[4], (DEPTH, DN_HEADS), f32, 1.0, 16.0))
    dt = jnp.exp(jax.random.uniform(ks[5], (DEPTH, DN_HEADS), f32, np.log(1e-3), np.log(1e-1)))
    dt_bias = dt + jnp.log(-jnp.expm1(-dt))
    dn_norm_w = 1.0 + 0.01 * jax.random.normal(ks[6], (DEPTH, DN_DV), f32)
    w_o_dn = jax.random.normal(ks[7], (DEPTH, DN_V_W, D_MODEL), f32) * DN_V_W ** -0.5
    w_o_dil = jax.random.normal(ks[8], (DEPTH, DIL_W, D_MODEL), f32) * DIL_W ** -0.5
    w_out = jax.random.normal(ks[9], (DEPTH, D_MODEL, D_MODEL), f32) * D_MODEL ** -0.5
    final_norm_w = 1.0 + 0.01 * jax.random.normal(ks[10], (D_MODEL,), f32)
    return {"x": x, "norm_w": norm_w, "w_in": w_in, "conv_w": conv_w, "a_log": a_log,
            "dt_bias": dt_bias, "dn_norm_w": dn_norm_w, "w_o_dn": w_o_dn, "w_o_dil": w_o_dil,
            "w_out": w_out, "final_norm_w": final_norm_w}


def _fwd_reference(x, norm_w, w_in, conv_w, a_log, dt_bias, dn_norm_w, w_o_dn, w_o_dil, w_out, final_norm_w):
    Bn, Sn, _ = x.shape
    f32 = jnp.float32
    for l in range(DEPTH):
        h = _rmsnorm(x, norm_w[l])
        proj = h @ w_in[l]
        (q_a, k_a, v_a, z_a, b_a, a_a, q_b, k_b, v_b, z_b, g_a, g_b) = _split_cols(proj, PROJ_SIZES)

        qkv = jax.nn.silu(_causal_conv(jnp.concatenate([q_a, k_a, v_a], axis=-1), conv_w[l]))
        q_a, k_a, v_a = _split_cols(qkv, (DN_QK_W, DN_QK_W, DN_V_W))
        qh = _l2norm(q_a.reshape(Bn, Sn, DN_HEADS, DN_DK).astype(f32))
        kh = _l2norm(k_a.reshape(Bn, Sn, DN_HEADS, DN_DK).astype(f32))
        vh = v_a.reshape(Bn, Sn, DN_HEADS, DN_DV).astype(f32)
        beta = jax.nn.sigmoid(b_a.astype(f32))
        g = -jnp.exp(a_log[l].astype(f32)) * jax.nn.softplus(a_a.astype(f32) + dt_bias[l].astype(f32))
        o_a = _gated_delta_rule(qh, kh, vh, beta, g)
        o_a = _rmsnorm(o_a, dn_norm_w[l]) * jax.nn.silu(z_a.reshape(Bn, Sn, DN_HEADS, DN_DV).astype(f32))
        y_a = o_a.reshape(Bn, Sn, DN_V_W).astype(x.dtype) @ w_o_dn[l]

        o_b = _dilated_attention(q_b, k_b, v_b) * jax.nn.silu(z_b.astype(f32))
        y_b = o_b.astype(x.dtype) @ w_o_dil[l]

        merged = jax.nn.sigmoid(g_a) * y_a + jax.nn.sigmoid(g_b) * y_b
        x = x + merged @ w_out[l]
    return _rmsnorm(x, final_norm_w)


import jax as _jax
import jax.numpy as _jnp

TWIN_FORMAT = 'train_step'
FWD_PARAMS = ['x', 'norm_w', 'w_in', 'conv_w', 'a_log', 'dt_bias', 'dn_norm_w', 'w_o_dn', 'w_o_dil', 'w_out', 'final_norm_w']
TWIN_WEIGHTS = ['norm_w', 'w_in', 'conv_w', 'a_log', 'dt_bias', 'dn_norm_w', 'w_o_dn', 'w_o_dil', 'w_out', 'final_norm_w']
TWIN_DIFF_INPUT = 'x'
TWIN_INPUTS = ['x', 'norm_w', 'w_in', 'conv_w', 'a_log', 'dt_bias', 'dn_norm_w', 'w_o_dn', 'w_o_dil', 'w_out', 'final_norm_w', 'loss_target', 'm_norm_w', 'm_w_in', 'm_conv_w', 'm_a_log', 'm_dt_bias', 'm_dn_norm_w', 'm_w_o_dn', 'm_w_o_dil', 'm_w_out', 'm_final_norm_w', 'v_norm_w', 'v_w_in', 'v_conv_w', 'v_a_log', 'v_dt_bias', 'v_dn_norm_w', 'v_w_o_dn', 'v_w_o_dil', 'v_w_out', 'v_final_norm_w']
TWIN_OUTPUTS = ['loss', 'grad_x', 'grad_norm_w', 'grad_w_in', 'grad_conv_w', 'grad_a_log', 'grad_dt_bias', 'grad_dn_norm_w', 'grad_w_o_dn', 'grad_w_o_dil', 'grad_w_out', 'grad_final_norm_w', 'delta_norm_w', 'delta_w_in', 'delta_conv_w', 'delta_a_log', 'delta_dt_bias', 'delta_dn_norm_w', 'delta_w_o_dn', 'delta_w_o_dil', 'delta_w_out', 'delta_final_norm_w', 'new_m_norm_w', 'new_m_w_in', 'new_m_conv_w', 'new_m_a_log', 'new_m_dt_bias', 'new_m_dn_norm_w', 'new_m_w_o_dn', 'new_m_w_o_dil', 'new_m_w_out', 'new_m_final_norm_w', 'new_v_norm_w', 'new_v_w_in', 'new_v_conv_w', 'new_v_a_log', 'new_v_dt_bias', 'new_v_dn_norm_w', 'new_v_w_o_dn', 'new_v_w_o_dil', 'new_v_w_out', 'new_v_final_norm_w']
TWIN_LEAF_KINDS = {'loss': 'loss', 'grad_x': 'grad_x', 'grad_norm_w': 'grad_w', 'grad_w_in': 'grad_w', 'grad_conv_w': 'grad_w', 'grad_a_log': 'grad_w', 'grad_dt_bias': 'grad_w', 'grad_dn_norm_w': 'grad_w', 'grad_w_o_dn': 'grad_w', 'grad_w_o_dil': 'grad_w', 'grad_w_out': 'grad_w', 'grad_final_norm_w': 'grad_w', 'delta_norm_w': 'delta_w', 'delta_w_in': 'delta_w', 'delta_conv_w': 'delta_w', 'delta_a_log': 'delta_w', 'delta_dt_bias': 'delta_w', 'delta_dn_norm_w': 'delta_w', 'delta_w_o_dn': 'delta_w', 'delta_w_o_dil': 'delta_w', 'delta_w_out': 'delta_w', 'delta_final_norm_w': 'delta_w', 'new_m_norm_w': 'new_m', 'new_m_w_in': 'new_m', 'new_m_conv_w': 'new_m', 'new_m_a_log': 'new_m', 'new_m_dt_bias': 'new_m', 'new_m_dn_norm_w': 'new_m', 'new_m_w_o_dn': 'new_m', 'new_m_w_o_dil': 'new_m', 'new_m_w_out': 'new_m', 'new_m_final_norm_w': 'new_m', 'new_v_norm_w': 'new_v', 'new_v_w_in': 'new_v', 'new_v_conv_w': 'new_v', 'new_v_a_log': 'new_v', 'new_v_dt_bias': 'new_v', 'new_v_dn_norm_w': 'new_v', 'new_v_w_o_dn': 'new_v', 'new_v_w_o_dil': 'new_v', 'new_v_w_out': 'new_v', 'new_v_final_norm_w': 'new_v'}


def _forward(args):
    return _fwd_reference(*[args[k] for k in FWD_PARAMS])


def _output_shape():
    out = _jax.eval_shape(lambda: _forward(_fwd_setup_inputs(0)))
    return out.shape, out.dtype

N_MICROBATCH = 1
ADAM_LR = 0.001
ADAM_B1 = 0.9
ADAM_B2 = 0.999
ADAM_EPS = 1e-08
ADAM_WD = 0.01
ADAM_STEP = 10
PER_EXAMPLE_BATCH_AXIS = {'x': 0, 'loss_target': 0}
SHARED_INPUTS = []
_WEIGHT_DTYPES = {'norm_w': _jnp.float32, 'w_in': _jnp.float32, 'conv_w': _jnp.float32, 'a_log': _jnp.float32, 'dt_bias': _jnp.float32, 'dn_norm_w': _jnp.float32, 'w_o_dn': _jnp.float32, 'w_o_dil': _jnp.float32, 'w_out': _jnp.float32, 'final_norm_w': _jnp.float32}
MOMENT_SCALE = {'norm_w': 1.004392e-01, 'w_in': 2.932285e-02, 'conv_w': 4.108626e-02, 'a_log': 2.715422e-01, 'dt_bias': 2.570587e-01, 'dn_norm_w': 1.562482e-01, 'w_o_dn': 5.360429e-02, 'w_o_dil': 1.780551e-02, 'w_out': 5.529725e-02, 'final_norm_w': 3.200738e+01}


def _to_microbatches(a, axis):
    t = _jnp.moveaxis(a, axis, 0)
    t = t.reshape((N_MICROBATCH, t.shape[0] // N_MICROBATCH) + t.shape[1:])
    return _jnp.moveaxis(t, 1, axis + 1)


def setup_inputs(seed: int = 0) -> dict:
    inp = _fwd_setup_inputs(seed)
    key = _jax.random.fold_in(_jax.random.key(seed), 7919)
    shape, _ = _output_shape()
    out = dict(inp)
    out["loss_target"] = _jax.random.normal(_jax.random.fold_in(key, 0), shape, _jnp.float32)
    for i, name in enumerate(TWIN_WEIGHTS):
        w = inp[name].astype(_jnp.float32)
        if MOMENT_SCALE is None:
            s = _jnp.sqrt(_jnp.mean(_jnp.square(w)) + 1e-30)
        else:
            s = MOMENT_SCALE[name]
        km, kv = _jax.random.split(_jax.random.fold_in(key, i + 1))
        out[name] = w
        out["m_" + name] = s * _jax.random.normal(km, w.shape, _jnp.float32)
        out["v_" + name] = (s * s) * _jax.random.uniform(kv, w.shape, _jnp.float32, 0.5, 1.5)
    if N_MICROBATCH > 1:
        for name, axis in PER_EXAMPLE_BATCH_AXIS.items():
            out[name] = _to_microbatches(out[name], axis)
    return {'x': out['x'], 'norm_w': out['norm_w'], 'w_in': out['w_in'], 'conv_w': out['conv_w'], 'a_log': out['a_log'], 'dt_bias': out['dt_bias'], 'dn_norm_w': out['dn_norm_w'], 'w_o_dn': out['w_o_dn'], 'w_o_dil': out['w_o_dil'], 'w_out': out['w_out'], 'final_norm_w': out['final_norm_w'], 'loss_target': out['loss_target'], 'm_norm_w': out['m_norm_w'], 'm_w_in': out['m_w_in'], 'm_conv_w': out['m_conv_w'], 'm_a_log': out['m_a_log'], 'm_dt_bias': out['m_dt_bias'], 'm_dn_norm_w': out['m_dn_norm_w'], 'm_w_o_dn': out['m_w_o_dn'], 'm_w_o_dil': out['m_w_o_dil'], 'm_w_out': out['m_w_out'], 'm_final_norm_w': out['m_final_norm_w'], 'v_norm_w': out['v_norm_w'], 'v_w_in': out['v_w_in'], 'v_conv_w': out['v_conv_w'], 'v_a_log': out['v_a_log'], 'v_dt_bias': out['v_dt_bias'], 'v_dn_norm_w': out['v_dn_norm_w'], 'v_w_o_dn': out['v_w_o_dn'], 'v_w_o_dil': out['v_w_o_dil'], 'v_w_out': out['v_w_out'], 'v_final_norm_w': out['v_final_norm_w']}


def _loss(weights, diff, rest, loss_target):
    with _jax.named_scope("forward"):
        args = {**rest, TWIN_DIFF_INPUT: diff, **{k: w.astype(_WEIGHT_DTYPES[k]) for k, w in weights.items()}}
        y = _forward(args)
    with _jax.named_scope("loss_head"):
        err = _jnp.square(y.astype(_jnp.float32) - loss_target)
        return 0.5 * _jnp.sum(_jnp.mean(err, axis=-1)) if err.ndim else 0.5 * err


def _adamw(w, g, m, v):
    m = ADAM_B1 * m + (1.0 - ADAM_B1) * g
    v = ADAM_B2 * v + (1.0 - ADAM_B2) * _jnp.square(g)
    m_hat = m / (1.0 - ADAM_B1 ** ADAM_STEP)
    v_hat = v / (1.0 - ADAM_B2 ** ADAM_STEP)
    delta = -ADAM_LR * (m_hat / (_jnp.sqrt(v_hat) + ADAM_EPS) + ADAM_WD * w)
    return delta, m, v


def reference(x, norm_w, w_in, conv_w, a_log, dt_bias, dn_norm_w, w_o_dn, w_o_dil, w_out, final_norm_w, loss_target, m_norm_w, m_w_in, m_conv_w, m_a_log, m_dt_bias, m_dn_norm_w, m_w_o_dn, m_w_o_dil, m_w_out, m_final_norm_w, v_norm_w, v_w_in, v_conv_w, v_a_log, v_dt_bias, v_dn_norm_w, v_w_o_dn, v_w_o_dil, v_w_out, v_final_norm_w):
    given = dict(x=x, norm_w=norm_w, w_in=w_in, conv_w=conv_w, a_log=a_log, dt_bias=dt_bias, dn_norm_w=dn_norm_w, w_o_dn=w_o_dn, w_o_dil=w_o_dil, w_out=w_out, final_norm_w=final_norm_w, loss_target=loss_target, m_norm_w=m_norm_w, m_w_in=m_w_in, m_conv_w=m_conv_w, m_a_log=m_a_log, m_dt_bias=m_dt_bias, m_dn_norm_w=m_dn_norm_w, m_w_o_dn=m_w_o_dn, m_w_o_dil=m_w_o_dil, m_w_out=m_w_out, m_final_norm_w=m_final_norm_w, v_norm_w=v_norm_w, v_w_in=v_w_in, v_conv_w=v_conv_w, v_a_log=v_a_log, v_dt_bias=v_dt_bias, v_dn_norm_w=v_dn_norm_w, v_w_o_dn=v_w_o_dn, v_w_o_dil=v_w_o_dil, v_w_out=v_w_out, v_final_norm_w=v_final_norm_w)
    weights = {n: given[n] for n in TWIN_WEIGHTS}
    shared = {n: given[n] for n in SHARED_INPUTS}
    per_example = {n: given[n] for n in ['x']}
    grad_fn = _jax.value_and_grad(_loss, argnums=(0, 1))

    def one_microbatch(ex, loss_target):
        ex = dict(ex)
        diff = ex.pop(TWIN_DIFF_INPUT)
        return grad_fn(weights, diff, {**shared, **ex}, loss_target)

    if N_MICROBATCH == 1:
        loss, (grad_w, grad_x) = one_microbatch(per_example, given["loss_target"])
    else:
        def body(carry, xs):
            loss_sum, grad_sum = carry
            l_k, (gw_k, gx_k) = one_microbatch(xs[0], xs[1])
            with _jax.named_scope("update"):
                return (loss_sum + l_k, _jax.tree.map(_jnp.add, grad_sum, gw_k)), gx_k

        init = (_jnp.zeros((), _jnp.float32), _jax.tree.map(_jnp.zeros_like, weights))
        (loss, grad_w), grad_x = _jax.lax.scan(body, init, (per_example, given["loss_target"]))
    with _jax.named_scope("update"):
        delta_w, new_m, new_v = {}, {}, {}
        for n in TWIN_WEIGHTS:
            delta_w[n], new_m[n], new_v[n] = _adamw(weights[n], grad_w[n], given["m_" + n], given["v_" + n])
    return (loss, grad_x, *[grad_w[n] for n in TWIN_WEIGHTS], *[delta_w[n] for n in TWIN_WEIGHTS],
            *[new_m[n] for n in TWIN_WEIGHTS], *[new_v[n] for n in TWIN_WEIGHTS])
```

```python
import functools
import math

import jax
import jax.numpy as jnp
from jax import lax
from jax.experimental import pallas as pl
from jax.experimental.pallas import tpu as pltpu

F32 = jnp.float32
MXU = jnp.bfloat16
MESH = pl.DeviceIdType.MESH

N_DEV = 8
D_MODEL = 1024
DN_HEADS = 8
DN_DK = 128
DN_CHUNK = 64
N_DIL = 3
DIL_HEADS = 4
DIL_DH = 128
DIL_W = DIL_HEADS * DIL_DH
DIL_GROUPS = ((128, 1), (512, 4), (2048, 16))
ATT_BLOCK = 128
NORM_EPS = 1e-6
QKV_W = 3 * D_MODEL
DILQ_W = N_DIL * DIL_W
PROJ_SIZES = (1024, 1024, 1024, 1024, 8, 8, DILQ_W, DILQ_W, DILQ_W, DIL_W, D_MODEL, D_MODEL)

ADAM_LR = 0.001
ADAM_B1 = 0.9
ADAM_B2 = 0.999
ADAM_EPS = 1e-08
ADAM_WD = 0.01
ADAM_STEP = 10

ROW_TILE = 256
LANES = 128
SUBLANES = 8
VMEM_LIMIT = 48 << 20


def _pcall(body, **kw):
    return pl.pallas_call(body, **kw)


def _params(*sem):
    return pltpu.CompilerParams(dimension_semantics=tuple(sem), vmem_limit_bytes=VMEM_LIMIT)


def _sigmoid(x):
    return 1.0 / (1.0 + jnp.exp(-x))


def _softplus(x):
    return jnp.maximum(x, 0.0) + jnp.log(1.0 + jnp.exp(-jnp.abs(x)))


def _dot(a, b):
    return jnp.dot(a.astype(MXU), b.astype(MXU), preferred_element_type=F32)


def _dot_nt(a, b):
    return lax.dot_general(a.astype(MXU), b.astype(MXU), (((1,), (1,)), ((), ())), preferred_element_type=F32)


def _dot_tn(a, b):
    return lax.dot_general(a.astype(MXU), b.astype(MXU), (((0,), (0,)), ((), ())), preferred_element_type=F32)


def _split3(x):
    hi = x.astype(jnp.bfloat16)
    r1 = x - hi.astype(F32)
    mid = r1.astype(jnp.bfloat16)
    lo = (r1 - mid.astype(F32)).astype(jnp.bfloat16)
    return hi, mid, lo


def _dot01(m01, x):
    m = m01.astype(jnp.bfloat16)
    hi, mid, lo = _split3(x)
    f = lambda p: jnp.dot(m, p, preferred_element_type=F32)
    return f(hi) + (f(mid) + f(lo))


def _dot01_tn(x, m01):
    m = m01.astype(jnp.bfloat16)
    hi, mid, lo = _split3(x)
    f = lambda p: lax.dot_general(p, m, (((0,), (0,)), ((), ())), preferred_element_type=F32)
    return f(hi) + (f(mid) + f(lo))


def _dot3(a, b):
    ah = a.astype(jnp.bfloat16)
    al = (a - ah.astype(F32)).astype(jnp.bfloat16)
    bh = b.astype(jnp.bfloat16)
    bl = (b - bh.astype(F32)).astype(jnp.bfloat16)
    f = lambda p, q: jnp.dot(p, q, preferred_element_type=F32)
    return f(ah, bh) + (f(ah, bl) + f(al, bh))


def _rows_call(body, name, n_rows, ins, outs, scratch=(), tm=ROW_TILE):
    steps = n_rows // tm
    per8 = tm // SUBLANES
    last8 = n_rows // SUBLANES - 1
    in_specs = []
    for arr, kind in ins:
        cols = arr.shape[-1]
        if kind == "tile":
            in_specs.append(pl.BlockSpec((tm, cols), lambda i: (i, 0)))
        elif kind == "full":
            in_specs.append(pl.BlockSpec(arr.shape, lambda i, nd=arr.ndim: (0,) * nd))
        elif kind == "prev8":
            in_specs.append(pl.BlockSpec((SUBLANES, cols), lambda i: (jnp.maximum(i * per8 - 1, 0), 0)))
        elif kind == "next8":
            in_specs.append(pl.BlockSpec((SUBLANES, cols), lambda i: (jnp.minimum((i + 1) * per8, last8), 0)))
        else:
            raise ValueError(kind)
    out_specs, out_shape, has_acc = [], [], False
    for shape, dtype, kind in outs:
        out_shape.append(jax.ShapeDtypeStruct(shape, dtype))
        if kind == "tile":
            out_specs.append(pl.BlockSpec((tm, shape[-1]), lambda i: (i, 0)))
        else:
            has_acc = True
            out_specs.append(pl.BlockSpec(shape, lambda i: (0, 0)))
    return _pcall(
        body, name=name, grid=(steps,), in_specs=in_specs, out_specs=out_specs, out_shape=out_shape,
        scratch_shapes=list(scratch),
        compiler_params=_params("arbitrary" if has_acc else "parallel"),
    )(*[a for a, _ in ins])


def _acc_add(ref, value):
    @pl.when(pl.program_id(0) == 0)
    def _():
        ref[...] = jnp.zeros_like(ref)
    ref[...] += value


def _col_chunks(n, width=512):
    return [(c, min(width, n - c)) for c in range(0, n, width)]


def _mm_nn(a, ws, name, out_dtype=F32, tm=ROW_TILE):
    m, k = a.shape
    ns = [w.shape[1] for w in ws]

    def body(a_ref, *refs):
        av = a_ref[...]
        for w_ref, o_ref, n in zip(refs[:len(ws)], refs[len(ws):], ns):
            for c, wd in _col_chunks(n):
                o_ref[:, c:c + wd] = jnp.dot(av, w_ref[:, c:c + wd], preferred_element_type=F32).astype(o_ref.dtype)

    return _pcall(
        body, name=name, grid=(m // tm,),
        in_specs=[pl.BlockSpec((tm, k), lambda i: (i, 0))] + [pl.BlockSpec((k, n), lambda i: (0, 0)) for n in ns],
        out_specs=[pl.BlockSpec((tm, n), lambda i: (i, 0)) for n in ns],
        out_shape=[jax.ShapeDtypeStruct((m, n), out_dtype) for n in ns],
        compiler_params=_params("parallel"),
    )(a, *ws)


def _mm_nt_sum(ds, ws, name, tm=ROW_TILE):
    m = ds[0].shape[0]
    k = ws[0].shape[0]
    ns = [d.shape[1] for d in ds]

    def body(*refs):
        d_refs, w_refs, o_ref = refs[:len(ds)], refs[len(ds):2 * len(ds)], refs[-1]
        first = True
        for d_ref, w_ref, n in zip(d_refs, w_refs, ns):
            for c, wd in _col_chunks(n, 1024):
                part = lax.dot_general(d_ref[:, c:c + wd], w_ref[:, c:c + wd], (((1,), (1,)), ((), ())),
                                       preferred_element_type=F32)
                if first:
                    o_ref[...] = part
                    first = False
                else:
                    o_ref[...] += part

    return _pcall(
        body, name=name, grid=(m // tm,),
        in_specs=[pl.BlockSpec((tm, n), lambda i: (i, 0)) for n in ns] + [pl.BlockSpec((k, n), lambda i: (0, 0)) for n in ns],
        out_specs=pl.BlockSpec((tm, k), lambda i: (i, 0)),
        out_shape=jax.ShapeDtypeStruct((m, k), F32),
        compiler_params=_params("parallel"),
    )(*ds, *ws)


def _mm_tn(a, d, name, tm=512):
    m, k = a.shape
    n = d.shape[1]
    tn = 1024 if n % 1024 == 0 else (512 if n % 512 == 0 else n)

    def body(a_ref, d_ref, o_ref):
        @pl.when(pl.program_id(1) == 0)
        def _():
            o_ref[...] = jnp.zeros_like(o_ref)
        o_ref[...] += lax.dot_general(a_ref[...], d_ref[...], (((0,), (0,)), ((), ())), preferred_element_type=F32)

    return _pcall(
        body, name=name, grid=(n // tn, m // tm),
        in_specs=[pl.BlockSpec((tm, k), lambda i, j: (j, 0)), pl.BlockSpec((tm, tn), lambda i, j: (j, i))],
        out_specs=pl.BlockSpec((k, tn), lambda i, j: (0, i)),
        out_shape=jax.ShapeDtypeStruct((k, n), F32),
        compiler_params=_params("parallel", "arbitrary"),
    )(a, d)


def _rms_in_fwd(x, norm_w):
    def body(x_ref, w_ref, h_ref):
        xv = x_ref[...]
        r = lax.rsqrt(jnp.mean(xv * xv, axis=-1, keepdims=True) + NORM_EPS)
        h_ref[...] = (xv * r * w_ref[...]).astype(h_ref.dtype)

    return _rows_call(body, "rms_in_fwd", x.shape[0], [(x, "tile"), (norm_w, "full")],
                      [(x.shape, MXU, "tile")])[0]


def _rms_in_bwd(x, dh, dx2, norm_w):
    def body(x_ref, dh_ref, dx2_ref, w_ref, dx_ref, dw_ref):
        xv = x_ref[...]
        r = lax.rsqrt(jnp.mean(xv * xv, axis=-1, keepdims=True) + NORM_EPS)
        dhv = dh_ref[...]
        dn = dhv * w_ref[...]
        dx_ref[...] = dx2_ref[...] + r * dn - xv * (r * r * r) * jnp.mean(dn * xv, axis=-1, keepdims=True)
        row = jnp.sum(dhv * xv * r, axis=0, keepdims=True)
        _acc_add(dw_ref, jnp.concatenate([row, jnp.zeros((SUBLANES - 1, row.shape[1]), F32)], axis=0))

    return _rows_call(body, "rms_in_bwd", x.shape[0],
                      [(x, "tile"), (dh, "tile"), (dx2, "tile"), (norm_w, "full")],
                      [(x.shape, F32, "tile"), ((SUBLANES, x.shape[1]), F32, "acc")])


def _conv_taps(ext_ref, cw_ref, cols, tm):
    c = None
    for j in range(4):
        term = cw_ref[3 - j:4 - j, cols] * ext_ref[SUBLANES - j:SUBLANES - j + tm, cols]
        c = term if c is None else c + term
    return c


def _fill_ext(ext_ref, u_ref, halo_ref, first):
    ext_ref[0:SUBLANES, :] = jnp.where(first, 0.0, halo_ref[...])
    ext_ref[SUBLANES:, :] = u_ref[...]


def _dn_prep_fwd(qkv_pre, ba, conv_w8, alog_row, dtb_row):
    s = qkv_pre.shape[0]
    tm = ROW_TILE

    def body(u_ref, halo_ref, cw_ref, ba_ref, al_ref, dtb_ref, q_ref, k_ref, v_ref, bg_ref, ext_ref):
        _fill_ext(ext_ref, u_ref, halo_ref, pl.program_id(0) == 0)
        for h in range(3 * DN_HEADS):
            cols = slice(h * LANES, (h + 1) * LANES)
            c = _conv_taps(ext_ref, cw_ref, cols, tm)
            a = c * _sigmoid(c)
            oc = slice((h % DN_HEADS) * LANES, (h % DN_HEADS + 1) * LANES)
            if h < 2 * DN_HEADS:
                rinv = lax.rsqrt(jnp.sum(a * a, axis=-1, keepdims=True) + NORM_EPS)
                if h < DN_HEADS:
                    q_ref[:, oc] = a * (rinv * DN_DK ** -0.5)
                else:
                    k_ref[:, oc] = a * rinv
            else:
                v_ref[:, oc] = a
        bav = ba_ref[...]
        lane = lax.broadcasted_iota(jnp.int32, bav.shape, 1)
        beta = _sigmoid(bav)
        g = -jnp.exp(al_ref[...]) * _softplus(bav + dtb_ref[...])
        bg_ref[...] = jnp.where(lane < DN_HEADS, beta, jnp.where(lane < 2 * DN_HEADS, g, 0.0))

    return _rows_call(
        body, "dn_prep_fwd", s,
        [(qkv_pre, "tile"), (qkv_pre, "prev8"), (conv_w8, "full"), (ba, "tile"), (alog_row, "full"), (dtb_row, "full")],
        [((s, D_MODEL), F32, "tile")] * 3 + [((s, LANES), F32, "tile")],
        scratch=[pltpu.VMEM((tm + SUBLANES, QKV_W), F32)])


def _dn_prep_bwd(qkv_pre, ba, conv_w8, alog_row, dtb_row, dq, dk, dv, dbg):
    s = qkv_pre.shape[0]
    tm = ROW_TILE

    def body(u_ref, halo_ref, cw_ref, ba_ref, al_ref, dtb_ref, dq_ref, dk_ref, dv_ref, dbg_ref,
             dc_ref, dba_ref, dsmall_ref, ext_ref):
        _fill_ext(ext_ref, u_ref, halo_ref, pl.program_id(0) == 0)
        for h in range(3 * DN_HEADS):
            cols = slice(h * LANES, (h + 1) * LANES)
            oc = slice((h % DN_HEADS) * LANES, (h % DN_HEADS + 1) * LANES)
            c = _conv_taps(ext_ref, cw_ref, cols, tm)
            sg = _sigmoid(c)
            a = c * sg
            if h < 2 * DN_HEADS:
                rinv = lax.rsqrt(jnp.sum(a * a, axis=-1, keepdims=True) + NORM_EPS)
                dy = dq_ref[:, oc] * DN_DK ** -0.5 if h < DN_HEADS else dk_ref[:, oc]
                da = rinv * dy - a * (rinv * rinv * rinv) * jnp.sum(dy * a, axis=-1, keepdims=True)
            else:
                da = dv_ref[:, oc]
            dc_ref[:, cols] = da * (sg * (1.0 + c * (1.0 - sg)))
        bav = ba_ref[...]
        dbgv = dbg_ref[...]
        lane = lax.broadcasted_iota(jnp.int32, bav.shape, 1)
        beta = _sigmoid(bav)
        ea = jnp.exp(al_ref[...])
        z = bav + dtb_ref[...]
        g = -ea * _softplus(z)
        is_b = lane < DN_HEADS
        is_g = jnp.logical_and(lane >= DN_HEADS, lane < 2 * DN_HEADS)
        d_aa = jnp.where(is_g, dbgv * (-ea) * _sigmoid(z), 0.0)
        dba = jnp.where(is_b, dbgv * beta * (1.0 - beta), d_aa)
        dba_ref[...] = dba.astype(dba_ref.dtype)
        r_alog = jnp.sum(jnp.where(is_g, dbgv * g, 0.0), axis=0, keepdims=True)
        r_dtb = jnp.sum(d_aa, axis=0, keepdims=True)
        _acc_add(dsmall_ref, jnp.concatenate([r_alog, r_dtb, jnp.zeros((SUBLANES - 2, LANES), F32)], axis=0))

    return _rows_call(
        body, "dn_prep_bwd", s,
        [(qkv_pre, "tile"), (qkv_pre, "prev8"), (conv_w8, "full"), (ba, "tile"), (alog_row, "full"), (dtb_row, "full"),
         (dq, "tile"), (dk, "tile"), (dv, "tile"), (dbg, "tile")],
        [((s, QKV_W), F32, "tile"), ((s, LANES), MXU, "tile"), ((SUBLANES, LANES), F32, "acc")],
        scratch=[pltpu.VMEM((tm + SUBLANES, QKV_W), F32)])


def _conv_bwd(dc, qkv_pre, conv_w8):
    s = dc.shape[0]
    tm = ROW_TILE
    steps = s // tm

    def body(dc_ref, dnext_ref, u_ref, halo_ref, cw_ref, du_ref, dcw_ref, extd_ref, ext_ref):
        i = pl.program_id(0)
        _fill_ext(ext_ref, u_ref, halo_ref, i == 0)
        extd_ref[0:tm, :] = dc_ref[...]
        extd_ref[tm:, :] = jnp.where(i == steps - 1, 0.0, dnext_ref[...])

        @pl.when(i == 0)
        def _():
            dcw_ref[...] = jnp.zeros_like(dcw_ref)

        for h in range(3 * DN_HEADS):
            cols = slice(h * LANES, (h + 1) * LANES)
            du = None
            for j in range(4):
                term = cw_ref[3 - j:4 - j, cols] * extd_ref[j:j + tm, cols]
                du = term if du is None else du + term
            du_ref[:, cols] = du.astype(du_ref.dtype)
            dcv = dc_ref[:, cols]
            for j in range(4):
                row = jnp.sum(dcv * ext_ref[SUBLANES - j:SUBLANES - j + tm, cols], axis=0, keepdims=True)
                dcw_ref[3 - j:4 - j, cols] += row

    return _rows_call(
        body, "conv_bwd", s,
        [(dc, "tile"), (dc, "next8"), (qkv_pre, "tile"), (qkv_pre, "prev8"), (conv_w8, "full")],
        [((s, QKV_W), MXU, "tile"), ((SUBLANES, QKV_W), F32, "acc")],
        scratch=[pltpu.VMEM((tm + SUBLANES, QKV_W), F32), pltpu.VMEM((tm + SUBLANES, QKV_W), F32)])


def _dn_post_fwd(o, z, dnw_row):
    def body(o_ref, z_ref, w_ref, on_ref):
        for h in range(DN_HEADS):
            cols = slice(h * LANES, (h + 1) * LANES)
            ov = o_ref[:, cols]
            zv = z_ref[:, cols]
            ro = lax.rsqrt(jnp.mean(ov * ov, axis=-1, keepdims=True) + NORM_EPS)
            on_ref[:, cols] = (ov * ro * w_ref[...] * (zv * _sigmoid(zv))).astype(on_ref.dtype)

    return _rows_call(body, "dn_post_fwd", o.shape[0], [(o, "tile"), (z, "tile"), (dnw_row, "full")],
                      [(o.shape, MXU, "tile")])[0]


def _dn_post_bwd(d_on, o, z, dnw_row):
    def body(d_ref, o_ref, z_ref, w_ref, do_ref, dz_ref, dw_ref):
        acc = jnp.zeros((1, LANES), F32)
        for h in range(DN_HEADS):
            cols = slice(h * LANES, (h + 1) * LANES)
            dv, ov, zv = d_ref[:, cols], o_ref[:, cols], z_ref[:, cols]
            sg = _sigmoid(zv)
            sz = zv * sg
            ro = lax.rsqrt(jnp.mean(ov * ov, axis=-1, keepdims=True) + NORM_EPS)
            nv = ov * ro
            dn = dv * w_ref[...] * sz
            acc = acc + jnp.sum(dv * nv * sz, axis=0, keepdims=True)
            dz_ref[:, cols] = (dv * nv * w_ref[...] * (sg * (1.0 + zv * (1.0 - sg)))).astype(dz_ref.dtype)
            do_ref[:, cols] = ro * dn - ov * (ro * ro * ro) * jnp.mean(dn * ov, axis=-1, keepdims=True)
        _acc_add(dw_ref, jnp.concatenate([acc, jnp.zeros((SUBLANES - 1, LANES), F32)], axis=0))

    return _rows_call(body, "dn_post_bwd", o.shape[0],
                      [(d_on, "tile"), (o, "tile"), (z, "tile"), (dnw_row, "full")],
                      [(o.shape, F32, "tile"), (o.shape, MXU, "tile"), ((SUBLANES, LANES), F32, "acc")])


def _attn_merge_fwd(parts, lses, zb):
    def body(o0, o1, o2, l0, l1, l2, z_ref, lse_ref, o_ref, g_ref):
        a, b, c = l0[...], l1[...], l2[...]
        m = jnp.maximum(a, jnp.maximum(b, c))
        ea, eb, ec = jnp.exp(a - m), jnp.exp(b - m), jnp.exp(c - m)
        den = ea + eb + ec
        out = (ea * o0[...] + eb * o1[...] + ec * o2[...]) / den
        lse_ref[...] = m + jnp.log(den)
        o_ref[...] = out
        zv = z_ref[...]
        g_ref[...] = (out * (zv * _sigmoid(zv))).astype(g_ref.dtype)

    s = zb.shape[0]
    return _rows_call(body, "attn_merge_fwd", s, [(p, "tile") for p in parts] + [(l, "tile") for l in lses] + [(zb, "tile")],
                      [((s, DIL_W), F32, "tile"), ((s, DIL_W), F32, "tile"), ((s, DIL_W), MXU, "tile")])


def _attn_merge_bwd(d_gated, o_joint, zb):
    def body(d_ref, o_ref, z_ref, do_ref, dz_ref, dl_ref):
        zv = z_ref[...]
        sg = _sigmoid(zv)
        dv = d_ref[...]
        ov = o_ref[...]
        do = dv * (zv * sg)
        do_ref[...] = do
        dz_ref[...] = (dv * ov * (sg * (1.0 + zv * (1.0 - sg)))).astype(dz_ref.dtype)
        for h in range(DIL_HEADS):
            cols = slice(h * LANES, (h + 1) * LANES)
            dl_ref[:, cols] = jnp.broadcast_to(jnp.sum(do[:, cols] * ov[:, cols], axis=-1, keepdims=True),
                                               (do.shape[0], LANES))

    s = zb.shape[0]
    return _rows_call(body, "attn_merge_bwd", s, [(d_gated, "tile"), (o_joint, "tile"), (zb, "tile")],
                      [((s, DIL_W), F32, "tile"), ((s, DIL_W), MXU, "tile"), ((s, DIL_W), F32, "tile")])


def _gate_merge_fwd(ga, gb, ya, yb):
    def body(ga_ref, gb_ref, ya_ref, yb_ref, m_ref):
        m_ref[...] = (_sigmoid(ga_ref[...]) * ya_ref[...] + _sigmoid(gb_ref[...]) * yb_ref[...]).astype(m_ref.dtype)

    return _rows_call(body, "gate_merge_fwd", ga.shape[0], [(ga, "tile"), (gb, "tile"), (ya, "tile"), (yb, "tile")],
                      [(ga.shape, MXU, "tile")])[0]


def _gate_merge_bwd(dm, ga, gb, ya, yb):
    def body(dm_ref, ga_ref, gb_ref, ya_ref, yb_ref, dya_ref, dyb_ref, dga_ref, dgb_ref):
        dmv = dm_ref[...]
        sa, sb = _sigmoid(ga_ref[...]), _sigmoid(gb_ref[...])
        dya_ref[...] = (dmv * sa).astype(dya_ref.dtype)
        dyb_ref[...] = (dmv * sb).astype(dyb_ref.dtype)
        dga_ref[...] = (dmv * ya_ref[...] * sa * (1.0 - sa)).astype(dga_ref.dtype)
        dgb_ref[...] = (dmv * yb_ref[...] * sb * (1.0 - sb)).astype(dgb_ref.dtype)

    return _rows_call(body, "gate_merge_bwd", ga.shape[0],
                      [(dm, "tile"), (ga, "tile"), (gb, "tile"), (ya, "tile"), (yb, "tile")],
                      [(ga.shape, MXU, "tile")] * 4)


def _final_fwd_bwd(x, x2pre, target, wf_row):
    s, dm = x.shape

    def body(x_ref, p_ref, t_ref, w_ref, loss_ref, dw_ref, dx_ref, dxb_ref):
        x2 = x_ref[...] + p_ref[...]
        r = lax.rsqrt(jnp.mean(x2 * x2, axis=-1, keepdims=True) + NORM_EPS)
        w = w_ref[...]
        err = x2 * r * w - t_ref[...]
        tile_loss = 0.5 * jnp.sum(jnp.mean(err * err, axis=-1, keepdims=True), axis=0, keepdims=True)
        _acc_add(loss_ref, jnp.broadcast_to(tile_loss, (SUBLANES, LANES)))
        dy = err * (1.0 / dm)
        row = jnp.sum(dy * x2 * r, axis=0, keepdims=True)
        _acc_add(dw_ref, jnp.concatenate([row, jnp.zeros((SUBLANES - 1, dm), F32)], axis=0))
        dn = dy * w
        dx2 = r * dn - x2 * (r * r * r) * jnp.mean(dn * x2, axis=-1, keepdims=True)
        dx_ref[...] = dx2
        dxb_ref[...] = dx2.astype(dxb_ref.dtype)

    return _rows_call(body, "final_fwd_bwd", s, [(x, "tile"), (x2pre, "tile"), (target, "tile"), (wf_row, "full")],
                      [((SUBLANES, LANES), F32, "acc"), ((SUBLANES, dm), F32, "acc"), ((s, dm), F32, "tile"),
                       ((s, dm), MXU, "tile")])


def _lane_pick(x, idx):
    lane = lax.broadcasted_iota(jnp.int32, x.shape, 1)
    return jnp.sum(jnp.where(lane == idx, x, 0.0), axis=-1, keepdims=True)


def _chunk_masks():
    c = DN_CHUNK
    row = lax.broadcasted_iota(jnp.int32, (c, c), 0)
    col = lax.broadcasted_iota(jnp.int32, (c, c), 1)
    return row >= col, row > col, row == col


def _chunk_decay(g_col, causal, eye):
    c = DN_CHUNK
    tri = causal.astype(F32)
    gcb = _dot01(tri, jnp.broadcast_to(g_col, (c, LANES)))
    gsq = gcb[:, :c]
    rowm = _dot01(jnp.ones((c, c), F32), jnp.where(eye, gsq, 0.0))
    gam = jnp.where(causal, jnp.exp(jnp.minimum(gsq - rowm, 0.0)), 0.0)
    return gcb, gam


def _unit_lower_inverse(a_strict, eye):
    n = -a_strict
    t = eye.astype(F32) + n
    p = n
    for _ in range(int(math.log2(DN_CHUNK)) - 1):
        p = _dot3(p, p)
        t = t + _dot3(t, p)
    return t


def _delta_fwd(q, k, v, bg):
    s = q.shape[0]
    c = DN_CHUNK
    n_chunks = s // c

    def body(q_ref, k_ref, v_ref, bg_ref, o_ref, t_ref, st_ref, state):
        @pl.when(pl.program_id(0) == 0)
        def _():
            state[...] = jnp.zeros_like(state)

        causal, strict, eye = _chunk_masks()
        bgv = bg_ref[...]
        for h in range(DN_HEADS):
            cols = slice(h * LANES, (h + 1) * LANES)
            qh, kh, vh = q_ref[:, cols], k_ref[:, cols], v_ref[:, cols]
            beta = _lane_pick(bgv, h)
            gcb, gam = _chunk_decay(_lane_pick(bgv, DN_HEADS + h), causal, eye)
            gl = gcb[c - 1:c, :]
            eg = jnp.exp(gcb)
            kb = kh * beta
            a = jnp.where(strict, _dot_nt(kb, kh) * gam, 0.0)
            t = _unit_lower_inverse(a, eye)
            u = _dot(t, vh * beta)
            w = _dot(t, kb * eg)
            aqk = _dot_nt(qh, kh) * gam
            sv = state[h]
            vnew = u - _dot(w, sv)
            o_ref[:, cols] = _dot(qh * eg, sv) + _dot(aqk, vnew)
            t_ref[0, h] = t
            st_ref[0, h] = sv
            state[h] = sv * jnp.exp(gl) + _dot_tn(kh * jnp.exp(gl - gcb), vnew)

    row_spec = lambda w: pl.BlockSpec((c, w), lambda i: (i, 0))
    return _pcall(
        body, name="delta_fwd", grid=(n_chunks,),
        in_specs=[row_spec(D_MODEL)] * 3 + [row_spec(LANES)],
        out_specs=[row_spec(D_MODEL), pl.BlockSpec((1, DN_HEADS, c, c), lambda i: (i, 0, 0, 0)),
                   pl.BlockSpec((1, DN_HEADS, DN_DK, DN_DK), lambda i: (i, 0, 0, 0))],
        out_shape=[jax.ShapeDtypeStruct((s, D_MODEL), F32), jax.ShapeDtypeStruct((n_chunks, DN_HEADS, c, c), F32),
                   jax.ShapeDtypeStruct((n_chunks, DN_HEADS, DN_DK, DN_DK), F32)],
        scratch_shapes=[pltpu.VMEM((DN_HEADS, DN_DK, DN_DK), F32)],
        compiler_params=_params("arbitrary"),
    )(q, k, v, bg)


def _delta_bwd(q, k, v, bg, t_all, st_all, do):
    s = q.shape[0]
    c = DN_CHUNK
    n_chunks = s // c

    def body(q_ref, k_ref, v_ref, bg_ref, t_ref, st_ref, do_ref, dq_ref, dk_ref, dv_ref, dbg_ref, dstate):
        @pl.when(pl.program_id(0) == 0)
        def _():
            dstate[...] = jnp.zeros_like(dstate)

        causal, strict, eye = _chunk_masks()
        tri = causal.astype(F32)
        ones_cl = jnp.ones((c, LANES), F32)
        row_id = lax.broadcasted_iota(jnp.int32, (c, LANES), 0)
        lane_id = lax.broadcasted_iota(jnp.int32, (c, LANES), 1)
        bgv = bg_ref[...]
        dbg = jnp.zeros((c, LANES), F32)
        for h in range(DN_HEADS):
            cols = slice(h * LANES, (h + 1) * LANES)
            qh, kh, vh, doh = q_ref[:, cols], k_ref[:, cols], v_ref[:, cols], do_ref[:, cols]
            beta = _lane_pick(bgv, h)
            gcb, gam = _chunk_decay(_lane_pick(bgv, DN_HEADS + h), causal, eye)
            gl = gcb[c - 1:c, :]
            eg = jnp.exp(gcb)
            egl = jnp.exp(gl - gcb)
            dl = jnp.exp(gl)
            kb = kh * beta
            kk = _dot_nt(kb, kh)
            a = jnp.where(strict, kk * gam, 0.0)
            t = t_ref[0, h]
            vb = vh * beta
            kbe = kb * eg
            u = _dot(t, vb)
            w = _dot(t, kbe)
            qk = _dot_nt(qh, kh)
            aqk = qk * gam
            qd = qh * eg
            kd = kh * egl
            sv = st_ref[0, h]
            vnew = u - _dot(w, sv)
            ds = dstate[h]

            dvnew = _dot_tn(aqk, doh) + _dot(kd, ds)
            dkd = _dot_nt(vnew, ds)
            dqd = _dot_nt(doh, sv)
            daqk = jnp.where(causal, _dot_nt(doh, vnew), 0.0)
            dw = -_dot_nt(dvnew, sv)
            ddl = jnp.sum(jnp.sum(sv * ds, axis=1, keepdims=True), axis=0, keepdims=True)
            dstate[h] = ds * dl + _dot_tn(qd, doh) - _dot_tn(w, dvnew)

            dvb = _dot_tn(t, dvnew)
            dkbe = _dot_tn(t, dw)
            da = jnp.where(strict, -(_dot_nt(dvb, u) + _dot_nt(dkbe, w)), 0.0)
            pm = da * gam
            qm = daqk * gam
            dkb = _dot(pm, kh) + dkbe * eg
            dkh = _dot_tn(pm, kb) + _dot_tn(qm, qh) + dkd * egl + dkb * beta
            dqh = _dot(qm, kh) + dqd * eg
            xm = da * a + daqk * aqk
            tmp = jnp.sum(dkd * kd, axis=-1, keepdims=True)
            dgc = (jnp.sum(xm, axis=-1, keepdims=True) - _dot01_tn(xm, jnp.ones((c, LANES), F32))
                   + jnp.sum(dkbe * kbe, axis=-1, keepdims=True) + jnp.sum(dqd * qd, axis=-1, keepdims=True) - tmp)
            dgl = jnp.sum(tmp, axis=0, keepdims=True) + ddl * dl
            dgc = dgc + jnp.where(row_id == c - 1, dgl, 0.0)
            dg = _dot01_tn_tri(tri, dgc)
            dbeta = jnp.sum(dvb * vh, axis=-1, keepdims=True) + jnp.sum(dkb * kh, axis=-1, keepdims=True)

            dq_ref[:, cols] = dqh
            dk_ref[:, cols] = dkh
            dv_ref[:, cols] = dvb * beta
            dbg = dbg + jnp.where(lane_id == h, dbeta * ones_cl, 0.0) + jnp.where(lane_id == DN_HEADS + h, dg, 0.0)
        dbg_ref[...] = dbg

    rev = lambda i: n_chunks - 1 - i
    row_spec = lambda w: pl.BlockSpec((c, w), lambda i: (rev(i), 0))
    return _pcall(
        body, name="delta_bwd", grid=(n_chunks,),
        in_specs=[row_spec(D_MODEL)] * 3 + [row_spec(LANES),
                  pl.BlockSpec((1, DN_HEADS, c, c), lambda i: (rev(i), 0, 0, 0)),
                  pl.BlockSpec((1, DN_HEADS, DN_DK, DN_DK), lambda i: (rev(i), 0, 0, 0)),
                  row_spec(D_MODEL)],
        out_specs=[row_spec(D_MODEL)] * 3 + [row_spec(LANES)],
        out_shape=[jax.ShapeDtypeStruct((s, D_MODEL), F32)] * 3 + [jax.ShapeDtypeStruct((s, LANES), F32)],
        scratch_shapes=[pltpu.VMEM((DN_HEADS, DN_DK, DN_DK), F32)],
        compiler_params=_params("arbitrary"),
    )(q, k, v, bg, t_all, st_all, do)


def _dot01_tn_tri(tri, x):
    m = tri.astype(jnp.bfloat16)
    hi, mid, lo = _split3(x)
    f = lambda p: lax.dot_general(m, p, (((0,), (0,)), ((), ())), preferred_element_type=F32)
    return f(hi) + (f(mid) + f(lo))


def _alibi_slope(group, head):
    n = N_DIL * DIL_HEADS
    return float(2.0 ** (-8.0 * (group * DIL_HEADS + head + 1) / n))


def _attn_views(s, dil):
    l = s // dil
    assert l % ATT_BLOCK == 0, "sub-sequence length must be a whole number of attention blocks"
    return l, l // ATT_BLOCK


def _window_bias(dil, n, first_cols):
    a = lax.broadcasted_iota(jnp.int32, (ATT_BLOCK, 2 * ATT_BLOCK), 0)
    b = lax.broadcasted_iota(jnp.int32, (ATT_BLOCK, 2 * ATT_BLOCK), 1)
    dist = ATT_BLOCK + a - b
    valid = (dist >= 0) & (dist <= ATT_BLOCK) & ((b >= ATT_BLOCK) | (n > 0))
    return (dist * dil).astype(F32), valid


def _attn_fwd(qb, kb, vb, group):
    window, dil = DIL_GROUPS[group]
    assert window // dil == ATT_BLOCK
    s = qb.shape[0]
    l, nb = _attn_views(s, dil)
    view = lambda t: t.reshape(l, dil * DILQ_W)

    def body(q_ref, kp_ref, kc_ref, vp_ref, vc_ref, o_ref, lse_ref):
        n = pl.program_id(1)
        distd, valid = _window_bias(dil, n, None)
        for h in range(DIL_HEADS):
            cols = slice(h * LANES, (h + 1) * LANES)
            kk = jnp.concatenate([kp_ref[:, cols].astype(MXU), kc_ref[:, cols].astype(MXU)], axis=0)
            vv = jnp.concatenate([vp_ref[:, cols].astype(MXU), vc_ref[:, cols].astype(MXU)], axis=0)
            sc = _dot_nt(q_ref[:, cols], kk) * DIL_DH ** -0.5 - _alibi_slope(group, h) * distd
            sc = jnp.where(valid, sc, -1e30)
            mx = jnp.max(sc, axis=-1, keepdims=True)
            p = jnp.where(valid, jnp.exp(sc - mx), 0.0)
            den = jnp.sum(p, axis=-1, keepdims=True)
            o_ref[:, cols] = _dot(p, vv) / den
            lse_ref[:, cols] = jnp.broadcast_to(mx + jnp.log(den), (ATT_BLOCK, LANES))

    cur = pl.BlockSpec((ATT_BLOCK, DIL_W), lambda r, n: (n, r * N_DIL + group))
    prev = pl.BlockSpec((ATT_BLOCK, DIL_W), lambda r, n: (jnp.maximum(n - 1, 0), r * N_DIL + group))
    out = pl.BlockSpec((ATT_BLOCK, DIL_W), lambda r, n: (n, r))
    o, lse = _pcall(
        body, name=f"attn_fwd_g{group}", grid=(dil, nb),
        in_specs=[cur, prev, cur, prev, cur], out_specs=[out, out],
        out_shape=[jax.ShapeDtypeStruct((l, dil * DIL_W), F32)] * 2,
        compiler_params=_params("parallel", "parallel"),
    )(view(qb), view(kb), view(kb), view(vb), view(vb))
    return o.reshape(s, DIL_W), lse.reshape(s, DIL_W)


def _attn_bwd(qb, kb, vb, d_o, lse, delta, group):
    window, dil = DIL_GROUPS[group]
    s = qb.shape[0]
    l, nb = _attn_views(s, dil)
    view = lambda t: t.reshape(l, dil * DILQ_W)
    view_o = lambda t: t.reshape(l, dil * DIL_W)
    scale = DIL_DH ** -0.5

    def body(qc_ref, qn_ref, kp_ref, kc_ref, vp_ref, vc_ref, doc_ref, don_ref, lc_ref, ln_ref, dc_ref, dn_ref,
             dq_ref, dk_ref, dv_ref):
        n = pl.program_id(1)
        distd, valid = _window_bias(dil, n, None)
        bk = lax.broadcasted_iota(jnp.int32, (ATT_BLOCK, 2 * ATT_BLOCK), 0)
        aq = lax.broadcasted_iota(jnp.int32, (ATT_BLOCK, 2 * ATT_BLOCK), 1)
        dist_t = aq - bk
        valid_t = (dist_t >= 0) & (dist_t <= ATT_BLOCK) & ((aq < ATT_BLOCK) | (n < nb - 1))
        distd_t = (dist_t * dil).astype(F32)
        for h in range(DIL_HEADS):
            cols = slice(h * LANES, (h + 1) * LANES)
            slope = _alibi_slope(group, h)
            qc = qc_ref[:, cols].astype(MXU)
            kc = kc_ref[:, cols].astype(MXU)
            vc = vc_ref[:, cols].astype(MXU)
            doc = doc_ref[:, cols]
            kk = jnp.concatenate([kp_ref[:, cols].astype(MXU), kc], axis=0)
            vv = jnp.concatenate([vp_ref[:, cols].astype(MXU), vc], axis=0)
            sc = _dot_nt(qc, kk) * scale - slope * distd
            lse2 = jnp.concatenate([lc_ref[:, cols]] * 2, axis=1)
            del2 = jnp.concatenate([dc_ref[:, cols]] * 2, axis=1)
            p = jnp.where(valid, jnp.exp(jnp.minimum(sc - lse2, 0.0)), 0.0)
            dsc = p * (_dot_nt(doc, vv) - del2)
            dq_ref[:, cols] = (_dot(dsc, kk) * scale).astype(dq_ref.dtype)
            qq = jnp.concatenate([qc, qn_ref[:, cols].astype(MXU)], axis=0)
            doo = jnp.concatenate([doc.astype(MXU), don_ref[:, cols].astype(MXU)], axis=0)
            lse_t = jnp.concatenate([lc_ref[:, cols], ln_ref[:, cols]], axis=0).T
            del_t = jnp.concatenate([dc_ref[:, cols], dn_ref[:, cols]], axis=0).T
            sc_t = _dot_nt(kc, qq) * scale - slope * distd_t
            p_t = jnp.where(valid_t, jnp.exp(jnp.minimum(sc_t - lse_t, 0.0)), 0.0)
            ds_t = p_t * (_dot_nt(vc, doo) - del_t)
            dk_ref[:, cols] = (_dot(ds_t, qq) * scale).astype(dk_ref.dtype)
            dv_ref[:, cols] = _dot(p_t, doo).astype(dv_ref.dtype)

    def spec(width, shift, per_r, off):
        if shift < 0:
            f = lambda r, n: (jnp.maximum(n - 1, 0), r * per_r + off)
        elif shift > 0:
            f = lambda r, n: (jnp.minimum(n + 1, nb - 1), r * per_r + off)
        else:
            f = lambda r, n: (n, r * per_r + off)
        return pl.BlockSpec((ATT_BLOCK, width), f)

    qkv = lambda shift: spec(DIL_W, shift, N_DIL, group)
    row = lambda shift: spec(DIL_W, shift, 1, 0)
    dq, dk, dv = _pcall(
        body, name=f"attn_bwd_g{group}", grid=(dil, nb),
        in_specs=[qkv(0), qkv(1), qkv(-1), qkv(0), qkv(-1), qkv(0), row(0), row(1), row(0), row(1), row(0), row(1)],
        out_specs=[row(0)] * 3,
        out_shape=[jax.ShapeDtypeStruct((l, dil * DIL_W), MXU)] * 3,
        compiler_params=_params("parallel", "parallel"),
    )(view(qb), view(qb), view(kb), view(kb), view(vb), view(vb), view_o(d_o), view_o(d_o), view_o(lse), view_o(lse),
      view_o(delta), view_o(delta))
    return dq.reshape(s, DIL_W), dk.reshape(s, DIL_W), dv.reshape(s, DIL_W)


def _my_place():
    mx, my, mc = lax.axis_index("x"), lax.axis_index("y"), lax.axis_index("c")
    return mx, my, mc, 4 * mx + 2 * my + mc


def _peer(mx, my, mc, k):
    px = 1 - mx if k & 4 else mx
    py = 1 - my if k & 2 else my
    pc = 1 - mc if k & 1 else mc
    return (px, py, pc), 4 * px + 2 * py + pc


def _all_gather(xs, name):
    n = len(xs)

    def body(*refs):
        x_refs, o_refs = refs[:n], refs[n:2 * n]
        send_sems, recv_sems, local_sems = refs[2 * n:]
        mx, my, mc, me = _my_place()
        local = [pltpu.make_async_copy(x_refs[a], o_refs[a].at[me], local_sems.at[a]) for a in range(n)]
        for cp in local:
            cp.start()
        for k in range(1, N_DEV):
            peer, _ = _peer(mx, my, mc, k)
            for a in range(n):
                pltpu.make_async_remote_copy(
                    src_ref=x_refs[a], dst_ref=o_refs[a].at[me], send_sem=send_sems.at[a, k - 1],
                    recv_sem=recv_sems.at[a, k - 1], device_id=peer, device_id_type=MESH).start()
        for k in range(1, N_DEV):
            peer, peer_id = _peer(mx, my, mc, k)
            for a in range(n):
                pltpu.make_async_remote_copy(
                    src_ref=x_refs[a], dst_ref=o_refs[a].at[peer_id], send_sem=send_sems.at[a, k - 1],
                    recv_sem=recv_sems.at[a, k - 1], device_id=peer, device_id_type=MESH).wait()
        for cp in local:
            cp.wait()

    any_spec = pl.BlockSpec(memory_space=pl.ANY)
    return _pcall(
        body, name=name,
        in_specs=[any_spec] * n, out_specs=[any_spec] * n,
        out_shape=[jax.ShapeDtypeStruct((N_DEV,) + x.shape, x.dtype) for x in xs],
        scratch_shapes=[pltpu.SemaphoreType.DMA((n, N_DEV - 1)), pltpu.SemaphoreType.DMA((n, N_DEV - 1)),
                        pltpu.SemaphoreType.DMA((n,))],
    )(*xs)


def _reduce_scatter_parts(gs, name):
    n = len(gs)

    def body(*refs):
        g_refs, o_refs = refs[:n], refs[n:2 * n]
        send_sems, recv_sems, local_sems = refs[2 * n:]
        mx, my, mc, me = _my_place()
        local = [pltpu.make_async_copy(g_refs[a].at[me], o_refs[a].at[me], local_sems.at[a]) for a in range(n)]
        for cp in local:
            cp.start()
        for k in range(1, N_DEV):
            peer, peer_id = _peer(mx, my, mc, k)
            for a in range(n):
                pltpu.make_async_remote_copy(
                    src_ref=g_refs[a].at[peer_id], dst_ref=o_refs[a].at[me], send_sem=send_sems.at[a, k - 1],
                    recv_sem=recv_sems.at[a, k - 1], device_id=peer, device_id_type=MESH).start()
        for k in range(1, N_DEV):
            peer, peer_id = _peer(mx, my, mc, k)
            for a in range(n):
                pltpu.make_async_remote_copy(
                    src_ref=g_refs[a].at[peer_id], dst_ref=o_refs[a].at[peer_id], send_sem=send_sems.at[a, k - 1],
                    recv_sem=recv_sems.at[a, k - 1], device_id=peer, device_id_type=MESH).wait()
        for cp in local:
            cp.wait()

    any_spec = pl.BlockSpec(memory_space=pl.ANY)
    return _pcall(
        body, name=name,
        in_specs=[any_spec] * n, out_specs=[any_spec] * n,
        out_shape=[jax.ShapeDtypeStruct(g.shape, g.dtype) for g in gs],
        scratch_shapes=[pltpu.SemaphoreType.DMA((n, N_DEV - 1)), pltpu.SemaphoreType.DMA((n, N_DEV - 1)),
                        pltpu.SemaphoreType.DMA((n,))],
    )(*gs)


def _adamw(parts, w, m, v, name):
    r, c = w.shape
    tr = 128 if r % 128 == 0 and r > 128 else r
    bc1 = 1.0 - ADAM_B1 ** ADAM_STEP
    bc2 = 1.0 - ADAM_B2 ** ADAM_STEP

    def body(p_ref, w_ref, m_ref, v_ref, g_ref, d_ref, nm_ref, nv_ref):
        g = p_ref[0].astype(F32)
        for j in range(1, N_DEV):
            g = g + p_ref[j].astype(F32)
        nm = ADAM_B1 * m_ref[...] + (1.0 - ADAM_B1) * g
        nv = ADAM_B2 * v_ref[...] + (1.0 - ADAM_B2) * (g * g)
        g_ref[...] = g
        nm_ref[...] = nm
        nv_ref[...] = nv
        d_ref[...] = -ADAM_LR * ((nm / bc1) / (jnp.sqrt(nv / bc2) + ADAM_EPS) + ADAM_WD * w_ref[...])

    blk = pl.BlockSpec((tr, c), lambda i: (i, 0))
    return _pcall(
        body, name=name, grid=(r // tr,),
        in_specs=[pl.BlockSpec((N_DEV, tr, c), lambda i: (0, i, 0)), blk, blk, blk],
        out_specs=[blk] * 4, out_shape=[jax.ShapeDtypeStruct((r, c), F32)] * 4,
        compiler_params=_params("parallel"),
    )(parts, w, m, v)


def _local_step(x, target, norm_w, w_segs, conv_w, a_log, dt_bias, dn_norm_w, w_o_dn, w_o_dil, w_out, final_norm_w):
    s = x.shape[0]
    w_qkv, w_za, w_ba, w_qb, w_kb, w_vb, w_zb, w_ga, w_gb = w_segs
    conv_w8 = jnp.concatenate([conv_w, jnp.zeros((SUBLANES - conv_w.shape[0], QKV_W), F32)], axis=0)
    pad8 = jnp.zeros((1, DN_HEADS), F32)
    alog_row = jnp.concatenate([pad8, a_log, jnp.zeros((1, LANES - 2 * DN_HEADS), F32)], axis=1)
    dtb_row = jnp.concatenate([pad8, dt_bias, jnp.zeros((1, LANES - 2 * DN_HEADS), F32)], axis=1)
    wf_row = final_norm_w.reshape(1, D_MODEL)

    hb = _rms_in_fwd(x, norm_w)
    qkv_pre, z_a, ba, z_b = _mm_nn(hb, [w_qkv, w_za, w_ba, w_zb], "proj_fwd_a")
    q_b, k_b, v_b, g_a, g_b = _mm_nn(hb, [w_qb, w_kb, w_vb, w_ga, w_gb], "proj_fwd_b")

    qn, kn, vn, bg = _dn_prep_fwd(qkv_pre, ba, conv_w8, alog_row, dtb_row)
    o_a, t_all, st_all = _delta_fwd(qn, kn, vn, bg)
    on_b = _dn_post_fwd(o_a, z_a, dn_norm_w)
    y_a = _mm_nn(on_b, [w_o_dn], "out_dn_fwd")[0]

    parts, lses = [], []
    for gi in range(N_DIL):
        o_g, l_g = _attn_fwd(q_b, k_b, v_b, gi)
        parts.append(o_g)
        lses.append(l_g)
    lse, o_joint, ob_b = _attn_merge_fwd(parts, lses, z_b)
    y_b = _mm_nn(ob_b, [w_o_dil], "out_dil_fwd")[0]

    merged_b = _gate_merge_fwd(g_a, g_b, y_a, y_b)
    x2pre = _mm_nn(merged_b, [w_out], "out_fwd")[0]
    loss8, dwf8, dx2, dx2_b = _final_fwd_bwd(x, x2pre, target, wf_row)

    d_merged = _mm_nt_sum([dx2_b], [w_out], "out_bwd")
    g_w_out = _mm_tn(merged_b, dx2_b, "out_wgrad")
    dya_b, dyb_b, dga_b, dgb_b = _gate_merge_bwd(d_merged, g_a, g_b, y_a, y_b)

    d_on = _mm_nt_sum([dya_b], [w_o_dn], "out_dn_bwd")
    g_w_o_dn = _mm_tn(on_b, dya_b, "out_dn_wgrad")
    d_o_a, dza_b, ddnw8 = _dn_post_bwd(d_on, o_a, z_a, dn_norm_w)

    d_ob = _mm_nt_sum([dyb_b], [w_o_dil], "out_dil_bwd")
    g_w_o_dil = _mm_tn(ob_b, dyb_b, "out_dil_wgrad")
    d_o, dzb_b, delta = _attn_merge_bwd(d_ob, o_joint, z_b)
    dqs, dks, dvs = [], [], []
    for gi in range(N_DIL):
        dq_g, dk_g, dv_g = _attn_bwd(q_b, k_b, v_b, d_o, lse, delta, gi)
        dqs.append(dq_g)
        dks.append(dk_g)
        dvs.append(dv_g)
    dqb_b, dkb_b, dvb_b = (jnp.concatenate(t, axis=1) for t in (dqs, dks, dvs))

    dqn, dkn, dvn, dbg = _delta_bwd(qn, kn, vn, bg, t_all, st_all, d_o_a)
    dc, dba_b, dsmall8 = _dn_prep_bwd(qkv_pre, ba, conv_w8, alog_row, dtb_row, dqn, dkn, dvn, dbg)
    dqkv_b, dconv8 = _conv_bwd(dc, qkv_pre, conv_w8)

    dh_a = _mm_nt_sum([dqkv_b, dza_b, dba_b, dzb_b], [w_qkv, w_za, w_ba, w_zb], "proj_bwd_a")
    dh_b = _mm_nt_sum([dqb_b, dkb_b, dvb_b, dga_b, dgb_b], [w_qb, w_kb, w_vb, w_ga, w_gb], "proj_bwd_b")
    dsegs = [dqkv_b, dza_b, dba_b, dqb_b, dkb_b, dvb_b, dzb_b, dga_b, dgb_b]
    g_segs = [_mm_tn(hb, d, f"proj_wgrad_{j}") for j, d in enumerate(dsegs)]
    grad_x, dnw8 = _rms_in_bwd_sum(x, dh_a, dh_b, dx2, norm_w)

    small = dict(norm_w=dnw8[0:1], final_norm_w=dwf8[0:1], dn_norm_w=ddnw8[0:1],
                 a_log=dsmall8[0:1, DN_HEADS:2 * DN_HEADS], dt_bias=dsmall8[1:2, DN_HEADS:2 * DN_HEADS])
    return loss8[0:1, 0:1], grad_x, g_segs, dconv8[0:4], g_w_o_dn, g_w_o_dil, g_w_out, small


def _rms_in_bwd_sum(x, dh_a, dh_b, dx2, norm_w):
    def body(x_ref, da_ref, db_ref, dx2_ref, w_ref, dx_ref, dw_ref):
        xv = x_ref[...]
        r = lax.rsqrt(jnp.mean(xv * xv, axis=-1, keepdims=True) + NORM_EPS)
        dhv = da_ref[...] + db_ref[...]
        dn = dhv * w_ref[...]
        dx_ref[...] = dx2_ref[...] + r * dn - xv * (r * r * r) * jnp.mean(dn * xv, axis=-1, keepdims=True)
        row = jnp.sum(dhv * xv * r, axis=0, keepdims=True)
        _acc_add(dw_ref, jnp.concatenate([row, jnp.zeros((SUBLANES - 1, row.shape[1]), F32)], axis=0))

    return _rows_call(body, "rms_in_bwd", x.shape[0],
                      [(x, "tile"), (dh_a, "tile"), (dh_b, "tile"), (dx2, "tile"), (norm_w, "full")],
                      [(x.shape, F32, "tile"), ((SUBLANES, x.shape[1]), F32, "acc")])


def _split_proj_cols(w_full):
    offs = [0]
    for n in PROJ_SIZES:
        offs.append(offs[-1] + n)
    seg = lambda a, b: w_full[:, offs[a]:offs[b]]
    w_ba = jnp.concatenate([seg(4, 6), jnp.zeros((w_full.shape[0], LANES - 2 * DN_HEADS), w_full.dtype)], axis=1)
    return [seg(0, 3), seg(3, 4), w_ba, seg(6, 7), seg(7, 8), seg(8, 9), seg(9, 10), seg(10, 11), seg(11, 12)]


def _join_proj_cols(g_segs):
    g_qkv, g_za, g_ba, g_qb, g_kb, g_vb, g_zb, g_ga, g_gb = g_segs
    return jnp.concatenate([g_qkv, g_za, g_ba[:, :2 * DN_HEADS], g_qb, g_kb, g_vb, g_zb, g_ga, g_gb], axis=1)


def _pack_small(norm_w, final_norm_w, dn_norm_w, a_log, dt_bias):
    pad = lambda r: jnp.concatenate([r, jnp.zeros((1, D_MODEL - r.shape[1]), F32)], axis=1)
    rows = [pad(norm_w.reshape(1, -1)), pad(final_norm_w.reshape(1, -1)), pad(dn_norm_w.reshape(1, -1)),
            pad(a_log.reshape(1, -1)), pad(dt_bias.reshape(1, -1)), jnp.zeros((SUBLANES - 5, D_MODEL), F32)]
    return jnp.concatenate(rows, axis=0)


def _unpack_small(p):
    return dict(norm_w=p[0:1], final_norm_w=p[1], dn_norm_w=p[2:3, :DN_DK], a_log=p[3:4, :DN_HEADS],
                dt_bias=p[4:5, :DN_HEADS])


def kernel(x, norm_w, w_in, conv_w, a_log, dt_bias, dn_norm_w, w_o_dn, w_o_dil, w_out, final_norm_w, loss_target, m_norm_w, m_w_in, m_conv_w, m_a_log, m_dt_bias, m_dn_norm_w, m_w_o_dn, m_w_o_dil, m_w_out, m_final_norm_w, v_norm_w, v_w_in, v_conv_w, v_a_log, v_dt_bias, v_dn_norm_w, v_w_o_dn, v_w_o_dil, v_w_out, v_final_norm_w):
    shard_w = w_in.shape[2]
    gathered = _all_gather([w_in[0].astype(MXU), w_o_dn[0].astype(MXU), w_o_dil[0].astype(MXU), w_out[0].astype(MXU),
                            conv_w[0]], "gather_weights")
    w_in_all, w_o_dn_all, w_o_dil_all, w_out_all, conv_all = gathered
    w_in_full = jnp.transpose(w_in_all, (1, 0, 2)).reshape(D_MODEL, N_DEV * shard_w)
    w_o_dn_full = w_o_dn_all.reshape(D_MODEL, D_MODEL)
    w_o_dil_full = jnp.transpose(w_o_dil_all, (1, 0, 2)).reshape(DIL_W, D_MODEL)
    w_out_full = w_out_all.reshape(D_MODEL, D_MODEL)
    conv_full = jnp.transpose(conv_all, (1, 0, 2)).reshape(conv_w.shape[1], QKV_W)

    loss11, grad_x, g_segs, g_conv, g_w_o_dn, g_w_o_dil, g_w_out, small = _local_step(
        x[0], loss_target[0], norm_w, _split_proj_cols(w_in_full), conv_full, a_log, dt_bias, dn_norm_w,
        w_o_dn_full, w_o_dil_full, w_out_full, final_norm_w)

    col_shards = lambda g, n: jnp.transpose(g.reshape(g.shape[0], N_DEV, n), (1, 0, 2))
    row_shards = lambda g: g.reshape(N_DEV, g.shape[0] // N_DEV, g.shape[1])
    sent = [col_shards(_join_proj_cols(g_segs), shard_w).astype(MXU), row_shards(g_w_o_dn).astype(MXU),
            col_shards(g_w_o_dil, w_o_dil.shape[2]).astype(MXU), row_shards(g_w_out).astype(MXU),
            col_shards(g_conv, conv_w.shape[2])]
    p_w_in, p_w_o_dn, p_w_o_dil, p_w_out, p_conv = _reduce_scatter_parts(sent, "scatter_grads")
    p_small = _all_gather([_pack_small(small["norm_w"], small["final_norm_w"], small["dn_norm_w"], small["a_log"],
                                       small["dt_bias"])], "gather_small_grads")[0]

    res = {}
    res["w_in"] = _adamw(p_w_in, w_in[0], m_w_in[0], v_w_in[0], "adamw_w_in")
    res["conv_w"] = _adamw(p_conv, conv_w[0], m_conv_w[0], v_conv_w[0], "adamw_conv_w")
    res["w_o_dn"] = _adamw(p_w_o_dn, w_o_dn[0], m_w_o_dn[0], v_w_o_dn[0], "adamw_w_o_dn")
    res["w_o_dil"] = _adamw(p_w_o_dil, w_o_dil[0], m_w_o_dil[0], v_w_o_dil[0], "adamw_w_o_dil")
    res["w_out"] = _adamw(p_w_out, w_out[0], m_w_out[0], v_w_out[0], "adamw_w_out")
    small_res = _adamw(p_small, _pack_small(norm_w, final_norm_w, dn_norm_w, a_log, dt_bias),
                       _pack_small(m_norm_w, m_final_norm_w, m_dn_norm_w, m_a_log, m_dt_bias),
                       _pack_small(v_norm_w, v_final_norm_w, v_dn_norm_w, v_a_log, v_dt_bias), "adamw_small")
    small_res = [_unpack_small(t) for t in small_res]

    loss = lax.psum(loss11[0, 0], ("x", "y", "c"))
    names = ["norm_w", "w_in", "conv_w", "a_log", "dt_bias", "dn_norm_w", "w_o_dn", "w_o_dil", "w_out", "final_norm_w"]
    outs = [loss, grad_x[None]]
    for kind in range(4):
        for nm in names:
            outs.append(res[nm][kind][None] if nm in res else small_res[kind][nm])
    return tuple(outs)
```

```python
import math

import jax
import jax.numpy as jnp
from jax import lax
from jax.experimental import pallas as pl
from jax.experimental.pallas import tpu as pltpu

F32 = jnp.float32
MXU = jnp.bfloat16
MESH = pl.DeviceIdType.MESH

N_DEV = 8
D_MODEL = 1024
DN_HEADS = 8
DN_DK = 128
DN_CHUNK = 64
N_DIL = 3
DIL_HEADS = 4
DIL_DH = 128
DIL_W = DIL_HEADS * DIL_DH
DIL_GROUPS = ((128, 1), (512, 4), (2048, 16))
ATT_BLOCK = 128
NORM_EPS = 1e-6
QKV_W = 3 * D_MODEL
DILQ_W = N_DIL * DIL_W
PROJ_SIZES = (1024, 1024, 1024, 1024, 8, 8, DILQ_W, DILQ_W, DILQ_W, DIL_W, D_MODEL, D_MODEL)

ADAM_LR = 0.001
ADAM_B1 = 0.9
ADAM_B2 = 0.999
ADAM_EPS = 1e-08
ADAM_WD = 0.01
ADAM_STEP = 10

ROW_TILE = 256
LANES = 128
SUBLANES = 8
VMEM_LIMIT = 48 << 20


def _pcall(body, **kw):
    return pl.pallas_call(body, **kw)


def _params(*sem):
    return pltpu.CompilerParams(dimension_semantics=tuple(sem), vmem_limit_bytes=VMEM_LIMIT)


def _sigmoid(x):
    return 1.0 / (1.0 + jnp.exp(-x))


def _softplus(x):
    return jnp.maximum(x, 0.0) + jnp.log(1.0 + jnp.exp(-jnp.abs(x)))


def _dot(a, b):
    return jnp.dot(a.astype(MXU), b.astype(MXU), preferred_element_type=F32)


def _dot_nt(a, b):
    return lax.dot_general(a.astype(MXU), b.astype(MXU), (((1,), (1,)), ((), ())), preferred_element_type=F32)


def _split3(x):
    hi = x.astype(jnp.bfloat16)
    r1 = x - hi.astype(F32)
    mid = r1.astype(jnp.bfloat16)
    lo = (r1 - mid.astype(F32)).astype(jnp.bfloat16)
    return hi, mid, lo


def _dot01(m01, x):
    m = m01.astype(jnp.bfloat16)
    hi, mid, lo = _split3(x)
    f = lambda p: jnp.dot(m, p, preferred_element_type=F32)
    return f(hi) + (f(mid) + f(lo))


def _rows_call(body, name, n_rows, ins, outs, scratch=(), tm=ROW_TILE):
    steps = n_rows // tm
    per8 = tm // SUBLANES
    last8 = n_rows // SUBLANES - 1
    in_specs = []
    for arr, kind in ins:
        cols = arr.shape[-1]
        if kind == "tile":
            in_specs.append(pl.BlockSpec((tm, cols), lambda i: (i, 0)))
        elif kind == "full":
            in_specs.append(pl.BlockSpec(arr.shape, lambda i, nd=arr.ndim: (0,) * nd))
        elif kind == "prev8":
            in_specs.append(pl.BlockSpec((SUBLANES, cols), lambda i: (jnp.maximum(i * per8 - 1, 0), 0)))
        elif kind == "next8":
            in_specs.append(pl.BlockSpec((SUBLANES, cols), lambda i: (jnp.minimum((i + 1) * per8, last8), 0)))
        else:
            raise ValueError(kind)
    out_specs, out_shape, has_acc = [], [], False
    for shape, dtype, kind in outs:
        out_shape.append(jax.ShapeDtypeStruct(shape, dtype))
        if kind == "tile":
            out_specs.append(pl.BlockSpec((tm, shape[-1]), lambda i: (i, 0)))
        else:
            has_acc = True
            out_specs.append(pl.BlockSpec(shape, lambda i: (0, 0)))
    return _pcall(
        body, name=name, grid=(steps,), in_specs=in_specs, out_specs=out_specs, out_shape=out_shape,
        scratch_shapes=list(scratch),
        compiler_params=_params("arbitrary" if has_acc else "parallel"),
    )(*[a for a, _ in ins])


def _acc_add(ref, value):
    @pl.when(pl.program_id(0) == 0)
    def _():
        ref[...] = jnp.zeros_like(ref)
    ref[...] += value


def _col_chunks(n, width=512):
    return [(c, min(width, n - c)) for c in range(0, n, width)]


def _mm_nn(a, ws, name, out_dtype=F32, tm=ROW_TILE):
    m, k = a.shape
    ns = [w.shape[1] for w in ws]

    def body(a_ref, *refs):
        av = a_ref[...]
        for w_ref, o_ref, n in zip(refs[:len(ws)], refs[len(ws):], ns):
            for c, wd in _col_chunks(n):
                o_ref[:, c:c + wd] = jnp.dot(av, w_ref[:, c:c + wd], preferred_element_type=F32).astype(o_ref.dtype)

    return _pcall(
        body, name=name, grid=(m // tm,),
        in_specs=[pl.BlockSpec((tm, k), lambda i: (i, 0))] + [pl.BlockSpec((k, n), lambda i: (0, 0)) for n in ns],
        out_specs=[pl.BlockSpec((tm, n), lambda i: (i, 0)) for n in ns],
        out_shape=[jax.ShapeDtypeStruct((m, n), out_dtype) for n in ns],
        compiler_params=_params("parallel"),
    )(a, *ws)


def _mm_nt_sum(ds, ws, name, tm=ROW_TILE):
    m = ds[0].shape[0]
    k = ws[0].shape[0]
    ns = [d.shape[1] for d in ds]

    def body(*refs):
        d_refs, w_refs, o_ref = refs[:len(ds)], refs[len(ds):2 * len(ds)], refs[-1]
        first = True
        for d_ref, w_ref, n in zip(d_refs, w_refs, ns):
            for c, wd in _col_chunks(n, 1024):
                part = lax.dot_general(d_ref[:, c:c + wd], w_ref[:, c:c + wd], (((1,), (1,)), ((), ())),
                                       preferred_element_type=F32)
                if first:
                    o_ref[...] = part
                    first = False
                else:
                    o_ref[...] += part

    return _pcall(
        body, name=name, grid=(m // tm,),
        in_specs=[pl.BlockSpec((tm, n), lambda i: (i, 0)) for n in ns] + [pl.BlockSpec((k, n), lambda i: (0, 0)) for n in ns],
        out_specs=pl.BlockSpec((tm, k), lambda i: (i, 0)),
        out_shape=jax.ShapeDtypeStruct((m, k), F32),
        compiler_params=_params("parallel"),
    )(*ds, *ws)


def _mm_tn(a, d, name, tm=512):
    m, k = a.shape
    n = d.shape[1]
    tn = 1024 if n % 1024 == 0 else (512 if n % 512 == 0 else n)

    def body(a_ref, d_ref, o_ref):
        @pl.when(pl.program_id(1) == 0)
        def _():
            o_ref[...] = jnp.zeros_like(o_ref)
        o_ref[...] += lax.dot_general(a_ref[...], d_ref[...], (((0,), (0,)), ((), ())), preferred_element_type=F32)

    return _pcall(
        body, name=name, grid=(n // tn, m // tm),
        in_specs=[pl.BlockSpec((tm, k), lambda i, j: (j, 0)), pl.BlockSpec((tm, tn), lambda i, j: (j, i))],
        out_specs=pl.BlockSpec((k, tn), lambda i, j: (0, i)),
        out_shape=jax.ShapeDtypeStruct((k, n), F32),
        compiler_params=_params("parallel", "arbitrary"),
    )(a, d)


def _rms_in_fwd(x, norm_w):
    def body(x_ref, w_ref, h_ref):
        xv = x_ref[...]
        r = lax.rsqrt(jnp.mean(xv * xv, axis=-1, keepdims=True) + NORM_EPS)
        h_ref[...] = (xv * r * w_ref[...]).astype(h_ref.dtype)

    return _rows_call(body, "rms_in_fwd", x.shape[0], [(x, "tile"), (norm_w, "full")],
                      [(x.shape, MXU, "tile")])[0]


def _conv_taps(ext_ref, cw_ref, cols, tm):
    c = None
    for j in range(4):
        term = cw_ref[3 - j:4 - j, cols] * ext_ref[SUBLANES - j:SUBLANES - j + tm, cols]
        c = term if c is None else c + term
    return c


def _fill_ext(ext_ref, u_ref, halo_ref, first):
    ext_ref[0:SUBLANES, :] = jnp.where(first, 0.0, halo_ref[...])
    ext_ref[SUBLANES:, :] = u_ref[...]


def _dn_prep_fwd(qkv_pre, ba, conv_w8, alog_row, dtb_row):
    s = qkv_pre.shape[0]
    tm = ROW_TILE

    def body(u_ref, halo_ref, cw_ref, ba_ref, al_ref, dtb_ref, q_ref, k_ref, v_ref, bg_ref, ext_ref):
        _fill_ext(ext_ref, u_ref, halo_ref, pl.program_id(0) == 0)
        for h in range(3 * DN_HEADS):
            cols = slice(h * LANES, (h + 1) * LANES)
            c = _conv_taps(ext_ref, cw_ref, cols, tm)
            a = c * _sigmoid(c)
            oc = slice((h % DN_HEADS) * LANES, (h % DN_HEADS + 1) * LANES)
            if h < 2 * DN_HEADS:
                rinv = lax.rsqrt(jnp.sum(a * a, axis=-1, keepdims=True) + NORM_EPS)
                if h < DN_HEADS:
                    q_ref[:, oc] = a * (rinv * DN_DK ** -0.5)
                else:
                    k_ref[:, oc] = a * rinv
            else:
                v_ref[:, oc] = a
        bav = ba_ref[...]
        lane = lax.broadcasted_iota(jnp.int32, bav.shape, 1)
        beta = _sigmoid(bav)
        g = -jnp.exp(al_ref[...]) * _softplus(bav + dtb_ref[...])
        bg_ref[...] = jnp.where(lane < DN_HEADS, beta, jnp.where(lane < 2 * DN_HEADS, g, 0.0))

    return _rows_call(
        body, "dn_prep_fwd", s,
        [(qkv_pre, "tile"), (qkv_pre, "prev8"), (conv_w8, "full"), (ba, "tile"), (alog_row, "full"), (dtb_row, "full")],
        [((s, D_MODEL), F32, "tile")] * 3 + [((s, LANES), F32, "tile")],
        scratch=[pltpu.VMEM((tm + SUBLANES, QKV_W), F32)])


def _dn_prep_bwd(qkv_pre, ba, conv_w8, alog_row, dtb_row, dq, dk, dv, dbg):
    s = qkv_pre.shape[0]
    tm = ROW_TILE

    def body(u_ref, halo_ref, cw_ref, ba_ref, al_ref, dtb_ref, dq_ref, dk_ref, dv_ref, dbg_ref,
             dc_ref, dba_ref, dsmall_ref, ext_ref):
        _fill_ext(ext_ref, u_ref, halo_ref, pl.program_id(0) == 0)
        for h in range(3 * DN_HEADS):
            cols = slice(h * LANES, (h + 1) * LANES)
            oc = slice((h % DN_HEADS) * LANES, (h % DN_HEADS + 1) * LANES)
            c = _conv_taps(ext_ref, cw_ref, cols, tm)
            sg = _sigmoid(c)
            a = c * sg
            if h < 2 * DN_HEADS:
                rinv = lax.rsqrt(jnp.sum(a * a, axis=-1, keepdims=True) + NORM_EPS)
                dy = dq_ref[:, oc] * DN_DK ** -0.5 if h < DN_HEADS else dk_ref[:, oc]
                da = rinv * dy - a * (rinv * rinv * rinv) * jnp.sum(dy * a, axis=-1, keepdims=True)
            else:
                da = dv_ref[:, oc]
            dc_ref[:, cols] = da * (sg * (1.0 + c * (1.0 - sg)))
        bav = ba_ref[...]
        dbgv = dbg_ref[...]
        lane = lax.broadcasted_iota(jnp.int32, bav.shape, 1)
        beta = _sigmoid(bav)
        ea = jnp.exp(al_ref[...])
        z = bav + dtb_ref[...]
        g = -ea * _softplus(z)
        is_b = lane < DN_HEADS
        is_g = jnp.logical_and(lane >= DN_HEADS, lane < 2 * DN_HEADS)
        d_aa = jnp.where(is_g, dbgv * (-ea) * _sigmoid(z), 0.0)
        dba = jnp.where(is_b, dbgv * beta * (1.0 - beta), d_aa)
        dba_ref[...] = dba.astype(dba_ref.dtype)
        r_alog = jnp.sum(jnp.where(is_g, dbgv * g, 0.0), axis=0, keepdims=True)
        r_dtb = jnp.sum(d_aa, axis=0, keepdims=True)
        _acc_add(dsmall_ref, jnp.concatenate([r_alog, r_dtb, jnp.zeros((SUBLANES - 2, LANES), F32)], axis=0))

    return _rows_call(
        body, "dn_prep_bwd", s,
        [(qkv_pre, "tile"), (qkv_pre, "prev8"), (conv_w8, "full"), (ba, "tile"), (alog_row, "full"), (dtb_row, "full"),
         (dq, "tile"), (dk, "tile"), (dv, "tile"), (dbg, "tile")],
        [((s, QKV_W), F32, "tile"), ((s, LANES), MXU, "tile"), ((SUBLANES, LANES), F32, "acc")],
        scratch=[pltpu.VMEM((tm + SUBLANES, QKV_W), F32)])


def _conv_bwd(dc, qkv_pre, conv_w8):
    s = dc.shape[0]
    tm = ROW_TILE
    steps = s // tm

    def body(dc_ref, dnext_ref, u_ref, halo_ref, cw_ref, du_ref, dcw_ref, extd_ref, ext_ref):
        i = pl.program_id(0)
        _fill_ext(ext_ref, u_ref, halo_ref, i == 0)
        extd_ref[0:tm, :] = dc_ref[...]
        extd_ref[tm:, :] = jnp.where(i == steps - 1, 0.0, dnext_ref[...])

        @pl.when(i == 0)
        def _():
            dcw_ref[...] = jnp.zeros_like(dcw_ref)

        for h in range(3 * DN_HEADS):
            cols = slice(h * LANES, (h + 1) * LANES)
            du = None
            for j in range(4):
                term = cw_ref[3 - j:4 - j, cols] * extd_ref[j:j + tm, cols]
                du = term if du is None else du + term
            du_ref[:, cols] = du.astype(du_ref.dtype)
            dcv = dc_ref[:, cols]
            for j in range(4):
                row = jnp.sum(dcv * ext_ref[SUBLANES - j:SUBLANES - j + tm, cols], axis=0, keepdims=True)
                dcw_ref[3 - j:4 - j, cols] += row

    return _rows_call(
        body, "conv_bwd", s,
        [(dc, "tile"), (dc, "next8"), (qkv_pre, "tile"), (qkv_pre, "prev8"), (conv_w8, "full")],
        [((s, QKV_W), MXU, "tile"), ((SUBLANES, QKV_W), F32, "acc")],
        scratch=[pltpu.VMEM((tm + SUBLANES, QKV_W), F32), pltpu.VMEM((tm + SUBLANES, QKV_W), F32)])


def _dn_post_fwd(o, z, dnw_row):
    def body(o_ref, z_ref, w_ref, on_ref):
        for h in range(DN_HEADS):
            cols = slice(h * LANES, (h + 1) * LANES)
            ov = o_ref[:, cols]
            zv = z_ref[:, cols]
            ro = lax.rsqrt(jnp.mean(ov * ov, axis=-1, keepdims=True) + NORM_EPS)
            on_ref[:, cols] = (ov * ro * w_ref[...] * (zv * _sigmoid(zv))).astype(on_ref.dtype)

    return _rows_call(body, "dn_post_fwd", o.shape[0], [(o, "tile"), (z, "tile"), (dnw_row, "full")],
                      [(o.shape, MXU, "tile")])[0]


def _dn_post_bwd(d_on, o, z, dnw_row):
    def body(d_ref, o_ref, z_ref, w_ref, do_ref, dz_ref, dw_ref):
        acc = jnp.zeros((1, LANES), F32)
        for h in range(DN_HEADS):
            cols = slice(h * LANES, (h + 1) * LANES)
            dv, ov, zv = d_ref[:, cols], o_ref[:, cols], z_ref[:, cols]
            sg = _sigmoid(zv)
            sz = zv * sg
            ro = lax.rsqrt(jnp.mean(ov * ov, axis=-1, keepdims=True) + NORM_EPS)
            nv = ov * ro
            dn = dv * w_ref[...] * sz
            acc = acc + jnp.sum(dv * nv * sz, axis=0, keepdims=True)
            dz_ref[:, cols] = (dv * nv * w_ref[...] * (sg * (1.0 + zv * (1.0 - sg)))).astype(dz_ref.dtype)
            do_ref[:, cols] = ro * dn - ov * (ro * ro * ro) * jnp.mean(dn * ov, axis=-1, keepdims=True)
        _acc_add(dw_ref, jnp.concatenate([acc, jnp.zeros((SUBLANES - 1, LANES), F32)], axis=0))

    return _rows_call(body, "dn_post_bwd", o.shape[0],
                      [(d_on, "tile"), (o, "tile"), (z, "tile"), (dnw_row, "full")],
                      [(o.shape, F32, "tile"), (o.shape, MXU, "tile"), ((SUBLANES, LANES), F32, "acc")])


def _attn_merge_fwd(parts, lses, zb):
    def body(o0, o1, o2, l0, l1, l2, z_ref, lse_ref, o_ref, g_ref):
        a, b, c = l0[...], l1[...], l2[...]
        m = jnp.maximum(a, jnp.maximum(b, c))
        ea, eb, ec = jnp.exp(a - m), jnp.exp(b - m), jnp.exp(c - m)
        den = ea + eb + ec
        out = (ea * o0[...] + eb * o1[...] + ec * o2[...]) / den
        lse_ref[...] = m + jnp.log(den)
        o_ref[...] = out
        zv = z_ref[...]
        g_ref[...] = (out * (zv * _sigmoid(zv))).astype(g_ref.dtype)

    s = zb.shape[0]
    return _rows_call(body, "attn_merge_fwd", s, [(p, "tile") for p in parts] + [(l, "tile") for l in lses] + [(zb, "tile")],
                      [((s, DIL_W), F32, "tile"), ((s, DIL_W), F32, "tile"), ((s, DIL_W), MXU, "tile")])


def _attn_merge_bwd(d_gated, o_joint, zb):
    def body(d_ref, o_ref, z_ref, do_ref, dz_ref, dl_ref):
        zv = z_ref[...]
        sg = _sigmoid(zv)
        dv = d_ref[...]
        ov = o_ref[...]
        do = dv * (zv * sg)
        do_ref[...] = do
        dz_ref[...] = (dv * ov * (sg * (1.0 + zv * (1.0 - sg)))).astype(dz_ref.dtype)
        for h in range(DIL_HEADS):
            cols = slice(h * LANES, (h + 1) * LANES)
            dl_ref[:, cols] = jnp.broadcast_to(jnp.sum(do[:, cols] * ov[:, cols], axis=-1, keepdims=True),
                                               (do.shape[0], LANES))

    s = zb.shape[0]
    return _rows_call(body, "attn_merge_bwd", s, [(d_gated, "tile"), (o_joint, "tile"), (zb, "tile")],
                      [((s, DIL_W), F32, "tile"), ((s, DIL_W), MXU, "tile"), ((s, DIL_W), F32, "tile")])


def _gate_merge_fwd(ga, gb, ya, yb):
    def body(ga_ref, gb_ref, ya_ref, yb_ref, m_ref):
        m_ref[...] = (_sigmoid(ga_ref[...]) * ya_ref[...] + _sigmoid(gb_ref[...]) * yb_ref[...]).astype(m_ref.dtype)

    return _rows_call(body, "gate_merge_fwd", ga.shape[0], [(ga, "tile"), (gb, "tile"), (ya, "tile"), (yb, "tile")],
                      [(ga.shape, MXU, "tile")])[0]


def _gate_merge_bwd(dm, ga, gb, ya, yb):
    def body(dm_ref, ga_ref, gb_ref, ya_ref, yb_ref, dya_ref, dyb_ref, dga_ref, dgb_ref):
        dmv = dm_ref[...]
        sa, sb = _sigmoid(ga_ref[...]), _sigmoid(gb_ref[...])
        dya_ref[...] = (dmv * sa).astype(dya_ref.dtype)
        dyb_ref[...] = (dmv * sb).astype(dyb_ref.dtype)
        dga_ref[...] = (dmv * ya_ref[...] * sa * (1.0 - sa)).astype(dga_ref.dtype)
        dgb_ref[...] = (dmv * yb_ref[...] * sb * (1.0 - sb)).astype(dgb_ref.dtype)

    return _rows_call(body, "gate_merge_bwd", ga.shape[0],
                      [(dm, "tile"), (ga, "tile"), (gb, "tile"), (ya, "tile"), (yb, "tile")],
                      [(ga.shape, MXU, "tile")] * 4)


def _final_fwd_bwd(x, x2pre, target, wf_row):
    s, dm = x.shape

    def body(x_ref, p_ref, t_ref, w_ref, loss_ref, dw_ref, dx_ref, dxb_ref):
        x2 = x_ref[...] + p_ref[...]
        r = lax.rsqrt(jnp.mean(x2 * x2, axis=-1, keepdims=True) + NORM_EPS)
        w = w_ref[...]
        err = x2 * r * w - t_ref[...]
        tile_loss = 0.5 * jnp.sum(jnp.mean(err * err, axis=-1, keepdims=True), axis=0, keepdims=True)
        _acc_add(loss_ref, jnp.broadcast_to(tile_loss, (SUBLANES, LANES)))
        dy = err * (1.0 / dm)
        row = jnp.sum(dy * x2 * r, axis=0, keepdims=True)
        _acc_add(dw_ref, jnp.concatenate([row, jnp.zeros((SUBLANES - 1, dm), F32)], axis=0))
        dn = dy * w
        dx2 = r * dn - x2 * (r * r * r) * jnp.mean(dn * x2, axis=-1, keepdims=True)
        dx_ref[...] = dx2
        dxb_ref[...] = dx2.astype(dxb_ref.dtype)

    return _rows_call(body, "final_fwd_bwd", s, [(x, "tile"), (x2pre, "tile"), (target, "tile"), (wf_row, "full")],
                      [((SUBLANES, LANES), F32, "acc"), ((SUBLANES, dm), F32, "acc"), ((s, dm), F32, "tile"),
                       ((s, dm), MXU, "tile")])


def _lane_pick(x, idx):
    lane = lax.broadcasted_iota(jnp.int32, x.shape, 1)
    return jnp.sum(jnp.where(lane == idx, x, 0.0), axis=-1, keepdims=True)


PAIR = 2 * DN_CHUNK
SCAN_CHUNKS = 4


def _bmm(a, b):
    return lax.dot_general(a.astype(MXU), b.astype(MXU), (((2,), (1,)), ((0,), (0,))), preferred_element_type=F32)


def _bmm_nt(a, b):
    return lax.dot_general(a.astype(MXU), b.astype(MXU), (((2,), (2,)), ((0,), (0,))), preferred_element_type=F32)


def _bmm_tn(a, b):
    return lax.dot_general(a.astype(MXU), b.astype(MXU), (((1,), (1,)), ((0,), (0,))), preferred_element_type=F32)


def _bmm3(a, b):
    ah = a.astype(jnp.bfloat16)
    al = (a - ah.astype(F32)).astype(jnp.bfloat16)
    bh = b.astype(jnp.bfloat16)
    bl = (b - bh.astype(F32)).astype(jnp.bfloat16)
    f = lambda p, q: lax.dot_general(p, q, (((2,), (1,)), ((0,), (0,))), preferred_element_type=F32)
    return f(ah, bh) + (f(ah, bl) + f(al, bh))


def _pair_masks():
    row = lax.broadcasted_iota(jnp.int32, (PAIR, PAIR), 0)
    col = lax.broadcasted_iota(jnp.int32, (PAIR, PAIR), 1)
    same = (row >= DN_CHUNK) == (col >= DN_CHUNK)
    return dict(causal=same & (row >= col), strict=same & (row > col), upper=same & (row <= col), eye=row == col,
                first=row < DN_CHUNK, row=row, lane=col)


def _pair_decay(bgv, masks):
    gc_all = _dot01(masks["causal"].astype(F32), bgv)
    out = []
    for h in range(DN_HEADS):
        beta = _lane_pick(bgv, h)
        gcb = jnp.broadcast_to(_lane_pick(gc_all, DN_HEADS + h), (PAIR, PAIR))
        gam = jnp.where(masks["causal"], jnp.exp(jnp.minimum(gcb - gcb.T, 0.0)), 0.0)
        gl = jnp.where(masks["first"], gcb[DN_CHUNK - 1:DN_CHUNK, :], gcb[PAIR - 1:PAIR, :])
        out.append((beta, gcb, gam, gl))
    return out


def _pair_inverse(a_strict, eye):
    n = -a_strict
    t = eye.astype(F32)[None] + n
    p = n
    for _ in range(int(math.log2(DN_CHUNK)) - 1):
        p = _bmm3(p, p)
        t = t + _bmm3(t, p)
    return t


def _head_cols(h):
    return slice(h * LANES, (h + 1) * LANES)


def _delta_prep(q, k, v, bg):
    s = q.shape[0]
    c = DN_CHUNK
    n_chunks = s // c

    def body(q_ref, k_ref, v_ref, bg_ref, u_ref, w_ref, qd_ref, kd_ref, aqk_ref, dl_ref, t2_ref):
        masks = _pair_masks()
        dec = _pair_decay(bg_ref[...], masks)
        kbs, ks, gams, vbs, kbes, qs, qds, kds, dls = ([] for _ in range(9))
        for h in range(DN_HEADS):
            beta, gcb, gam, gl = dec[h]
            qh, kh, vh = q_ref[:, _head_cols(h)], k_ref[:, _head_cols(h)], v_ref[:, _head_cols(h)]
            eg = jnp.exp(gcb)
            kb = kh * beta
            kbs.append(kb); ks.append(kh); gams.append(gam); vbs.append(vh * beta); kbes.append(kb * eg)
            qs.append(qh); qds.append(qh * eg); kds.append(kh * jnp.exp(gl - gcb)); dls.append(jnp.exp(gl))
        st = lambda xs: jnp.stack(xs, axis=0)
        kmat, gam = st(ks), st(gams)
        a = jnp.where(masks["strict"][None], _bmm_nt(st(kbs), kmat) * gam, 0.0)
        t = _pair_inverse(a, masks["eye"])
        u = _bmm(t, st(vbs))
        w = _bmm(t, st(kbes))
        aqk = _bmm_nt(st(qs), kmat) * gam
        t2_ref[0] = t.astype(t2_ref.dtype)
        for half in range(2):
            rows = slice(half * c, (half + 1) * c)
            u_ref[half] = u[:, rows, :]
            w_ref[half] = w[:, rows, :].astype(w_ref.dtype)
            qd_ref[half] = st(qds)[:, rows, :].astype(qd_ref.dtype)
            kd_ref[half] = st(kds)[:, rows, :].astype(kd_ref.dtype)
            aqk_ref[half] = aqk[:, rows, rows].astype(aqk_ref.dtype)
            dl_ref[half] = st(dls)[:, half * c:half * c + SUBLANES, :]

    row_spec = lambda w_: pl.BlockSpec((PAIR, w_), lambda i: (i, 0))
    hm = lambda a_, b_: pl.BlockSpec((2, DN_HEADS, a_, b_), lambda i: (i, 0, 0, 0))
    hm_shape = lambda a_, b_, dt: jax.ShapeDtypeStruct((n_chunks, DN_HEADS, a_, b_), dt)
    return _pcall(
        body, name="delta_prep", grid=(n_chunks // 2,),
        in_specs=[row_spec(D_MODEL)] * 3 + [row_spec(LANES)],
        out_specs=[hm(c, LANES)] * 4 + [hm(c, c), hm(SUBLANES, LANES),
                   pl.BlockSpec((1, DN_HEADS, PAIR, PAIR), lambda i: (i, 0, 0, 0))],
        out_shape=[hm_shape(c, LANES, F32), hm_shape(c, LANES, MXU), hm_shape(c, LANES, MXU), hm_shape(c, LANES, MXU),
                   hm_shape(c, c, MXU), hm_shape(SUBLANES, LANES, F32),
                   jax.ShapeDtypeStruct((n_chunks // 2, DN_HEADS, PAIR, PAIR), MXU)],
        compiler_params=_params("parallel"),
    )(q, k, v, bg)


def _delta_scan_fwd(u, w, qd, kd, aqk, dl):
    n_chunks = u.shape[0]
    c = DN_CHUNK
    g_n = SCAN_CHUNKS

    def body(u_ref, w_ref, qd_ref, kd_ref, aqk_ref, dl_ref, o_ref, vnew_ref, st_ref, state):
        @pl.when(pl.program_id(0) == 0)
        def _():
            state[...] = jnp.zeros_like(state)

        for g in range(g_n):
            sv = state[...]
            sb = sv.astype(MXU)
            vnew = u_ref[g] - _bmm(w_ref[g], sb)
            o = _bmm(qd_ref[g], sb) + _bmm(aqk_ref[g], vnew)
            state[...] = sv * dl_ref[g][:, 0:1, :] + _bmm_tn(kd_ref[g], vnew)
            vnew_ref[g] = vnew.astype(vnew_ref.dtype)
            st_ref[g] = sb
            for h in range(DN_HEADS):
                o_ref[g * c:(g + 1) * c, _head_cols(h)] = o[h]

    hm = lambda a_, b_: pl.BlockSpec((g_n, DN_HEADS, a_, b_), lambda i: (i, 0, 0, 0))
    return _pcall(
        body, name="delta_scan_fwd", grid=(n_chunks // g_n,),
        in_specs=[hm(c, LANES)] * 4 + [hm(c, c), hm(SUBLANES, LANES)],
        out_specs=[pl.BlockSpec((g_n * c, D_MODEL), lambda i: (i, 0)), hm(c, LANES), hm(DN_DK, DN_DK)],
        out_shape=[jax.ShapeDtypeStruct((n_chunks * c, D_MODEL), F32),
                   jax.ShapeDtypeStruct((n_chunks, DN_HEADS, c, LANES), MXU),
                   jax.ShapeDtypeStruct((n_chunks, DN_HEADS, DN_DK, DN_DK), MXU)],
        scratch_shapes=[pltpu.VMEM((DN_HEADS, DN_DK, DN_DK), F32)],
        compiler_params=_params("arbitrary"),
    )(u, w, qd, kd, aqk, dl)


def _delta_scan_bwd(w, qd, kd, aqk, dl, vnew, st, do):
    n_chunks = w.shape[0]
    c = DN_CHUNK
    g_n = SCAN_CHUNKS
    steps = n_chunks // g_n

    def body(w_ref, qd_ref, kd_ref, aqk_ref, dl_ref, vnew_ref, st_ref, do_ref, dvnew_ref, dkd_ref, ddl_ref, dstate):
        @pl.when(pl.program_id(0) == 0)
        def _():
            dstate[...] = jnp.zeros_like(dstate)

        for g in reversed(range(g_n)):
            ds = dstate[...]
            dsb = ds.astype(MXU)
            doh = jnp.stack([do_ref[g * c:(g + 1) * c, _head_cols(h)] for h in range(DN_HEADS)], axis=0)
            dvnew = _bmm_tn(aqk_ref[g], doh) + _bmm(kd_ref[g], dsb)
            dkd_ref[g] = _bmm_nt(vnew_ref[g], dsb)
            ddl = jnp.sum(jnp.sum(st_ref[g].astype(F32) * ds, axis=2, keepdims=True), axis=1, keepdims=True)
            ddl_ref[g] = jnp.broadcast_to(ddl, (DN_HEADS, SUBLANES, LANES))
            dstate[...] = ds * dl_ref[g][:, 0:1, :] + _bmm_tn(qd_ref[g], doh) - _bmm_tn(w_ref[g], dvnew)
            dvnew_ref[g] = dvnew.astype(dvnew_ref.dtype)

    rev = lambda i: steps - 1 - i
    hm = lambda a_, b_: pl.BlockSpec((g_n, DN_HEADS, a_, b_), lambda i: (rev(i), 0, 0, 0))
    return _pcall(
        body, name="delta_scan_bwd", grid=(steps,),
        in_specs=[hm(c, LANES)] * 3 + [hm(c, c), hm(SUBLANES, LANES), hm(c, LANES), hm(DN_DK, DN_DK),
                  pl.BlockSpec((g_n * c, D_MODEL), lambda i: (rev(i), 0))],
        out_specs=[hm(c, LANES), hm(c, LANES), hm(SUBLANES, LANES)],
        out_shape=[jax.ShapeDtypeStruct((n_chunks, DN_HEADS, c, LANES), MXU),
                   jax.ShapeDtypeStruct((n_chunks, DN_HEADS, c, LANES), F32),
                   jax.ShapeDtypeStruct((n_chunks, DN_HEADS, SUBLANES, LANES), F32)],
        scratch_shapes=[pltpu.VMEM((DN_HEADS, DN_DK, DN_DK), F32)],
        compiler_params=_params("arbitrary"),
    )(w, qd, kd, aqk, dl, vnew, st, do)


def _delta_post_bwd(q, k, v, bg, t2, st, vnew, do, dvnew, dkd, ddl):
    s = q.shape[0]
    c = DN_CHUNK

    def body(q_ref, k_ref, v_ref, bg_ref, t2_ref, st_ref, vnew_ref, do_ref, dvnew_ref, dkd_ref, ddl_ref,
             dq_ref, dk_ref, dv_ref, dbg_ref):
        masks = _pair_masks()
        first = masks["first"][None]
        dec = _pair_decay(bg_ref[...], masks)
        st_ = lambda xs: jnp.stack(xs, axis=0)
        heads = range(DN_HEADS)
        qm_, km_, vm_, dom = (st_([r[:, _head_cols(h)] for h in heads]) for r in (q_ref, k_ref, v_ref, do_ref))
        beta = st_([dec[h][0] for h in heads])
        gcb = st_([dec[h][1] for h in heads])
        gam = st_([dec[h][2] for h in heads])
        gl = st_([dec[h][3] for h in heads])
        pair = lambda ref: jnp.concatenate([ref[0], ref[1]], axis=1)
        vnew2, dvnew2, dkd2 = pair(vnew_ref), pair(dvnew_ref), pair(dkd_ref)
        halves = lambda x: (x[:, :c, :], x[:, c:, :])
        by_state = lambda x: jnp.concatenate([_bmm_nt(xh, st_ref[i]) for i, xh in enumerate(halves(x))], axis=1)
        dqd = by_state(dom)
        dw = -by_state(dvnew2)
        ddl2 = jnp.where(first, ddl_ref[0][:, 0:1, :], ddl_ref[1][:, 0:1, :])

        eg = jnp.exp(gcb)
        egl = jnp.exp(gl - gcb)
        dl = jnp.exp(gl)
        kb = km_ * beta
        kk = _bmm_nt(kb, km_)
        a = jnp.where(masks["strict"][None], kk * gam, 0.0)
        t = t2_ref[0]
        vb = vm_ * beta
        kbe = kb * eg
        u = _bmm(t, vb)
        w = _bmm(t, kbe)
        aqk = _bmm_nt(qm_, km_) * gam
        qd = qm_ * eg
        kd = km_ * egl

        daqk = jnp.where(masks["causal"][None], _bmm_nt(dom, vnew2), 0.0)
        dvb = _bmm_tn(t, dvnew2)
        dkbe = _bmm_tn(t, dw)
        da = jnp.where(masks["strict"][None], -(_bmm_nt(dvb, u) + _bmm_nt(dkbe, w)), 0.0)
        pm = da * gam
        qmm = daqk * gam
        dkb = _bmm(pm, km_) + dkbe * eg
        dkh = _bmm_tn(pm, kb) + _bmm_tn(qmm, qm_) + dkd2 * egl + dkb * beta
        dqh = _bmm(qmm, km_) + dqd * eg
        xm = da * a + daqk * aqk
        ones = jnp.ones((DN_HEADS, PAIR, LANES), F32)
        hi, mid, lo = _split3(xm)
        colsum = _bmm_tn(hi, ones) + (_bmm_tn(mid, ones) + _bmm_tn(lo, ones))
        tmp = jnp.sum(dkd2 * kd, axis=-1, keepdims=True)
        dgc = (jnp.sum(xm, axis=-1, keepdims=True) - colsum + jnp.sum(dkbe * kbe, axis=-1, keepdims=True)
               + jnp.sum(dqd * qd, axis=-1, keepdims=True) - tmp)
        sum0 = jnp.sum(jnp.where(first, tmp, 0.0), axis=1, keepdims=True)
        sum1 = jnp.sum(jnp.where(first, 0.0, tmp), axis=1, keepdims=True)
        dgl = jnp.where(first, sum0, sum1) + ddl2 * dl
        last = (masks["row"] == c - 1) | (masks["row"] == PAIR - 1)
        dgc = dgc + jnp.where(last[None], dgl, 0.0)
        dbeta = jnp.sum(dvb * vm_, axis=-1, keepdims=True) + jnp.sum(dkb * km_, axis=-1, keepdims=True)
        dvh = dvb * beta

        lane = masks["lane"]
        dgc_lanes = jnp.zeros((PAIR, LANES), F32)
        dbg = jnp.zeros((PAIR, LANES), F32)
        for h in heads:
            dq_ref[:, _head_cols(h)] = dqh[h]
            dk_ref[:, _head_cols(h)] = dkh[h]
            dv_ref[:, _head_cols(h)] = dvh[h]
            dgc_lanes = dgc_lanes + jnp.where(lane == DN_HEADS + h, dgc[h], 0.0)
            dbg = dbg + jnp.where(lane == h, dbeta[h], 0.0)
        dbg_ref[...] = dbg + _dot01(masks["upper"].astype(F32), dgc_lanes)

    n_pairs = s // PAIR
    row_spec = lambda w_: pl.BlockSpec((PAIR, w_), lambda i: (i, 0))
    hm = lambda a_, b_: pl.BlockSpec((2, DN_HEADS, a_, b_), lambda i: (i, 0, 0, 0))
    return _pcall(
        body, name="delta_post_bwd", grid=(n_pairs,),
        in_specs=[row_spec(D_MODEL)] * 3 + [row_spec(LANES), pl.BlockSpec((1, DN_HEADS, PAIR, PAIR), lambda i: (i, 0, 0, 0)),
                  hm(DN_DK, DN_DK), hm(c, LANES), row_spec(D_MODEL), hm(c, LANES), hm(c, LANES), hm(SUBLANES, LANES)],
        out_specs=[row_spec(D_MODEL)] * 3 + [row_spec(LANES)],
        out_shape=[jax.ShapeDtypeStruct((s, D_MODEL), F32)] * 3 + [jax.ShapeDtypeStruct((s, LANES), F32)],
        compiler_params=_params("parallel"),
    )(q, k, v, bg, t2, st, vnew, do, dvnew, dkd, ddl)


def _alibi_slope(group, head):
    n = N_DIL * DIL_HEADS
    return float(2.0 ** (-8.0 * (group * DIL_HEADS + head + 1) / n))


def _attn_views(s, dil):
    l = s // dil
    assert l % ATT_BLOCK == 0, "sub-sequence length must be a whole number of attention blocks"
    return l, l // ATT_BLOCK


def _window_bias(dil, n, first_cols):
    a = lax.broadcasted_iota(jnp.int32, (ATT_BLOCK, 2 * ATT_BLOCK), 0)
    b = lax.broadcasted_iota(jnp.int32, (ATT_BLOCK, 2 * ATT_BLOCK), 1)
    dist = ATT_BLOCK + a - b
    valid = (dist >= 0) & (dist <= ATT_BLOCK) & ((b >= ATT_BLOCK) | (n > 0))
    return (dist * dil).astype(F32), valid


def _attn_fwd(qb, kb, vb, group):
    window, dil = DIL_GROUPS[group]
    assert window // dil == ATT_BLOCK
    s = qb.shape[0]
    l, nb = _attn_views(s, dil)
    view = lambda t: t.reshape(l, dil * DILQ_W)

    def body(q_ref, kp_ref, kc_ref, vp_ref, vc_ref, o_ref, lse_ref):
        n = pl.program_id(1)
        distd, valid = _window_bias(dil, n, None)
        for h in range(DIL_HEADS):
            cols = slice(h * LANES, (h + 1) * LANES)
            kk = jnp.concatenate([kp_ref[:, cols].astype(MXU), kc_ref[:, cols].astype(MXU)], axis=0)
            vv = jnp.concatenate([vp_ref[:, cols].astype(MXU), vc_ref[:, cols].astype(MXU)], axis=0)
            sc = _dot_nt(q_ref[:, cols], kk) * DIL_DH ** -0.5 - _alibi_slope(group, h) * distd
            sc = jnp.where(valid, sc, -1e30)
            mx = jnp.max(sc, axis=-1, keepdims=True)
            p = jnp.where(valid, jnp.exp(sc - mx), 0.0)
            den = jnp.sum(p, axis=-1, keepdims=True)
            o_ref[:, cols] = _dot(p, vv) / den
            lse_ref[:, cols] = jnp.broadcast_to(mx + jnp.log(den), (ATT_BLOCK, LANES))

    cur = pl.BlockSpec((ATT_BLOCK, DIL_W), lambda r, n: (n, r * N_DIL + group))
    prev = pl.BlockSpec((ATT_BLOCK, DIL_W), lambda r, n: (jnp.maximum(n - 1, 0), r * N_DIL + group))
    out = pl.BlockSpec((ATT_BLOCK, DIL_W), lambda r, n: (n, r))
    o, lse = _pcall(
        body, name=f"attn_fwd_g{group}", grid=(dil, nb),
        in_specs=[cur, prev, cur, prev, cur], out_specs=[out, out],
        out_shape=[jax.ShapeDtypeStruct((l, dil * DIL_W), F32)] * 2,
        compiler_params=_params("parallel", "parallel"),
    )(view(qb), view(kb), view(kb), view(vb), view(vb))
    return o.reshape(s, DIL_W), lse.reshape(s, DIL_W)


def _attn_bwd(qb, kb, vb, d_o, lse, delta, group):
    window, dil = DIL_GROUPS[group]
    s = qb.shape[0]
    l, nb = _attn_views(s, dil)
    view = lambda t: t.reshape(l, dil * DILQ_W)
    view_o = lambda t: t.reshape(l, dil * DIL_W)
    scale = DIL_DH ** -0.5

    def body(qc_ref, qn_ref, kp_ref, kc_ref, vp_ref, vc_ref, doc_ref, don_ref, lc_ref, ln_ref, dc_ref, dn_ref,
             dq_ref, dk_ref, dv_ref):
        n = pl.program_id(1)
        distd, valid = _window_bias(dil, n, None)
        bk = lax.broadcasted_iota(jnp.int32, (ATT_BLOCK, 2 * ATT_BLOCK), 0)
        aq = lax.broadcasted_iota(jnp.int32, (ATT_BLOCK, 2 * ATT_BLOCK), 1)
        dist_t = aq - bk
        valid_t = (dist_t >= 0) & (dist_t <= ATT_BLOCK) & ((aq < ATT_BLOCK) | (n < nb - 1))
        distd_t = (dist_t * dil).astype(F32)
        for h in range(DIL_HEADS):
            cols = slice(h * LANES, (h + 1) * LANES)
            slope = _alibi_slope(group, h)
            qc = qc_ref[:, cols].astype(MXU)
            kc = kc_ref[:, cols].astype(MXU)
            vc = vc_ref[:, cols].astype(MXU)
            doc = doc_ref[:, cols]
            kk = jnp.concatenate([kp_ref[:, cols].astype(MXU), kc], axis=0)
            vv = jnp.concatenate([vp_ref[:, cols].astype(MXU), vc], axis=0)
            sc = _dot_nt(qc, kk) * scale - slope * distd
            lse2 = jnp.concatenate([lc_ref[:, cols]] * 2, axis=1)
            del2 = jnp.concatenate([dc_ref[:, cols]] * 2, axis=1)
            p = jnp.where(valid, jnp.exp(jnp.minimum(sc - lse2, 0.0)), 0.0)
            dsc = p * (_dot_nt(doc, vv) - del2)
            dq_ref[:, cols] = (_dot(dsc, kk) * scale).astype(dq_ref.dtype)
            qq = jnp.concatenate([qc, qn_ref[:, cols].astype(MXU)], axis=0)
            doo = jnp.concatenate([doc.astype(MXU), don_ref[:, cols].astype(MXU)], axis=0)
            lse_t = jnp.concatenate([lc_ref[:, cols], ln_ref[:, cols]], axis=0).T
            del_t = jnp.concatenate([dc_ref[:, cols], dn_ref[:, cols]], axis=0).T
            sc_t = _dot_nt(kc, qq) * scale - slope * distd_t
            p_t = jnp.where(valid_t, jnp.exp(jnp.minimum(sc_t - lse_t, 0.0)), 0.0)
            ds_t = p_t * (_dot_nt(vc, doo) - del_t)
            dk_ref[:, cols] = (_dot(ds_t, qq) * scale).astype(dk_ref.dtype)
            dv_ref[:, cols] = _dot(p_t, doo).astype(dv_ref.dtype)

    def spec(width, shift, per_r, off):
        if shift < 0:
            f = lambda r, n: (jnp.maximum(n - 1, 0), r * per_r + off)
        elif shift > 0:
            f = lambda r, n: (jnp.minimum(n + 1, nb - 1), r * per_r + off)
        else:
            f = lambda r, n: (n, r * per_r + off)
        return pl.BlockSpec((ATT_BLOCK, width), f)

    qkv = lambda shift: spec(DIL_W, shift, N_DIL, group)
    row = lambda shift: spec(DIL_W, shift, 1, 0)
    dq, dk, dv = _pcall(
        body, name=f"attn_bwd_g{group}", grid=(dil, nb),
        in_specs=[qkv(0), qkv(1), qkv(-1), qkv(0), qkv(-1), qkv(0), row(0), row(1), row(0), row(1), row(0), row(1)],
        out_specs=[row(0)] * 3,
        out_shape=[jax.ShapeDtypeStruct((l, dil * DIL_W), MXU)] * 3,
        compiler_params=_params("parallel", "parallel"),
    )(view(qb), view(qb), view(kb), view(kb), view(vb), view(vb), view_o(d_o), view_o(d_o), view_o(lse), view_o(lse),
      view_o(delta), view_o(delta))
    return dq.reshape(s, DIL_W), dk.reshape(s, DIL_W), dv.reshape(s, DIL_W)


def _my_place():
    mx, my, mc = lax.axis_index("x"), lax.axis_index("y"), lax.axis_index("c")
    return mx, my, mc, 4 * mx + 2 * my + mc


def _peer(mx, my, mc, k):
    px = 1 - mx if k & 4 else mx
    py = 1 - my if k & 2 else my
    pc = 1 - mc if k & 1 else mc
    return (px, py, pc), 4 * px + 2 * py + pc


def _all_gather(xs, name):
    n = len(xs)

    def body(*refs):
        x_refs, o_refs = refs[:n], refs[n:2 * n]
        send_sems, recv_sems, local_sems = refs[2 * n:]
        mx, my, mc, me = _my_place()
        local = [pltpu.make_async_copy(x_refs[a], o_refs[a].at[me], local_sems.at[a]) for a in range(n)]
        for cp in local:
            cp.start()
        for k in range(1, N_DEV):
            peer, _ = _peer(mx, my, mc, k)
            for a in range(n):
                pltpu.make_async_remote_copy(
                    src_ref=x_refs[a], dst_ref=o_refs[a].at[me], send_sem=send_sems.at[a, k - 1],
                    recv_sem=recv_sems.at[a, k - 1], device_id=peer, device_id_type=MESH).start()
        for k in range(1, N_DEV):
            peer, peer_id = _peer(mx, my, mc, k)
            for a in range(n):
                pltpu.make_async_remote_copy(
                    src_ref=x_refs[a], dst_ref=o_refs[a].at[peer_id], send_sem=send_sems.at[a, k - 1],
                    recv_sem=recv_sems.at[a, k - 1], device_id=peer, device_id_type=MESH).wait()
        for cp in local:
            cp.wait()

    any_spec = pl.BlockSpec(memory_space=pl.ANY)
    return _pcall(
        body, name=name,
        in_specs=[any_spec] * n, out_specs=[any_spec] * n,
        out_shape=[jax.ShapeDtypeStruct((N_DEV,) + x.shape, x.dtype) for x in xs],
        scratch_shapes=[pltpu.SemaphoreType.DMA((n, N_DEV - 1)), pltpu.SemaphoreType.DMA((n, N_DEV - 1)),
                        pltpu.SemaphoreType.DMA((n,))],
    )(*xs)


def _reduce_scatter_parts(gs, name):
    n = len(gs)

    def body(*refs):
        g_refs, o_refs = refs[:n], refs[n:2 * n]
        send_sems, recv_sems, local_sems = refs[2 * n:]
        mx, my, mc, me = _my_place()
        local = [pltpu.make_async_copy(g_refs[a].at[me], o_refs[a].at[me], local_sems.at[a]) for a in range(n)]
        for cp in local:
            cp.start()
        for k in range(1, N_DEV):
            peer, peer_id = _peer(mx, my, mc, k)
            for a in range(n):
                pltpu.make_async_remote_copy(
                    src_ref=g_refs[a].at[peer_id], dst_ref=o_refs[a].at[me], send_sem=send_sems.at[a, k - 1],
                    recv_sem=recv_sems.at[a, k - 1], device_id=peer, device_id_type=MESH).start()
        for k in range(1, N_DEV):
            peer, peer_id = _peer(mx, my, mc, k)
            for a in range(n):
                pltpu.make_async_remote_copy(
                    src_ref=g_refs[a].at[peer_id], dst_ref=o_refs[a].at[peer_id], send_sem=send_sems.at[a, k - 1],
                    recv_sem=recv_sems.at[a, k - 1], device_id=peer, device_id_type=MESH).wait()
        for cp in local:
            cp.wait()

    any_spec = pl.BlockSpec(memory_space=pl.ANY)
    return _pcall(
        body, name=name,
        in_specs=[any_spec] * n, out_specs=[any_spec] * n,
        out_shape=[jax.ShapeDtypeStruct(g.shape, g.dtype) for g in gs],
        scratch_shapes=[pltpu.SemaphoreType.DMA((n, N_DEV - 1)), pltpu.SemaphoreType.DMA((n, N_DEV - 1)),
                        pltpu.SemaphoreType.DMA((n,))],
    )(*gs)


def _adamw(parts, w, m, v, name):
    r, c = w.shape
    tr = 128 if r % 128 == 0 and r > 128 else r
    bc1 = 1.0 - ADAM_B1 ** ADAM_STEP
    bc2 = 1.0 - ADAM_B2 ** ADAM_STEP

    def body(p_ref, w_ref, m_ref, v_ref, g_ref, d_ref, nm_ref, nv_ref):
        g = p_ref[0].astype(F32)
        for j in range(1, N_DEV):
            g = g + p_ref[j].astype(F32)
        nm = ADAM_B1 * m_ref[...] + (1.0 - ADAM_B1) * g
        nv = ADAM_B2 * v_ref[...] + (1.0 - ADAM_B2) * (g * g)
        g_ref[...] = g
        nm_ref[...] = nm
        nv_ref[...] = nv
        d_ref[...] = -ADAM_LR * ((nm / bc1) / (jnp.sqrt(nv / bc2) + ADAM_EPS) + ADAM_WD * w_ref[...])

    blk = pl.BlockSpec((tr, c), lambda i: (i, 0))
    return _pcall(
        body, name=name, grid=(r // tr,),
        in_specs=[pl.BlockSpec((N_DEV, tr, c), lambda i: (0, i, 0)), blk, blk, blk],
        out_specs=[blk] * 4, out_shape=[jax.ShapeDtypeStruct((r, c), F32)] * 4,
        compiler_params=_params("parallel"),
    )(parts, w, m, v)


def _local_step(x, target, norm_w, w_segs, conv_w, a_log, dt_bias, dn_norm_w, w_o_dn, w_o_dil, w_out, final_norm_w):
    s = x.shape[0]
    w_qkv, w_za, w_ba, w_qb, w_kb, w_vb, w_zb, w_ga, w_gb = w_segs
    conv_w8 = jnp.concatenate([conv_w, jnp.zeros((SUBLANES - conv_w.shape[0], QKV_W), F32)], axis=0)
    pad8 = jnp.zeros((1, DN_HEADS), F32)
    alog_row = jnp.concatenate([pad8, a_log, jnp.zeros((1, LANES - 2 * DN_HEADS), F32)], axis=1)
    dtb_row = jnp.concatenate([pad8, dt_bias, jnp.zeros((1, LANES - 2 * DN_HEADS), F32)], axis=1)
    wf_row = final_norm_w.reshape(1, D_MODEL)

    hb = _rms_in_fwd(x, norm_w)
    qkv_pre, z_a, ba, z_b = _mm_nn(hb, [w_qkv, w_za, w_ba, w_zb], "proj_fwd_a")
    q_b, k_b, v_b, g_a, g_b = _mm_nn(hb, [w_qb, w_kb, w_vb, w_ga, w_gb], "proj_fwd_b")

    qn, kn, vn, bg = _dn_prep_fwd(qkv_pre, ba, conv_w8, alog_row, dtb_row)
    u_d, w_d, qd_d, kd_d, aqk_d, dl_d, t2_d = _delta_prep(qn, kn, vn, bg)
    o_a, vnew_d, st_d = _delta_scan_fwd(u_d, w_d, qd_d, kd_d, aqk_d, dl_d)
    on_b = _dn_post_fwd(o_a, z_a, dn_norm_w)
    y_a = _mm_nn(on_b, [w_o_dn], "out_dn_fwd")[0]

    parts, lses = [], []
    for gi in range(N_DIL):
        o_g, l_g = _attn_fwd(q_b, k_b, v_b, gi)
        parts.append(o_g)
        lses.append(l_g)
    lse, o_joint, ob_b = _attn_merge_fwd(parts, lses, z_b)
    y_b = _mm_nn(ob_b, [w_o_dil], "out_dil_fwd")[0]

    merged_b = _gate_merge_fwd(g_a, g_b, y_a, y_b)
    x2pre = _mm_nn(merged_b, [w_out], "out_fwd")[0]
    loss8, dwf8, dx2, dx2_b = _final_fwd_bwd(x, x2pre, target, wf_row)

    d_merged = _mm_nt_sum([dx2_b], [w_out], "out_bwd")
    g_w_out = _mm_tn(merged_b, dx2_b, "out_wgrad")
    dya_b, dyb_b, dga_b, dgb_b = _gate_merge_bwd(d_merged, g_a, g_b, y_a, y_b)

    d_on = _mm_nt_sum([dya_b], [w_o_dn], "out_dn_bwd")
    g_w_o_dn = _mm_tn(on_b, dya_b, "out_dn_wgrad")
    d_o_a, dza_b, ddnw8 = _dn_post_bwd(d_on, o_a, z_a, dn_norm_w)

    d_ob = _mm_nt_sum([dyb_b], [w_o_dil], "out_dil_bwd")
    g_w_o_dil = _mm_tn(ob_b, dyb_b, "out_dil_wgrad")
    d_o, dzb_b, delta = _attn_merge_bwd(d_ob, o_joint, z_b)
    dqs, dks, dvs = [], [], []
    for gi in range(N_DIL):
        dq_g, dk_g, dv_g = _attn_bwd(q_b, k_b, v_b, d_o, lse, delta, gi)
        dqs.append(dq_g)
        dks.append(dk_g)
        dvs.append(dv_g)
    dqb_b, dkb_b, dvb_b = (jnp.concatenate(t, axis=1) for t in (dqs, dks, dvs))

    dvnew_d, dkd_d, ddl_d = _delta_scan_bwd(w_d, qd_d, kd_d, aqk_d, dl_d, vnew_d, st_d, d_o_a)
    dqn, dkn, dvn, dbg = _delta_post_bwd(qn, kn, vn, bg, t2_d, st_d, vnew_d, d_o_a, dvnew_d, dkd_d, ddl_d)
    dc, dba_b, dsmall8 = _dn_prep_bwd(qkv_pre, ba, conv_w8, alog_row, dtb_row, dqn, dkn, dvn, dbg)
    dqkv_b, dconv8 = _conv_bwd(dc, qkv_pre, conv_w8)

    dh_a = _mm_nt_sum([dqkv_b, dza_b, dba_b, dzb_b], [w_qkv, w_za, w_ba, w_zb], "proj_bwd_a")
    dh_b = _mm_nt_sum([dqb_b, dkb_b, dvb_b, dga_b, dgb_b], [w_qb, w_kb, w_vb, w_ga, w_gb], "proj_bwd_b")
    dsegs = [dqkv_b, dza_b, dba_b, dqb_b, dkb_b, dvb_b, dzb_b, dga_b, dgb_b]
    g_segs = [_mm_tn(hb, d, f"proj_wgrad_{j}") for j, d in enumerate(dsegs)]
    grad_x, dnw8 = _rms_in_bwd_sum(x, dh_a, dh_b, dx2, norm_w)

    small = dict(norm_w=dnw8[0:1], final_norm_w=dwf8[0:1], dn_norm_w=ddnw8[0:1],
                 a_log=dsmall8[0:1, DN_HEADS:2 * DN_HEADS], dt_bias=dsmall8[1:2, DN_HEADS:2 * DN_HEADS])
    return loss8[0:1, 0:1], grad_x, g_segs, dconv8[0:4], g_w_o_dn, g_w_o_dil, g_w_out, small


def _rms_in_bwd_sum(x, dh_a, dh_b, dx2, norm_w):
    def body(x_ref, da_ref, db_ref, dx2_ref, w_ref, dx_ref, dw_ref):
        xv = x_ref[...]
        r = lax.rsqrt(jnp.mean(xv * xv, axis=-1, keepdims=True) + NORM_EPS)
        dhv = da_ref[...] + db_ref[...]
        dn = dhv * w_ref[...]
        dx_ref[...] = dx2_ref[...] + r * dn - xv * (r * r * r) * jnp.mean(dn * xv, axis=-1, keepdims=True)
        row = jnp.sum(dhv * xv * r, axis=0, keepdims=True)
        _acc_add(dw_ref, jnp.concatenate([row, jnp.zeros((SUBLANES - 1, row.shape[1]), F32)], axis=0))

    return _rows_call(body, "rms_in_bwd", x.shape[0],
                      [(x, "tile"), (dh_a, "tile"), (dh_b, "tile"), (dx2, "tile"), (norm_w, "full")],
                      [(x.shape, F32, "tile"), ((SUBLANES, x.shape[1]), F32, "acc")])


def _split_proj_cols(w_full):
    offs = [0]
    for n in PROJ_SIZES:
        offs.append(offs[-1] + n)
    seg = lambda a, b: w_full[:, offs[a]:offs[b]]
    w_ba = jnp.concatenate([seg(4, 6), jnp.zeros((w_full.shape[0], LANES - 2 * DN_HEADS), w_full.dtype)], axis=1)
    return [seg(0, 3), seg(3, 4), w_ba, seg(6, 7), seg(7, 8), seg(8, 9), seg(9, 10), seg(10, 11), seg(11, 12)]


def _join_proj_cols(g_segs):
    g_qkv, g_za, g_ba, g_qb, g_kb, g_vb, g_zb, g_ga, g_gb = g_segs
    return jnp.concatenate([g_qkv, g_za, g_ba[:, :2 * DN_HEADS], g_qb, g_kb, g_vb, g_zb, g_ga, g_gb], axis=1)


def _pack_small(norm_w, final_norm_w, dn_norm_w, a_log, dt_bias):
    pad = lambda r: jnp.concatenate([r, jnp.zeros((1, D_MODEL - r.shape[1]), F32)], axis=1)
    rows = [pad(norm_w.reshape(1, -1)), pad(final_norm_w.reshape(1, -1)), pad(dn_norm_w.reshape(1, -1)),
            pad(a_log.reshape(1, -1)), pad(dt_bias.reshape(1, -1)), jnp.zeros((SUBLANES - 5, D_MODEL), F32)]
    return jnp.concatenate(rows, axis=0)


def _unpack_small(p):
    return dict(norm_w=p[0:1], final_norm_w=p[1], dn_norm_w=p[2:3, :DN_DK], a_log=p[3:4, :DN_HEADS],
                dt_bias=p[4:5, :DN_HEADS])


def kernel(x, norm_w, w_in, conv_w, a_log, dt_bias, dn_norm_w, w_o_dn, w_o_dil, w_out, final_norm_w, loss_target, m_norm_w, m_w_in, m_conv_w, m_a_log, m_dt_bias, m_dn_norm_w, m_w_o_dn, m_w_o_dil, m_w_out, m_final_norm_w, v_norm_w, v_w_in, v_conv_w, v_a_log, v_dt_bias, v_dn_norm_w, v_w_o_dn, v_w_o_dil, v_w_out, v_final_norm_w):
    shard_w = w_in.shape[2]
    gathered = _all_gather([w_in[0].astype(MXU), w_o_dn[0].astype(MXU), w_o_dil[0].astype(MXU), w_out[0].astype(MXU),
                            conv_w[0]], "gather_weights")
    w_in_all, w_o_dn_all, w_o_dil_all, w_out_all, conv_all = gathered
    w_in_full = jnp.transpose(w_in_all, (1, 0, 2)).reshape(D_MODEL, N_DEV * shard_w)
    w_o_dn_full = w_o_dn_all.reshape(D_MODEL, D_MODEL)
    w_o_dil_full = jnp.transpose(w_o_dil_all, (1, 0, 2)).reshape(DIL_W, D_MODEL)
    w_out_full = w_out_all.reshape(D_MODEL, D_MODEL)
    conv_full = jnp.transpose(conv_all, (1, 0, 2)).reshape(conv_w.shape[1], QKV_W)

    loss11, grad_x, g_segs, g_conv, g_w_o_dn, g_w_o_dil, g_w_out, small = _local_step(
        x[0], loss_target[0], norm_w, _split_proj_cols(w_in_full), conv_full, a_log, dt_bias, dn_norm_w,
        w_o_dn_full, w_o_dil_full, w_out_full, final_norm_w)

    col_shards = lambda g, n: jnp.transpose(g.reshape(g.shape[0], N_DEV, n), (1, 0, 2))
    row_shards = lambda g: g.reshape(N_DEV, g.shape[0] // N_DEV, g.shape[1])
    sent = [col_shards(_join_proj_cols(g_segs), shard_w).astype(MXU), row_shards(g_w_o_dn).astype(MXU),
            col_shards(g_w_o_dil, w_o_dil.shape[2]).astype(MXU), row_shards(g_w_out).astype(MXU),
            col_shards(g_conv, conv_w.shape[2])]
    p_w_in, p_w_o_dn, p_w_o_dil, p_w_out, p_conv = _reduce_scatter_parts(sent, "scatter_grads")
    p_small = _all_gather([_pack_small(small["norm_w"], small["final_norm_w"], small["dn_norm_w"], small["a_log"],
                                       small["dt_bias"])], "gather_small_grads")[0]

    res = {}
    res["w_in"] = _adamw(p_w_in, w_in[0], m_w_in[0], v_w_in[0], "adamw_w_in")
    res["conv_w"] = _adamw(p_conv, conv_w[0], m_conv_w[0], v_conv_w[0], "adamw_conv_w")
    res["w_o_dn"] = _adamw(p_w_o_dn, w_o_dn[0], m_w_o_dn[0], v_w_o_dn[0], "adamw_w_o_dn")
    res["w_o_dil"] = _adamw(p_w_o_dil, w_o_dil[0], m_w_o_dil[0], v_w_o_dil[0], "adamw_w_o_dil")
    res["w_out"] = _adamw(p_w_out, w_out[0], m_w_out[0], v_w_out[0], "adamw_w_out")
    small_res = _adamw(p_small, _pack_small(norm_w, final_norm_w, dn_norm_w, a_log, dt_bias),
                       _pack_small(m_norm_w, m_final_norm_w, m_dn_norm_w, m_a_log, m_dt_bias),
                       _pack_small(v_norm_w, v_final_norm_w, v_dn_norm_w, v_a_log, v_dt_bias), "adamw_small")
    small_res = [_unpack_small(t) for t in small_res]

    loss = lax.psum(loss11[0, 0], ("x", "y", "c"))
    names = ["norm_w", "w_in", "conv_w", "a_log", "dt_bias", "dn_norm_w", "w_o_dn", "w_o_dil", "w_out", "final_norm_w"]
    outs = [loss, grad_x[None]]
    for kind in range(4):
        for nm in names:
            outs.append(res[nm][kind][None] if nm in res else small_res[kind][nm])
    return tuple(outs)
```

```python
import math

import jax
import jax.numpy as jnp
from jax import lax
from jax.experimental import pallas as pl
from jax.experimental.pallas import tpu as pltpu

F32 = jnp.float32
MXU = jnp.bfloat16
MESH = pl.DeviceIdType.MESH

N_DEV = 8
D_MODEL = 1024
DN_HEADS = 8
DN_DK = 128
DN_CHUNK = 64
N_DIL = 3
DIL_HEADS = 4
DIL_DH = 128
DIL_W = DIL_HEADS * DIL_DH
DIL_GROUPS = ((128, 1), (512, 4), (2048, 16))
ATT_BLOCK = 128
NORM_EPS = 1e-6
QKV_W = 3 * D_MODEL
DILQ_W = N_DIL * DIL_W
PROJ_SIZES = (1024, 1024, 1024, 1024, 8, 8, DILQ_W, DILQ_W, DILQ_W, DIL_W, D_MODEL, D_MODEL)

ADAM_LR = 0.001
ADAM_B1 = 0.9
ADAM_B2 = 0.999
ADAM_EPS = 1e-08
ADAM_WD = 0.01
ADAM_STEP = 10

ROW_TILE = 256
LANES = 128
SUBLANES = 8
VMEM_LIMIT = 48 << 20


def _pcall(body, **kw):
    return pl.pallas_call(body, **kw)


def _params(*sem):
    return pltpu.CompilerParams(dimension_semantics=tuple(sem), vmem_limit_bytes=VMEM_LIMIT)


def _sigmoid(x):
    return 1.0 / (1.0 + jnp.exp(-x))


def _softplus(x):
    return jnp.maximum(x, 0.0) + jnp.log(1.0 + jnp.exp(-jnp.abs(x)))


def _dot(a, b):
    return jnp.dot(a.astype(MXU), b.astype(MXU), preferred_element_type=F32)


def _dot_nt(a, b):
    return lax.dot_general(a.astype(MXU), b.astype(MXU), (((1,), (1,)), ((), ())), preferred_element_type=F32)


def _split3(x):
    hi = x.astype(jnp.bfloat16)
    r1 = x - hi.astype(F32)
    mid = r1.astype(jnp.bfloat16)
    lo = (r1 - mid.astype(F32)).astype(jnp.bfloat16)
    return hi, mid, lo


def _dot01(m01, x):
    m = m01.astype(jnp.bfloat16)
    hi, mid, lo = _split3(x)
    f = lambda p: jnp.dot(m, p, preferred_element_type=F32)
    return f(hi) + (f(mid) + f(lo))


def _rows_call(body, name, n_rows, ins, outs, scratch=(), tm=ROW_TILE):
    steps = n_rows // tm
    per8 = tm // SUBLANES
    last8 = n_rows // SUBLANES - 1
    in_specs = []
    for arr, kind in ins:
        cols = arr.shape[-1]
        if kind == "tile":
            in_specs.append(pl.BlockSpec((tm, cols), lambda i: (i, 0)))
        elif kind == "full":
            in_specs.append(pl.BlockSpec(arr.shape, lambda i, nd=arr.ndim: (0,) * nd))
        elif kind == "prev8":
            in_specs.append(pl.BlockSpec((SUBLANES, cols), lambda i: (jnp.maximum(i * per8 - 1, 0), 0)))
        elif kind == "next8":
            in_specs.append(pl.BlockSpec((SUBLANES, cols), lambda i: (jnp.minimum((i + 1) * per8, last8), 0)))
        else:
            raise ValueError(kind)
    out_specs, out_shape, has_acc = [], [], False
    for shape, dtype, kind in outs:
        out_shape.append(jax.ShapeDtypeStruct(shape, dtype))
        if kind == "tile":
            out_specs.append(pl.BlockSpec((tm, shape[-1]), lambda i: (i, 0)))
        else:
            has_acc = True
            out_specs.append(pl.BlockSpec(shape, lambda i: (0, 0)))
    return _pcall(
        body, name=name, grid=(steps,), in_specs=in_specs, out_specs=out_specs, out_shape=out_shape,
        scratch_shapes=list(scratch),
        compiler_params=_params("arbitrary" if has_acc else "parallel"),
    )(*[a for a, _ in ins])


def _acc_add(ref, value):
    @pl.when(pl.program_id(0) == 0)
    def _():
        ref[...] = jnp.zeros_like(ref)
    ref[...] += value


def _col_chunks(n, width=512):
    return [(c, min(width, n - c)) for c in range(0, n, width)]


def _mm_nn(a, ws, name, out_dtype=F32, tm=ROW_TILE):
    m, k = a.shape
    ns = [w.shape[1] for w in ws]

    def body(a_ref, *refs):
        av = a_ref[...]
        for w_ref, o_ref, n in zip(refs[:len(ws)], refs[len(ws):], ns):
            for c, wd in _col_chunks(n):
                o_ref[:, c:c + wd] = jnp.dot(av, w_ref[:, c:c + wd], preferred_element_type=F32).astype(o_ref.dtype)

    return _pcall(
        body, name=name, grid=(m // tm,),
        in_specs=[pl.BlockSpec((tm, k), lambda i: (i, 0))] + [pl.BlockSpec((k, n), lambda i: (0, 0)) for n in ns],
        out_specs=[pl.BlockSpec((tm, n), lambda i: (i, 0)) for n in ns],
        out_shape=[jax.ShapeDtypeStruct((m, n), out_dtype) for n in ns],
        compiler_params=_params("parallel"),
    )(a, *ws)


def _mm_nt_sum(ds, ws, name, tm=ROW_TILE):
    m = ds[0].shape[0]
    k = ws[0].shape[0]
    ns = [d.shape[1] for d in ds]

    def body(*refs):
        d_refs, w_refs, o_ref = refs[:len(ds)], refs[len(ds):2 * len(ds)], refs[-1]
        first = True
        for d_ref, w_ref, n in zip(d_refs, w_refs, ns):
            for c, wd in _col_chunks(n, 1024):
                part = lax.dot_general(d_ref[:, c:c + wd], w_ref[:, c:c + wd], (((1,), (1,)), ((), ())),
                                       preferred_element_type=F32)
                if first:
                    o_ref[...] = part
                    first = False
                else:
                    o_ref[...] += part

    return _pcall(
        body, name=name, grid=(m // tm,),
        in_specs=[pl.BlockSpec((tm, n), lambda i: (i, 0)) for n in ns] + [pl.BlockSpec((k, n), lambda i: (0, 0)) for n in ns],
        out_specs=pl.BlockSpec((tm, k), lambda i: (i, 0)),
        out_shape=jax.ShapeDtypeStruct((m, k), F32),
        compiler_params=_params("parallel"),
    )(*ds, *ws)


def _mm_tn(a, d, name, tm=512):
    m, k = a.shape
    n = d.shape[1]
    tn = 1024 if n % 1024 == 0 else (512 if n % 512 == 0 else n)

    def body(a_ref, d_ref, o_ref):
        @pl.when(pl.program_id(1) == 0)
        def _():
            o_ref[...] = jnp.zeros_like(o_ref)
        o_ref[...] += lax.dot_general(a_ref[...], d_ref[...], (((0,), (0,)), ((), ())), preferred_element_type=F32)

    return _pcall(
        body, name=name, grid=(n // tn, m // tm),
        in_specs=[pl.BlockSpec((tm, k), lambda i, j: (j, 0)), pl.BlockSpec((tm, tn), lambda i, j: (j, i))],
        out_specs=pl.BlockSpec((k, tn), lambda i, j: (0, i)),
        out_shape=jax.ShapeDtypeStruct((k, n), F32),
        compiler_params=_params("parallel", "arbitrary"),
    )(a, d)


def _rms_in_fwd(x, norm_w):
    def body(x_ref, w_ref, h_ref):
        xv = x_ref[...]
        r = lax.rsqrt(jnp.mean(xv * xv, axis=-1, keepdims=True) + NORM_EPS)
        h_ref[...] = (xv * r * w_ref[...]).astype(h_ref.dtype)

    return _rows_call(body, "rms_in_fwd", x.shape[0], [(x, "tile"), (norm_w, "full")],
                      [(x.shape, MXU, "tile")])[0]


def _conv_taps(ext_ref, cw_ref, cols, tm):
    c = None
    for j in range(4):
        term = cw_ref[3 - j:4 - j, cols] * ext_ref[SUBLANES - j:SUBLANES - j + tm, cols]
        c = term if c is None else c + term
    return c


def _fill_ext(ext_ref, u_ref, halo_ref, first):
    ext_ref[0:SUBLANES, :] = jnp.where(first, 0.0, halo_ref[...])
    ext_ref[SUBLANES:, :] = u_ref[...]


def _dn_prep_fwd(qkv_pre, ba, conv_w8, alog_row, dtb_row):
    s = qkv_pre.shape[0]
    tm = ROW_TILE

    def body(u_ref, halo_ref, cw_ref, ba_ref, al_ref, dtb_ref, q_ref, k_ref, v_ref, bg_ref, ext_ref):
        _fill_ext(ext_ref, u_ref, halo_ref, pl.program_id(0) == 0)
        for h in range(3 * DN_HEADS):
            cols = slice(h * LANES, (h + 1) * LANES)
            c = _conv_taps(ext_ref, cw_ref, cols, tm)
            a = c * _sigmoid(c)
            oc = slice((h % DN_HEADS) * LANES, (h % DN_HEADS + 1) * LANES)
            if h < 2 * DN_HEADS:
                rinv = lax.rsqrt(jnp.sum(a * a, axis=-1, keepdims=True) + NORM_EPS)
                if h < DN_HEADS:
                    q_ref[:, oc] = a * (rinv * DN_DK ** -0.5)
                else:
                    k_ref[:, oc] = a * rinv
            else:
                v_ref[:, oc] = a
        bav = ba_ref[...]
        lane = lax.broadcasted_iota(jnp.int32, bav.shape, 1)
        beta = _sigmoid(bav)
        g = -jnp.exp(al_ref[...]) * _softplus(bav + dtb_ref[...])
        bg_ref[...] = jnp.where(lane < DN_HEADS, beta, jnp.where(lane < 2 * DN_HEADS, g, 0.0))

    return _rows_call(
        body, "dn_prep_fwd", s,
        [(qkv_pre, "tile"), (qkv_pre, "prev8"), (conv_w8, "full"), (ba, "tile"), (alog_row, "full"), (dtb_row, "full")],
        [((s, D_MODEL), F32, "tile")] * 3 + [((s, LANES), F32, "tile")],
        scratch=[pltpu.VMEM((tm + SUBLANES, QKV_W), F32)])


def _dn_prep_bwd(qkv_pre, ba, conv_w8, alog_row, dtb_row, dq, dk, dv, dbg):
    s = qkv_pre.shape[0]
    tm = ROW_TILE

    def body(u_ref, halo_ref, cw_ref, ba_ref, al_ref, dtb_ref, dq_ref, dk_ref, dv_ref, dbg_ref,
             dc_ref, dba_ref, dsmall_ref, ext_ref):
        _fill_ext(ext_ref, u_ref, halo_ref, pl.program_id(0) == 0)
        for h in range(3 * DN_HEADS):
            cols = slice(h * LANES, (h + 1) * LANES)
            oc = slice((h % DN_HEADS) * LANES, (h % DN_HEADS + 1) * LANES)
            c = _conv_taps(ext_ref, cw_ref, cols, tm)
            sg = _sigmoid(c)
            a = c * sg
            if h < 2 * DN_HEADS:
                rinv = lax.rsqrt(jnp.sum(a * a, axis=-1, keepdims=True) + NORM_EPS)
                dy = dq_ref[:, oc] * DN_DK ** -0.5 if h < DN_HEADS else dk_ref[:, oc]
                da = rinv * dy - a * (rinv * rinv * rinv) * jnp.sum(dy * a, axis=-1, keepdims=True)
            else:
                da = dv_ref[:, oc]
            dc_ref[:, cols] = da * (sg * (1.0 + c * (1.0 - sg)))
        bav = ba_ref[...]
        dbgv = dbg_ref[...]
        lane = lax.broadcasted_iota(jnp.int32, bav.shape, 1)
        beta = _sigmoid(bav)
        ea = jnp.exp(al_ref[...])
        z = bav + dtb_ref[...]
        g = -ea * _softplus(z)
        is_b = lane < DN_HEADS
        is_g = jnp.logical_and(lane >= DN_HEADS, lane < 2 * DN_HEADS)
        d_aa = jnp.where(is_g, dbgv * (-ea) * _sigmoid(z), 0.0)
        dba = jnp.where(is_b, dbgv * beta * (1.0 - beta), d_aa)
        dba_ref[...] = dba.astype(dba_ref.dtype)
        r_alog = jnp.sum(jnp.where(is_g, dbgv * g, 0.0), axis=0, keepdims=True)
        r_dtb = jnp.sum(d_aa, axis=0, keepdims=True)
        _acc_add(dsmall_ref, jnp.concatenate([r_alog, r_dtb, jnp.zeros((SUBLANES - 2, LANES), F32)], axis=0))

    return _rows_call(
        body, "dn_prep_bwd", s,
        [(qkv_pre, "tile"), (qkv_pre, "prev8"), (conv_w8, "full"), (ba, "tile"), (alog_row, "full"), (dtb_row, "full"),
         (dq, "tile"), (dk, "tile"), (dv, "tile"), (dbg, "tile")],
        [((s, QKV_W), F32, "tile"), ((s, LANES), MXU, "tile"), ((SUBLANES, LANES), F32, "acc")],
        scratch=[pltpu.VMEM((tm + SUBLANES, QKV_W), F32)])


def _conv_bwd(dc, qkv_pre, conv_w8):
    s = dc.shape[0]
    tm = ROW_TILE
    steps = s // tm

    def body(dc_ref, dnext_ref, u_ref, halo_ref, cw_ref, du_ref, dcw_ref, extd_ref, ext_ref):
        i = pl.program_id(0)
        _fill_ext(ext_ref, u_ref, halo_ref, i == 0)
        extd_ref[0:tm, :] = dc_ref[...]
        extd_ref[tm:, :] = jnp.where(i == steps - 1, 0.0, dnext_ref[...])

        @pl.when(i == 0)
        def _():
            dcw_ref[...] = jnp.zeros_like(dcw_ref)

        for h in range(3 * DN_HEADS):
            cols = slice(h * LANES, (h + 1) * LANES)
            du = None
            for j in range(4):
                term = cw_ref[3 - j:4 - j, cols] * extd_ref[j:j + tm, cols]
                du = term if du is None else du + term
            du_ref[:, cols] = du.astype(du_ref.dtype)
            dcv = dc_ref[:, cols]
            for j in range(4):
                row = jnp.sum(dcv * ext_ref[SUBLANES - j:SUBLANES - j + tm, cols], axis=0, keepdims=True)
                dcw_ref[3 - j:4 - j, cols] += row

    return _rows_call(
        body, "conv_bwd", s,
        [(dc, "tile"), (dc, "next8"), (qkv_pre, "tile"), (qkv_pre, "prev8"), (conv_w8, "full")],
        [((s, QKV_W), MXU, "tile"), ((SUBLANES, QKV_W), F32, "acc")],
        scratch=[pltpu.VMEM((tm + SUBLANES, QKV_W), F32), pltpu.VMEM((tm + SUBLANES, QKV_W), F32)])


def _dn_post_fwd(o, z, dnw_row):
    def body(o_ref, z_ref, w_ref, on_ref):
        for h in range(DN_HEADS):
            cols = slice(h * LANES, (h + 1) * LANES)
            ov = o_ref[:, cols]
            zv = z_ref[:, cols]
            ro = lax.rsqrt(jnp.mean(ov * ov, axis=-1, keepdims=True) + NORM_EPS)
            on_ref[:, cols] = (ov * ro * w_ref[...] * (zv * _sigmoid(zv))).astype(on_ref.dtype)

    return _rows_call(body, "dn_post_fwd", o.shape[0], [(o, "tile"), (z, "tile"), (dnw_row, "full")],
                      [(o.shape, MXU, "tile")])[0]


def _dn_post_bwd(d_on, o, z, dnw_row):
    def body(d_ref, o_ref, z_ref, w_ref, do_ref, dz_ref, dw_ref):
        acc = jnp.zeros((1, LANES), F32)
        for h in range(DN_HEADS):
            cols = slice(h * LANES, (h + 1) * LANES)
            dv, ov, zv = d_ref[:, cols], o_ref[:, cols], z_ref[:, cols]
            sg = _sigmoid(zv)
            sz = zv * sg
            ro = lax.rsqrt(jnp.mean(ov * ov, axis=-1, keepdims=True) + NORM_EPS)
            nv = ov * ro
            dn = dv * w_ref[...] * sz
            acc = acc + jnp.sum(dv * nv * sz, axis=0, keepdims=True)
            dz_ref[:, cols] = (dv * nv * w_ref[...] * (sg * (1.0 + zv * (1.0 - sg)))).astype(dz_ref.dtype)
            do_ref[:, cols] = ro * dn - ov * (ro * ro * ro) * jnp.mean(dn * ov, axis=-1, keepdims=True)
        _acc_add(dw_ref, jnp.concatenate([acc, jnp.zeros((SUBLANES - 1, LANES), F32)], axis=0))

    return _rows_call(body, "dn_post_bwd", o.shape[0],
                      [(d_on, "tile"), (o, "tile"), (z, "tile"), (dnw_row, "full")],
                      [(o.shape, F32, "tile"), (o.shape, MXU, "tile"), ((SUBLANES, LANES), F32, "acc")])


def _attn_merge_fwd(parts, lses, zb):
    def body(o0, o1, o2, l0, l1, l2, z_ref, lse_ref, o_ref, g_ref):
        a, b, c = l0[...], l1[...], l2[...]
        m = jnp.maximum(a, jnp.maximum(b, c))
        ea, eb, ec = jnp.exp(a - m), jnp.exp(b - m), jnp.exp(c - m)
        den = ea + eb + ec
        out = (ea * o0[...] + eb * o1[...] + ec * o2[...]) / den
        lse_ref[...] = m + jnp.log(den)
        o_ref[...] = out
        zv = z_ref[...]
        g_ref[...] = (out * (zv * _sigmoid(zv))).astype(g_ref.dtype)

    s = zb.shape[0]
    return _rows_call(body, "attn_merge_fwd", s, [(p, "tile") for p in parts] + [(l, "tile") for l in lses] + [(zb, "tile")],
                      [((s, DIL_W), F32, "tile"), ((s, DIL_W), F32, "tile"), ((s, DIL_W), MXU, "tile")])


def _attn_merge_bwd(d_gated, o_joint, zb):
    def body(d_ref, o_ref, z_ref, do_ref, dz_ref, dl_ref):
        zv = z_ref[...]
        sg = _sigmoid(zv)
        dv = d_ref[...]
        ov = o_ref[...]
        do = dv * (zv * sg)
        do_ref[...] = do
        dz_ref[...] = (dv * ov * (sg * (1.0 + zv * (1.0 - sg)))).astype(dz_ref.dtype)
        for h in range(DIL_HEADS):
            cols = slice(h * LANES, (h + 1) * LANES)
            dl_ref[:, cols] = jnp.broadcast_to(jnp.sum(do[:, cols] * ov[:, cols], axis=-1, keepdims=True),
                                               (do.shape[0], LANES))

    s = zb.shape[0]
    return _rows_call(body, "attn_merge_bwd", s, [(d_gated, "tile"), (o_joint, "tile"), (zb, "tile")],
                      [((s, DIL_W), F32, "tile"), ((s, DIL_W), MXU, "tile"), ((s, DIL_W), F32, "tile")])


def _gate_merge_fwd(ga, gb, ya, yb):
    def body(ga_ref, gb_ref, ya_ref, yb_ref, m_ref):
        m_ref[...] = (_sigmoid(ga_ref[...]) * ya_ref[...] + _sigmoid(gb_ref[...]) * yb_ref[...]).astype(m_ref.dtype)

    return _rows_call(body, "gate_merge_fwd", ga.shape[0], [(ga, "tile"), (gb, "tile"), (ya, "tile"), (yb, "tile")],
                      [(ga.shape, MXU, "tile")])[0]


def _gate_merge_bwd(dm, ga, gb, ya, yb):
    def body(dm_ref, ga_ref, gb_ref, ya_ref, yb_ref, dya_ref, dyb_ref, dga_ref, dgb_ref):
        dmv = dm_ref[...]
        sa, sb = _sigmoid(ga_ref[...]), _sigmoid(gb_ref[...])
        dya_ref[...] = (dmv * sa).astype(dya_ref.dtype)
        dyb_ref[...] = (dmv * sb).astype(dyb_ref.dtype)
        dga_ref[...] = (dmv * ya_ref[...] * sa * (1.0 - sa)).astype(dga_ref.dtype)
        dgb_ref[...] = (dmv * yb_ref[...] * sb * (1.0 - sb)).astype(dgb_ref.dtype)

    return _rows_call(body, "gate_merge_bwd", ga.shape[0],
                      [(dm, "tile"), (ga, "tile"), (gb, "tile"), (ya, "tile"), (yb, "tile")],
                      [(ga.shape, MXU, "tile")] * 4)


def _final_fwd_bwd(x, x2pre, target, wf_row):
    s, dm = x.shape

    def body(x_ref, p_ref, t_ref, w_ref, loss_ref, dw_ref, dx_ref, dxb_ref):
        x2 = x_ref[...] + p_ref[...]
        r = lax.rsqrt(jnp.mean(x2 * x2, axis=-1, keepdims=True) + NORM_EPS)
        w = w_ref[...]
        err = x2 * r * w - t_ref[...]
        tile_loss = 0.5 * jnp.sum(jnp.mean(err * err, axis=-1, keepdims=True), axis=0, keepdims=True)
        _acc_add(loss_ref, jnp.broadcast_to(tile_loss, (SUBLANES, LANES)))
        dy = err * (1.0 / dm)
        row = jnp.sum(dy * x2 * r, axis=0, keepdims=True)
        _acc_add(dw_ref, jnp.concatenate([row, jnp.zeros((SUBLANES - 1, dm), F32)], axis=0))
        dn = dy * w
        dx2 = r * dn - x2 * (r * r * r) * jnp.mean(dn * x2, axis=-1, keepdims=True)
        dx_ref[...] = dx2
        dxb_ref[...] = dx2.astype(dxb_ref.dtype)

    return _rows_call(body, "final_fwd_bwd", s, [(x, "tile"), (x2pre, "tile"), (target, "tile"), (wf_row, "full")],
                      [((SUBLANES, LANES), F32, "acc"), ((SUBLANES, dm), F32, "acc"), ((s, dm), F32, "tile"),
                       ((s, dm), MXU, "tile")])


def _lane_pick(x, idx):
    lane = lax.broadcasted_iota(jnp.int32, x.shape, 1)
    return jnp.sum(jnp.where(lane == idx, x, 0.0), axis=-1, keepdims=True)


PAIR = 2 * DN_CHUNK
SCAN_CHUNKS = 4


def _bmm(a, b):
    return lax.dot_general(a.astype(MXU), b.astype(MXU), (((2,), (1,)), ((0,), (0,))), preferred_element_type=F32)


def _bmm_nt(a, b):
    return lax.dot_general(a.astype(MXU), b.astype(MXU), (((2,), (2,)), ((0,), (0,))), preferred_element_type=F32)


def _bmm_tn(a, b):
    return lax.dot_general(a.astype(MXU), b.astype(MXU), (((1,), (1,)), ((0,), (0,))), preferred_element_type=F32)


def _bmm3(a, b):
    ah = a.astype(jnp.bfloat16)
    al = (a - ah.astype(F32)).astype(jnp.bfloat16)
    bh = b.astype(jnp.bfloat16)
    bl = (b - bh.astype(F32)).astype(jnp.bfloat16)
    f = lambda p, q: lax.dot_general(p, q, (((2,), (1,)), ((0,), (0,))), preferred_element_type=F32)
    return f(ah, bh) + (f(ah, bl) + f(al, bh))


def _pair_masks():
    row = lax.broadcasted_iota(jnp.int32, (PAIR, PAIR), 0)
    col = lax.broadcasted_iota(jnp.int32, (PAIR, PAIR), 1)
    same = (row >= DN_CHUNK) == (col >= DN_CHUNK)
    return dict(causal=same & (row >= col), strict=same & (row > col), upper=same & (row <= col), eye=row == col,
                first=row < DN_CHUNK, row=row, lane=col)


def _pair_decay(bgv, masks):
    gc_all = _dot01(masks["causal"].astype(F32), bgv)
    out = []
    for h in range(DN_HEADS):
        beta = _lane_pick(bgv, h)
        gcb = jnp.broadcast_to(_lane_pick(gc_all, DN_HEADS + h), (PAIR, PAIR))
        gam = jnp.where(masks["causal"], jnp.exp(jnp.minimum(gcb - gcb.T, 0.0)), 0.0)
        gl = jnp.where(masks["first"], gcb[DN_CHUNK - 1:DN_CHUNK, :], gcb[PAIR - 1:PAIR, :])
        out.append((beta, gcb, gam, gl))
    return out


def _pair_inverse(a_strict, eye):
    n = -a_strict
    t = eye.astype(F32)[None] + n
    p = n
    for _ in range(int(math.log2(DN_CHUNK)) - 1):
        p = _bmm3(p, p)
        t = t + _bmm3(t, p)
    return t


def _head_cols(h):
    return slice(h * LANES, (h + 1) * LANES)


def _delta_prep(q, k, v, bg):
    s = q.shape[0]
    c = DN_CHUNK
    n_chunks = s // c

    def body(q_ref, k_ref, v_ref, bg_ref, u_ref, w_ref, qd_ref, kd_ref, aqk_ref, dl_ref, t2_ref):
        masks = _pair_masks()
        dec = _pair_decay(bg_ref[...], masks)
        kbs, ks, gams, vbs, kbes, qs, qds, kds, dls = ([] for _ in range(9))
        for h in range(DN_HEADS):
            beta, gcb, gam, gl = dec[h]
            qh, kh, vh = q_ref[:, _head_cols(h)], k_ref[:, _head_cols(h)], v_ref[:, _head_cols(h)]
            eg = jnp.exp(gcb)
            kb = kh * beta
            kbs.append(kb); ks.append(kh); gams.append(gam); vbs.append(vh * beta); kbes.append(kb * eg)
            qs.append(qh); qds.append(qh * eg); kds.append(kh * jnp.exp(gl - gcb)); dls.append(jnp.exp(gl))
        st = lambda xs: jnp.stack(xs, axis=0)
        kmat, gam = st(ks), st(gams)
        a = jnp.where(masks["strict"][None], _bmm_nt(st(kbs), kmat) * gam, 0.0)
        t = _pair_inverse(a, masks["eye"])
        u = _bmm(t, st(vbs))
        w = _bmm(t, st(kbes))
        aqk = _bmm_nt(st(qs), kmat) * gam
        t2_ref[0] = t.astype(t2_ref.dtype)
        for half in range(2):
            rows = slice(half * c, (half + 1) * c)
            u_ref[half] = u[:, rows, :]
            w_ref[half] = w[:, rows, :].astype(w_ref.dtype)
            qd_ref[half] = st(qds)[:, rows, :].astype(qd_ref.dtype)
            kd_ref[half] = st(kds)[:, rows, :].astype(kd_ref.dtype)
            aqk_ref[half] = aqk[:, rows, rows].astype(aqk_ref.dtype)
            dl_ref[half] = st(dls)[:, half * c:half * c + SUBLANES, :]

    row_spec = lambda w_: pl.BlockSpec((PAIR, w_), lambda i: (i, 0))
    hm = lambda a_, b_: pl.BlockSpec((2, DN_HEADS, a_, b_), lambda i: (i, 0, 0, 0))
    hm_shape = lambda a_, b_, dt: jax.ShapeDtypeStruct((n_chunks, DN_HEADS, a_, b_), dt)
    return _pcall(
        body, name="delta_prep", grid=(n_chunks // 2,),
        in_specs=[row_spec(D_MODEL)] * 3 + [row_spec(LANES)],
        out_specs=[hm(c, LANES)] * 4 + [hm(c, c), hm(SUBLANES, LANES),
                   pl.BlockSpec((1, DN_HEADS, PAIR, PAIR), lambda i: (i, 0, 0, 0))],
        out_shape=[hm_shape(c, LANES, F32), hm_shape(c, LANES, MXU), hm_shape(c, LANES, MXU), hm_shape(c, LANES, MXU),
                   hm_shape(c, c, MXU), hm_shape(SUBLANES, LANES, F32),
                   jax.ShapeDtypeStruct((n_chunks // 2, DN_HEADS, PAIR, PAIR), MXU)],
        compiler_params=_params("parallel"),
    )(q, k, v, bg)


def _delta_scan_fwd(u, w, qd, kd, aqk, dl):
    n_chunks = u.shape[0]
    c = DN_CHUNK
    g_n = SCAN_CHUNKS

    def body(u_ref, w_ref, qd_ref, kd_ref, aqk_ref, dl_ref, o_ref, vnew_ref, st_ref, state):
        @pl.when(pl.program_id(0) == 0)
        def _():
            state[...] = jnp.zeros_like(state)

        for g in range(g_n):
            sv = state[...]
            sb = sv.astype(MXU)
            vnew = u_ref[g] - _bmm(w_ref[g], sb)
            o = _bmm(qd_ref[g], sb) + _bmm(aqk_ref[g], vnew)
            state[...] = sv * dl_ref[g][:, 0:1, :] + _bmm_tn(kd_ref[g], vnew)
            vnew_ref[g] = vnew.astype(vnew_ref.dtype)
            st_ref[g] = sb
            for h in range(DN_HEADS):
                o_ref[g * c:(g + 1) * c, _head_cols(h)] = o[h]

    hm = lambda a_, b_: pl.BlockSpec((g_n, DN_HEADS, a_, b_), lambda i: (i, 0, 0, 0))
    return _pcall(
        body, name="delta_scan_fwd", grid=(n_chunks // g_n,),
        in_specs=[hm(c, LANES)] * 4 + [hm(c, c), hm(SUBLANES, LANES)],
        out_specs=[pl.BlockSpec((g_n * c, D_MODEL), lambda i: (i, 0)), hm(c, LANES), hm(DN_DK, DN_DK)],
        out_shape=[jax.ShapeDtypeStruct((n_chunks * c, D_MODEL), F32),
                   jax.ShapeDtypeStruct((n_chunks, DN_HEADS, c, LANES), MXU),
                   jax.ShapeDtypeStruct((n_chunks, DN_HEADS, DN_DK, DN_DK), MXU)],
        scratch_shapes=[pltpu.VMEM((DN_HEADS, DN_DK, DN_DK), F32)],
        compiler_params=_params("arbitrary"),
    )(u, w, qd, kd, aqk, dl)


def _delta_scan_bwd(w, qd, kd, aqk, dl, vnew, st, do):
    n_chunks = w.shape[0]
    c = DN_CHUNK
    g_n = SCAN_CHUNKS
    steps = n_chunks // g_n

    def body(w_ref, qd_ref, kd_ref, aqk_ref, dl_ref, vnew_ref, st_ref, do_ref, dvnew_ref, dkd_ref, ddl_ref, dstate):
        @pl.when(pl.program_id(0) == 0)
        def _():
            dstate[...] = jnp.zeros_like(dstate)

        for g in reversed(range(g_n)):
            ds = dstate[...]
            dsb = ds.astype(MXU)
            doh = jnp.stack([do_ref[g * c:(g + 1) * c, _head_cols(h)] for h in range(DN_HEADS)], axis=0)
            dvnew = _bmm_tn(aqk_ref[g], doh) + _bmm(kd_ref[g], dsb)
            dkd_ref[g] = _bmm_nt(vnew_ref[g], dsb)
            ddl = jnp.sum(jnp.sum(st_ref[g].astype(F32) * ds, axis=2, keepdims=True), axis=1, keepdims=True)
            ddl_ref[g] = jnp.broadcast_to(ddl, (DN_HEADS, SUBLANES, LANES))
            dstate[...] = ds * dl_ref[g][:, 0:1, :] + _bmm_tn(qd_ref[g], doh) - _bmm_tn(w_ref[g], dvnew)
            dvnew_ref[g] = dvnew.astype(dvnew_ref.dtype)

    rev = lambda i: steps - 1 - i
    hm = lambda a_, b_: pl.BlockSpec((g_n, DN_HEADS, a_, b_), lambda i: (rev(i), 0, 0, 0))
    return _pcall(
        body, name="delta_scan_bwd", grid=(steps,),
        in_specs=[hm(c, LANES)] * 3 + [hm(c, c), hm(SUBLANES, LANES), hm(c, LANES), hm(DN_DK, DN_DK),
                  pl.BlockSpec((g_n * c, D_MODEL), lambda i: (rev(i), 0))],
        out_specs=[hm(c, LANES), hm(c, LANES), hm(SUBLANES, LANES)],
        out_shape=[jax.ShapeDtypeStruct((n_chunks, DN_HEADS, c, LANES), MXU),
                   jax.ShapeDtypeStruct((n_chunks, DN_HEADS, c, LANES), F32),
                   jax.ShapeDtypeStruct((n_chunks, DN_HEADS, SUBLANES, LANES), F32)],
        scratch_shapes=[pltpu.VMEM((DN_HEADS, DN_DK, DN_DK), F32)],
        compiler_params=_params("arbitrary"),
    )(w, qd, kd, aqk, dl, vnew, st, do)


def _delta_post_bwd(q, k, v, bg, t2, st, vnew, do, dvnew, dkd, ddl):
    s = q.shape[0]
    c = DN_CHUNK

    def body(q_ref, k_ref, v_ref, bg_ref, t2_ref, st_ref, vnew_ref, do_ref, dvnew_ref, dkd_ref, ddl_ref,
             dq_ref, dk_ref, dv_ref, dbg_ref):
        masks = _pair_masks()
        first = masks["first"][None]
        dec = _pair_decay(bg_ref[...], masks)
        st_ = lambda xs: jnp.stack(xs, axis=0)
        heads = range(DN_HEADS)
        qm_, km_, vm_, dom = (st_([r[:, _head_cols(h)] for h in heads]) for r in (q_ref, k_ref, v_ref, do_ref))
        beta = st_([dec[h][0] for h in heads])
        gcb = st_([dec[h][1] for h in heads])
        gam = st_([dec[h][2] for h in heads])
        gl = st_([dec[h][3] for h in heads])
        pair = lambda ref: jnp.concatenate([ref[0], ref[1]], axis=1)
        vnew2, dvnew2, dkd2 = pair(vnew_ref), pair(dvnew_ref), pair(dkd_ref)
        halves = lambda x: (x[:, :c, :], x[:, c:, :])
        by_state = lambda x: jnp.concatenate([_bmm_nt(xh, st_ref[i]) for i, xh in enumerate(halves(x))], axis=1)
        dqd = by_state(dom)
        dw = -by_state(dvnew2)
        ddl2 = jnp.where(first, ddl_ref[0][:, 0:1, :], ddl_ref[1][:, 0:1, :])

        eg = jnp.exp(gcb)
        egl = jnp.exp(gl - gcb)
        dl = jnp.exp(gl)
        kb = km_ * beta
        kk = _bmm_nt(kb, km_)
        a = jnp.where(masks["strict"][None], kk * gam, 0.0)
        t = t2_ref[0]
        vb = vm_ * beta
        kbe = kb * eg
        u = _bmm(t, vb)
        w = _bmm(t, kbe)
        aqk = _bmm_nt(qm_, km_) * gam
        qd = qm_ * eg
        kd = km_ * egl

        daqk = jnp.where(masks["causal"][None], _bmm_nt(dom, vnew2), 0.0)
        dvb = _bmm_tn(t, dvnew2)
        dkbe = _bmm_tn(t, dw)
        da = jnp.where(masks["strict"][None], -(_bmm_nt(dvb, u) + _bmm_nt(dkbe, w)), 0.0)
        pm = da * gam
        qmm = daqk * gam
        dkb = _bmm(pm, km_) + dkbe * eg
        dkh = _bmm_tn(pm, kb) + _bmm_tn(qmm, qm_) + dkd2 * egl + dkb * beta
        dqh = _bmm(qmm, km_) + dqd * eg
        xm = da * a + daqk * aqk
        ones = jnp.ones((DN_HEADS, PAIR, LANES), F32)
        hi, mid, lo = _split3(xm)
        colsum = _bmm_tn(hi, ones) + (_bmm_tn(mid, ones) + _bmm_tn(lo, ones))
        tmp = jnp.sum(dkd2 * kd, axis=-1, keepdims=True)
        dgc = (jnp.sum(xm, axis=-1, keepdims=True) - colsum + jnp.sum(dkbe * kbe, axis=-1, keepdims=True)
               + jnp.sum(dqd * qd, axis=-1, keepdims=True) - tmp)
        sum0 = jnp.sum(jnp.where(first, tmp, 0.0), axis=1, keepdims=True)
        sum1 = jnp.sum(jnp.where(first, 0.0, tmp), axis=1, keepdims=True)
        dgl = jnp.where(first, sum0, sum1) + ddl2 * dl
        last = (masks["row"] == c - 1) | (masks["row"] == PAIR - 1)
        dgc = dgc + jnp.where(last[None], dgl, 0.0)
        dbeta = jnp.sum(dvb * vm_, axis=-1, keepdims=True) + jnp.sum(dkb * km_, axis=-1, keepdims=True)
        dvh = dvb * beta

        lane = masks["lane"]
        dgc_lanes = jnp.zeros((PAIR, LANES), F32)
        dbg = jnp.zeros((PAIR, LANES), F32)
        for h in heads:
            dq_ref[:, _head_cols(h)] = dqh[h]
            dk_ref[:, _head_cols(h)] = dkh[h]
            dv_ref[:, _head_cols(h)] = dvh[h]
            dgc_lanes = dgc_lanes + jnp.where(lane == DN_HEADS + h, dgc[h], 0.0)
            dbg = dbg + jnp.where(lane == h, dbeta[h], 0.0)
        dbg_ref[...] = dbg + _dot01(masks["upper"].astype(F32), dgc_lanes)

    n_pairs = s // PAIR
    row_spec = lambda w_: pl.BlockSpec((PAIR, w_), lambda i: (i, 0))
    hm = lambda a_, b_: pl.BlockSpec((2, DN_HEADS, a_, b_), lambda i: (i, 0, 0, 0))
    return _pcall(
        body, name="delta_post_bwd", grid=(n_pairs,),
        in_specs=[row_spec(D_MODEL)] * 3 + [row_spec(LANES), pl.BlockSpec((1, DN_HEADS, PAIR, PAIR), lambda i: (i, 0, 0, 0)),
                  hm(DN_DK, DN_DK), hm(c, LANES), row_spec(D_MODEL), hm(c, LANES), hm(c, LANES), hm(SUBLANES, LANES)],
        out_specs=[row_spec(D_MODEL)] * 3 + [row_spec(LANES)],
        out_shape=[jax.ShapeDtypeStruct((s, D_MODEL), F32)] * 3 + [jax.ShapeDtypeStruct((s, LANES), F32)],
        compiler_params=_params("parallel"),
    )(q, k, v, bg, t2, st, vnew, do, dvnew, dkd, ddl)


def _alibi_slope(group, head):
    n = N_DIL * DIL_HEADS
    return float(2.0 ** (-8.0 * (group * DIL_HEADS + head + 1) / n))


def _attn_plan(s, group):
    window, dil = DIL_GROUPS[group]
    assert window // dil == ATT_BLOCK
    assert (s // dil) % ATT_BLOCK == 0, "sub-sequence length must be a whole number of attention blocks"
    return dil, s // dil // ATT_BLOCK, (DIL_HEADS if dil == 1 else 1)


def _attn_specs(group, dil, nb, hp):
    rows = ATT_BLOCK * dil

    def spec(col0, shift):
        if shift < 0:
            f = lambda hb, n: (jnp.maximum(n - 1, 0), col0 + hb)
        elif shift > 0:
            f = lambda hb, n: (jnp.minimum(n + 1, nb - 1), col0 + hb)
        else:
            f = lambda hb, n: (n, col0 + hb)
        return pl.BlockSpec((rows, hp * LANES), f)

    return (lambda shift: spec(group * (DIL_HEADS // hp), shift)), (lambda shift: spec(0, shift))


def _sub_rows(ref, r, dil, cols):
    return ref[:, cols] if dil == 1 else ref[pl.ds(r, ATT_BLOCK, stride=dil), cols]


def _set_sub_rows(ref, r, dil, cols, value):
    if dil == 1:
        ref[:, cols] = value
    else:
        ref[pl.ds(r, ATT_BLOCK, stride=dil), cols] = value


def _step_slope(group, hp, hh):
    if hp == DIL_HEADS:
        return _alibi_slope(group, hh)
    hb = pl.program_id(0)
    slope = _alibi_slope(group, DIL_HEADS - 1)
    for h in reversed(range(DIL_HEADS - 1)):
        slope = jnp.where(hb == h, _alibi_slope(group, h), slope)
    return slope


def _window_bias(dil, n):
    a = lax.broadcasted_iota(jnp.int32, (ATT_BLOCK, 2 * ATT_BLOCK), 0)
    b = lax.broadcasted_iota(jnp.int32, (ATT_BLOCK, 2 * ATT_BLOCK), 1)
    dist = ATT_BLOCK + a - b
    valid = (dist >= 0) & (dist <= ATT_BLOCK) & ((b >= ATT_BLOCK) | (n > 0))
    return (dist * dil).astype(F32), valid


def _attn_fwd(qb, kb, vb, group):
    s = qb.shape[0]
    dil, nb, hp = _attn_plan(s, group)
    qkv, per_head = _attn_specs(group, dil, nb, hp)

    def body(q_ref, kp_ref, kc_ref, vp_ref, vc_ref, o_ref, lse_ref):
        n = pl.program_id(1)
        distd, valid = _window_bias(dil, n)
        for hh in range(hp):
            cols = _head_cols(hh)
            slope = _step_slope(group, hp, hh)
            for r in range(dil):
                sub = lambda ref: _sub_rows(ref, r, dil, cols).astype(MXU)
                kk = jnp.concatenate([sub(kp_ref), sub(kc_ref)], axis=0)
                vv = jnp.concatenate([sub(vp_ref), sub(vc_ref)], axis=0)
                sc = _dot_nt(sub(q_ref), kk) * DIL_DH ** -0.5 - slope * distd
                sc = jnp.where(valid, sc, -1e30)
                mx = jnp.max(sc, axis=-1, keepdims=True)
                p = jnp.where(valid, jnp.exp(sc - mx), 0.0)
                den = jnp.sum(p, axis=-1, keepdims=True)
                _set_sub_rows(o_ref, r, dil, cols, _dot(p, vv) / den)
                _set_sub_rows(lse_ref, r, dil, cols, jnp.broadcast_to(mx + jnp.log(den), (ATT_BLOCK, LANES)))

    return _pcall(
        body, name=f"attn_fwd_g{group}", grid=(DIL_HEADS // hp, nb),
        in_specs=[qkv(0), qkv(-1), qkv(0), qkv(-1), qkv(0)], out_specs=[per_head(0)] * 2,
        out_shape=[jax.ShapeDtypeStruct((s, DIL_W), F32)] * 2,
        compiler_params=_params("parallel", "parallel"),
    )(qb, kb, kb, vb, vb)


def _attn_bwd(qb, kb, vb, d_o, lse, delta, group):
    s = qb.shape[0]
    dil, nb, hp = _attn_plan(s, group)
    qkv, per_head = _attn_specs(group, dil, nb, hp)
    scale = DIL_DH ** -0.5

    def body(qc_ref, qn_ref, kp_ref, kc_ref, vp_ref, vc_ref, doc_ref, don_ref, lc_ref, ln_ref, dc_ref, dn_ref,
             dq_ref, dk_ref, dv_ref, dq_acc, dk_acc, dv_acc):
        n = pl.program_id(1)
        distd, valid = _window_bias(dil, n)
        bk = lax.broadcasted_iota(jnp.int32, (ATT_BLOCK, 2 * ATT_BLOCK), 0)
        aq = lax.broadcasted_iota(jnp.int32, (ATT_BLOCK, 2 * ATT_BLOCK), 1)
        dist_t = aq - bk
        valid_t = (dist_t >= 0) & (dist_t <= ATT_BLOCK) & ((aq < ATT_BLOCK) | (n < nb - 1))
        distd_t = (dist_t * dil).astype(F32)
        for hh in range(hp):
            cols = _head_cols(hh)
            slope = _step_slope(group, hp, hh)
            for r in range(dil):
                sub = lambda ref: _sub_rows(ref, r, dil, cols)
                qc, kc, vc = sub(qc_ref).astype(MXU), sub(kc_ref).astype(MXU), sub(vc_ref).astype(MXU)
                doc, lc, dc = sub(doc_ref), sub(lc_ref), sub(dc_ref)
                kk = jnp.concatenate([sub(kp_ref).astype(MXU), kc], axis=0)
                vv = jnp.concatenate([sub(vp_ref).astype(MXU), vc], axis=0)
                sc = _dot_nt(qc, kk) * scale - slope * distd
                p = jnp.where(valid, jnp.exp(jnp.minimum(sc - jnp.concatenate([lc] * 2, axis=1), 0.0)), 0.0)
                dsc = p * (_dot_nt(doc, vv) - jnp.concatenate([dc] * 2, axis=1))
                _set_sub_rows(dq_acc, r, dil, cols, _dot(dsc, kk) * scale)
                qq = jnp.concatenate([qc, sub(qn_ref).astype(MXU)], axis=0)
                doo = jnp.concatenate([doc.astype(MXU), sub(don_ref).astype(MXU)], axis=0)
                lse_t = jnp.concatenate([lc, sub(ln_ref)], axis=0).T
                del_t = jnp.concatenate([dc, sub(dn_ref)], axis=0).T
                sc_t = _dot_nt(kc, qq) * scale - slope * distd_t
                p_t = jnp.where(valid_t, jnp.exp(jnp.minimum(sc_t - lse_t, 0.0)), 0.0)
                ds_t = p_t * (_dot_nt(vc, doo) - del_t)
                _set_sub_rows(dk_acc, r, dil, cols, _dot(ds_t, qq) * scale)
                _set_sub_rows(dv_acc, r, dil, cols, _dot(p_t, doo))
        dq_ref[...] = dq_acc[...].astype(dq_ref.dtype)
        dk_ref[...] = dk_acc[...].astype(dk_ref.dtype)
        dv_ref[...] = dv_acc[...].astype(dv_ref.dtype)

    return _pcall(
        body, name=f"attn_bwd_g{group}", grid=(DIL_HEADS // hp, nb),
        in_specs=[qkv(0), qkv(1), qkv(-1), qkv(0), qkv(-1), qkv(0)] + [per_head(0), per_head(1)] * 3,
        out_specs=[per_head(0)] * 3,
        out_shape=[jax.ShapeDtypeStruct((s, DIL_W), MXU)] * 3,
        scratch_shapes=[pltpu.VMEM((ATT_BLOCK * dil, hp * LANES), F32)] * 3,
        compiler_params=_params("parallel", "parallel"),
    )(qb, qb, kb, kb, vb, vb, d_o, d_o, lse, lse, delta, delta)


def _my_place():
    mx, my, mc = lax.axis_index("x"), lax.axis_index("y"), lax.axis_index("c")
    return mx, my, mc, 4 * mx + 2 * my + mc


def _peer(mx, my, mc, k):
    px = 1 - mx if k & 4 else mx
    py = 1 - my if k & 2 else my
    pc = 1 - mc if k & 1 else mc
    return (px, py, pc), 4 * px + 2 * py + pc


def _all_gather(xs, name):
    n = len(xs)

    def body(*refs):
        x_refs, o_refs = refs[:n], refs[n:2 * n]
        send_sems, recv_sems, local_sems = refs[2 * n:]
        mx, my, mc, me = _my_place()
        local = [pltpu.make_async_copy(x_refs[a], o_refs[a].at[me], local_sems.at[a]) for a in range(n)]
        for cp in local:
            cp.start()
        for k in range(1, N_DEV):
            peer, _ = _peer(mx, my, mc, k)
            for a in range(n):
                pltpu.make_async_remote_copy(
                    src_ref=x_refs[a], dst_ref=o_refs[a].at[me], send_sem=send_sems.at[a, k - 1],
                    recv_sem=recv_sems.at[a, k - 1], device_id=peer, device_id_type=MESH).start()
        for k in range(1, N_DEV):
            peer, peer_id = _peer(mx, my, mc, k)
            for a in range(n):
                pltpu.make_async_remote_copy(
                    src_ref=x_refs[a], dst_ref=o_refs[a].at[peer_id], send_sem=send_sems.at[a, k - 1],
                    recv_sem=recv_sems.at[a, k - 1], device_id=peer, device_id_type=MESH).wait()
        for cp in local:
            cp.wait()

    any_spec = pl.BlockSpec(memory_space=pl.ANY)
    return _pcall(
        body, name=name,
        in_specs=[any_spec] * n, out_specs=[any_spec] * n,
        out_shape=[jax.ShapeDtypeStruct((N_DEV,) + x.shape, x.dtype) for x in xs],
        scratch_shapes=[pltpu.SemaphoreType.DMA((n, N_DEV - 1)), pltpu.SemaphoreType.DMA((n, N_DEV - 1)),
                        pltpu.SemaphoreType.DMA((n,))],
    )(*xs)


def _reduce_scatter_parts(gs, name):
    n = len(gs)

    def body(*refs):
        g_refs, o_refs = refs[:n], refs[n:2 * n]
        send_sems, recv_sems, local_sems = refs[2 * n:]
        mx, my, mc, me = _my_place()
        local = [pltpu.make_async_copy(g_refs[a].at[me], o_refs[a].at[me], local_sems.at[a]) for a in range(n)]
        for cp in local:
            cp.start()
        for k in range(1, N_DEV):
            peer, peer_id = _peer(mx, my, mc, k)
            for a in range(n):
                pltpu.make_async_remote_copy(
                    src_ref=g_refs[a].at[peer_id], dst_ref=o_refs[a].at[me], send_sem=send_sems.at[a, k - 1],
                    recv_sem=recv_sems.at[a, k - 1], device_id=peer, device_id_type=MESH).start()
        for k in range(1, N_DEV):
            peer, peer_id = _peer(mx, my, mc, k)
            for a in range(n):
                pltpu.make_async_remote_copy(
                    src_ref=g_refs[a].at[peer_id], dst_ref=o_refs[a].at[peer_id], send_sem=send_sems.at[a, k - 1],
                    recv_sem=recv_sems.at[a, k - 1], device_id=peer, device_id_type=MESH).wait()
        for cp in local:
            cp.wait()

    any_spec = pl.BlockSpec(memory_space=pl.ANY)
    return _pcall(
        body, name=name,
        in_specs=[any_spec] * n, out_specs=[any_spec] * n,
        out_shape=[jax.ShapeDtypeStruct(g.shape, g.dtype) for g in gs],
        scratch_shapes=[pltpu.SemaphoreType.DMA((n, N_DEV - 1)), pltpu.SemaphoreType.DMA((n, N_DEV - 1)),
                        pltpu.SemaphoreType.DMA((n,))],
    )(*gs)


def _adamw(parts, w, m, v, name):
    r, c = w.shape
    tr = 128 if r % 128 == 0 and r > 128 else r
    bc1 = 1.0 - ADAM_B1 ** ADAM_STEP
    bc2 = 1.0 - ADAM_B2 ** ADAM_STEP

    def body(p_ref, w_ref, m_ref, v_ref, g_ref, d_ref, nm_ref, nv_ref):
        g = p_ref[0].astype(F32)
        for j in range(1, N_DEV):
            g = g + p_ref[j].astype(F32)
        nm = ADAM_B1 * m_ref[...] + (1.0 - ADAM_B1) * g
        nv = ADAM_B2 * v_ref[...] + (1.0 - ADAM_B2) * (g * g)
        g_ref[...] = g
        nm_ref[...] = nm
        nv_ref[...] = nv
        d_ref[...] = -ADAM_LR * ((nm / bc1) / (jnp.sqrt(nv / bc2) + ADAM_EPS) + ADAM_WD * w_ref[...])

    blk = pl.BlockSpec((tr, c), lambda i: (i, 0))
    return _pcall(
        body, name=name, grid=(r // tr,),
        in_specs=[pl.BlockSpec((N_DEV, tr, c), lambda i: (0, i, 0)), blk, blk, blk],
        out_specs=[blk] * 4, out_shape=[jax.ShapeDtypeStruct((r, c), F32)] * 4,
        compiler_params=_params("parallel"),
    )(parts, w, m, v)


def _local_step(x, target, norm_w, w_segs, conv_w, a_log, dt_bias, dn_norm_w, w_o_dn, w_o_dil, w_out, final_norm_w):
    s = x.shape[0]
    w_qkv, w_za, w_ba, w_qb, w_kb, w_vb, w_zb, w_ga, w_gb = w_segs
    conv_w8 = jnp.concatenate([conv_w, jnp.zeros((SUBLANES - conv_w.shape[0], QKV_W), F32)], axis=0)
    pad8 = jnp.zeros((1, DN_HEADS), F32)
    alog_row = jnp.concatenate([pad8, a_log, jnp.zeros((1, LANES - 2 * DN_HEADS), F32)], axis=1)
    dtb_row = jnp.concatenate([pad8, dt_bias, jnp.zeros((1, LANES - 2 * DN_HEADS), F32)], axis=1)
    wf_row = final_norm_w.reshape(1, D_MODEL)

    hb = _rms_in_fwd(x, norm_w)
    qkv_pre, z_a, ba, z_b = _mm_nn(hb, [w_qkv, w_za, w_ba, w_zb], "proj_fwd_a")
    q_b, k_b, v_b, g_a, g_b = _mm_nn(hb, [w_qb, w_kb, w_vb, w_ga, w_gb], "proj_fwd_b")

    qn, kn, vn, bg = _dn_prep_fwd(qkv_pre, ba, conv_w8, alog_row, dtb_row)
    u_d, w_d, qd_d, kd_d, aqk_d, dl_d, t2_d = _delta_prep(qn, kn, vn, bg)
    o_a, vnew_d, st_d = _delta_scan_fwd(u_d, w_d, qd_d, kd_d, aqk_d, dl_d)
    on_b = _dn_post_fwd(o_a, z_a, dn_norm_w)
    y_a = _mm_nn(on_b, [w_o_dn], "out_dn_fwd")[0]

    parts, lses = [], []
    for gi in range(N_DIL):
        o_g, l_g = _attn_fwd(q_b, k_b, v_b, gi)
        parts.append(o_g)
        lses.append(l_g)
    lse, o_joint, ob_b = _attn_merge_fwd(parts, lses, z_b)
    y_b = _mm_nn(ob_b, [w_o_dil], "out_dil_fwd")[0]

    merged_b = _gate_merge_fwd(g_a, g_b, y_a, y_b)
    x2pre = _mm_nn(merged_b, [w_out], "out_fwd")[0]
    loss8, dwf8, dx2, dx2_b = _final_fwd_bwd(x, x2pre, target, wf_row)

    d_merged = _mm_nt_sum([dx2_b], [w_out], "out_bwd")
    g_w_out = _mm_tn(merged_b, dx2_b, "out_wgrad")
    dya_b, dyb_b, dga_b, dgb_b = _gate_merge_bwd(d_merged, g_a, g_b, y_a, y_b)

    d_on = _mm_nt_sum([dya_b], [w_o_dn], "out_dn_bwd")
    g_w_o_dn = _mm_tn(on_b, dya_b, "out_dn_wgrad")
    d_o_a, dza_b, ddnw8 = _dn_post_bwd(d_on, o_a, z_a, dn_norm_w)

    d_ob = _mm_nt_sum([dyb_b], [w_o_dil], "out_dil_bwd")
    g_w_o_dil = _mm_tn(ob_b, dyb_b, "out_dil_wgrad")
    d_o, dzb_b, delta = _attn_merge_bwd(d_ob, o_joint, z_b)
    dqs, dks, dvs = [], [], []
    for gi in range(N_DIL):
        dq_g, dk_g, dv_g = _attn_bwd(q_b, k_b, v_b, d_o, lse, delta, gi)
        dqs.append(dq_g)
        dks.append(dk_g)
        dvs.append(dv_g)

    dvnew_d, dkd_d, ddl_d = _delta_scan_bwd(w_d, qd_d, kd_d, aqk_d, dl_d, vnew_d, st_d, d_o_a)
    dqn, dkn, dvn, dbg = _delta_post_bwd(qn, kn, vn, bg, t2_d, st_d, vnew_d, d_o_a, dvnew_d, dkd_d, ddl_d)
    dc, dba_b, dsmall8 = _dn_prep_bwd(qkv_pre, ba, conv_w8, alog_row, dtb_row, dqn, dkn, dvn, dbg)
    dqkv_b, dconv8 = _conv_bwd(dc, qkv_pre, conv_w8)

    dh_a = _mm_nt_sum([dqkv_b, dza_b, dba_b, dzb_b], [w_qkv, w_za, w_ba, w_zb], "proj_bwd_a")
    per_group = lambda w: [w[:, g * DIL_W:(g + 1) * DIL_W] for g in range(N_DIL)]
    dh_b = _mm_nt_sum(dqs + dks + dvs + [dga_b, dgb_b],
                      per_group(w_qb) + per_group(w_kb) + per_group(w_vb) + [w_ga, w_gb], "proj_bwd_b")
    dsegs = [dqkv_b, dza_b, dba_b] + dqs + dks + dvs + [dzb_b, dga_b, dgb_b]
    g_segs = [_mm_tn(hb, d, f"proj_wgrad_{j}") for j, d in enumerate(dsegs)]
    grad_x, dnw8 = _rms_in_bwd_sum(x, dh_a, dh_b, dx2, norm_w)

    small = dict(norm_w=dnw8[0:1], final_norm_w=dwf8[0:1], dn_norm_w=ddnw8[0:1],
                 a_log=dsmall8[0:1, DN_HEADS:2 * DN_HEADS], dt_bias=dsmall8[1:2, DN_HEADS:2 * DN_HEADS])
    return loss8[0:1, 0:1], grad_x, g_segs, dconv8[0:4], g_w_o_dn, g_w_o_dil, g_w_out, small


def _rms_in_bwd_sum(x, dh_a, dh_b, dx2, norm_w):
    def body(x_ref, da_ref, db_ref, dx2_ref, w_ref, dx_ref, dw_ref):
        xv = x_ref[...]
        r = lax.rsqrt(jnp.mean(xv * xv, axis=-1, keepdims=True) + NORM_EPS)
        dhv = da_ref[...] + db_ref[...]
        dn = dhv * w_ref[...]
        dx_ref[...] = dx2_ref[...] + r * dn - xv * (r * r * r) * jnp.mean(dn * xv, axis=-1, keepdims=True)
        row = jnp.sum(dhv * xv * r, axis=0, keepdims=True)
        _acc_add(dw_ref, jnp.concatenate([row, jnp.zeros((SUBLANES - 1, row.shape[1]), F32)], axis=0))

    return _rows_call(body, "rms_in_bwd", x.shape[0],
                      [(x, "tile"), (dh_a, "tile"), (dh_b, "tile"), (dx2, "tile"), (norm_w, "full")],
                      [(x.shape, F32, "tile"), ((SUBLANES, x.shape[1]), F32, "acc")])


def _split_proj_cols(w_full):
    offs = [0]
    for n in PROJ_SIZES:
        offs.append(offs[-1] + n)
    seg = lambda a, b: w_full[:, offs[a]:offs[b]]
    w_ba = jnp.concatenate([seg(4, 6), jnp.zeros((w_full.shape[0], LANES - 2 * DN_HEADS), w_full.dtype)], axis=1)
    return [seg(0, 3), seg(3, 4), w_ba, seg(6, 7), seg(7, 8), seg(8, 9), seg(9, 10), seg(10, 11), seg(11, 12)]


def _join_proj_cols(g_segs):
    parts = list(g_segs)
    parts[2] = parts[2][:, :2 * DN_HEADS]
    return jnp.concatenate(parts, axis=1)


def _pack_small(norm_w, final_norm_w, dn_norm_w, a_log, dt_bias):
    pad = lambda r: jnp.concatenate([r, jnp.zeros((1, D_MODEL - r.shape[1]), F32)], axis=1)
    rows = [pad(norm_w.reshape(1, -1)), pad(final_norm_w.reshape(1, -1)), pad(dn_norm_w.reshape(1, -1)),
            pad(a_log.reshape(1, -1)), pad(dt_bias.reshape(1, -1)), jnp.zeros((SUBLANES - 5, D_MODEL), F32)]
    return jnp.concatenate(rows, axis=0)


def _unpack_small(p):
    return dict(norm_w=p[0:1], final_norm_w=p[1], dn_norm_w=p[2:3, :DN_DK], a_log=p[3:4, :DN_HEADS],
                dt_bias=p[4:5, :DN_HEADS])


def kernel(x, norm_w, w_in, conv_w, a_log, dt_bias, dn_norm_w, w_o_dn, w_o_dil, w_out, final_norm_w, loss_target, m_norm_w, m_w_in, m_conv_w, m_a_log, m_dt_bias, m_dn_norm_w, m_w_o_dn, m_w_o_dil, m_w_out, m_final_norm_w, v_norm_w, v_w_in, v_conv_w, v_a_log, v_dt_bias, v_dn_norm_w, v_w_o_dn, v_w_o_dil, v_w_out, v_final_norm_w):
    shard_w = w_in.shape[2]
    gathered = _all_gather([w_in[0].astype(MXU), w_o_dn[0].astype(MXU), w_o_dil[0].astype(MXU), w_out[0].astype(MXU),
                            conv_w[0]], "gather_weights")
    w_in_all, w_o_dn_all, w_o_dil_all, w_out_all, conv_all = gathered
    w_in_full = jnp.transpose(w_in_all, (1, 0, 2)).reshape(D_MODEL, N_DEV * shard_w)
    w_o_dn_full = w_o_dn_all.reshape(D_MODEL, D_MODEL)
    w_o_dil_full = jnp.transpose(w_o_dil_all, (1, 0, 2)).reshape(DIL_W, D_MODEL)
    w_out_full = w_out_all.reshape(D_MODEL, D_MODEL)
    conv_full = jnp.transpose(conv_all, (1, 0, 2)).reshape(conv_w.shape[1], QKV_W)

    loss11, grad_x, g_segs, g_conv, g_w_o_dn, g_w_o_dil, g_w_out, small = _local_step(
        x[0], loss_target[0], norm_w, _split_proj_cols(w_in_full), conv_full, a_log, dt_bias, dn_norm_w,
        w_o_dn_full, w_o_dil_full, w_out_full, final_norm_w)

    col_shards = lambda g, n: jnp.transpose(g.reshape(g.shape[0], N_DEV, n), (1, 0, 2))
    row_shards = lambda g: g.reshape(N_DEV, g.shape[0] // N_DEV, g.shape[1])
    sent = [col_shards(_join_proj_cols(g_segs), shard_w).astype(MXU), row_shards(g_w_o_dn).astype(MXU),
            col_shards(g_w_o_dil, w_o_dil.shape[2]).astype(MXU), row_shards(g_w_out).astype(MXU),
            col_shards(g_conv, conv_w.shape[2])]
    p_w_in, p_w_o_dn, p_w_o_dil, p_w_out, p_conv = _reduce_scatter_parts(sent, "scatter_grads")
    p_small = _all_gather([_pack_small(small["norm_w"], small["final_norm_w"], small["dn_norm_w"], small["a_log"],
                                       small["dt_bias"])], "gather_small_grads")[0]

    res = {}
    res["w_in"] = _adamw(p_w_in, w_in[0], m_w_in[0], v_w_in[0], "adamw_w_in")
    res["conv_w"] = _adamw(p_conv, conv_w[0], m_conv_w[0], v_conv_w[0], "adamw_conv_w")
    res["w_o_dn"] = _adamw(p_w_o_dn, w_o_dn[0], m_w_o_dn[0], v_w_o_dn[0], "adamw_w_o_dn")
    res["w_o_dil"] = _adamw(p_w_o_dil, w_o_dil[0], m_w_o_dil[0], v_w_o_dil[0], "adamw_w_o_dil")
    res["w_out"] = _adamw(p_w_out, w_out[0], m_w_out[0], v_w_out[0], "adamw_w_out")
    small_res = _adamw(p_small, _pack_small(norm_w, final_norm_w, dn_norm_w, a_log, dt_bias),
                       _pack_small(m_norm_w, m_final_norm_w, m_dn_norm_w, m_a_log, m_dt_bias),
                       _pack_small(v_norm_w, v_final_norm_w, v_dn_norm_w, v_a_log, v_dt_bias), "adamw_small")
    small_res = [_unpack_small(t) for t in small_res]

    loss = lax.psum(loss11[0, 0], ("x", "y", "c"))
    names = ["norm_w", "w_in", "conv_w", "a_log", "dt_bias", "dn_norm_w", "w_o_dn", "w_o_dil", "w_out", "final_norm_w"]
    outs = [loss, grad_x[None]]
    for kind in range(4):
        for nm in names:
            outs.append(res[nm][kind][None] if nm in res else small_res[kind][nm])
    return tuple(outs)
```

```python
import math

import jax
import jax.numpy as jnp
from jax import lax
from jax.experimental import pallas as pl
from jax.experimental.pallas import tpu as pltpu

F32 = jnp.float32
MXU = jnp.bfloat16
MESH = pl.DeviceIdType.MESH

N_DEV = 8
D_MODEL = 1024
DN_HEADS = 8
DN_DK = 128
DN_CHUNK = 64
N_DIL = 3
DIL_HEADS = 4
DIL_DH = 128
DIL_W = DIL_HEADS * DIL_DH
DIL_GROUPS = ((128, 1), (512, 4), (2048, 16))
ATT_BLOCK = 128
NORM_EPS = 1e-6
QKV_W = 3 * D_MODEL
DILQ_W = N_DIL * DIL_W
PROJ_SIZES = (1024, 1024, 1024, 1024, 8, 8, DILQ_W, DILQ_W, DILQ_W, DIL_W, D_MODEL, D_MODEL)

ADAM_LR = 0.001
ADAM_B1 = 0.9
ADAM_B2 = 0.999
ADAM_EPS = 1e-08
ADAM_WD = 0.01
ADAM_STEP = 10

ROW_TILE = 256
LANES = 128
SUBLANES = 8
VMEM_LIMIT = 48 << 20


def _pcall(body, **kw):
    return pl.pallas_call(body, **kw)


def _params(*sem):
    return pltpu.CompilerParams(dimension_semantics=tuple(sem), vmem_limit_bytes=VMEM_LIMIT)


def _sigmoid(x):
    return 1.0 / (1.0 + jnp.exp(-x))


def _softplus(x):
    return jnp.maximum(x, 0.0) + jnp.log(1.0 + jnp.exp(-jnp.abs(x)))


def _dot(a, b):
    return jnp.dot(a.astype(MXU), b.astype(MXU), preferred_element_type=F32)


def _dot_nt(a, b):
    return lax.dot_general(a.astype(MXU), b.astype(MXU), (((1,), (1,)), ((), ())), preferred_element_type=F32)


def _split3(x):
    hi = x.astype(jnp.bfloat16)
    r1 = x - hi.astype(F32)
    mid = r1.astype(jnp.bfloat16)
    lo = (r1 - mid.astype(F32)).astype(jnp.bfloat16)
    return hi, mid, lo


def _dot01(m01, x):
    m = m01.astype(jnp.bfloat16)
    hi, mid, lo = _split3(x)
    f = lambda p: jnp.dot(m, p, preferred_element_type=F32)
    return f(hi) + (f(mid) + f(lo))


def _rows_call(body, name, n_rows, ins, outs, scratch=(), tm=ROW_TILE):
    steps = n_rows // tm
    per8 = tm // SUBLANES
    last8 = n_rows // SUBLANES - 1
    in_specs = []
    for arr, kind in ins:
        cols = arr.shape[-1]
        if kind == "tile":
            in_specs.append(pl.BlockSpec((tm, cols), lambda i: (i, 0)))
        elif kind == "full":
            in_specs.append(pl.BlockSpec(arr.shape, lambda i, nd=arr.ndim: (0,) * nd))
        elif kind == "prev8":
            in_specs.append(pl.BlockSpec((SUBLANES, cols), lambda i: (jnp.maximum(i * per8 - 1, 0), 0)))
        elif kind == "next8":
            in_specs.append(pl.BlockSpec((SUBLANES, cols), lambda i: (jnp.minimum((i + 1) * per8, last8), 0)))
        else:
            raise ValueError(kind)
    out_specs, out_shape, has_acc = [], [], False
    for shape, dtype, kind in outs:
        out_shape.append(jax.ShapeDtypeStruct(shape, dtype))
        if kind == "tile":
            out_specs.append(pl.BlockSpec((tm, shape[-1]), lambda i: (i, 0)))
        else:
            has_acc = True
            out_specs.append(pl.BlockSpec(shape, lambda i: (0, 0)))
    return _pcall(
        body, name=name, grid=(steps,), in_specs=in_specs, out_specs=out_specs, out_shape=out_shape,
        scratch_shapes=list(scratch),
        compiler_params=_params("arbitrary" if has_acc else "parallel"),
    )(*[a for a, _ in ins])


def _acc_add(ref, value):
    @pl.when(pl.program_id(0) == 0)
    def _():
        ref[...] = jnp.zeros_like(ref)
    ref[...] += value


def _col_chunks(n, width=512):
    return [(c, min(width, n - c)) for c in range(0, n, width)]


def _mm_nn(a, ws, name, out_dtype=F32, tm=ROW_TILE):
    m, k = a.shape
    ns = [w.shape[1] for w in ws]

    def body(a_ref, *refs):
        av = a_ref[...]
        for w_ref, o_ref, n in zip(refs[:len(ws)], refs[len(ws):], ns):
            for c, wd in _col_chunks(n):
                o_ref[:, c:c + wd] = jnp.dot(av, w_ref[:, c:c + wd], preferred_element_type=F32).astype(o_ref.dtype)

    return _pcall(
        body, name=name, grid=(m // tm,),
        in_specs=[pl.BlockSpec((tm, k), lambda i: (i, 0))] + [pl.BlockSpec((k, n), lambda i: (0, 0)) for n in ns],
        out_specs=[pl.BlockSpec((tm, n), lambda i: (i, 0)) for n in ns],
        out_shape=[jax.ShapeDtypeStruct((m, n), out_dtype) for n in ns],
        compiler_params=_params("parallel"),
    )(a, *ws)


def _mm_nt_sum(ds, ws, name, tm=ROW_TILE):
    m = ds[0].shape[0]
    k = ws[0].shape[0]
    ns = [d.shape[1] for d in ds]

    def body(*refs):
        d_refs, w_refs, o_ref = refs[:len(ds)], refs[len(ds):2 * len(ds)], refs[-1]
        first = True
        for d_ref, w_ref, n in zip(d_refs, w_refs, ns):
            for c, wd in _col_chunks(n, 1024):
                part = lax.dot_general(d_ref[:, c:c + wd], w_ref[:, c:c + wd], (((1,), (1,)), ((), ())),
                                       preferred_element_type=F32)
                if first:
                    o_ref[...] = part
                    first = False
                else:
                    o_ref[...] += part

    return _pcall(
        body, name=name, grid=(m // tm,),
        in_specs=[pl.BlockSpec((tm, n), lambda i: (i, 0)) for n in ns] + [pl.BlockSpec((k, n), lambda i: (0, 0)) for n in ns],
        out_specs=pl.BlockSpec((tm, k), lambda i: (i, 0)),
        out_shape=jax.ShapeDtypeStruct((m, k), F32),
        compiler_params=_params("parallel"),
    )(*ds, *ws)


def _mm_tn(a, d, name, tm=512):
    m, k = a.shape
    n = d.shape[1]
    tn = 1024 if n % 1024 == 0 else (512 if n % 512 == 0 else n)

    def body(a_ref, d_ref, o_ref):
        @pl.when(pl.program_id(1) == 0)
        def _():
            o_ref[...] = jnp.zeros_like(o_ref)
        o_ref[...] += lax.dot_general(a_ref[...], d_ref[...], (((0,), (0,)), ((), ())), preferred_element_type=F32)

    return _pcall(
        body, name=name, grid=(n // tn, m // tm),
        in_specs=[pl.BlockSpec((tm, k), lambda i, j: (j, 0)), pl.BlockSpec((tm, tn), lambda i, j: (j, i))],
        out_specs=pl.BlockSpec((k, tn), lambda i, j: (0, i)),
        out_shape=jax.ShapeDtypeStruct((k, n), F32),
        compiler_params=_params("parallel", "arbitrary"),
    )(a, d)


def _rms_in_fwd(x, norm_w):
    def body(x_ref, w_ref, h_ref):
        xv = x_ref[...]
        r = lax.rsqrt(jnp.mean(xv * xv, axis=-1, keepdims=True) + NORM_EPS)
        h_ref[...] = (xv * r * w_ref[...]).astype(h_ref.dtype)

    return _rows_call(body, "rms_in_fwd", x.shape[0], [(x, "tile"), (norm_w, "full")],
                      [(x.shape, MXU, "tile")])[0]


def _conv_taps(ext_ref, cw_ref, cols, tm):
    c = None
    for j in range(4):
        term = cw_ref[3 - j:4 - j, cols] * ext_ref[SUBLANES - j:SUBLANES - j + tm, cols]
        c = term if c is None else c + term
    return c


def _fill_ext(ext_ref, u_ref, halo_ref, first):
    ext_ref[0:SUBLANES, :] = jnp.where(first, 0.0, halo_ref[...])
    ext_ref[SUBLANES:, :] = u_ref[...]


def _dn_prep_fwd(qkv_pre, ba, conv_w8, alog_row, dtb_row):
    s = qkv_pre.shape[0]
    tm = ROW_TILE

    def body(u_ref, halo_ref, cw_ref, ba_ref, al_ref, dtb_ref, q_ref, k_ref, v_ref, bg_ref, ext_ref):
        _fill_ext(ext_ref, u_ref, halo_ref, pl.program_id(0) == 0)
        for h in range(3 * DN_HEADS):
            cols = slice(h * LANES, (h + 1) * LANES)
            c = _conv_taps(ext_ref, cw_ref, cols, tm)
            a = c * _sigmoid(c)
            oc = slice((h % DN_HEADS) * LANES, (h % DN_HEADS + 1) * LANES)
            if h < 2 * DN_HEADS:
                rinv = lax.rsqrt(jnp.sum(a * a, axis=-1, keepdims=True) + NORM_EPS)
                if h < DN_HEADS:
                    q_ref[:, oc] = a * (rinv * DN_DK ** -0.5)
                else:
                    k_ref[:, oc] = a * rinv
            else:
                v_ref[:, oc] = a
        bav = ba_ref[...]
        lane = lax.broadcasted_iota(jnp.int32, bav.shape, 1)
        beta = _sigmoid(bav)
        g = -jnp.exp(al_ref[...]) * _softplus(bav + dtb_ref[...])
        bg_ref[...] = jnp.where(lane < DN_HEADS, beta, jnp.where(lane < 2 * DN_HEADS, g, 0.0))

    return _rows_call(
        body, "dn_prep_fwd", s,
        [(qkv_pre, "tile"), (qkv_pre, "prev8"), (conv_w8, "full"), (ba, "tile"), (alog_row, "full"), (dtb_row, "full")],
        [((s, D_MODEL), F32, "tile")] * 3 + [((s, LANES), F32, "tile")],
        scratch=[pltpu.VMEM((tm + SUBLANES, QKV_W), F32)])


def _dn_prep_bwd(qkv_pre, ba, conv_w8, alog_row, dtb_row, dq, dk, dv, dbg):
    s = qkv_pre.shape[0]
    tm = ROW_TILE

    def body(u_ref, halo_ref, cw_ref, ba_ref, al_ref, dtb_ref, dq_ref, dk_ref, dv_ref, dbg_ref,
             dc_ref, dba_ref, dsmall_ref, ext_ref):
        _fill_ext(ext_ref, u_ref, halo_ref, pl.program_id(0) == 0)
        for h in range(3 * DN_HEADS):
            cols = slice(h * LANES, (h + 1) * LANES)
            oc = slice((h % DN_HEADS) * LANES, (h % DN_HEADS + 1) * LANES)
            c = _conv_taps(ext_ref, cw_ref, cols, tm)
            sg = _sigmoid(c)
            a = c * sg
            if h < 2 * DN_HEADS:
                rinv = lax.rsqrt(jnp.sum(a * a, axis=-1, keepdims=True) + NORM_EPS)
                dy = dq_ref[:, oc] * DN_DK ** -0.5 if h < DN_HEADS else dk_ref[:, oc]
                da = rinv * dy - a * (rinv * rinv * rinv) * jnp.sum(dy * a, axis=-1, keepdims=True)
            else:
                da = dv_ref[:, oc]
            dc_ref[:, cols] = da * (sg * (1.0 + c * (1.0 - sg)))
        bav = ba_ref[...]
        dbgv = dbg_ref[...]
        lane = lax.broadcasted_iota(jnp.int32, bav.shape, 1)
        beta = _sigmoid(bav)
        ea = jnp.exp(al_ref[...])
        z = bav + dtb_ref[...]
        g = -ea * _softplus(z)
        is_b = lane < DN_HEADS
        is_g = jnp.logical_and(lane >= DN_HEADS, lane < 2 * DN_HEADS)
        d_aa = jnp.where(is_g, dbgv * (-ea) * _sigmoid(z), 0.0)
        dba = jnp.where(is_b, dbgv * beta * (1.0 - beta), d_aa)
        dba_ref[...] = dba.astype(dba_ref.dtype)
        r_alog = jnp.sum(jnp.where(is_g, dbgv * g, 0.0), axis=0, keepdims=True)
        r_dtb = jnp.sum(d_aa, axis=0, keepdims=True)
        _acc_add(dsmall_ref, jnp.concatenate([r_alog, r_dtb, jnp.zeros((SUBLANES - 2, LANES), F32)], axis=0))

    return _rows_call(
        body, "dn_prep_bwd", s,
        [(qkv_pre, "tile"), (qkv_pre, "prev8"), (conv_w8, "full"), (ba, "tile"), (alog_row, "full"), (dtb_row, "full"),
         (dq, "tile"), (dk, "tile"), (dv, "tile"), (dbg, "tile")],
        [((s, QKV_W), F32, "tile"), ((s, LANES), MXU, "tile"), ((SUBLANES, LANES), F32, "acc")],
        scratch=[pltpu.VMEM((tm + SUBLANES, QKV_W), F32)])


def _conv_bwd(dc, qkv_pre, conv_w8):
    s = dc.shape[0]
    tm = ROW_TILE
    steps = s // tm

    def body(dc_ref, dnext_ref, u_ref, halo_ref, cw_ref, du_ref, dcw_ref, extd_ref, ext_ref):
        i = pl.program_id(0)
        _fill_ext(ext_ref, u_ref, halo_ref, i == 0)
        extd_ref[0:tm, :] = dc_ref[...]
        extd_ref[tm:, :] = jnp.where(i == steps - 1, 0.0, dnext_ref[...])

        @pl.when(i == 0)
        def _():
            dcw_ref[...] = jnp.zeros_like(dcw_ref)

        for h in range(3 * DN_HEADS):
            cols = slice(h * LANES, (h + 1) * LANES)
            du = None
            for j in range(4):
                term = cw_ref[3 - j:4 - j, cols] * extd_ref[j:j + tm, cols]
                du = term if du is None else du + term
            du_ref[:, cols] = du.astype(du_ref.dtype)
            dcv = dc_ref[:, cols]
            for j in range(4):
                row = jnp.sum(dcv * ext_ref[SUBLANES - j:SUBLANES - j + tm, cols], axis=0, keepdims=True)
                dcw_ref[3 - j:4 - j, cols] += row

    return _rows_call(
        body, "conv_bwd", s,
        [(dc, "tile"), (dc, "next8"), (qkv_pre, "tile"), (qkv_pre, "prev8"), (conv_w8, "full")],
        [((s, QKV_W), MXU, "tile"), ((SUBLANES, QKV_W), F32, "acc")],
        scratch=[pltpu.VMEM((tm + SUBLANES, QKV_W), F32), pltpu.VMEM((tm + SUBLANES, QKV_W), F32)])


def _dn_post_fwd(o, z, dnw_row):
    def body(o_ref, z_ref, w_ref, on_ref):
        for h in range(DN_HEADS):
            cols = slice(h * LANES, (h + 1) * LANES)
            ov = o_ref[:, cols]
            zv = z_ref[:, cols]
            ro = lax.rsqrt(jnp.mean(ov * ov, axis=-1, keepdims=True) + NORM_EPS)
            on_ref[:, cols] = (ov * ro * w_ref[...] * (zv * _sigmoid(zv))).astype(on_ref.dtype)

    return _rows_call(body, "dn_post_fwd", o.shape[0], [(o, "tile"), (z, "tile"), (dnw_row, "full")],
                      [(o.shape, MXU, "tile")])[0]


def _dn_post_bwd(d_on, o, z, dnw_row):
    def body(d_ref, o_ref, z_ref, w_ref, do_ref, dz_ref, dw_ref):
        acc = jnp.zeros((1, LANES), F32)
        for h in range(DN_HEADS):
            cols = slice(h * LANES, (h + 1) * LANES)
            dv, ov, zv = d_ref[:, cols], o_ref[:, cols], z_ref[:, cols]
            sg = _sigmoid(zv)
            sz = zv * sg
            ro = lax.rsqrt(jnp.mean(ov * ov, axis=-1, keepdims=True) + NORM_EPS)
            nv = ov * ro
            dn = dv * w_ref[...] * sz
            acc = acc + jnp.sum(dv * nv * sz, axis=0, keepdims=True)
            dz_ref[:, cols] = (dv * nv * w_ref[...] * (sg * (1.0 + zv * (1.0 - sg)))).astype(dz_ref.dtype)
            do_ref[:, cols] = ro * dn - ov * (ro * ro * ro) * jnp.mean(dn * ov, axis=-1, keepdims=True)
        _acc_add(dw_ref, jnp.concatenate([acc, jnp.zeros((SUBLANES - 1, LANES), F32)], axis=0))

    return _rows_call(body, "dn_post_bwd", o.shape[0],
                      [(d_on, "tile"), (o, "tile"), (z, "tile"), (dnw_row, "full")],
                      [(o.shape, F32, "tile"), (o.shape, MXU, "tile"), ((SUBLANES, LANES), F32, "acc")])


def _attn_merge_fwd(parts, lses, zb):
    def body(o0, o1, o2, l0, l1, l2, z_ref, lse_ref, o_ref, g_ref):
        a, b, c = l0[...], l1[...], l2[...]
        m = jnp.maximum(a, jnp.maximum(b, c))
        ea, eb, ec = jnp.exp(a - m), jnp.exp(b - m), jnp.exp(c - m)
        den = ea + eb + ec
        out = (ea * o0[...] + eb * o1[...] + ec * o2[...]) / den
        lse_ref[...] = m + jnp.log(den)
        o_ref[...] = out
        zv = z_ref[...]
        g_ref[...] = (out * (zv * _sigmoid(zv))).astype(g_ref.dtype)

    s = zb.shape[0]
    return _rows_call(body, "attn_merge_fwd", s, [(p, "tile") for p in parts] + [(l, "tile") for l in lses] + [(zb, "tile")],
                      [((s, DIL_W), F32, "tile"), ((s, DIL_W), F32, "tile"), ((s, DIL_W), MXU, "tile")])


def _attn_merge_bwd(d_gated, o_joint, zb):
    def body(d_ref, o_ref, z_ref, do_ref, dz_ref, dl_ref):
        zv = z_ref[...]
        sg = _sigmoid(zv)
        dv = d_ref[...]
        ov = o_ref[...]
        do = dv * (zv * sg)
        do_ref[...] = do
        dz_ref[...] = (dv * ov * (sg * (1.0 + zv * (1.0 - sg)))).astype(dz_ref.dtype)
        for h in range(DIL_HEADS):
            cols = slice(h * LANES, (h + 1) * LANES)
            dl_ref[:, cols] = jnp.broadcast_to(jnp.sum(do[:, cols] * ov[:, cols], axis=-1, keepdims=True),
                                               (do.shape[0], LANES))

    s = zb.shape[0]
    return _rows_call(body, "attn_merge_bwd", s, [(d_gated, "tile"), (o_joint, "tile"), (zb, "tile")],
                      [((s, DIL_W), F32, "tile"), ((s, DIL_W), MXU, "tile"), ((s, DIL_W), F32, "tile")])


def _gate_merge_fwd(ga, gb, ya, yb):
    def body(ga_ref, gb_ref, ya_ref, yb_ref, m_ref):
        m_ref[...] = (_sigmoid(ga_ref[...]) * ya_ref[...] + _sigmoid(gb_ref[...]) * yb_ref[...]).astype(m_ref.dtype)

    return _rows_call(body, "gate_merge_fwd", ga.shape[0], [(ga, "tile"), (gb, "tile"), (ya, "tile"), (yb, "tile")],
                      [(ga.shape, MXU, "tile")])[0]


def _gate_merge_bwd(dm, ga, gb, ya, yb):
    def body(dm_ref, ga_ref, gb_ref, ya_ref, yb_ref, dya_ref, dyb_ref, dga_ref, dgb_ref):
        dmv = dm_ref[...]
        sa, sb = _sigmoid(ga_ref[...]), _sigmoid(gb_ref[...])
        dya_ref[...] = (dmv * sa).astype(dya_ref.dtype)
        dyb_ref[...] = (dmv * sb).astype(dyb_ref.dtype)
        dga_ref[...] = (dmv * ya_ref[...] * sa * (1.0 - sa)).astype(dga_ref.dtype)
        dgb_ref[...] = (dmv * yb_ref[...] * sb * (1.0 - sb)).astype(dgb_ref.dtype)

    return _rows_call(body, "gate_merge_bwd", ga.shape[0],
                      [(dm, "tile"), (ga, "tile"), (gb, "tile"), (ya, "tile"), (yb, "tile")],
                      [(ga.shape, MXU, "tile")] * 4)


def _final_fwd_bwd(x, x2pre, target, wf_row):
    s, dm = x.shape

    def body(x_ref, p_ref, t_ref, w_ref, loss_ref, dw_ref, dx_ref, dxb_ref):
        x2 = x_ref[...] + p_ref[...]
        r = lax.rsqrt(jnp.mean(x2 * x2, axis=-1, keepdims=True) + NORM_EPS)
        w = w_ref[...]
        err = x2 * r * w - t_ref[...]
        tile_loss = 0.5 * jnp.sum(jnp.mean(err * err, axis=-1, keepdims=True), axis=0, keepdims=True)
        _acc_add(loss_ref, jnp.broadcast_to(tile_loss, (SUBLANES, LANES)))
        dy = err * (1.0 / dm)
        row = jnp.sum(dy * x2 * r, axis=0, keepdims=True)
        _acc_add(dw_ref, jnp.concatenate([row, jnp.zeros((SUBLANES - 1, dm), F32)], axis=0))
        dn = dy * w
        dx2 = r * dn - x2 * (r * r * r) * jnp.mean(dn * x2, axis=-1, keepdims=True)
        dx_ref[...] = dx2
        dxb_ref[...] = dx2.astype(dxb_ref.dtype)

    return _rows_call(body, "final_fwd_bwd", s, [(x, "tile"), (x2pre, "tile"), (target, "tile"), (wf_row, "full")],
                      [((SUBLANES, LANES), F32, "acc"), ((SUBLANES, dm), F32, "acc"), ((s, dm), F32, "tile"),
                       ((s, dm), MXU, "tile")])


def _lane_pick(x, idx):
    lane = lax.broadcasted_iota(jnp.int32, x.shape, 1)
    return jnp.sum(jnp.where(lane == idx, x, 0.0), axis=-1, keepdims=True)


PAIR = 2 * DN_CHUNK
SCAN_CHUNKS = 4


def _bmm(a, b):
    return lax.dot_general(a.astype(MXU), b.astype(MXU), (((2,), (1,)), ((0,), (0,))), preferred_element_type=F32)


def _bmm_nt(a, b):
    return lax.dot_general(a.astype(MXU), b.astype(MXU), (((2,), (2,)), ((0,), (0,))), preferred_element_type=F32)


def _bmm_tn(a, b):
    return lax.dot_general(a.astype(MXU), b.astype(MXU), (((1,), (1,)), ((0,), (0,))), preferred_element_type=F32)


def _bmm3(a, b):
    ah = a.astype(jnp.bfloat16)
    al = (a - ah.astype(F32)).astype(jnp.bfloat16)
    bh = b.astype(jnp.bfloat16)
    bl = (b - bh.astype(F32)).astype(jnp.bfloat16)
    f = lambda p, q: lax.dot_general(p, q, (((2,), (1,)), ((0,), (0,))), preferred_element_type=F32)
    return f(ah, bh) + (f(ah, bl) + f(al, bh))


def _pair_masks():
    row = lax.broadcasted_iota(jnp.int32, (PAIR, PAIR), 0)
    col = lax.broadcasted_iota(jnp.int32, (PAIR, PAIR), 1)
    same = (row >= DN_CHUNK) == (col >= DN_CHUNK)
    return dict(causal=same & (row >= col), strict=same & (row > col), upper=same & (row <= col), eye=row == col,
                first=row < DN_CHUNK, row=row, lane=col)


def _pair_decay(bgv, masks):
    gc_all = _dot01(masks["causal"].astype(F32), bgv)
    out = []
    for h in range(DN_HEADS):
        beta = _lane_pick(bgv, h)
        gcb = jnp.broadcast_to(_lane_pick(gc_all, DN_HEADS + h), (PAIR, PAIR))
        gam = jnp.where(masks["causal"], jnp.exp(jnp.minimum(gcb - gcb.T, 0.0)), 0.0)
        gl = jnp.where(masks["first"], gcb[DN_CHUNK - 1:DN_CHUNK, :], gcb[PAIR - 1:PAIR, :])
        out.append((beta, gcb, gam, gl))
    return out


def _pair_inverse(a_strict, eye):
    n = -a_strict
    t = eye.astype(F32)[None] + n
    p = n
    for _ in range(int(math.log2(DN_CHUNK)) - 1):
        p = _bmm3(p, p)
        t = t + _bmm3(t, p)
    return t


def _head_cols(h):
    return slice(h * LANES, (h + 1) * LANES)


def _delta_prep(q, k, v, bg):
    s = q.shape[0]
    c = DN_CHUNK
    n_chunks = s // c

    def body(q_ref, k_ref, v_ref, bg_ref, u_ref, w_ref, qd_ref, kd_ref, aqk_ref, dl_ref, t2_ref):
        masks = _pair_masks()
        dec = _pair_decay(bg_ref[...], masks)
        kbs, ks, gams, vbs, kbes, qs, qds, kds, dls = ([] for _ in range(9))
        for h in range(DN_HEADS):
            beta, gcb, gam, gl = dec[h]
            qh, kh, vh = q_ref[:, _head_cols(h)], k_ref[:, _head_cols(h)], v_ref[:, _head_cols(h)]
            eg = jnp.exp(gcb)
            kb = kh * beta
            kbs.append(kb); ks.append(kh); gams.append(gam); vbs.append(vh * beta); kbes.append(kb * eg)
            qs.append(qh); qds.append(qh * eg); kds.append(kh * jnp.exp(gl - gcb)); dls.append(jnp.exp(gl))
        st = lambda xs: jnp.stack(xs, axis=0)
        kmat, gam = st(ks), st(gams)
        a = jnp.where(masks["strict"][None], _bmm_nt(st(kbs), kmat) * gam, 0.0)
        t = _pair_inverse(a, masks["eye"])
        u = _bmm(t, st(vbs))
        w = _bmm(t, st(kbes))
        aqk = _bmm_nt(st(qs), kmat) * gam
        t2_ref[0] = t.astype(t2_ref.dtype)
        for half in range(2):
            rows = slice(half * c, (half + 1) * c)
            u_ref[half] = u[:, rows, :]
            w_ref[half] = w[:, rows, :].astype(w_ref.dtype)
            qd_ref[half] = st(qds)[:, rows, :].astype(qd_ref.dtype)
            kd_ref[half] = st(kds)[:, rows, :].astype(kd_ref.dtype)
            aqk_ref[half] = aqk[:, rows, rows].astype(aqk_ref.dtype)
            dl_ref[half] = st(dls)[:, half * c:half * c + SUBLANES, :]

    row_spec = lambda w_: pl.BlockSpec((PAIR, w_), lambda i: (i, 0))
    hm = lambda a_, b_: pl.BlockSpec((2, DN_HEADS, a_, b_), lambda i: (i, 0, 0, 0))
    hm_shape = lambda a_, b_, dt: jax.ShapeDtypeStruct((n_chunks, DN_HEADS, a_, b_), dt)
    return _pcall(
        body, name="delta_prep", grid=(n_chunks // 2,),
        in_specs=[row_spec(D_MODEL)] * 3 + [row_spec(LANES)],
        out_specs=[hm(c, LANES)] * 4 + [hm(c, c), hm(SUBLANES, LANES),
                   pl.BlockSpec((1, DN_HEADS, PAIR, PAIR), lambda i: (i, 0, 0, 0))],
        out_shape=[hm_shape(c, LANES, F32), hm_shape(c, LANES, MXU), hm_shape(c, LANES, MXU), hm_shape(c, LANES, MXU),
                   hm_shape(c, c, MXU), hm_shape(SUBLANES, LANES, F32),
                   jax.ShapeDtypeStruct((n_chunks // 2, DN_HEADS, PAIR, PAIR), MXU)],
        compiler_params=_params("parallel"),
    )(q, k, v, bg)


def _delta_scan_fwd(u, w, qd, kd, aqk, dl):
    n_chunks = u.shape[0]
    c = DN_CHUNK
    g_n = SCAN_CHUNKS

    def body(u_ref, w_ref, qd_ref, kd_ref, aqk_ref, dl_ref, o_ref, vnew_ref, st_ref, state):
        @pl.when(pl.program_id(0) == 0)
        def _():
            state[...] = jnp.zeros_like(state)

        for g in range(g_n):
            sv = state[...]
            sb = sv.astype(MXU)
            vnew = u_ref[g] - _bmm(w_ref[g], sb)
            o = _bmm(qd_ref[g], sb) + _bmm(aqk_ref[g], vnew)
            state[...] = sv * dl_ref[g][:, 0:1, :] + _bmm_tn(kd_ref[g], vnew)
            vnew_ref[g] = vnew.astype(vnew_ref.dtype)
            st_ref[g] = sb
            for h in range(DN_HEADS):
                o_ref[g * c:(g + 1) * c, _head_cols(h)] = o[h]

    hm = lambda a_, b_: pl.BlockSpec((g_n, DN_HEADS, a_, b_), lambda i: (i, 0, 0, 0))
    return _pcall(
        body, name="delta_scan_fwd", grid=(n_chunks // g_n,),
        in_specs=[hm(c, LANES)] * 4 + [hm(c, c), hm(SUBLANES, LANES)],
        out_specs=[pl.BlockSpec((g_n * c, D_MODEL), lambda i: (i, 0)), hm(c, LANES), hm(DN_DK, DN_DK)],
        out_shape=[jax.ShapeDtypeStruct((n_chunks * c, D_MODEL), F32),
                   jax.ShapeDtypeStruct((n_chunks, DN_HEADS, c, LANES), MXU),
                   jax.ShapeDtypeStruct((n_chunks, DN_HEADS, DN_DK, DN_DK), MXU)],
        scratch_shapes=[pltpu.VMEM((DN_HEADS, DN_DK, DN_DK), F32)],
        compiler_params=_params("arbitrary"),
    )(u, w, qd, kd, aqk, dl)


def _delta_scan_bwd(w, qd, kd, aqk, dl, vnew, st, do):
    n_chunks = w.shape[0]
    c = DN_CHUNK
    g_n = SCAN_CHUNKS
    steps = n_chunks // g_n

    def body(w_ref, qd_ref, kd_ref, aqk_ref, dl_ref, vnew_ref, st_ref, do_ref, dvnew_ref, dkd_ref, ddl_ref, dstate):
        @pl.when(pl.program_id(0) == 0)
        def _():
            dstate[...] = jnp.zeros_like(dstate)

        for g in reversed(range(g_n)):
            ds = dstate[...]
            dsb = ds.astype(MXU)
            doh = jnp.stack([do_ref[g * c:(g + 1) * c, _head_cols(h)] for h in range(DN_HEADS)], axis=0)
            dvnew = _bmm_tn(aqk_ref[g], doh) + _bmm(kd_ref[g], dsb)
            dkd_ref[g] = _bmm_nt(vnew_ref[g], dsb)
            ddl = jnp.sum(jnp.sum(st_ref[g].astype(F32) * ds, axis=2, keepdims=True), axis=1, keepdims=True)
            ddl_ref[g] = jnp.broadcast_to(ddl, (DN_HEADS, SUBLANES, LANES))
            dstate[...] = ds * dl_ref[g][:, 0:1, :] + _bmm_tn(qd_ref[g], doh) - _bmm_tn(w_ref[g], dvnew)
            dvnew_ref[g] = dvnew.astype(dvnew_ref.dtype)

    rev = lambda i: steps - 1 - i
    hm = lambda a_, b_: pl.BlockSpec((g_n, DN_HEADS, a_, b_), lambda i: (rev(i), 0, 0, 0))
    return _pcall(
        body, name="delta_scan_bwd", grid=(steps,),
        in_specs=[hm(c, LANES)] * 3 + [hm(c, c), hm(SUBLANES, LANES), hm(c, LANES), hm(DN_DK, DN_DK),
                  pl.BlockSpec((g_n * c, D_MODEL), lambda i: (rev(i), 0))],
        out_specs=[hm(c, LANES), hm(c, LANES), hm(SUBLANES, LANES)],
        out_shape=[jax.ShapeDtypeStruct((n_chunks, DN_HEADS, c, LANES), MXU),
                   jax.ShapeDtypeStruct((n_chunks, DN_HEADS, c, LANES), F32),
                   jax.ShapeDtypeStruct((n_chunks, DN_HEADS, SUBLANES, LANES), F32)],
        scratch_shapes=[pltpu.VMEM((DN_HEADS, DN_DK, DN_DK), F32)],
        compiler_params=_params("arbitrary"),
    )(w, qd, kd, aqk, dl, vnew, st, do)


def _delta_post_bwd(q, k, v, bg, t2, st, vnew, do, dvnew, dkd, ddl):
    s = q.shape[0]
    c = DN_CHUNK

    def body(q_ref, k_ref, v_ref, bg_ref, t2_ref, st_ref, vnew_ref, do_ref, dvnew_ref, dkd_ref, ddl_ref,
             dq_ref, dk_ref, dv_ref, dbg_ref):
        masks = _pair_masks()
        first = masks["first"][None]
        dec = _pair_decay(bg_ref[...], masks)
        st_ = lambda xs: jnp.stack(xs, axis=0)
        heads = range(DN_HEADS)
        qm_, km_, vm_, dom = (st_([r[:, _head_cols(h)] for h in heads]) for r in (q_ref, k_ref, v_ref, do_ref))
        beta = st_([dec[h][0] for h in heads])
        gcb = st_([dec[h][1] for h in heads])
        gam = st_([dec[h][2] for h in heads])
        gl = st_([dec[h][3] for h in heads])
        pair = lambda ref: jnp.concatenate([ref[0], ref[1]], axis=1)
        vnew2, dvnew2, dkd2 = pair(vnew_ref), pair(dvnew_ref), pair(dkd_ref)
        halves = lambda x: (x[:, :c, :], x[:, c:, :])
        by_state = lambda x: jnp.concatenate([_bmm_nt(xh, st_ref[i]) for i, xh in enumerate(halves(x))], axis=1)
        dqd = by_state(dom)
        dw = -by_state(dvnew2)
        ddl2 = jnp.where(first, ddl_ref[0][:, 0:1, :], ddl_ref[1][:, 0:1, :])

        eg = jnp.exp(gcb)
        egl = jnp.exp(gl - gcb)
        dl = jnp.exp(gl)
        kb = km_ * beta
        kk = _bmm_nt(kb, km_)
        a = jnp.where(masks["strict"][None], kk * gam, 0.0)
        t = t2_ref[0]
        vb = vm_ * beta
        kbe = kb * eg
        u = _bmm(t, vb)
        w = _bmm(t, kbe)
        aqk = _bmm_nt(qm_, km_) * gam
        qd = qm_ * eg
        kd = km_ * egl

        daqk = jnp.where(masks["causal"][None], _bmm_nt(dom, vnew2), 0.0)
        dvb = _bmm_tn(t, dvnew2)
        dkbe = _bmm_tn(t, dw)
        da = jnp.where(masks["strict"][None], -(_bmm_nt(dvb, u) + _bmm_nt(dkbe, w)), 0.0)
        pm = da * gam
        qmm = daqk * gam
        dkb = _bmm(pm, km_) + dkbe * eg
        dkh = _bmm_tn(pm, kb) + _bmm_tn(qmm, qm_) + dkd2 * egl + dkb * beta
        dqh = _bmm(qmm, km_) + dqd * eg
        xm = da * a + daqk * aqk
        ones = jnp.ones((DN_HEADS, PAIR, LANES), F32)
        hi, mid, lo = _split3(xm)
        colsum = _bmm_tn(hi, ones) + (_bmm_tn(mid, ones) + _bmm_tn(lo, ones))
        tmp = jnp.sum(dkd2 * kd, axis=-1, keepdims=True)
        dgc = (jnp.sum(xm, axis=-1, keepdims=True) - colsum + jnp.sum(dkbe * kbe, axis=-1, keepdims=True)
               + jnp.sum(dqd * qd, axis=-1, keepdims=True) - tmp)
        sum0 = jnp.sum(jnp.where(first, tmp, 0.0), axis=1, keepdims=True)
        sum1 = jnp.sum(jnp.where(first, 0.0, tmp), axis=1, keepdims=True)
        dgl = jnp.where(first, sum0, sum1) + ddl2 * dl
        last = (masks["row"] == c - 1) | (masks["row"] == PAIR - 1)
        dgc = dgc + jnp.where(last[None], dgl, 0.0)
        dbeta = jnp.sum(dvb * vm_, axis=-1, keepdims=True) + jnp.sum(dkb * km_, axis=-1, keepdims=True)
        dvh = dvb * beta

        lane = masks["lane"]
        dgc_lanes = jnp.zeros((PAIR, LANES), F32)
        dbg = jnp.zeros((PAIR, LANES), F32)
        for h in heads:
            dq_ref[:, _head_cols(h)] = dqh[h]
            dk_ref[:, _head_cols(h)] = dkh[h]
            dv_ref[:, _head_cols(h)] = dvh[h]
            dgc_lanes = dgc_lanes + jnp.where(lane == DN_HEADS + h, dgc[h], 0.0)
            dbg = dbg + jnp.where(lane == h, dbeta[h], 0.0)
        dbg_ref[...] = dbg + _dot01(masks["upper"].astype(F32), dgc_lanes)

    n_pairs = s // PAIR
    row_spec = lambda w_: pl.BlockSpec((PAIR, w_), lambda i: (i, 0))
    hm = lambda a_, b_: pl.BlockSpec((2, DN_HEADS, a_, b_), lambda i: (i, 0, 0, 0))
    return _pcall(
        body, name="delta_post_bwd", grid=(n_pairs,),
        in_specs=[row_spec(D_MODEL)] * 3 + [row_spec(LANES), pl.BlockSpec((1, DN_HEADS, PAIR, PAIR), lambda i: (i, 0, 0, 0)),
                  hm(DN_DK, DN_DK), hm(c, LANES), row_spec(D_MODEL), hm(c, LANES), hm(c, LANES), hm(SUBLANES, LANES)],
        out_specs=[row_spec(D_MODEL)] * 3 + [row_spec(LANES)],
        out_shape=[jax.ShapeDtypeStruct((s, D_MODEL), F32)] * 3 + [jax.ShapeDtypeStruct((s, LANES), F32)],
        compiler_params=_params("parallel"),
    )(q, k, v, bg, t2, st, vnew, do, dvnew, dkd, ddl)


def _alibi_slope(group, head):
    n = N_DIL * DIL_HEADS
    return float(2.0 ** (-8.0 * (group * DIL_HEADS + head + 1) / n))


def _attn_plan(s, group):
    window, dil = DIL_GROUPS[group]
    assert window // dil == ATT_BLOCK
    assert (s // dil) % ATT_BLOCK == 0, "sub-sequence length must be a whole number of attention blocks"
    return dil, s // dil // ATT_BLOCK, (DIL_HEADS if dil == 1 else 1)


def _attn_specs(group, dil, nb, hp):
    rows = ATT_BLOCK * dil

    def spec(col0, shift):
        if shift < 0:
            f = lambda hb, n: (jnp.maximum(n - 1, 0), col0 + hb)
        elif shift > 0:
            f = lambda hb, n: (jnp.minimum(n + 1, nb - 1), col0 + hb)
        else:
            f = lambda hb, n: (n, col0 + hb)
        return pl.BlockSpec((rows, hp * LANES), f)

    return (lambda shift: spec(group * (DIL_HEADS // hp), shift)), (lambda shift: spec(0, shift))


def _sub_rows(ref, r, dil, cols):
    return ref[:, cols] if dil == 1 else ref[pl.ds(r, ATT_BLOCK, stride=dil), cols]


def _set_sub_rows(ref, r, dil, cols, value):
    if dil == 1:
        ref[:, cols] = value
    else:
        ref[pl.ds(r, ATT_BLOCK, stride=dil), cols] = value


def _step_slope(group, hp, hh):
    if hp == DIL_HEADS:
        return _alibi_slope(group, hh)
    hb = pl.program_id(0)
    slope = _alibi_slope(group, DIL_HEADS - 1)
    for h in reversed(range(DIL_HEADS - 1)):
        slope = jnp.where(hb == h, _alibi_slope(group, h), slope)
    return slope


def _window_bias(dil, n):
    a = lax.broadcasted_iota(jnp.int32, (ATT_BLOCK, 2 * ATT_BLOCK), 0)
    b = lax.broadcasted_iota(jnp.int32, (ATT_BLOCK, 2 * ATT_BLOCK), 1)
    dist = ATT_BLOCK + a - b
    valid = (dist >= 0) & (dist <= ATT_BLOCK) & ((b >= ATT_BLOCK) | (n > 0))
    return (dist * dil).astype(F32), valid


def _attn_fwd(qb, kb, vb, group):
    s = qb.shape[0]
    dil, nb, hp = _attn_plan(s, group)
    qkv, per_head = _attn_specs(group, dil, nb, hp)

    def body(q_ref, kp_ref, kc_ref, vp_ref, vc_ref, o_ref, lse_ref):
        n = pl.program_id(1)
        distd, valid = _window_bias(dil, n)
        for hh in range(hp):
            cols = _head_cols(hh)
            slope = _step_slope(group, hp, hh)
            for r in range(dil):
                sub = lambda ref: _sub_rows(ref, r, dil, cols).astype(MXU)
                kk = jnp.concatenate([sub(kp_ref), sub(kc_ref)], axis=0)
                vv = jnp.concatenate([sub(vp_ref), sub(vc_ref)], axis=0)
                sc = _dot_nt(sub(q_ref), kk) * DIL_DH ** -0.5 - slope * distd
                sc = jnp.where(valid, sc, -1e30)
                mx = jnp.max(sc, axis=-1, keepdims=True)
                p = jnp.where(valid, jnp.exp(sc - mx), 0.0)
                den = jnp.sum(p, axis=-1, keepdims=True)
                _set_sub_rows(o_ref, r, dil, cols, _dot(p, vv) / den)
                _set_sub_rows(lse_ref, r, dil, cols, jnp.broadcast_to(mx + jnp.log(den), (ATT_BLOCK, LANES)))

    return _pcall(
        body, name=f"attn_fwd_g{group}", grid=(DIL_HEADS // hp, nb),
        in_specs=[qkv(0), qkv(-1), qkv(0), qkv(-1), qkv(0)], out_specs=[per_head(0)] * 2,
        out_shape=[jax.ShapeDtypeStruct((s, DIL_W), F32)] * 2,
        compiler_params=_params("parallel", "parallel"),
    )(qb, kb, kb, vb, vb)


def _attn_bwd(qb, kb, vb, d_o, lse, delta, group):
    s = qb.shape[0]
    dil, nb, hp = _attn_plan(s, group)
    qkv, per_head = _attn_specs(group, dil, nb, hp)
    scale = DIL_DH ** -0.5

    def body(qc_ref, qn_ref, kp_ref, kc_ref, vp_ref, vc_ref, doc_ref, don_ref, lc_ref, ln_ref, dc_ref, dn_ref,
             dq_ref, dk_ref, dv_ref, dq_acc, dk_acc, dv_acc):
        n = pl.program_id(1)
        distd, valid = _window_bias(dil, n)
        bk = lax.broadcasted_iota(jnp.int32, (ATT_BLOCK, 2 * ATT_BLOCK), 0)
        aq = lax.broadcasted_iota(jnp.int32, (ATT_BLOCK, 2 * ATT_BLOCK), 1)
        dist_t = aq - bk
        valid_t = (dist_t >= 0) & (dist_t <= ATT_BLOCK) & ((aq < ATT_BLOCK) | (n < nb - 1))
        distd_t = (dist_t * dil).astype(F32)
        for hh in range(hp):
            cols = _head_cols(hh)
            slope = _step_slope(group, hp, hh)
            for r in range(dil):
                sub = lambda ref: _sub_rows(ref, r, dil, cols)
                qc, kc, vc = sub(qc_ref).astype(MXU), sub(kc_ref).astype(MXU), sub(vc_ref).astype(MXU)
                doc, lc, dc = sub(doc_ref), sub(lc_ref), sub(dc_ref)
                kk = jnp.concatenate([sub(kp_ref).astype(MXU), kc], axis=0)
                vv = jnp.concatenate([sub(vp_ref).astype(MXU), vc], axis=0)
                sc = _dot_nt(qc, kk) * scale - slope * distd
                p = jnp.where(valid, jnp.exp(jnp.minimum(sc - jnp.concatenate([lc] * 2, axis=1), 0.0)), 0.0)
                dsc = p * (_dot_nt(doc, vv) - jnp.concatenate([dc] * 2, axis=1))
                _set_sub_rows(dq_acc, r, dil, cols, _dot(dsc, kk) * scale)
                qq = jnp.concatenate([qc, sub(qn_ref).astype(MXU)], axis=0)
                doo = jnp.concatenate([doc.astype(MXU), sub(don_ref).astype(MXU)], axis=0)
                lse_t = jnp.concatenate([lc, sub(ln_ref)], axis=0).T
                del_t = jnp.concatenate([dc, sub(dn_ref)], axis=0).T
                sc_t = _dot_nt(kc, qq) * scale - slope * distd_t
                p_t = jnp.where(valid_t, jnp.exp(jnp.minimum(sc_t - lse_t, 0.0)), 0.0)
                ds_t = p_t * (_dot_nt(vc, doo) - del_t)
                _set_sub_rows(dk_acc, r, dil, cols, _dot(ds_t, qq) * scale)
                _set_sub_rows(dv_acc, r, dil, cols, _dot(p_t, doo))
        dq_ref[...] = dq_acc[...].astype(dq_ref.dtype)
        dk_ref[...] = dk_acc[...].astype(dk_ref.dtype)
        dv_ref[...] = dv_acc[...].astype(dv_ref.dtype)

    return _pcall(
        body, name=f"attn_bwd_g{group}", grid=(DIL_HEADS // hp, nb),
        in_specs=[qkv(0), qkv(1), qkv(-1), qkv(0), qkv(-1), qkv(0)] + [per_head(0), per_head(1)] * 3,
        out_specs=[per_head(0)] * 3,
        out_shape=[jax.ShapeDtypeStruct((s, DIL_W), MXU)] * 3,
        scratch_shapes=[pltpu.VMEM((ATT_BLOCK * dil, hp * LANES), F32)] * 3,
        compiler_params=_params("parallel", "parallel"),
    )(qb, qb, kb, kb, vb, vb, d_o, d_o, lse, lse, delta, delta)


def _my_place():
    mx, my, mc = lax.axis_index("x"), lax.axis_index("y"), lax.axis_index("c")
    return mx, my, mc, 4 * mx + 2 * my + mc


N_CHIPS = 4


def _other_chips(mx, my):
    return [(1 - mx, my), (mx, 1 - my), (1 - mx, 1 - my)]


def _all_gather(xs, name):
    n = len(xs)

    def body(*refs):
        x_refs, o_refs = refs[:n], refs[n:2 * n]
        send_sems, recv_sems, local_sems = refs[2 * n:]
        mx, my, mc, me = _my_place()
        sibling, sibling_id = (mx, my, 1 - mc), 4 * mx + 2 * my + (1 - mc)
        chips = _other_chips(mx, my)

        def copy(a, k, slot, to, src=None):
            dst = o_refs[a].at[slot]
            return pltpu.make_async_remote_copy(
                src_ref=dst if src is None else src, dst_ref=dst, send_sem=send_sems.at[a, k],
                recv_sem=recv_sems.at[a, k], device_id=to, device_id_type=MESH)

        local = [pltpu.make_async_copy(x_refs[a], o_refs[a].at[me], local_sems.at[a]) for a in range(n)]
        for cp in local:
            cp.start()
        sends = []
        for a in range(n):
            sends.append(copy(a, 0, me, sibling, src=x_refs[a]))
            sends += [copy(a, 1 + j, me, (px, py, mc), src=x_refs[a]) for j, (px, py) in enumerate(chips)]
        for cp in sends:
            cp.start()
        for j, (px, py) in enumerate(chips):
            slot = 4 * px + 2 * py + mc
            for a in range(n):
                copy(a, 1 + j, slot, (px, py, mc)).wait_recv()
                passed = copy(a, 4 + j, slot, sibling)
                passed.start()
                sends.append(passed)
        for a in range(n):
            copy(a, 0, sibling_id, sibling).wait_recv()
            for j, (px, py) in enumerate(chips):
                copy(a, 4 + j, 4 * px + 2 * py + (1 - mc), sibling).wait_recv()
        for cp in sends:
            cp.wait_send()
        for cp in local:
            cp.wait()

    any_spec = pl.BlockSpec(memory_space=pl.ANY)
    return _pcall(
        body, name=name,
        in_specs=[any_spec] * n, out_specs=[any_spec] * n,
        out_shape=[jax.ShapeDtypeStruct((N_DEV,) + x.shape, x.dtype) for x in xs],
        scratch_shapes=[pltpu.SemaphoreType.DMA((n, N_DEV - 1)), pltpu.SemaphoreType.DMA((n, N_DEV - 1)),
                        pltpu.SemaphoreType.DMA((n,))],
    )(*xs)


def _pair_exchange(gs, name):
    n = len(gs)

    def body(*refs):
        g_refs, o_refs = refs[:n], refs[n:2 * n]
        send_sems, recv_sems = refs[2 * n:]
        mx, my, mc, _ = _my_place()
        copies = [pltpu.make_async_remote_copy(
            src_ref=g_refs[a].at[1 - mc], dst_ref=o_refs[a], send_sem=send_sems.at[a], recv_sem=recv_sems.at[a],
            device_id=(mx, my, 1 - mc), device_id_type=MESH) for a in range(n)]
        for cp in copies:
            cp.start()
        for cp in copies:
            cp.wait()

    any_spec = pl.BlockSpec(memory_space=pl.ANY)
    return _pcall(
        body, name=name,
        in_specs=[any_spec] * n, out_specs=[any_spec] * n,
        out_shape=[jax.ShapeDtypeStruct(g.shape[1:], g.dtype) for g in gs],
        scratch_shapes=[pltpu.SemaphoreType.DMA((n,)), pltpu.SemaphoreType.DMA((n,))],
    )(*gs)


def _pair_add(g, other, name):
    _, chips, r, c = g.shape
    tr = 128 if r % 128 == 0 and r > 128 else r
    core = lax.axis_index("c").astype(jnp.int32).reshape(1)

    def body(core_ref, g_ref, o_ref, h_ref):
        h_ref[...] = (g_ref[...].astype(F32)[0] + o_ref[...].astype(F32)).astype(h_ref.dtype)

    blk = pl.BlockSpec((1, tr, c), lambda p, i, core_ref: (p, i, 0))
    return _pcall(
        body, name=name,
        grid_spec=pltpu.PrefetchScalarGridSpec(
            num_scalar_prefetch=1, grid=(chips, r // tr),
            in_specs=[pl.BlockSpec((1, 1, tr, c), lambda p, i, core_ref: (core_ref[0], p, i, 0)), blk],
            out_specs=blk),
        out_shape=jax.ShapeDtypeStruct((chips, r, c), g.dtype),
        compiler_params=_params("parallel", "parallel"),
    )(core, g, other)


def _chip_exchange(hs, name):
    n = len(hs)

    def body(*refs):
        h_refs, o_refs = refs[:n], refs[n:2 * n]
        send_sems, recv_sems, local_sems = refs[2 * n:]
        mx, my, mc, _ = _my_place()
        my_chip = 2 * mx + my
        chips = _other_chips(mx, my)
        local = [pltpu.make_async_copy(h_refs[a].at[my_chip], o_refs[a].at[my_chip], local_sems.at[a]) for a in range(n)]
        for cp in local:
            cp.start()
        for j, (px, py) in enumerate(chips):
            for a in range(n):
                pltpu.make_async_remote_copy(
                    src_ref=h_refs[a].at[2 * px + py], dst_ref=o_refs[a].at[my_chip], send_sem=send_sems.at[a, j],
                    recv_sem=recv_sems.at[a, j], device_id=(px, py, mc), device_id_type=MESH).start()
        for j, (px, py) in enumerate(chips):
            for a in range(n):
                pltpu.make_async_remote_copy(
                    src_ref=h_refs[a].at[2 * px + py], dst_ref=o_refs[a].at[2 * px + py], send_sem=send_sems.at[a, j],
                    recv_sem=recv_sems.at[a, j], device_id=(px, py, mc), device_id_type=MESH).wait()
        for cp in local:
            cp.wait()

    any_spec = pl.BlockSpec(memory_space=pl.ANY)
    return _pcall(
        body, name=name,
        in_specs=[any_spec] * n, out_specs=[any_spec] * n,
        out_shape=[jax.ShapeDtypeStruct(h.shape, h.dtype) for h in hs],
        scratch_shapes=[pltpu.SemaphoreType.DMA((n, N_CHIPS - 1)), pltpu.SemaphoreType.DMA((n, N_CHIPS - 1)),
                        pltpu.SemaphoreType.DMA((n,))],
    )(*hs)


def _adamw(parts, w, m, v, name):
    r, c = w.shape
    n_parts = parts.shape[0]
    tr = 128 if r % 128 == 0 and r > 128 else r
    bc1 = 1.0 - ADAM_B1 ** ADAM_STEP
    bc2 = 1.0 - ADAM_B2 ** ADAM_STEP

    def body(p_ref, w_ref, m_ref, v_ref, g_ref, d_ref, nm_ref, nv_ref):
        g = p_ref[0].astype(F32)
        for j in range(1, n_parts):
            g = g + p_ref[j].astype(F32)
        nm = ADAM_B1 * m_ref[...] + (1.0 - ADAM_B1) * g
        nv = ADAM_B2 * v_ref[...] + (1.0 - ADAM_B2) * (g * g)
        g_ref[...] = g
        nm_ref[...] = nm
        nv_ref[...] = nv
        d_ref[...] = -ADAM_LR * ((nm / bc1) / (jnp.sqrt(nv / bc2) + ADAM_EPS) + ADAM_WD * w_ref[...])

    blk = pl.BlockSpec((tr, c), lambda i: (i, 0))
    return _pcall(
        body, name=name, grid=(r // tr,),
        in_specs=[pl.BlockSpec((n_parts, tr, c), lambda i: (0, i, 0)), blk, blk, blk],
        out_specs=[blk] * 4, out_shape=[jax.ShapeDtypeStruct((r, c), F32)] * 4,
        compiler_params=_params("parallel"),
    )(parts, w, m, v)


def _local_step(x, target, norm_w, w_segs, conv_w, a_log, dt_bias, dn_norm_w, w_o_dn, w_o_dil, w_out, final_norm_w):
    s = x.shape[0]
    w_qkv, w_za, w_ba, w_qb, w_kb, w_vb, w_zb, w_ga, w_gb = w_segs
    conv_w8 = jnp.concatenate([conv_w, jnp.zeros((SUBLANES - conv_w.shape[0], QKV_W), F32)], axis=0)
    pad8 = jnp.zeros((1, DN_HEADS), F32)
    alog_row = jnp.concatenate([pad8, a_log, jnp.zeros((1, LANES - 2 * DN_HEADS), F32)], axis=1)
    dtb_row = jnp.concatenate([pad8, dt_bias, jnp.zeros((1, LANES - 2 * DN_HEADS), F32)], axis=1)
    wf_row = final_norm_w.reshape(1, D_MODEL)

    hb = _rms_in_fwd(x, norm_w)
    qkv_pre, z_a, ba, z_b = _mm_nn(hb, [w_qkv, w_za, w_ba, w_zb], "proj_fwd_a")
    q_b, k_b, v_b, g_a, g_b = _mm_nn(hb, [w_qb, w_kb, w_vb, w_ga, w_gb], "proj_fwd_b")

    qn, kn, vn, bg = _dn_prep_fwd(qkv_pre, ba, conv_w8, alog_row, dtb_row)
    u_d, w_d, qd_d, kd_d, aqk_d, dl_d, t2_d = _delta_prep(qn, kn, vn, bg)
    o_a, vnew_d, st_d = _delta_scan_fwd(u_d, w_d, qd_d, kd_d, aqk_d, dl_d)
    on_b = _dn_post_fwd(o_a, z_a, dn_norm_w)
    y_a = _mm_nn(on_b, [w_o_dn], "out_dn_fwd")[0]

    parts, lses = [], []
    for gi in range(N_DIL):
        o_g, l_g = _attn_fwd(q_b, k_b, v_b, gi)
        parts.append(o_g)
        lses.append(l_g)
    lse, o_joint, ob_b = _attn_merge_fwd(parts, lses, z_b)
    y_b = _mm_nn(ob_b, [w_o_dil], "out_dil_fwd")[0]

    merged_b = _gate_merge_fwd(g_a, g_b, y_a, y_b)
    x2pre = _mm_nn(merged_b, [w_out], "out_fwd")[0]
    loss8, dwf8, dx2, dx2_b = _final_fwd_bwd(x, x2pre, target, wf_row)

    d_merged = _mm_nt_sum([dx2_b], [w_out], "out_bwd")
    g_w_out = _mm_tn(merged_b, dx2_b, "out_wgrad")
    dya_b, dyb_b, dga_b, dgb_b = _gate_merge_bwd(d_merged, g_a, g_b, y_a, y_b)

    d_on = _mm_nt_sum([dya_b], [w_o_dn], "out_dn_bwd")
    g_w_o_dn = _mm_tn(on_b, dya_b, "out_dn_wgrad")
    d_o_a, dza_b, ddnw8 = _dn_post_bwd(d_on, o_a, z_a, dn_norm_w)

    d_ob = _mm_nt_sum([dyb_b], [w_o_dil], "out_dil_bwd")
    g_w_o_dil = _mm_tn(ob_b, dyb_b, "out_dil_wgrad")
    d_o, dzb_b, delta = _attn_merge_bwd(d_ob, o_joint, z_b)
    dqs, dks, dvs = [], [], []
    for gi in range(N_DIL):
        dq_g, dk_g, dv_g = _attn_bwd(q_b, k_b, v_b, d_o, lse, delta, gi)
        dqs.append(dq_g)
        dks.append(dk_g)
        dvs.append(dv_g)

    dvnew_d, dkd_d, ddl_d = _delta_scan_bwd(w_d, qd_d, kd_d, aqk_d, dl_d, vnew_d, st_d, d_o_a)
    dqn, dkn, dvn, dbg = _delta_post_bwd(qn, kn, vn, bg, t2_d, st_d, vnew_d, d_o_a, dvnew_d, dkd_d, ddl_d)
    dc, dba_b, dsmall8 = _dn_prep_bwd(qkv_pre, ba, conv_w8, alog_row, dtb_row, dqn, dkn, dvn, dbg)
    dqkv_b, dconv8 = _conv_bwd(dc, qkv_pre, conv_w8)

    dh_a = _mm_nt_sum([dqkv_b, dza_b, dba_b, dzb_b], [w_qkv, w_za, w_ba, w_zb], "proj_bwd_a")
    per_group = lambda w: [w[:, g * DIL_W:(g + 1) * DIL_W] for g in range(N_DIL)]
    dh_b = _mm_nt_sum(dqs + dks + dvs + [dga_b, dgb_b],
                      per_group(w_qb) + per_group(w_kb) + per_group(w_vb) + [w_ga, w_gb], "proj_bwd_b")
    dsegs = [dqkv_b, dza_b, dba_b] + dqs + dks + dvs + [dzb_b, dga_b, dgb_b]
    g_segs = [_mm_tn(hb, d, f"proj_wgrad_{j}") for j, d in enumerate(dsegs)]
    grad_x, dnw8 = _rms_in_bwd_sum(x, dh_a, dh_b, dx2, norm_w)

    small = dict(norm_w=dnw8[0:1], final_norm_w=dwf8[0:1], dn_norm_w=ddnw8[0:1],
                 a_log=dsmall8[0:1, DN_HEADS:2 * DN_HEADS], dt_bias=dsmall8[1:2, DN_HEADS:2 * DN_HEADS])
    return loss8[0:1, 0:1], grad_x, g_segs, dconv8[0:4], g_w_o_dn, g_w_o_dil, g_w_out, small


def _rms_in_bwd_sum(x, dh_a, dh_b, dx2, norm_w):
    def body(x_ref, da_ref, db_ref, dx2_ref, w_ref, dx_ref, dw_ref):
        xv = x_ref[...]
        r = lax.rsqrt(jnp.mean(xv * xv, axis=-1, keepdims=True) + NORM_EPS)
        dhv = da_ref[...] + db_ref[...]
        dn = dhv * w_ref[...]
        dx_ref[...] = dx2_ref[...] + r * dn - xv * (r * r * r) * jnp.mean(dn * xv, axis=-1, keepdims=True)
        row = jnp.sum(dhv * xv * r, axis=0, keepdims=True)
        _acc_add(dw_ref, jnp.concatenate([row, jnp.zeros((SUBLANES - 1, row.shape[1]), F32)], axis=0))

    return _rows_call(body, "rms_in_bwd", x.shape[0],
                      [(x, "tile"), (dh_a, "tile"), (dh_b, "tile"), (dx2, "tile"), (norm_w, "full")],
                      [(x.shape, F32, "tile"), ((SUBLANES, x.shape[1]), F32, "acc")])


def _split_proj_cols(w_full):
    offs = [0]
    for n in PROJ_SIZES:
        offs.append(offs[-1] + n)
    seg = lambda a, b: w_full[:, offs[a]:offs[b]]
    w_ba = jnp.concatenate([seg(4, 6), jnp.zeros((w_full.shape[0], LANES - 2 * DN_HEADS), w_full.dtype)], axis=1)
    return [seg(0, 3), seg(3, 4), w_ba, seg(6, 7), seg(7, 8), seg(8, 9), seg(9, 10), seg(10, 11), seg(11, 12)]


def _join_proj_cols(g_segs):
    parts = list(g_segs)
    parts[2] = parts[2][:, :2 * DN_HEADS]
    return jnp.concatenate(parts, axis=1)


def _pack_small(norm_w, final_norm_w, dn_norm_w, a_log, dt_bias):
    pad = lambda r: jnp.concatenate([r, jnp.zeros((1, D_MODEL - r.shape[1]), F32)], axis=1)
    rows = [pad(norm_w.reshape(1, -1)), pad(final_norm_w.reshape(1, -1)), pad(dn_norm_w.reshape(1, -1)),
            pad(a_log.reshape(1, -1)), pad(dt_bias.reshape(1, -1)), jnp.zeros((SUBLANES - 5, D_MODEL), F32)]
    return jnp.concatenate(rows, axis=0)


def _unpack_small(p):
    return dict(norm_w=p[0:1], final_norm_w=p[1], dn_norm_w=p[2:3, :DN_DK], a_log=p[3:4, :DN_HEADS],
                dt_bias=p[4:5, :DN_HEADS])


def kernel(x, norm_w, w_in, conv_w, a_log, dt_bias, dn_norm_w, w_o_dn, w_o_dil, w_out, final_norm_w, loss_target, m_norm_w, m_w_in, m_conv_w, m_a_log, m_dt_bias, m_dn_norm_w, m_w_o_dn, m_w_o_dil, m_w_out, m_final_norm_w, v_norm_w, v_w_in, v_conv_w, v_a_log, v_dt_bias, v_dn_norm_w, v_w_o_dn, v_w_o_dil, v_w_out, v_final_norm_w):
    shard_w = w_in.shape[2]
    gathered = _all_gather([w_in[0].astype(MXU), w_o_dn[0].astype(MXU), w_o_dil[0].astype(MXU), w_out[0].astype(MXU),
                            conv_w[0]], "gather_weights")
    w_in_all, w_o_dn_all, w_o_dil_all, w_out_all, conv_all = gathered
    w_in_full = jnp.transpose(w_in_all, (1, 0, 2)).reshape(D_MODEL, N_DEV * shard_w)
    w_o_dn_full = w_o_dn_all.reshape(D_MODEL, D_MODEL)
    w_o_dil_full = jnp.transpose(w_o_dil_all, (1, 0, 2)).reshape(DIL_W, D_MODEL)
    w_out_full = w_out_all.reshape(D_MODEL, D_MODEL)
    conv_full = jnp.transpose(conv_all, (1, 0, 2)).reshape(conv_w.shape[1], QKV_W)

    loss11, grad_x, g_segs, g_conv, g_w_o_dn, g_w_o_dil, g_w_out, small = _local_step(
        x[0], loss_target[0], norm_w, _split_proj_cols(w_in_full), conv_full, a_log, dt_bias, dn_norm_w,
        w_o_dn_full, w_o_dil_full, w_out_full, final_norm_w)

    col_shards = lambda g, n: jnp.transpose(g.reshape(g.shape[0], N_DEV, n), (1, 0, 2))
    row_shards = lambda g: g.reshape(N_DEV, g.shape[0] // N_DEV, g.shape[1])
    sent = [col_shards(_join_proj_cols(g_segs), shard_w).astype(MXU), row_shards(g_w_o_dn).astype(MXU),
            col_shards(g_w_o_dil, w_o_dil.shape[2]).astype(MXU), row_shards(g_w_out).astype(MXU),
            col_shards(g_conv, conv_w.shape[2])]
    by_core = lambda g8: jnp.transpose(g8.reshape((N_CHIPS, 2) + g8.shape[1:]), (1, 0, 2, 3))
    sent = [by_core(g8) for g8 in sent]
    from_sibling = _pair_exchange(sent, "scatter_pair")
    summed = [_pair_add(g, o, f"pair_add_{i}") for i, (g, o) in enumerate(zip(sent, from_sibling))]
    p_w_in, p_w_o_dn, p_w_o_dil, p_w_out, p_conv = _chip_exchange(summed, "scatter_chips")
    p_small = _all_gather([_pack_small(small["norm_w"], small["final_norm_w"], small["dn_norm_w"], small["a_log"],
                                       small["dt_bias"])], "gather_small_grads")[0]

    res = {}
    res["w_in"] = _adamw(p_w_in, w_in[0], m_w_in[0], v_w_in[0], "adamw_w_in")
    res["conv_w"] = _adamw(p_conv, conv_w[0], m_conv_w[0], v_conv_w[0], "adamw_conv_w")
    res["w_o_dn"] = _adamw(p_w_o_dn, w_o_dn[0], m_w_o_dn[0], v_w_o_dn[0], "adamw_w_o_dn")
    res["w_o_dil"] = _adamw(p_w_o_dil, w_o_dil[0], m_w_o_dil[0], v_w_o_dil[0], "adamw_w_o_dil")
    res["w_out"] = _adamw(p_w_out, w_out[0], m_w_out[0], v_w_out[0], "adamw_w_out")
    small_res = _adamw(p_small, _pack_small(norm_w, final_norm_w, dn_norm_w, a_log, dt_bias),
                       _pack_small(m_norm_w, m_final_norm_w, m_dn_norm_w, m_a_log, m_dt_bias),
                       _pack_small(v_norm_w, v_final_norm_w, v_dn_norm_w, v_a_log, v_dt_bias), "adamw_small")
    small_res = [_unpack_small(t) for t in small_res]

    loss = lax.psum(loss11[0, 0], ("x", "y", "c"))
    names = ["norm_w", "w_in", "conv_w", "a_log", "dt_bias", "dn_norm_w", "w_o_dn", "w_o_dil", "w_out", "final_norm_w"]
    outs = [loss, grad_x[None]]
    for kind in range(4):
        for nm in names:
            outs.append(res[nm][kind][None] if nm in res else small_res[kind][nm])
    return tuple(outs)
```

```python
import math

import jax
import jax.numpy as jnp
from jax import lax
from jax.experimental import pallas as pl
from jax.experimental.pallas import tpu as pltpu

F32 = jnp.float32
MXU = jnp.bfloat16
MESH = pl.DeviceIdType.MESH

N_DEV = 8
D_MODEL = 1024
DN_HEADS = 8
DN_DK = 128
DN_CHUNK = 64
N_DIL = 3
DIL_HEADS = 4
DIL_DH = 128
DIL_W = DIL_HEADS * DIL_DH
DIL_GROUPS = ((128, 1), (512, 4), (2048, 16))
ATT_BLOCK = 128
NORM_EPS = 1e-6
QKV_W = 3 * D_MODEL
DILQ_W = N_DIL * DIL_W
PROJ_SIZES = (1024, 1024, 1024, 1024, 8, 8, DILQ_W, DILQ_W, DILQ_W, DIL_W, D_MODEL, D_MODEL)

ADAM_LR = 0.001
ADAM_B1 = 0.9
ADAM_B2 = 0.999
ADAM_EPS = 1e-08
ADAM_WD = 0.01
ADAM_STEP = 10

ROW_TILE = 256
LANES = 128
SUBLANES = 8
VMEM_LIMIT = 48 << 20


def _pcall(body, **kw):
    return pl.pallas_call(body, **kw)


def _params(*sem):
    return pltpu.CompilerParams(dimension_semantics=tuple(sem), vmem_limit_bytes=VMEM_LIMIT)


def _sigmoid(x):
    return 1.0 / (1.0 + jnp.exp(-x))


def _softplus(x):
    return jnp.maximum(x, 0.0) + jnp.log(1.0 + jnp.exp(-jnp.abs(x)))


def _dot(a, b):
    return jnp.dot(a.astype(MXU), b.astype(MXU), preferred_element_type=F32)


def _dot_nt(a, b):
    return lax.dot_general(a.astype(MXU), b.astype(MXU), (((1,), (1,)), ((), ())), preferred_element_type=F32)


def _split3(x):
    hi = x.astype(jnp.bfloat16)
    r1 = x - hi.astype(F32)
    mid = r1.astype(jnp.bfloat16)
    lo = (r1 - mid.astype(F32)).astype(jnp.bfloat16)
    return hi, mid, lo


def _dot01(m01, x):
    m = m01.astype(jnp.bfloat16)
    hi, mid, lo = _split3(x)
    f = lambda p: jnp.dot(m, p, preferred_element_type=F32)
    return f(hi) + (f(mid) + f(lo))


def _rows_call(body, name, n_rows, ins, outs, scratch=(), tm=ROW_TILE):
    steps = n_rows // tm
    per8 = tm // SUBLANES
    last8 = n_rows // SUBLANES - 1
    in_specs = []
    for arr, kind in ins:
        cols = arr.shape[-1]
        if kind == "tile":
            in_specs.append(pl.BlockSpec((tm, cols), lambda i: (i, 0)))
        elif kind == "full":
            in_specs.append(pl.BlockSpec(arr.shape, lambda i, nd=arr.ndim: (0,) * nd))
        elif kind == "prev8":
            in_specs.append(pl.BlockSpec((SUBLANES, cols), lambda i: (jnp.maximum(i * per8 - 1, 0), 0)))
        elif kind == "next8":
            in_specs.append(pl.BlockSpec((SUBLANES, cols), lambda i: (jnp.minimum((i + 1) * per8, last8), 0)))
        else:
            raise ValueError(kind)
    out_specs, out_shape, has_acc = [], [], False
    for shape, dtype, kind in outs:
        out_shape.append(jax.ShapeDtypeStruct(shape, dtype))
        if kind == "tile":
            out_specs.append(pl.BlockSpec((tm, shape[-1]), lambda i: (i, 0)))
        else:
            has_acc = True
            out_specs.append(pl.BlockSpec(shape, lambda i: (0, 0)))
    return _pcall(
        body, name=name, grid=(steps,), in_specs=in_specs, out_specs=out_specs, out_shape=out_shape,
        scratch_shapes=list(scratch),
        compiler_params=_params("arbitrary" if has_acc else "parallel"),
    )(*[a for a, _ in ins])


def _acc_add(ref, value):
    @pl.when(pl.program_id(0) == 0)
    def _():
        ref[...] = jnp.zeros_like(ref)
    ref[...] += value


def _col_chunks(n, width=512):
    return [(c, min(width, n - c)) for c in range(0, n, width)]


NT_DIMS = (((1,), (1,)), ((), ()))
TN_DIMS = (((0,), (0,)), ((), ()))


def _mm_out(a, ws, name, w_is_out_by_in=False, out_dtype=F32, tm=ROW_TILE):
    m, k = a.shape
    ns = [w.shape[0] if w_is_out_by_in else w.shape[1] for w in ws]

    def body(a_ref, *refs):
        av = a_ref[...]
        for w_ref, o_ref, n in zip(refs[:len(ws)], refs[len(ws):], ns):
            for c, wd in _col_chunks(n):
                if w_is_out_by_in:
                    part = lax.dot_general(av, w_ref[c:c + wd, :], NT_DIMS, preferred_element_type=F32)
                else:
                    part = jnp.dot(av, w_ref[:, c:c + wd], preferred_element_type=F32)
                o_ref[:, c:c + wd] = part.astype(o_ref.dtype)

    return _pcall(
        body, name=name, grid=(m // tm,),
        in_specs=[pl.BlockSpec((tm, k), lambda i: (i, 0))] + [pl.BlockSpec(w.shape, lambda i: (0, 0)) for w in ws],
        out_specs=[pl.BlockSpec((tm, n), lambda i: (i, 0)) for n in ns],
        out_shape=[jax.ShapeDtypeStruct((m, n), out_dtype) for n in ns],
        compiler_params=_params("parallel"),
    )(a, *ws)


def _mm_in(ds, ws, name, w_is_out_by_in=False, tm=ROW_TILE):
    m = ds[0].shape[0]
    k = ws[0].shape[1] if w_is_out_by_in else ws[0].shape[0]
    ns = [d.shape[1] for d in ds]

    def body(*refs):
        d_refs, w_refs, o_ref = refs[:len(ds)], refs[len(ds):2 * len(ds)], refs[-1]
        first = True
        for d_ref, w_ref, n in zip(d_refs, w_refs, ns):
            for c, wd in _col_chunks(n, 1024):
                if w_is_out_by_in:
                    part = jnp.dot(d_ref[:, c:c + wd], w_ref[c:c + wd, :], preferred_element_type=F32)
                else:
                    part = lax.dot_general(d_ref[:, c:c + wd], w_ref[:, c:c + wd], NT_DIMS, preferred_element_type=F32)
                if first:
                    o_ref[...] = part
                    first = False
                else:
                    o_ref[...] += part

    return _pcall(
        body, name=name, grid=(m // tm,),
        in_specs=[pl.BlockSpec((tm, n), lambda i: (i, 0)) for n in ns] + [pl.BlockSpec(w.shape, lambda i: (0, 0)) for w in ws],
        out_specs=pl.BlockSpec((tm, k), lambda i: (i, 0)),
        out_shape=jax.ShapeDtypeStruct((m, k), F32),
        compiler_params=_params("parallel"),
    )(*ds, *ws)


def _mm_tn(a, d, name, tm=512):
    m, k = a.shape
    n = d.shape[1]
    tile = lambda t: 1024 if t % 1024 == 0 else (512 if t % 512 == 0 else t)
    tk, tn = tile(k), tile(n)

    def body(a_ref, d_ref, o_ref):
        @pl.when(pl.program_id(2) == 0)
        def _():
            o_ref[...] = jnp.zeros_like(o_ref)
        o_ref[...] += lax.dot_general(a_ref[...], d_ref[...], TN_DIMS, preferred_element_type=F32)

    return _pcall(
        body, name=name, grid=(k // tk, n // tn, m // tm),
        in_specs=[pl.BlockSpec((tm, tk), lambda p, i, j: (j, p)), pl.BlockSpec((tm, tn), lambda p, i, j: (j, i))],
        out_specs=pl.BlockSpec((tk, tn), lambda p, i, j: (p, i)),
        out_shape=jax.ShapeDtypeStruct((k, n), F32),
        compiler_params=_params("parallel", "parallel", "arbitrary"),
    )(a, d)


def _rms_in_fwd(x, norm_w):
    def body(x_ref, w_ref, h_ref):
        xv = x_ref[...]
        r = lax.rsqrt(jnp.mean(xv * xv, axis=-1, keepdims=True) + NORM_EPS)
        h_ref[...] = (xv * r * w_ref[...]).astype(h_ref.dtype)

    return _rows_call(body, "rms_in_fwd", x.shape[0], [(x, "tile"), (norm_w, "full")],
                      [(x.shape, MXU, "tile")])[0]


def _conv_taps(ext_ref, cw_ref, cols, tm):
    c = None
    for j in range(4):
        term = cw_ref[3 - j:4 - j, cols] * ext_ref[SUBLANES - j:SUBLANES - j + tm, cols]
        c = term if c is None else c + term
    return c


def _fill_ext(ext_ref, u_ref, halo_ref, first):
    ext_ref[0:SUBLANES, :] = jnp.where(first, 0.0, halo_ref[...])
    ext_ref[SUBLANES:, :] = u_ref[...]


def _dn_prep_fwd(qkv_pre, ba, conv_w8, alog_row, dtb_row):
    s = qkv_pre.shape[0]
    tm = ROW_TILE

    def body(u_ref, halo_ref, cw_ref, ba_ref, al_ref, dtb_ref, q_ref, k_ref, v_ref, bg_ref, ext_ref):
        _fill_ext(ext_ref, u_ref, halo_ref, pl.program_id(0) == 0)
        for h in range(3 * DN_HEADS):
            cols = slice(h * LANES, (h + 1) * LANES)
            c = _conv_taps(ext_ref, cw_ref, cols, tm)
            a = c * _sigmoid(c)
            oc = slice((h % DN_HEADS) * LANES, (h % DN_HEADS + 1) * LANES)
            if h < 2 * DN_HEADS:
                rinv = lax.rsqrt(jnp.sum(a * a, axis=-1, keepdims=True) + NORM_EPS)
                if h < DN_HEADS:
                    q_ref[:, oc] = a * (rinv * DN_DK ** -0.5)
                else:
                    k_ref[:, oc] = a * rinv
            else:
                v_ref[:, oc] = a
        bav = ba_ref[...]
        lane = lax.broadcasted_iota(jnp.int32, bav.shape, 1)
        beta = _sigmoid(bav)
        g = -jnp.exp(al_ref[...]) * _softplus(bav + dtb_ref[...])
        bg_ref[...] = jnp.where(lane < DN_HEADS, beta, jnp.where(lane < 2 * DN_HEADS, g, 0.0))

    return _rows_call(
        body, "dn_prep_fwd", s,
        [(qkv_pre, "tile"), (qkv_pre, "prev8"), (conv_w8, "full"), (ba, "tile"), (alog_row, "full"), (dtb_row, "full")],
        [((s, D_MODEL), F32, "tile")] * 3 + [((s, LANES), F32, "tile")],
        scratch=[pltpu.VMEM((tm + SUBLANES, QKV_W), F32)])


def _dn_prep_bwd(qkv_pre, ba, conv_w8, alog_row, dtb_row, dq, dk, dv, dbg):
    s = qkv_pre.shape[0]
    tm = ROW_TILE

    def body(u_ref, halo_ref, cw_ref, ba_ref, al_ref, dtb_ref, dq_ref, dk_ref, dv_ref, dbg_ref,
             dc_ref, dba_ref, dsmall_ref, ext_ref):
        _fill_ext(ext_ref, u_ref, halo_ref, pl.program_id(0) == 0)
        for h in range(3 * DN_HEADS):
            cols = slice(h * LANES, (h + 1) * LANES)
            oc = slice((h % DN_HEADS) * LANES, (h % DN_HEADS + 1) * LANES)
            c = _conv_taps(ext_ref, cw_ref, cols, tm)
            sg = _sigmoid(c)
            a = c * sg
            if h < 2 * DN_HEADS:
                rinv = lax.rsqrt(jnp.sum(a * a, axis=-1, keepdims=True) + NORM_EPS)
                dy = dq_ref[:, oc] * DN_DK ** -0.5 if h < DN_HEADS else dk_ref[:, oc]
                da = rinv * dy - a * (rinv * rinv * rinv) * jnp.sum(dy * a, axis=-1, keepdims=True)
            else:
                da = dv_ref[:, oc]
            dc_ref[:, cols] = da * (sg * (1.0 + c * (1.0 - sg)))
        bav = ba_ref[...]
        dbgv = dbg_ref[...]
        lane = lax.broadcasted_iota(jnp.int32, bav.shape, 1)
        beta = _sigmoid(bav)
        ea = jnp.exp(al_ref[...])
        z = bav + dtb_ref[...]
        g = -ea * _softplus(z)
        is_b = lane < DN_HEADS
        is_g = jnp.logical_and(lane >= DN_HEADS, lane < 2 * DN_HEADS)
        d_aa = jnp.where(is_g, dbgv * (-ea) * _sigmoid(z), 0.0)
        dba = jnp.where(is_b, dbgv * beta * (1.0 - beta), d_aa)
        dba_ref[...] = dba.astype(dba_ref.dtype)
        r_alog = jnp.sum(jnp.where(is_g, dbgv * g, 0.0), axis=0, keepdims=True)
        r_dtb = jnp.sum(d_aa, axis=0, keepdims=True)
        _acc_add(dsmall_ref, jnp.concatenate([r_alog, r_dtb, jnp.zeros((SUBLANES - 2, LANES), F32)], axis=0))

    return _rows_call(
        body, "dn_prep_bwd", s,
        [(qkv_pre, "tile"), (qkv_pre, "prev8"), (conv_w8, "full"), (ba, "tile"), (alog_row, "full"), (dtb_row, "full"),
         (dq, "tile"), (dk, "tile"), (dv, "tile"), (dbg, "tile")],
        [((s, QKV_W), F32, "tile"), ((s, LANES), MXU, "tile"), ((SUBLANES, LANES), F32, "acc")],
        scratch=[pltpu.VMEM((tm + SUBLANES, QKV_W), F32)])


def _conv_bwd(dc, qkv_pre, conv_w8):
    s = dc.shape[0]
    tm = ROW_TILE
    steps = s // tm

    def body(dc_ref, dnext_ref, u_ref, halo_ref, cw_ref, du_ref, dcw_ref, extd_ref, ext_ref):
        i = pl.program_id(0)
        _fill_ext(ext_ref, u_ref, halo_ref, i == 0)
        extd_ref[0:tm, :] = dc_ref[...]
        extd_ref[tm:, :] = jnp.where(i == steps - 1, 0.0, dnext_ref[...])

        @pl.when(i == 0)
        def _():
            dcw_ref[...] = jnp.zeros_like(dcw_ref)

        for h in range(3 * DN_HEADS):
            cols = slice(h * LANES, (h + 1) * LANES)
            du = None
            for j in range(4):
                term = cw_ref[3 - j:4 - j, cols] * extd_ref[j:j + tm, cols]
                du = term if du is None else du + term
            du_ref[:, cols] = du.astype(du_ref.dtype)
            dcv = dc_ref[:, cols]
            for j in range(4):
                row = jnp.sum(dcv * ext_ref[SUBLANES - j:SUBLANES - j + tm, cols], axis=0, keepdims=True)
                dcw_ref[3 - j:4 - j, cols] += row

    return _rows_call(
        body, "conv_bwd", s,
        [(dc, "tile"), (dc, "next8"), (qkv_pre, "tile"), (qkv_pre, "prev8"), (conv_w8, "full")],
        [((s, QKV_W), MXU, "tile"), ((SUBLANES, QKV_W), F32, "acc")],
        scratch=[pltpu.VMEM((tm + SUBLANES, QKV_W), F32), pltpu.VMEM((tm + SUBLANES, QKV_W), F32)])


def _dn_post_fwd(o, z, dnw_row):
    def body(o_ref, z_ref, w_ref, on_ref):
        for h in range(DN_HEADS):
            cols = slice(h * LANES, (h + 1) * LANES)
            ov = o_ref[:, cols]
            zv = z_ref[:, cols]
            ro = lax.rsqrt(jnp.mean(ov * ov, axis=-1, keepdims=True) + NORM_EPS)
            on_ref[:, cols] = (ov * ro * w_ref[...] * (zv * _sigmoid(zv))).astype(on_ref.dtype)

    return _rows_call(body, "dn_post_fwd", o.shape[0], [(o, "tile"), (z, "tile"), (dnw_row, "full")],
                      [(o.shape, MXU, "tile")])[0]


def _dn_post_bwd(d_on, o, z, dnw_row):
    def body(d_ref, o_ref, z_ref, w_ref, do_ref, dz_ref, dw_ref):
        acc = jnp.zeros((1, LANES), F32)
        for h in range(DN_HEADS):
            cols = slice(h * LANES, (h + 1) * LANES)
            dv, ov, zv = d_ref[:, cols], o_ref[:, cols], z_ref[:, cols]
            sg = _sigmoid(zv)
            sz = zv * sg
            ro = lax.rsqrt(jnp.mean(ov * ov, axis=-1, keepdims=True) + NORM_EPS)
            nv = ov * ro
            dn = dv * w_ref[...] * sz
            acc = acc + jnp.sum(dv * nv * sz, axis=0, keepdims=True)
            dz_ref[:, cols] = (dv * nv * w_ref[...] * (sg * (1.0 + zv * (1.0 - sg)))).astype(dz_ref.dtype)
            do_ref[:, cols] = ro * dn - ov * (ro * ro * ro) * jnp.mean(dn * ov, axis=-1, keepdims=True)
        _acc_add(dw_ref, jnp.concatenate([acc, jnp.zeros((SUBLANES - 1, LANES), F32)], axis=0))

    return _rows_call(body, "dn_post_bwd", o.shape[0],
                      [(d_on, "tile"), (o, "tile"), (z, "tile"), (dnw_row, "full")],
                      [(o.shape, F32, "tile"), (o.shape, MXU, "tile"), ((SUBLANES, LANES), F32, "acc")])


def _attn_merge_fwd(parts, lses, zb):
    def body(o0, o1, o2, l0, l1, l2, z_ref, lse_ref, o_ref, g_ref):
        a, b, c = l0[...], l1[...], l2[...]
        m = jnp.maximum(a, jnp.maximum(b, c))
        ea, eb, ec = jnp.exp(a - m), jnp.exp(b - m), jnp.exp(c - m)
        den = ea + eb + ec
        out = (ea * o0[...] + eb * o1[...] + ec * o2[...]) / den
        lse_ref[...] = m + jnp.log(den)
        o_ref[...] = out
        zv = z_ref[...]
        g_ref[...] = (out * (zv * _sigmoid(zv))).astype(g_ref.dtype)

    s = zb.shape[0]
    return _rows_call(body, "attn_merge_fwd", s, [(p, "tile") for p in parts] + [(l, "tile") for l in lses] + [(zb, "tile")],
                      [((s, DIL_W), F32, "tile"), ((s, DIL_W), F32, "tile"), ((s, DIL_W), MXU, "tile")])


def _attn_merge_bwd(d_gated, o_joint, zb):
    def body(d_ref, o_ref, z_ref, do_ref, dz_ref, dl_ref):
        zv = z_ref[...]
        sg = _sigmoid(zv)
        dv = d_ref[...]
        ov = o_ref[...]
        do = dv * (zv * sg)
        do_ref[...] = do
        dz_ref[...] = (dv * ov * (sg * (1.0 + zv * (1.0 - sg)))).astype(dz_ref.dtype)
        for h in range(DIL_HEADS):
            cols = slice(h * LANES, (h + 1) * LANES)
            dl_ref[:, cols] = jnp.broadcast_to(jnp.sum(do[:, cols] * ov[:, cols], axis=-1, keepdims=True),
                                               (do.shape[0], LANES))

    s = zb.shape[0]
    return _rows_call(body, "attn_merge_bwd", s, [(d_gated, "tile"), (o_joint, "tile"), (zb, "tile")],
                      [((s, DIL_W), F32, "tile"), ((s, DIL_W), MXU, "tile"), ((s, DIL_W), F32, "tile")])


def _gate_merge_fwd(ga, gb, ya, yb):
    def body(ga_ref, gb_ref, ya_ref, yb_ref, m_ref):
        m_ref[...] = (_sigmoid(ga_ref[...]) * ya_ref[...] + _sigmoid(gb_ref[...]) * yb_ref[...]).astype(m_ref.dtype)

    return _rows_call(body, "gate_merge_fwd", ga.shape[0], [(ga, "tile"), (gb, "tile"), (ya, "tile"), (yb, "tile")],
                      [(ga.shape, MXU, "tile")])[0]


def _gate_merge_bwd(dm, ga, gb, ya, yb):
    def body(dm_ref, ga_ref, gb_ref, ya_ref, yb_ref, dya_ref, dyb_ref, dga_ref, dgb_ref):
        dmv = dm_ref[...]
        sa, sb = _sigmoid(ga_ref[...]), _sigmoid(gb_ref[...])
        dya_ref[...] = (dmv * sa).astype(dya_ref.dtype)
        dyb_ref[...] = (dmv * sb).astype(dyb_ref.dtype)
        dga_ref[...] = (dmv * ya_ref[...] * sa * (1.0 - sa)).astype(dga_ref.dtype)
        dgb_ref[...] = (dmv * yb_ref[...] * sb * (1.0 - sb)).astype(dgb_ref.dtype)

    return _rows_call(body, "gate_merge_bwd", ga.shape[0],
                      [(dm, "tile"), (ga, "tile"), (gb, "tile"), (ya, "tile"), (yb, "tile")],
                      [(ga.shape, MXU, "tile")] * 4)


def _final_fwd_bwd(x, x2pre, target, wf_row):
    s, dm = x.shape

    def body(x_ref, p_ref, t_ref, w_ref, loss_ref, dw_ref, dx_ref, dxb_ref):
        x2 = x_ref[...] + p_ref[...]
        r = lax.rsqrt(jnp.mean(x2 * x2, axis=-1, keepdims=True) + NORM_EPS)
        w = w_ref[...]
        err = x2 * r * w - t_ref[...]
        tile_loss = 0.5 * jnp.sum(jnp.mean(err * err, axis=-1, keepdims=True), axis=0, keepdims=True)
        _acc_add(loss_ref, jnp.broadcast_to(tile_loss, (SUBLANES, LANES)))
        dy = err * (1.0 / dm)
        row = jnp.sum(dy * x2 * r, axis=0, keepdims=True)
        _acc_add(dw_ref, jnp.concatenate([row, jnp.zeros((SUBLANES - 1, dm), F32)], axis=0))
        dn = dy * w
        dx2 = r * dn - x2 * (r * r * r) * jnp.mean(dn * x2, axis=-1, keepdims=True)
        dx_ref[...] = dx2
        dxb_ref[...] = dx2.astype(dxb_ref.dtype)

    return _rows_call(body, "final_fwd_bwd", s, [(x, "tile"), (x2pre, "tile"), (target, "tile"), (wf_row, "full")],
                      [((SUBLANES, LANES), F32, "acc"), ((SUBLANES, dm), F32, "acc"), ((s, dm), F32, "tile"),
                       ((s, dm), MXU, "tile")])


def _lane_pick(x, idx):
    lane = lax.broadcasted_iota(jnp.int32, x.shape, 1)
    return jnp.sum(jnp.where(lane == idx, x, 0.0), axis=-1, keepdims=True)


PAIR = 2 * DN_CHUNK
SCAN_CHUNKS = 4


def _bmm(a, b):
    return lax.dot_general(a.astype(MXU), b.astype(MXU), (((2,), (1,)), ((0,), (0,))), preferred_element_type=F32)


def _bmm_nt(a, b):
    return lax.dot_general(a.astype(MXU), b.astype(MXU), (((2,), (2,)), ((0,), (0,))), preferred_element_type=F32)


def _bmm_tn(a, b):
    return lax.dot_general(a.astype(MXU), b.astype(MXU), (((1,), (1,)), ((0,), (0,))), preferred_element_type=F32)


def _bmm3(a, b):
    ah = a.astype(jnp.bfloat16)
    al = (a - ah.astype(F32)).astype(jnp.bfloat16)
    bh = b.astype(jnp.bfloat16)
    bl = (b - bh.astype(F32)).astype(jnp.bfloat16)
    f = lambda p, q: lax.dot_general(p, q, (((2,), (1,)), ((0,), (0,))), preferred_element_type=F32)
    return f(ah, bh) + (f(ah, bl) + f(al, bh))


def _pair_masks():
    row = lax.broadcasted_iota(jnp.int32, (PAIR, PAIR), 0)
    col = lax.broadcasted_iota(jnp.int32, (PAIR, PAIR), 1)
    same = (row >= DN_CHUNK) == (col >= DN_CHUNK)
    return dict(causal=same & (row >= col), strict=same & (row > col), upper=same & (row <= col), eye=row == col,
                first=row < DN_CHUNK, row=row, lane=col)


def _pair_decay(bgv, masks):
    gc_all = _dot01(masks["causal"].astype(F32), bgv)
    out = []
    for h in range(DN_HEADS):
        beta = _lane_pick(bgv, h)
        gcb = jnp.broadcast_to(_lane_pick(gc_all, DN_HEADS + h), (PAIR, PAIR))
        gam = jnp.where(masks["causal"], jnp.exp(jnp.minimum(gcb - gcb.T, 0.0)), 0.0)
        gl = jnp.where(masks["first"], gcb[DN_CHUNK - 1:DN_CHUNK, :], gcb[PAIR - 1:PAIR, :])
        out.append((beta, gcb, gam, gl))
    return out


def _pair_inverse(a_strict, eye):
    n = -a_strict
    t = eye.astype(F32)[None] + n
    p = n
    for _ in range(int(math.log2(DN_CHUNK)) - 1):
        p = _bmm3(p, p)
        t = t + _bmm3(t, p)
    return t


def _head_cols(h):
    return slice(h * LANES, (h + 1) * LANES)


def _delta_prep(q, k, v, bg):
    s = q.shape[0]
    c = DN_CHUNK
    n_chunks = s // c

    def body(q_ref, k_ref, v_ref, bg_ref, u_ref, w_ref, qd_ref, kd_ref, aqk_ref, dl_ref, t2_ref):
        masks = _pair_masks()
        dec = _pair_decay(bg_ref[...], masks)
        kbs, ks, gams, vbs, kbes, qs, qds, kds, dls = ([] for _ in range(9))
        for h in range(DN_HEADS):
            beta, gcb, gam, gl = dec[h]
            qh, kh, vh = q_ref[:, _head_cols(h)], k_ref[:, _head_cols(h)], v_ref[:, _head_cols(h)]
            eg = jnp.exp(gcb)
            kb = kh * beta
            kbs.append(kb); ks.append(kh); gams.append(gam); vbs.append(vh * beta); kbes.append(kb * eg)
            qs.append(qh); qds.append(qh * eg); kds.append(kh * jnp.exp(gl - gcb)); dls.append(jnp.exp(gl))
        st = lambda xs: jnp.stack(xs, axis=0)
        kmat, gam = st(ks), st(gams)
        a = jnp.where(masks["strict"][None], _bmm_nt(st(kbs), kmat) * gam, 0.0)
        t = _pair_inverse(a, masks["eye"])
        u = _bmm(t, st(vbs))
        w = _bmm(t, st(kbes))
        aqk = _bmm_nt(st(qs), kmat) * gam
        t2_ref[0] = t.astype(t2_ref.dtype)
        for half in range(2):
            rows = slice(half * c, (half + 1) * c)
            u_ref[half] = u[:, rows, :]
            w_ref[half] = w[:, rows, :].astype(w_ref.dtype)
            qd_ref[half] = st(qds)[:, rows, :].astype(qd_ref.dtype)
            kd_ref[half] = st(kds)[:, rows, :].astype(kd_ref.dtype)
            aqk_ref[half] = aqk[:, rows, rows].astype(aqk_ref.dtype)
            dl_ref[half] = st(dls)[:, half * c:half * c + SUBLANES, :]

    row_spec = lambda w_: pl.BlockSpec((PAIR, w_), lambda i: (i, 0))
    hm = lambda a_, b_: pl.BlockSpec((2, DN_HEADS, a_, b_), lambda i: (i, 0, 0, 0))
    hm_shape = lambda a_, b_, dt: jax.ShapeDtypeStruct((n_chunks, DN_HEADS, a_, b_), dt)
    return _pcall(
        body, name="delta_prep", grid=(n_chunks // 2,),
        in_specs=[row_spec(D_MODEL)] * 3 + [row_spec(LANES)],
        out_specs=[hm(c, LANES)] * 4 + [hm(c, c), hm(SUBLANES, LANES),
                   pl.BlockSpec((1, DN_HEADS, PAIR, PAIR), lambda i: (i, 0, 0, 0))],
        out_shape=[hm_shape(c, LANES, F32), hm_shape(c, LANES, MXU), hm_shape(c, LANES, MXU), hm_shape(c, LANES, MXU),
                   hm_shape(c, c, MXU), hm_shape(SUBLANES, LANES, F32),
                   jax.ShapeDtypeStruct((n_chunks // 2, DN_HEADS, PAIR, PAIR), MXU)],
        compiler_params=_params("parallel"),
    )(q, k, v, bg)


def _delta_scan_fwd(u, w, qd, kd, aqk, dl):
    n_chunks = u.shape[0]
    c = DN_CHUNK
    g_n = SCAN_CHUNKS

    def body(u_ref, w_ref, qd_ref, kd_ref, aqk_ref, dl_ref, o_ref, vnew_ref, st_ref, state):
        @pl.when(pl.program_id(0) == 0)
        def _():
            state[...] = jnp.zeros_like(state)

        for g in range(g_n):
            sv = state[...]
            sb = sv.astype(MXU)
            vnew = u_ref[g] - _bmm(w_ref[g], sb)
            o = _bmm(qd_ref[g], sb) + _bmm(aqk_ref[g], vnew)
            state[...] = sv * dl_ref[g][:, 0:1, :] + _bmm_tn(kd_ref[g], vnew)
            vnew_ref[g] = vnew.astype(vnew_ref.dtype)
            st_ref[g] = sb
            for h in range(DN_HEADS):
                o_ref[g * c:(g + 1) * c, _head_cols(h)] = o[h]

    hm = lambda a_, b_: pl.BlockSpec((g_n, DN_HEADS, a_, b_), lambda i: (i, 0, 0, 0))
    return _pcall(
        body, name="delta_scan_fwd", grid=(n_chunks // g_n,),
        in_specs=[hm(c, LANES)] * 4 + [hm(c, c), hm(SUBLANES, LANES)],
        out_specs=[pl.BlockSpec((g_n * c, D_MODEL), lambda i: (i, 0)), hm(c, LANES), hm(DN_DK, DN_DK)],
        out_shape=[jax.ShapeDtypeStruct((n_chunks * c, D_MODEL), F32),
                   jax.ShapeDtypeStruct((n_chunks, DN_HEADS, c, LANES), MXU),
                   jax.ShapeDtypeStruct((n_chunks, DN_HEADS, DN_DK, DN_DK), MXU)],
        scratch_shapes=[pltpu.VMEM((DN_HEADS, DN_DK, DN_DK), F32)],
        compiler_params=_params("arbitrary"),
    )(u, w, qd, kd, aqk, dl)


def _delta_scan_bwd(w, qd, kd, aqk, dl, vnew, st, do):
    n_chunks = w.shape[0]
    c = DN_CHUNK
    g_n = SCAN_CHUNKS
    steps = n_chunks // g_n

    def body(w_ref, qd_ref, kd_ref, aqk_ref, dl_ref, vnew_ref, st_ref, do_ref, dvnew_ref, dkd_ref, ddl_ref, dstate):
        @pl.when(pl.program_id(0) == 0)
        def _():
            dstate[...] = jnp.zeros_like(dstate)

        for g in reversed(range(g_n)):
            ds = dstate[...]
            dsb = ds.astype(MXU)
            doh = jnp.stack([do_ref[g * c:(g + 1) * c, _head_cols(h)] for h in range(DN_HEADS)], axis=0)
            dvnew = _bmm_tn(aqk_ref[g], doh) + _bmm(kd_ref[g], dsb)
            dkd_ref[g] = _bmm_nt(vnew_ref[g], dsb)
            ddl = jnp.sum(jnp.sum(st_ref[g].astype(F32) * ds, axis=2, keepdims=True), axis=1, keepdims=True)
            ddl_ref[g] = jnp.broadcast_to(ddl, (DN_HEADS, SUBLANES, LANES))
            dstate[...] = ds * dl_ref[g][:, 0:1, :] + _bmm_tn(qd_ref[g], doh) - _bmm_tn(w_ref[g], dvnew)
            dvnew_ref[g] = dvnew.astype(dvnew_ref.dtype)

    rev = lambda i: steps - 1 - i
    hm = lambda a_, b_: pl.BlockSpec((g_n, DN_HEADS, a_, b_), lambda i: (rev(i), 0, 0, 0))
    return _pcall(
        body, name="delta_scan_bwd", grid=(steps,),
        in_specs=[hm(c, LANES)] * 3 + [hm(c, c), hm(SUBLANES, LANES), hm(c, LANES), hm(DN_DK, DN_DK),
                  pl.BlockSpec((g_n * c, D_MODEL), lambda i: (rev(i), 0))],
        out_specs=[hm(c, LANES), hm(c, LANES), hm(SUBLANES, LANES)],
        out_shape=[jax.ShapeDtypeStruct((n_chunks, DN_HEADS, c, LANES), MXU),
                   jax.ShapeDtypeStruct((n_chunks, DN_HEADS, c, LANES), F32),
                   jax.ShapeDtypeStruct((n_chunks, DN_HEADS, SUBLANES, LANES), F32)],
        scratch_shapes=[pltpu.VMEM((DN_HEADS, DN_DK, DN_DK), F32)],
        compiler_params=_params("arbitrary"),
    )(w, qd, kd, aqk, dl, vnew, st, do)


def _delta_post_bwd(q, k, v, bg, t2, st, vnew, do, dvnew, dkd, ddl):
    s = q.shape[0]
    c = DN_CHUNK

    def body(q_ref, k_ref, v_ref, bg_ref, t2_ref, st_ref, vnew_ref, do_ref, dvnew_ref, dkd_ref, ddl_ref,
             dq_ref, dk_ref, dv_ref, dbg_ref):
        masks = _pair_masks()
        first = masks["first"][None]
        dec = _pair_decay(bg_ref[...], masks)
        st_ = lambda xs: jnp.stack(xs, axis=0)
        heads = range(DN_HEADS)
        qm_, km_, vm_, dom = (st_([r[:, _head_cols(h)] for h in heads]) for r in (q_ref, k_ref, v_ref, do_ref))
        beta = st_([dec[h][0] for h in heads])
        gcb = st_([dec[h][1] for h in heads])
        gam = st_([dec[h][2] for h in heads])
        gl = st_([dec[h][3] for h in heads])
        pair = lambda ref: jnp.concatenate([ref[0], ref[1]], axis=1)
        vnew2, dvnew2, dkd2 = pair(vnew_ref), pair(dvnew_ref), pair(dkd_ref)
        halves = lambda x: (x[:, :c, :], x[:, c:, :])
        by_state = lambda x: jnp.concatenate([_bmm_nt(xh, st_ref[i]) for i, xh in enumerate(halves(x))], axis=1)
        dqd = by_state(dom)
        dw = -by_state(dvnew2)
        ddl2 = jnp.where(first, ddl_ref[0][:, 0:1, :], ddl_ref[1][:, 0:1, :])

        eg = jnp.exp(gcb)
        egl = jnp.exp(gl - gcb)
        dl = jnp.exp(gl)
        kb = km_ * beta
        kk = _bmm_nt(kb, km_)
        a = jnp.where(masks["strict"][None], kk * gam, 0.0)
        t = t2_ref[0]
        vb = vm_ * beta
        kbe = kb * eg
        u = _bmm(t, vb)
        w = _bmm(t, kbe)
        aqk = _bmm_nt(qm_, km_) * gam
        qd = qm_ * eg
        kd = km_ * egl

        daqk = jnp.where(masks["causal"][None], _bmm_nt(dom, vnew2), 0.0)
        dvb = _bmm_tn(t, dvnew2)
        dkbe = _bmm_tn(t, dw)
        da = jnp.where(masks["strict"][None], -(_bmm_nt(dvb, u) + _bmm_nt(dkbe, w)), 0.0)
        pm = da * gam
        qmm = daqk * gam
        dkb = _bmm(pm, km_) + dkbe * eg
        dkh = _bmm_tn(pm, kb) + _bmm_tn(qmm, qm_) + dkd2 * egl + dkb * beta
        dqh = _bmm(qmm, km_) + dqd * eg
        xm = da * a + daqk * aqk
        ones = jnp.ones((DN_HEADS, PAIR, LANES), F32)
        hi, mid, lo = _split3(xm)
        colsum = _bmm_tn(hi, ones) + (_bmm_tn(mid, ones) + _bmm_tn(lo, ones))
        tmp = jnp.sum(dkd2 * kd, axis=-1, keepdims=True)
        dgc = (jnp.sum(xm, axis=-1, keepdims=True) - colsum + jnp.sum(dkbe * kbe, axis=-1, keepdims=True)
               + jnp.sum(dqd * qd, axis=-1, keepdims=True) - tmp)
        sum0 = jnp.sum(jnp.where(first, tmp, 0.0), axis=1, keepdims=True)
        sum1 = jnp.sum(jnp.where(first, 0.0, tmp), axis=1, keepdims=True)
        dgl = jnp.where(first, sum0, sum1) + ddl2 * dl
        last = (masks["row"] == c - 1) | (masks["row"] == PAIR - 1)
        dgc = dgc + jnp.where(last[None], dgl, 0.0)
        dbeta = jnp.sum(dvb * vm_, axis=-1, keepdims=True) + jnp.sum(dkb * km_, axis=-1, keepdims=True)
        dvh = dvb * beta

        lane = masks["lane"]
        dgc_lanes = jnp.zeros((PAIR, LANES), F32)
        dbg = jnp.zeros((PAIR, LANES), F32)
        for h in heads:
            dq_ref[:, _head_cols(h)] = dqh[h]
            dk_ref[:, _head_cols(h)] = dkh[h]
            dv_ref[:, _head_cols(h)] = dvh[h]
            dgc_lanes = dgc_lanes + jnp.where(lane == DN_HEADS + h, dgc[h], 0.0)
            dbg = dbg + jnp.where(lane == h, dbeta[h], 0.0)
        dbg_ref[...] = dbg + _dot01(masks["upper"].astype(F32), dgc_lanes)

    n_pairs = s // PAIR
    row_spec = lambda w_: pl.BlockSpec((PAIR, w_), lambda i: (i, 0))
    hm = lambda a_, b_: pl.BlockSpec((2, DN_HEADS, a_, b_), lambda i: (i, 0, 0, 0))
    return _pcall(
        body, name="delta_post_bwd", grid=(n_pairs,),
        in_specs=[row_spec(D_MODEL)] * 3 + [row_spec(LANES), pl.BlockSpec((1, DN_HEADS, PAIR, PAIR), lambda i: (i, 0, 0, 0)),
                  hm(DN_DK, DN_DK), hm(c, LANES), row_spec(D_MODEL), hm(c, LANES), hm(c, LANES), hm(SUBLANES, LANES)],
        out_specs=[row_spec(D_MODEL)] * 3 + [row_spec(LANES)],
        out_shape=[jax.ShapeDtypeStruct((s, D_MODEL), F32)] * 3 + [jax.ShapeDtypeStruct((s, LANES), F32)],
        compiler_params=_params("parallel"),
    )(q, k, v, bg, t2, st, vnew, do, dvnew, dkd, ddl)


def _alibi_slope(group, head):
    n = N_DIL * DIL_HEADS
    return float(2.0 ** (-8.0 * (group * DIL_HEADS + head + 1) / n))


def _attn_plan(s, group):
    window, dil = DIL_GROUPS[group]
    assert window // dil == ATT_BLOCK
    assert (s // dil) % ATT_BLOCK == 0, "sub-sequence length must be a whole number of attention blocks"
    return dil, s // dil // ATT_BLOCK, (DIL_HEADS if dil == 1 else 1)


def _attn_specs(group, dil, nb, hp):
    rows = ATT_BLOCK * dil

    def spec(col0, shift):
        if shift < 0:
            f = lambda hb, n: (jnp.maximum(n - 1, 0), col0 + hb)
        elif shift > 0:
            f = lambda hb, n: (jnp.minimum(n + 1, nb - 1), col0 + hb)
        else:
            f = lambda hb, n: (n, col0 + hb)
        return pl.BlockSpec((rows, hp * LANES), f)

    return (lambda shift: spec(group * (DIL_HEADS // hp), shift)), (lambda shift: spec(0, shift))


def _sub_rows(ref, r, dil, cols):
    return ref[:, cols] if dil == 1 else ref[pl.ds(r, ATT_BLOCK, stride=dil), cols]


def _set_sub_rows(ref, r, dil, cols, value):
    if dil == 1:
        ref[:, cols] = value
    else:
        ref[pl.ds(r, ATT_BLOCK, stride=dil), cols] = value


def _step_slope(group, hp, hh):
    if hp == DIL_HEADS:
        return _alibi_slope(group, hh)
    hb = pl.program_id(0)
    slope = _alibi_slope(group, DIL_HEADS - 1)
    for h in reversed(range(DIL_HEADS - 1)):
        slope = jnp.where(hb == h, _alibi_slope(group, h), slope)
    return slope


def _window_bias(dil, n):
    a = lax.broadcasted_iota(jnp.int32, (ATT_BLOCK, 2 * ATT_BLOCK), 0)
    b = lax.broadcasted_iota(jnp.int32, (ATT_BLOCK, 2 * ATT_BLOCK), 1)
    dist = ATT_BLOCK + a - b
    valid = (dist >= 0) & (dist <= ATT_BLOCK) & ((b >= ATT_BLOCK) | (n > 0))
    return (dist * dil).astype(F32), valid


def _attn_fwd(qb, kb, vb, group):
    s = qb.shape[0]
    dil, nb, hp = _attn_plan(s, group)
    qkv, per_head = _attn_specs(group, dil, nb, hp)

    def body(q_ref, kp_ref, kc_ref, vp_ref, vc_ref, o_ref, lse_ref):
        n = pl.program_id(1)
        distd, valid = _window_bias(dil, n)
        for hh in range(hp):
            cols = _head_cols(hh)
            slope = _step_slope(group, hp, hh)
            for r in range(dil):
                sub = lambda ref: _sub_rows(ref, r, dil, cols).astype(MXU)
                kk = jnp.concatenate([sub(kp_ref), sub(kc_ref)], axis=0)
                vv = jnp.concatenate([sub(vp_ref), sub(vc_ref)], axis=0)
                sc = _dot_nt(sub(q_ref), kk) * DIL_DH ** -0.5 - slope * distd
                sc = jnp.where(valid, sc, -1e30)
                mx = jnp.max(sc, axis=-1, keepdims=True)
                p = jnp.where(valid, jnp.exp(sc - mx), 0.0)
                den = jnp.sum(p, axis=-1, keepdims=True)
                _set_sub_rows(o_ref, r, dil, cols, _dot(p, vv) / den)
                _set_sub_rows(lse_ref, r, dil, cols, jnp.broadcast_to(mx + jnp.log(den), (ATT_BLOCK, LANES)))

    return _pcall(
        body, name=f"attn_fwd_g{group}", grid=(DIL_HEADS // hp, nb),
        in_specs=[qkv(0), qkv(-1), qkv(0), qkv(-1), qkv(0)], out_specs=[per_head(0)] * 2,
        out_shape=[jax.ShapeDtypeStruct((s, DIL_W), F32)] * 2,
        compiler_params=_params("parallel", "parallel"),
    )(qb, kb, kb, vb, vb)


def _attn_bwd(qb, kb, vb, d_o, lse, delta, group):
    s = qb.shape[0]
    dil, nb, hp = _attn_plan(s, group)
    qkv, per_head = _attn_specs(group, dil, nb, hp)
    scale = DIL_DH ** -0.5

    def body(qc_ref, qn_ref, kp_ref, kc_ref, vp_ref, vc_ref, doc_ref, don_ref, lc_ref, ln_ref, dc_ref, dn_ref,
             dq_ref, dk_ref, dv_ref, dq_acc, dk_acc, dv_acc):
        n = pl.program_id(1)
        distd, valid = _window_bias(dil, n)
        bk = lax.broadcasted_iota(jnp.int32, (ATT_BLOCK, 2 * ATT_BLOCK), 0)
        aq = lax.broadcasted_iota(jnp.int32, (ATT_BLOCK, 2 * ATT_BLOCK), 1)
        dist_t = aq - bk
        valid_t = (dist_t >= 0) & (dist_t <= ATT_BLOCK) & ((aq < ATT_BLOCK) | (n < nb - 1))
        distd_t = (dist_t * dil).astype(F32)
        for hh in range(hp):
            cols = _head_cols(hh)
            slope = _step_slope(group, hp, hh)
            for r in range(dil):
                sub = lambda ref: _sub_rows(ref, r, dil, cols)
                qc, kc, vc = sub(qc_ref).astype(MXU), sub(kc_ref).astype(MXU), sub(vc_ref).astype(MXU)
                doc, lc, dc = sub(doc_ref), sub(lc_ref), sub(dc_ref)
                kk = jnp.concatenate([sub(kp_ref).astype(MXU), kc], axis=0)
                vv = jnp.concatenate([sub(vp_ref).astype(MXU), vc], axis=0)
                sc = _dot_nt(qc, kk) * scale - slope * distd
                p = jnp.where(valid, jnp.exp(jnp.minimum(sc - jnp.concatenate([lc] * 2, axis=1), 0.0)), 0.0)
                dsc = p * (_dot_nt(doc, vv) - jnp.concatenate([dc] * 2, axis=1))
                _set_sub_rows(dq_acc, r, dil, cols, _dot(dsc, kk) * scale)
                qq = jnp.concatenate([qc, sub(qn_ref).astype(MXU)], axis=0)
                doo = jnp.concatenate([doc.astype(MXU), sub(don_ref).astype(MXU)], axis=0)
                lse_t = jnp.concatenate([lc, sub(ln_ref)], axis=0).T
                del_t = jnp.concatenate([dc, sub(dn_ref)], axis=0).T
                sc_t = _dot_nt(kc, qq) * scale - slope * distd_t
                p_t = jnp.where(valid_t, jnp.exp(jnp.minimum(sc_t - lse_t, 0.0)), 0.0)
                ds_t = p_t * (_dot_nt(vc, doo) - del_t)
                _set_sub_rows(dk_acc, r, dil, cols, _dot(ds_t, qq) * scale)
                _set_sub_rows(dv_acc, r, dil, cols, _dot(p_t, doo))
        dq_ref[...] = dq_acc[...].astype(dq_ref.dtype)
        dk_ref[...] = dk_acc[...].astype(dk_ref.dtype)
        dv_ref[...] = dv_acc[...].astype(dv_ref.dtype)

    return _pcall(
        body, name=f"attn_bwd_g{group}", grid=(DIL_HEADS // hp, nb),
        in_specs=[qkv(0), qkv(1), qkv(-1), qkv(0), qkv(-1), qkv(0)] + [per_head(0), per_head(1)] * 3,
        out_specs=[per_head(0)] * 3,
        out_shape=[jax.ShapeDtypeStruct((s, DIL_W), MXU)] * 3,
        scratch_shapes=[pltpu.VMEM((ATT_BLOCK * dil, hp * LANES), F32)] * 3,
        compiler_params=_params("parallel", "parallel"),
    )(qb, qb, kb, kb, vb, vb, d_o, d_o, lse, lse, delta, delta)


def _my_place():
    mx, my, mc = lax.axis_index("x"), lax.axis_index("y"), lax.axis_index("c")
    return mx, my, mc, 4 * mx + 2 * my + mc


N_CHIPS = 4


def _shard_row_tile(r):
    if r <= 512:
        return r
    return 128 if r % 128 == 0 else 480


def _other_chips(mx, my):
    return [(1 - mx, my), (mx, 1 - my), (1 - mx, 1 - my)]


def _all_gather(xs, name):
    n = len(xs)

    def body(*refs):
        x_refs, o_refs = refs[:n], refs[n:2 * n]
        send_sems, recv_sems, local_sems = refs[2 * n:]
        mx, my, mc, me = _my_place()
        sibling, sibling_id = (mx, my, 1 - mc), 4 * mx + 2 * my + (1 - mc)
        chips = _other_chips(mx, my)

        def copy(a, k, slot, to, src=None):
            dst = o_refs[a].at[slot]
            return pltpu.make_async_remote_copy(
                src_ref=dst if src is None else src, dst_ref=dst, send_sem=send_sems.at[a, k],
                recv_sem=recv_sems.at[a, k], device_id=to, device_id_type=MESH)

        local = [pltpu.make_async_copy(x_refs[a], o_refs[a].at[me], local_sems.at[a]) for a in range(n)]
        for cp in local:
            cp.start()
        sends = []
        for a in range(n):
            sends.append(copy(a, 0, me, sibling, src=x_refs[a]))
            sends += [copy(a, 1 + j, me, (px, py, mc), src=x_refs[a]) for j, (px, py) in enumerate(chips)]
        for cp in sends:
            cp.start()
        for j, (px, py) in enumerate(chips):
            slot = 4 * px + 2 * py + mc
            for a in range(n):
                copy(a, 1 + j, slot, (px, py, mc)).wait_recv()
                passed = copy(a, 4 + j, slot, sibling)
                passed.start()
                sends.append(passed)
        for a in range(n):
            copy(a, 0, sibling_id, sibling).wait_recv()
            for j, (px, py) in enumerate(chips):
                copy(a, 4 + j, 4 * px + 2 * py + (1 - mc), sibling).wait_recv()
        for cp in sends:
            cp.wait_send()
        for cp in local:
            cp.wait()

    any_spec = pl.BlockSpec(memory_space=pl.ANY)
    return _pcall(
        body, name=name,
        in_specs=[any_spec] * n, out_specs=[any_spec] * n,
        out_shape=[jax.ShapeDtypeStruct((N_DEV,) + x.shape, x.dtype) for x in xs],
        scratch_shapes=[pltpu.SemaphoreType.DMA((n, N_DEV - 1)), pltpu.SemaphoreType.DMA((n, N_DEV - 1)),
                        pltpu.SemaphoreType.DMA((n,))],
    )(*xs)


def _pair_exchange(gs, name):
    n = len(gs)

    def body(*refs):
        g_refs, o_refs = refs[:n], refs[n:2 * n]
        send_sems, recv_sems = refs[2 * n:]
        mx, my, mc, _ = _my_place()
        copies = [pltpu.make_async_remote_copy(
            src_ref=g_refs[a].at[1 - mc], dst_ref=o_refs[a], send_sem=send_sems.at[a], recv_sem=recv_sems.at[a],
            device_id=(mx, my, 1 - mc), device_id_type=MESH) for a in range(n)]
        for cp in copies:
            cp.start()
        for cp in copies:
            cp.wait()

    any_spec = pl.BlockSpec(memory_space=pl.ANY)
    return _pcall(
        body, name=name,
        in_specs=[any_spec] * n, out_specs=[any_spec] * n,
        out_shape=[jax.ShapeDtypeStruct(g.shape[1:], g.dtype) for g in gs],
        scratch_shapes=[pltpu.SemaphoreType.DMA((n,)), pltpu.SemaphoreType.DMA((n,))],
    )(*gs)


def _pair_add(g, other, name):
    _, chips, r, c = g.shape
    tr = _shard_row_tile(r)
    core = lax.axis_index("c").astype(jnp.int32).reshape(1)

    def body(core_ref, g_ref, o_ref, h_ref):
        h_ref[...] = (g_ref[...].astype(F32)[0] + o_ref[...].astype(F32)).astype(h_ref.dtype)

    blk = pl.BlockSpec((1, tr, c), lambda p, i, core_ref: (p, i, 0))
    return _pcall(
        body, name=name,
        grid_spec=pltpu.PrefetchScalarGridSpec(
            num_scalar_prefetch=1, grid=(chips, pl.cdiv(r, tr)),
            in_specs=[pl.BlockSpec((1, 1, tr, c), lambda p, i, core_ref: (core_ref[0], p, i, 0)), blk],
            out_specs=blk),
        out_shape=jax.ShapeDtypeStruct((chips, r, c), g.dtype),
        compiler_params=_params("parallel", "parallel"),
    )(core, g, other)


def _chip_exchange(hs, name):
    n = len(hs)

    def body(*refs):
        h_refs, o_refs = refs[:n], refs[n:2 * n]
        send_sems, recv_sems, local_sems = refs[2 * n:]
        mx, my, mc, _ = _my_place()
        my_chip = 2 * mx + my
        chips = _other_chips(mx, my)
        local = [pltpu.make_async_copy(h_refs[a].at[my_chip], o_refs[a].at[my_chip], local_sems.at[a]) for a in range(n)]
        for cp in local:
            cp.start()
        for j, (px, py) in enumerate(chips):
            for a in range(n):
                pltpu.make_async_remote_copy(
                    src_ref=h_refs[a].at[2 * px + py], dst_ref=o_refs[a].at[my_chip], send_sem=send_sems.at[a, j],
                    recv_sem=recv_sems.at[a, j], device_id=(px, py, mc), device_id_type=MESH).start()
        for j, (px, py) in enumerate(chips):
            for a in range(n):
                pltpu.make_async_remote_copy(
                    src_ref=h_refs[a].at[2 * px + py], dst_ref=o_refs[a].at[2 * px + py], send_sem=send_sems.at[a, j],
                    recv_sem=recv_sems.at[a, j], device_id=(px, py, mc), device_id_type=MESH).wait()
        for cp in local:
            cp.wait()

    any_spec = pl.BlockSpec(memory_space=pl.ANY)
    return _pcall(
        body, name=name,
        in_specs=[any_spec] * n, out_specs=[any_spec] * n,
        out_shape=[jax.ShapeDtypeStruct(h.shape, h.dtype) for h in hs],
        scratch_shapes=[pltpu.SemaphoreType.DMA((n, N_CHIPS - 1)), pltpu.SemaphoreType.DMA((n, N_CHIPS - 1)),
                        pltpu.SemaphoreType.DMA((n,))],
    )(*hs)


def _adamw(parts, w, m, v, name):
    r, c = w.shape
    n_parts = parts.shape[0]
    tr = _shard_row_tile(r)
    bc1 = 1.0 - ADAM_B1 ** ADAM_STEP
    bc2 = 1.0 - ADAM_B2 ** ADAM_STEP

    def body(p_ref, w_ref, m_ref, v_ref, g_ref, d_ref, nm_ref, nv_ref):
        g = p_ref[0].astype(F32)
        for j in range(1, n_parts):
            g = g + p_ref[j].astype(F32)
        nm = ADAM_B1 * m_ref[...] + (1.0 - ADAM_B1) * g
        nv = ADAM_B2 * v_ref[...] + (1.0 - ADAM_B2) * (g * g)
        g_ref[...] = g
        nm_ref[...] = nm
        nv_ref[...] = nv
        d_ref[...] = -ADAM_LR * ((nm / bc1) / (jnp.sqrt(nv / bc2) + ADAM_EPS) + ADAM_WD * w_ref[...])

    blk = pl.BlockSpec((tr, c), lambda i: (i, 0))
    return _pcall(
        body, name=name, grid=(pl.cdiv(r, tr),),
        in_specs=[pl.BlockSpec((n_parts, tr, c), lambda i: (0, i, 0)), blk, blk, blk],
        out_specs=[blk] * 4, out_shape=[jax.ShapeDtypeStruct((r, c), F32)] * 4,
        compiler_params=_params("parallel"),
    )(parts, w, m, v)


def _local_step(x, target, norm_w, w_segs, conv_w, a_log, dt_bias, dn_norm_w, w_o_dn, w_o_dil, w_out, final_norm_w):
    s = x.shape[0]
    w_qkv, w_za, w_ba, w_qb, w_kb, w_vb, w_zb, w_ga, w_gb = w_segs
    conv_w8 = jnp.concatenate([conv_w, jnp.zeros((SUBLANES - conv_w.shape[0], QKV_W), F32)], axis=0)
    pad8 = jnp.zeros((1, DN_HEADS), F32)
    alog_row = jnp.concatenate([pad8, a_log, jnp.zeros((1, LANES - 2 * DN_HEADS), F32)], axis=1)
    dtb_row = jnp.concatenate([pad8, dt_bias, jnp.zeros((1, LANES - 2 * DN_HEADS), F32)], axis=1)
    wf_row = final_norm_w.reshape(1, D_MODEL)

    hb = _rms_in_fwd(x, norm_w)
    qkv_pre, z_a, ba, z_b = _mm_out(hb, [w_qkv, w_za, w_ba, w_zb], "proj_fwd_a", w_is_out_by_in=True)
    q_b, k_b, v_b, g_a, g_b = _mm_out(hb, [w_qb, w_kb, w_vb, w_ga, w_gb], "proj_fwd_b", w_is_out_by_in=True)

    qn, kn, vn, bg = _dn_prep_fwd(qkv_pre, ba, conv_w8, alog_row, dtb_row)
    u_d, w_d, qd_d, kd_d, aqk_d, dl_d, t2_d = _delta_prep(qn, kn, vn, bg)
    o_a, vnew_d, st_d = _delta_scan_fwd(u_d, w_d, qd_d, kd_d, aqk_d, dl_d)
    on_b = _dn_post_fwd(o_a, z_a, dn_norm_w)
    y_a = _mm_out(on_b, [w_o_dn], "out_dn_fwd")[0]

    parts, lses = [], []
    for gi in range(N_DIL):
        o_g, l_g = _attn_fwd(q_b, k_b, v_b, gi)
        parts.append(o_g)
        lses.append(l_g)
    lse, o_joint, ob_b = _attn_merge_fwd(parts, lses, z_b)
    y_b = _mm_out(ob_b, [w_o_dil], "out_dil_fwd")[0]

    merged_b = _gate_merge_fwd(g_a, g_b, y_a, y_b)
    x2pre = _mm_out(merged_b, [w_out], "out_fwd")[0]
    loss8, dwf8, dx2, dx2_b = _final_fwd_bwd(x, x2pre, target, wf_row)

    d_merged = _mm_in([dx2_b], [w_out], "out_bwd")
    g_w_out = _mm_tn(merged_b, dx2_b, "out_wgrad")
    dya_b, dyb_b, dga_b, dgb_b = _gate_merge_bwd(d_merged, g_a, g_b, y_a, y_b)

    d_on = _mm_in([dya_b], [w_o_dn], "out_dn_bwd")
    g_w_o_dn = _mm_tn(on_b, dya_b, "out_dn_wgrad")
    d_o_a, dza_b, ddnw8 = _dn_post_bwd(d_on, o_a, z_a, dn_norm_w)

    d_ob = _mm_in([dyb_b], [w_o_dil], "out_dil_bwd")
    g_w_o_dil = _mm_tn(ob_b, dyb_b, "out_dil_wgrad")
    d_o, dzb_b, delta = _attn_merge_bwd(d_ob, o_joint, z_b)
    dqs, dks, dvs = [], [], []
    for gi in range(N_DIL):
        dq_g, dk_g, dv_g = _attn_bwd(q_b, k_b, v_b, d_o, lse, delta, gi)
        dqs.append(dq_g)
        dks.append(dk_g)
        dvs.append(dv_g)

    dvnew_d, dkd_d, ddl_d = _delta_scan_bwd(w_d, qd_d, kd_d, aqk_d, dl_d, vnew_d, st_d, d_o_a)
    dqn, dkn, dvn, dbg = _delta_post_bwd(qn, kn, vn, bg, t2_d, st_d, vnew_d, d_o_a, dvnew_d, dkd_d, ddl_d)
    dc, dba_b, dsmall8 = _dn_prep_bwd(qkv_pre, ba, conv_w8, alog_row, dtb_row, dqn, dkn, dvn, dbg)
    dqkv_b, dconv8 = _conv_bwd(dc, qkv_pre, conv_w8)

    dh_a = _mm_in([dqkv_b, dza_b, dba_b, dzb_b], [w_qkv, w_za, w_ba, w_zb], "proj_bwd_a", w_is_out_by_in=True)
    per_group = lambda w: [w[g * DIL_W:(g + 1) * DIL_W] for g in range(N_DIL)]
    dh_b = _mm_in(dqs + dks + dvs + [dga_b, dgb_b],
                  per_group(w_qb) + per_group(w_kb) + per_group(w_vb) + [w_ga, w_gb], "proj_bwd_b", w_is_out_by_in=True)
    dsegs = [dqkv_b, dza_b, dba_b] + dqs + dks + dvs + [dzb_b, dga_b, dgb_b]
    g_segs = [_mm_tn(d, hb, f"proj_wgrad_{j}") for j, d in enumerate(dsegs)]
    grad_x, dnw8 = _rms_in_bwd_sum(x, dh_a, dh_b, dx2, norm_w)

    small = dict(norm_w=dnw8[0:1], final_norm_w=dwf8[0:1], dn_norm_w=ddnw8[0:1],
                 a_log=dsmall8[0:1, DN_HEADS:2 * DN_HEADS], dt_bias=dsmall8[1:2, DN_HEADS:2 * DN_HEADS])
    return loss8[0:1, 0:1], grad_x, g_segs, dconv8[0:4], g_w_o_dn, g_w_o_dil, g_w_out, small


def _rms_in_bwd_sum(x, dh_a, dh_b, dx2, norm_w):
    def body(x_ref, da_ref, db_ref, dx2_ref, w_ref, dx_ref, dw_ref):
        xv = x_ref[...]
        r = lax.rsqrt(jnp.mean(xv * xv, axis=-1, keepdims=True) + NORM_EPS)
        dhv = da_ref[...] + db_ref[...]
        dn = dhv * w_ref[...]
        dx_ref[...] = dx2_ref[...] + r * dn - xv * (r * r * r) * jnp.mean(dn * xv, axis=-1, keepdims=True)
        row = jnp.sum(dhv * xv * r, axis=0, keepdims=True)
        _acc_add(dw_ref, jnp.concatenate([row, jnp.zeros((SUBLANES - 1, row.shape[1]), F32)], axis=0))

    return _rows_call(body, "rms_in_bwd", x.shape[0],
                      [(x, "tile"), (dh_a, "tile"), (dh_b, "tile"), (dx2, "tile"), (norm_w, "full")],
                      [(x.shape, F32, "tile"), ((SUBLANES, x.shape[1]), F32, "acc")])


def _split_proj_rows(wt_full):
    offs = [0]
    for n in PROJ_SIZES:
        offs.append(offs[-1] + n)
    seg = lambda a, b: wt_full[offs[a]:offs[b]]
    w_ba = jnp.concatenate([seg(4, 6), jnp.zeros((LANES - 2 * DN_HEADS, wt_full.shape[1]), wt_full.dtype)], axis=0)
    return [seg(0, 3), seg(3, 4), w_ba, seg(6, 7), seg(7, 8), seg(8, 9), seg(9, 10), seg(10, 11), seg(11, 12)]


def _join_proj_rows(g_segs):
    parts = list(g_segs)
    parts[2] = parts[2][:2 * DN_HEADS]
    return jnp.concatenate(parts, axis=0)


def _pack_small(norm_w, final_norm_w, dn_norm_w, a_log, dt_bias):
    pad = lambda r: jnp.concatenate([r, jnp.zeros((1, D_MODEL - r.shape[1]), F32)], axis=1)
    rows = [pad(norm_w.reshape(1, -1)), pad(final_norm_w.reshape(1, -1)), pad(dn_norm_w.reshape(1, -1)),
            pad(a_log.reshape(1, -1)), pad(dt_bias.reshape(1, -1)), jnp.zeros((SUBLANES - 5, D_MODEL), F32)]
    return jnp.concatenate(rows, axis=0)


def _unpack_small(p):
    return dict(norm_w=p[0:1], final_norm_w=p[1], dn_norm_w=p[2:3, :DN_DK], a_log=p[3:4, :DN_HEADS],
                dt_bias=p[4:5, :DN_HEADS])


def kernel(x, norm_w, w_in, conv_w, a_log, dt_bias, dn_norm_w, w_o_dn, w_o_dil, w_out, final_norm_w, loss_target, m_norm_w, m_w_in, m_conv_w, m_a_log, m_dt_bias, m_dn_norm_w, m_w_o_dn, m_w_o_dil, m_w_out, m_final_norm_w, v_norm_w, v_w_in, v_conv_w, v_a_log, v_dt_bias, v_dn_norm_w, v_w_o_dn, v_w_o_dil, v_w_out, v_final_norm_w):
    shard_w = w_in.shape[2]
    wt, m_wt, v_wt = (jnp.transpose(t[0]) for t in (w_in, m_w_in, v_w_in))
    gathered = _all_gather([wt.astype(MXU), w_o_dn[0].astype(MXU), w_o_dil[0].astype(MXU), w_out[0].astype(MXU),
                            conv_w[0]], "gather_weights")
    w_in_all, w_o_dn_all, w_o_dil_all, w_out_all, conv_all = gathered
    wt_full = w_in_all.reshape(N_DEV * shard_w, D_MODEL)
    w_o_dn_full = w_o_dn_all.reshape(D_MODEL, D_MODEL)
    w_o_dil_full = jnp.transpose(w_o_dil_all, (1, 0, 2)).reshape(DIL_W, D_MODEL)
    w_out_full = w_out_all.reshape(D_MODEL, D_MODEL)
    conv_full = jnp.transpose(conv_all, (1, 0, 2)).reshape(conv_w.shape[1], QKV_W)

    loss11, grad_x, g_segs, g_conv, g_w_o_dn, g_w_o_dil, g_w_out, small = _local_step(
        x[0], loss_target[0], norm_w, _split_proj_rows(wt_full), conv_full, a_log, dt_bias, dn_norm_w,
        w_o_dn_full, w_o_dil_full, w_out_full, final_norm_w)

    col_shards = lambda g, n: jnp.transpose(g.reshape(g.shape[0], N_DEV, n), (1, 0, 2))
    row_shards = lambda g: g.reshape(N_DEV, g.shape[0] // N_DEV, g.shape[1])
    sent = [row_shards(_join_proj_rows(g_segs)).astype(MXU), row_shards(g_w_o_dn).astype(MXU),
            col_shards(g_w_o_dil, w_o_dil.shape[2]).astype(MXU), row_shards(g_w_out).astype(MXU),
            col_shards(g_conv, conv_w.shape[2])]
    by_core = lambda g8: jnp.transpose(g8.reshape((N_CHIPS, 2) + g8.shape[1:]), (1, 0, 2, 3))
    sent = [by_core(g8) for g8 in sent]
    from_sibling = _pair_exchange(sent, "scatter_pair")
    summed = [_pair_add(g, o, f"pair_add_{i}") for i, (g, o) in enumerate(zip(sent, from_sibling))]
    p_w_in, p_w_o_dn, p_w_o_dil, p_w_out, p_conv = _chip_exchange(summed, "scatter_chips")
    p_small = _all_gather([_pack_small(small["norm_w"], small["final_norm_w"], small["dn_norm_w"], small["a_log"],
                                       small["dt_bias"])], "gather_small_grads")[0]

    res = {}
    res["w_in"] = [jnp.transpose(t) for t in _adamw(p_w_in, wt, m_wt, v_wt, "adamw_w_in")]
    res["conv_w"] = _adamw(p_conv, conv_w[0], m_conv_w[0], v_conv_w[0], "adamw_conv_w")
    res["w_o_dn"] = _adamw(p_w_o_dn, w_o_dn[0], m_w_o_dn[0], v_w_o_dn[0], "adamw_w_o_dn")
    res["w_o_dil"] = _adamw(p_w_o_dil, w_o_dil[0], m_w_o_dil[0], v_w_o_dil[0], "adamw_w_o_dil")
    res["w_out"] = _adamw(p_w_out, w_out[0], m_w_out[0], v_w_out[0], "adamw_w_out")
    small_res = _adamw(p_small, _pack_small(norm_w, final_norm_w, dn_norm_w, a_log, dt_bias),
                       _pack_small(m_norm_w, m_final_norm_w, m_dn_norm_w, m_a_log, m_dt_bias),
                       _pack_small(v_norm_w, v_final_norm_w, v_dn_norm_w, v_a_log, v_dt_bias), "adamw_small")
    small_res = [_unpack_small(t) for t in small_res]

    loss = lax.psum(loss11[0, 0], ("x", "y", "c"))
    names = ["norm_w", "w_in", "conv_w", "a_log", "dt_bias", "dn_norm_w", "w_o_dn", "w_o_dil", "w_out", "final_norm_w"]
    outs = [loss, grad_x[None]]
    for kind in range(4):
        for nm in names:
            outs.append(res[nm][kind][None] if nm in res else small_res[kind][nm])
    return tuple(outs)
```

```python
import math

import jax
import jax.numpy as jnp
from jax import lax
from jax.experimental import pallas as pl
from jax.experimental.pallas import tpu as pltpu

F32 = jnp.float32
MXU = jnp.bfloat16
MESH = pl.DeviceIdType.MESH

N_DEV = 8
D_MODEL = 1024
DN_HEADS = 8
DN_DK = 128
DN_CHUNK = 64
N_DIL = 3
DIL_HEADS = 4
DIL_DH = 128
DIL_W = DIL_HEADS * DIL_DH
DIL_GROUPS = ((128, 1), (512, 4), (2048, 16))
ATT_BLOCK = 128
NORM_EPS = 1e-6
QKV_W = 3 * D_MODEL
DILQ_W = N_DIL * DIL_W
PROJ_SIZES = (1024, 1024, 1024, 1024, 8, 8, DILQ_W, DILQ_W, DILQ_W, DIL_W, D_MODEL, D_MODEL)

ADAM_LR = 0.001
ADAM_B1 = 0.9
ADAM_B2 = 0.999
ADAM_EPS = 1e-08
ADAM_WD = 0.01
ADAM_STEP = 10

ROW_TILE = 256
LANES = 128
SUBLANES = 8
VMEM_LIMIT = 48 << 20


def _pcall(body, **kw):
    return pl.pallas_call(body, **kw)


def _params(*sem):
    return pltpu.CompilerParams(dimension_semantics=tuple(sem), vmem_limit_bytes=VMEM_LIMIT)


def _sigmoid(x):
    return 1.0 / (1.0 + jnp.exp(-x))


def _softplus(x):
    return jnp.maximum(x, 0.0) + jnp.log(1.0 + jnp.exp(-jnp.abs(x)))


def _dot(a, b):
    return jnp.dot(a.astype(MXU), b.astype(MXU), preferred_element_type=F32)


def _dot_nt(a, b):
    return lax.dot_general(a.astype(MXU), b.astype(MXU), (((1,), (1,)), ((), ())), preferred_element_type=F32)


def _split3(x):
    hi = x.astype(jnp.bfloat16)
    r1 = x - hi.astype(F32)
    mid = r1.astype(jnp.bfloat16)
    lo = (r1 - mid.astype(F32)).astype(jnp.bfloat16)
    return hi, mid, lo


def _dot01(m01, x):
    m = m01.astype(jnp.bfloat16)
    hi, mid, lo = _split3(x)
    f = lambda p: jnp.dot(m, p, preferred_element_type=F32)
    return f(hi) + (f(mid) + f(lo))


def _rows_call(body, name, n_rows, ins, outs, scratch=(), tm=ROW_TILE):
    steps = n_rows // tm
    per8 = tm // SUBLANES
    last8 = n_rows // SUBLANES - 1
    in_specs = []
    for arr, kind in ins:
        cols = arr.shape[-1]
        if kind == "tile":
            in_specs.append(pl.BlockSpec((tm, cols), lambda i: (i, 0)))
        elif kind == "full":
            in_specs.append(pl.BlockSpec(arr.shape, lambda i, nd=arr.ndim: (0,) * nd))
        elif kind == "prev8":
            in_specs.append(pl.BlockSpec((SUBLANES, cols), lambda i: (jnp.maximum(i * per8 - 1, 0), 0)))
        elif kind == "next8":
            in_specs.append(pl.BlockSpec((SUBLANES, cols), lambda i: (jnp.minimum((i + 1) * per8, last8), 0)))
        else:
            raise ValueError(kind)
    out_specs, out_shape, has_acc = [], [], False
    for shape, dtype, kind in outs:
        out_shape.append(jax.ShapeDtypeStruct(shape, dtype))
        if kind == "tile":
            out_specs.append(pl.BlockSpec((tm, shape[-1]), lambda i: (i, 0)))
        else:
            has_acc = True
            out_specs.append(pl.BlockSpec(shape, lambda i: (0, 0)))
    return _pcall(
        body, name=name, grid=(steps,), in_specs=in_specs, out_specs=out_specs, out_shape=out_shape,
        scratch_shapes=list(scratch),
        compiler_params=_params("arbitrary" if has_acc else "parallel"),
    )(*[a for a, _ in ins])


def _acc_add(ref, value):
    @pl.when(pl.program_id(0) == 0)
    def _():
        ref[...] = jnp.zeros_like(ref)
    ref[...] += value


def _col_chunks(n, width=512):
    return [(c, min(width, n - c)) for c in range(0, n, width)]


NT_DIMS = (((1,), (1,)), ((), ()))
TN_DIMS = (((0,), (0,)), ((), ()))


def _mm_out(a, ws, name, w_is_out_by_in=False, out_dtype=F32, tm=ROW_TILE):
    m, k = a.shape
    ns = [w.shape[0] if w_is_out_by_in else w.shape[1] for w in ws]

    def body(a_ref, *refs):
        av = a_ref[...]
        for w_ref, o_ref, n in zip(refs[:len(ws)], refs[len(ws):], ns):
            for c, wd in _col_chunks(n):
                if w_is_out_by_in:
                    part = lax.dot_general(av, w_ref[c:c + wd, :], NT_DIMS, preferred_element_type=F32)
                else:
                    part = jnp.dot(av, w_ref[:, c:c + wd], preferred_element_type=F32)
                o_ref[:, c:c + wd] = part.astype(o_ref.dtype)

    return _pcall(
        body, name=name, grid=(m // tm,),
        in_specs=[pl.BlockSpec((tm, k), lambda i: (i, 0))] + [pl.BlockSpec(w.shape, lambda i: (0, 0)) for w in ws],
        out_specs=[pl.BlockSpec((tm, n), lambda i: (i, 0)) for n in ns],
        out_shape=[jax.ShapeDtypeStruct((m, n), out_dtype) for n in ns],
        compiler_params=_params("parallel"),
    )(a, *ws)


def _mm_in(ds, ws, name, w_is_out_by_in=False, tm=ROW_TILE):
    m = ds[0].shape[0]
    k = ws[0].shape[1] if w_is_out_by_in else ws[0].shape[0]
    ns = [d.shape[1] for d in ds]

    def body(*refs):
        d_refs, w_refs, o_ref = refs[:len(ds)], refs[len(ds):2 * len(ds)], refs[-1]
        first = True
        for d_ref, w_ref, n in zip(d_refs, w_refs, ns):
            for c, wd in _col_chunks(n, 1024):
                if w_is_out_by_in:
                    part = jnp.dot(d_ref[:, c:c + wd], w_ref[c:c + wd, :], preferred_element_type=F32)
                else:
                    part = lax.dot_general(d_ref[:, c:c + wd], w_ref[:, c:c + wd], NT_DIMS, preferred_element_type=F32)
                if first:
                    o_ref[...] = part
                    first = False
                else:
                    o_ref[...] += part

    return _pcall(
        body, name=name, grid=(m // tm,),
        in_specs=[pl.BlockSpec((tm, n), lambda i: (i, 0)) for n in ns] + [pl.BlockSpec(w.shape, lambda i: (0, 0)) for w in ws],
        out_specs=pl.BlockSpec((tm, k), lambda i: (i, 0)),
        out_shape=jax.ShapeDtypeStruct((m, k), F32),
        compiler_params=_params("parallel"),
    )(*ds, *ws)


def _mm_tn(a, d, name, tm=512):
    m, k = a.shape
    n = d.shape[1]
    tile = lambda t: 1024 if t % 1024 == 0 else (512 if t % 512 == 0 else t)
    tk, tn = tile(k), tile(n)

    def body(a_ref, d_ref, o_ref):
        @pl.when(pl.program_id(2) == 0)
        def _():
            o_ref[...] = jnp.zeros_like(o_ref)
        o_ref[...] += lax.dot_general(a_ref[...], d_ref[...], TN_DIMS, preferred_element_type=F32)

    return _pcall(
        body, name=name, grid=(k // tk, n // tn, m // tm),
        in_specs=[pl.BlockSpec((tm, tk), lambda p, i, j: (j, p)), pl.BlockSpec((tm, tn), lambda p, i, j: (j, i))],
        out_specs=pl.BlockSpec((tk, tn), lambda p, i, j: (p, i)),
        out_shape=jax.ShapeDtypeStruct((k, n), F32),
        compiler_params=_params("parallel", "parallel", "arbitrary"),
    )(a, d)


def _rms_in_fwd(x, norm_w):
    def body(x_ref, w_ref, h_ref):
        xv = x_ref[...]
        r = lax.rsqrt(jnp.mean(xv * xv, axis=-1, keepdims=True) + NORM_EPS)
        h_ref[...] = (xv * r * w_ref[...]).astype(h_ref.dtype)

    return _rows_call(body, "rms_in_fwd", x.shape[0], [(x, "tile"), (norm_w, "full")],
                      [(x.shape, MXU, "tile")])[0]


def _conv_taps(ext_ref, cw_ref, cols, tm):
    c = None
    for j in range(4):
        term = cw_ref[3 - j:4 - j, cols] * ext_ref[SUBLANES - j:SUBLANES - j + tm, cols]
        c = term if c is None else c + term
    return c


def _fill_ext(ext_ref, u_ref, halo_ref, first):
    ext_ref[0:SUBLANES, :] = jnp.where(first, 0.0, halo_ref[...])
    ext_ref[SUBLANES:, :] = u_ref[...]


def _dn_prep_fwd(qkv_pre, ba, conv_w8, alog_row, dtb_row):
    s = qkv_pre.shape[0]
    tm = ROW_TILE

    def body(u_ref, halo_ref, cw_ref, ba_ref, al_ref, dtb_ref, q_ref, k_ref, v_ref, bg_ref, ext_ref):
        _fill_ext(ext_ref, u_ref, halo_ref, pl.program_id(0) == 0)
        for h in range(3 * DN_HEADS):
            cols = slice(h * LANES, (h + 1) * LANES)
            c = _conv_taps(ext_ref, cw_ref, cols, tm)
            a = c * _sigmoid(c)
            oc = slice((h % DN_HEADS) * LANES, (h % DN_HEADS + 1) * LANES)
            if h < 2 * DN_HEADS:
                rinv = lax.rsqrt(jnp.sum(a * a, axis=-1, keepdims=True) + NORM_EPS)
                if h < DN_HEADS:
                    q_ref[:, oc] = a * (rinv * DN_DK ** -0.5)
                else:
                    k_ref[:, oc] = a * rinv
            else:
                v_ref[:, oc] = a
        bav = ba_ref[...]
        lane = lax.broadcasted_iota(jnp.int32, bav.shape, 1)
        beta = _sigmoid(bav)
        g = -jnp.exp(al_ref[...]) * _softplus(bav + dtb_ref[...])
        bg_ref[...] = jnp.where(lane < DN_HEADS, beta, jnp.where(lane < 2 * DN_HEADS, g, 0.0))

    return _rows_call(
        body, "dn_prep_fwd", s,
        [(qkv_pre, "tile"), (qkv_pre, "prev8"), (conv_w8, "full"), (ba, "tile"), (alog_row, "full"), (dtb_row, "full")],
        [((s, D_MODEL), F32, "tile")] * 3 + [((s, LANES), F32, "tile")],
        scratch=[pltpu.VMEM((tm + SUBLANES, QKV_W), F32)])


def _dn_prep_bwd(qkv_pre, ba, conv_w8, alog_row, dtb_row, dq, dk, dv, dbg):
    s = qkv_pre.shape[0]
    tm = ROW_TILE

    def body(u_ref, halo_ref, cw_ref, ba_ref, al_ref, dtb_ref, dq_ref, dk_ref, dv_ref, dbg_ref,
             dc_ref, dba_ref, dsmall_ref, ext_ref):
        _fill_ext(ext_ref, u_ref, halo_ref, pl.program_id(0) == 0)
        for h in range(3 * DN_HEADS):
            cols = slice(h * LANES, (h + 1) * LANES)
            oc = slice((h % DN_HEADS) * LANES, (h % DN_HEADS + 1) * LANES)
            c = _conv_taps(ext_ref, cw_ref, cols, tm)
            sg = _sigmoid(c)
            a = c * sg
            if h < 2 * DN_HEADS:
                rinv = lax.rsqrt(jnp.sum(a * a, axis=-1, keepdims=True) + NORM_EPS)
                dy = dq_ref[:, oc] * DN_DK ** -0.5 if h < DN_HEADS else dk_ref[:, oc]
                da = rinv * dy - a * (rinv * rinv * rinv) * jnp.sum(dy * a, axis=-1, keepdims=True)
            else:
                da = dv_ref[:, oc]
            dc_ref[:, cols] = da * (sg * (1.0 + c * (1.0 - sg)))
        bav = ba_ref[...]
        dbgv = dbg_ref[...]
        lane = lax.broadcasted_iota(jnp.int32, bav.shape, 1)
        beta = _sigmoid(bav)
        ea = jnp.exp(al_ref[...])
        z = bav + dtb_ref[...]
        g = -ea * _softplus(z)
        is_b = lane < DN_HEADS
        is_g = jnp.logical_and(lane >= DN_HEADS, lane < 2 * DN_HEADS)
        d_aa = jnp.where(is_g, dbgv * (-ea) * _sigmoid(z), 0.0)
        dba = jnp.where(is_b, dbgv * beta * (1.0 - beta), d_aa)
        dba_ref[...] = dba.astype(dba_ref.dtype)
        r_alog = jnp.sum(jnp.where(is_g, dbgv * g, 0.0), axis=0, keepdims=True)
        r_dtb = jnp.sum(d_aa, axis=0, keepdims=True)
        _acc_add(dsmall_ref, jnp.concatenate([r_alog, r_dtb, jnp.zeros((SUBLANES - 2, LANES), F32)], axis=0))

    return _rows_call(
        body, "dn_prep_bwd", s,
        [(qkv_pre, "tile"), (qkv_pre, "prev8"), (conv_w8, "full"), (ba, "tile"), (alog_row, "full"), (dtb_row, "full"),
         (dq, "tile"), (dk, "tile"), (dv, "tile"), (dbg, "tile")],
        [((s, QKV_W), F32, "tile"), ((s, LANES), MXU, "tile"), ((SUBLANES, LANES), F32, "acc")],
        scratch=[pltpu.VMEM((tm + SUBLANES, QKV_W), F32)])


def _conv_bwd(dc, qkv_pre, conv_w8):
    s = dc.shape[0]
    tm = ROW_TILE
    steps = s // tm

    def body(dc_ref, dnext_ref, u_ref, halo_ref, cw_ref, du_ref, dcw_ref, extd_ref, ext_ref):
        i = pl.program_id(0)
        _fill_ext(ext_ref, u_ref, halo_ref, i == 0)
        extd_ref[0:tm, :] = dc_ref[...]
        extd_ref[tm:, :] = jnp.where(i == steps - 1, 0.0, dnext_ref[...])

        @pl.when(i == 0)
        def _():
            dcw_ref[...] = jnp.zeros_like(dcw_ref)

        for h in range(3 * DN_HEADS):
            cols = slice(h * LANES, (h + 1) * LANES)
            du = None
            for j in range(4):
                term = cw_ref[3 - j:4 - j, cols] * extd_ref[j:j + tm, cols]
                du = term if du is None else du + term
            du_ref[:, cols] = du.astype(du_ref.dtype)
            dcv = dc_ref[:, cols]
            for j in range(4):
                row = jnp.sum(dcv * ext_ref[SUBLANES - j:SUBLANES - j + tm, cols], axis=0, keepdims=True)
                dcw_ref[3 - j:4 - j, cols] += row

    return _rows_call(
        body, "conv_bwd", s,
        [(dc, "tile"), (dc, "next8"), (qkv_pre, "tile"), (qkv_pre, "prev8"), (conv_w8, "full")],
        [((s, QKV_W), MXU, "tile"), ((SUBLANES, QKV_W), F32, "acc")],
        scratch=[pltpu.VMEM((tm + SUBLANES, QKV_W), F32), pltpu.VMEM((tm + SUBLANES, QKV_W), F32)])


def _dn_out_fwd(o, z, dnw_row, w_o_dn):
    def body(o_ref, z_ref, w_ref, wo_ref, on_ref, y_ref):
        for h in range(DN_HEADS):
            cols = slice(h * LANES, (h + 1) * LANES)
            ov = o_ref[:, cols]
            zv = z_ref[:, cols]
            ro = lax.rsqrt(jnp.mean(ov * ov, axis=-1, keepdims=True) + NORM_EPS)
            on_ref[:, cols] = (ov * ro * w_ref[...] * (zv * _sigmoid(zv))).astype(on_ref.dtype)
        y_ref[...] = jnp.dot(on_ref[...], wo_ref[...], preferred_element_type=F32)

    return _rows_call(body, "dn_out_fwd", o.shape[0], [(o, "tile"), (z, "tile"), (dnw_row, "full"), (w_o_dn, "full")],
                      [(o.shape, MXU, "tile"), ((o.shape[0], w_o_dn.shape[1]), F32, "tile")])


def _dn_out_bwd(dy, o, z, dnw_row, w_o_dn):
    def body(dy_ref, o_ref, z_ref, w_ref, wo_ref, do_ref, dz_ref, dw_ref, d_ref):
        d_ref[...] = lax.dot_general(dy_ref[...], wo_ref[...], NT_DIMS, preferred_element_type=F32)
        acc = jnp.zeros((1, LANES), F32)
        for h in range(DN_HEADS):
            cols = slice(h * LANES, (h + 1) * LANES)
            dv, ov, zv = d_ref[:, cols], o_ref[:, cols], z_ref[:, cols]
            sg = _sigmoid(zv)
            sz = zv * sg
            ro = lax.rsqrt(jnp.mean(ov * ov, axis=-1, keepdims=True) + NORM_EPS)
            nv = ov * ro
            dn = dv * w_ref[...] * sz
            acc = acc + jnp.sum(dv * nv * sz, axis=0, keepdims=True)
            dz_ref[:, cols] = (dv * nv * w_ref[...] * (sg * (1.0 + zv * (1.0 - sg)))).astype(dz_ref.dtype)
            do_ref[:, cols] = ro * dn - ov * (ro * ro * ro) * jnp.mean(dn * ov, axis=-1, keepdims=True)
        _acc_add(dw_ref, jnp.concatenate([acc, jnp.zeros((SUBLANES - 1, LANES), F32)], axis=0))

    return _rows_call(body, "dn_out_bwd", o.shape[0],
                      [(dy, "tile"), (o, "tile"), (z, "tile"), (dnw_row, "full"), (w_o_dn, "full")],
                      [(o.shape, F32, "tile"), (o.shape, MXU, "tile"), ((SUBLANES, LANES), F32, "acc")],
                      scratch=[pltpu.VMEM((ROW_TILE, o.shape[1]), F32)])


def _attn_out_fwd(parts, lses, zb, w_o_dil):
    def body(o0, o1, o2, l0, l1, l2, z_ref, wo_ref, lse_ref, o_ref, g_ref, y_ref):
        a, b, c = l0[...], l1[...], l2[...]
        m = jnp.maximum(a, jnp.maximum(b, c))
        ea, eb, ec = jnp.exp(a - m), jnp.exp(b - m), jnp.exp(c - m)
        den = ea + eb + ec
        out = (ea * o0[...] + eb * o1[...] + ec * o2[...]) / den
        lse_ref[...] = m + jnp.log(den)
        o_ref[...] = out
        zv = z_ref[...]
        gated = (out * (zv * _sigmoid(zv))).astype(g_ref.dtype)
        g_ref[...] = gated
        y_ref[...] = jnp.dot(gated, wo_ref[...], preferred_element_type=F32)

    s = zb.shape[0]
    return _rows_call(body, "attn_out_fwd", s,
                      [(p, "tile") for p in parts] + [(l, "tile") for l in lses] + [(zb, "tile"), (w_o_dil, "full")],
                      [((s, DIL_W), F32, "tile"), ((s, DIL_W), F32, "tile"), ((s, DIL_W), MXU, "tile"),
                       ((s, w_o_dil.shape[1]), F32, "tile")])


def _attn_out_bwd(dy, o_joint, zb, w_o_dil):
    def body(dy_ref, o_ref, z_ref, wo_ref, do_ref, dz_ref, dl_ref):
        zv = z_ref[...]
        sg = _sigmoid(zv)
        dv = lax.dot_general(dy_ref[...], wo_ref[...], NT_DIMS, preferred_element_type=F32)
        ov = o_ref[...]
        do = dv * (zv * sg)
        do_ref[...] = do
        dz_ref[...] = (dv * ov * (sg * (1.0 + zv * (1.0 - sg)))).astype(dz_ref.dtype)
        for h in range(DIL_HEADS):
            cols = slice(h * LANES, (h + 1) * LANES)
            dl_ref[:, cols] = jnp.broadcast_to(jnp.sum(do[:, cols] * ov[:, cols], axis=-1, keepdims=True),
                                               (do.shape[0], LANES))

    s = zb.shape[0]
    return _rows_call(body, "attn_out_bwd", s, [(dy, "tile"), (o_joint, "tile"), (zb, "tile"), (w_o_dil, "full")],
                      [((s, DIL_W), F32, "tile"), ((s, DIL_W), MXU, "tile"), ((s, DIL_W), F32, "tile")])


def _merge_out_final(ga, gb, ya, yb, x, target, w_out, wf_row):
    s, dm = x.shape

    def body(ga_ref, gb_ref, ya_ref, yb_ref, x_ref, t_ref, wo_ref, w_ref,
             loss_ref, dw_ref, m_ref, dxb_ref, dx_ref, dya_ref, dyb_ref, dga_ref, dgb_ref):
        sa, sb = _sigmoid(ga_ref[...]), _sigmoid(gb_ref[...])
        ya, yb = ya_ref[...], yb_ref[...]
        merged = (sa * ya + sb * yb).astype(MXU)
        m_ref[...] = merged
        x2 = x_ref[...] + jnp.dot(merged, wo_ref[...], preferred_element_type=F32)
        r = lax.rsqrt(jnp.mean(x2 * x2, axis=-1, keepdims=True) + NORM_EPS)
        w = w_ref[...]
        err = x2 * r * w - t_ref[...]
        tile_loss = 0.5 * jnp.sum(jnp.mean(err * err, axis=-1, keepdims=True), axis=0, keepdims=True)
        _acc_add(loss_ref, jnp.broadcast_to(tile_loss, (SUBLANES, LANES)))
        dy = err * (1.0 / dm)
        row = jnp.sum(dy * x2 * r, axis=0, keepdims=True)
        _acc_add(dw_ref, jnp.concatenate([row, jnp.zeros((SUBLANES - 1, dm), F32)], axis=0))
        dn = dy * w
        dx2 = r * dn - x2 * (r * r * r) * jnp.mean(dn * x2, axis=-1, keepdims=True)
        dx_ref[...] = dx2
        dxb = dx2.astype(MXU)
        dxb_ref[...] = dxb
        dmv = lax.dot_general(dxb, wo_ref[...], NT_DIMS, preferred_element_type=F32)
        dya_ref[...] = (dmv * sa).astype(dya_ref.dtype)
        dyb_ref[...] = (dmv * sb).astype(dyb_ref.dtype)
        dga_ref[...] = (dmv * ya * sa * (1.0 - sa)).astype(dga_ref.dtype)
        dgb_ref[...] = (dmv * yb * sb * (1.0 - sb)).astype(dgb_ref.dtype)

    return _rows_call(body, "merge_out_final", s,
                      [(ga, "tile"), (gb, "tile"), (ya, "tile"), (yb, "tile"), (x, "tile"), (target, "tile"),
                       (w_out, "full"), (wf_row, "full")],
                      [((SUBLANES, LANES), F32, "acc"), ((SUBLANES, dm), F32, "acc"), ((s, dm), MXU, "tile"),
                       ((s, dm), MXU, "tile"), ((s, dm), F32, "tile")] + [((s, dm), MXU, "tile")] * 4)


def _lane_pick(x, idx):
    lane = lax.broadcasted_iota(jnp.int32, x.shape, 1)
    return jnp.sum(jnp.where(lane == idx, x, 0.0), axis=-1, keepdims=True)


PAIR = 2 * DN_CHUNK
SCAN_CHUNKS = 4


def _bmm(a, b):
    return lax.dot_general(a.astype(MXU), b.astype(MXU), (((2,), (1,)), ((0,), (0,))), preferred_element_type=F32)


def _bmm_nt(a, b):
    return lax.dot_general(a.astype(MXU), b.astype(MXU), (((2,), (2,)), ((0,), (0,))), preferred_element_type=F32)


def _bmm_tn(a, b):
    return lax.dot_general(a.astype(MXU), b.astype(MXU), (((1,), (1,)), ((0,), (0,))), preferred_element_type=F32)


def _bmm3(a, b):
    ah = a.astype(jnp.bfloat16)
    al = (a - ah.astype(F32)).astype(jnp.bfloat16)
    bh = b.astype(jnp.bfloat16)
    bl = (b - bh.astype(F32)).astype(jnp.bfloat16)
    f = lambda p, q: lax.dot_general(p, q, (((2,), (1,)), ((0,), (0,))), preferred_element_type=F32)
    return f(ah, bh) + (f(ah, bl) + f(al, bh))


def _pair_masks():
    row = lax.broadcasted_iota(jnp.int32, (PAIR, PAIR), 0)
    col = lax.broadcasted_iota(jnp.int32, (PAIR, PAIR), 1)
    same = (row >= DN_CHUNK) == (col >= DN_CHUNK)
    return dict(causal=same & (row >= col), strict=same & (row > col), upper=same & (row <= col), eye=row == col,
                first=row < DN_CHUNK, row=row, lane=col)


def _pair_decay(bgv, masks):
    gc_all = _dot01(masks["causal"].astype(F32), bgv)
    out = []
    for h in range(DN_HEADS):
        beta = _lane_pick(bgv, h)
        gcb = jnp.broadcast_to(_lane_pick(gc_all, DN_HEADS + h), (PAIR, PAIR))
        gam = jnp.where(masks["causal"], jnp.exp(jnp.minimum(gcb - gcb.T, 0.0)), 0.0)
        gl = jnp.where(masks["first"], gcb[DN_CHUNK - 1:DN_CHUNK, :], gcb[PAIR - 1:PAIR, :])
        out.append((beta, gcb, gam, gl))
    return out


def _pair_inverse(a_strict, eye):
    n = -a_strict
    t = eye.astype(F32)[None] + n
    p = n
    for _ in range(int(math.log2(DN_CHUNK)) - 1):
        p = _bmm3(p, p)
        t = t + _bmm3(t, p)
    return t


def _head_cols(h):
    return slice(h * LANES, (h + 1) * LANES)


def _delta_prep(q, k, v, bg):
    s = q.shape[0]
    c = DN_CHUNK
    n_chunks = s // c

    def body(q_ref, k_ref, v_ref, bg_ref, u_ref, w_ref, qd_ref, kd_ref, aqk_ref, dl_ref, t2_ref):
        masks = _pair_masks()
        dec = _pair_decay(bg_ref[...], masks)
        kbs, ks, gams, vbs, kbes, qs, qds, kds, dls = ([] for _ in range(9))
        for h in range(DN_HEADS):
            beta, gcb, gam, gl = dec[h]
            qh, kh, vh = q_ref[:, _head_cols(h)], k_ref[:, _head_cols(h)], v_ref[:, _head_cols(h)]
            eg = jnp.exp(gcb)
            kb = kh * beta
            kbs.append(kb); ks.append(kh); gams.append(gam); vbs.append(vh * beta); kbes.append(kb * eg)
            qs.append(qh); qds.append(qh * eg); kds.append(kh * jnp.exp(gl - gcb)); dls.append(jnp.exp(gl))
        st = lambda xs: jnp.stack(xs, axis=0)
        kmat, gam = st(ks), st(gams)
        a = jnp.where(masks["strict"][None], _bmm_nt(st(kbs), kmat) * gam, 0.0)
        t = _pair_inverse(a, masks["eye"])
        u = _bmm(t, st(vbs))
        w = _bmm(t, st(kbes))
        aqk = _bmm_nt(st(qs), kmat) * gam
        t2_ref[0] = t.astype(t2_ref.dtype)
        for half in range(2):
            rows = slice(half * c, (half + 1) * c)
            u_ref[half] = u[:, rows, :]
            w_ref[half] = w[:, rows, :].astype(w_ref.dtype)
            qd_ref[half] = st(qds)[:, rows, :].astype(qd_ref.dtype)
            kd_ref[half] = st(kds)[:, rows, :].astype(kd_ref.dtype)
            aqk_ref[half] = aqk[:, rows, rows].astype(aqk_ref.dtype)
            dl_ref[half] = st(dls)[:, half * c:half * c + SUBLANES, :]

    row_spec = lambda w_: pl.BlockSpec((PAIR, w_), lambda i: (i, 0))
    hm = lambda a_, b_: pl.BlockSpec((2, DN_HEADS, a_, b_), lambda i: (i, 0, 0, 0))
    hm_shape = lambda a_, b_, dt: jax.ShapeDtypeStruct((n_chunks, DN_HEADS, a_, b_), dt)
    return _pcall(
        body, name="delta_prep", grid=(n_chunks // 2,),
        in_specs=[row_spec(D_MODEL)] * 3 + [row_spec(LANES)],
        out_specs=[hm(c, LANES)] * 4 + [hm(c, c), hm(SUBLANES, LANES),
                   pl.BlockSpec((1, DN_HEADS, PAIR, PAIR), lambda i: (i, 0, 0, 0))],
        out_shape=[hm_shape(c, LANES, F32), hm_shape(c, LANES, MXU), hm_shape(c, LANES, MXU), hm_shape(c, LANES, MXU),
                   hm_shape(c, c, MXU), hm_shape(SUBLANES, LANES, F32),
                   jax.ShapeDtypeStruct((n_chunks // 2, DN_HEADS, PAIR, PAIR), MXU)],
        compiler_params=_params("parallel"),
    )(q, k, v, bg)


def _delta_scan_fwd(u, w, qd, kd, aqk, dl):
    n_chunks = u.shape[0]
    c = DN_CHUNK
    g_n = SCAN_CHUNKS

    def body(u_ref, w_ref, qd_ref, kd_ref, aqk_ref, dl_ref, o_ref, vnew_ref, st_ref, state):
        @pl.when(pl.program_id(0) == 0)
        def _():
            state[...] = jnp.zeros_like(state)

        for g in range(g_n):
            sv = state[...]
            sb = sv.astype(MXU)
            vnew = u_ref[g] - _bmm(w_ref[g], sb)
            o = _bmm(qd_ref[g], sb) + _bmm(aqk_ref[g], vnew)
            state[...] = sv * dl_ref[g][:, 0:1, :] + _bmm_tn(kd_ref[g], vnew)
            vnew_ref[g] = vnew.astype(vnew_ref.dtype)
            st_ref[g] = sb
            for h in range(DN_HEADS):
                o_ref[g * c:(g + 1) * c, _head_cols(h)] = o[h]

    hm = lambda a_, b_: pl.BlockSpec((g_n, DN_HEADS, a_, b_), lambda i: (i, 0, 0, 0))
    return _pcall(
        body, name="delta_scan_fwd", grid=(n_chunks // g_n,),
        in_specs=[hm(c, LANES)] * 4 + [hm(c, c), hm(SUBLANES, LANES)],
        out_specs=[pl.BlockSpec((g_n * c, D_MODEL), lambda i: (i, 0)), hm(c, LANES), hm(DN_DK, DN_DK)],
        out_shape=[jax.ShapeDtypeStruct((n_chunks * c, D_MODEL), F32),
                   jax.ShapeDtypeStruct((n_chunks, DN_HEADS, c, LANES), MXU),
                   jax.ShapeDtypeStruct((n_chunks, DN_HEADS, DN_DK, DN_DK), MXU)],
        scratch_shapes=[pltpu.VMEM((DN_HEADS, DN_DK, DN_DK), F32)],
        compiler_params=_params("arbitrary"),
    )(u, w, qd, kd, aqk, dl)


def _delta_scan_bwd(w, qd, kd, aqk, dl, vnew, st, do):
    n_chunks = w.shape[0]
    c = DN_CHUNK
    g_n = SCAN_CHUNKS
    steps = n_chunks // g_n

    def body(w_ref, qd_ref, kd_ref, aqk_ref, dl_ref, vnew_ref, st_ref, do_ref, dvnew_ref, dkd_ref, ddl_ref, dstate):
        @pl.when(pl.program_id(0) == 0)
        def _():
            dstate[...] = jnp.zeros_like(dstate)

        for g in reversed(range(g_n)):
            ds = dstate[...]
            dsb = ds.astype(MXU)
            doh = jnp.stack([do_ref[g * c:(g + 1) * c, _head_cols(h)] for h in range(DN_HEADS)], axis=0)
            dvnew = _bmm_tn(aqk_ref[g], doh) + _bmm(kd_ref[g], dsb)
            dkd_ref[g] = _bmm_nt(vnew_ref[g], dsb)
            ddl = jnp.sum(jnp.sum(st_ref[g].astype(F32) * ds, axis=2, keepdims=True), axis=1, keepdims=True)
            ddl_ref[g] = jnp.broadcast_to(ddl, (DN_HEADS, SUBLANES, LANES))
            dstate[...] = ds * dl_ref[g][:, 0:1, :] + _bmm_tn(qd_ref[g], doh) - _bmm_tn(w_ref[g], dvnew)
            dvnew_ref[g] = dvnew.astype(dvnew_ref.dtype)

    rev = lambda i: steps - 1 - i
    hm = lambda a_, b_: pl.BlockSpec((g_n, DN_HEADS, a_, b_), lambda i: (rev(i), 0, 0, 0))
    return _pcall(
        body, name="delta_scan_bwd", grid=(steps,),
        in_specs=[hm(c, LANES)] * 3 + [hm(c, c), hm(SUBLANES, LANES), hm(c, LANES), hm(DN_DK, DN_DK),
                  pl.BlockSpec((g_n * c, D_MODEL), lambda i: (rev(i), 0))],
        out_specs=[hm(c, LANES), hm(c, LANES), hm(SUBLANES, LANES)],
        out_shape=[jax.ShapeDtypeStruct((n_chunks, DN_HEADS, c, LANES), MXU),
                   jax.ShapeDtypeStruct((n_chunks, DN_HEADS, c, LANES), F32),
                   jax.ShapeDtypeStruct((n_chunks, DN_HEADS, SUBLANES, LANES), F32)],
        scratch_shapes=[pltpu.VMEM((DN_HEADS, DN_DK, DN_DK), F32)],
        compiler_params=_params("arbitrary"),
    )(w, qd, kd, aqk, dl, vnew, st, do)


def _delta_post_bwd(q, k, v, bg, t2, st, vnew, do, dvnew, dkd, ddl):
    s = q.shape[0]
    c = DN_CHUNK

    def body(q_ref, k_ref, v_ref, bg_ref, t2_ref, st_ref, vnew_ref, do_ref, dvnew_ref, dkd_ref, ddl_ref,
             dq_ref, dk_ref, dv_ref, dbg_ref):
        masks = _pair_masks()
        first = masks["first"][None]
        dec = _pair_decay(bg_ref[...], masks)
        st_ = lambda xs: jnp.stack(xs, axis=0)
        heads = range(DN_HEADS)
        qm_, km_, vm_, dom = (st_([r[:, _head_cols(h)] for h in heads]) for r in (q_ref, k_ref, v_ref, do_ref))
        beta = st_([dec[h][0] for h in heads])
        gcb = st_([dec[h][1] for h in heads])
        gam = st_([dec[h][2] for h in heads])
        gl = st_([dec[h][3] for h in heads])
        pair = lambda ref: jnp.concatenate([ref[0], ref[1]], axis=1)
        vnew2, dvnew2, dkd2 = pair(vnew_ref), pair(dvnew_ref), pair(dkd_ref)
        halves = lambda x: (x[:, :c, :], x[:, c:, :])
        by_state = lambda x: jnp.concatenate([_bmm_nt(xh, st_ref[i]) for i, xh in enumerate(halves(x))], axis=1)
        dqd = by_state(dom)
        dw = -by_state(dvnew2)
        ddl2 = jnp.where(first, ddl_ref[0][:, 0:1, :], ddl_ref[1][:, 0:1, :])

        eg = jnp.exp(gcb)
        egl = jnp.exp(gl - gcb)
        dl = jnp.exp(gl)
        kb = km_ * beta
        kk = _bmm_nt(kb, km_)
        a = jnp.where(masks["strict"][None], kk * gam, 0.0)
        t = t2_ref[0]
        vb = vm_ * beta
        kbe = kb * eg
        u = _bmm(t, vb)
        w = _bmm(t, kbe)
        aqk = _bmm_nt(qm_, km_) * gam
        qd = qm_ * eg
        kd = km_ * egl

        daqk = jnp.where(masks["causal"][None], _bmm_nt(dom, vnew2), 0.0)
        dvb = _bmm_tn(t, dvnew2)
        dkbe = _bmm_tn(t, dw)
        da = jnp.where(masks["strict"][None], -(_bmm_nt(dvb, u) + _bmm_nt(dkbe, w)), 0.0)
        pm = da * gam
        qmm = daqk * gam
        dkb = _bmm(pm, km_) + dkbe * eg
        dkh = _bmm_tn(pm, kb) + _bmm_tn(qmm, qm_) + dkd2 * egl + dkb * beta
        dqh = _bmm(qmm, km_) + dqd * eg
        xm = da * a + daqk * aqk
        ones = jnp.ones((DN_HEADS, PAIR, LANES), F32)
        hi, mid, lo = _split3(xm)
        colsum = _bmm_tn(hi, ones) + (_bmm_tn(mid, ones) + _bmm_tn(lo, ones))
        tmp = jnp.sum(dkd2 * kd, axis=-1, keepdims=True)
        dgc = (jnp.sum(xm, axis=-1, keepdims=True) - colsum + jnp.sum(dkbe * kbe, axis=-1, keepdims=True)
               + jnp.sum(dqd * qd, axis=-1, keepdims=True) - tmp)
        sum0 = jnp.sum(jnp.where(first, tmp, 0.0), axis=1, keepdims=True)
        sum1 = jnp.sum(jnp.where(first, 0.0, tmp), axis=1, keepdims=True)
        dgl = jnp.where(first, sum0, sum1) + ddl2 * dl
        last = (masks["row"] == c - 1) | (masks["row"] == PAIR - 1)
        dgc = dgc + jnp.where(last[None], dgl, 0.0)
        dbeta = jnp.sum(dvb * vm_, axis=-1, keepdims=True) + jnp.sum(dkb * km_, axis=-1, keepdims=True)
        dvh = dvb * beta

        lane = masks["lane"]
        dgc_lanes = jnp.zeros((PAIR, LANES), F32)
        dbg = jnp.zeros((PAIR, LANES), F32)
        for h in heads:
            dq_ref[:, _head_cols(h)] = dqh[h]
            dk_ref[:, _head_cols(h)] = dkh[h]
            dv_ref[:, _head_cols(h)] = dvh[h]
            dgc_lanes = dgc_lanes + jnp.where(lane == DN_HEADS + h, dgc[h], 0.0)
            dbg = dbg + jnp.where(lane == h, dbeta[h], 0.0)
        dbg_ref[...] = dbg + _dot01(masks["upper"].astype(F32), dgc_lanes)

    n_pairs = s // PAIR
    row_spec = lambda w_: pl.BlockSpec((PAIR, w_), lambda i: (i, 0))
    hm = lambda a_, b_: pl.BlockSpec((2, DN_HEADS, a_, b_), lambda i: (i, 0, 0, 0))
    return _pcall(
        body, name="delta_post_bwd", grid=(n_pairs,),
        in_specs=[row_spec(D_MODEL)] * 3 + [row_spec(LANES), pl.BlockSpec((1, DN_HEADS, PAIR, PAIR), lambda i: (i, 0, 0, 0)),
                  hm(DN_DK, DN_DK), hm(c, LANES), row_spec(D_MODEL), hm(c, LANES), hm(c, LANES), hm(SUBLANES, LANES)],
        out_specs=[row_spec(D_MODEL)] * 3 + [row_spec(LANES)],
        out_shape=[jax.ShapeDtypeStruct((s, D_MODEL), F32)] * 3 + [jax.ShapeDtypeStruct((s, LANES), F32)],
        compiler_params=_params("parallel"),
    )(q, k, v, bg, t2, st, vnew, do, dvnew, dkd, ddl)


def _alibi_slope(group, head):
    n = N_DIL * DIL_HEADS
    return float(2.0 ** (-8.0 * (group * DIL_HEADS + head + 1) / n))


def _attn_plan(s, group):
    window, dil = DIL_GROUPS[group]
    assert window // dil == ATT_BLOCK
    assert (s // dil) % ATT_BLOCK == 0, "sub-sequence length must be a whole number of attention blocks"
    return dil, s // dil // ATT_BLOCK, (DIL_HEADS if dil == 1 else 1)


def _attn_specs(group, dil, nb, hp):
    rows = ATT_BLOCK * dil

    def spec(col0, shift):
        if shift < 0:
            f = lambda hb, n: (jnp.maximum(n - 1, 0), col0 + hb)
        elif shift > 0:
            f = lambda hb, n: (jnp.minimum(n + 1, nb - 1), col0 + hb)
        else:
            f = lambda hb, n: (n, col0 + hb)
        return pl.BlockSpec((rows, hp * LANES), f)

    return (lambda shift: spec(group * (DIL_HEADS // hp), shift)), (lambda shift: spec(0, shift))


def _sub_rows(ref, r, dil, cols):
    return ref[:, cols] if dil == 1 else ref[pl.ds(r, ATT_BLOCK, stride=dil), cols]


def _set_sub_rows(ref, r, dil, cols, value):
    if dil == 1:
        ref[:, cols] = value
    else:
        ref[pl.ds(r, ATT_BLOCK, stride=dil), cols] = value


def _step_slope(group, hp, hh):
    if hp == DIL_HEADS:
        return _alibi_slope(group, hh)
    hb = pl.program_id(0)
    slope = _alibi_slope(group, DIL_HEADS - 1)
    for h in reversed(range(DIL_HEADS - 1)):
        slope = jnp.where(hb == h, _alibi_slope(group, h), slope)
    return slope


def _window_bias(dil, n):
    a = lax.broadcasted_iota(jnp.int32, (ATT_BLOCK, 2 * ATT_BLOCK), 0)
    b = lax.broadcasted_iota(jnp.int32, (ATT_BLOCK, 2 * ATT_BLOCK), 1)
    dist = ATT_BLOCK + a - b
    valid = (dist >= 0) & (dist <= ATT_BLOCK) & ((b >= ATT_BLOCK) | (n > 0))
    return (dist * dil).astype(F32), valid


def _attn_fwd(qb, kb, vb, group):
    s = qb.shape[0]
    dil, nb, hp = _attn_plan(s, group)
    qkv, per_head = _attn_specs(group, dil, nb, hp)

    def body(q_ref, kp_ref, kc_ref, vp_ref, vc_ref, o_ref, lse_ref):
        n = pl.program_id(1)
        distd, valid = _window_bias(dil, n)
        for hh in range(hp):
            cols = _head_cols(hh)
            slope = _step_slope(group, hp, hh)
            for r in range(dil):
                sub = lambda ref: _sub_rows(ref, r, dil, cols).astype(MXU)
                kk = jnp.concatenate([sub(kp_ref), sub(kc_ref)], axis=0)
                vv = jnp.concatenate([sub(vp_ref), sub(vc_ref)], axis=0)
                sc = _dot_nt(sub(q_ref), kk) * DIL_DH ** -0.5 - slope * distd
                sc = jnp.where(valid, sc, -1e30)
                mx = jnp.max(sc, axis=-1, keepdims=True)
                p = jnp.where(valid, jnp.exp(sc - mx), 0.0)
                den = jnp.sum(p, axis=-1, keepdims=True)
                _set_sub_rows(o_ref, r, dil, cols, _dot(p, vv) / den)
                _set_sub_rows(lse_ref, r, dil, cols, jnp.broadcast_to(mx + jnp.log(den), (ATT_BLOCK, LANES)))

    return _pcall(
        body, name=f"attn_fwd_g{group}", grid=(DIL_HEADS // hp, nb),
        in_specs=[qkv(0), qkv(-1), qkv(0), qkv(-1), qkv(0)], out_specs=[per_head(0)] * 2,
        out_shape=[jax.ShapeDtypeStruct((s, DIL_W), F32)] * 2,
        compiler_params=_params("parallel", "parallel"),
    )(qb, kb, kb, vb, vb)


def _attn_bwd(qb, kb, vb, d_o, lse, delta, group):
    s = qb.shape[0]
    dil, nb, hp = _attn_plan(s, group)
    qkv, per_head = _attn_specs(group, dil, nb, hp)
    scale = DIL_DH ** -0.5

    def body(qc_ref, qn_ref, kp_ref, kc_ref, vp_ref, vc_ref, doc_ref, don_ref, lc_ref, ln_ref, dc_ref, dn_ref,
             dq_ref, dk_ref, dv_ref, dq_acc, dk_acc, dv_acc):
        n = pl.program_id(1)
        distd, valid = _window_bias(dil, n)
        bk = lax.broadcasted_iota(jnp.int32, (ATT_BLOCK, 2 * ATT_BLOCK), 0)
        aq = lax.broadcasted_iota(jnp.int32, (ATT_BLOCK, 2 * ATT_BLOCK), 1)
        dist_t = aq - bk
        valid_t = (dist_t >= 0) & (dist_t <= ATT_BLOCK) & ((aq < ATT_BLOCK) | (n < nb - 1))
        distd_t = (dist_t * dil).astype(F32)
        for hh in range(hp):
            cols = _head_cols(hh)
            slope = _step_slope(group, hp, hh)
            for r in range(dil):
                sub = lambda ref: _sub_rows(ref, r, dil, cols)
                qc, kc, vc = sub(qc_ref).astype(MXU), sub(kc_ref).astype(MXU), sub(vc_ref).astype(MXU)
                doc, lc, dc = sub(doc_ref), sub(lc_ref), sub(dc_ref)
                kk = jnp.concatenate([sub(kp_ref).astype(MXU), kc], axis=0)
                vv = jnp.concatenate([sub(vp_ref).astype(MXU), vc], axis=0)
                sc = _dot_nt(qc, kk) * scale - slope * distd
                p = jnp.where(valid, jnp.exp(jnp.minimum(sc - jnp.concatenate([lc] * 2, axis=1), 0.0)), 0.0)
                dsc = p * (_dot_nt(doc, vv) - jnp.concatenate([dc] * 2, axis=1))
                _set_sub_rows(dq_acc, r, dil, cols, _dot(dsc, kk) * scale)
                qq = jnp.concatenate([qc, sub(qn_ref).astype(MXU)], axis=0)
                doo = jnp.concatenate([doc.astype(MXU), sub(don_ref).astype(MXU)], axis=0)
                lse_t = jnp.concatenate([lc, sub(ln_ref)], axis=0).T
                del_t = jnp.concatenate([dc, sub(dn_ref)], axis=0).T
                sc_t = _dot_nt(kc, qq) * scale - slope * distd_t
                p_t = jnp.where(valid_t, jnp.exp(jnp.minimum(sc_t - lse_t, 0.0)), 0.0)
                ds_t = p_t * (_dot_nt(vc, doo) - del_t)
                _set_sub_rows(dk_acc, r, dil, cols, _dot(ds_t, qq) * scale)
                _set_sub_rows(dv_acc, r, dil, cols, _dot(p_t, doo))
        dq_ref[...] = dq_acc[...].astype(dq_ref.dtype)
        dk_ref[...] = dk_acc[...].astype(dk_ref.dtype)
        dv_ref[...] = dv_acc[...].astype(dv_ref.dtype)

    return _pcall(
        body, name=f"attn_bwd_g{group}", grid=(DIL_HEADS // hp, nb),
        in_specs=[qkv(0), qkv(1), qkv(-1), qkv(0), qkv(-1), qkv(0)] + [per_head(0), per_head(1)] * 3,
        out_specs=[per_head(0)] * 3,
        out_shape=[jax.ShapeDtypeStruct((s, DIL_W), MXU)] * 3,
        scratch_shapes=[pltpu.VMEM((ATT_BLOCK * dil, hp * LANES), F32)] * 3,
        compiler_params=_params("parallel", "parallel"),
    )(qb, qb, kb, kb, vb, vb, d_o, d_o, lse, lse, delta, delta)


def _my_place():
    mx, my, mc = lax.axis_index("x"), lax.axis_index("y"), lax.axis_index("c")
    return mx, my, mc, 4 * mx + 2 * my + mc


N_CHIPS = 4


def _shard_row_tile(r):
    if r <= 512:
        return r
    return 128 if r % 128 == 0 else 480


def _other_chips(mx, my):
    return [(1 - mx, my), (mx, 1 - my), (1 - mx, 1 - my)]


def _all_gather(xs, name):
    n = len(xs)

    def body(*refs):
        x_refs, o_refs = refs[:n], refs[n:2 * n]
        send_sems, recv_sems, local_sems = refs[2 * n:]
        mx, my, mc, me = _my_place()
        sibling, sibling_id = (mx, my, 1 - mc), 4 * mx + 2 * my + (1 - mc)
        chips = _other_chips(mx, my)

        def copy(a, k, slot, to, src=None):
            dst = o_refs[a].at[slot]
            return pltpu.make_async_remote_copy(
                src_ref=dst if src is None else src, dst_ref=dst, send_sem=send_sems.at[a, k],
                recv_sem=recv_sems.at[a, k], device_id=to, device_id_type=MESH)

        local = [pltpu.make_async_copy(x_refs[a], o_refs[a].at[me], local_sems.at[a]) for a in range(n)]
        for cp in local:
            cp.start()
        sends = []
        for a in range(n):
            sends.append(copy(a, 0, me, sibling, src=x_refs[a]))
            sends += [copy(a, 1 + j, me, (px, py, mc), src=x_refs[a]) for j, (px, py) in enumerate(chips)]
        for cp in sends:
            cp.start()
        for j, (px, py) in enumerate(chips):
            slot = 4 * px + 2 * py + mc
            for a in range(n):
                copy(a, 1 + j, slot, (px, py, mc)).wait_recv()
                passed = copy(a, 4 + j, slot, sibling)
                passed.start()
                sends.append(passed)
        for a in range(n):
            copy(a, 0, sibling_id, sibling).wait_recv()
            for j, (px, py) in enumerate(chips):
                copy(a, 4 + j, 4 * px + 2 * py + (1 - mc), sibling).wait_recv()
        for cp in sends:
            cp.wait_send()
        for cp in local:
            cp.wait()

    any_spec = pl.BlockSpec(memory_space=pl.ANY)
    return _pcall(
        body, name=name,
        in_specs=[any_spec] * n, out_specs=[any_spec] * n,
        out_shape=[jax.ShapeDtypeStruct((N_DEV,) + x.shape, x.dtype) for x in xs],
        scratch_shapes=[pltpu.SemaphoreType.DMA((n, N_DEV - 1)), pltpu.SemaphoreType.DMA((n, N_DEV - 1)),
                        pltpu.SemaphoreType.DMA((n,))],
    )(*xs)


def _pair_exchange(gs, name):
    n = len(gs)

    def body(*refs):
        g_refs, o_refs = refs[:n], refs[n:2 * n]
        send_sems, recv_sems = refs[2 * n:]
        mx, my, mc, _ = _my_place()
        copies = [pltpu.make_async_remote_copy(
            src_ref=g_refs[a].at[1 - mc], dst_ref=o_refs[a], send_sem=send_sems.at[a], recv_sem=recv_sems.at[a],
            device_id=(mx, my, 1 - mc), device_id_type=MESH) for a in range(n)]
        for cp in copies:
            cp.start()
        for cp in copies:
            cp.wait()

    any_spec = pl.BlockSpec(memory_space=pl.ANY)
    return _pcall(
        body, name=name,
        in_specs=[any_spec] * n, out_specs=[any_spec] * n,
        out_shape=[jax.ShapeDtypeStruct(g.shape[1:], g.dtype) for g in gs],
        scratch_shapes=[pltpu.SemaphoreType.DMA((n,)), pltpu.SemaphoreType.DMA((n,))],
    )(*gs)


def _pair_add(g, other, name):
    _, chips, r, c = g.shape
    tr = _shard_row_tile(r)
    core = lax.axis_index("c").astype(jnp.int32).reshape(1)

    def body(core_ref, g_ref, o_ref, h_ref):
        h_ref[...] = (g_ref[...].astype(F32)[0] + o_ref[...].astype(F32)).astype(h_ref.dtype)

    blk = pl.BlockSpec((1, tr, c), lambda p, i, core_ref: (p, i, 0))
    return _pcall(
        body, name=name,
        grid_spec=pltpu.PrefetchScalarGridSpec(
            num_scalar_prefetch=1, grid=(chips, pl.cdiv(r, tr)),
            in_specs=[pl.BlockSpec((1, 1, tr, c), lambda p, i, core_ref: (core_ref[0], p, i, 0)), blk],
            out_specs=blk),
        out_shape=jax.ShapeDtypeStruct((chips, r, c), g.dtype),
        compiler_params=_params("parallel", "parallel"),
    )(core, g, other)


def _chip_exchange(hs, name):
    n = len(hs)

    def body(*refs):
        h_refs, o_refs = refs[:n], refs[n:2 * n]
        send_sems, recv_sems, local_sems = refs[2 * n:]
        mx, my, mc, _ = _my_place()
        my_chip = 2 * mx + my
        chips = _other_chips(mx, my)
        local = [pltpu.make_async_copy(h_refs[a].at[my_chip], o_refs[a].at[my_chip], local_sems.at[a]) for a in range(n)]
        for cp in local:
            cp.start()
        for j, (px, py) in enumerate(chips):
            for a in range(n):
                pltpu.make_async_remote_copy(
                    src_ref=h_refs[a].at[2 * px + py], dst_ref=o_refs[a].at[my_chip], send_sem=send_sems.at[a, j],
                    recv_sem=recv_sems.at[a, j], device_id=(px, py, mc), device_id_type=MESH).start()
        for j, (px, py) in enumerate(chips):
            for a in range(n):
                pltpu.make_async_remote_copy(
                    src_ref=h_refs[a].at[2 * px + py], dst_ref=o_refs[a].at[2 * px + py], send_sem=send_sems.at[a, j],
                    recv_sem=recv_sems.at[a, j], device_id=(px, py, mc), device_id_type=MESH).wait()
        for cp in local:
            cp.wait()

    any_spec = pl.BlockSpec(memory_space=pl.ANY)
    return _pcall(
        body, name=name,
        in_specs=[any_spec] * n, out_specs=[any_spec] * n,
        out_shape=[jax.ShapeDtypeStruct(h.shape, h.dtype) for h in hs],
        scratch_shapes=[pltpu.SemaphoreType.DMA((n, N_CHIPS - 1)), pltpu.SemaphoreType.DMA((n, N_CHIPS - 1)),
                        pltpu.SemaphoreType.DMA((n,))],
    )(*hs)


def _adamw(parts, w, m, v, name):
    r, c = w.shape
    n_parts = parts.shape[0]
    tr = _shard_row_tile(r)
    bc1 = 1.0 - ADAM_B1 ** ADAM_STEP
    bc2 = 1.0 - ADAM_B2 ** ADAM_STEP

    def body(p_ref, w_ref, m_ref, v_ref, g_ref, d_ref, nm_ref, nv_ref):
        g = p_ref[0].astype(F32)
        for j in range(1, n_parts):
            g = g + p_ref[j].astype(F32)
        nm = ADAM_B1 * m_ref[...] + (1.0 - ADAM_B1) * g
        nv = ADAM_B2 * v_ref[...] + (1.0 - ADAM_B2) * (g * g)
        g_ref[...] = g
        nm_ref[...] = nm
        nv_ref[...] = nv
        d_ref[...] = -ADAM_LR * ((nm / bc1) / (jnp.sqrt(nv / bc2) + ADAM_EPS) + ADAM_WD * w_ref[...])

    blk = pl.BlockSpec((tr, c), lambda i: (i, 0))
    return _pcall(
        body, name=name, grid=(pl.cdiv(r, tr),),
        in_specs=[pl.BlockSpec((n_parts, tr, c), lambda i: (0, i, 0)), blk, blk, blk],
        out_specs=[blk] * 4, out_shape=[jax.ShapeDtypeStruct((r, c), F32)] * 4,
        compiler_params=_params("parallel"),
    )(parts, w, m, v)


def _local_step(x, target, norm_w, w_segs, conv_w, a_log, dt_bias, dn_norm_w, w_o_dn, w_o_dil, w_out, final_norm_w):
    s = x.shape[0]
    w_qkv, w_za, w_ba, w_qb, w_kb, w_vb, w_zb, w_ga, w_gb = w_segs
    conv_w8 = jnp.concatenate([conv_w, jnp.zeros((SUBLANES - conv_w.shape[0], QKV_W), F32)], axis=0)
    pad8 = jnp.zeros((1, DN_HEADS), F32)
    alog_row = jnp.concatenate([pad8, a_log, jnp.zeros((1, LANES - 2 * DN_HEADS), F32)], axis=1)
    dtb_row = jnp.concatenate([pad8, dt_bias, jnp.zeros((1, LANES - 2 * DN_HEADS), F32)], axis=1)
    wf_row = final_norm_w.reshape(1, D_MODEL)

    hb = _rms_in_fwd(x, norm_w)
    qkv_pre, z_a, ba, z_b = _mm_out(hb, [w_qkv, w_za, w_ba, w_zb], "proj_fwd_a", w_is_out_by_in=True)
    q_b, k_b, v_b, g_a, g_b = _mm_out(hb, [w_qb, w_kb, w_vb, w_ga, w_gb], "proj_fwd_b", w_is_out_by_in=True)

    qn, kn, vn, bg = _dn_prep_fwd(qkv_pre, ba, conv_w8, alog_row, dtb_row)
    u_d, w_d, qd_d, kd_d, aqk_d, dl_d, t2_d = _delta_prep(qn, kn, vn, bg)
    o_a, vnew_d, st_d = _delta_scan_fwd(u_d, w_d, qd_d, kd_d, aqk_d, dl_d)
    on_b, y_a = _dn_out_fwd(o_a, z_a, dn_norm_w, w_o_dn)

    parts, lses = [], []
    for gi in range(N_DIL):
        o_g, l_g = _attn_fwd(q_b, k_b, v_b, gi)
        parts.append(o_g)
        lses.append(l_g)
    lse, o_joint, ob_b, y_b = _attn_out_fwd(parts, lses, z_b, w_o_dil)

    loss8, dwf8, merged_b, dx2_b, dx2, dya_b, dyb_b, dga_b, dgb_b = _merge_out_final(
        g_a, g_b, y_a, y_b, x, target, w_out, wf_row)

    g_w_out = _mm_tn(merged_b, dx2_b, "out_wgrad")
    g_w_o_dn = _mm_tn(on_b, dya_b, "out_dn_wgrad")
    d_o_a, dza_b, ddnw8 = _dn_out_bwd(dya_b, o_a, z_a, dn_norm_w, w_o_dn)

    g_w_o_dil = _mm_tn(ob_b, dyb_b, "out_dil_wgrad")
    d_o, dzb_b, delta = _attn_out_bwd(dyb_b, o_joint, z_b, w_o_dil)
    dqs, dks, dvs = [], [], []
    for gi in range(N_DIL):
        dq_g, dk_g, dv_g = _attn_bwd(q_b, k_b, v_b, d_o, lse, delta, gi)
        dqs.append(dq_g)
        dks.append(dk_g)
        dvs.append(dv_g)

    dvnew_d, dkd_d, ddl_d = _delta_scan_bwd(w_d, qd_d, kd_d, aqk_d, dl_d, vnew_d, st_d, d_o_a)
    dqn, dkn, dvn, dbg = _delta_post_bwd(qn, kn, vn, bg, t2_d, st_d, vnew_d, d_o_a, dvnew_d, dkd_d, ddl_d)
    dc, dba_b, dsmall8 = _dn_prep_bwd(qkv_pre, ba, conv_w8, alog_row, dtb_row, dqn, dkn, dvn, dbg)
    dqkv_b, dconv8 = _conv_bwd(dc, qkv_pre, conv_w8)

    per_group = lambda w: [w[g * DIL_W:(g + 1) * DIL_W] for g in range(N_DIL)]
    dh_b = _mm_in(dqs + dks + dvs + [dga_b, dgb_b],
                  per_group(w_qb) + per_group(w_kb) + per_group(w_vb) + [w_ga, w_gb], "proj_bwd_b", w_is_out_by_in=True)
    dsegs = [dqkv_b, dza_b, dba_b] + dqs + dks + dvs + [dzb_b, dga_b, dgb_b]
    g_segs = [_mm_tn(d, hb, f"proj_wgrad_{j}") for j, d in enumerate(dsegs)]
    grad_x, dnw8 = _proj_bwd_rms_in([dqkv_b, dza_b, dba_b, dzb_b], [w_qkv, w_za, w_ba, w_zb], dh_b, x, dx2, norm_w)

    small = dict(norm_w=dnw8[0:1], final_norm_w=dwf8[0:1], dn_norm_w=ddnw8[0:1],
                 a_log=dsmall8[0:1, DN_HEADS:2 * DN_HEADS], dt_bias=dsmall8[1:2, DN_HEADS:2 * DN_HEADS])
    return loss8[0:1, 0:1], grad_x, g_segs, dconv8[0:4], g_w_o_dn, g_w_o_dil, g_w_out, small


def _proj_bwd_rms_in(ds, ws, dh_a, x, dx2, norm_w):
    n_seg = len(ds)

    def body(*refs):
        d_refs, w_refs = refs[:n_seg], refs[n_seg:2 * n_seg]
        da_ref, x_ref, dx2_ref, w_ref, dx_ref, dw_ref = refs[2 * n_seg:]
        dx_ref[...] = da_ref[...]
        for d_ref, wt_ref in zip(d_refs, w_refs):
            for c, wd in _col_chunks(d_ref.shape[1], 1024):
                dx_ref[...] += jnp.dot(d_ref[:, c:c + wd], wt_ref[c:c + wd, :], preferred_element_type=F32)
        xv = x_ref[...]
        r = lax.rsqrt(jnp.mean(xv * xv, axis=-1, keepdims=True) + NORM_EPS)
        dhv = dx_ref[...]
        dn = dhv * w_ref[...]
        dx_ref[...] = dx2_ref[...] + r * dn - xv * (r * r * r) * jnp.mean(dn * xv, axis=-1, keepdims=True)
        row = jnp.sum(dhv * xv * r, axis=0, keepdims=True)
        _acc_add(dw_ref, jnp.concatenate([row, jnp.zeros((SUBLANES - 1, row.shape[1]), F32)], axis=0))

    return _rows_call(body, "proj_bwd_b_rms_in", x.shape[0],
                      [(d, "tile") for d in ds] + [(w, "full") for w in ws]
                      + [(dh_a, "tile"), (x, "tile"), (dx2, "tile"), (norm_w, "full")],
                      [(x.shape, F32, "tile"), ((SUBLANES, x.shape[1]), F32, "acc")])


def _split_proj_rows(wt_full):
    offs = [0]
    for n in PROJ_SIZES:
        offs.append(offs[-1] + n)
    seg = lambda a, b: wt_full[offs[a]:offs[b]]
    w_ba = jnp.concatenate([seg(4, 6), jnp.zeros((LANES - 2 * DN_HEADS, wt_full.shape[1]), wt_full.dtype)], axis=0)
    return [seg(0, 3), seg(3, 4), w_ba, seg(6, 7), seg(7, 8), seg(8, 9), seg(9, 10), seg(10, 11), seg(11, 12)]


def _join_proj_rows(g_segs):
    parts = list(g_segs)
    parts[2] = parts[2][:2 * DN_HEADS]
    return jnp.concatenate(parts, axis=0)


def _pack_small(norm_w, final_norm_w, dn_norm_w, a_log, dt_bias):
    pad = lambda r: jnp.concatenate([r, jnp.zeros((1, D_MODEL - r.shape[1]), F32)], axis=1)
    rows = [pad(norm_w.reshape(1, -1)), pad(final_norm_w.reshape(1, -1)), pad(dn_norm_w.reshape(1, -1)),
            pad(a_log.reshape(1, -1)), pad(dt_bias.reshape(1, -1)), jnp.zeros((SUBLANES - 5, D_MODEL), F32)]
    return jnp.concatenate(rows, axis=0)


def _unpack_small(p):
    return dict(norm_w=p[0:1], final_norm_w=p[1], dn_norm_w=p[2:3, :DN_DK], a_log=p[3:4, :DN_HEADS],
                dt_bias=p[4:5, :DN_HEADS])


def kernel(x, norm_w, w_in, conv_w, a_log, dt_bias, dn_norm_w, w_o_dn, w_o_dil, w_out, final_norm_w, loss_target, m_norm_w, m_w_in, m_conv_w, m_a_log, m_dt_bias, m_dn_norm_w, m_w_o_dn, m_w_o_dil, m_w_out, m_final_norm_w, v_norm_w, v_w_in, v_conv_w, v_a_log, v_dt_bias, v_dn_norm_w, v_w_o_dn, v_w_o_dil, v_w_out, v_final_norm_w):
    shard_w = w_in.shape[2]
    wt, m_wt, v_wt = (jnp.transpose(t[0]) for t in (w_in, m_w_in, v_w_in))
    gathered = _all_gather([wt.astype(MXU), w_o_dn[0].astype(MXU), w_o_dil[0].astype(MXU), w_out[0].astype(MXU),
                            conv_w[0]], "gather_weights")
    w_in_all, w_o_dn_all, w_o_dil_all, w_out_all, conv_all = gathered
    wt_full = w_in_all.reshape(N_DEV * shard_w, D_MODEL)
    w_o_dn_full = w_o_dn_all.reshape(D_MODEL, D_MODEL)
    w_o_dil_full = jnp.transpose(w_o_dil_all, (1, 0, 2)).reshape(DIL_W, D_MODEL)
    w_out_full = w_out_all.reshape(D_MODEL, D_MODEL)
    conv_full = jnp.transpose(conv_all, (1, 0, 2)).reshape(conv_w.shape[1], QKV_W)

    loss11, grad_x, g_segs, g_conv, g_w_o_dn, g_w_o_dil, g_w_out, small = _local_step(
        x[0], loss_target[0], norm_w, _split_proj_rows(wt_full), conv_full, a_log, dt_bias, dn_norm_w,
        w_o_dn_full, w_o_dil_full, w_out_full, final_norm_w)

    col_shards = lambda g, n: jnp.transpose(g.reshape(g.shape[0], N_DEV, n), (1, 0, 2))
    row_shards = lambda g: g.reshape(N_DEV, g.shape[0] // N_DEV, g.shape[1])
    sent = [row_shards(_join_proj_rows(g_segs)).astype(MXU), row_shards(g_w_o_dn).astype(MXU),
            col_shards(g_w_o_dil, w_o_dil.shape[2]).astype(MXU), row_shards(g_w_out).astype(MXU),
            col_shards(g_conv, conv_w.shape[2])]
    by_core = lambda g8: jnp.transpose(g8.reshape((N_CHIPS, 2) + g8.shape[1:]), (1, 0, 2, 3))
    sent = [by_core(g8) for g8 in sent]
    from_sibling = _pair_exchange(sent, "scatter_pair")
    summed = [_pair_add(g, o, f"pair_add_{i}") for i, (g, o) in enumerate(zip(sent, from_sibling))]
    p_w_in, p_w_o_dn, p_w_o_dil, p_w_out, p_conv = _chip_exchange(summed, "scatter_chips")
    p_small = _all_gather([_pack_small(small["norm_w"], small["final_norm_w"], small["dn_norm_w"], small["a_log"],
                                       small["dt_bias"])], "gather_small_grads")[0]

    res = {}
    res["w_in"] = [jnp.transpose(t) for t in _adamw(p_w_in, wt, m_wt, v_wt, "adamw_w_in")]
    res["conv_w"] = _adamw(p_conv, conv_w[0], m_conv_w[0], v_conv_w[0], "adamw_conv_w")
    res["w_o_dn"] = _adamw(p_w_o_dn, w_o_dn[0], m_w_o_dn[0], v_w_o_dn[0], "adamw_w_o_dn")
    res["w_o_dil"] = _adamw(p_w_o_dil, w_o_dil[0], m_w_o_dil[0], v_w_o_dil[0], "adamw_w_o_dil")
    res["w_out"] = _adamw(p_w_out, w_out[0], m_w_out[0], v_w_out[0], "adamw_w_out")
    small_res = _adamw(p_small, _pack_small(norm_w, final_norm_w, dn_norm_w, a_log, dt_bias),
                       _pack_small(m_norm_w, m_final_norm_w, m_dn_norm_w, m_a_log, m_dt_bias),
                       _pack_small(v_norm_w, v_final_norm_w, v_dn_norm_w, v_a_log, v_dt_bias), "adamw_small")
    small_res = [_unpack_small(t) for t in small_res]

    loss = lax.psum(loss11[0, 0], ("x", "y", "c"))
    names = ["norm_w", "w_in", "conv_w", "a_log", "dt_bias", "dn_norm_w", "w_o_dn", "w_o_dil", "w_out", "final_norm_w"]
    outs = [loss, grad_x[None]]
    for kind in range(4):
        for nm in names:
            outs.append(res[nm][kind][None] if nm in res else small_res[kind][nm])
    return tuple(outs)
```

```python
import math

import jax
import jax.numpy as jnp
from jax import lax
from jax.experimental import pallas as pl
from jax.experimental.pallas import tpu as pltpu

F32 = jnp.float32
MXU = jnp.bfloat16
MESH = pl.DeviceIdType.MESH

N_DEV = 8
D_MODEL = 1024
DN_HEADS = 8
DN_DK = 128
DN_CHUNK = 64
N_DIL = 3
DIL_HEADS = 4
DIL_DH = 128
DIL_W = DIL_HEADS * DIL_DH
DIL_GROUPS = ((128, 1), (512, 4), (2048, 16))
ATT_BLOCK = 128
NORM_EPS = 1e-6
QKV_W = 3 * D_MODEL
DILQ_W = N_DIL * DIL_W
PROJ_SIZES = (1024, 1024, 1024, 1024, 8, 8, DILQ_W, DILQ_W, DILQ_W, DIL_W, D_MODEL, D_MODEL)

ADAM_LR = 0.001
ADAM_B1 = 0.9
ADAM_B2 = 0.999
ADAM_EPS = 1e-08
ADAM_WD = 0.01
ADAM_STEP = 10

ROW_TILE = 256
LANES = 128
SUBLANES = 8
VMEM_LIMIT = 48 << 20


def _pcall(body, **kw):
    return pl.pallas_call(body, **kw)


def _params(*sem):
    return pltpu.CompilerParams(dimension_semantics=tuple(sem), vmem_limit_bytes=VMEM_LIMIT)


def _sigmoid(x):
    return 1.0 / (1.0 + jnp.exp(-x))


def _softplus(x):
    return jnp.maximum(x, 0.0) + jnp.log(1.0 + jnp.exp(-jnp.abs(x)))


def _dot(a, b):
    return jnp.dot(a.astype(MXU), b.astype(MXU), preferred_element_type=F32)


def _dot_nt(a, b):
    return lax.dot_general(a.astype(MXU), b.astype(MXU), (((1,), (1,)), ((), ())), preferred_element_type=F32)


def _dot_tn(a, b):
    return lax.dot_general(a.astype(MXU), b.astype(MXU), (((0,), (0,)), ((), ())), preferred_element_type=F32)


def _split3(x):
    hi = x.astype(jnp.bfloat16)
    r1 = x - hi.astype(F32)
    mid = r1.astype(jnp.bfloat16)
    lo = (r1 - mid.astype(F32)).astype(jnp.bfloat16)
    return hi, mid, lo


def _dot01(m01, x):
    m = m01.astype(jnp.bfloat16)
    hi, mid, lo = _split3(x)
    f = lambda p: jnp.dot(m, p, preferred_element_type=F32)
    return f(hi) + (f(mid) + f(lo))


def _rows_call(body, name, n_rows, ins, outs, scratch=(), tm=ROW_TILE):
    steps = n_rows // tm
    per8 = tm // SUBLANES
    last8 = n_rows // SUBLANES - 1
    in_specs = []
    for arr, kind in ins:
        cols = arr.shape[-1]
        if kind == "tile":
            in_specs.append(pl.BlockSpec((tm, cols), lambda i: (i, 0)))
        elif kind == "full":
            in_specs.append(pl.BlockSpec(arr.shape, lambda i, nd=arr.ndim: (0,) * nd))
        elif kind == "prev8":
            in_specs.append(pl.BlockSpec((SUBLANES, cols), lambda i: (jnp.maximum(i * per8 - 1, 0), 0)))
        elif kind == "next8":
            in_specs.append(pl.BlockSpec((SUBLANES, cols), lambda i: (jnp.minimum((i + 1) * per8, last8), 0)))
        else:
            raise ValueError(kind)
    out_specs, out_shape, has_acc = [], [], False
    for shape, dtype, kind in outs:
        out_shape.append(jax.ShapeDtypeStruct(shape, dtype))
        if kind == "tile":
            out_specs.append(pl.BlockSpec((tm, shape[-1]), lambda i: (i, 0)))
        else:
            has_acc = True
            out_specs.append(pl.BlockSpec(shape, lambda i: (0, 0)))
    return _pcall(
        body, name=name, grid=(steps,), in_specs=in_specs, out_specs=out_specs, out_shape=out_shape,
        scratch_shapes=list(scratch),
        compiler_params=_params("arbitrary" if has_acc else "parallel"),
    )(*[a for a, _ in ins])


def _acc_add(ref, value):
    @pl.when(pl.program_id(0) == 0)
    def _():
        ref[...] = jnp.zeros_like(ref)
    ref[...] += value


def _col_chunks(n, width=512):
    return [(c, min(width, n - c)) for c in range(0, n, width)]


NT_DIMS = (((1,), (1,)), ((), ()))
TN_DIMS = (((0,), (0,)), ((), ()))


def _mm_out(a, ws, name, w_is_out_by_in=False, out_dtype=F32, tm=ROW_TILE):
    m, k = a.shape
    ns = [w.shape[0] if w_is_out_by_in else w.shape[1] for w in ws]

    def body(a_ref, *refs):
        av = a_ref[...]
        for w_ref, o_ref, n in zip(refs[:len(ws)], refs[len(ws):], ns):
            for c, wd in _col_chunks(n):
                if w_is_out_by_in:
                    part = lax.dot_general(av, w_ref[c:c + wd, :], NT_DIMS, preferred_element_type=F32)
                else:
                    part = jnp.dot(av, w_ref[:, c:c + wd], preferred_element_type=F32)
                o_ref[:, c:c + wd] = part.astype(o_ref.dtype)

    return _pcall(
        body, name=name, grid=(m // tm,),
        in_specs=[pl.BlockSpec((tm, k), lambda i: (i, 0))] + [pl.BlockSpec(w.shape, lambda i: (0, 0)) for w in ws],
        out_specs=[pl.BlockSpec((tm, n), lambda i: (i, 0)) for n in ns],
        out_shape=[jax.ShapeDtypeStruct((m, n), out_dtype) for n in ns],
        compiler_params=_params("parallel"),
    )(a, *ws)


def _mm_in(ds, ws, name, w_is_out_by_in=False, tm=ROW_TILE):
    m = ds[0].shape[0]
    k = ws[0].shape[1] if w_is_out_by_in else ws[0].shape[0]
    ns = [d.shape[1] for d in ds]

    def body(*refs):
        d_refs, w_refs, o_ref = refs[:len(ds)], refs[len(ds):2 * len(ds)], refs[-1]
        first = True
        for d_ref, w_ref, n in zip(d_refs, w_refs, ns):
            for c, wd in _col_chunks(n, 1024):
                if w_is_out_by_in:
                    part = jnp.dot(d_ref[:, c:c + wd], w_ref[c:c + wd, :], preferred_element_type=F32)
                else:
                    part = lax.dot_general(d_ref[:, c:c + wd], w_ref[:, c:c + wd], NT_DIMS, preferred_element_type=F32)
                if first:
                    o_ref[...] = part
                    first = False
                else:
                    o_ref[...] += part

    return _pcall(
        body, name=name, grid=(m // tm,),
        in_specs=[pl.BlockSpec((tm, n), lambda i: (i, 0)) for n in ns] + [pl.BlockSpec(w.shape, lambda i: (0, 0)) for w in ws],
        out_specs=pl.BlockSpec((tm, k), lambda i: (i, 0)),
        out_shape=jax.ShapeDtypeStruct((m, k), F32),
        compiler_params=_params("parallel"),
    )(*ds, *ws)


def _mm_tn(a, d, name):
    m, k = a.shape
    n = d.shape[1]
    tk = 512 if k % 512 == 0 else k

    def body(a_ref, d_ref, o_ref):
        o_ref[...] = lax.dot_general(a_ref[...], d_ref[...], TN_DIMS, preferred_element_type=F32)

    return _pcall(
        body, name=name, grid=(k // tk,),
        in_specs=[pl.BlockSpec((m, tk), lambda p: (0, p)), pl.BlockSpec((m, n), lambda p: (0, 0))],
        out_specs=pl.BlockSpec((tk, n), lambda p: (p, 0)),
        out_shape=jax.ShapeDtypeStruct((k, n), F32),
        compiler_params=_params("parallel"),
    )(a, d)


def _rms_in_fwd(x, norm_w):
    def body(x_ref, w_ref, h_ref):
        xv = x_ref[...]
        r = lax.rsqrt(jnp.mean(xv * xv, axis=-1, keepdims=True) + NORM_EPS)
        h_ref[...] = (xv * r * w_ref[...]).astype(h_ref.dtype)

    return _rows_call(body, "rms_in_fwd", x.shape[0], [(x, "tile"), (norm_w, "full")],
                      [(x.shape, MXU, "tile")])[0]


def _conv_taps(ext_ref, cw_ref, cols, tm):
    c = None
    for j in range(4):
        term = cw_ref[3 - j:4 - j, cols] * ext_ref[SUBLANES - j:SUBLANES - j + tm, cols]
        c = term if c is None else c + term
    return c


def _fill_ext(ext_ref, u_ref, halo_ref, first):
    ext_ref[0:SUBLANES, :] = jnp.where(first, 0.0, halo_ref[...])
    ext_ref[SUBLANES:, :] = u_ref[...]


def _dn_prep_fwd(qkv_pre, ba, conv_w8, alog_row, dtb_row):
    s = qkv_pre.shape[0]
    tm = ROW_TILE

    def body(u_ref, halo_ref, cw_ref, ba_ref, al_ref, dtb_ref, q_ref, k_ref, v_ref, bg_ref, ext_ref):
        _fill_ext(ext_ref, u_ref, halo_ref, pl.program_id(0) == 0)
        for h in range(3 * DN_HEADS):
            cols = slice(h * LANES, (h + 1) * LANES)
            c = _conv_taps(ext_ref, cw_ref, cols, tm)
            a = c * _sigmoid(c)
            oc = slice((h % DN_HEADS) * LANES, (h % DN_HEADS + 1) * LANES)
            if h < 2 * DN_HEADS:
                rinv = lax.rsqrt(jnp.sum(a * a, axis=-1, keepdims=True) + NORM_EPS)
                if h < DN_HEADS:
                    q_ref[:, oc] = a * (rinv * DN_DK ** -0.5)
                else:
                    k_ref[:, oc] = a * rinv
            else:
                v_ref[:, oc] = a
        bav = ba_ref[...]
        lane = lax.broadcasted_iota(jnp.int32, bav.shape, 1)
        beta = _sigmoid(bav)
        g = -jnp.exp(al_ref[...]) * _softplus(bav + dtb_ref[...])
        bg_ref[...] = jnp.where(lane < DN_HEADS, beta, jnp.where(lane < 2 * DN_HEADS, g, 0.0))

    return _rows_call(
        body, "dn_prep_fwd", s,
        [(qkv_pre, "tile"), (qkv_pre, "prev8"), (conv_w8, "full"), (ba, "tile"), (alog_row, "full"), (dtb_row, "full")],
        [((s, D_MODEL), F32, "tile")] * 3 + [((s, LANES), F32, "tile")],
        scratch=[pltpu.VMEM((tm + SUBLANES, QKV_W), F32)])


def _dn_prep_bwd(qkv_pre, ba, conv_w8, alog_row, dtb_row, dq, dk, dv, dbg):
    s = qkv_pre.shape[0]
    tm = ROW_TILE

    def body(u_ref, halo_ref, cw_ref, ba_ref, al_ref, dtb_ref, dq_ref, dk_ref, dv_ref, dbg_ref,
             dc_ref, dba_ref, dsmall_ref, ext_ref):
        _fill_ext(ext_ref, u_ref, halo_ref, pl.program_id(0) == 0)
        for h in range(3 * DN_HEADS):
            cols = slice(h * LANES, (h + 1) * LANES)
            oc = slice((h % DN_HEADS) * LANES, (h % DN_HEADS + 1) * LANES)
            c = _conv_taps(ext_ref, cw_ref, cols, tm)
            sg = _sigmoid(c)
            a = c * sg
            if h < 2 * DN_HEADS:
                rinv = lax.rsqrt(jnp.sum(a * a, axis=-1, keepdims=True) + NORM_EPS)
                dy = dq_ref[:, oc] * DN_DK ** -0.5 if h < DN_HEADS else dk_ref[:, oc]
                da = rinv * dy - a * (rinv * rinv * rinv) * jnp.sum(dy * a, axis=-1, keepdims=True)
            else:
                da = dv_ref[:, oc]
            dc_ref[:, cols] = da * (sg * (1.0 + c * (1.0 - sg)))
        bav = ba_ref[...]
        dbgv = dbg_ref[...]
        lane = lax.broadcasted_iota(jnp.int32, bav.shape, 1)
        beta = _sigmoid(bav)
        ea = jnp.exp(al_ref[...])
        z = bav + dtb_ref[...]
        g = -ea * _softplus(z)
        is_b = lane < DN_HEADS
        is_g = jnp.logical_and(lane >= DN_HEADS, lane < 2 * DN_HEADS)
        d_aa = jnp.where(is_g, dbgv * (-ea) * _sigmoid(z), 0.0)
        dba = jnp.where(is_b, dbgv * beta * (1.0 - beta), d_aa)
        dba_ref[...] = dba.astype(dba_ref.dtype)
        r_alog = jnp.sum(jnp.where(is_g, dbgv * g, 0.0), axis=0, keepdims=True)
        r_dtb = jnp.sum(d_aa, axis=0, keepdims=True)
        _acc_add(dsmall_ref, jnp.concatenate([r_alog, r_dtb, jnp.zeros((SUBLANES - 2, LANES), F32)], axis=0))

    return _rows_call(
        body, "dn_prep_bwd", s,
        [(qkv_pre, "tile"), (qkv_pre, "prev8"), (conv_w8, "full"), (ba, "tile"), (alog_row, "full"), (dtb_row, "full"),
         (dq, "tile"), (dk, "tile"), (dv, "tile"), (dbg, "tile")],
        [((s, QKV_W), F32, "tile"), ((s, LANES), MXU, "tile"), ((SUBLANES, LANES), F32, "acc")],
        scratch=[pltpu.VMEM((tm + SUBLANES, QKV_W), F32)])


def _conv_bwd(dc, qkv_pre, conv_w8):
    s = dc.shape[0]
    tm = ROW_TILE
    steps = s // tm

    def body(dc_ref, dnext_ref, u_ref, halo_ref, cw_ref, du_ref, dcw_ref, extd_ref, ext_ref):
        i = pl.program_id(0)
        _fill_ext(ext_ref, u_ref, halo_ref, i == 0)
        extd_ref[0:tm, :] = dc_ref[...]
        extd_ref[tm:, :] = jnp.where(i == steps - 1, 0.0, dnext_ref[...])

        @pl.when(i == 0)
        def _():
            dcw_ref[...] = jnp.zeros_like(dcw_ref)

        for h in range(3 * DN_HEADS):
            cols = slice(h * LANES, (h + 1) * LANES)
            du = None
            for j in range(4):
                term = cw_ref[3 - j:4 - j, cols] * extd_ref[j:j + tm, cols]
                du = term if du is None else du + term
            du_ref[:, cols] = du.astype(du_ref.dtype)
            dcv = dc_ref[:, cols]
            for j in range(4):
                row = jnp.sum(dcv * ext_ref[SUBLANES - j:SUBLANES - j + tm, cols], axis=0, keepdims=True)
                dcw_ref[3 - j:4 - j, cols] += row

    return _rows_call(
        body, "conv_bwd", s,
        [(dc, "tile"), (dc, "next8"), (qkv_pre, "tile"), (qkv_pre, "prev8"), (conv_w8, "full")],
        [((s, QKV_W), MXU, "tile"), ((SUBLANES, QKV_W), F32, "acc")],
        scratch=[pltpu.VMEM((tm + SUBLANES, QKV_W), F32), pltpu.VMEM((tm + SUBLANES, QKV_W), F32)])


def _dn_out_fwd(o, z, dnw_row, w_o_dn):
    def body(o_ref, z_ref, w_ref, wo_ref, on_ref, y_ref):
        for h in range(DN_HEADS):
            cols = slice(h * LANES, (h + 1) * LANES)
            ov = o_ref[:, cols]
            zv = z_ref[:, cols]
            ro = lax.rsqrt(jnp.mean(ov * ov, axis=-1, keepdims=True) + NORM_EPS)
            on_ref[:, cols] = (ov * ro * w_ref[...] * (zv * _sigmoid(zv))).astype(on_ref.dtype)
        y_ref[...] = jnp.dot(on_ref[...], wo_ref[...], preferred_element_type=F32)

    return _rows_call(body, "dn_out_fwd", o.shape[0], [(o, "tile"), (z, "tile"), (dnw_row, "full"), (w_o_dn, "full")],
                      [(o.shape, MXU, "tile"), ((o.shape[0], w_o_dn.shape[1]), F32, "tile")])


def _dn_out_bwd(dy, o, z, dnw_row, w_o_dn):
    def body(dy_ref, o_ref, z_ref, w_ref, wo_ref, do_ref, dz_ref, dw_ref, d_ref):
        d_ref[...] = lax.dot_general(dy_ref[...], wo_ref[...], NT_DIMS, preferred_element_type=F32)
        acc = jnp.zeros((1, LANES), F32)
        for h in range(DN_HEADS):
            cols = slice(h * LANES, (h + 1) * LANES)
            dv, ov, zv = d_ref[:, cols], o_ref[:, cols], z_ref[:, cols]
            sg = _sigmoid(zv)
            sz = zv * sg
            ro = lax.rsqrt(jnp.mean(ov * ov, axis=-1, keepdims=True) + NORM_EPS)
            nv = ov * ro
            dn = dv * w_ref[...] * sz
            acc = acc + jnp.sum(dv * nv * sz, axis=0, keepdims=True)
            dz_ref[:, cols] = (dv * nv * w_ref[...] * (sg * (1.0 + zv * (1.0 - sg)))).astype(dz_ref.dtype)
            do_ref[:, cols] = ro * dn - ov * (ro * ro * ro) * jnp.mean(dn * ov, axis=-1, keepdims=True)
        _acc_add(dw_ref, jnp.concatenate([acc, jnp.zeros((SUBLANES - 1, LANES), F32)], axis=0))

    return _rows_call(body, "dn_out_bwd", o.shape[0],
                      [(dy, "tile"), (o, "tile"), (z, "tile"), (dnw_row, "full"), (w_o_dn, "full")],
                      [(o.shape, F32, "tile"), (o.shape, MXU, "tile"), ((SUBLANES, LANES), F32, "acc")],
                      scratch=[pltpu.VMEM((ROW_TILE, o.shape[1]), F32)])


def _attn_out_fwd(parts, lses, zb, w_o_dil):
    def body(o0, o1, o2, l0, l1, l2, z_ref, wo_ref, lse_ref, o_ref, g_ref, y_ref):
        a, b, c = l0[...], l1[...], l2[...]
        m = jnp.maximum(a, jnp.maximum(b, c))
        ea, eb, ec = jnp.exp(a - m), jnp.exp(b - m), jnp.exp(c - m)
        den = ea + eb + ec
        out = (ea * o0[...] + eb * o1[...] + ec * o2[...]) / den
        lse_ref[...] = m + jnp.log(den)
        o_ref[...] = out
        zv = z_ref[...]
        gated = (out * (zv * _sigmoid(zv))).astype(g_ref.dtype)
        g_ref[...] = gated
        y_ref[...] = jnp.dot(gated, wo_ref[...], preferred_element_type=F32)

    s = zb.shape[0]
    return _rows_call(body, "attn_out_fwd", s,
                      [(p, "tile") for p in parts] + [(l, "tile") for l in lses] + [(zb, "tile"), (w_o_dil, "full")],
                      [((s, DIL_W), F32, "tile"), ((s, DIL_W), F32, "tile"), ((s, DIL_W), MXU, "tile"),
                       ((s, w_o_dil.shape[1]), F32, "tile")])


def _attn_out_bwd(dy, o_joint, zb, w_o_dil):
    def body(dy_ref, o_ref, z_ref, wo_ref, do_ref, dz_ref, dl_ref):
        zv = z_ref[...]
        sg = _sigmoid(zv)
        dv = lax.dot_general(dy_ref[...], wo_ref[...], NT_DIMS, preferred_element_type=F32)
        ov = o_ref[...]
        do = dv * (zv * sg)
        do_ref[...] = do
        dz_ref[...] = (dv * ov * (sg * (1.0 + zv * (1.0 - sg)))).astype(dz_ref.dtype)
        for h in range(DIL_HEADS):
            cols = slice(h * LANES, (h + 1) * LANES)
            dl_ref[:, cols] = jnp.broadcast_to(jnp.sum(do[:, cols] * ov[:, cols], axis=-1, keepdims=True),
                                               (do.shape[0], LANES))

    s = zb.shape[0]
    return _rows_call(body, "attn_out_bwd", s, [(dy, "tile"), (o_joint, "tile"), (zb, "tile"), (w_o_dil, "full")],
                      [((s, DIL_W), F32, "tile"), ((s, DIL_W), MXU, "tile"), ((s, DIL_W), F32, "tile")])


def _merge_out_final(ga, gb, ya, yb, x, target, w_out, wf_row):
    s, dm = x.shape

    def body(ga_ref, gb_ref, ya_ref, yb_ref, x_ref, t_ref, wo_ref, w_ref,
             loss_ref, dw_ref, m_ref, dxb_ref, dx_ref, dya_ref, dyb_ref, dga_ref, dgb_ref):
        sa, sb = _sigmoid(ga_ref[...]), _sigmoid(gb_ref[...])
        ya, yb = ya_ref[...], yb_ref[...]
        merged = (sa * ya + sb * yb).astype(MXU)
        m_ref[...] = merged
        x2 = x_ref[...] + jnp.dot(merged, wo_ref[...], preferred_element_type=F32)
        r = lax.rsqrt(jnp.mean(x2 * x2, axis=-1, keepdims=True) + NORM_EPS)
        w = w_ref[...]
        err = x2 * r * w - t_ref[...]
        tile_loss = 0.5 * jnp.sum(jnp.mean(err * err, axis=-1, keepdims=True), axis=0, keepdims=True)
        _acc_add(loss_ref, jnp.broadcast_to(tile_loss, (SUBLANES, LANES)))
        dy = err * (1.0 / dm)
        row = jnp.sum(dy * x2 * r, axis=0, keepdims=True)
        _acc_add(dw_ref, jnp.concatenate([row, jnp.zeros((SUBLANES - 1, dm), F32)], axis=0))
        dn = dy * w
        dx2 = r * dn - x2 * (r * r * r) * jnp.mean(dn * x2, axis=-1, keepdims=True)
        dx_ref[...] = dx2
        dxb = dx2.astype(MXU)
        dxb_ref[...] = dxb
        dmv = lax.dot_general(dxb, wo_ref[...], NT_DIMS, preferred_element_type=F32)
        dya_ref[...] = (dmv * sa).astype(dya_ref.dtype)
        dyb_ref[...] = (dmv * sb).astype(dyb_ref.dtype)
        dga_ref[...] = (dmv * ya * sa * (1.0 - sa)).astype(dga_ref.dtype)
        dgb_ref[...] = (dmv * yb * sb * (1.0 - sb)).astype(dgb_ref.dtype)

    return _rows_call(body, "merge_out_final", s,
                      [(ga, "tile"), (gb, "tile"), (ya, "tile"), (yb, "tile"), (x, "tile"), (target, "tile"),
                       (w_out, "full"), (wf_row, "full")],
                      [((SUBLANES, LANES), F32, "acc"), ((SUBLANES, dm), F32, "acc"), ((s, dm), MXU, "tile"),
                       ((s, dm), MXU, "tile"), ((s, dm), F32, "tile")] + [((s, dm), MXU, "tile")] * 4)


def _lane_pick(x, idx):
    lane = lax.broadcasted_iota(jnp.int32, x.shape, 1)
    return jnp.sum(jnp.where(lane == idx, x, 0.0), axis=-1, keepdims=True)


PAIR = 2 * DN_CHUNK
SCAN_CHUNKS = 4


def _bmm(a, b):
    return lax.dot_general(a.astype(MXU), b.astype(MXU), (((2,), (1,)), ((0,), (0,))), preferred_element_type=F32)


def _bmm_nt(a, b):
    return lax.dot_general(a.astype(MXU), b.astype(MXU), (((2,), (2,)), ((0,), (0,))), preferred_element_type=F32)


def _bmm_tn(a, b):
    return lax.dot_general(a.astype(MXU), b.astype(MXU), (((1,), (1,)), ((0,), (0,))), preferred_element_type=F32)


def _bmm3(a, b):
    ah = a.astype(jnp.bfloat16)
    al = (a - ah.astype(F32)).astype(jnp.bfloat16)
    bh = b.astype(jnp.bfloat16)
    bl = (b - bh.astype(F32)).astype(jnp.bfloat16)
    f = lambda p, q: lax.dot_general(p, q, (((2,), (1,)), ((0,), (0,))), preferred_element_type=F32)
    return f(ah, bh) + (f(ah, bl) + f(al, bh))


def _pair_masks():
    row = lax.broadcasted_iota(jnp.int32, (PAIR, PAIR), 0)
    col = lax.broadcasted_iota(jnp.int32, (PAIR, PAIR), 1)
    same = (row >= DN_CHUNK) == (col >= DN_CHUNK)
    return dict(causal=same & (row >= col), strict=same & (row > col), upper=same & (row <= col), eye=row == col,
                first=row < DN_CHUNK, row=row, lane=col)


def _pair_decay(bgv, masks):
    gc_all = _dot01(masks["causal"].astype(F32), bgv)
    out = []
    for h in range(DN_HEADS):
        beta = _lane_pick(bgv, h)
        gcb = jnp.broadcast_to(_lane_pick(gc_all, DN_HEADS + h), (PAIR, PAIR))
        gam = jnp.where(masks["causal"], jnp.exp(jnp.minimum(gcb - gcb.T, 0.0)), 0.0)
        gl = jnp.where(masks["first"], gcb[DN_CHUNK - 1:DN_CHUNK, :], gcb[PAIR - 1:PAIR, :])
        out.append((beta, gcb, gam, gl))
    return out


def _pair_inverse(a_strict, eye):
    n = -a_strict
    t = eye.astype(F32)[None] + n
    p = n
    for _ in range(int(math.log2(DN_CHUNK)) - 1):
        p = _bmm3(p, p)
        t = t + _bmm3(t, p)
    return t


def _head_cols(h):
    return slice(h * LANES, (h + 1) * LANES)


def _delta_prep(q, k, v, bg):
    s = q.shape[0]
    c = DN_CHUNK
    n_chunks = s // c

    def body(q_ref, k_ref, v_ref, bg_ref, u_ref, w_ref, qd_ref, kd_ref, aqk_ref, dl_ref, t2_ref):
        masks = _pair_masks()
        dec = _pair_decay(bg_ref[...], masks)
        kbs, ks, gams, vbs, kbes, qs, qds, kds, dls = ([] for _ in range(9))
        for h in range(DN_HEADS):
            beta, gcb, gam, gl = dec[h]
            qh, kh, vh = q_ref[:, _head_cols(h)], k_ref[:, _head_cols(h)], v_ref[:, _head_cols(h)]
            eg = jnp.exp(gcb)
            kb = kh * beta
            kbs.append(kb); ks.append(kh); gams.append(gam); vbs.append(vh * beta); kbes.append(kb * eg)
            qs.append(qh); qds.append(qh * eg); kds.append(kh * jnp.exp(gl - gcb)); dls.append(jnp.exp(gl))
        st = lambda xs: jnp.stack(xs, axis=0)
        kmat, gam = st(ks), st(gams)
        a = jnp.where(masks["strict"][None], _bmm_nt(st(kbs), kmat) * gam, 0.0)
        t = _pair_inverse(a, masks["eye"])
        u = _bmm(t, st(vbs))
        w = _bmm(t, st(kbes))
        aqk = _bmm_nt(st(qs), kmat) * gam
        t2_ref[0] = t.astype(t2_ref.dtype)
        for half in range(2):
            rows = slice(half * c, (half + 1) * c)
            u_ref[half] = u[:, rows, :]
            w_ref[half] = w[:, rows, :].astype(w_ref.dtype)
            qd_ref[half] = st(qds)[:, rows, :].astype(qd_ref.dtype)
            kd_ref[half] = st(kds)[:, rows, :].astype(kd_ref.dtype)
            aqk_ref[half] = aqk[:, rows, rows].astype(aqk_ref.dtype)
            dl_ref[half] = st(dls)[:, half * c:half * c + SUBLANES, :]

    row_spec = lambda w_: pl.BlockSpec((PAIR, w_), lambda i: (i, 0))
    hm = lambda a_, b_: pl.BlockSpec((2, DN_HEADS, a_, b_), lambda i: (i, 0, 0, 0))
    hm_shape = lambda a_, b_, dt: jax.ShapeDtypeStruct((n_chunks, DN_HEADS, a_, b_), dt)
    return _pcall(
        body, name="delta_prep", grid=(n_chunks // 2,),
        in_specs=[row_spec(D_MODEL)] * 3 + [row_spec(LANES)],
        out_specs=[hm(c, LANES)] * 4 + [hm(c, c), hm(SUBLANES, LANES),
                   pl.BlockSpec((1, DN_HEADS, PAIR, PAIR), lambda i: (i, 0, 0, 0))],
        out_shape=[hm_shape(c, LANES, F32), hm_shape(c, LANES, MXU), hm_shape(c, LANES, MXU), hm_shape(c, LANES, MXU),
                   hm_shape(c, c, MXU), hm_shape(SUBLANES, LANES, F32),
                   jax.ShapeDtypeStruct((n_chunks // 2, DN_HEADS, PAIR, PAIR), MXU)],
        compiler_params=_params("parallel"),
    )(q, k, v, bg)


def _delta_scan_fwd(u, w, qd, kd, aqk, dl):
    n_chunks = u.shape[0]
    c = DN_CHUNK
    g_n = SCAN_CHUNKS

    def body(u_ref, w_ref, qd_ref, kd_ref, aqk_ref, dl_ref, o_ref, vnew_ref, st_ref, state):
        @pl.when(pl.program_id(0) == 0)
        def _():
            state[...] = jnp.zeros_like(state)

        for g in range(g_n):
            sv = state[...]
            sb = sv.astype(MXU)
            vnew = u_ref[g] - _bmm(w_ref[g], sb)
            o = _bmm(qd_ref[g], sb) + _bmm(aqk_ref[g], vnew)
            state[...] = sv * dl_ref[g][:, 0:1, :] + _bmm_tn(kd_ref[g], vnew)
            vnew_ref[g] = vnew.astype(vnew_ref.dtype)
            st_ref[g] = sb
            for h in range(DN_HEADS):
                o_ref[g * c:(g + 1) * c, _head_cols(h)] = o[h]

    hm = lambda a_, b_: pl.BlockSpec((g_n, DN_HEADS, a_, b_), lambda i: (i, 0, 0, 0))
    return _pcall(
        body, name="delta_scan_fwd", grid=(n_chunks // g_n,),
        in_specs=[hm(c, LANES)] * 4 + [hm(c, c), hm(SUBLANES, LANES)],
        out_specs=[pl.BlockSpec((g_n * c, D_MODEL), lambda i: (i, 0)), hm(c, LANES), hm(DN_DK, DN_DK)],
        out_shape=[jax.ShapeDtypeStruct((n_chunks * c, D_MODEL), F32),
                   jax.ShapeDtypeStruct((n_chunks, DN_HEADS, c, LANES), MXU),
                   jax.ShapeDtypeStruct((n_chunks, DN_HEADS, DN_DK, DN_DK), MXU)],
        scratch_shapes=[pltpu.VMEM((DN_HEADS, DN_DK, DN_DK), F32)],
        compiler_params=_params("arbitrary"),
    )(u, w, qd, kd, aqk, dl)


def _delta_scan_bwd(w, qd, kd, aqk, dl, vnew, st, do):
    n_chunks = w.shape[0]
    c = DN_CHUNK
    g_n = SCAN_CHUNKS
    steps = n_chunks // g_n

    def body(w_ref, qd_ref, kd_ref, aqk_ref, dl_ref, vnew_ref, st_ref, do_ref, dvnew_ref, dkd_ref, ddl_ref, dstate):
        @pl.when(pl.program_id(0) == 0)
        def _():
            dstate[...] = jnp.zeros_like(dstate)

        for g in reversed(range(g_n)):
            ds = dstate[...]
            dsb = ds.astype(MXU)
            doh = jnp.stack([do_ref[g * c:(g + 1) * c, _head_cols(h)] for h in range(DN_HEADS)], axis=0)
            dvnew = _bmm_tn(aqk_ref[g], doh) + _bmm(kd_ref[g], dsb)
            dkd_ref[g] = _bmm_nt(vnew_ref[g], dsb)
            ddl = jnp.sum(jnp.sum(st_ref[g].astype(F32) * ds, axis=2, keepdims=True), axis=1, keepdims=True)
            ddl_ref[g] = jnp.broadcast_to(ddl, (DN_HEADS, SUBLANES, LANES))
            dstate[...] = ds * dl_ref[g][:, 0:1, :] + _bmm_tn(qd_ref[g], doh) - _bmm_tn(w_ref[g], dvnew)
            dvnew_ref[g] = dvnew.astype(dvnew_ref.dtype)

    rev = lambda i: steps - 1 - i
    hm = lambda a_, b_: pl.BlockSpec((g_n, DN_HEADS, a_, b_), lambda i: (rev(i), 0, 0, 0))
    return _pcall(
        body, name="delta_scan_bwd", grid=(steps,),
        in_specs=[hm(c, LANES)] * 3 + [hm(c, c), hm(SUBLANES, LANES), hm(c, LANES), hm(DN_DK, DN_DK),
                  pl.BlockSpec((g_n * c, D_MODEL), lambda i: (rev(i), 0))],
        out_specs=[hm(c, LANES), hm(c, LANES), hm(SUBLANES, LANES)],
        out_shape=[jax.ShapeDtypeStruct((n_chunks, DN_HEADS, c, LANES), MXU),
                   jax.ShapeDtypeStruct((n_chunks, DN_HEADS, c, LANES), F32),
                   jax.ShapeDtypeStruct((n_chunks, DN_HEADS, SUBLANES, LANES), F32)],
        scratch_shapes=[pltpu.VMEM((DN_HEADS, DN_DK, DN_DK), F32)],
        compiler_params=_params("arbitrary"),
    )(w, qd, kd, aqk, dl, vnew, st, do)


def _delta_post_bwd(q, k, v, bg, t2, st, vnew, do, dvnew, dkd, ddl):
    s = q.shape[0]
    c = DN_CHUNK

    def body(q_ref, k_ref, v_ref, bg_ref, t2_ref, st_ref, vnew_ref, do_ref, dvnew_ref, dkd_ref, ddl_ref,
             dq_ref, dk_ref, dv_ref, dbg_ref):
        masks = _pair_masks()
        first = masks["first"][None]
        dec = _pair_decay(bg_ref[...], masks)
        st_ = lambda xs: jnp.stack(xs, axis=0)
        heads = range(DN_HEADS)
        qm_, km_, vm_, dom = (st_([r[:, _head_cols(h)] for h in heads]) for r in (q_ref, k_ref, v_ref, do_ref))
        beta = st_([dec[h][0] for h in heads])
        gcb = st_([dec[h][1] for h in heads])
        gam = st_([dec[h][2] for h in heads])
        gl = st_([dec[h][3] for h in heads])
        pair = lambda ref: jnp.concatenate([ref[0], ref[1]], axis=1)
        vnew2, dvnew2, dkd2 = pair(vnew_ref), pair(dvnew_ref), pair(dkd_ref)
        halves = lambda x: (x[:, :c, :], x[:, c:, :])
        by_state = lambda x: jnp.concatenate([_bmm_nt(xh, st_ref[i]) for i, xh in enumerate(halves(x))], axis=1)
        dqd = by_state(dom)
        dw = -by_state(dvnew2)
        ddl2 = jnp.where(first, ddl_ref[0][:, 0:1, :], ddl_ref[1][:, 0:1, :])

        eg = jnp.exp(gcb)
        egl = jnp.exp(gl - gcb)
        dl = jnp.exp(gl)
        kb = km_ * beta
        kk = _bmm_nt(kb, km_)
        a = jnp.where(masks["strict"][None], kk * gam, 0.0)
        t = t2_ref[0]
        vb = vm_ * beta
        kbe = kb * eg
        u = _bmm(t, vb)
        w = _bmm(t, kbe)
        aqk = _bmm_nt(qm_, km_) * gam
        qd = qm_ * eg
        kd = km_ * egl

        daqk = jnp.where(masks["causal"][None], _bmm_nt(dom, vnew2), 0.0)
        dvb = _bmm_tn(t, dvnew2)
        dkbe = _bmm_tn(t, dw)
        da = jnp.where(masks["strict"][None], -(_bmm_nt(dvb, u) + _bmm_nt(dkbe, w)), 0.0)
        pm = da * gam
        qmm = daqk * gam
        dkb = _bmm(pm, km_) + dkbe * eg
        dkh = _bmm_tn(pm, kb) + _bmm_tn(qmm, qm_) + dkd2 * egl + dkb * beta
        dqh = _bmm(qmm, km_) + dqd * eg
        xm = da * a + daqk * aqk
        ones = jnp.ones((DN_HEADS, PAIR, LANES), F32)
        hi, mid, lo = _split3(xm)
        colsum = _bmm_tn(hi, ones) + (_bmm_tn(mid, ones) + _bmm_tn(lo, ones))
        tmp = jnp.sum(dkd2 * kd, axis=-1, keepdims=True)
        dgc = (jnp.sum(xm, axis=-1, keepdims=True) - colsum + jnp.sum(dkbe * kbe, axis=-1, keepdims=True)
               + jnp.sum(dqd * qd, axis=-1, keepdims=True) - tmp)
        sum0 = jnp.sum(jnp.where(first, tmp, 0.0), axis=1, keepdims=True)
        sum1 = jnp.sum(jnp.where(first, 0.0, tmp), axis=1, keepdims=True)
        dgl = jnp.where(first, sum0, sum1) + ddl2 * dl
        last = (masks["row"] == c - 1) | (masks["row"] == PAIR - 1)
        dgc = dgc + jnp.where(last[None], dgl, 0.0)
        dbeta = jnp.sum(dvb * vm_, axis=-1, keepdims=True) + jnp.sum(dkb * km_, axis=-1, keepdims=True)
        dvh = dvb * beta

        lane = masks["lane"]
        dgc_lanes = jnp.zeros((PAIR, LANES), F32)
        dbg = jnp.zeros((PAIR, LANES), F32)
        for h in heads:
            dq_ref[:, _head_cols(h)] = dqh[h]
            dk_ref[:, _head_cols(h)] = dkh[h]
            dv_ref[:, _head_cols(h)] = dvh[h]
            dgc_lanes = dgc_lanes + jnp.where(lane == DN_HEADS + h, dgc[h], 0.0)
            dbg = dbg + jnp.where(lane == h, dbeta[h], 0.0)
        dbg_ref[...] = dbg + _dot01(masks["upper"].astype(F32), dgc_lanes)

    n_pairs = s // PAIR
    row_spec = lambda w_: pl.BlockSpec((PAIR, w_), lambda i: (i, 0))
    hm = lambda a_, b_: pl.BlockSpec((2, DN_HEADS, a_, b_), lambda i: (i, 0, 0, 0))
    return _pcall(
        body, name="delta_post_bwd", grid=(n_pairs,),
        in_specs=[row_spec(D_MODEL)] * 3 + [row_spec(LANES), pl.BlockSpec((1, DN_HEADS, PAIR, PAIR), lambda i: (i, 0, 0, 0)),
                  hm(DN_DK, DN_DK), hm(c, LANES), row_spec(D_MODEL), hm(c, LANES), hm(c, LANES), hm(SUBLANES, LANES)],
        out_specs=[row_spec(D_MODEL)] * 3 + [row_spec(LANES)],
        out_shape=[jax.ShapeDtypeStruct((s, D_MODEL), F32)] * 3 + [jax.ShapeDtypeStruct((s, LANES), F32)],
        compiler_params=_params("parallel"),
    )(q, k, v, bg, t2, st, vnew, do, dvnew, dkd, ddl)


def _alibi_slope(group, head):
    n = N_DIL * DIL_HEADS
    return float(2.0 ** (-8.0 * (group * DIL_HEADS + head + 1) / n))


def _attn_plan(s, group):
    window, dil = DIL_GROUPS[group]
    assert window // dil == ATT_BLOCK
    assert (s // dil) % ATT_BLOCK == 0, "sub-sequence length must be a whole number of attention blocks"
    return dil, s // dil // ATT_BLOCK, (DIL_HEADS if dil == 1 else 1)


def _attn_specs(group, dil, nb, hp):
    rows = ATT_BLOCK * dil

    def spec(col0, shift):
        if shift < 0:
            f = lambda hb, n: (jnp.maximum(n - 1, 0), col0 + hb)
        elif shift > 0:
            f = lambda hb, n: (jnp.minimum(n + 1, nb - 1), col0 + hb)
        else:
            f = lambda hb, n: (jnp.minimum(n, nb - 1), col0 + hb)
        return pl.BlockSpec((rows, hp * LANES), f)

    return (lambda shift: spec(group * (DIL_HEADS // hp), shift)), (lambda shift: spec(0, shift))


def _sub_rows(ref, r, dil, cols):
    return ref[:, cols] if dil == 1 else ref[pl.ds(r, ATT_BLOCK, stride=dil), cols]


def _set_sub_rows(ref, r, dil, cols, value):
    if dil == 1:
        ref[:, cols] = value
    else:
        ref[pl.ds(r, ATT_BLOCK, stride=dil), cols] = value


def _step_slope(group, hp, hh):
    if hp == DIL_HEADS:
        return _alibi_slope(group, hh)
    hb = pl.program_id(0)
    slope = _alibi_slope(group, DIL_HEADS - 1)
    for h in reversed(range(DIL_HEADS - 1)):
        slope = jnp.where(hb == h, _alibi_slope(group, h), slope)
    return slope


def _window_bias(dil, n):
    a = lax.broadcasted_iota(jnp.int32, (ATT_BLOCK, 2 * ATT_BLOCK), 0)
    b = lax.broadcasted_iota(jnp.int32, (ATT_BLOCK, 2 * ATT_BLOCK), 1)
    dist = ATT_BLOCK + a - b
    valid = (dist >= 0) & (dist <= ATT_BLOCK) & ((b >= ATT_BLOCK) | (n > 0))
    return (dist * dil).astype(F32), valid


def _attn_fwd(qb, kb, vb, group):
    s = qb.shape[0]
    dil, nb, hp = _attn_plan(s, group)
    qkv, per_head = _attn_specs(group, dil, nb, hp)

    def body(q_ref, kp_ref, kc_ref, vp_ref, vc_ref, o_ref, lse_ref):
        n = pl.program_id(1)
        distd, valid = _window_bias(dil, n)
        for hh in range(hp):
            cols = _head_cols(hh)
            slope = _step_slope(group, hp, hh)
            for r in range(dil):
                sub = lambda ref: _sub_rows(ref, r, dil, cols).astype(MXU)
                kk = jnp.concatenate([sub(kp_ref), sub(kc_ref)], axis=0)
                vv = jnp.concatenate([sub(vp_ref), sub(vc_ref)], axis=0)
                sc = _dot_nt(sub(q_ref), kk) * DIL_DH ** -0.5 - slope * distd
                sc = jnp.where(valid, sc, -1e30)
                mx = jnp.max(sc, axis=-1, keepdims=True)
                p = jnp.where(valid, jnp.exp(sc - mx), 0.0)
                den = jnp.sum(p, axis=-1, keepdims=True)
                _set_sub_rows(o_ref, r, dil, cols, _dot(p, vv) / den)
                _set_sub_rows(lse_ref, r, dil, cols, jnp.broadcast_to(mx + jnp.log(den), (ATT_BLOCK, LANES)))

    return _pcall(
        body, name=f"attn_fwd_g{group}", grid=(DIL_HEADS // hp, nb),
        in_specs=[qkv(0), qkv(-1), qkv(0), qkv(-1), qkv(0)], out_specs=[per_head(0)] * 2,
        out_shape=[jax.ShapeDtypeStruct((s, DIL_W), F32)] * 2,
        compiler_params=_params("parallel", "parallel"),
    )(qb, kb, kb, vb, vb)


def _attn_bwd(qb, kb, vb, d_o, lse, delta, group):
    s = qb.shape[0]
    dil, nb, hp = _attn_plan(s, group)
    qkv, per_head = _attn_specs(group, dil, nb, hp)
    scale = DIL_DH ** -0.5

    def body(q_ref, kp_ref, kc_ref, vp_ref, vc_ref, do_ref, l_ref, dl_ref, dq_ref, dk_ref, dv_ref,
             dq_acc, dk_done, dv_done, dk_carry, dv_carry):
        n = pl.program_id(1)
        slopes = [_step_slope(group, hp, hh) for hh in range(hp)]

        @pl.when(n == 0)
        def _():
            dk_carry[...] = jnp.zeros_like(dk_carry)
            dv_carry[...] = jnp.zeros_like(dv_carry)

        @pl.when(n < nb)
        def _():
            distd, valid = _window_bias(dil, n)
            for hh in range(hp):
                cols = _head_cols(hh)
                slope = slopes[hh]
                for r in range(dil):
                    sub = lambda ref: _sub_rows(ref, r, dil, cols)
                    qc, do = sub(q_ref).astype(MXU), sub(do_ref).astype(MXU)
                    kk = jnp.concatenate([sub(kp_ref).astype(MXU), sub(kc_ref).astype(MXU)], axis=0)
                    vv = jnp.concatenate([sub(vp_ref).astype(MXU), sub(vc_ref).astype(MXU)], axis=0)
                    sc = _dot_nt(qc, kk) * scale - slope * distd
                    p = jnp.where(valid, jnp.exp(jnp.minimum(sc - jnp.concatenate([sub(l_ref)] * 2, axis=1), 0.0)), 0.0)
                    dsc = p * (_dot_nt(do, vv) - jnp.concatenate([sub(dl_ref)] * 2, axis=1))
                    _set_sub_rows(dq_acc, r, dil, cols, _dot(dsc, kk) * scale)
                    dkk = _dot_tn(dsc, qc) * scale
                    dvv = _dot_tn(p, do)
                    _set_sub_rows(dk_done, r, dil, cols, _sub_rows(dk_carry, r, dil, cols) + dkk[:ATT_BLOCK])
                    _set_sub_rows(dv_done, r, dil, cols, _sub_rows(dv_carry, r, dil, cols) + dvv[:ATT_BLOCK])
                    _set_sub_rows(dk_carry, r, dil, cols, dkk[ATT_BLOCK:])
                    _set_sub_rows(dv_carry, r, dil, cols, dvv[ATT_BLOCK:])
            dq_ref[...] = dq_acc[...].astype(dq_ref.dtype)
            dk_ref[...] = dk_done[...].astype(dk_ref.dtype)
            dv_ref[...] = dv_done[...].astype(dv_ref.dtype)

        @pl.when(n == nb)
        def _():
            dk_ref[...] = dk_carry[...].astype(dk_ref.dtype)
            dv_ref[...] = dv_carry[...].astype(dv_ref.dtype)

    return _pcall(
        body, name=f"attn_bwd_g{group}", grid=(DIL_HEADS // hp, nb + 1),
        in_specs=[qkv(0), qkv(-1), qkv(0), qkv(-1), qkv(0)] + [per_head(0)] * 3,
        out_specs=[per_head(0), per_head(-1), per_head(-1)],
        out_shape=[jax.ShapeDtypeStruct((s, DIL_W), MXU)] * 3,
        scratch_shapes=[pltpu.VMEM((ATT_BLOCK * dil, hp * LANES), F32)] * 5,
        compiler_params=_params("parallel", "arbitrary"),
    )(qb, kb, kb, vb, vb, d_o, lse, delta)


def _my_place():
    mx, my, mc = lax.axis_index("x"), lax.axis_index("y"), lax.axis_index("c")
    return mx, my, mc, 4 * mx + 2 * my + mc


N_CHIPS = 4


def _shard_row_tile(r):
    if r <= 512:
        return r
    return 128 if r % 128 == 0 else 480


def _other_chips(mx, my):
    return [(1 - mx, my), (mx, 1 - my), (1 - mx, 1 - my)]


def _all_gather(xs, name):
    n = len(xs)

    def body(*refs):
        x_refs, o_refs = refs[:n], refs[n:2 * n]
        send_sems, recv_sems, local_sems = refs[2 * n:]
        mx, my, mc, me = _my_place()
        sibling, sibling_id = (mx, my, 1 - mc), 4 * mx + 2 * my + (1 - mc)
        chips = _other_chips(mx, my)

        def copy(a, k, slot, to, src=None):
            dst = o_refs[a].at[slot]
            return pltpu.make_async_remote_copy(
                src_ref=dst if src is None else src, dst_ref=dst, send_sem=send_sems.at[a, k],
                recv_sem=recv_sems.at[a, k], device_id=to, device_id_type=MESH)

        local = [pltpu.make_async_copy(x_refs[a], o_refs[a].at[me], local_sems.at[a]) for a in range(n)]
        for cp in local:
            cp.start()
        sends = []
        for a in range(n):
            sends.append(copy(a, 0, me, sibling, src=x_refs[a]))
            sends += [copy(a, 1 + j, me, (px, py, mc), src=x_refs[a]) for j, (px, py) in enumerate(chips)]
        for cp in sends:
            cp.start()
        for j, (px, py) in enumerate(chips):
            slot = 4 * px + 2 * py + mc
            for a in range(n):
                copy(a, 1 + j, slot, (px, py, mc)).wait_recv()
                passed = copy(a, 4 + j, slot, sibling)
                passed.start()
                sends.append(passed)
        for a in range(n):
            copy(a, 0, sibling_id, sibling).wait_recv()
            for j, (px, py) in enumerate(chips):
                copy(a, 4 + j, 4 * px + 2 * py + (1 - mc), sibling).wait_recv()
        for cp in sends:
            cp.wait_send()
        for cp in local:
            cp.wait()

    any_spec = pl.BlockSpec(memory_space=pl.ANY)
    return _pcall(
        body, name=name,
        in_specs=[any_spec] * n, out_specs=[any_spec] * n,
        out_shape=[jax.ShapeDtypeStruct((N_DEV,) + x.shape, x.dtype) for x in xs],
        scratch_shapes=[pltpu.SemaphoreType.DMA((n, N_DEV - 1)), pltpu.SemaphoreType.DMA((n, N_DEV - 1)),
                        pltpu.SemaphoreType.DMA((n,))],
    )(*xs)


def _pair_exchange(gs, name):
    n = len(gs)

    def body(*refs):
        g_refs, o_refs = refs[:n], refs[n:2 * n]
        send_sems, recv_sems = refs[2 * n:]
        mx, my, mc, _ = _my_place()
        copies = [pltpu.make_async_remote_copy(
            src_ref=g_refs[a].at[1 - mc], dst_ref=o_refs[a], send_sem=send_sems.at[a], recv_sem=recv_sems.at[a],
            device_id=(mx, my, 1 - mc), device_id_type=MESH) for a in range(n)]
        for cp in copies:
            cp.start()
        for cp in copies:
            cp.wait()

    any_spec = pl.BlockSpec(memory_space=pl.ANY)
    return _pcall(
        body, name=name,
        in_specs=[any_spec] * n, out_specs=[any_spec] * n,
        out_shape=[jax.ShapeDtypeStruct(g.shape[1:], g.dtype) for g in gs],
        scratch_shapes=[pltpu.SemaphoreType.DMA((n,)), pltpu.SemaphoreType.DMA((n,))],
    )(*gs)


def _pair_add(g, other, name):
    _, chips, r, c = g.shape
    tr = _shard_row_tile(r)
    core = lax.axis_index("c").astype(jnp.int32).reshape(1)

    def body(core_ref, g_ref, o_ref, h_ref):
        h_ref[...] = (g_ref[...].astype(F32)[0] + o_ref[...].astype(F32)).astype(h_ref.dtype)

    blk = pl.BlockSpec((1, tr, c), lambda p, i, core_ref: (p, i, 0))
    return _pcall(
        body, name=name,
        grid_spec=pltpu.PrefetchScalarGridSpec(
            num_scalar_prefetch=1, grid=(chips, pl.cdiv(r, tr)),
            in_specs=[pl.BlockSpec((1, 1, tr, c), lambda p, i, core_ref: (core_ref[0], p, i, 0)), blk],
            out_specs=blk),
        out_shape=jax.ShapeDtypeStruct((chips, r, c), g.dtype),
        compiler_params=_params("parallel", "parallel"),
    )(core, g, other)


def _chip_exchange(hs, name):
    n = len(hs)

    def body(*refs):
        h_refs, o_refs = refs[:n], refs[n:2 * n]
        send_sems, recv_sems, local_sems = refs[2 * n:]
        mx, my, mc, _ = _my_place()
        my_chip = 2 * mx + my
        chips = _other_chips(mx, my)
        local = [pltpu.make_async_copy(h_refs[a].at[my_chip], o_refs[a].at[my_chip], local_sems.at[a]) for a in range(n)]
        for cp in local:
            cp.start()
        for j, (px, py) in enumerate(chips):
            for a in range(n):
                pltpu.make_async_remote_copy(
                    src_ref=h_refs[a].at[2 * px + py], dst_ref=o_refs[a].at[my_chip], send_sem=send_sems.at[a, j],
                    recv_sem=recv_sems.at[a, j], device_id=(px, py, mc), device_id_type=MESH).start()
        for j, (px, py) in enumerate(chips):
            for a in range(n):
                pltpu.make_async_remote_copy(
                    src_ref=h_refs[a].at[2 * px + py], dst_ref=o_refs[a].at[2 * px + py], send_sem=send_sems.at[a, j],
                    recv_sem=recv_sems.at[a, j], device_id=(px, py, mc), device_id_type=MESH).wait()
        for cp in local:
            cp.wait()

    any_spec = pl.BlockSpec(memory_space=pl.ANY)
    return _pcall(
        body, name=name,
        in_specs=[any_spec] * n, out_specs=[any_spec] * n,
        out_shape=[jax.ShapeDtypeStruct(h.shape, h.dtype) for h in hs],
        scratch_shapes=[pltpu.SemaphoreType.DMA((n, N_CHIPS - 1)), pltpu.SemaphoreType.DMA((n, N_CHIPS - 1)),
                        pltpu.SemaphoreType.DMA((n,))],
    )(*hs)


def _adamw(parts, w, m, v, name):
    r, c = w.shape
    n_parts = parts.shape[0]
    tr = _shard_row_tile(r)
    bc1 = 1.0 - ADAM_B1 ** ADAM_STEP
    bc2 = 1.0 - ADAM_B2 ** ADAM_STEP

    def body(p_ref, w_ref, m_ref, v_ref, g_ref, d_ref, nm_ref, nv_ref):
        g = p_ref[0].astype(F32)
        for j in range(1, n_parts):
            g = g + p_ref[j].astype(F32)
        nm = ADAM_B1 * m_ref[...] + (1.0 - ADAM_B1) * g
        nv = ADAM_B2 * v_ref[...] + (1.0 - ADAM_B2) * (g * g)
        g_ref[...] = g
        nm_ref[...] = nm
        nv_ref[...] = nv
        d_ref[...] = -ADAM_LR * ((nm / bc1) / (jnp.sqrt(nv / bc2) + ADAM_EPS) + ADAM_WD * w_ref[...])

    blk = pl.BlockSpec((tr, c), lambda i: (i, 0))
    return _pcall(
        body, name=name, grid=(pl.cdiv(r, tr),),
        in_specs=[pl.BlockSpec((n_parts, tr, c), lambda i: (0, i, 0)), blk, blk, blk],
        out_specs=[blk] * 4, out_shape=[jax.ShapeDtypeStruct((r, c), F32)] * 4,
        compiler_params=_params("parallel"),
    )(parts, w, m, v)


def _local_step(x, target, norm_w, w_segs, conv_w, a_log, dt_bias, dn_norm_w, w_o_dn, w_o_dil, w_out, final_norm_w):
    s = x.shape[0]
    w_qkv, w_za, w_ba, w_qb, w_kb, w_vb, w_zb, w_ga, w_gb = w_segs
    conv_w8 = jnp.concatenate([conv_w, jnp.zeros((SUBLANES - conv_w.shape[0], QKV_W), F32)], axis=0)
    pad8 = jnp.zeros((1, DN_HEADS), F32)
    alog_row = jnp.concatenate([pad8, a_log, jnp.zeros((1, LANES - 2 * DN_HEADS), F32)], axis=1)
    dtb_row = jnp.concatenate([pad8, dt_bias, jnp.zeros((1, LANES - 2 * DN_HEADS), F32)], axis=1)
    wf_row = final_norm_w.reshape(1, D_MODEL)

    hb = _rms_in_fwd(x, norm_w)
    qkv_pre, z_a, ba, z_b = _mm_out(hb, [w_qkv, w_za, w_ba, w_zb], "proj_fwd_a", w_is_out_by_in=True)
    q_b, k_b, v_b, g_a, g_b = _mm_out(hb, [w_qb, w_kb, w_vb, w_ga, w_gb], "proj_fwd_b", w_is_out_by_in=True)

    qn, kn, vn, bg = _dn_prep_fwd(qkv_pre, ba, conv_w8, alog_row, dtb_row)
    u_d, w_d, qd_d, kd_d, aqk_d, dl_d, t2_d = _delta_prep(qn, kn, vn, bg)
    o_a, vnew_d, st_d = _delta_scan_fwd(u_d, w_d, qd_d, kd_d, aqk_d, dl_d)
    on_b, y_a = _dn_out_fwd(o_a, z_a, dn_norm_w, w_o_dn)

    parts, lses = [], []
    for gi in range(N_DIL):
        o_g, l_g = _attn_fwd(q_b, k_b, v_b, gi)
        parts.append(o_g)
        lses.append(l_g)
    lse, o_joint, ob_b, y_b = _attn_out_fwd(parts, lses, z_b, w_o_dil)

    loss8, dwf8, merged_b, dx2_b, dx2, dya_b, dyb_b, dga_b, dgb_b = _merge_out_final(
        g_a, g_b, y_a, y_b, x, target, w_out, wf_row)

    g_w_out = _mm_tn(merged_b, dx2_b, "out_wgrad")
    g_w_o_dn = _mm_tn(on_b, dya_b, "out_dn_wgrad")
    d_o_a, dza_b, ddnw8 = _dn_out_bwd(dya_b, o_a, z_a, dn_norm_w, w_o_dn)

    g_w_o_dil = _mm_tn(ob_b, dyb_b, "out_dil_wgrad")
    d_o, dzb_b, delta = _attn_out_bwd(dyb_b, o_joint, z_b, w_o_dil)
    dqs, dks, dvs = [], [], []
    for gi in range(N_DIL):
        dq_g, dk_g, dv_g = _attn_bwd(q_b, k_b, v_b, d_o, lse, delta, gi)
        dqs.append(dq_g)
        dks.append(dk_g)
        dvs.append(dv_g)

    dvnew_d, dkd_d, ddl_d = _delta_scan_bwd(w_d, qd_d, kd_d, aqk_d, dl_d, vnew_d, st_d, d_o_a)
    dqn, dkn, dvn, dbg = _delta_post_bwd(qn, kn, vn, bg, t2_d, st_d, vnew_d, d_o_a, dvnew_d, dkd_d, ddl_d)
    dc, dba_b, dsmall8 = _dn_prep_bwd(qkv_pre, ba, conv_w8, alog_row, dtb_row, dqn, dkn, dvn, dbg)
    dqkv_b, dconv8 = _conv_bwd(dc, qkv_pre, conv_w8)

    per_group = lambda w: [w[g * DIL_W:(g + 1) * DIL_W] for g in range(N_DIL)]
    dh_b = _mm_in(dqs + dks + dvs + [dga_b, dgb_b],
                  per_group(w_qb) + per_group(w_kb) + per_group(w_vb) + [w_ga, w_gb], "proj_bwd_b", w_is_out_by_in=True)
    dsegs = [dqkv_b, dza_b, dba_b] + dqs + dks + dvs + [dzb_b, dga_b, dgb_b]
    g_segs = [_mm_tn(d, hb, f"proj_wgrad_{j}") for j, d in enumerate(dsegs)]
    grad_x, dnw8 = _proj_bwd_rms_in([dqkv_b, dza_b, dba_b, dzb_b], [w_qkv, w_za, w_ba, w_zb], dh_b, x, dx2, norm_w)

    small = dict(norm_w=dnw8[0:1], final_norm_w=dwf8[0:1], dn_norm_w=ddnw8[0:1],
                 a_log=dsmall8[0:1, DN_HEADS:2 * DN_HEADS], dt_bias=dsmall8[1:2, DN_HEADS:2 * DN_HEADS])
    return loss8[0:1, 0:1], grad_x, g_segs, dconv8[0:4], g_w_o_dn, g_w_o_dil, g_w_out, small


def _proj_bwd_rms_in(ds, ws, dh_a, x, dx2, norm_w):
    n_seg = len(ds)

    def body(*refs):
        d_refs, w_refs = refs[:n_seg], refs[n_seg:2 * n_seg]
        da_ref, x_ref, dx2_ref, w_ref, dx_ref, dw_ref = refs[2 * n_seg:]
        dx_ref[...] = da_ref[...]
        for d_ref, wt_ref in zip(d_refs, w_refs):
            for c, wd in _col_chunks(d_ref.shape[1], 1024):
                dx_ref[...] += jnp.dot(d_ref[:, c:c + wd], wt_ref[c:c + wd, :], preferred_element_type=F32)
        xv = x_ref[...]
        r = lax.rsqrt(jnp.mean(xv * xv, axis=-1, keepdims=True) + NORM_EPS)
        dhv = dx_ref[...]
        dn = dhv * w_ref[...]
        dx_ref[...] = dx2_ref[...] + r * dn - xv * (r * r * r) * jnp.mean(dn * xv, axis=-1, keepdims=True)
        row = jnp.sum(dhv * xv * r, axis=0, keepdims=True)
        _acc_add(dw_ref, jnp.concatenate([row, jnp.zeros((SUBLANES - 1, row.shape[1]), F32)], axis=0))

    return _rows_call(body, "proj_bwd_b_rms_in", x.shape[0],
                      [(d, "tile") for d in ds] + [(w, "full") for w in ws]
                      + [(dh_a, "tile"), (x, "tile"), (dx2, "tile"), (norm_w, "full")],
                      [(x.shape, F32, "tile"), ((SUBLANES, x.shape[1]), F32, "acc")])


def _split_proj_rows(wt_full):
    offs = [0]
    for n in PROJ_SIZES:
        offs.append(offs[-1] + n)
    seg = lambda a, b: wt_full[offs[a]:offs[b]]
    w_ba = jnp.concatenate([seg(4, 6), jnp.zeros((LANES - 2 * DN_HEADS, wt_full.shape[1]), wt_full.dtype)], axis=0)
    return [seg(0, 3), seg(3, 4), w_ba, seg(6, 7), seg(7, 8), seg(8, 9), seg(9, 10), seg(10, 11), seg(11, 12)]


def _join_proj_rows(g_segs):
    parts = list(g_segs)
    parts[2] = parts[2][:2 * DN_HEADS]
    return jnp.concatenate(parts, axis=0)


def _pack_small(norm_w, final_norm_w, dn_norm_w, a_log, dt_bias):
    pad = lambda r: jnp.concatenate([r, jnp.zeros((1, D_MODEL - r.shape[1]), F32)], axis=1)
    rows = [pad(norm_w.reshape(1, -1)), pad(final_norm_w.reshape(1, -1)), pad(dn_norm_w.reshape(1, -1)),
            pad(a_log.reshape(1, -1)), pad(dt_bias.reshape(1, -1)), jnp.zeros((SUBLANES - 5, D_MODEL), F32)]
    return jnp.concatenate(rows, axis=0)


def _unpack_small(p):
    return dict(norm_w=p[0:1], final_norm_w=p[1], dn_norm_w=p[2:3, :DN_DK], a_log=p[3:4, :DN_HEADS],
                dt_bias=p[4:5, :DN_HEADS])


def kernel(x, norm_w, w_in, conv_w, a_log, dt_bias, dn_norm_w, w_o_dn, w_o_dil, w_out, final_norm_w, loss_target, m_norm_w, m_w_in, m_conv_w, m_a_log, m_dt_bias, m_dn_norm_w, m_w_o_dn, m_w_o_dil, m_w_out, m_final_norm_w, v_norm_w, v_w_in, v_conv_w, v_a_log, v_dt_bias, v_dn_norm_w, v_w_o_dn, v_w_o_dil, v_w_out, v_final_norm_w):
    shard_w = w_in.shape[2]
    wt, m_wt, v_wt = (jnp.transpose(t[0]) for t in (w_in, m_w_in, v_w_in))
    gathered = _all_gather([wt.astype(MXU), w_o_dn[0].astype(MXU), w_o_dil[0].astype(MXU), w_out[0].astype(MXU),
                            conv_w[0]], "gather_weights")
    w_in_all, w_o_dn_all, w_o_dil_all, w_out_all, conv_all = gathered
    wt_full = w_in_all.reshape(N_DEV * shard_w, D_MODEL)
    w_o_dn_full = w_o_dn_all.reshape(D_MODEL, D_MODEL)
    w_o_dil_full = jnp.transpose(w_o_dil_all, (1, 0, 2)).reshape(DIL_W, D_MODEL)
    w_out_full = w_out_all.reshape(D_MODEL, D_MODEL)
    conv_full = jnp.transpose(conv_all, (1, 0, 2)).reshape(conv_w.shape[1], QKV_W)

    loss11, grad_x, g_segs, g_conv, g_w_o_dn, g_w_o_dil, g_w_out, small = _local_step(
        x[0], loss_target[0], norm_w, _split_proj_rows(wt_full), conv_full, a_log, dt_bias, dn_norm_w,
        w_o_dn_full, w_o_dil_full, w_out_full, final_norm_w)

    col_shards = lambda g, n: jnp.transpose(g.reshape(g.shape[0], N_DEV, n), (1, 0, 2))
    row_shards = lambda g: g.reshape(N_DEV, g.shape[0] // N_DEV, g.shape[1])
    sent = [row_shards(_join_proj_rows(g_segs)).astype(MXU), row_shards(g_w_o_dn).astype(MXU),
            col_shards(g_w_o_dil, w_o_dil.shape[2]).astype(MXU), row_shards(g_w_out).astype(MXU),
            col_shards(g_conv, conv_w.shape[2])]
    by_core = lambda g8: jnp.transpose(g8.reshape((N_CHIPS, 2) + g8.shape[1:]), (1, 0, 2, 3))
    sent = [by_core(g8) for g8 in sent]
    from_sibling = _pair_exchange(sent, "scatter_pair")
    summed = [_pair_add(g, o, f"pair_add_{i}") for i, (g, o) in enumerate(zip(sent, from_sibling))]
    p_w_in, p_w_o_dn, p_w_o_dil, p_w_out, p_conv = _chip_exchange(summed, "scatter_chips")
    p_small = _all_gather([_pack_small(small["norm_w"], small["final_norm_w"], small["dn_norm_w"], small["a_log"],
                                       small["dt_bias"])], "gather_small_grads")[0]

    res = {}
    res["w_in"] = [jnp.transpose(t) for t in _adamw(p_w_in, wt, m_wt, v_wt, "adamw_w_in")]
    res["conv_w"] = _adamw(p_conv, conv_w[0], m_conv_w[0], v_conv_w[0], "adamw_conv_w")
    res["w_o_dn"] = _adamw(p_w_o_dn, w_o_dn[0], m_w_o_dn[0], v_w_o_dn[0], "adamw_w_o_dn")
    res["w_o_dil"] = _adamw(p_w_o_dil, w_o_dil[0], m_w_o_dil[0], v_w_o_dil[0], "adamw_w_o_dil")
    res["w_out"] = _adamw(p_w_out, w_out[0], m_w_out[0], v_w_out[0], "adamw_w_out")
    small_res = _adamw(p_small, _pack_small(norm_w, final_norm_w, dn_norm_w, a_log, dt_bias),
                       _pack_small(m_norm_w, m_final_norm_w, m_dn_norm_w, m_a_log, m_dt_bias),
                       _pack_small(v_norm_w, v_final_norm_w, v_dn_norm_w, v_a_log, v_dt_bias), "adamw_small")
    small_res = [_unpack_small(t) for t in small_res]

    loss = lax.psum(loss11[0, 0], ("x", "y", "c"))
    names = ["norm_w", "w_in", "conv_w", "a_log", "dt_bias", "dn_norm_w", "w_o_dn", "w_o_dil", "w_out", "final_norm_w"]
    outs = [loss, grad_x[None]]
    for kind in range(4):
        for nm in names:
            outs.append(res[nm][kind][None] if nm in res else small_res[kind][nm])
    return tuple(outs)
```

```python
import math

import jax
import jax.numpy as jnp
from jax import lax
from jax.experimental import pallas as pl
from jax.experimental.pallas import tpu as pltpu

F32 = jnp.float32
MXU = jnp.bfloat16
MESH = pl.DeviceIdType.MESH

N_DEV = 8
D_MODEL = 1024
DN_HEADS = 8
DN_DK = 128
DN_CHUNK = 64
N_DIL = 3
DIL_HEADS = 4
DIL_DH = 128
DIL_W = DIL_HEADS * DIL_DH
DIL_GROUPS = ((128, 1), (512, 4), (2048, 16))
ATT_BLOCK = 128
NORM_EPS = 1e-6
QKV_W = 3 * D_MODEL
DILQ_W = N_DIL * DIL_W
PROJ_SIZES = (1024, 1024, 1024, 1024, 8, 8, DILQ_W, DILQ_W, DILQ_W, DIL_W, D_MODEL, D_MODEL)

ADAM_LR = 0.001
ADAM_B1 = 0.9
ADAM_B2 = 0.999
ADAM_EPS = 1e-08
ADAM_WD = 0.01
ADAM_STEP = 10

ROW_TILE = 256
LANES = 128
SUBLANES = 8
VMEM_LIMIT = 48 << 20


def _pcall(body, **kw):
    return pl.pallas_call(body, **kw)


def _params(*sem):
    return pltpu.CompilerParams(dimension_semantics=tuple(sem), vmem_limit_bytes=VMEM_LIMIT)


def _sigmoid(x):
    return 1.0 / (1.0 + jnp.exp(-x))


def _softplus(x):
    return jnp.maximum(x, 0.0) + jnp.log(1.0 + jnp.exp(-jnp.abs(x)))


def _dot(a, b):
    return jnp.dot(a.astype(MXU), b.astype(MXU), preferred_element_type=F32)


def _dot_nt(a, b):
    return lax.dot_general(a.astype(MXU), b.astype(MXU), (((1,), (1,)), ((), ())), preferred_element_type=F32)


def _dot_tn(a, b):
    return lax.dot_general(a.astype(MXU), b.astype(MXU), (((0,), (0,)), ((), ())), preferred_element_type=F32)


def _split3(x):
    hi = x.astype(jnp.bfloat16)
    r1 = x - hi.astype(F32)
    mid = r1.astype(jnp.bfloat16)
    lo = (r1 - mid.astype(F32)).astype(jnp.bfloat16)
    return hi, mid, lo


def _dot01(m01, x):
    m = m01.astype(jnp.bfloat16)
    hi, mid, lo = _split3(x)
    f = lambda p: jnp.dot(m, p, preferred_element_type=F32)
    return f(hi) + (f(mid) + f(lo))


def _rows_call(body, name, n_rows, ins, outs, scratch=(), tm=ROW_TILE):
    steps = n_rows // tm
    per8 = tm // SUBLANES
    last8 = n_rows // SUBLANES - 1
    in_specs = []
    for arr, kind in ins:
        cols = arr.shape[-1]
        if kind == "tile":
            in_specs.append(pl.BlockSpec((tm, cols), lambda i: (i, 0)))
        elif kind == "full":
            in_specs.append(pl.BlockSpec(arr.shape, lambda i, nd=arr.ndim: (0,) * nd))
        elif kind == "prev8":
            in_specs.append(pl.BlockSpec((SUBLANES, cols), lambda i: (jnp.maximum(i * per8 - 1, 0), 0)))
        elif kind == "next8":
            in_specs.append(pl.BlockSpec((SUBLANES, cols), lambda i: (jnp.minimum((i + 1) * per8, last8), 0)))
        else:
            raise ValueError(kind)
    out_specs, out_shape, has_acc = [], [], False
    for shape, dtype, kind in outs:
        out_shape.append(jax.ShapeDtypeStruct(shape, dtype))
        if kind == "tile":
            out_specs.append(pl.BlockSpec((tm, shape[-1]), lambda i: (i, 0)))
        else:
            has_acc = True
            out_specs.append(pl.BlockSpec(shape, lambda i: (0, 0)))
    return _pcall(
        body, name=name, grid=(steps,), in_specs=in_specs, out_specs=out_specs, out_shape=out_shape,
        scratch_shapes=list(scratch),
        compiler_params=_params("arbitrary" if has_acc else "parallel"),
    )(*[a for a, _ in ins])


def _acc_add(ref, value):
    @pl.when(pl.program_id(0) == 0)
    def _():
        ref[...] = jnp.zeros_like(ref)
    ref[...] += value


def _col_chunks(n, width=512):
    return [(c, min(width, n - c)) for c in range(0, n, width)]


NT_DIMS = (((1,), (1,)), ((), ()))
TN_DIMS = (((0,), (0,)), ((), ()))


def _mm_out(a, ws, name, w_is_out_by_in=False, out_dtype=F32, tm=ROW_TILE):
    m, k = a.shape
    ns = [w.shape[0] if w_is_out_by_in else w.shape[1] for w in ws]

    def body(a_ref, *refs):
        av = a_ref[...]
        for w_ref, o_ref, n in zip(refs[:len(ws)], refs[len(ws):], ns):
            for c, wd in _col_chunks(n):
                if w_is_out_by_in:
                    part = lax.dot_general(av, w_ref[c:c + wd, :], NT_DIMS, preferred_element_type=F32)
                else:
                    part = jnp.dot(av, w_ref[:, c:c + wd], preferred_element_type=F32)
                o_ref[:, c:c + wd] = part.astype(o_ref.dtype)

    return _pcall(
        body, name=name, grid=(m // tm,),
        in_specs=[pl.BlockSpec((tm, k), lambda i: (i, 0))] + [pl.BlockSpec(w.shape, lambda i: (0, 0)) for w in ws],
        out_specs=[pl.BlockSpec((tm, n), lambda i: (i, 0)) for n in ns],
        out_shape=[jax.ShapeDtypeStruct((m, n), out_dtype) for n in ns],
        compiler_params=_params("parallel"),
    )(a, *ws)


def _mm_in(ds, ws, name, w_is_out_by_in=False, tm=ROW_TILE):
    m = ds[0].shape[0]
    k = ws[0].shape[1] if w_is_out_by_in else ws[0].shape[0]
    ns = [d.shape[1] for d in ds]

    def body(*refs):
        d_refs, w_refs, o_ref = refs[:len(ds)], refs[len(ds):2 * len(ds)], refs[-1]
        first = True
        for d_ref, w_ref, n in zip(d_refs, w_refs, ns):
            for c, wd in _col_chunks(n, 1024):
                if w_is_out_by_in:
                    part = jnp.dot(d_ref[:, c:c + wd], w_ref[c:c + wd, :], preferred_element_type=F32)
                else:
                    part = lax.dot_general(d_ref[:, c:c + wd], w_ref[:, c:c + wd], NT_DIMS, preferred_element_type=F32)
                if first:
                    o_ref[...] = part
                    first = False
                else:
                    o_ref[...] += part

    return _pcall(
        body, name=name, grid=(m // tm,),
        in_specs=[pl.BlockSpec((tm, n), lambda i: (i, 0)) for n in ns] + [pl.BlockSpec(w.shape, lambda i: (0, 0)) for w in ws],
        out_specs=pl.BlockSpec((tm, k), lambda i: (i, 0)),
        out_shape=jax.ShapeDtypeStruct((m, k), F32),
        compiler_params=_params("parallel"),
    )(*ds, *ws)


def _mm_tn(a, d, name):
    m, k = a.shape
    n = d.shape[1]
    tk = 512 if k % 512 == 0 else k

    def body(a_ref, d_ref, o_ref):
        o_ref[...] = lax.dot_general(a_ref[...], d_ref[...], TN_DIMS, preferred_element_type=F32)

    return _pcall(
        body, name=name, grid=(k // tk,),
        in_specs=[pl.BlockSpec((m, tk), lambda p: (0, p)), pl.BlockSpec((m, n), lambda p: (0, 0))],
        out_specs=pl.BlockSpec((tk, n), lambda p: (p, 0)),
        out_shape=jax.ShapeDtypeStruct((k, n), F32),
        compiler_params=_params("parallel"),
    )(a, d)


WGRAD_TILE = 512


def _proj_wgrad_all(dsegs, valid_rows, hb):
    m, k = hb.shape
    n_seg = len(dsegs)
    tiles, row = [], 0
    for si, (d, valid) in enumerate(zip(dsegs, valid_rows)):
        for c in range(0, valid, WGRAD_TILE):
            width = min(WGRAD_TILE, d.shape[1] - c)
            tiles.append((si, c, width, row + c, min(width, valid - c)))
        row += valid
    total_rows = row

    def body(*refs):
        d_refs, hb_ref, o_ref = refs[:n_seg], refs[n_seg], refs[n_seg + 1]
        a_buf, hb_buf, o_buf, load_sems, store_sems, hb_sem = refs[n_seg + 2:]

        def load(t):
            si, c, width, _, _ = tiles[t]
            return pltpu.make_async_copy(d_refs[si].at[:, pl.ds(c, width)], a_buf.at[t % 2, :, pl.ds(0, width)],
                                         load_sems.at[t % 2])

        def store(t):
            _, _, _, orow, valid = tiles[t]
            return pltpu.make_async_copy(o_buf.at[t % 2, pl.ds(0, valid), :], o_ref.at[pl.ds(orow, valid), :],
                                         store_sems.at[t % 2])

        hb_copy = pltpu.make_async_copy(hb_ref, hb_buf, hb_sem)
        hb_copy.start()
        load(0).start()
        hb_copy.wait()
        for t in range(len(tiles)):
            width = tiles[t][2]
            load(t).wait()
            if t + 1 < len(tiles):
                load(t + 1).start()
            if t >= 2:
                store(t - 2).wait()
            o_buf[t % 2, 0:width, :] = lax.dot_general(a_buf[t % 2, :, 0:width], hb_buf[...], TN_DIMS,
                                                        preferred_element_type=F32)
            store(t).start()
        for t in range(max(len(tiles) - 2, 0), len(tiles)):
            store(t).wait()

    any_spec = pl.BlockSpec(memory_space=pl.ANY)
    return _pcall(
        body, name="proj_wgrad",
        in_specs=[any_spec] * (n_seg + 1), out_specs=any_spec,
        out_shape=jax.ShapeDtypeStruct((total_rows, k), F32),
        scratch_shapes=[pltpu.VMEM((2, m, WGRAD_TILE), hb.dtype), pltpu.VMEM((m, k), hb.dtype),
                        pltpu.VMEM((2, WGRAD_TILE, k), F32), pltpu.SemaphoreType.DMA((2,)),
                        pltpu.SemaphoreType.DMA((2,)), pltpu.SemaphoreType.DMA],
        compiler_params=pltpu.CompilerParams(vmem_limit_bytes=VMEM_LIMIT),
    )(*dsegs, hb)


def _rms_in_fwd(x, norm_w):
    def body(x_ref, w_ref, h_ref):
        xv = x_ref[...]
        r = lax.rsqrt(jnp.mean(xv * xv, axis=-1, keepdims=True) + NORM_EPS)
        h_ref[...] = (xv * r * w_ref[...]).astype(h_ref.dtype)

    return _rows_call(body, "rms_in_fwd", x.shape[0], [(x, "tile"), (norm_w, "full")],
                      [(x.shape, MXU, "tile")])[0]


def _conv_taps(ext_ref, cw_ref, cols, tm):
    c = None
    for j in range(4):
        term = cw_ref[3 - j:4 - j, cols] * ext_ref[SUBLANES - j:SUBLANES - j + tm, cols]
        c = term if c is None else c + term
    return c


def _fill_ext(ext_ref, u_ref, halo_ref, first):
    ext_ref[0:SUBLANES, :] = jnp.where(first, 0.0, halo_ref[...])
    ext_ref[SUBLANES:, :] = u_ref[...]


def _dn_prep_fwd(qkv_pre, ba, conv_w8, alog_row, dtb_row):
    s = qkv_pre.shape[0]
    tm = ROW_TILE

    def body(u_ref, halo_ref, cw_ref, ba_ref, al_ref, dtb_ref, q_ref, k_ref, v_ref, bg_ref, ext_ref):
        _fill_ext(ext_ref, u_ref, halo_ref, pl.program_id(0) == 0)
        for h in range(3 * DN_HEADS):
            cols = slice(h * LANES, (h + 1) * LANES)
            c = _conv_taps(ext_ref, cw_ref, cols, tm)
            a = c * _sigmoid(c)
            oc = slice((h % DN_HEADS) * LANES, (h % DN_HEADS + 1) * LANES)
            if h < 2 * DN_HEADS:
                rinv = lax.rsqrt(jnp.sum(a * a, axis=-1, keepdims=True) + NORM_EPS)
                if h < DN_HEADS:
                    q_ref[:, oc] = a * (rinv * DN_DK ** -0.5)
                else:
                    k_ref[:, oc] = a * rinv
            else:
                v_ref[:, oc] = a
        bav = ba_ref[...]
        lane = lax.broadcasted_iota(jnp.int32, bav.shape, 1)
        beta = _sigmoid(bav)
        g = -jnp.exp(al_ref[...]) * _softplus(bav + dtb_ref[...])
        bg_ref[...] = jnp.where(lane < DN_HEADS, beta, jnp.where(lane < 2 * DN_HEADS, g, 0.0))

    return _rows_call(
        body, "dn_prep_fwd", s,
        [(qkv_pre, "tile"), (qkv_pre, "prev8"), (conv_w8, "full"), (ba, "tile"), (alog_row, "full"), (dtb_row, "full")],
        [((s, D_MODEL), F32, "tile")] * 3 + [((s, LANES), F32, "tile")],
        scratch=[pltpu.VMEM((tm + SUBLANES, QKV_W), F32)])


def _dn_prep_bwd(qkv_pre, ba, conv_w8, alog_row, dtb_row, dq, dk, dv, dbg):
    s = qkv_pre.shape[0]
    tm = ROW_TILE

    def body(u_ref, halo_ref, cw_ref, ba_ref, al_ref, dtb_ref, dq_ref, dk_ref, dv_ref, dbg_ref,
             dc_ref, dba_ref, dsmall_ref, ext_ref):
        _fill_ext(ext_ref, u_ref, halo_ref, pl.program_id(0) == 0)
        for h in range(3 * DN_HEADS):
            cols = slice(h * LANES, (h + 1) * LANES)
            oc = slice((h % DN_HEADS) * LANES, (h % DN_HEADS + 1) * LANES)
            c = _conv_taps(ext_ref, cw_ref, cols, tm)
            sg = _sigmoid(c)
            a = c * sg
            if h < 2 * DN_HEADS:
                rinv = lax.rsqrt(jnp.sum(a * a, axis=-1, keepdims=True) + NORM_EPS)
                dy = dq_ref[:, oc] * DN_DK ** -0.5 if h < DN_HEADS else dk_ref[:, oc]
                da = rinv * dy - a * (rinv * rinv * rinv) * jnp.sum(dy * a, axis=-1, keepdims=True)
            else:
                da = dv_ref[:, oc]
            dc_ref[:, cols] = da * (sg * (1.0 + c * (1.0 - sg)))
        bav = ba_ref[...]
        dbgv = dbg_ref[...]
        lane = lax.broadcasted_iota(jnp.int32, bav.shape, 1)
        beta = _sigmoid(bav)
        ea = jnp.exp(al_ref[...])
        z = bav + dtb_ref[...]
        g = -ea * _softplus(z)
        is_b = lane < DN_HEADS
        is_g = jnp.logical_and(lane >= DN_HEADS, lane < 2 * DN_HEADS)
        d_aa = jnp.where(is_g, dbgv * (-ea) * _sigmoid(z), 0.0)
        dba = jnp.where(is_b, dbgv * beta * (1.0 - beta), d_aa)
        dba_ref[...] = dba.astype(dba_ref.dtype)
        r_alog = jnp.sum(jnp.where(is_g, dbgv * g, 0.0), axis=0, keepdims=True)
        r_dtb = jnp.sum(d_aa, axis=0, keepdims=True)
        _acc_add(dsmall_ref, jnp.concatenate([r_alog, r_dtb, jnp.zeros((SUBLANES - 2, LANES), F32)], axis=0))

    return _rows_call(
        body, "dn_prep_bwd", s,
        [(qkv_pre, "tile"), (qkv_pre, "prev8"), (conv_w8, "full"), (ba, "tile"), (alog_row, "full"), (dtb_row, "full"),
         (dq, "tile"), (dk, "tile"), (dv, "tile"), (dbg, "tile")],
        [((s, QKV_W), F32, "tile"), ((s, LANES), MXU, "tile"), ((SUBLANES, LANES), F32, "acc")],
        scratch=[pltpu.VMEM((tm + SUBLANES, QKV_W), F32)])


def _conv_bwd(dc, qkv_pre, conv_w8):
    s = dc.shape[0]
    tm = ROW_TILE
    steps = s // tm

    def body(dc_ref, dnext_ref, u_ref, halo_ref, cw_ref, du_ref, dcw_ref, extd_ref, ext_ref):
        i = pl.program_id(0)
        _fill_ext(ext_ref, u_ref, halo_ref, i == 0)
        extd_ref[0:tm, :] = dc_ref[...]
        extd_ref[tm:, :] = jnp.where(i == steps - 1, 0.0, dnext_ref[...])

        @pl.when(i == 0)
        def _():
            dcw_ref[...] = jnp.zeros_like(dcw_ref)

        for h in range(3 * DN_HEADS):
            cols = slice(h * LANES, (h + 1) * LANES)
            du = None
            for j in range(4):
                term = cw_ref[3 - j:4 - j, cols] * extd_ref[j:j + tm, cols]
                du = term if du is None else du + term
            du_ref[:, cols] = du.astype(du_ref.dtype)
            dcv = dc_ref[:, cols]
            for j in range(4):
                row = jnp.sum(dcv * ext_ref[SUBLANES - j:SUBLANES - j + tm, cols], axis=0, keepdims=True)
                dcw_ref[3 - j:4 - j, cols] += row

    return _rows_call(
        body, "conv_bwd", s,
        [(dc, "tile"), (dc, "next8"), (qkv_pre, "tile"), (qkv_pre, "prev8"), (conv_w8, "full")],
        [((s, QKV_W), MXU, "tile"), ((SUBLANES, QKV_W), F32, "acc")],
        scratch=[pltpu.VMEM((tm + SUBLANES, QKV_W), F32), pltpu.VMEM((tm + SUBLANES, QKV_W), F32)])


def _dn_out_fwd(o, z, dnw_row, w_o_dn):
    def body(o_ref, z_ref, w_ref, wo_ref, on_ref, y_ref):
        for h in range(DN_HEADS):
            cols = slice(h * LANES, (h + 1) * LANES)
            ov = o_ref[:, cols]
            zv = z_ref[:, cols]
            ro = lax.rsqrt(jnp.mean(ov * ov, axis=-1, keepdims=True) + NORM_EPS)
            on_ref[:, cols] = (ov * ro * w_ref[...] * (zv * _sigmoid(zv))).astype(on_ref.dtype)
        y_ref[...] = jnp.dot(on_ref[...], wo_ref[...], preferred_element_type=F32)

    return _rows_call(body, "dn_out_fwd", o.shape[0], [(o, "tile"), (z, "tile"), (dnw_row, "full"), (w_o_dn, "full")],
                      [(o.shape, MXU, "tile"), ((o.shape[0], w_o_dn.shape[1]), F32, "tile")])


def _dn_out_bwd(dy, o, z, dnw_row, w_o_dn):
    def body(dy_ref, o_ref, z_ref, w_ref, wo_ref, do_ref, dz_ref, dw_ref, d_ref):
        d_ref[...] = lax.dot_general(dy_ref[...], wo_ref[...], NT_DIMS, preferred_element_type=F32)
        acc = jnp.zeros((1, LANES), F32)
        for h in range(DN_HEADS):
            cols = slice(h * LANES, (h + 1) * LANES)
            dv, ov, zv = d_ref[:, cols], o_ref[:, cols], z_ref[:, cols]
            sg = _sigmoid(zv)
            sz = zv * sg
            ro = lax.rsqrt(jnp.mean(ov * ov, axis=-1, keepdims=True) + NORM_EPS)
            nv = ov * ro
            dn = dv * w_ref[...] * sz
            acc = acc + jnp.sum(dv * nv * sz, axis=0, keepdims=True)
            dz_ref[:, cols] = (dv * nv * w_ref[...] * (sg * (1.0 + zv * (1.0 - sg)))).astype(dz_ref.dtype)
            do_ref[:, cols] = ro * dn - ov * (ro * ro * ro) * jnp.mean(dn * ov, axis=-1, keepdims=True)
        _acc_add(dw_ref, jnp.concatenate([acc, jnp.zeros((SUBLANES - 1, LANES), F32)], axis=0))

    return _rows_call(body, "dn_out_bwd", o.shape[0],
                      [(dy, "tile"), (o, "tile"), (z, "tile"), (dnw_row, "full"), (w_o_dn, "full")],
                      [(o.shape, F32, "tile"), (o.shape, MXU, "tile"), ((SUBLANES, LANES), F32, "acc")],
                      scratch=[pltpu.VMEM((ROW_TILE, o.shape[1]), F32)])


def _attn_out_fwd(parts, lses, zb, w_o_dil):
    def body(o0, o1, o2, l0, l1, l2, z_ref, wo_ref, lse_ref, o_ref, g_ref, y_ref):
        a, b, c = l0[...], l1[...], l2[...]
        m = jnp.maximum(a, jnp.maximum(b, c))
        ea, eb, ec = jnp.exp(a - m), jnp.exp(b - m), jnp.exp(c - m)
        den = ea + eb + ec
        out = (ea * o0[...] + eb * o1[...] + ec * o2[...]) / den
        lse_ref[...] = m + jnp.log(den)
        o_ref[...] = out
        zv = z_ref[...]
        gated = (out * (zv * _sigmoid(zv))).astype(g_ref.dtype)
        g_ref[...] = gated
        y_ref[...] = jnp.dot(gated, wo_ref[...], preferred_element_type=F32)

    s = zb.shape[0]
    return _rows_call(body, "attn_out_fwd", s,
                      [(p, "tile") for p in parts] + [(l, "tile") for l in lses] + [(zb, "tile"), (w_o_dil, "full")],
                      [((s, DIL_W), F32, "tile"), ((s, DIL_W), F32, "tile"), ((s, DIL_W), MXU, "tile"),
                       ((s, w_o_dil.shape[1]), F32, "tile")])


def _attn_out_bwd(dy, o_joint, zb, w_o_dil):
    def body(dy_ref, o_ref, z_ref, wo_ref, do_ref, dz_ref, dl_ref):
        zv = z_ref[...]
        sg = _sigmoid(zv)
        dv = lax.dot_general(dy_ref[...], wo_ref[...], NT_DIMS, preferred_element_type=F32)
        ov = o_ref[...]
        do = dv * (zv * sg)
        do_ref[...] = do
        dz_ref[...] = (dv * ov * (sg * (1.0 + zv * (1.0 - sg)))).astype(dz_ref.dtype)
        for h in range(DIL_HEADS):
            cols = slice(h * LANES, (h + 1) * LANES)
            dl_ref[:, cols] = jnp.broadcast_to(jnp.sum(do[:, cols] * ov[:, cols], axis=-1, keepdims=True),
                                               (do.shape[0], LANES))

    s = zb.shape[0]
    return _rows_call(body, "attn_out_bwd", s, [(dy, "tile"), (o_joint, "tile"), (zb, "tile"), (w_o_dil, "full")],
                      [((s, DIL_W), F32, "tile"), ((s, DIL_W), MXU, "tile"), ((s, DIL_W), F32, "tile")])


def _merge_out_final(ga, gb, ya, yb, x, target, w_out, wf_row):
    s, dm = x.shape

    def body(ga_ref, gb_ref, ya_ref, yb_ref, x_ref, t_ref, wo_ref, w_ref,
             loss_ref, dw_ref, m_ref, dxb_ref, dx_ref, dya_ref, dyb_ref, dga_ref, dgb_ref):
        sa, sb = _sigmoid(ga_ref[...]), _sigmoid(gb_ref[...])
        ya, yb = ya_ref[...], yb_ref[...]
        merged = (sa * ya + sb * yb).astype(MXU)
        m_ref[...] = merged
        x2 = x_ref[...] + jnp.dot(merged, wo_ref[...], preferred_element_type=F32)
        r = lax.rsqrt(jnp.mean(x2 * x2, axis=-1, keepdims=True) + NORM_EPS)
        w = w_ref[...]
        err = x2 * r * w - t_ref[...]
        tile_loss = 0.5 * jnp.sum(jnp.mean(err * err, axis=-1, keepdims=True), axis=0, keepdims=True)
        _acc_add(loss_ref, jnp.broadcast_to(tile_loss, (SUBLANES, LANES)))
        dy = err * (1.0 / dm)
        row = jnp.sum(dy * x2 * r, axis=0, keepdims=True)
        _acc_add(dw_ref, jnp.concatenate([row, jnp.zeros((SUBLANES - 1, dm), F32)], axis=0))
        dn = dy * w
        dx2 = r * dn - x2 * (r * r * r) * jnp.mean(dn * x2, axis=-1, keepdims=True)
        dx_ref[...] = dx2
        dxb = dx2.astype(MXU)
        dxb_ref[...] = dxb
        dmv = lax.dot_general(dxb, wo_ref[...], NT_DIMS, preferred_element_type=F32)
        dya_ref[...] = (dmv * sa).astype(dya_ref.dtype)
        dyb_ref[...] = (dmv * sb).astype(dyb_ref.dtype)
        dga_ref[...] = (dmv * ya * sa * (1.0 - sa)).astype(dga_ref.dtype)
        dgb_ref[...] = (dmv * yb * sb * (1.0 - sb)).astype(dgb_ref.dtype)

    return _rows_call(body, "merge_out_final", s,
                      [(ga, "tile"), (gb, "tile"), (ya, "tile"), (yb, "tile"), (x, "tile"), (target, "tile"),
                       (w_out, "full"), (wf_row, "full")],
                      [((SUBLANES, LANES), F32, "acc"), ((SUBLANES, dm), F32, "acc"), ((s, dm), MXU, "tile"),
                       ((s, dm), MXU, "tile"), ((s, dm), F32, "tile")] + [((s, dm), MXU, "tile")] * 4)


def _lane_pick(x, idx):
    lane = lax.broadcasted_iota(jnp.int32, x.shape, 1)
    return jnp.sum(jnp.where(lane == idx, x, 0.0), axis=-1, keepdims=True)


PAIR = 2 * DN_CHUNK
SCAN_CHUNKS = 4


def _bmm(a, b):
    return lax.dot_general(a.astype(MXU), b.astype(MXU), (((2,), (1,)), ((0,), (0,))), preferred_element_type=F32)


def _bmm_nt(a, b):
    return lax.dot_general(a.astype(MXU), b.astype(MXU), (((2,), (2,)), ((0,), (0,))), preferred_element_type=F32)


def _bmm_tn(a, b):
    return lax.dot_general(a.astype(MXU), b.astype(MXU), (((1,), (1,)), ((0,), (0,))), preferred_element_type=F32)


def _bmm3(a, b):
    ah = a.astype(jnp.bfloat16)
    al = (a - ah.astype(F32)).astype(jnp.bfloat16)
    bh = b.astype(jnp.bfloat16)
    bl = (b - bh.astype(F32)).astype(jnp.bfloat16)
    f = lambda p, q: lax.dot_general(p, q, (((2,), (1,)), ((0,), (0,))), preferred_element_type=F32)
    return f(ah, bh) + (f(ah, bl) + f(al, bh))


def _pair_masks():
    row = lax.broadcasted_iota(jnp.int32, (PAIR, PAIR), 0)
    col = lax.broadcasted_iota(jnp.int32, (PAIR, PAIR), 1)
    same = (row >= DN_CHUNK) == (col >= DN_CHUNK)
    return dict(causal=same & (row >= col), strict=same & (row > col), upper=same & (row <= col), eye=row == col,
                first=row < DN_CHUNK, row=row, lane=col)


def _pair_decay(bgv, masks):
    gc_all = _dot01(masks["causal"].astype(F32), bgv)
    out = []
    for h in range(DN_HEADS):
        beta = _lane_pick(bgv, h)
        gcb = jnp.broadcast_to(_lane_pick(gc_all, DN_HEADS + h), (PAIR, PAIR))
        gam = jnp.where(masks["causal"], jnp.exp(jnp.minimum(gcb - gcb.T, 0.0)), 0.0)
        gl = jnp.where(masks["first"], gcb[DN_CHUNK - 1:DN_CHUNK, :], gcb[PAIR - 1:PAIR, :])
        out.append((beta, gcb, gam, gl))
    return out


def _pair_inverse(a_strict, eye):
    n = -a_strict
    t = eye.astype(F32)[None] + n
    p = n
    for _ in range(int(math.log2(DN_CHUNK)) - 1):
        p = _bmm3(p, p)
        t = t + _bmm3(t, p)
    return t


def _head_cols(h):
    return slice(h * LANES, (h + 1) * LANES)


def _delta_prep(q, k, v, bg):
    s = q.shape[0]
    c = DN_CHUNK
    n_chunks = s // c

    def body(q_ref, k_ref, v_ref, bg_ref, u_ref, w_ref, qd_ref, kd_ref, aqk_ref, dl_ref, t2_ref):
        masks = _pair_masks()
        dec = _pair_decay(bg_ref[...], masks)
        kbs, ks, gams, vbs, kbes, qs, qds, kds, dls = ([] for _ in range(9))
        for h in range(DN_HEADS):
            beta, gcb, gam, gl = dec[h]
            qh, kh, vh = q_ref[:, _head_cols(h)], k_ref[:, _head_cols(h)], v_ref[:, _head_cols(h)]
            eg = jnp.exp(gcb)
            kb = kh * beta
            kbs.append(kb); ks.append(kh); gams.append(gam); vbs.append(vh * beta); kbes.append(kb * eg)
            qs.append(qh); qds.append(qh * eg); kds.append(kh * jnp.exp(gl - gcb)); dls.append(jnp.exp(gl))
        st = lambda xs: jnp.stack(xs, axis=0)
        kmat, gam = st(ks), st(gams)
        a = jnp.where(masks["strict"][None], _bmm_nt(st(kbs), kmat) * gam, 0.0)
        t = _pair_inverse(a, masks["eye"])
        u = _bmm(t, st(vbs))
        w = _bmm(t, st(kbes))
        aqk = _bmm_nt(st(qs), kmat) * gam
        t2_ref[0] = t.astype(t2_ref.dtype)
        for half in range(2):
            rows = slice(half * c, (half + 1) * c)
            u_ref[half] = u[:, rows, :]
            w_ref[half] = w[:, rows, :].astype(w_ref.dtype)
            qd_ref[half] = st(qds)[:, rows, :].astype(qd_ref.dtype)
            kd_ref[half] = st(kds)[:, rows, :].astype(kd_ref.dtype)
            aqk_ref[half] = aqk[:, rows, rows].astype(aqk_ref.dtype)
            dl_ref[half] = st(dls)[:, half * c:half * c + SUBLANES, :]

    row_spec = lambda w_: pl.BlockSpec((PAIR, w_), lambda i: (i, 0))
    hm = lambda a_, b_: pl.BlockSpec((2, DN_HEADS, a_, b_), lambda i: (i, 0, 0, 0))
    hm_shape = lambda a_, b_, dt: jax.ShapeDtypeStruct((n_chunks, DN_HEADS, a_, b_), dt)
    return _pcall(
        body, name="delta_prep", grid=(n_chunks // 2,),
        in_specs=[row_spec(D_MODEL)] * 3 + [row_spec(LANES)],
        out_specs=[hm(c, LANES)] * 4 + [hm(c, c), hm(SUBLANES, LANES),
                   pl.BlockSpec((1, DN_HEADS, PAIR, PAIR), lambda i: (i, 0, 0, 0))],
        out_shape=[hm_shape(c, LANES, F32), hm_shape(c, LANES, MXU), hm_shape(c, LANES, MXU), hm_shape(c, LANES, MXU),
                   hm_shape(c, c, MXU), hm_shape(SUBLANES, LANES, F32),
                   jax.ShapeDtypeStruct((n_chunks // 2, DN_HEADS, PAIR, PAIR), MXU)],
        compiler_params=_params("parallel"),
    )(q, k, v, bg)


def _delta_scan_fwd(u, w, qd, kd, aqk, dl):
    n_chunks = u.shape[0]
    c = DN_CHUNK
    g_n = SCAN_CHUNKS

    def body(u_ref, w_ref, qd_ref, kd_ref, aqk_ref, dl_ref, o_ref, vnew_ref, st_ref, state):
        @pl.when(pl.program_id(0) == 0)
        def _():
            state[...] = jnp.zeros_like(state)

        for g in range(g_n):
            sv = state[...]
            sb = sv.astype(MXU)
            vnew = u_ref[g] - _bmm(w_ref[g], sb)
            o = _bmm(qd_ref[g], sb) + _bmm(aqk_ref[g], vnew)
            state[...] = sv * dl_ref[g][:, 0:1, :] + _bmm_tn(kd_ref[g], vnew)
            vnew_ref[g] = vnew.astype(vnew_ref.dtype)
            st_ref[g] = sb
            for h in range(DN_HEADS):
                o_ref[g * c:(g + 1) * c, _head_cols(h)] = o[h]

    hm = lambda a_, b_: pl.BlockSpec((g_n, DN_HEADS, a_, b_), lambda i: (i, 0, 0, 0))
    return _pcall(
        body, name="delta_scan_fwd", grid=(n_chunks // g_n,),
        in_specs=[hm(c, LANES)] * 4 + [hm(c, c), hm(SUBLANES, LANES)],
        out_specs=[pl.BlockSpec((g_n * c, D_MODEL), lambda i: (i, 0)), hm(c, LANES), hm(DN_DK, DN_DK)],
        out_shape=[jax.ShapeDtypeStruct((n_chunks * c, D_MODEL), F32),
                   jax.ShapeDtypeStruct((n_chunks, DN_HEADS, c, LANES), MXU),
                   jax.ShapeDtypeStruct((n_chunks, DN_HEADS, DN_DK, DN_DK), MXU)],
        scratch_shapes=[pltpu.VMEM((DN_HEADS, DN_DK, DN_DK), F32)],
        compiler_params=_params("arbitrary"),
    )(u, w, qd, kd, aqk, dl)


def _delta_scan_bwd(w, qd, kd, aqk, dl, vnew, st, do):
    n_chunks = w.shape[0]
    c = DN_CHUNK
    g_n = SCAN_CHUNKS
    steps = n_chunks // g_n

    def body(w_ref, qd_ref, kd_ref, aqk_ref, dl_ref, vnew_ref, st_ref, do_ref, dvnew_ref, dkd_ref, ddl_ref, dstate):
        @pl.when(pl.program_id(0) == 0)
        def _():
            dstate[...] = jnp.zeros_like(dstate)

        for g in reversed(range(g_n)):
            ds = dstate[...]
            dsb = ds.astype(MXU)
            doh = jnp.stack([do_ref[g * c:(g + 1) * c, _head_cols(h)] for h in range(DN_HEADS)], axis=0)
            dvnew = _bmm_tn(aqk_ref[g], doh) + _bmm(kd_ref[g], dsb)
            dkd_ref[g] = _bmm_nt(vnew_ref[g], dsb)
            ddl = jnp.sum(jnp.sum(st_ref[g].astype(F32) * ds, axis=2, keepdims=True), axis=1, keepdims=True)
            ddl_ref[g] = jnp.broadcast_to(ddl, (DN_HEADS, SUBLANES, LANES))
            dstate[...] = ds * dl_ref[g][:, 0:1, :] + _bmm_tn(qd_ref[g], doh) - _bmm_tn(w_ref[g], dvnew)
            dvnew_ref[g] = dvnew.astype(dvnew_ref.dtype)

    rev = lambda i: steps - 1 - i
    hm = lambda a_, b_: pl.BlockSpec((g_n, DN_HEADS, a_, b_), lambda i: (rev(i), 0, 0, 0))
    return _pcall(
        body, name="delta_scan_bwd", grid=(steps,),
        in_specs=[hm(c, LANES)] * 3 + [hm(c, c), hm(SUBLANES, LANES), hm(c, LANES), hm(DN_DK, DN_DK),
                  pl.BlockSpec((g_n * c, D_MODEL), lambda i: (rev(i), 0))],
        out_specs=[hm(c, LANES), hm(c, LANES), hm(SUBLANES, LANES)],
        out_shape=[jax.ShapeDtypeStruct((n_chunks, DN_HEADS, c, LANES), MXU),
                   jax.ShapeDtypeStruct((n_chunks, DN_HEADS, c, LANES), F32),
                   jax.ShapeDtypeStruct((n_chunks, DN_HEADS, SUBLANES, LANES), F32)],
        scratch_shapes=[pltpu.VMEM((DN_HEADS, DN_DK, DN_DK), F32)],
        compiler_params=_params("arbitrary"),
    )(w, qd, kd, aqk, dl, vnew, st, do)


def _delta_post_bwd(q, k, v, bg, t2, st, vnew, do, dvnew, dkd, ddl):
    s = q.shape[0]
    c = DN_CHUNK

    def body(q_ref, k_ref, v_ref, bg_ref, t2_ref, st_ref, vnew_ref, do_ref, dvnew_ref, dkd_ref, ddl_ref,
             dq_ref, dk_ref, dv_ref, dbg_ref):
        masks = _pair_masks()
        first = masks["first"][None]
        dec = _pair_decay(bg_ref[...], masks)
        st_ = lambda xs: jnp.stack(xs, axis=0)
        heads = range(DN_HEADS)
        qm_, km_, vm_, dom = (st_([r[:, _head_cols(h)] for h in heads]) for r in (q_ref, k_ref, v_ref, do_ref))
        beta = st_([dec[h][0] for h in heads])
        gcb = st_([dec[h][1] for h in heads])
        gam = st_([dec[h][2] for h in heads])
        gl = st_([dec[h][3] for h in heads])
        pair = lambda ref: jnp.concatenate([ref[0], ref[1]], axis=1)
        vnew2, dvnew2, dkd2 = pair(vnew_ref), pair(dvnew_ref), pair(dkd_ref)
        halves = lambda x: (x[:, :c, :], x[:, c:, :])
        by_state = lambda x: jnp.concatenate([_bmm_nt(xh, st_ref[i]) for i, xh in enumerate(halves(x))], axis=1)
        dqd = by_state(dom)
        dw = -by_state(dvnew2)
        ddl2 = jnp.where(first, ddl_ref[0][:, 0:1, :], ddl_ref[1][:, 0:1, :])

        eg = jnp.exp(gcb)
        egl = jnp.exp(gl - gcb)
        dl = jnp.exp(gl)
        kb = km_ * beta
        kk = _bmm_nt(kb, km_)
        a = jnp.where(masks["strict"][None], kk * gam, 0.0)
        t = t2_ref[0]
        vb = vm_ * beta
        kbe = kb * eg
        u = _bmm(t, vb)
        w = _bmm(t, kbe)
        aqk = _bmm_nt(qm_, km_) * gam
        qd = qm_ * eg
        kd = km_ * egl

        daqk = jnp.where(masks["causal"][None], _bmm_nt(dom, vnew2), 0.0)
        dvb = _bmm_tn(t, dvnew2)
        dkbe = _bmm_tn(t, dw)
        da = jnp.where(masks["strict"][None], -(_bmm_nt(dvb, u) + _bmm_nt(dkbe, w)), 0.0)
        pm = da * gam
        qmm = daqk * gam
        dkb = _bmm(pm, km_) + dkbe * eg
        dkh = _bmm_tn(pm, kb) + _bmm_tn(qmm, qm_) + dkd2 * egl + dkb * beta
        dqh = _bmm(qmm, km_) + dqd * eg
        xm = da * a + daqk * aqk
        ones = jnp.ones((DN_HEADS, PAIR, LANES), F32)
        hi, mid, lo = _split3(xm)
        colsum = _bmm_tn(hi, ones) + (_bmm_tn(mid, ones) + _bmm_tn(lo, ones))
        tmp = jnp.sum(dkd2 * kd, axis=-1, keepdims=True)
        dgc = (jnp.sum(xm, axis=-1, keepdims=True) - colsum + jnp.sum(dkbe * kbe, axis=-1, keepdims=True)
               + jnp.sum(dqd * qd, axis=-1, keepdims=True) - tmp)
        sum0 = jnp.sum(jnp.where(first, tmp, 0.0), axis=1, keepdims=True)
        sum1 = jnp.sum(jnp.where(first, 0.0, tmp), axis=1, keepdims=True)
        dgl = jnp.where(first, sum0, sum1) + ddl2 * dl
        last = (masks["row"] == c - 1) | (masks["row"] == PAIR - 1)
        dgc = dgc + jnp.where(last[None], dgl, 0.0)
        dbeta = jnp.sum(dvb * vm_, axis=-1, keepdims=True) + jnp.sum(dkb * km_, axis=-1, keepdims=True)
        dvh = dvb * beta

        lane = masks["lane"]
        dgc_lanes = jnp.zeros((PAIR, LANES), F32)
        dbg = jnp.zeros((PAIR, LANES), F32)
        for h in heads:
            dq_ref[:, _head_cols(h)] = dqh[h]
            dk_ref[:, _head_cols(h)] = dkh[h]
            dv_ref[:, _head_cols(h)] = dvh[h]
            dgc_lanes = dgc_lanes + jnp.where(lane == DN_HEADS + h, dgc[h], 0.0)
            dbg = dbg + jnp.where(lane == h, dbeta[h], 0.0)
        dbg_ref[...] = dbg + _dot01(masks["upper"].astype(F32), dgc_lanes)

    n_pairs = s // PAIR
    row_spec = lambda w_: pl.BlockSpec((PAIR, w_), lambda i: (i, 0))
    hm = lambda a_, b_: pl.BlockSpec((2, DN_HEADS, a_, b_), lambda i: (i, 0, 0, 0))
    return _pcall(
        body, name="delta_post_bwd", grid=(n_pairs,),
        in_specs=[row_spec(D_MODEL)] * 3 + [row_spec(LANES), pl.BlockSpec((1, DN_HEADS, PAIR, PAIR), lambda i: (i, 0, 0, 0)),
                  hm(DN_DK, DN_DK), hm(c, LANES), row_spec(D_MODEL), hm(c, LANES), hm(c, LANES), hm(SUBLANES, LANES)],
        out_specs=[row_spec(D_MODEL)] * 3 + [row_spec(LANES)],
        out_shape=[jax.ShapeDtypeStruct((s, D_MODEL), F32)] * 3 + [jax.ShapeDtypeStruct((s, LANES), F32)],
        compiler_params=_params("parallel"),
    )(q, k, v, bg, t2, st, vnew, do, dvnew, dkd, ddl)


def _alibi_slope(group, head):
    n = N_DIL * DIL_HEADS
    return float(2.0 ** (-8.0 * (group * DIL_HEADS + head + 1) / n))


def _attn_plan(s, group):
    window, dil = DIL_GROUPS[group]
    assert window // dil == ATT_BLOCK
    assert (s // dil) % ATT_BLOCK == 0, "sub-sequence length must be a whole number of attention blocks"
    return dil, s // dil // ATT_BLOCK, (DIL_HEADS if dil == 1 else 1)


def _attn_specs(group, dil, nb, hp):
    rows = ATT_BLOCK * dil

    def spec(col0, shift):
        if shift < 0:
            f = lambda hb, n: (jnp.maximum(n - 1, 0), col0 + hb)
        elif shift > 0:
            f = lambda hb, n: (jnp.minimum(n + 1, nb - 1), col0 + hb)
        else:
            f = lambda hb, n: (jnp.minimum(n, nb - 1), col0 + hb)
        return pl.BlockSpec((rows, hp * LANES), f)

    return (lambda shift: spec(group * (DIL_HEADS // hp), shift)), (lambda shift: spec(0, shift))


def _sub_rows(ref, r, dil, cols):
    return ref[:, cols] if dil == 1 else ref[pl.ds(r, ATT_BLOCK, stride=dil), cols]


def _set_sub_rows(ref, r, dil, cols, value):
    if dil == 1:
        ref[:, cols] = value
    else:
        ref[pl.ds(r, ATT_BLOCK, stride=dil), cols] = value


def _step_slope(group, hp, hh):
    if hp == DIL_HEADS:
        return _alibi_slope(group, hh)
    hb = pl.program_id(0)
    slope = _alibi_slope(group, DIL_HEADS - 1)
    for h in reversed(range(DIL_HEADS - 1)):
        slope = jnp.where(hb == h, _alibi_slope(group, h), slope)
    return slope


def _window_bias(dil, n):
    a = lax.broadcasted_iota(jnp.int32, (ATT_BLOCK, 2 * ATT_BLOCK), 0)
    b = lax.broadcasted_iota(jnp.int32, (ATT_BLOCK, 2 * ATT_BLOCK), 1)
    dist = ATT_BLOCK + a - b
    valid = (dist >= 0) & (dist <= ATT_BLOCK) & ((b >= ATT_BLOCK) | (n > 0))
    return (dist * dil).astype(F32), valid


def _attn_fwd(qb, kb, vb, group):
    s = qb.shape[0]
    dil, nb, hp = _attn_plan(s, group)
    qkv, per_head = _attn_specs(group, dil, nb, hp)

    def body(q_ref, kp_ref, kc_ref, vp_ref, vc_ref, o_ref, lse_ref):
        n = pl.program_id(1)
        distd, valid = _window_bias(dil, n)
        for hh in range(hp):
            cols = _head_cols(hh)
            slope = _step_slope(group, hp, hh)
            for r in range(dil):
                sub = lambda ref: _sub_rows(ref, r, dil, cols).astype(MXU)
                kk = jnp.concatenate([sub(kp_ref), sub(kc_ref)], axis=0)
                vv = jnp.concatenate([sub(vp_ref), sub(vc_ref)], axis=0)
                sc = _dot_nt(sub(q_ref), kk) * DIL_DH ** -0.5 - slope * distd
                sc = jnp.where(valid, sc, -1e30)
                mx = jnp.max(sc, axis=-1, keepdims=True)
                p = jnp.where(valid, jnp.exp(sc - mx), 0.0)
                den = jnp.sum(p, axis=-1, keepdims=True)
                _set_sub_rows(o_ref, r, dil, cols, _dot(p, vv) / den)
                _set_sub_rows(lse_ref, r, dil, cols, jnp.broadcast_to(mx + jnp.log(den), (ATT_BLOCK, LANES)))

    return _pcall(
        body, name=f"attn_fwd_g{group}", grid=(DIL_HEADS // hp, nb),
        in_specs=[qkv(0), qkv(-1), qkv(0), qkv(-1), qkv(0)], out_specs=[per_head(0)] * 2,
        out_shape=[jax.ShapeDtypeStruct((s, DIL_W), F32)] * 2,
        compiler_params=_params("parallel", "parallel"),
    )(qb, kb, kb, vb, vb)


def _attn_bwd(qb, kb, vb, d_o, lse, delta, group):
    s = qb.shape[0]
    dil, nb, hp = _attn_plan(s, group)
    qkv, per_head = _attn_specs(group, dil, nb, hp)
    scale = DIL_DH ** -0.5

    def body(q_ref, kp_ref, kc_ref, vp_ref, vc_ref, do_ref, l_ref, dl_ref, dq_ref, dk_ref, dv_ref,
             dq_acc, dk_done, dv_done, dk_carry, dv_carry):
        n = pl.program_id(1)
        slopes = [_step_slope(group, hp, hh) for hh in range(hp)]

        @pl.when(n == 0)
        def _():
            dk_carry[...] = jnp.zeros_like(dk_carry)
            dv_carry[...] = jnp.zeros_like(dv_carry)

        @pl.when(n < nb)
        def _():
            distd, valid = _window_bias(dil, n)
            for hh in range(hp):
                cols = _head_cols(hh)
                slope = slopes[hh]
                for r in range(dil):
                    sub = lambda ref: _sub_rows(ref, r, dil, cols)
                    qc, do = sub(q_ref).astype(MXU), sub(do_ref).astype(MXU)
                    kk = jnp.concatenate([sub(kp_ref).astype(MXU), sub(kc_ref).astype(MXU)], axis=0)
                    vv = jnp.concatenate([sub(vp_ref).astype(MXU), sub(vc_ref).astype(MXU)], axis=0)
                    sc = _dot_nt(qc, kk) * scale - slope * distd
                    p = jnp.where(valid, jnp.exp(jnp.minimum(sc - jnp.concatenate([sub(l_ref)] * 2, axis=1), 0.0)), 0.0)
                    dsc = p * (_dot_nt(do, vv) - jnp.concatenate([sub(dl_ref)] * 2, axis=1))
                    _set_sub_rows(dq_acc, r, dil, cols, _dot(dsc, kk) * scale)
                    dkk = _dot_tn(dsc, qc) * scale
                    dvv = _dot_tn(p, do)
                    _set_sub_rows(dk_done, r, dil, cols, _sub_rows(dk_carry, r, dil, cols) + dkk[:ATT_BLOCK])
                    _set_sub_rows(dv_done, r, dil, cols, _sub_rows(dv_carry, r, dil, cols) + dvv[:ATT_BLOCK])
                    _set_sub_rows(dk_carry, r, dil, cols, dkk[ATT_BLOCK:])
                    _set_sub_rows(dv_carry, r, dil, cols, dvv[ATT_BLOCK:])
            dq_ref[...] = dq_acc[...].astype(dq_ref.dtype)
            dk_ref[...] = dk_done[...].astype(dk_ref.dtype)
            dv_ref[...] = dv_done[...].astype(dv_ref.dtype)

        @pl.when(n == nb)
        def _():
            dk_ref[...] = dk_carry[...].astype(dk_ref.dtype)
            dv_ref[...] = dv_carry[...].astype(dv_ref.dtype)

    return _pcall(
        body, name=f"attn_bwd_g{group}", grid=(DIL_HEADS // hp, nb + 1),
        in_specs=[qkv(0), qkv(-1), qkv(0), qkv(-1), qkv(0)] + [per_head(0)] * 3,
        out_specs=[per_head(0), per_head(-1), per_head(-1)],
        out_shape=[jax.ShapeDtypeStruct((s, DIL_W), MXU)] * 3,
        scratch_shapes=[pltpu.VMEM((ATT_BLOCK * dil, hp * LANES), F32)] * 5,
        compiler_params=_params("parallel", "arbitrary"),
    )(qb, kb, kb, vb, vb, d_o, lse, delta)


def _my_place():
    mx, my, mc = lax.axis_index("x"), lax.axis_index("y"), lax.axis_index("c")
    return mx, my, mc, 4 * mx + 2 * my + mc


N_CHIPS = 4


def _shard_row_tile(r):
    if r <= 512:
        return r
    return 128 if r % 128 == 0 else 480


def _other_chips(mx, my):
    return [(1 - mx, my), (mx, 1 - my), (1 - mx, 1 - my)]


def _all_gather(xs, name):
    n = len(xs)

    def body(*refs):
        x_refs, o_refs = refs[:n], refs[n:2 * n]
        send_sems, recv_sems, local_sems = refs[2 * n:]
        mx, my, mc, me = _my_place()
        sibling, sibling_id = (mx, my, 1 - mc), 4 * mx + 2 * my + (1 - mc)
        chips = _other_chips(mx, my)

        def copy(a, k, slot, to, src=None):
            dst = o_refs[a].at[slot]
            return pltpu.make_async_remote_copy(
                src_ref=dst if src is None else src, dst_ref=dst, send_sem=send_sems.at[a, k],
                recv_sem=recv_sems.at[a, k], device_id=to, device_id_type=MESH)

        local = [pltpu.make_async_copy(x_refs[a], o_refs[a].at[me], local_sems.at[a]) for a in range(n)]
        for cp in local:
            cp.start()
        sends = []
        for a in range(n):
            sends.append(copy(a, 0, me, sibling, src=x_refs[a]))
            sends += [copy(a, 1 + j, me, (px, py, mc), src=x_refs[a]) for j, (px, py) in enumerate(chips)]
        for cp in sends:
            cp.start()
        for j, (px, py) in enumerate(chips):
            slot = 4 * px + 2 * py + mc
            for a in range(n):
                copy(a, 1 + j, slot, (px, py, mc)).wait_recv()
                passed = copy(a, 4 + j, slot, sibling)
                passed.start()
                sends.append(passed)
        for a in range(n):
            copy(a, 0, sibling_id, sibling).wait_recv()
            for j, (px, py) in enumerate(chips):
                copy(a, 4 + j, 4 * px + 2 * py + (1 - mc), sibling).wait_recv()
        for cp in sends:
            cp.wait_send()
        for cp in local:
            cp.wait()

    any_spec = pl.BlockSpec(memory_space=pl.ANY)
    return _pcall(
        body, name=name,
        in_specs=[any_spec] * n, out_specs=[any_spec] * n,
        out_shape=[jax.ShapeDtypeStruct((N_DEV,) + x.shape, x.dtype) for x in xs],
        scratch_shapes=[pltpu.SemaphoreType.DMA((n, N_DEV - 1)), pltpu.SemaphoreType.DMA((n, N_DEV - 1)),
                        pltpu.SemaphoreType.DMA((n,))],
    )(*xs)


def _pair_exchange(gs, name):
    n = len(gs)

    def body(*refs):
        g_refs, o_refs = refs[:n], refs[n:2 * n]
        send_sems, recv_sems = refs[2 * n:]
        mx, my, mc, _ = _my_place()
        copies = [pltpu.make_async_remote_copy(
            src_ref=g_refs[a].at[1 - mc], dst_ref=o_refs[a], send_sem=send_sems.at[a], recv_sem=recv_sems.at[a],
            device_id=(mx, my, 1 - mc), device_id_type=MESH) for a in range(n)]
        for cp in copies:
            cp.start()
        for cp in copies:
            cp.wait()

    any_spec = pl.BlockSpec(memory_space=pl.ANY)
    return _pcall(
        body, name=name,
        in_specs=[any_spec] * n, out_specs=[any_spec] * n,
        out_shape=[jax.ShapeDtypeStruct(g.shape[1:], g.dtype) for g in gs],
        scratch_shapes=[pltpu.SemaphoreType.DMA((n,)), pltpu.SemaphoreType.DMA((n,))],
    )(*gs)


def _pair_add(g, other, name):
    _, chips, r, c = g.shape
    tr = _shard_row_tile(r)
    core = lax.axis_index("c").astype(jnp.int32).reshape(1)

    def body(core_ref, g_ref, o_ref, h_ref):
        h_ref[...] = (g_ref[...].astype(F32)[0] + o_ref[...].astype(F32)).astype(h_ref.dtype)

    blk = pl.BlockSpec((1, tr, c), lambda p, i, core_ref: (p, i, 0))
    return _pcall(
        body, name=name,
        grid_spec=pltpu.PrefetchScalarGridSpec(
            num_scalar_prefetch=1, grid=(chips, pl.cdiv(r, tr)),
            in_specs=[pl.BlockSpec((1, 1, tr, c), lambda p, i, core_ref: (core_ref[0], p, i, 0)), blk],
            out_specs=blk),
        out_shape=jax.ShapeDtypeStruct((chips, r, c), g.dtype),
        compiler_params=_params("parallel", "parallel"),
    )(core, g, other)


def _chip_exchange(hs, name):
    n = len(hs)

    def body(*refs):
        h_refs, o_refs = refs[:n], refs[n:2 * n]
        send_sems, recv_sems, local_sems = refs[2 * n:]
        mx, my, mc, _ = _my_place()
        my_chip = 2 * mx + my
        chips = _other_chips(mx, my)
        local = [pltpu.make_async_copy(h_refs[a].at[my_chip], o_refs[a].at[my_chip], local_sems.at[a]) for a in range(n)]
        for cp in local:
            cp.start()
        for j, (px, py) in enumerate(chips):
            for a in range(n):
                pltpu.make_async_remote_copy(
                    src_ref=h_refs[a].at[2 * px + py], dst_ref=o_refs[a].at[my_chip], send_sem=send_sems.at[a, j],
                    recv_sem=recv_sems.at[a, j], device_id=(px, py, mc), device_id_type=MESH).start()
        for j, (px, py) in enumerate(chips):
            for a in range(n):
                pltpu.make_async_remote_copy(
                    src_ref=h_refs[a].at[2 * px + py], dst_ref=o_refs[a].at[2 * px + py], send_sem=send_sems.at[a, j],
                    recv_sem=recv_sems.at[a, j], device_id=(px, py, mc), device_id_type=MESH).wait()
        for cp in local:
            cp.wait()

    any_spec = pl.BlockSpec(memory_space=pl.ANY)
    return _pcall(
        body, name=name,
        in_specs=[any_spec] * n, out_specs=[any_spec] * n,
        out_shape=[jax.ShapeDtypeStruct(h.shape, h.dtype) for h in hs],
        scratch_shapes=[pltpu.SemaphoreType.DMA((n, N_CHIPS - 1)), pltpu.SemaphoreType.DMA((n, N_CHIPS - 1)),
                        pltpu.SemaphoreType.DMA((n,))],
    )(*hs)


def _adamw(parts, w, m, v, name):
    r, c = w.shape
    n_parts = parts.shape[0]
    tr = _shard_row_tile(r)
    bc1 = 1.0 - ADAM_B1 ** ADAM_STEP
    bc2 = 1.0 - ADAM_B2 ** ADAM_STEP

    def body(p_ref, w_ref, m_ref, v_ref, g_ref, d_ref, nm_ref, nv_ref):
        g = p_ref[0].astype(F32)
        for j in range(1, n_parts):
            g = g + p_ref[j].astype(F32)
        nm = ADAM_B1 * m_ref[...] + (1.0 - ADAM_B1) * g
        nv = ADAM_B2 * v_ref[...] + (1.0 - ADAM_B2) * (g * g)
        g_ref[...] = g
        nm_ref[...] = nm
        nv_ref[...] = nv
        d_ref[...] = -ADAM_LR * ((nm / bc1) / (jnp.sqrt(nv / bc2) + ADAM_EPS) + ADAM_WD * w_ref[...])

    blk = pl.BlockSpec((tr, c), lambda i: (i, 0))
    return _pcall(
        body, name=name, grid=(pl.cdiv(r, tr),),
        in_specs=[pl.BlockSpec((n_parts, tr, c), lambda i: (0, i, 0)), blk, blk, blk],
        out_specs=[blk] * 4, out_shape=[jax.ShapeDtypeStruct((r, c), F32)] * 4,
        compiler_params=_params("parallel"),
    )(parts, w, m, v)


def _local_step(x, target, norm_w, w_segs, conv_w, a_log, dt_bias, dn_norm_w, w_o_dn, w_o_dil, w_out, final_norm_w):
    s = x.shape[0]
    w_qkv, w_za, w_ba, w_qb, w_kb, w_vb, w_zb, w_ga, w_gb = w_segs
    conv_w8 = jnp.concatenate([conv_w, jnp.zeros((SUBLANES - conv_w.shape[0], QKV_W), F32)], axis=0)
    pad8 = jnp.zeros((1, DN_HEADS), F32)
    alog_row = jnp.concatenate([pad8, a_log, jnp.zeros((1, LANES - 2 * DN_HEADS), F32)], axis=1)
    dtb_row = jnp.concatenate([pad8, dt_bias, jnp.zeros((1, LANES - 2 * DN_HEADS), F32)], axis=1)
    wf_row = final_norm_w.reshape(1, D_MODEL)

    hb = _rms_in_fwd(x, norm_w)
    qkv_pre, z_a, ba, z_b = _mm_out(hb, [w_qkv, w_za, w_ba, w_zb], "proj_fwd_a", w_is_out_by_in=True)
    q_b, k_b, v_b, g_a, g_b = _mm_out(hb, [w_qb, w_kb, w_vb, w_ga, w_gb], "proj_fwd_b", w_is_out_by_in=True)

    qn, kn, vn, bg = _dn_prep_fwd(qkv_pre, ba, conv_w8, alog_row, dtb_row)
    u_d, w_d, qd_d, kd_d, aqk_d, dl_d, t2_d = _delta_prep(qn, kn, vn, bg)
    o_a, vnew_d, st_d = _delta_scan_fwd(u_d, w_d, qd_d, kd_d, aqk_d, dl_d)
    on_b, y_a = _dn_out_fwd(o_a, z_a, dn_norm_w, w_o_dn)

    parts, lses = [], []
    for gi in range(N_DIL):
        o_g, l_g = _attn_fwd(q_b, k_b, v_b, gi)
        parts.append(o_g)
        lses.append(l_g)
    lse, o_joint, ob_b, y_b = _attn_out_fwd(parts, lses, z_b, w_o_dil)

    loss8, dwf8, merged_b, dx2_b, dx2, dya_b, dyb_b, dga_b, dgb_b = _merge_out_final(
        g_a, g_b, y_a, y_b, x, target, w_out, wf_row)

    g_w_out = _mm_tn(merged_b, dx2_b, "out_wgrad")
    g_w_o_dn = _mm_tn(on_b, dya_b, "out_dn_wgrad")
    d_o_a, dza_b, ddnw8 = _dn_out_bwd(dya_b, o_a, z_a, dn_norm_w, w_o_dn)

    g_w_o_dil = _mm_tn(ob_b, dyb_b, "out_dil_wgrad")
    d_o, dzb_b, delta = _attn_out_bwd(dyb_b, o_joint, z_b, w_o_dil)
    dqs, dks, dvs = [], [], []
    for gi in range(N_DIL):
        dq_g, dk_g, dv_g = _attn_bwd(q_b, k_b, v_b, d_o, lse, delta, gi)
        dqs.append(dq_g)
        dks.append(dk_g)
        dvs.append(dv_g)

    dvnew_d, dkd_d, ddl_d = _delta_scan_bwd(w_d, qd_d, kd_d, aqk_d, dl_d, vnew_d, st_d, d_o_a)
    dqn, dkn, dvn, dbg = _delta_post_bwd(qn, kn, vn, bg, t2_d, st_d, vnew_d, d_o_a, dvnew_d, dkd_d, ddl_d)
    dc, dba_b, dsmall8 = _dn_prep_bwd(qkv_pre, ba, conv_w8, alog_row, dtb_row, dqn, dkn, dvn, dbg)
    dqkv_b, dconv8 = _conv_bwd(dc, qkv_pre, conv_w8)

    per_group = lambda w: [w[g * DIL_W:(g + 1) * DIL_W] for g in range(N_DIL)]
    dh_b = _mm_in(dqs + dks + dvs + [dga_b, dgb_b],
                  per_group(w_qb) + per_group(w_kb) + per_group(w_vb) + [w_ga, w_gb], "proj_bwd_b", w_is_out_by_in=True)
    dsegs = [dqkv_b, dza_b, dba_b] + dqs + dks + dvs + [dzb_b, dga_b, dgb_b]
    valid_rows = [d.shape[1] for d in dsegs]
    valid_rows[2] = 2 * DN_HEADS
    g_wt = _proj_wgrad_all(dsegs, valid_rows, hb)
    grad_x, dnw8 = _proj_bwd_rms_in([dqkv_b, dza_b, dba_b, dzb_b], [w_qkv, w_za, w_ba, w_zb], dh_b, x, dx2, norm_w)

    small = dict(norm_w=dnw8[0:1], final_norm_w=dwf8[0:1], dn_norm_w=ddnw8[0:1],
                 a_log=dsmall8[0:1, DN_HEADS:2 * DN_HEADS], dt_bias=dsmall8[1:2, DN_HEADS:2 * DN_HEADS])
    return loss8[0:1, 0:1], grad_x, g_wt, dconv8[0:4], g_w_o_dn, g_w_o_dil, g_w_out, small


def _proj_bwd_rms_in(ds, ws, dh_a, x, dx2, norm_w):
    n_seg = len(ds)

    def body(*refs):
        d_refs, w_refs = refs[:n_seg], refs[n_seg:2 * n_seg]
        da_ref, x_ref, dx2_ref, w_ref, dx_ref, dw_ref = refs[2 * n_seg:]
        dx_ref[...] = da_ref[...]
        for d_ref, wt_ref in zip(d_refs, w_refs):
            for c, wd in _col_chunks(d_ref.shape[1], 1024):
                dx_ref[...] += jnp.dot(d_ref[:, c:c + wd], wt_ref[c:c + wd, :], preferred_element_type=F32)
        xv = x_ref[...]
        r = lax.rsqrt(jnp.mean(xv * xv, axis=-1, keepdims=True) + NORM_EPS)
        dhv = dx_ref[...]
        dn = dhv * w_ref[...]
        dx_ref[...] = dx2_ref[...] + r * dn - xv * (r * r * r) * jnp.mean(dn * xv, axis=-1, keepdims=True)
        row = jnp.sum(dhv * xv * r, axis=0, keepdims=True)
        _acc_add(dw_ref, jnp.concatenate([row, jnp.zeros((SUBLANES - 1, row.shape[1]), F32)], axis=0))

    return _rows_call(body, "proj_bwd_b_rms_in", x.shape[0],
                      [(d, "tile") for d in ds] + [(w, "full") for w in ws]
                      + [(dh_a, "tile"), (x, "tile"), (dx2, "tile"), (norm_w, "full")],
                      [(x.shape, F32, "tile"), ((SUBLANES, x.shape[1]), F32, "acc")])


def _split_proj_rows(wt_full):
    offs = [0]
    for n in PROJ_SIZES:
        offs.append(offs[-1] + n)
    seg = lambda a, b: wt_full[offs[a]:offs[b]]
    w_ba = jnp.concatenate([seg(4, 6), jnp.zeros((LANES - 2 * DN_HEADS, wt_full.shape[1]), wt_full.dtype)], axis=0)
    return [seg(0, 3), seg(3, 4), w_ba, seg(6, 7), seg(7, 8), seg(8, 9), seg(9, 10), seg(10, 11), seg(11, 12)]


def _pack_small(norm_w, final_norm_w, dn_norm_w, a_log, dt_bias):
    pad = lambda r: jnp.concatenate([r, jnp.zeros((1, D_MODEL - r.shape[1]), F32)], axis=1)
    rows = [pad(norm_w.reshape(1, -1)), pad(final_norm_w.reshape(1, -1)), pad(dn_norm_w.reshape(1, -1)),
            pad(a_log.reshape(1, -1)), pad(dt_bias.reshape(1, -1)), jnp.zeros((SUBLANES - 5, D_MODEL), F32)]
    return jnp.concatenate(rows, axis=0)


def _unpack_small(p):
    return dict(norm_w=p[0:1], final_norm_w=p[1], dn_norm_w=p[2:3, :DN_DK], a_log=p[3:4, :DN_HEADS],
                dt_bias=p[4:5, :DN_HEADS])


def kernel(x, norm_w, w_in, conv_w, a_log, dt_bias, dn_norm_w, w_o_dn, w_o_dil, w_out, final_norm_w, loss_target, m_norm_w, m_w_in, m_conv_w, m_a_log, m_dt_bias, m_dn_norm_w, m_w_o_dn, m_w_o_dil, m_w_out, m_final_norm_w, v_norm_w, v_w_in, v_conv_w, v_a_log, v_dt_bias, v_dn_norm_w, v_w_o_dn, v_w_o_dil, v_w_out, v_final_norm_w):
    shard_w = w_in.shape[2]
    wt, m_wt, v_wt = (jnp.transpose(t[0]) for t in (w_in, m_w_in, v_w_in))
    gathered = _all_gather([wt.astype(MXU), w_o_dn[0].astype(MXU), w_o_dil[0].astype(MXU), w_out[0].astype(MXU),
                            conv_w[0]], "gather_weights")
    w_in_all, w_o_dn_all, w_o_dil_all, w_out_all, conv_all = gathered
    wt_full = w_in_all.reshape(N_DEV * shard_w, D_MODEL)
    w_o_dn_full = w_o_dn_all.reshape(D_MODEL, D_MODEL)
    w_o_dil_full = jnp.transpose(w_o_dil_all, (1, 0, 2)).reshape(DIL_W, D_MODEL)
    w_out_full = w_out_all.reshape(D_MODEL, D_MODEL)
    conv_full = jnp.transpose(conv_all, (1, 0, 2)).reshape(conv_w.shape[1], QKV_W)

    loss11, grad_x, g_wt, g_conv, g_w_o_dn, g_w_o_dil, g_w_out, small = _local_step(
        x[0], loss_target[0], norm_w, _split_proj_rows(wt_full), conv_full, a_log, dt_bias, dn_norm_w,
        w_o_dn_full, w_o_dil_full, w_out_full, final_norm_w)

    col_shards = lambda g, n: jnp.transpose(g.reshape(g.shape[0], N_DEV, n), (1, 0, 2))
    row_shards = lambda g: g.reshape(N_DEV, g.shape[0] // N_DEV, g.shape[1])
    sent = [row_shards(g_wt).astype(MXU), row_shards(g_w_o_dn).astype(MXU),
            col_shards(g_w_o_dil, w_o_dil.shape[2]).astype(MXU), row_shards(g_w_out).astype(MXU),
            col_shards(g_conv, conv_w.shape[2])]
    by_core = lambda g8: jnp.transpose(g8.reshape((N_CHIPS, 2) + g8.shape[1:]), (1, 0, 2, 3))
    sent = [by_core(g8) for g8 in sent]
    from_sibling = _pair_exchange(sent, "scatter_pair")
    summed = [_pair_add(g, o, f"pair_add_{i}") for i, (g, o) in enumerate(zip(sent, from_sibling))]
    p_w_in, p_w_o_dn, p_w_o_dil, p_w_out, p_conv = _chip_exchange(summed, "scatter_chips")
    p_small = _all_gather([_pack_small(small["norm_w"], small["final_norm_w"], small["dn_norm_w"], small["a_log"],
                                       small["dt_bias"])], "gather_small_grads")[0]

    res = {}
    res["w_in"] = [jnp.transpose(t) for t in _adamw(p_w_in, wt, m_wt, v_wt, "adamw_w_in")]
    res["conv_w"] = _adamw(p_conv, conv_w[0], m_conv_w[0], v_conv_w[0], "adamw_conv_w")
    res["w_o_dn"] = _adamw(p_w_o_dn, w_o_dn[0], m_w_o_dn[0], v_w_o_dn[0], "adamw_w_o_dn")
    res["w_o_dil"] = _adamw(p_w_o_dil, w_o_dil[0], m_w_o_dil[0], v_w_o_dil[0], "adamw_w_o_dil")
    res["w_out"] = _adamw(p_w_out, w_out[0], m_w_out[0], v_w_out[0], "adamw_w_out")
    small_res = _adamw(p_small, _pack_small(norm_w, final_norm_w, dn_norm_w, a_log, dt_bias),
                       _pack_small(m_norm_w, m_final_norm_w, m_dn_norm_w, m_a_log, m_dt_bias),
                       _pack_small(v_norm_w, v_final_norm_w, v_dn_norm_w, v_a_log, v_dt_bias), "adamw_small")
    small_res = [_unpack_small(t) for t in small_res]

    loss = lax.psum(loss11[0, 0], ("x", "y", "c"))
    names = ["norm_w", "w_in", "conv_w", "a_log", "dt_bias", "dn_norm_w", "w_o_dn", "w_o_dil", "w_out", "final_norm_w"]
    outs = [loss, grad_x[None]]
    for kind in range(4):
        for nm in names:
            outs.append(res[nm][kind][None] if nm in res else small_res[kind][nm])
    return tuple(outs)
```

```python
import math

import jax
import jax.numpy as jnp
from jax import lax
from jax.experimental import pallas as pl
from jax.experimental.pallas import tpu as pltpu

F32 = jnp.float32
MXU = jnp.bfloat16
MESH = pl.DeviceIdType.MESH

N_DEV = 8
D_MODEL = 1024
DN_HEADS = 8
DN_DK = 128
DN_CHUNK = 64
N_DIL = 3
DIL_HEADS = 4
DIL_DH = 128
DIL_W = DIL_HEADS * DIL_DH
DIL_GROUPS = ((128, 1), (512, 4), (2048, 16))
ATT_BLOCK = 128
NORM_EPS = 1e-6
QKV_W = 3 * D_MODEL
DILQ_W = N_DIL * DIL_W
PROJ_SIZES = (1024, 1024, 1024, 1024, 8, 8, DILQ_W, DILQ_W, DILQ_W, DIL_W, D_MODEL, D_MODEL)

ADAM_LR = 0.001
ADAM_B1 = 0.9
ADAM_B2 = 0.999
ADAM_EPS = 1e-08
ADAM_WD = 0.01
ADAM_STEP = 10

ROW_TILE = 256
LANES = 128
SUBLANES = 8
VMEM_LIMIT = 48 << 20


def _pcall(body, **kw):
    return pl.pallas_call(body, **kw)


def _params(*sem):
    return pltpu.CompilerParams(dimension_semantics=tuple(sem), vmem_limit_bytes=VMEM_LIMIT)


def _sigmoid(x):
    return 1.0 / (1.0 + jnp.exp(-x))


def _softplus(x):
    return jnp.maximum(x, 0.0) + jnp.log(1.0 + jnp.exp(-jnp.abs(x)))


def _dot(a, b):
    return jnp.dot(a.astype(MXU), b.astype(MXU), preferred_element_type=F32)


def _dot_nt(a, b):
    return lax.dot_general(a.astype(MXU), b.astype(MXU), (((1,), (1,)), ((), ())), preferred_element_type=F32)


def _dot_tn(a, b):
    return lax.dot_general(a.astype(MXU), b.astype(MXU), (((0,), (0,)), ((), ())), preferred_element_type=F32)


def _split3(x):
    hi = x.astype(jnp.bfloat16)
    r1 = x - hi.astype(F32)
    mid = r1.astype(jnp.bfloat16)
    lo = (r1 - mid.astype(F32)).astype(jnp.bfloat16)
    return hi, mid, lo


def _dot01(m01, x):
    m = m01.astype(jnp.bfloat16)
    hi, mid, lo = _split3(x)
    f = lambda p: jnp.dot(m, p, preferred_element_type=F32)
    return f(hi) + (f(mid) + f(lo))


def _rows_call(body, name, n_rows, ins, outs, scratch=(), tm=ROW_TILE):
    steps = n_rows // tm
    per8 = tm // SUBLANES
    last8 = n_rows // SUBLANES - 1
    in_specs = []
    for arr, kind in ins:
        cols = arr.shape[-1]
        if kind == "tile":
            in_specs.append(pl.BlockSpec((tm, cols), lambda i: (i, 0)))
        elif kind == "full":
            in_specs.append(pl.BlockSpec(arr.shape, lambda i, nd=arr.ndim: (0,) * nd))
        elif kind == "prev8":
            in_specs.append(pl.BlockSpec((SUBLANES, cols), lambda i: (jnp.maximum(i * per8 - 1, 0), 0)))
        elif kind == "next8":
            in_specs.append(pl.BlockSpec((SUBLANES, cols), lambda i: (jnp.minimum((i + 1) * per8, last8), 0)))
        else:
            raise ValueError(kind)
    out_specs, out_shape, has_acc = [], [], False
    for shape, dtype, kind in outs:
        out_shape.append(jax.ShapeDtypeStruct(shape, dtype))
        if kind == "tile":
            out_specs.append(pl.BlockSpec((tm, shape[-1]), lambda i: (i, 0)))
        else:
            has_acc = True
            out_specs.append(pl.BlockSpec(shape, lambda i: (0, 0)))
    return _pcall(
        body, name=name, grid=(steps,), in_specs=in_specs, out_specs=out_specs, out_shape=out_shape,
        scratch_shapes=list(scratch),
        compiler_params=_params("arbitrary" if has_acc else "parallel"),
    )(*[a for a, _ in ins])


def _acc_add(ref, value):
    @pl.when(pl.program_id(0) == 0)
    def _():
        ref[...] = jnp.zeros_like(ref)
    ref[...] += value


def _col_chunks(n, width=512):
    return [(c, min(width, n - c)) for c in range(0, n, width)]


NT_DIMS = (((1,), (1,)), ((), ()))
TN_DIMS = (((0,), (0,)), ((), ()))


def _mm_out(a, ws, name, w_is_out_by_in=False, out_dtype=F32, tm=ROW_TILE):
    m, k = a.shape
    ns = [w.shape[0] if w_is_out_by_in else w.shape[1] for w in ws]

    def body(a_ref, *refs):
        av = a_ref[...]
        for w_ref, o_ref, n in zip(refs[:len(ws)], refs[len(ws):], ns):
            for c, wd in _col_chunks(n):
                if w_is_out_by_in:
                    part = lax.dot_general(av, w_ref[c:c + wd, :], NT_DIMS, preferred_element_type=F32)
                else:
                    part = jnp.dot(av, w_ref[:, c:c + wd], preferred_element_type=F32)
                o_ref[:, c:c + wd] = part.astype(o_ref.dtype)

    return _pcall(
        body, name=name, grid=(m // tm,),
        in_specs=[pl.BlockSpec((tm, k), lambda i: (i, 0))] + [pl.BlockSpec(w.shape, lambda i: (0, 0)) for w in ws],
        out_specs=[pl.BlockSpec((tm, n), lambda i: (i, 0)) for n in ns],
        out_shape=[jax.ShapeDtypeStruct((m, n), out_dtype) for n in ns],
        compiler_params=_params("parallel"),
    )(a, *ws)


def _mm_in(ds, ws, name, w_is_out_by_in=False, tm=ROW_TILE):
    m = ds[0].shape[0]
    k = ws[0].shape[1] if w_is_out_by_in else ws[0].shape[0]
    ns = [d.shape[1] for d in ds]

    def body(*refs):
        d_refs, w_refs, o_ref = refs[:len(ds)], refs[len(ds):2 * len(ds)], refs[-1]
        first = True
        for d_ref, w_ref, n in zip(d_refs, w_refs, ns):
            for c, wd in _col_chunks(n, 1024):
                if w_is_out_by_in:
                    part = jnp.dot(d_ref[:, c:c + wd], w_ref[c:c + wd, :], preferred_element_type=F32)
                else:
                    part = lax.dot_general(d_ref[:, c:c + wd], w_ref[:, c:c + wd], NT_DIMS, preferred_element_type=F32)
                if first:
                    o_ref[...] = part
                    first = False
                else:
                    o_ref[...] += part

    return _pcall(
        body, name=name, grid=(m // tm,),
        in_specs=[pl.BlockSpec((tm, n), lambda i: (i, 0)) for n in ns] + [pl.BlockSpec(w.shape, lambda i: (0, 0)) for w in ws],
        out_specs=pl.BlockSpec((tm, k), lambda i: (i, 0)),
        out_shape=jax.ShapeDtypeStruct((m, k), F32),
        compiler_params=_params("parallel"),
    )(*ds, *ws)


def _mm_tn(a, d, name):
    m, k = a.shape
    n = d.shape[1]
    tk = 512 if k % 512 == 0 else k

    def body(a_ref, d_ref, o_ref):
        o_ref[...] = lax.dot_general(a_ref[...], d_ref[...], TN_DIMS, preferred_element_type=F32)

    return _pcall(
        body, name=name, grid=(k // tk,),
        in_specs=[pl.BlockSpec((m, tk), lambda p: (0, p)), pl.BlockSpec((m, n), lambda p: (0, 0))],
        out_specs=pl.BlockSpec((tk, n), lambda p: (p, 0)),
        out_shape=jax.ShapeDtypeStruct((k, n), F32),
        compiler_params=_params("parallel"),
    )(a, d)


WGRAD_TILE = 512


def _proj_wgrad_all(dsegs, valid_rows, hb):
    m, k = hb.shape
    n_seg = len(dsegs)
    tiles, row = [], 0
    for si, (d, valid) in enumerate(zip(dsegs, valid_rows)):
        for c in range(0, valid, WGRAD_TILE):
            width = min(WGRAD_TILE, d.shape[1] - c)
            tiles.append((si, c, width, row + c, min(width, valid - c)))
        row += valid
    total_rows = row

    def body(*refs):
        d_refs, hb_ref, o_ref = refs[:n_seg], refs[n_seg], refs[n_seg + 1]
        a_buf, hb_buf, o_buf, load_sems, store_sems, hb_sem = refs[n_seg + 2:]

        def load(t):
            si, c, width, _, _ = tiles[t]
            return pltpu.make_async_copy(d_refs[si].at[:, pl.ds(c, width)], a_buf.at[t % 2, :, pl.ds(0, width)],
                                         load_sems.at[t % 2])

        def store(t):
            _, _, _, orow, valid = tiles[t]
            return pltpu.make_async_copy(o_buf.at[t % 2, pl.ds(0, valid), :], o_ref.at[pl.ds(orow, valid), :],
                                         store_sems.at[t % 2])

        hb_copy = pltpu.make_async_copy(hb_ref, hb_buf, hb_sem)
        hb_copy.start()
        load(0).start()
        hb_copy.wait()
        for t in range(len(tiles)):
            width = tiles[t][2]
            load(t).wait()
            if t + 1 < len(tiles):
                load(t + 1).start()
            if t >= 2:
                store(t - 2).wait()
            o_buf[t % 2, 0:width, :] = lax.dot_general(a_buf[t % 2, :, 0:width], hb_buf[...], TN_DIMS,
                                                        preferred_element_type=F32)
            store(t).start()
        for t in range(max(len(tiles) - 2, 0), len(tiles)):
            store(t).wait()

    any_spec = pl.BlockSpec(memory_space=pl.ANY)
    return _pcall(
        body, name="proj_wgrad",
        in_specs=[any_spec] * (n_seg + 1), out_specs=any_spec,
        out_shape=jax.ShapeDtypeStruct((total_rows, k), F32),
        scratch_shapes=[pltpu.VMEM((2, m, WGRAD_TILE), hb.dtype), pltpu.VMEM((m, k), hb.dtype),
                        pltpu.VMEM((2, WGRAD_TILE, k), F32), pltpu.SemaphoreType.DMA((2,)),
                        pltpu.SemaphoreType.DMA((2,)), pltpu.SemaphoreType.DMA],
        compiler_params=pltpu.CompilerParams(vmem_limit_bytes=VMEM_LIMIT),
    )(*dsegs, hb)


def _rms_in_fwd(x, norm_w):
    def body(x_ref, w_ref, h_ref):
        xv = x_ref[...]
        r = lax.rsqrt(jnp.mean(xv * xv, axis=-1, keepdims=True) + NORM_EPS)
        h_ref[...] = (xv * r * w_ref[...]).astype(h_ref.dtype)

    return _rows_call(body, "rms_in_fwd", x.shape[0], [(x, "tile"), (norm_w, "full")],
                      [(x.shape, MXU, "tile")])[0]


def _conv_taps(ext_ref, cw_ref, cols, tm):
    c = None
    for j in range(4):
        term = cw_ref[3 - j:4 - j, cols] * ext_ref[SUBLANES - j:SUBLANES - j + tm, cols]
        c = term if c is None else c + term
    return c


def _fill_ext(ext_ref, u_ref, halo_ref, first):
    ext_ref[0:SUBLANES, :] = jnp.where(first, 0.0, halo_ref[...])
    ext_ref[SUBLANES:, :] = u_ref[...]


def _dn_prep_fwd(qkv_pre, ba, conv_w8, alog_row, dtb_row):
    s = qkv_pre.shape[0]
    tm = ROW_TILE

    def body(u_ref, halo_ref, cw_ref, ba_ref, al_ref, dtb_ref, q_ref, k_ref, v_ref, bg_ref, ext_ref):
        _fill_ext(ext_ref, u_ref, halo_ref, pl.program_id(0) == 0)
        for h in range(3 * DN_HEADS):
            cols = slice(h * LANES, (h + 1) * LANES)
            c = _conv_taps(ext_ref, cw_ref, cols, tm)
            a = c * _sigmoid(c)
            oc = slice((h % DN_HEADS) * LANES, (h % DN_HEADS + 1) * LANES)
            if h < 2 * DN_HEADS:
                rinv = lax.rsqrt(jnp.sum(a * a, axis=-1, keepdims=True) + NORM_EPS)
                if h < DN_HEADS:
                    q_ref[:, oc] = a * (rinv * DN_DK ** -0.5)
                else:
                    k_ref[:, oc] = a * rinv
            else:
                v_ref[:, oc] = a
        bav = ba_ref[...]
        lane = lax.broadcasted_iota(jnp.int32, bav.shape, 1)
        beta = _sigmoid(bav)
        g = -jnp.exp(al_ref[...]) * _softplus(bav + dtb_ref[...])
        bg_ref[...] = jnp.where(lane < DN_HEADS, beta, jnp.where(lane < 2 * DN_HEADS, g, 0.0))

    return _rows_call(
        body, "dn_prep_fwd", s,
        [(qkv_pre, "tile"), (qkv_pre, "prev8"), (conv_w8, "full"), (ba, "tile"), (alog_row, "full"), (dtb_row, "full")],
        [((s, D_MODEL), F32, "tile")] * 3 + [((s, LANES), F32, "tile")],
        scratch=[pltpu.VMEM((tm + SUBLANES, QKV_W), F32)])


def _dn_prep_bwd(qkv_pre, ba, conv_w8, alog_row, dtb_row, dq, dk, dv, dbg):
    s = qkv_pre.shape[0]
    tm = ROW_TILE

    def body(u_ref, halo_ref, cw_ref, ba_ref, al_ref, dtb_ref, dq_ref, dk_ref, dv_ref, dbg_ref,
             dc_ref, dba_ref, dsmall_ref, ext_ref):
        _fill_ext(ext_ref, u_ref, halo_ref, pl.program_id(0) == 0)
        for h in range(3 * DN_HEADS):
            cols = slice(h * LANES, (h + 1) * LANES)
            oc = slice((h % DN_HEADS) * LANES, (h % DN_HEADS + 1) * LANES)
            c = _conv_taps(ext_ref, cw_ref, cols, tm)
            sg = _sigmoid(c)
            a = c * sg
            if h < 2 * DN_HEADS:
                rinv = lax.rsqrt(jnp.sum(a * a, axis=-1, keepdims=True) + NORM_EPS)
                dy = dq_ref[:, oc] * DN_DK ** -0.5 if h < DN_HEADS else dk_ref[:, oc]
                da = rinv * dy - a * (rinv * rinv * rinv) * jnp.sum(dy * a, axis=-1, keepdims=True)
            else:
                da = dv_ref[:, oc]
            dc_ref[:, cols] = da * (sg * (1.0 + c * (1.0 - sg)))
        bav = ba_ref[...]
        dbgv = dbg_ref[...]
        lane = lax.broadcasted_iota(jnp.int32, bav.shape, 1)
        beta = _sigmoid(bav)
        ea = jnp.exp(al_ref[...])
        z = bav + dtb_ref[...]
        g = -ea * _softplus(z)
        is_b = lane < DN_HEADS
        is_g = jnp.logical_and(lane >= DN_HEADS, lane < 2 * DN_HEADS)
        d_aa = jnp.where(is_g, dbgv * (-ea) * _sigmoid(z), 0.0)
        dba = jnp.where(is_b, dbgv * beta * (1.0 - beta), d_aa)
        dba_ref[...] = dba.astype(dba_ref.dtype)
        r_alog = jnp.sum(jnp.where(is_g, dbgv * g, 0.0), axis=0, keepdims=True)
        r_dtb = jnp.sum(d_aa, axis=0, keepdims=True)
        _acc_add(dsmall_ref, jnp.concatenate([r_alog, r_dtb, jnp.zeros((SUBLANES - 2, LANES), F32)], axis=0))

    return _rows_call(
        body, "dn_prep_bwd", s,
        [(qkv_pre, "tile"), (qkv_pre, "prev8"), (conv_w8, "full"), (ba, "tile"), (alog_row, "full"), (dtb_row, "full"),
         (dq, "tile"), (dk, "tile"), (dv, "tile"), (dbg, "tile")],
        [((s, QKV_W), F32, "tile"), ((s, LANES), MXU, "tile"), ((SUBLANES, LANES), F32, "acc")],
        scratch=[pltpu.VMEM((tm + SUBLANES, QKV_W), F32)])


def _conv_bwd(dc, qkv_pre, conv_w8):
    s = dc.shape[0]
    tm = ROW_TILE
    steps = s // tm

    def body(dc_ref, dnext_ref, u_ref, halo_ref, cw_ref, du_ref, dcw_ref, extd_ref, ext_ref):
        i = pl.program_id(0)
        _fill_ext(ext_ref, u_ref, halo_ref, i == 0)
        extd_ref[0:tm, :] = dc_ref[...]
        extd_ref[tm:, :] = jnp.where(i == steps - 1, 0.0, dnext_ref[...])

        @pl.when(i == 0)
        def _():
            dcw_ref[...] = jnp.zeros_like(dcw_ref)

        for h in range(3 * DN_HEADS):
            cols = slice(h * LANES, (h + 1) * LANES)
            du = None
            for j in range(4):
                term = cw_ref[3 - j:4 - j, cols] * extd_ref[j:j + tm, cols]
                du = term if du is None else du + term
            du_ref[:, cols] = du.astype(du_ref.dtype)
            dcv = dc_ref[:, cols]
            for j in range(4):
                row = jnp.sum(dcv * ext_ref[SUBLANES - j:SUBLANES - j + tm, cols], axis=0, keepdims=True)
                dcw_ref[3 - j:4 - j, cols] += row

    return _rows_call(
        body, "conv_bwd", s,
        [(dc, "tile"), (dc, "next8"), (qkv_pre, "tile"), (qkv_pre, "prev8"), (conv_w8, "full")],
        [((s, QKV_W), MXU, "tile"), ((SUBLANES, QKV_W), F32, "acc")],
        scratch=[pltpu.VMEM((tm + SUBLANES, QKV_W), F32), pltpu.VMEM((tm + SUBLANES, QKV_W), F32)])


def _dn_out_fwd(o, z, dnw_row, w_o_dn):
    def body(o_ref, z_ref, w_ref, wo_ref, on_ref, y_ref):
        for h in range(DN_HEADS):
            cols = slice(h * LANES, (h + 1) * LANES)
            ov = o_ref[:, cols]
            zv = z_ref[:, cols]
            ro = lax.rsqrt(jnp.mean(ov * ov, axis=-1, keepdims=True) + NORM_EPS)
            on_ref[:, cols] = (ov * ro * w_ref[...] * (zv * _sigmoid(zv))).astype(on_ref.dtype)
        y_ref[...] = jnp.dot(on_ref[...], wo_ref[...], preferred_element_type=F32)

    return _rows_call(body, "dn_out_fwd", o.shape[0], [(o, "tile"), (z, "tile"), (dnw_row, "full"), (w_o_dn, "full")],
                      [(o.shape, MXU, "tile"), ((o.shape[0], w_o_dn.shape[1]), F32, "tile")])


def _dn_out_bwd(dy, o, z, dnw_row, w_o_dn):
    def body(dy_ref, o_ref, z_ref, w_ref, wo_ref, do_ref, dz_ref, dw_ref, d_ref):
        d_ref[...] = lax.dot_general(dy_ref[...], wo_ref[...], NT_DIMS, preferred_element_type=F32)
        acc = jnp.zeros((1, LANES), F32)
        for h in range(DN_HEADS):
            cols = slice(h * LANES, (h + 1) * LANES)
            dv, ov, zv = d_ref[:, cols], o_ref[:, cols], z_ref[:, cols]
            sg = _sigmoid(zv)
            sz = zv * sg
            ro = lax.rsqrt(jnp.mean(ov * ov, axis=-1, keepdims=True) + NORM_EPS)
            nv = ov * ro
            dn = dv * w_ref[...] * sz
            acc = acc + jnp.sum(dv * nv * sz, axis=0, keepdims=True)
            dz_ref[:, cols] = (dv * nv * w_ref[...] * (sg * (1.0 + zv * (1.0 - sg)))).astype(dz_ref.dtype)
            do_ref[:, cols] = ro * dn - ov * (ro * ro * ro) * jnp.mean(dn * ov, axis=-1, keepdims=True)
        _acc_add(dw_ref, jnp.concatenate([acc, jnp.zeros((SUBLANES - 1, LANES), F32)], axis=0))

    return _rows_call(body, "dn_out_bwd", o.shape[0],
                      [(dy, "tile"), (o, "tile"), (z, "tile"), (dnw_row, "full"), (w_o_dn, "full")],
                      [(o.shape, F32, "tile"), (o.shape, MXU, "tile"), ((SUBLANES, LANES), F32, "acc")],
                      scratch=[pltpu.VMEM((ROW_TILE, o.shape[1]), F32)])


def _attn_out_fwd(parts, lses, zb, w_o_dil):
    def body(o0, o1, o2, l0, l1, l2, z_ref, wo_ref, lse_ref, o_ref, g_ref, y_ref):
        a, b, c = l0[...], l1[...], l2[...]
        m = jnp.maximum(a, jnp.maximum(b, c))
        ea, eb, ec = jnp.exp(a - m), jnp.exp(b - m), jnp.exp(c - m)
        den = ea + eb + ec
        out = (ea * o0[...] + eb * o1[...] + ec * o2[...]) / den
        lse_ref[...] = m + jnp.log(den)
        o_ref[...] = out
        zv = z_ref[...]
        gated = (out * (zv * _sigmoid(zv))).astype(g_ref.dtype)
        g_ref[...] = gated
        y_ref[...] = jnp.dot(gated, wo_ref[...], preferred_element_type=F32)

    s = zb.shape[0]
    return _rows_call(body, "attn_out_fwd", s,
                      [(p, "tile") for p in parts] + [(l, "tile") for l in lses] + [(zb, "tile"), (w_o_dil, "full")],
                      [((s, DIL_W), F32, "tile"), ((s, DIL_W), F32, "tile"), ((s, DIL_W), MXU, "tile"),
                       ((s, w_o_dil.shape[1]), F32, "tile")])


def _attn_out_bwd(dy, o_joint, zb, w_o_dil):
    def body(dy_ref, o_ref, z_ref, wo_ref, do_ref, dz_ref, dl_ref):
        zv = z_ref[...]
        sg = _sigmoid(zv)
        dv = lax.dot_general(dy_ref[...], wo_ref[...], NT_DIMS, preferred_element_type=F32)
        ov = o_ref[...]
        do = dv * (zv * sg)
        do_ref[...] = do
        dz_ref[...] = (dv * ov * (sg * (1.0 + zv * (1.0 - sg)))).astype(dz_ref.dtype)
        for h in range(DIL_HEADS):
            cols = slice(h * LANES, (h + 1) * LANES)
            dl_ref[:, cols] = jnp.broadcast_to(jnp.sum(do[:, cols] * ov[:, cols], axis=-1, keepdims=True),
                                               (do.shape[0], LANES))

    s = zb.shape[0]
    return _rows_call(body, "attn_out_bwd", s, [(dy, "tile"), (o_joint, "tile"), (zb, "tile"), (w_o_dil, "full")],
                      [((s, DIL_W), F32, "tile"), ((s, DIL_W), MXU, "tile"), ((s, DIL_W), F32, "tile")])


def _merge_out_final(ga, gb, ya, yb, x, target, w_out, wf_row):
    s, dm = x.shape

    def body(ga_ref, gb_ref, ya_ref, yb_ref, x_ref, t_ref, wo_ref, w_ref,
             loss_ref, dw_ref, m_ref, dxb_ref, dx_ref, dya_ref, dyb_ref, dga_ref, dgb_ref):
        sa, sb = _sigmoid(ga_ref[...]), _sigmoid(gb_ref[...])
        ya, yb = ya_ref[...], yb_ref[...]
        merged = (sa * ya + sb * yb).astype(MXU)
        m_ref[...] = merged
        x2 = x_ref[...] + jnp.dot(merged, wo_ref[...], preferred_element_type=F32)
        r = lax.rsqrt(jnp.mean(x2 * x2, axis=-1, keepdims=True) + NORM_EPS)
        w = w_ref[...]
        err = x2 * r * w - t_ref[...]
        tile_loss = 0.5 * jnp.sum(jnp.mean(err * err, axis=-1, keepdims=True), axis=0, keepdims=True)
        _acc_add(loss_ref, jnp.broadcast_to(tile_loss, (SUBLANES, LANES)))
        dy = err * (1.0 / dm)
        row = jnp.sum(dy * x2 * r, axis=0, keepdims=True)
        _acc_add(dw_ref, jnp.concatenate([row, jnp.zeros((SUBLANES - 1, dm), F32)], axis=0))
        dn = dy * w
        dx2 = r * dn - x2 * (r * r * r) * jnp.mean(dn * x2, axis=-1, keepdims=True)
        dx_ref[...] = dx2
        dxb = dx2.astype(MXU)
        dxb_ref[...] = dxb
        dmv = lax.dot_general(dxb, wo_ref[...], NT_DIMS, preferred_element_type=F32)
        dya_ref[...] = (dmv * sa).astype(dya_ref.dtype)
        dyb_ref[...] = (dmv * sb).astype(dyb_ref.dtype)
        dga_ref[...] = (dmv * ya * sa * (1.0 - sa)).astype(dga_ref.dtype)
        dgb_ref[...] = (dmv * yb * sb * (1.0 - sb)).astype(dgb_ref.dtype)

    return _rows_call(body, "merge_out_final", s,
                      [(ga, "tile"), (gb, "tile"), (ya, "tile"), (yb, "tile"), (x, "tile"), (target, "tile"),
                       (w_out, "full"), (wf_row, "full")],
                      [((SUBLANES, LANES), F32, "acc"), ((SUBLANES, dm), F32, "acc"), ((s, dm), MXU, "tile"),
                       ((s, dm), MXU, "tile"), ((s, dm), F32, "tile")] + [((s, dm), MXU, "tile")] * 4)


def _lane_pick(x, idx):
    lane = lax.broadcasted_iota(jnp.int32, x.shape, 1)
    return jnp.sum(jnp.where(lane == idx, x, 0.0), axis=-1, keepdims=True)


PAIR = 2 * DN_CHUNK
SCAN_CHUNKS = 4


def _bmm(a, b):
    return lax.dot_general(a.astype(MXU), b.astype(MXU), (((2,), (1,)), ((0,), (0,))), preferred_element_type=F32)


def _bmm_nt(a, b):
    return lax.dot_general(a.astype(MXU), b.astype(MXU), (((2,), (2,)), ((0,), (0,))), preferred_element_type=F32)


def _bmm_tn(a, b):
    return lax.dot_general(a.astype(MXU), b.astype(MXU), (((1,), (1,)), ((0,), (0,))), preferred_element_type=F32)


def _bmm3(a, b):
    ah = a.astype(jnp.bfloat16)
    al = (a - ah.astype(F32)).astype(jnp.bfloat16)
    bh = b.astype(jnp.bfloat16)
    bl = (b - bh.astype(F32)).astype(jnp.bfloat16)
    f = lambda p, q: lax.dot_general(p, q, (((2,), (1,)), ((0,), (0,))), preferred_element_type=F32)
    return f(ah, bh) + (f(ah, bl) + f(al, bh))


def _pair_masks():
    row = lax.broadcasted_iota(jnp.int32, (PAIR, PAIR), 0)
    col = lax.broadcasted_iota(jnp.int32, (PAIR, PAIR), 1)
    same = (row >= DN_CHUNK) == (col >= DN_CHUNK)
    return dict(causal=same & (row >= col), strict=same & (row > col), upper=same & (row <= col), eye=row == col,
                first=row < DN_CHUNK, row=row, lane=col)


def _pair_decay(bgv, masks):
    gc_all = _dot01(masks["causal"].astype(F32), bgv)
    out = []
    for h in range(DN_HEADS):
        beta = _lane_pick(bgv, h)
        gcb = jnp.broadcast_to(_lane_pick(gc_all, DN_HEADS + h), (PAIR, PAIR))
        gam = jnp.where(masks["causal"], jnp.exp(jnp.minimum(gcb - gcb.T, 0.0)), 0.0)
        gl = jnp.where(masks["first"], gcb[DN_CHUNK - 1:DN_CHUNK, :], gcb[PAIR - 1:PAIR, :])
        out.append((beta, gcb, gam, gl))
    return out


def _pair_inverse(a_strict, eye):
    eye_f = eye.astype(F32)[None]
    m = eye_f + a_strict
    x = eye_f - a_strict
    steps = int(math.log2(DN_CHUNK)) - 1
    for i in range(steps):
        mm = _bmm3 if i == steps - 1 else _bmm
        x = x + mm(x, eye_f - mm(m, x))
    return x


def _head_cols(h):
    return slice(h * LANES, (h + 1) * LANES)


def _delta_prep(q, k, v, bg):
    s = q.shape[0]
    c = DN_CHUNK
    n_chunks = s // c

    def body(q_ref, k_ref, v_ref, bg_ref, u_ref, w_ref, qd_ref, kd_ref, aqk_ref, dl_ref, t2_ref):
        masks = _pair_masks()
        dec = _pair_decay(bg_ref[...], masks)
        kbs, ks, gams, vbs, kbes, qs, qds, kds, dls = ([] for _ in range(9))
        for h in range(DN_HEADS):
            beta, gcb, gam, gl = dec[h]
            qh, kh, vh = q_ref[:, _head_cols(h)], k_ref[:, _head_cols(h)], v_ref[:, _head_cols(h)]
            eg = jnp.exp(gcb)
            kb = kh * beta
            kbs.append(kb); ks.append(kh); gams.append(gam); vbs.append(vh * beta); kbes.append(kb * eg)
            qs.append(qh); qds.append(qh * eg); kds.append(kh * jnp.exp(gl - gcb)); dls.append(jnp.exp(gl))
        st = lambda xs: jnp.stack(xs, axis=0)
        kmat, gam = st(ks), st(gams)
        a = jnp.where(masks["strict"][None], _bmm_nt(st(kbs), kmat) * gam, 0.0)
        t = _pair_inverse(a, masks["eye"])
        u = _bmm(t, st(vbs))
        w = _bmm(t, st(kbes))
        aqk = _bmm_nt(st(qs), kmat) * gam
        t2_ref[0] = t.astype(t2_ref.dtype)
        for half in range(2):
            rows = slice(half * c, (half + 1) * c)
            u_ref[half] = u[:, rows, :]
            w_ref[half] = w[:, rows, :].astype(w_ref.dtype)
            qd_ref[half] = st(qds)[:, rows, :].astype(qd_ref.dtype)
            kd_ref[half] = st(kds)[:, rows, :].astype(kd_ref.dtype)
            aqk_ref[half] = aqk[:, rows, rows].astype(aqk_ref.dtype)
            dl_ref[half] = st(dls)[:, half * c:half * c + SUBLANES, :]

    row_spec = lambda w_: pl.BlockSpec((PAIR, w_), lambda i: (i, 0))
    hm = lambda a_, b_: pl.BlockSpec((2, DN_HEADS, a_, b_), lambda i: (i, 0, 0, 0))
    hm_shape = lambda a_, b_, dt: jax.ShapeDtypeStruct((n_chunks, DN_HEADS, a_, b_), dt)
    return _pcall(
        body, name="delta_prep", grid=(n_chunks // 2,),
        in_specs=[row_spec(D_MODEL)] * 3 + [row_spec(LANES)],
        out_specs=[hm(c, LANES)] * 4 + [hm(c, c), hm(SUBLANES, LANES),
                   pl.BlockSpec((1, DN_HEADS, PAIR, PAIR), lambda i: (i, 0, 0, 0))],
        out_shape=[hm_shape(c, LANES, F32), hm_shape(c, LANES, MXU), hm_shape(c, LANES, MXU), hm_shape(c, LANES, MXU),
                   hm_shape(c, c, MXU), hm_shape(SUBLANES, LANES, F32),
                   jax.ShapeDtypeStruct((n_chunks // 2, DN_HEADS, PAIR, PAIR), MXU)],
        compiler_params=_params("parallel"),
    )(q, k, v, bg)


def _delta_scan_fwd(u, w, qd, kd, aqk, dl):
    n_chunks = u.shape[0]
    c = DN_CHUNK
    g_n = SCAN_CHUNKS

    def body(u_ref, w_ref, qd_ref, kd_ref, aqk_ref, dl_ref, o_ref, vnew_ref, st_ref, state):
        @pl.when(pl.program_id(0) == 0)
        def _():
            state[...] = jnp.zeros_like(state)

        for g in range(g_n):
            sv = state[...]
            sb = sv.astype(MXU)
            vnew = u_ref[g] - _bmm(w_ref[g], sb)
            o = _bmm(qd_ref[g], sb) + _bmm(aqk_ref[g], vnew)
            state[...] = sv * dl_ref[g][:, 0:1, :] + _bmm_tn(kd_ref[g], vnew)
            vnew_ref[g] = vnew.astype(vnew_ref.dtype)
            st_ref[g] = sb
            for h in range(DN_HEADS):
                o_ref[g * c:(g + 1) * c, _head_cols(h)] = o[h]

    hm = lambda a_, b_: pl.BlockSpec((g_n, DN_HEADS, a_, b_), lambda i: (i, 0, 0, 0))
    return _pcall(
        body, name="delta_scan_fwd", grid=(n_chunks // g_n,),
        in_specs=[hm(c, LANES)] * 4 + [hm(c, c), hm(SUBLANES, LANES)],
        out_specs=[pl.BlockSpec((g_n * c, D_MODEL), lambda i: (i, 0)), hm(c, LANES), hm(DN_DK, DN_DK)],
        out_shape=[jax.ShapeDtypeStruct((n_chunks * c, D_MODEL), F32),
                   jax.ShapeDtypeStruct((n_chunks, DN_HEADS, c, LANES), MXU),
                   jax.ShapeDtypeStruct((n_chunks, DN_HEADS, DN_DK, DN_DK), MXU)],
        scratch_shapes=[pltpu.VMEM((DN_HEADS, DN_DK, DN_DK), F32)],
        compiler_params=_params("arbitrary"),
    )(u, w, qd, kd, aqk, dl)


def _delta_scan_bwd(w, qd, kd, aqk, dl, vnew, st, do):
    n_chunks = w.shape[0]
    c = DN_CHUNK
    g_n = SCAN_CHUNKS
    steps = n_chunks // g_n

    def body(w_ref, qd_ref, kd_ref, aqk_ref, dl_ref, vnew_ref, st_ref, do_ref, dvnew_ref, dkd_ref, ddl_ref, dstate):
        @pl.when(pl.program_id(0) == 0)
        def _():
            dstate[...] = jnp.zeros_like(dstate)

        for g in reversed(range(g_n)):
            ds = dstate[...]
            dsb = ds.astype(MXU)
            doh = jnp.stack([do_ref[g * c:(g + 1) * c, _head_cols(h)] for h in range(DN_HEADS)], axis=0)
            dvnew = _bmm_tn(aqk_ref[g], doh) + _bmm(kd_ref[g], dsb)
            dkd_ref[g] = _bmm_nt(vnew_ref[g], dsb)
            ddl = jnp.sum(jnp.sum(st_ref[g].astype(F32) * ds, axis=2, keepdims=True), axis=1, keepdims=True)
            ddl_ref[g] = jnp.broadcast_to(ddl, (DN_HEADS, SUBLANES, LANES))
            dstate[...] = ds * dl_ref[g][:, 0:1, :] + _bmm_tn(qd_ref[g], doh) - _bmm_tn(w_ref[g], dvnew)
            dvnew_ref[g] = dvnew.astype(dvnew_ref.dtype)

    rev = lambda i: steps - 1 - i
    hm = lambda a_, b_: pl.BlockSpec((g_n, DN_HEADS, a_, b_), lambda i: (rev(i), 0, 0, 0))
    return _pcall(
        body, name="delta_scan_bwd", grid=(steps,),
        in_specs=[hm(c, LANES)] * 3 + [hm(c, c), hm(SUBLANES, LANES), hm(c, LANES), hm(DN_DK, DN_DK),
                  pl.BlockSpec((g_n * c, D_MODEL), lambda i: (rev(i), 0))],
        out_specs=[hm(c, LANES), hm(c, LANES), hm(SUBLANES, LANES)],
        out_shape=[jax.ShapeDtypeStruct((n_chunks, DN_HEADS, c, LANES), MXU),
                   jax.ShapeDtypeStruct((n_chunks, DN_HEADS, c, LANES), F32),
                   jax.ShapeDtypeStruct((n_chunks, DN_HEADS, SUBLANES, LANES), F32)],
        scratch_shapes=[pltpu.VMEM((DN_HEADS, DN_DK, DN_DK), F32)],
        compiler_params=_params("arbitrary"),
    )(w, qd, kd, aqk, dl, vnew, st, do)


def _delta_post_bwd(q, k, v, bg, t2, st, vnew, do, dvnew, dkd, ddl):
    s = q.shape[0]
    c = DN_CHUNK

    def body(q_ref, k_ref, v_ref, bg_ref, t2_ref, st_ref, vnew_ref, do_ref, dvnew_ref, dkd_ref, ddl_ref,
             dq_ref, dk_ref, dv_ref, dbg_ref):
        masks = _pair_masks()
        first = masks["first"][None]
        dec = _pair_decay(bg_ref[...], masks)
        st_ = lambda xs: jnp.stack(xs, axis=0)
        heads = range(DN_HEADS)
        qm_, km_, vm_, dom = (st_([r[:, _head_cols(h)] for h in heads]) for r in (q_ref, k_ref, v_ref, do_ref))
        beta = st_([dec[h][0] for h in heads])
        gcb = st_([dec[h][1] for h in heads])
        gam = st_([dec[h][2] for h in heads])
        gl = st_([dec[h][3] for h in heads])
        pair = lambda ref: jnp.concatenate([ref[0], ref[1]], axis=1)
        vnew2, dvnew2, dkd2 = pair(vnew_ref), pair(dvnew_ref), pair(dkd_ref)
        halves = lambda x: (x[:, :c, :], x[:, c:, :])
        by_state = lambda x: jnp.concatenate([_bmm_nt(xh, st_ref[i]) for i, xh in enumerate(halves(x))], axis=1)
        dqd = by_state(dom)
        dw = -by_state(dvnew2)
        ddl2 = jnp.where(first, ddl_ref[0][:, 0:1, :], ddl_ref[1][:, 0:1, :])

        eg = jnp.exp(gcb)
        egl = jnp.exp(gl - gcb)
        dl = jnp.exp(gl)
        kb = km_ * beta
        kk = _bmm_nt(kb, km_)
        a = jnp.where(masks["strict"][None], kk * gam, 0.0)
        t = t2_ref[0]
        vb = vm_ * beta
        kbe = kb * eg
        u = _bmm(t, vb)
        w = _bmm(t, kbe)
        aqk = _bmm_nt(qm_, km_) * gam
        qd = qm_ * eg
        kd = km_ * egl

        daqk = jnp.where(masks["causal"][None], _bmm_nt(dom, vnew2), 0.0)
        dvb = _bmm_tn(t, dvnew2)
        dkbe = _bmm_tn(t, dw)
        da = jnp.where(masks["strict"][None], -(_bmm_nt(dvb, u) + _bmm_nt(dkbe, w)), 0.0)
        pm = da * gam
        qmm = daqk * gam
        dkb = _bmm(pm, km_) + dkbe * eg
        dkh = _bmm_tn(pm, kb) + _bmm_tn(qmm, qm_) + dkd2 * egl + dkb * beta
        dqh = _bmm(qmm, km_) + dqd * eg
        xm = da * a + daqk * aqk
        ones = jnp.ones((DN_HEADS, PAIR, LANES), F32)
        hi, mid, lo = _split3(xm)
        colsum = _bmm_tn(hi, ones) + (_bmm_tn(mid, ones) + _bmm_tn(lo, ones))
        tmp = jnp.sum(dkd2 * kd, axis=-1, keepdims=True)
        dgc = (jnp.sum(xm, axis=-1, keepdims=True) - colsum + jnp.sum(dkbe * kbe, axis=-1, keepdims=True)
               + jnp.sum(dqd * qd, axis=-1, keepdims=True) - tmp)
        sum0 = jnp.sum(jnp.where(first, tmp, 0.0), axis=1, keepdims=True)
        sum1 = jnp.sum(jnp.where(first, 0.0, tmp), axis=1, keepdims=True)
        dgl = jnp.where(first, sum0, sum1) + ddl2 * dl
        last = (masks["row"] == c - 1) | (masks["row"] == PAIR - 1)
        dgc = dgc + jnp.where(last[None], dgl, 0.0)
        dbeta = jnp.sum(dvb * vm_, axis=-1, keepdims=True) + jnp.sum(dkb * km_, axis=-1, keepdims=True)
        dvh = dvb * beta

        lane = masks["lane"]
        dgc_lanes = jnp.zeros((PAIR, LANES), F32)
        dbg = jnp.zeros((PAIR, LANES), F32)
        for h in heads:
            dq_ref[:, _head_cols(h)] = dqh[h]
            dk_ref[:, _head_cols(h)] = dkh[h]
            dv_ref[:, _head_cols(h)] = dvh[h]
            dgc_lanes = dgc_lanes + jnp.where(lane == DN_HEADS + h, dgc[h], 0.0)
            dbg = dbg + jnp.where(lane == h, dbeta[h], 0.0)
        dbg_ref[...] = dbg + _dot01(masks["upper"].astype(F32), dgc_lanes)

    n_pairs = s // PAIR
    row_spec = lambda w_: pl.BlockSpec((PAIR, w_), lambda i: (i, 0))
    hm = lambda a_, b_: pl.BlockSpec((2, DN_HEADS, a_, b_), lambda i: (i, 0, 0, 0))
    return _pcall(
        body, name="delta_post_bwd", grid=(n_pairs,),
        in_specs=[row_spec(D_MODEL)] * 3 + [row_spec(LANES), pl.BlockSpec((1, DN_HEADS, PAIR, PAIR), lambda i: (i, 0, 0, 0)),
                  hm(DN_DK, DN_DK), hm(c, LANES), row_spec(D_MODEL), hm(c, LANES), hm(c, LANES), hm(SUBLANES, LANES)],
        out_specs=[row_spec(D_MODEL)] * 3 + [row_spec(LANES)],
        out_shape=[jax.ShapeDtypeStruct((s, D_MODEL), F32)] * 3 + [jax.ShapeDtypeStruct((s, LANES), F32)],
        compiler_params=_params("parallel"),
    )(q, k, v, bg, t2, st, vnew, do, dvnew, dkd, ddl)


def _alibi_slope(group, head):
    n = N_DIL * DIL_HEADS
    return float(2.0 ** (-8.0 * (group * DIL_HEADS + head + 1) / n))


def _attn_plan(s, group):
    window, dil = DIL_GROUPS[group]
    assert window // dil == ATT_BLOCK
    assert (s // dil) % ATT_BLOCK == 0, "sub-sequence length must be a whole number of attention blocks"
    return dil, s // dil // ATT_BLOCK, (DIL_HEADS if dil == 1 else 1)


def _attn_specs(group, dil, nb, hp):
    rows = ATT_BLOCK * dil

    def spec(col0, shift):
        if shift < 0:
            f = lambda hb, n: (jnp.maximum(n - 1, 0), col0 + hb)
        elif shift > 0:
            f = lambda hb, n: (jnp.minimum(n + 1, nb - 1), col0 + hb)
        else:
            f = lambda hb, n: (jnp.minimum(n, nb - 1), col0 + hb)
        return pl.BlockSpec((rows, hp * LANES), f)

    return (lambda shift: spec(group * (DIL_HEADS // hp), shift)), (lambda shift: spec(0, shift))


def _sub_rows(ref, r, dil, cols):
    return ref[:, cols] if dil == 1 else ref[pl.ds(r, ATT_BLOCK, stride=dil), cols]


def _set_sub_rows(ref, r, dil, cols, value):
    if dil == 1:
        ref[:, cols] = value
    else:
        ref[pl.ds(r, ATT_BLOCK, stride=dil), cols] = value


def _step_slope(group, hp, hh):
    if hp == DIL_HEADS:
        return _alibi_slope(group, hh)
    hb = pl.program_id(0)
    slope = _alibi_slope(group, DIL_HEADS - 1)
    for h in reversed(range(DIL_HEADS - 1)):
        slope = jnp.where(hb == h, _alibi_slope(group, h), slope)
    return slope


def _window_bias(dil, n):
    a = lax.broadcasted_iota(jnp.int32, (ATT_BLOCK, 2 * ATT_BLOCK), 0)
    b = lax.broadcasted_iota(jnp.int32, (ATT_BLOCK, 2 * ATT_BLOCK), 1)
    dist = ATT_BLOCK + a - b
    valid = (dist >= 0) & (dist <= ATT_BLOCK) & ((b >= ATT_BLOCK) | (n > 0))
    return (dist * dil).astype(F32), valid


def _attn_fwd(qb, kb, vb, group):
    s = qb.shape[0]
    dil, nb, hp = _attn_plan(s, group)
    qkv, per_head = _attn_specs(group, dil, nb, hp)

    def body(q_ref, kp_ref, kc_ref, vp_ref, vc_ref, o_ref, lse_ref):
        n = pl.program_id(1)
        distd, valid = _window_bias(dil, n)
        for hh in range(hp):
            cols = _head_cols(hh)
            slope = _step_slope(group, hp, hh)
            for r in range(dil):
                sub = lambda ref: _sub_rows(ref, r, dil, cols).astype(MXU)
                kk = jnp.concatenate([sub(kp_ref), sub(kc_ref)], axis=0)
                vv = jnp.concatenate([sub(vp_ref), sub(vc_ref)], axis=0)
                sc = _dot_nt(sub(q_ref), kk) * DIL_DH ** -0.5 - slope * distd
                sc = jnp.where(valid, sc, -1e30)
                mx = jnp.max(sc, axis=-1, keepdims=True)
                p = jnp.where(valid, jnp.exp(sc - mx), 0.0)
                den = jnp.sum(p, axis=-1, keepdims=True)
                _set_sub_rows(o_ref, r, dil, cols, _dot(p, vv) / den)
                _set_sub_rows(lse_ref, r, dil, cols, jnp.broadcast_to(mx + jnp.log(den), (ATT_BLOCK, LANES)))

    return _pcall(
        body, name=f"attn_fwd_g{group}", grid=(DIL_HEADS // hp, nb),
        in_specs=[qkv(0), qkv(-1), qkv(0), qkv(-1), qkv(0)], out_specs=[per_head(0)] * 2,
        out_shape=[jax.ShapeDtypeStruct((s, DIL_W), F32)] * 2,
        compiler_params=_params("parallel", "parallel"),
    )(qb, kb, kb, vb, vb)


def _attn_bwd(qb, kb, vb, d_o, lse, delta, group):
    s = qb.shape[0]
    dil, nb, hp = _attn_plan(s, group)
    qkv, per_head = _attn_specs(group, dil, nb, hp)
    scale = DIL_DH ** -0.5

    def body(q_ref, kp_ref, kc_ref, vp_ref, vc_ref, do_ref, l_ref, dl_ref, dq_ref, dk_ref, dv_ref,
             dq_acc, dk_done, dv_done, dk_carry, dv_carry):
        n = pl.program_id(1)
        slopes = [_step_slope(group, hp, hh) for hh in range(hp)]

        @pl.when(n == 0)
        def _():
            dk_carry[...] = jnp.zeros_like(dk_carry)
            dv_carry[...] = jnp.zeros_like(dv_carry)

        @pl.when(n < nb)
        def _():
            distd, valid = _window_bias(dil, n)
            for hh in range(hp):
                cols = _head_cols(hh)
                slope = slopes[hh]
                for r in range(dil):
                    sub = lambda ref: _sub_rows(ref, r, dil, cols)
                    qc, do = sub(q_ref).astype(MXU), sub(do_ref).astype(MXU)
                    kk = jnp.concatenate([sub(kp_ref).astype(MXU), sub(kc_ref).astype(MXU)], axis=0)
                    vv = jnp.concatenate([sub(vp_ref).astype(MXU), sub(vc_ref).astype(MXU)], axis=0)
                    sc = _dot_nt(qc, kk) * scale - slope * distd
                    p = jnp.where(valid, jnp.exp(jnp.minimum(sc - jnp.concatenate([sub(l_ref)] * 2, axis=1), 0.0)), 0.0)
                    dsc = p * (_dot_nt(do, vv) - jnp.concatenate([sub(dl_ref)] * 2, axis=1))
                    _set_sub_rows(dq_acc, r, dil, cols, _dot(dsc, kk) * scale)
                    dkk = _dot_tn(dsc, qc) * scale
                    dvv = _dot_tn(p, do)
                    _set_sub_rows(dk_done, r, dil, cols, _sub_rows(dk_carry, r, dil, cols) + dkk[:ATT_BLOCK])
                    _set_sub_rows(dv_done, r, dil, cols, _sub_rows(dv_carry, r, dil, cols) + dvv[:ATT_BLOCK])
                    _set_sub_rows(dk_carry, r, dil, cols, dkk[ATT_BLOCK:])
                    _set_sub_rows(dv_carry, r, dil, cols, dvv[ATT_BLOCK:])
            dq_ref[...] = dq_acc[...].astype(dq_ref.dtype)
            dk_ref[...] = dk_done[...].astype(dk_ref.dtype)
            dv_ref[...] = dv_done[...].astype(dv_ref.dtype)

        @pl.when(n == nb)
        def _():
            dk_ref[...] = dk_carry[...].astype(dk_ref.dtype)
            dv_ref[...] = dv_carry[...].astype(dv_ref.dtype)

    return _pcall(
        body, name=f"attn_bwd_g{group}", grid=(DIL_HEADS // hp, nb + 1),
        in_specs=[qkv(0), qkv(-1), qkv(0), qkv(-1), qkv(0)] + [per_head(0)] * 3,
        out_specs=[per_head(0), per_head(-1), per_head(-1)],
        out_shape=[jax.ShapeDtypeStruct((s, DIL_W), MXU)] * 3,
        scratch_shapes=[pltpu.VMEM((ATT_BLOCK * dil, hp * LANES), F32)] * 5,
        compiler_params=_params("parallel", "arbitrary"),
    )(qb, kb, kb, vb, vb, d_o, lse, delta)


def _my_place():
    mx, my, mc = lax.axis_index("x"), lax.axis_index("y"), lax.axis_index("c")
    return mx, my, mc, 4 * mx + 2 * my + mc


N_CHIPS = 4


def _shard_row_tile(r):
    if r <= 512:
        return r
    return 128 if r % 128 == 0 else 480


def _other_chips(mx, my):
    return [(1 - mx, my), (mx, 1 - my), (1 - mx, 1 - my)]


def _all_gather(xs, name):
    n = len(xs)

    def body(*refs):
        x_refs, o_refs = refs[:n], refs[n:2 * n]
        send_sems, recv_sems, local_sems = refs[2 * n:]
        mx, my, mc, me = _my_place()
        sibling, sibling_id = (mx, my, 1 - mc), 4 * mx + 2 * my + (1 - mc)
        chips = _other_chips(mx, my)

        def copy(a, k, slot, to, src=None):
            dst = o_refs[a].at[slot]
            return pltpu.make_async_remote_copy(
                src_ref=dst if src is None else src, dst_ref=dst, send_sem=send_sems.at[a, k],
                recv_sem=recv_sems.at[a, k], device_id=to, device_id_type=MESH)

        local = [pltpu.make_async_copy(x_refs[a], o_refs[a].at[me], local_sems.at[a]) for a in range(n)]
        for cp in local:
            cp.start()
        sends = []
        for a in range(n):
            sends.append(copy(a, 0, me, sibling, src=x_refs[a]))
            sends += [copy(a, 1 + j, me, (px, py, mc), src=x_refs[a]) for j, (px, py) in enumerate(chips)]
        for cp in sends:
            cp.start()
        for j, (px, py) in enumerate(chips):
            slot = 4 * px + 2 * py + mc
            for a in range(n):
                copy(a, 1 + j, slot, (px, py, mc)).wait_recv()
                passed = copy(a, 4 + j, slot, sibling)
                passed.start()
                sends.append(passed)
        for a in range(n):
            copy(a, 0, sibling_id, sibling).wait_recv()
            for j, (px, py) in enumerate(chips):
                copy(a, 4 + j, 4 * px + 2 * py + (1 - mc), sibling).wait_recv()
        for cp in sends:
            cp.wait_send()
        for cp in local:
            cp.wait()

    any_spec = pl.BlockSpec(memory_space=pl.ANY)
    return _pcall(
        body, name=name,
        in_specs=[any_spec] * n, out_specs=[any_spec] * n,
        out_shape=[jax.ShapeDtypeStruct((N_DEV,) + x.shape, x.dtype) for x in xs],
        scratch_shapes=[pltpu.SemaphoreType.DMA((n, N_DEV - 1)), pltpu.SemaphoreType.DMA((n, N_DEV - 1)),
                        pltpu.SemaphoreType.DMA((n,))],
    )(*xs)


def _pair_exchange(gs, name):
    n = len(gs)

    def body(*refs):
        g_refs, o_refs = refs[:n], refs[n:2 * n]
        send_sems, recv_sems = refs[2 * n:]
        mx, my, mc, _ = _my_place()
        copies = [pltpu.make_async_remote_copy(
            src_ref=g_refs[a].at[p, 1 - mc], dst_ref=o_refs[a].at[p], send_sem=send_sems.at[a, p],
            recv_sem=recv_sems.at[a, p], device_id=(mx, my, 1 - mc), device_id_type=MESH)
            for a in range(n) for p in range(N_CHIPS)]
        for cp in copies:
            cp.start()
        for cp in copies:
            cp.wait()

    any_spec = pl.BlockSpec(memory_space=pl.ANY)
    return _pcall(
        body, name=name,
        in_specs=[any_spec] * n, out_specs=[any_spec] * n,
        out_shape=[jax.ShapeDtypeStruct((N_CHIPS,) + g.shape[2:], g.dtype) for g in gs],
        scratch_shapes=[pltpu.SemaphoreType.DMA((n, N_CHIPS)), pltpu.SemaphoreType.DMA((n, N_CHIPS))],
    )(*gs)


def _pair_add(g, other, name):
    chips, _, r, c = g.shape
    tr = _shard_row_tile(r)
    core = lax.axis_index("c").astype(jnp.int32).reshape(1)

    def body(core_ref, g_ref, o_ref, h_ref):
        h_ref[...] = (g_ref[...].astype(F32)[0] + o_ref[...].astype(F32)).astype(h_ref.dtype)

    blk = pl.BlockSpec((1, tr, c), lambda p, i, core_ref: (p, i, 0))
    return _pcall(
        body, name=name,
        grid_spec=pltpu.PrefetchScalarGridSpec(
            num_scalar_prefetch=1, grid=(chips, pl.cdiv(r, tr)),
            in_specs=[pl.BlockSpec((1, 1, tr, c), lambda p, i, core_ref: (p, core_ref[0], i, 0)), blk],
            out_specs=blk),
        out_shape=jax.ShapeDtypeStruct((chips, r, c), g.dtype),
        compiler_params=_params("parallel", "parallel"),
    )(core, g, other)


def _chip_exchange(hs, name):
    n = len(hs)

    def body(*refs):
        h_refs, o_refs = refs[:n], refs[n:2 * n]
        send_sems, recv_sems, local_sems = refs[2 * n:]
        mx, my, mc, _ = _my_place()
        my_chip = 2 * mx + my
        chips = _other_chips(mx, my)
        local = [pltpu.make_async_copy(h_refs[a].at[my_chip], o_refs[a].at[my_chip], local_sems.at[a]) for a in range(n)]
        for cp in local:
            cp.start()
        for j, (px, py) in enumerate(chips):
            for a in range(n):
                pltpu.make_async_remote_copy(
                    src_ref=h_refs[a].at[2 * px + py], dst_ref=o_refs[a].at[my_chip], send_sem=send_sems.at[a, j],
                    recv_sem=recv_sems.at[a, j], device_id=(px, py, mc), device_id_type=MESH).start()
        for j, (px, py) in enumerate(chips):
            for a in range(n):
                pltpu.make_async_remote_copy(
                    src_ref=h_refs[a].at[2 * px + py], dst_ref=o_refs[a].at[2 * px + py], send_sem=send_sems.at[a, j],
                    recv_sem=recv_sems.at[a, j], device_id=(px, py, mc), device_id_type=MESH).wait()
        for cp in local:
            cp.wait()

    any_spec = pl.BlockSpec(memory_space=pl.ANY)
    return _pcall(
        body, name=name,
        in_specs=[any_spec] * n, out_specs=[any_spec] * n,
        out_shape=[jax.ShapeDtypeStruct(h.shape, h.dtype) for h in hs],
        scratch_shapes=[pltpu.SemaphoreType.DMA((n, N_CHIPS - 1)), pltpu.SemaphoreType.DMA((n, N_CHIPS - 1)),
                        pltpu.SemaphoreType.DMA((n,))],
    )(*hs)


def _adamw(parts, w, m, v, name):
    r, c = w.shape
    n_parts = parts.shape[0]
    tr = _shard_row_tile(r)
    bc1 = 1.0 - ADAM_B1 ** ADAM_STEP
    bc2 = 1.0 - ADAM_B2 ** ADAM_STEP

    def body(p_ref, w_ref, m_ref, v_ref, g_ref, d_ref, nm_ref, nv_ref):
        g = p_ref[0].astype(F32)
        for j in range(1, n_parts):
            g = g + p_ref[j].astype(F32)
        nm = ADAM_B1 * m_ref[...] + (1.0 - ADAM_B1) * g
        nv = ADAM_B2 * v_ref[...] + (1.0 - ADAM_B2) * (g * g)
        g_ref[...] = g
        nm_ref[...] = nm
        nv_ref[...] = nv
        d_ref[...] = -ADAM_LR * ((nm / bc1) / (jnp.sqrt(nv / bc2) + ADAM_EPS) + ADAM_WD * w_ref[...])

    blk = pl.BlockSpec((tr, c), lambda i: (i, 0))
    return _pcall(
        body, name=name, grid=(pl.cdiv(r, tr),),
        in_specs=[pl.BlockSpec((n_parts, tr, c), lambda i: (0, i, 0)), blk, blk, blk],
        out_specs=[blk] * 4, out_shape=[jax.ShapeDtypeStruct((r, c), F32)] * 4,
        compiler_params=_params("parallel"),
    )(parts, w, m, v)


def _local_step(x, target, norm_w, w_segs, conv_w, a_log, dt_bias, dn_norm_w, w_o_dn, w_o_dil, w_out, final_norm_w):
    s = x.shape[0]
    w_qkv, w_za, w_ba, w_qb, w_kb, w_vb, w_zb, w_ga, w_gb = w_segs
    conv_w8 = jnp.concatenate([conv_w, jnp.zeros((SUBLANES - conv_w.shape[0], QKV_W), F32)], axis=0)
    pad8 = jnp.zeros((1, DN_HEADS), F32)
    alog_row = jnp.concatenate([pad8, a_log, jnp.zeros((1, LANES - 2 * DN_HEADS), F32)], axis=1)
    dtb_row = jnp.concatenate([pad8, dt_bias, jnp.zeros((1, LANES - 2 * DN_HEADS), F32)], axis=1)
    wf_row = final_norm_w.reshape(1, D_MODEL)

    hb = _rms_in_fwd(x, norm_w)
    qkv_pre, z_a, ba, z_b = _mm_out(hb, [w_qkv, w_za, w_ba, w_zb], "proj_fwd_a", w_is_out_by_in=True)
    q_b, k_b, v_b, g_a, g_b = _mm_out(hb, [w_qb, w_kb, w_vb, w_ga, w_gb], "proj_fwd_b", w_is_out_by_in=True)

    qn, kn, vn, bg = _dn_prep_fwd(qkv_pre, ba, conv_w8, alog_row, dtb_row)
    u_d, w_d, qd_d, kd_d, aqk_d, dl_d, t2_d = _delta_prep(qn, kn, vn, bg)
    o_a, vnew_d, st_d = _delta_scan_fwd(u_d, w_d, qd_d, kd_d, aqk_d, dl_d)
    on_b, y_a = _dn_out_fwd(o_a, z_a, dn_norm_w, w_o_dn)

    parts, lses = [], []
    for gi in range(N_DIL):
        o_g, l_g = _attn_fwd(q_b, k_b, v_b, gi)
        parts.append(o_g)
        lses.append(l_g)
    lse, o_joint, ob_b, y_b = _attn_out_fwd(parts, lses, z_b, w_o_dil)

    loss8, dwf8, merged_b, dx2_b, dx2, dya_b, dyb_b, dga_b, dgb_b = _merge_out_final(
        g_a, g_b, y_a, y_b, x, target, w_out, wf_row)

    g_w_out = _mm_tn(merged_b, dx2_b, "out_wgrad")
    g_w_o_dn = _mm_tn(on_b, dya_b, "out_dn_wgrad")
    d_o_a, dza_b, ddnw8 = _dn_out_bwd(dya_b, o_a, z_a, dn_norm_w, w_o_dn)

    g_w_o_dil = _mm_tn(ob_b, dyb_b, "out_dil_wgrad")
    d_o, dzb_b, delta = _attn_out_bwd(dyb_b, o_joint, z_b, w_o_dil)
    dqs, dks, dvs = [], [], []
    for gi in range(N_DIL):
        dq_g, dk_g, dv_g = _attn_bwd(q_b, k_b, v_b, d_o, lse, delta, gi)
        dqs.append(dq_g)
        dks.append(dk_g)
        dvs.append(dv_g)

    dvnew_d, dkd_d, ddl_d = _delta_scan_bwd(w_d, qd_d, kd_d, aqk_d, dl_d, vnew_d, st_d, d_o_a)
    dqn, dkn, dvn, dbg = _delta_post_bwd(qn, kn, vn, bg, t2_d, st_d, vnew_d, d_o_a, dvnew_d, dkd_d, ddl_d)
    dc, dba_b, dsmall8 = _dn_prep_bwd(qkv_pre, ba, conv_w8, alog_row, dtb_row, dqn, dkn, dvn, dbg)
    dqkv_b, dconv8 = _conv_bwd(dc, qkv_pre, conv_w8)

    per_group = lambda w: [w[g * DIL_W:(g + 1) * DIL_W] for g in range(N_DIL)]
    dh_b = _mm_in(dqs + dks + dvs + [dga_b, dgb_b],
                  per_group(w_qb) + per_group(w_kb) + per_group(w_vb) + [w_ga, w_gb], "proj_bwd_b", w_is_out_by_in=True)
    dsegs = [dqkv_b, dza_b, dba_b] + dqs + dks + dvs + [dzb_b, dga_b, dgb_b]
    valid_rows = [d.shape[1] for d in dsegs]
    valid_rows[2] = 2 * DN_HEADS
    g_wt = _proj_wgrad_all(dsegs, valid_rows, hb)
    grad_x, dnw8 = _proj_bwd_rms_in([dqkv_b, dza_b, dba_b, dzb_b], [w_qkv, w_za, w_ba, w_zb], dh_b, x, dx2, norm_w)

    small = dict(norm_w=dnw8[0:1], final_norm_w=dwf8[0:1], dn_norm_w=ddnw8[0:1],
                 a_log=dsmall8[0:1, DN_HEADS:2 * DN_HEADS], dt_bias=dsmall8[1:2, DN_HEADS:2 * DN_HEADS])
    return loss8[0:1, 0:1], grad_x, g_wt, dconv8[0:4], g_w_o_dn, g_w_o_dil, g_w_out, small


def _proj_bwd_rms_in(ds, ws, dh_a, x, dx2, norm_w):
    n_seg = len(ds)

    def body(*refs):
        d_refs, w_refs = refs[:n_seg], refs[n_seg:2 * n_seg]
        da_ref, x_ref, dx2_ref, w_ref, dx_ref, dw_ref = refs[2 * n_seg:]
        dx_ref[...] = da_ref[...]
        for d_ref, wt_ref in zip(d_refs, w_refs):
            for c, wd in _col_chunks(d_ref.shape[1], 1024):
                dx_ref[...] += jnp.dot(d_ref[:, c:c + wd], wt_ref[c:c + wd, :], preferred_element_type=F32)
        xv = x_ref[...]
        r = lax.rsqrt(jnp.mean(xv * xv, axis=-1, keepdims=True) + NORM_EPS)
        dhv = dx_ref[...]
        dn = dhv * w_ref[...]
        dx_ref[...] = dx2_ref[...] + r * dn - xv * (r * r * r) * jnp.mean(dn * xv, axis=-1, keepdims=True)
        row = jnp.sum(dhv * xv * r, axis=0, keepdims=True)
        _acc_add(dw_ref, jnp.concatenate([row, jnp.zeros((SUBLANES - 1, row.shape[1]), F32)], axis=0))

    return _rows_call(body, "proj_bwd_b_rms_in", x.shape[0],
                      [(d, "tile") for d in ds] + [(w, "full") for w in ws]
                      + [(dh_a, "tile"), (x, "tile"), (dx2, "tile"), (norm_w, "full")],
                      [(x.shape, F32, "tile"), ((SUBLANES, x.shape[1]), F32, "acc")])


def _split_proj_rows(wt_full):
    offs = [0]
    for n in PROJ_SIZES:
        offs.append(offs[-1] + n)
    seg = lambda a, b: wt_full[offs[a]:offs[b]]
    w_ba = jnp.concatenate([seg(4, 6), jnp.zeros((LANES - 2 * DN_HEADS, wt_full.shape[1]), wt_full.dtype)], axis=0)
    return [seg(0, 3), seg(3, 4), w_ba, seg(6, 7), seg(7, 8), seg(8, 9), seg(9, 10), seg(10, 11), seg(11, 12)]


def _pack_small(norm_w, final_norm_w, dn_norm_w, a_log, dt_bias):
    pad = lambda r: jnp.concatenate([r, jnp.zeros((1, D_MODEL - r.shape[1]), F32)], axis=1)
    rows = [pad(norm_w.reshape(1, -1)), pad(final_norm_w.reshape(1, -1)), pad(dn_norm_w.reshape(1, -1)),
            pad(a_log.reshape(1, -1)), pad(dt_bias.reshape(1, -1)), jnp.zeros((SUBLANES - 5, D_MODEL), F32)]
    return jnp.concatenate(rows, axis=0)


def _unpack_small(p):
    return dict(norm_w=p[0:1], final_norm_w=p[1], dn_norm_w=p[2:3, :DN_DK], a_log=p[3:4, :DN_HEADS],
                dt_bias=p[4:5, :DN_HEADS])


def kernel(x, norm_w, w_in, conv_w, a_log, dt_bias, dn_norm_w, w_o_dn, w_o_dil, w_out, final_norm_w, loss_target, m_norm_w, m_w_in, m_conv_w, m_a_log, m_dt_bias, m_dn_norm_w, m_w_o_dn, m_w_o_dil, m_w_out, m_final_norm_w, v_norm_w, v_w_in, v_conv_w, v_a_log, v_dt_bias, v_dn_norm_w, v_w_o_dn, v_w_o_dil, v_w_out, v_final_norm_w):
    shard_w = w_in.shape[2]
    wt, m_wt, v_wt = (jnp.transpose(t[0]) for t in (w_in, m_w_in, v_w_in))
    gathered = _all_gather([wt.astype(MXU), w_o_dn[0].astype(MXU), w_o_dil[0].astype(MXU), w_out[0].astype(MXU),
                            conv_w[0]], "gather_weights")
    w_in_all, w_o_dn_all, w_o_dil_all, w_out_all, conv_all = gathered
    wt_full = w_in_all.reshape(N_DEV * shard_w, D_MODEL)
    w_o_dn_full = w_o_dn_all.reshape(D_MODEL, D_MODEL)
    w_o_dil_full = jnp.transpose(w_o_dil_all, (1, 0, 2)).reshape(DIL_W, D_MODEL)
    w_out_full = w_out_all.reshape(D_MODEL, D_MODEL)
    conv_full = jnp.transpose(conv_all, (1, 0, 2)).reshape(conv_w.shape[1], QKV_W)

    loss11, grad_x, g_wt, g_conv, g_w_o_dn, g_w_o_dil, g_w_out, small = _local_step(
        x[0], loss_target[0], norm_w, _split_proj_rows(wt_full), conv_full, a_log, dt_bias, dn_norm_w,
        w_o_dn_full, w_o_dil_full, w_out_full, final_norm_w)

    col_shards = lambda g, n: jnp.transpose(g.reshape(g.shape[0], N_DEV, n), (1, 0, 2))
    row_shards = lambda g: g.reshape(N_DEV, g.shape[0] // N_DEV, g.shape[1])
    sent = [row_shards(g_wt).astype(MXU), row_shards(g_w_o_dn).astype(MXU),
            col_shards(g_w_o_dil, w_o_dil.shape[2]).astype(MXU), row_shards(g_w_out).astype(MXU),
            col_shards(g_conv, conv_w.shape[2])]
    sent = [g8.reshape((N_CHIPS, 2) + g8.shape[1:]) for g8 in sent]
    from_sibling = _pair_exchange(sent, "scatter_pair")
    summed = [_pair_add(g, o, f"pair_add_{i}") for i, (g, o) in enumerate(zip(sent, from_sibling))]
    p_w_in, p_w_o_dn, p_w_o_dil, p_w_out, p_conv = _chip_exchange(summed, "scatter_chips")
    p_small = _all_gather([_pack_small(small["norm_w"], small["final_norm_w"], small["dn_norm_w"], small["a_log"],
                                       small["dt_bias"])], "gather_small_grads")[0]

    res = {}
    res["w_in"] = [jnp.transpose(t) for t in _adamw(p_w_in, wt, m_wt, v_wt, "adamw_w_in")]
    res["conv_w"] = _adamw(p_conv, conv_w[0], m_conv_w[0], v_conv_w[0], "adamw_conv_w")
    res["w_o_dn"] = _adamw(p_w_o_dn, w_o_dn[0], m_w_o_dn[0], v_w_o_dn[0], "adamw_w_o_dn")
    res["w_o_dil"] = _adamw(p_w_o_dil, w_o_dil[0], m_w_o_dil[0], v_w_o_dil[0], "adamw_w_o_dil")
    res["w_out"] = _adamw(p_w_out, w_out[0], m_w_out[0], v_w_out[0], "adamw_w_out")
    small_res = _adamw(p_small, _pack_small(norm_w, final_norm_w, dn_norm_w, a_log, dt_bias),
                       _pack_small(m_norm_w, m_final_norm_w, m_dn_norm_w, m_a_log, m_dt_bias),
                       _pack_small(v_norm_w, v_final_norm_w, v_dn_norm_w, v_a_log, v_dt_bias), "adamw_small")
    small_res = [_unpack_small(t) for t in small_res]

    loss = lax.psum(loss11[0, 0], ("x", "y", "c"))
    names = ["norm_w", "w_in", "conv_w", "a_log", "dt_bias", "dn_norm_w", "w_o_dn", "w_o_dil", "w_out", "final_norm_w"]
    outs = [loss, grad_x[None]]
    for kind in range(4):
        for nm in names:
            outs.append(res[nm][kind][None] if nm in res else small_res[kind][nm])
    return tuple(outs)
```

```python
import math

import jax
import jax.numpy as jnp
from jax import lax
from jax.experimental import pallas as pl
from jax.experimental.pallas import tpu as pltpu

F32 = jnp.float32
MXU = jnp.bfloat16
MESH = pl.DeviceIdType.MESH

N_DEV = 8
D_MODEL = 1024
DN_HEADS = 8
DN_DK = 128
DN_CHUNK = 64
N_DIL = 3
DIL_HEADS = 4
DIL_DH = 128
DIL_W = DIL_HEADS * DIL_DH
DIL_GROUPS = ((128, 1), (512, 4), (2048, 16))
ATT_BLOCK = 128
NORM_EPS = 1e-6
QKV_W = 3 * D_MODEL
DILQ_W = N_DIL * DIL_W
PROJ_SIZES = (1024, 1024, 1024, 1024, 8, 8, DILQ_W, DILQ_W, DILQ_W, DIL_W, D_MODEL, D_MODEL)

ADAM_LR = 0.001
ADAM_B1 = 0.9
ADAM_B2 = 0.999
ADAM_EPS = 1e-08
ADAM_WD = 0.01
ADAM_STEP = 10

ROW_TILE = 256
LANES = 128
SUBLANES = 8
VMEM_LIMIT = 48 << 20


def _pcall(body, **kw):
    return pl.pallas_call(body, **kw)


def _params(*sem):
    return pltpu.CompilerParams(dimension_semantics=tuple(sem), vmem_limit_bytes=VMEM_LIMIT)


def _sigmoid(x):
    return 1.0 / (1.0 + jnp.exp(-x))


def _softplus(x):
    return jnp.maximum(x, 0.0) + jnp.log(1.0 + jnp.exp(-jnp.abs(x)))


def _dot(a, b):
    return jnp.dot(a.astype(MXU), b.astype(MXU), preferred_element_type=F32)


def _dot_nt(a, b):
    return lax.dot_general(a.astype(MXU), b.astype(MXU), (((1,), (1,)), ((), ())), preferred_element_type=F32)


def _dot_tn(a, b):
    return lax.dot_general(a.astype(MXU), b.astype(MXU), (((0,), (0,)), ((), ())), preferred_element_type=F32)


def _split3(x):
    hi = x.astype(jnp.bfloat16)
    r1 = x - hi.astype(F32)
    mid = r1.astype(jnp.bfloat16)
    lo = (r1 - mid.astype(F32)).astype(jnp.bfloat16)
    return hi, mid, lo


def _dot01(m01, x):
    m = m01.astype(jnp.bfloat16)
    hi, mid, lo = _split3(x)
    f = lambda p: jnp.dot(m, p, preferred_element_type=F32)
    return f(hi) + (f(mid) + f(lo))


def _rows_call(body, name, n_rows, ins, outs, scratch=(), tm=ROW_TILE):
    steps = n_rows // tm
    per8 = tm // SUBLANES
    last8 = n_rows // SUBLANES - 1
    in_specs = []
    for arr, kind in ins:
        cols = arr.shape[-1]
        if kind == "tile":
            in_specs.append(pl.BlockSpec((tm, cols), lambda i: (i, 0)))
        elif kind == "full":
            in_specs.append(pl.BlockSpec(arr.shape, lambda i, nd=arr.ndim: (0,) * nd))
        elif kind == "prev8":
            in_specs.append(pl.BlockSpec((SUBLANES, cols), lambda i: (jnp.maximum(i * per8 - 1, 0), 0)))
        elif kind == "next8":
            in_specs.append(pl.BlockSpec((SUBLANES, cols), lambda i: (jnp.minimum((i + 1) * per8, last8), 0)))
        else:
            raise ValueError(kind)
    out_specs, out_shape, has_acc = [], [], False
    for shape, dtype, kind in outs:
        out_shape.append(jax.ShapeDtypeStruct(shape, dtype))
        if kind == "tile":
            out_specs.append(pl.BlockSpec((tm, shape[-1]), lambda i: (i, 0)))
        else:
            has_acc = True
            out_specs.append(pl.BlockSpec(shape, lambda i: (0, 0)))
    return _pcall(
        body, name=name, grid=(steps,), in_specs=in_specs, out_specs=out_specs, out_shape=out_shape,
        scratch_shapes=list(scratch),
        compiler_params=_params("arbitrary" if has_acc else "parallel"),
    )(*[a for a, _ in ins])


def _acc_add(ref, value):
    @pl.when(pl.program_id(0) == 0)
    def _():
        ref[...] = jnp.zeros_like(ref)
    ref[...] += value


def _col_chunks(n, width=512):
    return [(c, min(width, n - c)) for c in range(0, n, width)]


NT_DIMS = (((1,), (1,)), ((), ()))
TN_DIMS = (((0,), (0,)), ((), ()))


def _mm_out(a, ws, name, w_is_out_by_in=False, out_dtype=F32, tm=ROW_TILE):
    m, k = a.shape
    ns = [w.shape[0] if w_is_out_by_in else w.shape[1] for w in ws]

    def body(a_ref, *refs):
        av = a_ref[...]
        for w_ref, o_ref, n in zip(refs[:len(ws)], refs[len(ws):], ns):
            for c, wd in _col_chunks(n):
                if w_is_out_by_in:
                    part = lax.dot_general(av, w_ref[c:c + wd, :], NT_DIMS, preferred_element_type=F32)
                else:
                    part = jnp.dot(av, w_ref[:, c:c + wd], preferred_element_type=F32)
                o_ref[:, c:c + wd] = part.astype(o_ref.dtype)

    return _pcall(
        body, name=name, grid=(m // tm,),
        in_specs=[pl.BlockSpec((tm, k), lambda i: (i, 0))] + [pl.BlockSpec(w.shape, lambda i: (0, 0)) for w in ws],
        out_specs=[pl.BlockSpec((tm, n), lambda i: (i, 0)) for n in ns],
        out_shape=[jax.ShapeDtypeStruct((m, n), out_dtype) for n in ns],
        compiler_params=_params("parallel"),
    )(a, *ws)


def _rms_proj_fwd(x, norm_w, wts, name, tm=ROW_TILE):
    m, k = x.shape
    ns = [w.shape[0] for w in wts]

    def body(x_ref, nw_ref, *refs):
        w_refs, h_ref, o_refs = refs[:len(wts)], refs[len(wts)], refs[len(wts) + 1:]
        xv = x_ref[...]
        r = lax.rsqrt(jnp.mean(xv * xv, axis=-1, keepdims=True) + NORM_EPS)
        hv = (xv * r * nw_ref[...]).astype(h_ref.dtype)
        h_ref[...] = hv
        for w_ref, o_ref, n in zip(w_refs, o_refs, ns):
            for c, wd in _col_chunks(n):
                o_ref[:, c:c + wd] = lax.dot_general(hv, w_ref[c:c + wd, :], NT_DIMS, preferred_element_type=F32)

    return _pcall(
        body, name=name, grid=(m // tm,),
        in_specs=[pl.BlockSpec((tm, k), lambda i: (i, 0)), pl.BlockSpec(norm_w.shape, lambda i: (0, 0))]
        + [pl.BlockSpec(w.shape, lambda i: (0, 0)) for w in wts],
        out_specs=[pl.BlockSpec((tm, k), lambda i: (i, 0))] + [pl.BlockSpec((tm, n), lambda i: (i, 0)) for n in ns],
        out_shape=[jax.ShapeDtypeStruct((m, k), MXU)] + [jax.ShapeDtypeStruct((m, n), F32) for n in ns],
        compiler_params=_params("parallel"),
    )(x, norm_w, *wts)


def _mm_in(ds, ws, name, w_is_out_by_in=False, tm=ROW_TILE):
    m = ds[0].shape[0]
    k = ws[0].shape[1] if w_is_out_by_in else ws[0].shape[0]
    ns = [d.shape[1] for d in ds]

    def body(*refs):
        d_refs, w_refs, o_ref = refs[:len(ds)], refs[len(ds):2 * len(ds)], refs[-1]
        first = True
        for d_ref, w_ref, n in zip(d_refs, w_refs, ns):
            for c, wd in _col_chunks(n, 1024):
                if w_is_out_by_in:
                    part = jnp.dot(d_ref[:, c:c + wd], w_ref[c:c + wd, :], preferred_element_type=F32)
                else:
                    part = lax.dot_general(d_ref[:, c:c + wd], w_ref[:, c:c + wd], NT_DIMS, preferred_element_type=F32)
                if first:
                    o_ref[...] = part
                    first = False
                else:
                    o_ref[...] += part

    return _pcall(
        body, name=name, grid=(m // tm,),
        in_specs=[pl.BlockSpec((tm, n), lambda i: (i, 0)) for n in ns] + [pl.BlockSpec(w.shape, lambda i: (0, 0)) for w in ws],
        out_specs=pl.BlockSpec((tm, k), lambda i: (i, 0)),
        out_shape=jax.ShapeDtypeStruct((m, k), F32),
        compiler_params=_params("parallel"),
    )(*ds, *ws)


def _mm_tn(a, d, name):
    m, k = a.shape
    n = d.shape[1]
    tk = 512 if k % 512 == 0 else k

    def body(a_ref, d_ref, o_ref):
        o_ref[...] = lax.dot_general(a_ref[...], d_ref[...], TN_DIMS, preferred_element_type=F32)

    return _pcall(
        body, name=name, grid=(k // tk,),
        in_specs=[pl.BlockSpec((m, tk), lambda p: (0, p)), pl.BlockSpec((m, n), lambda p: (0, 0))],
        out_specs=pl.BlockSpec((tk, n), lambda p: (p, 0)),
        out_shape=jax.ShapeDtypeStruct((k, n), F32),
        compiler_params=_params("parallel"),
    )(a, d)


WGRAD_TILE = 512


def _proj_wgrad_all(dsegs, valid_rows, hb):
    m, k = hb.shape
    n_seg = len(dsegs)
    tiles, row = [], 0
    for si, (d, valid) in enumerate(zip(dsegs, valid_rows)):
        for c in range(0, valid, WGRAD_TILE):
            width = min(WGRAD_TILE, d.shape[1] - c)
            tiles.append((si, c, width, row + c, min(width, valid - c)))
        row += valid
    total_rows = row

    def body(*refs):
        d_refs, hb_ref, o_ref = refs[:n_seg], refs[n_seg], refs[n_seg + 1]
        a_buf, hb_buf, o_buf, load_sems, store_sems, hb_sem = refs[n_seg + 2:]

        def load(t):
            si, c, width, _, _ = tiles[t]
            return pltpu.make_async_copy(d_refs[si].at[:, pl.ds(c, width)], a_buf.at[t % 2, :, pl.ds(0, width)],
                                         load_sems.at[t % 2])

        def stores(t):
            _, _, _, orow, valid = tiles[t]
            return [pltpu.make_async_copy(o_buf.at[t % 2, pl.ds(0, valid), :], o_ref.at[pl.ds(orow, valid), :],
                                          store_sems.at[t % 2])]

        hb_copy = pltpu.make_async_copy(hb_ref, hb_buf, hb_sem)
        hb_copy.start()
        load(0).start()
        hb_copy.wait()
        for t in range(len(tiles)):
            width = tiles[t][2]
            load(t).wait()
            if t + 1 < len(tiles):
                load(t + 1).start()
            if t >= 2:
                for cp in stores(t - 2):
                    cp.wait()
            o_buf[t % 2, 0:width, :] = lax.dot_general(a_buf[t % 2, :, 0:width], hb_buf[...], TN_DIMS,
                                                        preferred_element_type=F32).astype(o_buf.dtype)
            for cp in stores(t):
                cp.start()
        for t in range(max(len(tiles) - 2, 0), len(tiles)):
            for cp in stores(t):
                cp.wait()

    any_spec = pl.BlockSpec(memory_space=pl.ANY)
    return _pcall(
        body, name="proj_wgrad",
        in_specs=[any_spec] * (n_seg + 1), out_specs=any_spec,
        out_shape=jax.ShapeDtypeStruct((total_rows, k), hb.dtype),
        scratch_shapes=[pltpu.VMEM((2, m, WGRAD_TILE), hb.dtype), pltpu.VMEM((m, k), hb.dtype),
                        pltpu.VMEM((2, WGRAD_TILE, k), hb.dtype), pltpu.SemaphoreType.DMA((2,)),
                        pltpu.SemaphoreType.DMA((2,)), pltpu.SemaphoreType.DMA],
        compiler_params=pltpu.CompilerParams(vmem_limit_bytes=VMEM_LIMIT),
    )(*dsegs, hb)


def _conv_taps(ext_ref, cw_ref, cols, tm):
    c = None
    for j in range(4):
        term = cw_ref[3 - j:4 - j, cols] * ext_ref[SUBLANES - j:SUBLANES - j + tm, cols]
        c = term if c is None else c + term
    return c


def _fill_ext(ext_ref, u_ref, halo_ref, first):
    ext_ref[0:SUBLANES, :] = jnp.where(first, 0.0, halo_ref[...])
    ext_ref[SUBLANES:, :] = u_ref[...]


def _dn_prep_fwd(qkv_pre, ba, conv_w8, alog_row, dtb_row):
    s = qkv_pre.shape[0]
    tm = ROW_TILE

    def body(u_ref, halo_ref, cw_ref, ba_ref, al_ref, dtb_ref, q_ref, k_ref, v_ref, bg_ref, ext_ref):
        _fill_ext(ext_ref, u_ref, halo_ref, pl.program_id(0) == 0)
        for h in range(3 * DN_HEADS):
            cols = slice(h * LANES, (h + 1) * LANES)
            c = _conv_taps(ext_ref, cw_ref, cols, tm)
            a = c * _sigmoid(c)
            oc = slice((h % DN_HEADS) * LANES, (h % DN_HEADS + 1) * LANES)
            if h < 2 * DN_HEADS:
                rinv = lax.rsqrt(jnp.sum(a * a, axis=-1, keepdims=True) + NORM_EPS)
                if h < DN_HEADS:
                    q_ref[:, oc] = a * (rinv * DN_DK ** -0.5)
                else:
                    k_ref[:, oc] = a * rinv
            else:
                v_ref[:, oc] = a
        bav = ba_ref[...]
        lane = lax.broadcasted_iota(jnp.int32, bav.shape, 1)
        beta = _sigmoid(bav)
        g = -jnp.exp(al_ref[...]) * _softplus(bav + dtb_ref[...])
        bg_ref[...] = jnp.where(lane < DN_HEADS, beta, jnp.where(lane < 2 * DN_HEADS, g, 0.0))

    return _rows_call(
        body, "dn_prep_fwd", s,
        [(qkv_pre, "tile"), (qkv_pre, "prev8"), (conv_w8, "full"), (ba, "tile"), (alog_row, "full"), (dtb_row, "full")],
        [((s, D_MODEL), F32, "tile")] * 3 + [((s, LANES), F32, "tile")],
        scratch=[pltpu.VMEM((tm + SUBLANES, QKV_W), F32)])


def _dn_prep_bwd(qkv_pre, ba, conv_w8, alog_row, dtb_row, dq, dk, dv, dbg):
    s = qkv_pre.shape[0]
    tm = ROW_TILE

    def body(u_ref, halo_ref, cw_ref, ba_ref, al_ref, dtb_ref, dq_ref, dk_ref, dv_ref, dbg_ref,
             dc_ref, dba_ref, dsmall_ref, ext_ref):
        _fill_ext(ext_ref, u_ref, halo_ref, pl.program_id(0) == 0)
        for h in range(3 * DN_HEADS):
            cols = slice(h * LANES, (h + 1) * LANES)
            oc = slice((h % DN_HEADS) * LANES, (h % DN_HEADS + 1) * LANES)
            c = _conv_taps(ext_ref, cw_ref, cols, tm)
            sg = _sigmoid(c)
            a = c * sg
            if h < 2 * DN_HEADS:
                rinv = lax.rsqrt(jnp.sum(a * a, axis=-1, keepdims=True) + NORM_EPS)
                dy = dq_ref[:, oc] * DN_DK ** -0.5 if h < DN_HEADS else dk_ref[:, oc]
                da = rinv * dy - a * (rinv * rinv * rinv) * jnp.sum(dy * a, axis=-1, keepdims=True)
            else:
                da = dv_ref[:, oc]
            dc_ref[:, cols] = da * (sg * (1.0 + c * (1.0 - sg)))
        bav = ba_ref[...]
        dbgv = dbg_ref[...]
        lane = lax.broadcasted_iota(jnp.int32, bav.shape, 1)
        beta = _sigmoid(bav)
        ea = jnp.exp(al_ref[...])
        z = bav + dtb_ref[...]
        g = -ea * _softplus(z)
        is_b = lane < DN_HEADS
        is_g = jnp.logical_and(lane >= DN_HEADS, lane < 2 * DN_HEADS)
        d_aa = jnp.where(is_g, dbgv * (-ea) * _sigmoid(z), 0.0)
        dba = jnp.where(is_b, dbgv * beta * (1.0 - beta), d_aa)
        dba_ref[...] = dba.astype(dba_ref.dtype)
        r_alog = jnp.sum(jnp.where(is_g, dbgv * g, 0.0), axis=0, keepdims=True)
        r_dtb = jnp.sum(d_aa, axis=0, keepdims=True)
        _acc_add(dsmall_ref, jnp.concatenate([r_alog, r_dtb, jnp.zeros((SUBLANES - 2, LANES), F32)], axis=0))

    return _rows_call(
        body, "dn_prep_bwd", s,
        [(qkv_pre, "tile"), (qkv_pre, "prev8"), (conv_w8, "full"), (ba, "tile"), (alog_row, "full"), (dtb_row, "full"),
         (dq, "tile"), (dk, "tile"), (dv, "tile"), (dbg, "tile")],
        [((s, QKV_W), F32, "tile"), ((s, LANES), MXU, "tile"), ((SUBLANES, LANES), F32, "acc")],
        scratch=[pltpu.VMEM((tm + SUBLANES, QKV_W), F32)])


def _conv_bwd(dc, qkv_pre, conv_w8):
    s = dc.shape[0]
    tm = ROW_TILE
    steps = s // tm

    def body(dc_ref, dnext_ref, u_ref, halo_ref, cw_ref, du_ref, dcw_ref, extd_ref, ext_ref):
        i = pl.program_id(0)
        _fill_ext(ext_ref, u_ref, halo_ref, i == 0)
        extd_ref[0:tm, :] = dc_ref[...]
        extd_ref[tm:, :] = jnp.where(i == steps - 1, 0.0, dnext_ref[...])

        @pl.when(i == 0)
        def _():
            dcw_ref[...] = jnp.zeros_like(dcw_ref)

        for h in range(3 * DN_HEADS):
            cols = slice(h * LANES, (h + 1) * LANES)
            du = None
            for j in range(4):
                term = cw_ref[3 - j:4 - j, cols] * extd_ref[j:j + tm, cols]
                du = term if du is None else du + term
            du_ref[:, cols] = du.astype(du_ref.dtype)
            dcv = dc_ref[:, cols]
            for j in range(4):
                row = jnp.sum(dcv * ext_ref[SUBLANES - j:SUBLANES - j + tm, cols], axis=0, keepdims=True)
                dcw_ref[3 - j:4 - j, cols] += row

    return _rows_call(
        body, "conv_bwd", s,
        [(dc, "tile"), (dc, "next8"), (qkv_pre, "tile"), (qkv_pre, "prev8"), (conv_w8, "full")],
        [((s, QKV_W), MXU, "tile"), ((SUBLANES, QKV_W), F32, "acc")],
        scratch=[pltpu.VMEM((tm + SUBLANES, QKV_W), F32), pltpu.VMEM((tm + SUBLANES, QKV_W), F32)])


def _dn_out_fwd(o, z, dnw_row, w_o_dn):
    def body(o_ref, z_ref, w_ref, wo_ref, on_ref, y_ref):
        for h in range(DN_HEADS):
            cols = slice(h * LANES, (h + 1) * LANES)
            ov = o_ref[:, cols]
            zv = z_ref[:, cols]
            ro = lax.rsqrt(jnp.mean(ov * ov, axis=-1, keepdims=True) + NORM_EPS)
            on_ref[:, cols] = (ov * ro * w_ref[...] * (zv * _sigmoid(zv))).astype(on_ref.dtype)
        y_ref[...] = jnp.dot(on_ref[...], wo_ref[...], preferred_element_type=F32)

    return _rows_call(body, "dn_out_fwd", o.shape[0], [(o, "tile"), (z, "tile"), (dnw_row, "full"), (w_o_dn, "full")],
                      [(o.shape, MXU, "tile"), ((o.shape[0], w_o_dn.shape[1]), F32, "tile")])


def _dn_out_bwd(dy, o, z, dnw_row, w_o_dn):
    def body(dy_ref, o_ref, z_ref, w_ref, wo_ref, do_ref, dz_ref, dw_ref, d_ref):
        d_ref[...] = lax.dot_general(dy_ref[...], wo_ref[...], NT_DIMS, preferred_element_type=F32)
        acc = jnp.zeros((1, LANES), F32)
        for h in range(DN_HEADS):
            cols = slice(h * LANES, (h + 1) * LANES)
            dv, ov, zv = d_ref[:, cols], o_ref[:, cols], z_ref[:, cols]
            sg = _sigmoid(zv)
            sz = zv * sg
            ro = lax.rsqrt(jnp.mean(ov * ov, axis=-1, keepdims=True) + NORM_EPS)
            nv = ov * ro
            dn = dv * w_ref[...] * sz
            acc = acc + jnp.sum(dv * nv * sz, axis=0, keepdims=True)
            dz_ref[:, cols] = (dv * nv * w_ref[...] * (sg * (1.0 + zv * (1.0 - sg)))).astype(dz_ref.dtype)
            do_ref[:, cols] = ro * dn - ov * (ro * ro * ro) * jnp.mean(dn * ov, axis=-1, keepdims=True)
        _acc_add(dw_ref, jnp.concatenate([acc, jnp.zeros((SUBLANES - 1, LANES), F32)], axis=0))

    return _rows_call(body, "dn_out_bwd", o.shape[0],
                      [(dy, "tile"), (o, "tile"), (z, "tile"), (dnw_row, "full"), (w_o_dn, "full")],
                      [(o.shape, F32, "tile"), (o.shape, MXU, "tile"), ((SUBLANES, LANES), F32, "acc")],
                      scratch=[pltpu.VMEM((ROW_TILE, o.shape[1]), F32)])


def _attn_out_fwd(parts, lses, zb, w_o_dil):
    def body(o0, o1, o2, l0, l1, l2, z_ref, wo_ref, lse_ref, o_ref, g_ref, y_ref):
        a, b, c = l0[...], l1[...], l2[...]
        m = jnp.maximum(a, jnp.maximum(b, c))
        ea, eb, ec = jnp.exp(a - m), jnp.exp(b - m), jnp.exp(c - m)
        den = ea + eb + ec
        out = (ea * o0[...] + eb * o1[...] + ec * o2[...]) / den
        lse_ref[...] = m + jnp.log(den)
        o_ref[...] = out
        zv = z_ref[...]
        gated = (out * (zv * _sigmoid(zv))).astype(g_ref.dtype)
        g_ref[...] = gated
        y_ref[...] = jnp.dot(gated, wo_ref[...], preferred_element_type=F32)

    s = zb.shape[0]
    return _rows_call(body, "attn_out_fwd", s,
                      [(p, "tile") for p in parts] + [(l, "tile") for l in lses] + [(zb, "tile"), (w_o_dil, "full")],
                      [((s, DIL_W), F32, "tile"), ((s, DIL_W), F32, "tile"), ((s, DIL_W), MXU, "tile"),
                       ((s, w_o_dil.shape[1]), F32, "tile")])


def _attn_out_bwd(dy, o_joint, zb, w_o_dil):
    def body(dy_ref, o_ref, z_ref, wo_ref, do_ref, dz_ref, dl_ref):
        zv = z_ref[...]
        sg = _sigmoid(zv)
        dv = lax.dot_general(dy_ref[...], wo_ref[...], NT_DIMS, preferred_element_type=F32)
        ov = o_ref[...]
        do = dv * (zv * sg)
        do_ref[...] = do
        dz_ref[...] = (dv * ov * (sg * (1.0 + zv * (1.0 - sg)))).astype(dz_ref.dtype)
        for h in range(DIL_HEADS):
            cols = slice(h * LANES, (h + 1) * LANES)
            dl_ref[:, cols] = jnp.broadcast_to(jnp.sum(do[:, cols] * ov[:, cols], axis=-1, keepdims=True),
                                               (do.shape[0], LANES))

    s = zb.shape[0]
    return _rows_call(body, "attn_out_bwd", s, [(dy, "tile"), (o_joint, "tile"), (zb, "tile"), (w_o_dil, "full")],
                      [((s, DIL_W), F32, "tile"), ((s, DIL_W), MXU, "tile"), ((s, DIL_W), F32, "tile")])


def _merge_out_final(ga, gb, ya, yb, x, target, w_out, wf_row):
    s, dm = x.shape

    def body(ga_ref, gb_ref, ya_ref, yb_ref, x_ref, t_ref, wo_ref, w_ref,
             loss_ref, dw_ref, m_ref, dxb_ref, dx_ref, dya_ref, dyb_ref, dga_ref, dgb_ref):
        sa, sb = _sigmoid(ga_ref[...]), _sigmoid(gb_ref[...])
        ya, yb = ya_ref[...], yb_ref[...]
        merged = (sa * ya + sb * yb).astype(MXU)
        m_ref[...] = merged
        x2 = x_ref[...] + jnp.dot(merged, wo_ref[...], preferred_element_type=F32)
        r = lax.rsqrt(jnp.mean(x2 * x2, axis=-1, keepdims=True) + NORM_EPS)
        w = w_ref[...]
        err = x2 * r * w - t_ref[...]
        tile_loss = 0.5 * jnp.sum(jnp.mean(err * err, axis=-1, keepdims=True), axis=0, keepdims=True)
        _acc_add(loss_ref, jnp.broadcast_to(tile_loss, (SUBLANES, LANES)))
        dy = err * (1.0 / dm)
        row = jnp.sum(dy * x2 * r, axis=0, keepdims=True)
        _acc_add(dw_ref, jnp.concatenate([row, jnp.zeros((SUBLANES - 1, dm), F32)], axis=0))
        dn = dy * w
        dx2 = r * dn - x2 * (r * r * r) * jnp.mean(dn * x2, axis=-1, keepdims=True)
        dx_ref[...] = dx2
        dxb = dx2.astype(MXU)
        dxb_ref[...] = dxb
        dmv = lax.dot_general(dxb, wo_ref[...], NT_DIMS, preferred_element_type=F32)
        dya_ref[...] = (dmv * sa).astype(dya_ref.dtype)
        dyb_ref[...] = (dmv * sb).astype(dyb_ref.dtype)
        dga_ref[...] = (dmv * ya * sa * (1.0 - sa)).astype(dga_ref.dtype)
        dgb_ref[...] = (dmv * yb * sb * (1.0 - sb)).astype(dgb_ref.dtype)

    return _rows_call(body, "merge_out_final", s,
                      [(ga, "tile"), (gb, "tile"), (ya, "tile"), (yb, "tile"), (x, "tile"), (target, "tile"),
                       (w_out, "full"), (wf_row, "full")],
                      [((SUBLANES, LANES), F32, "acc"), ((SUBLANES, dm), F32, "acc"), ((s, dm), MXU, "tile"),
                       ((s, dm), MXU, "tile"), ((s, dm), F32, "tile")] + [((s, dm), MXU, "tile")] * 4)


def _lane_pick(x, idx):
    lane = lax.broadcasted_iota(jnp.int32, x.shape, 1)
    return jnp.sum(jnp.where(lane == idx, x, 0.0), axis=-1, keepdims=True)


PAIR = 2 * DN_CHUNK
SCAN_CHUNKS = 4


def _bmm(a, b):
    return lax.dot_general(a.astype(MXU), b.astype(MXU), (((2,), (1,)), ((0,), (0,))), preferred_element_type=F32)


def _bmm_nt(a, b):
    return lax.dot_general(a.astype(MXU), b.astype(MXU), (((2,), (2,)), ((0,), (0,))), preferred_element_type=F32)


def _bmm_tn(a, b):
    return lax.dot_general(a.astype(MXU), b.astype(MXU), (((1,), (1,)), ((0,), (0,))), preferred_element_type=F32)


def _bmm3(a, b):
    ah = a.astype(jnp.bfloat16)
    al = (a - ah.astype(F32)).astype(jnp.bfloat16)
    bh = b.astype(jnp.bfloat16)
    bl = (b - bh.astype(F32)).astype(jnp.bfloat16)
    f = lambda p, q: lax.dot_general(p, q, (((2,), (1,)), ((0,), (0,))), preferred_element_type=F32)
    return f(ah, bh) + (f(ah, bl) + f(al, bh))


def _pair_masks():
    row = lax.broadcasted_iota(jnp.int32, (PAIR, PAIR), 0)
    col = lax.broadcasted_iota(jnp.int32, (PAIR, PAIR), 1)
    same = (row >= DN_CHUNK) == (col >= DN_CHUNK)
    return dict(causal=same & (row >= col), strict=same & (row > col), upper=same & (row <= col), eye=row == col,
                first=row < DN_CHUNK, row=row, lane=col)


def _pair_decay(bgv, masks):
    gc_all = _dot01(masks["causal"].astype(F32), bgv)
    out = []
    for h in range(DN_HEADS):
        beta = _lane_pick(bgv, h)
        gcb = jnp.broadcast_to(_lane_pick(gc_all, DN_HEADS + h), (PAIR, PAIR))
        gam = jnp.where(masks["causal"], jnp.exp(jnp.minimum(gcb - gcb.T, 0.0)), 0.0)
        gl = jnp.where(masks["first"], gcb[DN_CHUNK - 1:DN_CHUNK, :], gcb[PAIR - 1:PAIR, :])
        out.append((beta, gcb, gam, gl))
    return out


def _pair_inverse(a_strict, eye):
    eye_f = eye.astype(F32)[None]
    m = eye_f + a_strict
    x = eye_f - a_strict
    steps = int(math.log2(DN_CHUNK)) - 1
    for i in range(steps):
        mm = _bmm3 if i == steps - 1 else _bmm
        x = x + mm(x, eye_f - mm(m, x))
    return x


def _head_cols(h):
    return slice(h * LANES, (h + 1) * LANES)


def _delta_prep(q, k, v, bg):
    s = q.shape[0]
    c = DN_CHUNK
    n_chunks = s // c

    def body(q_ref, k_ref, v_ref, bg_ref, u_ref, w_ref, qd_ref, kd_ref, aqk_ref, dl_ref, t2_ref):
        masks = _pair_masks()
        dec = _pair_decay(bg_ref[...], masks)
        kbs, ks, gams, vbs, kbes, qs, qds, kds, dls = ([] for _ in range(9))
        for h in range(DN_HEADS):
            beta, gcb, gam, gl = dec[h]
            qh, kh, vh = q_ref[:, _head_cols(h)], k_ref[:, _head_cols(h)], v_ref[:, _head_cols(h)]
            eg = jnp.exp(gcb)
            kb = kh * beta
            kbs.append(kb); ks.append(kh); gams.append(gam); vbs.append(vh * beta); kbes.append(kb * eg)
            qs.append(qh); qds.append(qh * eg); kds.append(kh * jnp.exp(gl - gcb)); dls.append(jnp.exp(gl))
        st = lambda xs: jnp.stack(xs, axis=0)
        kmat, gam = st(ks), st(gams)
        a = jnp.where(masks["strict"][None], _bmm_nt(st(kbs), kmat) * gam, 0.0)
        t = _pair_inverse(a, masks["eye"])
        u = _bmm(t, st(vbs))
        w = _bmm(t, st(kbes))
        aqk = _bmm_nt(st(qs), kmat) * gam
        t2_ref[0] = t.astype(t2_ref.dtype)
        for half in range(2):
            rows = slice(half * c, (half + 1) * c)
            u_ref[half] = u[:, rows, :]
            w_ref[half] = w[:, rows, :].astype(w_ref.dtype)
            qd_ref[half] = st(qds)[:, rows, :].astype(qd_ref.dtype)
            kd_ref[half] = st(kds)[:, rows, :].astype(kd_ref.dtype)
            aqk_ref[half] = aqk[:, rows, rows].astype(aqk_ref.dtype)
            dl_ref[half] = st(dls)[:, half * c:half * c + SUBLANES, :]

    row_spec = lambda w_: pl.BlockSpec((PAIR, w_), lambda i: (i, 0))
    hm = lambda a_, b_: pl.BlockSpec((2, DN_HEADS, a_, b_), lambda i: (i, 0, 0, 0))
    hm_shape = lambda a_, b_, dt: jax.ShapeDtypeStruct((n_chunks, DN_HEADS, a_, b_), dt)
    return _pcall(
        body, name="delta_prep", grid=(n_chunks // 2,),
        in_specs=[row_spec(D_MODEL)] * 3 + [row_spec(LANES)],
        out_specs=[hm(c, LANES)] * 4 + [hm(c, c), hm(SUBLANES, LANES),
                   pl.BlockSpec((1, DN_HEADS, PAIR, PAIR), lambda i: (i, 0, 0, 0))],
        out_shape=[hm_shape(c, LANES, F32), hm_shape(c, LANES, MXU), hm_shape(c, LANES, MXU), hm_shape(c, LANES, MXU),
                   hm_shape(c, c, MXU), hm_shape(SUBLANES, LANES, F32),
                   jax.ShapeDtypeStruct((n_chunks // 2, DN_HEADS, PAIR, PAIR), MXU)],
        compiler_params=_params("parallel"),
    )(q, k, v, bg)


def _delta_scan_fwd(u, w, qd, kd, aqk, dl):
    n_chunks = u.shape[0]
    c = DN_CHUNK
    g_n = SCAN_CHUNKS

    def body(u_ref, w_ref, qd_ref, kd_ref, aqk_ref, dl_ref, o_ref, vnew_ref, st_ref, state):
        @pl.when(pl.program_id(0) == 0)
        def _():
            state[...] = jnp.zeros_like(state)

        for g in range(g_n):
            sv = state[...]
            sb = sv.astype(MXU)
            vnew = u_ref[g] - _bmm(w_ref[g], sb)
            o = _bmm(qd_ref[g], sb) + _bmm(aqk_ref[g], vnew)
            state[...] = sv * dl_ref[g][:, 0:1, :] + _bmm_tn(kd_ref[g], vnew)
            vnew_ref[g] = vnew.astype(vnew_ref.dtype)
            st_ref[g] = sb
            for h in range(DN_HEADS):
                o_ref[g * c:(g + 1) * c, _head_cols(h)] = o[h]

    hm = lambda a_, b_: pl.BlockSpec((g_n, DN_HEADS, a_, b_), lambda i: (i, 0, 0, 0))
    return _pcall(
        body, name="delta_scan_fwd", grid=(n_chunks // g_n,),
        in_specs=[hm(c, LANES)] * 4 + [hm(c, c), hm(SUBLANES, LANES)],
        out_specs=[pl.BlockSpec((g_n * c, D_MODEL), lambda i: (i, 0)), hm(c, LANES), hm(DN_DK, DN_DK)],
        out_shape=[jax.ShapeDtypeStruct((n_chunks * c, D_MODEL), F32),
                   jax.ShapeDtypeStruct((n_chunks, DN_HEADS, c, LANES), MXU),
                   jax.ShapeDtypeStruct((n_chunks, DN_HEADS, DN_DK, DN_DK), MXU)],
        scratch_shapes=[pltpu.VMEM((DN_HEADS, DN_DK, DN_DK), F32)],
        compiler_params=_params("arbitrary"),
    )(u, w, qd, kd, aqk, dl)


def _delta_scan_bwd(w, qd, kd, aqk, dl, vnew, st, do):
    n_chunks = w.shape[0]
    c = DN_CHUNK
    g_n = SCAN_CHUNKS
    steps = n_chunks // g_n

    def body(w_ref, qd_ref, kd_ref, aqk_ref, dl_ref, vnew_ref, st_ref, do_ref, dvnew_ref, dkd_ref, ddl_ref, dstate):
        @pl.when(pl.program_id(0) == 0)
        def _():
            dstate[...] = jnp.zeros_like(dstate)

        for g in reversed(range(g_n)):
            ds = dstate[...]
            dsb = ds.astype(MXU)
            doh = jnp.stack([do_ref[g * c:(g + 1) * c, _head_cols(h)] for h in range(DN_HEADS)], axis=0)
            dvnew = _bmm_tn(aqk_ref[g], doh) + _bmm(kd_ref[g], dsb)
            dkd_ref[g] = _bmm_nt(vnew_ref[g], dsb)
            ddl = jnp.sum(jnp.sum(st_ref[g].astype(F32) * ds, axis=2, keepdims=True), axis=1, keepdims=True)
            ddl_ref[g] = jnp.broadcast_to(ddl, (DN_HEADS, SUBLANES, LANES))
            dstate[...] = ds * dl_ref[g][:, 0:1, :] + _bmm_tn(qd_ref[g], doh) - _bmm_tn(w_ref[g], dvnew)
            dvnew_ref[g] = dvnew.astype(dvnew_ref.dtype)

    rev = lambda i: steps - 1 - i
    hm = lambda a_, b_: pl.BlockSpec((g_n, DN_HEADS, a_, b_), lambda i: (rev(i), 0, 0, 0))
    return _pcall(
        body, name="delta_scan_bwd", grid=(steps,),
        in_specs=[hm(c, LANES)] * 3 + [hm(c, c), hm(SUBLANES, LANES), hm(c, LANES), hm(DN_DK, DN_DK),
                  pl.BlockSpec((g_n * c, D_MODEL), lambda i: (rev(i), 0))],
        out_specs=[hm(c, LANES), hm(c, LANES), hm(SUBLANES, LANES)],
        out_shape=[jax.ShapeDtypeStruct((n_chunks, DN_HEADS, c, LANES), MXU),
                   jax.ShapeDtypeStruct((n_chunks, DN_HEADS, c, LANES), F32),
                   jax.ShapeDtypeStruct((n_chunks, DN_HEADS, SUBLANES, LANES), F32)],
        scratch_shapes=[pltpu.VMEM((DN_HEADS, DN_DK, DN_DK), F32)],
        compiler_params=_params("arbitrary"),
    )(w, qd, kd, aqk, dl, vnew, st, do)


def _delta_post_bwd(q, k, v, bg, t2, st, vnew, do, dvnew, dkd, ddl):
    s = q.shape[0]
    c = DN_CHUNK

    def body(q_ref, k_ref, v_ref, bg_ref, t2_ref, st_ref, vnew_ref, do_ref, dvnew_ref, dkd_ref, ddl_ref,
             dq_ref, dk_ref, dv_ref, dbg_ref):
        masks = _pair_masks()
        first = masks["first"][None]
        dec = _pair_decay(bg_ref[...], masks)
        st_ = lambda xs: jnp.stack(xs, axis=0)
        heads = range(DN_HEADS)
        qm_, km_, vm_, dom = (st_([r[:, _head_cols(h)] for h in heads]) for r in (q_ref, k_ref, v_ref, do_ref))
        beta = st_([dec[h][0] for h in heads])
        gcb = st_([dec[h][1] for h in heads])
        gam = st_([dec[h][2] for h in heads])
        gl = st_([dec[h][3] for h in heads])
        pair = lambda ref: jnp.concatenate([ref[0], ref[1]], axis=1)
        vnew2, dvnew2, dkd2 = pair(vnew_ref), pair(dvnew_ref), pair(dkd_ref)
        halves = lambda x: (x[:, :c, :], x[:, c:, :])
        by_state = lambda x: jnp.concatenate([_bmm_nt(xh, st_ref[i]) for i, xh in enumerate(halves(x))], axis=1)
        dqd = by_state(dom)
        dw = -by_state(dvnew2)
        ddl2 = jnp.where(first, ddl_ref[0][:, 0:1, :], ddl_ref[1][:, 0:1, :])

        eg = jnp.exp(gcb)
        egl = jnp.exp(gl - gcb)
        dl = jnp.exp(gl)
        kb = km_ * beta
        kk = _bmm_nt(kb, km_)
        a = jnp.where(masks["strict"][None], kk * gam, 0.0)
        t = t2_ref[0]
        vb = vm_ * beta
        kbe = kb * eg
        u = _bmm(t, vb)
        w = _bmm(t, kbe)
        aqk = _bmm_nt(qm_, km_) * gam
        qd = qm_ * eg
        kd = km_ * egl

        daqk = jnp.where(masks["causal"][None], _bmm_nt(dom, vnew2), 0.0)
        dvb = _bmm_tn(t, dvnew2)
        dkbe = _bmm_tn(t, dw)
        da = jnp.where(masks["strict"][None], -(_bmm_nt(dvb, u) + _bmm_nt(dkbe, w)), 0.0)
        pm = da * gam
        qmm = daqk * gam
        dkb = _bmm(pm, km_) + dkbe * eg
        dkh = _bmm_tn(pm, kb) + _bmm_tn(qmm, qm_) + dkd2 * egl + dkb * beta
        dqh = _bmm(qmm, km_) + dqd * eg
        xm = da * a + daqk * aqk
        ones = jnp.ones((DN_HEADS, PAIR, LANES), F32)
        hi, mid, lo = _split3(xm)
        colsum = _bmm_tn(hi, ones) + (_bmm_tn(mid, ones) + _bmm_tn(lo, ones))
        tmp = jnp.sum(dkd2 * kd, axis=-1, keepdims=True)
        dgc = (jnp.sum(xm, axis=-1, keepdims=True) - colsum + jnp.sum(dkbe * kbe, axis=-1, keepdims=True)
               + jnp.sum(dqd * qd, axis=-1, keepdims=True) - tmp)
        sum0 = jnp.sum(jnp.where(first, tmp, 0.0), axis=1, keepdims=True)
        sum1 = jnp.sum(jnp.where(first, 0.0, tmp), axis=1, keepdims=True)
        dgl = jnp.where(first, sum0, sum1) + ddl2 * dl
        last = (masks["row"] == c - 1) | (masks["row"] == PAIR - 1)
        dgc = dgc + jnp.where(last[None], dgl, 0.0)
        dbeta = jnp.sum(dvb * vm_, axis=-1, keepdims=True) + jnp.sum(dkb * km_, axis=-1, keepdims=True)
        dvh = dvb * beta

        lane = masks["lane"]
        dgc_lanes = jnp.zeros((PAIR, LANES), F32)
        dbg = jnp.zeros((PAIR, LANES), F32)
        for h in heads:
            dq_ref[:, _head_cols(h)] = dqh[h]
            dk_ref[:, _head_cols(h)] = dkh[h]
            dv_ref[:, _head_cols(h)] = dvh[h]
            dgc_lanes = dgc_lanes + jnp.where(lane == DN_HEADS + h, dgc[h], 0.0)
            dbg = dbg + jnp.where(lane == h, dbeta[h], 0.0)
        dbg_ref[...] = dbg + _dot01(masks["upper"].astype(F32), dgc_lanes)

    n_pairs = s // PAIR
    row_spec = lambda w_: pl.BlockSpec((PAIR, w_), lambda i: (i, 0))
    hm = lambda a_, b_: pl.BlockSpec((2, DN_HEADS, a_, b_), lambda i: (i, 0, 0, 0))
    return _pcall(
        body, name="delta_post_bwd", grid=(n_pairs,),
        in_specs=[row_spec(D_MODEL)] * 3 + [row_spec(LANES), pl.BlockSpec((1, DN_HEADS, PAIR, PAIR), lambda i: (i, 0, 0, 0)),
                  hm(DN_DK, DN_DK), hm(c, LANES), row_spec(D_MODEL), hm(c, LANES), hm(c, LANES), hm(SUBLANES, LANES)],
        out_specs=[row_spec(D_MODEL)] * 3 + [row_spec(LANES)],
        out_shape=[jax.ShapeDtypeStruct((s, D_MODEL), F32)] * 3 + [jax.ShapeDtypeStruct((s, LANES), F32)],
        compiler_params=_params("parallel"),
    )(q, k, v, bg, t2, st, vnew, do, dvnew, dkd, ddl)


def _alibi_slope(group, head):
    n = N_DIL * DIL_HEADS
    return float(2.0 ** (-8.0 * (group * DIL_HEADS + head + 1) / n))


def _attn_plan(s, group):
    window, dil = DIL_GROUPS[group]
    assert window // dil == ATT_BLOCK
    assert (s // dil) % ATT_BLOCK == 0, "sub-sequence length must be a whole number of attention blocks"
    return dil, s // dil // ATT_BLOCK, (DIL_HEADS if dil == 1 else 1)


def _attn_specs(group, dil, nb, hp):
    rows = ATT_BLOCK * dil

    def spec(col0, shift):
        if shift < 0:
            f = lambda hb, n: (jnp.maximum(n - 1, 0), col0 + hb)
        elif shift > 0:
            f = lambda hb, n: (jnp.minimum(n + 1, nb - 1), col0 + hb)
        else:
            f = lambda hb, n: (jnp.minimum(n, nb - 1), col0 + hb)
        return pl.BlockSpec((rows, hp * LANES), f)

    return (lambda shift: spec(group * (DIL_HEADS // hp), shift)), (lambda shift: spec(0, shift))


def _sub_rows(ref, r, dil, cols):
    return ref[:, cols] if dil == 1 else ref[pl.ds(r, ATT_BLOCK, stride=dil), cols]


def _set_sub_rows(ref, r, dil, cols, value):
    if dil == 1:
        ref[:, cols] = value
    else:
        ref[pl.ds(r, ATT_BLOCK, stride=dil), cols] = value


def _step_slope(group, hp, hh):
    if hp == DIL_HEADS:
        return _alibi_slope(group, hh)
    hb = pl.program_id(0)
    slope = _alibi_slope(group, DIL_HEADS - 1)
    for h in reversed(range(DIL_HEADS - 1)):
        slope = jnp.where(hb == h, _alibi_slope(group, h), slope)
    return slope


def _window_bias(dil, n):
    a = lax.broadcasted_iota(jnp.int32, (ATT_BLOCK, 2 * ATT_BLOCK), 0)
    b = lax.broadcasted_iota(jnp.int32, (ATT_BLOCK, 2 * ATT_BLOCK), 1)
    dist = ATT_BLOCK + a - b
    valid = (dist >= 0) & (dist <= ATT_BLOCK) & ((b >= ATT_BLOCK) | (n > 0))
    return (dist * dil).astype(F32), valid


def _attn_fwd(qb, kb, vb, group):
    s = qb.shape[0]
    dil, nb, hp = _attn_plan(s, group)
    qkv, per_head = _attn_specs(group, dil, nb, hp)

    def body(q_ref, kp_ref, kc_ref, vp_ref, vc_ref, o_ref, lse_ref):
        n = pl.program_id(1)
        distd, valid = _window_bias(dil, n)
        for hh in range(hp):
            cols = _head_cols(hh)
            slope = _step_slope(group, hp, hh)
            for r in range(dil):
                sub = lambda ref: _sub_rows(ref, r, dil, cols).astype(MXU)
                kk = jnp.concatenate([sub(kp_ref), sub(kc_ref)], axis=0)
                vv = jnp.concatenate([sub(vp_ref), sub(vc_ref)], axis=0)
                sc = _dot_nt(sub(q_ref), kk) * DIL_DH ** -0.5 - slope * distd
                sc = jnp.where(valid, sc, -1e30)
                mx = jnp.max(sc, axis=-1, keepdims=True)
                p = jnp.where(valid, jnp.exp(sc - mx), 0.0)
                den = jnp.sum(p, axis=-1, keepdims=True)
                _set_sub_rows(o_ref, r, dil, cols, _dot(p, vv) / den)
                _set_sub_rows(lse_ref, r, dil, cols, jnp.broadcast_to(mx + jnp.log(den), (ATT_BLOCK, LANES)))

    return _pcall(
        body, name=f"attn_fwd_g{group}", grid=(DIL_HEADS // hp, nb),
        in_specs=[qkv(0), qkv(-1), qkv(0), qkv(-1), qkv(0)], out_specs=[per_head(0)] * 2,
        out_shape=[jax.ShapeDtypeStruct((s, DIL_W), F32)] * 2,
        compiler_params=_params("parallel", "parallel"),
    )(qb, kb, kb, vb, vb)


def _attn_bwd(qb, kb, vb, d_o, lse, delta, group):
    s = qb.shape[0]
    dil, nb, hp = _attn_plan(s, group)
    qkv, per_head = _attn_specs(group, dil, nb, hp)
    scale = DIL_DH ** -0.5

    def body(q_ref, kp_ref, kc_ref, vp_ref, vc_ref, do_ref, l_ref, dl_ref, dq_ref, dk_ref, dv_ref,
             dq_acc, dk_done, dv_done, dk_carry, dv_carry):
        n = pl.program_id(1)
        slopes = [_step_slope(group, hp, hh) for hh in range(hp)]

        @pl.when(n == 0)
        def _():
            dk_carry[...] = jnp.zeros_like(dk_carry)
            dv_carry[...] = jnp.zeros_like(dv_carry)

        @pl.when(n < nb)
        def _():
            distd, valid = _window_bias(dil, n)
            for hh in range(hp):
                cols = _head_cols(hh)
                slope = slopes[hh]
                for r in range(dil):
                    sub = lambda ref: _sub_rows(ref, r, dil, cols)
                    qc, do = sub(q_ref).astype(MXU), sub(do_ref).astype(MXU)
                    kk = jnp.concatenate([sub(kp_ref).astype(MXU), sub(kc_ref).astype(MXU)], axis=0)
                    vv = jnp.concatenate([sub(vp_ref).astype(MXU), sub(vc_ref).astype(MXU)], axis=0)
                    sc = _dot_nt(qc, kk) * scale - slope * distd
                    p = jnp.where(valid, jnp.exp(jnp.minimum(sc - jnp.concatenate([sub(l_ref)] * 2, axis=1), 0.0)), 0.0)
                    dsc = p * (_dot_nt(do, vv) - jnp.concatenate([sub(dl_ref)] * 2, axis=1))
                    _set_sub_rows(dq_acc, r, dil, cols, _dot(dsc, kk) * scale)
                    dkk = _dot_tn(dsc, qc) * scale
                    dvv = _dot_tn(p, do)
                    _set_sub_rows(dk_done, r, dil, cols, _sub_rows(dk_carry, r, dil, cols) + dkk[:ATT_BLOCK])
                    _set_sub_rows(dv_done, r, dil, cols, _sub_rows(dv_carry, r, dil, cols) + dvv[:ATT_BLOCK])
                    _set_sub_rows(dk_carry, r, dil, cols, dkk[ATT_BLOCK:])
                    _set_sub_rows(dv_carry, r, dil, cols, dvv[ATT_BLOCK:])
            dq_ref[...] = dq_acc[...].astype(dq_ref.dtype)
            dk_ref[...] = dk_done[...].astype(dk_ref.dtype)
            dv_ref[...] = dv_done[...].astype(dv_ref.dtype)

        @pl.when(n == nb)
        def _():
            dk_ref[...] = dk_carry[...].astype(dk_ref.dtype)
            dv_ref[...] = dv_carry[...].astype(dv_ref.dtype)

    return _pcall(
        body, name=f"attn_bwd_g{group}", grid=(DIL_HEADS // hp, nb + 1),
        in_specs=[qkv(0), qkv(-1), qkv(0), qkv(-1), qkv(0)] + [per_head(0)] * 3,
        out_specs=[per_head(0), per_head(-1), per_head(-1)],
        out_shape=[jax.ShapeDtypeStruct((s, DIL_W), MXU)] * 3,
        scratch_shapes=[pltpu.VMEM((ATT_BLOCK * dil, hp * LANES), F32)] * 5,
        compiler_params=_params("parallel", "arbitrary"),
    )(qb, kb, kb, vb, vb, d_o, lse, delta)


def _my_place():
    mx, my, mc = lax.axis_index("x"), lax.axis_index("y"), lax.axis_index("c")
    return mx, my, mc, 4 * mx + 2 * my + mc


N_CHIPS = 4


def _shard_row_tile(r):
    if r <= 512:
        return r
    return 128 if r % 128 == 0 else 480


def _other_chips(mx, my):
    return [(1 - mx, my), (mx, 1 - my), (1 - mx, 1 - my)]


def _all_gather(xs, name):
    n = len(xs)

    def body(*refs):
        x_refs, o_refs = refs[:n], refs[n:2 * n]
        send_sems, recv_sems, local_sems = refs[2 * n:]
        mx, my, mc, me = _my_place()
        sibling, sibling_id = (mx, my, 1 - mc), 4 * mx + 2 * my + (1 - mc)
        chips = _other_chips(mx, my)

        def copy(a, k, slot, to, src=None):
            dst = o_refs[a].at[slot]
            return pltpu.make_async_remote_copy(
                src_ref=dst if src is None else src, dst_ref=dst, send_sem=send_sems.at[a, k],
                recv_sem=recv_sems.at[a, k], device_id=to, device_id_type=MESH)

        local = [pltpu.make_async_copy(x_refs[a], o_refs[a].at[me], local_sems.at[a]) for a in range(n)]
        for cp in local:
            cp.start()
        sends = []
        for a in range(n):
            sends.append(copy(a, 0, me, sibling, src=x_refs[a]))
            sends += [copy(a, 1 + j, me, (px, py, mc), src=x_refs[a]) for j, (px, py) in enumerate(chips)]
        for cp in sends:
            cp.start()
        for j, (px, py) in enumerate(chips):
            slot = 4 * px + 2 * py + mc
            for a in range(n):
                copy(a, 1 + j, slot, (px, py, mc)).wait_recv()
                passed = copy(a, 4 + j, slot, sibling)
                passed.start()
                sends.append(passed)
        for a in range(n):
            copy(a, 0, sibling_id, sibling).wait_recv()
            for j, (px, py) in enumerate(chips):
                copy(a, 4 + j, 4 * px + 2 * py + (1 - mc), sibling).wait_recv()
        for cp in sends:
            cp.wait_send()
        for cp in local:
            cp.wait()

    any_spec = pl.BlockSpec(memory_space=pl.ANY)
    return _pcall(
        body, name=name,
        in_specs=[any_spec] * n, out_specs=[any_spec] * n,
        out_shape=[jax.ShapeDtypeStruct((N_DEV,) + x.shape, x.dtype) for x in xs],
        scratch_shapes=[pltpu.SemaphoreType.DMA((n, N_DEV - 1)), pltpu.SemaphoreType.DMA((n, N_DEV - 1)),
                        pltpu.SemaphoreType.DMA((n,))],
    )(*xs)


def _pair_exchange(gs, name):
    n = len(gs)

    def body(*refs):
        g_refs, o_refs = refs[:n], refs[n:2 * n]
        send_sems, recv_sems = refs[2 * n:]
        mx, my, mc, _ = _my_place()
        copies = [pltpu.make_async_remote_copy(
            src_ref=g_refs[a].at[p, 1 - mc], dst_ref=o_refs[a].at[p], send_sem=send_sems.at[a, p],
            recv_sem=recv_sems.at[a, p], device_id=(mx, my, 1 - mc), device_id_type=MESH)
            for a in range(n) for p in range(N_CHIPS)]
        for cp in copies:
            cp.start()
        for cp in copies:
            cp.wait()

    any_spec = pl.BlockSpec(memory_space=pl.ANY)
    return _pcall(
        body, name=name,
        in_specs=[any_spec] * n, out_specs=[any_spec] * n,
        out_shape=[jax.ShapeDtypeStruct((N_CHIPS,) + g.shape[2:], g.dtype) for g in gs],
        scratch_shapes=[pltpu.SemaphoreType.DMA((n, N_CHIPS)), pltpu.SemaphoreType.DMA((n, N_CHIPS))],
    )(*gs)


def _pair_add(g, other, name):
    chips, _, r, c = g.shape
    tr = _shard_row_tile(r)
    core = lax.axis_index("c").astype(jnp.int32).reshape(1)

    def body(core_ref, g_ref, o_ref, h_ref):
        h_ref[...] = (g_ref[...].astype(F32)[0] + o_ref[...].astype(F32)).astype(h_ref.dtype)

    blk = pl.BlockSpec((1, tr, c), lambda p, i, core_ref: (p, i, 0))
    return _pcall(
        body, name=name,
        grid_spec=pltpu.PrefetchScalarGridSpec(
            num_scalar_prefetch=1, grid=(chips, pl.cdiv(r, tr)),
            in_specs=[pl.BlockSpec((1, 1, tr, c), lambda p, i, core_ref: (p, core_ref[0], i, 0)), blk],
            out_specs=blk),
        out_shape=jax.ShapeDtypeStruct((chips, r, c), g.dtype),
        compiler_params=_params("parallel", "parallel"),
    )(core, g, other)


def _chip_exchange(hs, name):
    n = len(hs)

    def body(*refs):
        h_refs, o_refs = refs[:n], refs[n:2 * n]
        send_sems, recv_sems, local_sems = refs[2 * n:]
        mx, my, mc, _ = _my_place()
        my_chip = 2 * mx + my
        chips = _other_chips(mx, my)
        local = [pltpu.make_async_copy(h_refs[a].at[my_chip], o_refs[a].at[my_chip], local_sems.at[a]) for a in range(n)]
        for cp in local:
            cp.start()
        for j, (px, py) in enumerate(chips):
            for a in range(n):
                pltpu.make_async_remote_copy(
                    src_ref=h_refs[a].at[2 * px + py], dst_ref=o_refs[a].at[my_chip], send_sem=send_sems.at[a, j],
                    recv_sem=recv_sems.at[a, j], device_id=(px, py, mc), device_id_type=MESH).start()
        for j, (px, py) in enumerate(chips):
            for a in range(n):
                pltpu.make_async_remote_copy(
                    src_ref=h_refs[a].at[2 * px + py], dst_ref=o_refs[a].at[2 * px + py], send_sem=send_sems.at[a, j],
                    recv_sem=recv_sems.at[a, j], device_id=(px, py, mc), device_id_type=MESH).wait()
        for cp in local:
            cp.wait()

    any_spec = pl.BlockSpec(memory_space=pl.ANY)
    return _pcall(
        body, name=name,
        in_specs=[any_spec] * n, out_specs=[any_spec] * n,
        out_shape=[jax.ShapeDtypeStruct(h.shape, h.dtype) for h in hs],
        scratch_shapes=[pltpu.SemaphoreType.DMA((n, N_CHIPS - 1)), pltpu.SemaphoreType.DMA((n, N_CHIPS - 1)),
                        pltpu.SemaphoreType.DMA((n,))],
    )(*hs)


def _adamw(parts, w, m, v, name):
    r, c = w.shape
    n_parts = parts.shape[0]
    tr = _shard_row_tile(r)
    bc1 = 1.0 - ADAM_B1 ** ADAM_STEP
    bc2 = 1.0 - ADAM_B2 ** ADAM_STEP

    def body(p_ref, w_ref, m_ref, v_ref, g_ref, d_ref, nm_ref, nv_ref):
        g = p_ref[0].astype(F32)
        for j in range(1, n_parts):
            g = g + p_ref[j].astype(F32)
        nm = ADAM_B1 * m_ref[...] + (1.0 - ADAM_B1) * g
        nv = ADAM_B2 * v_ref[...] + (1.0 - ADAM_B2) * (g * g)
        g_ref[...] = g
        nm_ref[...] = nm
        nv_ref[...] = nv
        d_ref[...] = -ADAM_LR * ((nm / bc1) / (jnp.sqrt(nv / bc2) + ADAM_EPS) + ADAM_WD * w_ref[...])

    blk = pl.BlockSpec((tr, c), lambda i: (i, 0))
    return _pcall(
        body, name=name, grid=(pl.cdiv(r, tr),),
        in_specs=[pl.BlockSpec((n_parts, tr, c), lambda i: (0, i, 0)), blk, blk, blk],
        out_specs=[blk] * 4, out_shape=[jax.ShapeDtypeStruct((r, c), F32)] * 4,
        compiler_params=_params("parallel"),
    )(parts, w, m, v)


def _local_step(x, target, norm_w, w_segs, conv_w, a_log, dt_bias, dn_norm_w, w_o_dn, w_o_dil, w_out, final_norm_w):
    s = x.shape[0]
    w_qkv, w_za, w_ba, w_qb, w_kb, w_vb, w_zb, w_ga, w_gb = w_segs
    conv_w8 = jnp.concatenate([conv_w, jnp.zeros((SUBLANES - conv_w.shape[0], QKV_W), F32)], axis=0)
    pad8 = jnp.zeros((1, DN_HEADS), F32)
    alog_row = jnp.concatenate([pad8, a_log, jnp.zeros((1, LANES - 2 * DN_HEADS), F32)], axis=1)
    dtb_row = jnp.concatenate([pad8, dt_bias, jnp.zeros((1, LANES - 2 * DN_HEADS), F32)], axis=1)
    wf_row = final_norm_w.reshape(1, D_MODEL)

    hb, qkv_pre, z_a, ba, z_b = _rms_proj_fwd(x, norm_w, [w_qkv, w_za, w_ba, w_zb], "rms_proj_fwd_a")
    q_b, k_b, v_b, g_a, g_b = _mm_out(hb, [w_qb, w_kb, w_vb, w_ga, w_gb], "proj_fwd_b", w_is_out_by_in=True)

    qn, kn, vn, bg = _dn_prep_fwd(qkv_pre, ba, conv_w8, alog_row, dtb_row)
    u_d, w_d, qd_d, kd_d, aqk_d, dl_d, t2_d = _delta_prep(qn, kn, vn, bg)
    o_a, vnew_d, st_d = _delta_scan_fwd(u_d, w_d, qd_d, kd_d, aqk_d, dl_d)
    on_b, y_a = _dn_out_fwd(o_a, z_a, dn_norm_w, w_o_dn)

    parts, lses = [], []
    for gi in range(N_DIL):
        o_g, l_g = _attn_fwd(q_b, k_b, v_b, gi)
        parts.append(o_g)
        lses.append(l_g)
    lse, o_joint, ob_b, y_b = _attn_out_fwd(parts, lses, z_b, w_o_dil)

    loss8, dwf8, merged_b, dx2_b, dx2, dya_b, dyb_b, dga_b, dgb_b = _merge_out_final(
        g_a, g_b, y_a, y_b, x, target, w_out, wf_row)

    g_w_out = _mm_tn(merged_b, dx2_b, "out_wgrad")
    g_w_o_dn = _mm_tn(on_b, dya_b, "out_dn_wgrad")
    d_o_a, dza_b, ddnw8 = _dn_out_bwd(dya_b, o_a, z_a, dn_norm_w, w_o_dn)

    g_w_o_dil = _mm_tn(ob_b, dyb_b, "out_dil_wgrad")
    d_o, dzb_b, delta = _attn_out_bwd(dyb_b, o_joint, z_b, w_o_dil)
    dqs, dks, dvs = [], [], []
    for gi in range(N_DIL):
        dq_g, dk_g, dv_g = _attn_bwd(q_b, k_b, v_b, d_o, lse, delta, gi)
        dqs.append(dq_g)
        dks.append(dk_g)
        dvs.append(dv_g)

    dvnew_d, dkd_d, ddl_d = _delta_scan_bwd(w_d, qd_d, kd_d, aqk_d, dl_d, vnew_d, st_d, d_o_a)
    dqn, dkn, dvn, dbg = _delta_post_bwd(qn, kn, vn, bg, t2_d, st_d, vnew_d, d_o_a, dvnew_d, dkd_d, ddl_d)
    dc, dba_b, dsmall8 = _dn_prep_bwd(qkv_pre, ba, conv_w8, alog_row, dtb_row, dqn, dkn, dvn, dbg)
    dqkv_b, dconv8 = _conv_bwd(dc, qkv_pre, conv_w8)

    per_group = lambda w: [w[g * DIL_W:(g + 1) * DIL_W] for g in range(N_DIL)]
    dh_b = _mm_in(dqs + dks + dvs + [dga_b, dgb_b],
                  per_group(w_qb) + per_group(w_kb) + per_group(w_vb) + [w_ga, w_gb], "proj_bwd_b", w_is_out_by_in=True)
    dsegs = [dqkv_b, dza_b, dba_b] + dqs + dks + dvs + [dzb_b, dga_b, dgb_b]
    valid_rows = [d.shape[1] for d in dsegs]
    valid_rows[2] = 2 * DN_HEADS
    g_wt = _proj_wgrad_all(dsegs, valid_rows, hb)
    grad_x, dnw8 = _proj_bwd_rms_in([dqkv_b, dza_b, dba_b, dzb_b], [w_qkv, w_za, w_ba, w_zb], dh_b, x, dx2, norm_w)

    small = dict(norm_w=dnw8[0:1], final_norm_w=dwf8[0:1], dn_norm_w=ddnw8[0:1],
                 a_log=dsmall8[0:1, DN_HEADS:2 * DN_HEADS], dt_bias=dsmall8[1:2, DN_HEADS:2 * DN_HEADS])
    return loss8[0:1, 0:1], grad_x, g_wt, dconv8[0:4], g_w_o_dn, g_w_o_dil, g_w_out, small


def _proj_bwd_rms_in(ds, ws, dh_a, x, dx2, norm_w):
    n_seg = len(ds)

    def body(*refs):
        d_refs, w_refs = refs[:n_seg], refs[n_seg:2 * n_seg]
        da_ref, x_ref, dx2_ref, w_ref, dx_ref, dw_ref = refs[2 * n_seg:]
        dx_ref[...] = da_ref[...]
        for d_ref, wt_ref in zip(d_refs, w_refs):
            for c, wd in _col_chunks(d_ref.shape[1], 1024):
                dx_ref[...] += jnp.dot(d_ref[:, c:c + wd], wt_ref[c:c + wd, :], preferred_element_type=F32)
        xv = x_ref[...]
        r = lax.rsqrt(jnp.mean(xv * xv, axis=-1, keepdims=True) + NORM_EPS)
        dhv = dx_ref[...]
        dn = dhv * w_ref[...]
        dx_ref[...] = dx2_ref[...] + r * dn - xv * (r * r * r) * jnp.mean(dn * xv, axis=-1, keepdims=True)
        row = jnp.sum(dhv * xv * r, axis=0, keepdims=True)
        _acc_add(dw_ref, jnp.concatenate([row, jnp.zeros((SUBLANES - 1, row.shape[1]), F32)], axis=0))

    return _rows_call(body, "proj_bwd_b_rms_in", x.shape[0],
                      [(d, "tile") for d in ds] + [(w, "full") for w in ws]
                      + [(dh_a, "tile"), (x, "tile"), (dx2, "tile"), (norm_w, "full")],
                      [(x.shape, F32, "tile"), ((SUBLANES, x.shape[1]), F32, "acc")])


def _split_proj_rows(wt_full):
    offs = [0]
    for n in PROJ_SIZES:
        offs.append(offs[-1] + n)
    seg = lambda a, b: wt_full[offs[a]:offs[b]]
    w_ba = jnp.concatenate([seg(4, 6), jnp.zeros((LANES - 2 * DN_HEADS, wt_full.shape[1]), wt_full.dtype)], axis=0)
    return [seg(0, 3), seg(3, 4), w_ba, seg(6, 7), seg(7, 8), seg(8, 9), seg(9, 10), seg(10, 11), seg(11, 12)]


LOSS_ROW = 5


def _pack_small(norm_w, final_norm_w, dn_norm_w, a_log, dt_bias, loss=None):
    pad = lambda r: jnp.concatenate([r, jnp.zeros((1, D_MODEL - r.shape[1]), F32)], axis=1)
    rows = [pad(norm_w.reshape(1, -1)), pad(final_norm_w.reshape(1, -1)), pad(dn_norm_w.reshape(1, -1)),
            pad(a_log.reshape(1, -1)), pad(dt_bias.reshape(1, -1)),
            pad(jnp.zeros((1, 1), F32) if loss is None else loss.reshape(1, 1)),
            jnp.zeros((SUBLANES - LOSS_ROW - 1, D_MODEL), F32)]
    return jnp.concatenate(rows, axis=0)


def _unpack_small(p):
    return dict(norm_w=p[0:1], final_norm_w=p[1], dn_norm_w=p[2:3, :DN_DK], a_log=p[3:4, :DN_HEADS],
                dt_bias=p[4:5, :DN_HEADS])


def kernel(x, norm_w, w_in, conv_w, a_log, dt_bias, dn_norm_w, w_o_dn, w_o_dil, w_out, final_norm_w, loss_target, m_norm_w, m_w_in, m_conv_w, m_a_log, m_dt_bias, m_dn_norm_w, m_w_o_dn, m_w_o_dil, m_w_out, m_final_norm_w, v_norm_w, v_w_in, v_conv_w, v_a_log, v_dt_bias, v_dn_norm_w, v_w_o_dn, v_w_o_dil, v_w_out, v_final_norm_w):
    shard_w = w_in.shape[2]
    wt, m_wt, v_wt = (jnp.transpose(t[0]) for t in (w_in, m_w_in, v_w_in))
    gathered = _all_gather([wt.astype(MXU), w_o_dn[0].astype(MXU), w_o_dil[0].astype(MXU), w_out[0].astype(MXU),
                            conv_w[0]], "gather_weights")
    w_in_all, w_o_dn_all, w_o_dil_all, w_out_all, conv_all = gathered
    wt_full = w_in_all.reshape(N_DEV * shard_w, D_MODEL)
    w_o_dn_full = w_o_dn_all.reshape(D_MODEL, D_MODEL)
    w_o_dil_full = jnp.transpose(w_o_dil_all, (1, 0, 2)).reshape(DIL_W, D_MODEL)
    w_out_full = w_out_all.reshape(D_MODEL, D_MODEL)
    conv_full = jnp.transpose(conv_all, (1, 0, 2)).reshape(conv_w.shape[1], QKV_W)

    loss11, grad_x, g_wt, g_conv, g_w_o_dn, g_w_o_dil, g_w_out, small = _local_step(
        x[0], loss_target[0], norm_w, _split_proj_rows(wt_full), conv_full, a_log, dt_bias, dn_norm_w,
        w_o_dn_full, w_o_dil_full, w_out_full, final_norm_w)

    col_shards = lambda g, n: jnp.transpose(g.reshape(g.shape[0], N_DEV, n), (1, 0, 2))
    row_shards = lambda g: g.reshape(N_DEV, g.shape[0] // N_DEV, g.shape[1])
    sent = [row_shards(g_wt), row_shards(g_w_o_dn).astype(MXU),
            col_shards(g_w_o_dil, w_o_dil.shape[2]).astype(MXU), row_shards(g_w_out).astype(MXU),
            col_shards(g_conv, conv_w.shape[2])]
    sent = [g8.reshape((N_CHIPS, 2) + g8.shape[1:]) for g8 in sent]
    from_sibling = _pair_exchange(sent, "scatter_pair")
    summed = [_pair_add(g, o, f"pair_add_{i}") for i, (g, o) in enumerate(zip(sent, from_sibling))]
    p_w_in, p_w_o_dn, p_w_o_dil, p_w_out, p_conv = _chip_exchange(summed, "scatter_chips")
    p_small = _all_gather([_pack_small(small["norm_w"], small["final_norm_w"], small["dn_norm_w"], small["a_log"],
                                       small["dt_bias"], loss11)], "gather_small_grads")[0]

    res = {}
    res["w_in"] = [jnp.transpose(t) for t in _adamw(p_w_in, wt, m_wt, v_wt, "adamw_w_in")]
    res["conv_w"] = _adamw(p_conv, conv_w[0], m_conv_w[0], v_conv_w[0], "adamw_conv_w")
    res["w_o_dn"] = _adamw(p_w_o_dn, w_o_dn[0], m_w_o_dn[0], v_w_o_dn[0], "adamw_w_o_dn")
    res["w_o_dil"] = _adamw(p_w_o_dil, w_o_dil[0], m_w_o_dil[0], v_w_o_dil[0], "adamw_w_o_dil")
    res["w_out"] = _adamw(p_w_out, w_out[0], m_w_out[0], v_w_out[0], "adamw_w_out")
    small_res = _adamw(p_small, _pack_small(norm_w, final_norm_w, dn_norm_w, a_log, dt_bias),
                       _pack_small(m_norm_w, m_final_norm_w, m_dn_norm_w, m_a_log, m_dt_bias),
                       _pack_small(v_norm_w, v_final_norm_w, v_dn_norm_w, v_a_log, v_dt_bias), "adamw_small")
    loss = small_res[0][LOSS_ROW, 0]
    small_res = [_unpack_small(t) for t in small_res]

    names = ["norm_w", "w_in", "conv_w", "a_log", "dt_bias", "dn_norm_w", "w_o_dn", "w_o_dil", "w_out", "final_norm_w"]
    outs = [loss, grad_x[None]]
    for kind in range(4):
        for nm in names:
            outs.append(res[nm][kind][None] if nm in res else small_res[kind][nm])
    return tuple(outs)
```

```python
import math

import jax
import jax.numpy as jnp
from jax import lax
from jax.experimental import pallas as pl
from jax.experimental.pallas import tpu as pltpu

F32 = jnp.float32
MXU = jnp.bfloat16
MESH = pl.DeviceIdType.MESH

N_DEV = 8
D_MODEL = 1024
DN_HEADS = 8
DN_DK = 128
DN_CHUNK = 64
N_DIL = 3
DIL_HEADS = 4
DIL_DH = 128
DIL_W = DIL_HEADS * DIL_DH
DIL_GROUPS = ((128, 1), (512, 4), (2048, 16))
ATT_BLOCK = 128
NORM_EPS = 1e-6
QKV_W = 3 * D_MODEL
DILQ_W = N_DIL * DIL_W
PROJ_SIZES = (1024, 1024, 1024, 1024, 8, 8, DILQ_W, DILQ_W, DILQ_W, DIL_W, D_MODEL, D_MODEL)

ADAM_LR = 0.001
ADAM_B1 = 0.9
ADAM_B2 = 0.999
ADAM_EPS = 1e-08
ADAM_WD = 0.01
ADAM_STEP = 10

ROW_TILE = 256
LANES = 128
SUBLANES = 8
VMEM_LIMIT = 48 << 20


def _pcall(body, **kw):
    return pl.pallas_call(body, **kw)


def _params(*sem):
    return pltpu.CompilerParams(dimension_semantics=tuple(sem), vmem_limit_bytes=VMEM_LIMIT)


def _sigmoid(x):
    return 1.0 / (1.0 + jnp.exp(-x))


def _softplus(x):
    return jnp.maximum(x, 0.0) + jnp.log(1.0 + jnp.exp(-jnp.abs(x)))


def _dot(a, b):
    return jnp.dot(a.astype(MXU), b.astype(MXU), preferred_element_type=F32)


def _dot_nt(a, b):
    return lax.dot_general(a.astype(MXU), b.astype(MXU), (((1,), (1,)), ((), ())), preferred_element_type=F32)


def _dot_tn(a, b):
    return lax.dot_general(a.astype(MXU), b.astype(MXU), (((0,), (0,)), ((), ())), preferred_element_type=F32)


def _split3(x):
    hi = x.astype(jnp.bfloat16)
    r1 = x - hi.astype(F32)
    mid = r1.astype(jnp.bfloat16)
    lo = (r1 - mid.astype(F32)).astype(jnp.bfloat16)
    return hi, mid, lo


def _dot01(m01, x):
    m = m01.astype(jnp.bfloat16)
    hi, mid, lo = _split3(x)
    f = lambda p: jnp.dot(m, p, preferred_element_type=F32)
    return f(hi) + (f(mid) + f(lo))


def _rows_call(body, name, n_rows, ins, outs, scratch=(), tm=ROW_TILE):
    steps = n_rows // tm
    per8 = tm // SUBLANES
    last8 = n_rows // SUBLANES - 1
    in_specs = []
    for arr, kind in ins:
        cols = arr.shape[-1]
        if kind == "tile":
            in_specs.append(pl.BlockSpec((tm, cols), lambda i: (i, 0)))
        elif kind == "full":
            in_specs.append(pl.BlockSpec(arr.shape, lambda i, nd=arr.ndim: (0,) * nd))
        elif kind == "prev8":
            in_specs.append(pl.BlockSpec((SUBLANES, cols), lambda i: (jnp.maximum(i * per8 - 1, 0), 0)))
        elif kind == "next8":
            in_specs.append(pl.BlockSpec((SUBLANES, cols), lambda i: (jnp.minimum((i + 1) * per8, last8), 0)))
        else:
            raise ValueError(kind)
    out_specs, out_shape, has_acc = [], [], False
    for shape, dtype, kind in outs:
        out_shape.append(jax.ShapeDtypeStruct(shape, dtype))
        if kind == "tile":
            out_specs.append(pl.BlockSpec((tm, shape[-1]), lambda i: (i, 0)))
        else:
            has_acc = True
            out_specs.append(pl.BlockSpec(shape, lambda i: (0, 0)))
    return _pcall(
        body, name=name, grid=(steps,), in_specs=in_specs, out_specs=out_specs, out_shape=out_shape,
        scratch_shapes=list(scratch),
        compiler_params=_params("arbitrary" if has_acc else "parallel"),
    )(*[a for a, _ in ins])


def _acc_add(ref, value):
    @pl.when(pl.program_id(0) == 0)
    def _():
        ref[...] = jnp.zeros_like(ref)
    ref[...] += value


def _col_chunks(n, width=512):
    return [(c, min(width, n - c)) for c in range(0, n, width)]


NT_DIMS = (((1,), (1,)), ((), ()))
TN_DIMS = (((0,), (0,)), ((), ()))


def _mm_out(a, ws, name, w_is_out_by_in=False, out_dtype=F32, tm=ROW_TILE):
    m, k = a.shape
    ns = [w.shape[0] if w_is_out_by_in else w.shape[1] for w in ws]

    def body(a_ref, *refs):
        av = a_ref[...]
        for w_ref, o_ref, n in zip(refs[:len(ws)], refs[len(ws):], ns):
            for c, wd in _col_chunks(n):
                if w_is_out_by_in:
                    part = lax.dot_general(av, w_ref[c:c + wd, :], NT_DIMS, preferred_element_type=F32)
                else:
                    part = jnp.dot(av, w_ref[:, c:c + wd], preferred_element_type=F32)
                o_ref[:, c:c + wd] = part.astype(o_ref.dtype)

    return _pcall(
        body, name=name, grid=(m // tm,),
        in_specs=[pl.BlockSpec((tm, k), lambda i: (i, 0))] + [pl.BlockSpec(w.shape, lambda i: (0, 0)) for w in ws],
        out_specs=[pl.BlockSpec((tm, n), lambda i: (i, 0)) for n in ns],
        out_shape=[jax.ShapeDtypeStruct((m, n), out_dtype) for n in ns],
        compiler_params=_params("parallel"),
    )(a, *ws)


def _rms_proj_fwd(x, norm_w, wts, name, tm=ROW_TILE):
    m, k = x.shape
    ns = [w.shape[0] for w in wts]

    def body(x_ref, nw_ref, *refs):
        w_refs, h_ref, o_refs = refs[:len(wts)], refs[len(wts)], refs[len(wts) + 1:]
        xv = x_ref[...]
        r = lax.rsqrt(jnp.mean(xv * xv, axis=-1, keepdims=True) + NORM_EPS)
        hv = (xv * r * nw_ref[...]).astype(h_ref.dtype)
        h_ref[...] = hv
        for w_ref, o_ref, n in zip(w_refs, o_refs, ns):
            for c, wd in _col_chunks(n):
                o_ref[:, c:c + wd] = lax.dot_general(hv, w_ref[c:c + wd, :], NT_DIMS, preferred_element_type=F32)

    return _pcall(
        body, name=name, grid=(m // tm,),
        in_specs=[pl.BlockSpec((tm, k), lambda i: (i, 0)), pl.BlockSpec(norm_w.shape, lambda i: (0, 0))]
        + [pl.BlockSpec(w.shape, lambda i: (0, 0)) for w in wts],
        out_specs=[pl.BlockSpec((tm, k), lambda i: (i, 0))] + [pl.BlockSpec((tm, n), lambda i: (i, 0)) for n in ns],
        out_shape=[jax.ShapeDtypeStruct((m, k), MXU)] + [jax.ShapeDtypeStruct((m, n), F32) for n in ns],
        compiler_params=_params("parallel"),
    )(x, norm_w, *wts)


def _mm_in(ds, ws, name, w_is_out_by_in=False, tm=ROW_TILE):
    m = ds[0].shape[0]
    k = ws[0].shape[1] if w_is_out_by_in else ws[0].shape[0]
    ns = [d.shape[1] for d in ds]

    def body(*refs):
        d_refs, w_refs, o_ref = refs[:len(ds)], refs[len(ds):2 * len(ds)], refs[-1]
        first = True
        for d_ref, w_ref, n in zip(d_refs, w_refs, ns):
            for c, wd in _col_chunks(n, 1024):
                if w_is_out_by_in:
                    part = jnp.dot(d_ref[:, c:c + wd], w_ref[c:c + wd, :], preferred_element_type=F32)
                else:
                    part = lax.dot_general(d_ref[:, c:c + wd], w_ref[:, c:c + wd], NT_DIMS, preferred_element_type=F32)
                if first:
                    o_ref[...] = part
                    first = False
                else:
                    o_ref[...] += part

    return _pcall(
        body, name=name, grid=(m // tm,),
        in_specs=[pl.BlockSpec((tm, n), lambda i: (i, 0)) for n in ns] + [pl.BlockSpec(w.shape, lambda i: (0, 0)) for w in ws],
        out_specs=pl.BlockSpec((tm, k), lambda i: (i, 0)),
        out_shape=jax.ShapeDtypeStruct((m, k), F32),
        compiler_params=_params("parallel"),
    )(*ds, *ws)


def _mm_tn(a, d, name):
    m, k = a.shape
    n = d.shape[1]
    tk = 512 if k % 512 == 0 else k

    def body(a_ref, d_ref, o_ref):
        o_ref[...] = lax.dot_general(a_ref[...], d_ref[...], TN_DIMS, preferred_element_type=F32)

    return _pcall(
        body, name=name, grid=(k // tk,),
        in_specs=[pl.BlockSpec((m, tk), lambda p: (0, p)), pl.BlockSpec((m, n), lambda p: (0, 0))],
        out_specs=pl.BlockSpec((tk, n), lambda p: (p, 0)),
        out_shape=jax.ShapeDtypeStruct((k, n), F32),
        compiler_params=_params("parallel"),
    )(a, d)


WGRAD_TILE = 512


def _proj_wgrad_all(dsegs, valid_rows, hb):
    m, k = hb.shape
    n_seg = len(dsegs)
    tiles, row = [], 0
    for si, (d, valid) in enumerate(zip(dsegs, valid_rows)):
        for c in range(0, valid, WGRAD_TILE):
            width = min(WGRAD_TILE, d.shape[1] - c)
            tiles.append((si, c, width, row + c, min(width, valid - c)))
        row += valid
    total_rows = row

    def body(*refs):
        d_refs, hb_ref, o_ref = refs[:n_seg], refs[n_seg], refs[n_seg + 1]
        a_buf, hb_buf, o_buf, load_sems, store_sems, hb_sem = refs[n_seg + 2:]

        def load(t):
            si, c, width, _, _ = tiles[t]
            return pltpu.make_async_copy(d_refs[si].at[:, pl.ds(c, width)], a_buf.at[t % 2, :, pl.ds(0, width)],
                                         load_sems.at[t % 2])

        def stores(t):
            _, _, _, orow, valid = tiles[t]
            return [pltpu.make_async_copy(o_buf.at[t % 2, pl.ds(0, valid), :], o_ref.at[pl.ds(orow, valid), :],
                                          store_sems.at[t % 2])]

        hb_copy = pltpu.make_async_copy(hb_ref, hb_buf, hb_sem)
        hb_copy.start()
        load(0).start()
        hb_copy.wait()
        for t in range(len(tiles)):
            width = tiles[t][2]
            load(t).wait()
            if t + 1 < len(tiles):
                load(t + 1).start()
            if t >= 2:
                for cp in stores(t - 2):
                    cp.wait()
            o_buf[t % 2, 0:width, :] = lax.dot_general(a_buf[t % 2, :, 0:width], hb_buf[...], TN_DIMS,
                                                        preferred_element_type=F32).astype(o_buf.dtype)
            for cp in stores(t):
                cp.start()
        for t in range(max(len(tiles) - 2, 0), len(tiles)):
            for cp in stores(t):
                cp.wait()

    any_spec = pl.BlockSpec(memory_space=pl.ANY)
    return _pcall(
        body, name="proj_wgrad",
        in_specs=[any_spec] * (n_seg + 1), out_specs=any_spec,
        out_shape=jax.ShapeDtypeStruct((total_rows, k), hb.dtype),
        scratch_shapes=[pltpu.VMEM((2, m, WGRAD_TILE), hb.dtype), pltpu.VMEM((m, k), hb.dtype),
                        pltpu.VMEM((2, WGRAD_TILE, k), hb.dtype), pltpu.SemaphoreType.DMA((2,)),
                        pltpu.SemaphoreType.DMA((2,)), pltpu.SemaphoreType.DMA],
        compiler_params=pltpu.CompilerParams(vmem_limit_bytes=VMEM_LIMIT),
    )(*dsegs, hb)


def _conv_taps(ext_ref, cw_ref, cols, tm):
    c = None
    for j in range(4):
        term = cw_ref[3 - j:4 - j, cols] * ext_ref[SUBLANES - j:SUBLANES - j + tm, cols]
        c = term if c is None else c + term
    return c


def _fill_ext(ext_ref, u_ref, halo_ref, first):
    ext_ref[0:SUBLANES, :] = jnp.where(first, 0.0, halo_ref[...])
    ext_ref[SUBLANES:, :] = u_ref[...]


def _dn_prep_fwd(qkv_pre, ba, conv_w8, alog_row, dtb_row):
    s = qkv_pre.shape[0]
    tm = ROW_TILE

    def body(u_ref, halo_ref, cw_ref, ba_ref, al_ref, dtb_ref, q_ref, k_ref, v_ref, bg_ref, ext_ref):
        _fill_ext(ext_ref, u_ref, halo_ref, pl.program_id(0) == 0)
        for h in range(3 * DN_HEADS):
            cols = slice(h * LANES, (h + 1) * LANES)
            c = _conv_taps(ext_ref, cw_ref, cols, tm)
            a = c * _sigmoid(c)
            oc = slice((h % DN_HEADS) * LANES, (h % DN_HEADS + 1) * LANES)
            if h < 2 * DN_HEADS:
                rinv = lax.rsqrt(jnp.sum(a * a, axis=-1, keepdims=True) + NORM_EPS)
                if h < DN_HEADS:
                    q_ref[:, oc] = a * (rinv * DN_DK ** -0.5)
                else:
                    k_ref[:, oc] = a * rinv
            else:
                v_ref[:, oc] = a
        bav = ba_ref[...]
        lane = lax.broadcasted_iota(jnp.int32, bav.shape, 1)
        beta = _sigmoid(bav)
        g = -jnp.exp(al_ref[...]) * _softplus(bav + dtb_ref[...])
        bg_ref[...] = jnp.where(lane < DN_HEADS, beta, jnp.where(lane < 2 * DN_HEADS, g, 0.0))

    return _rows_call(
        body, "dn_prep_fwd", s,
        [(qkv_pre, "tile"), (qkv_pre, "prev8"), (conv_w8, "full"), (ba, "tile"), (alog_row, "full"), (dtb_row, "full")],
        [((s, D_MODEL), F32, "tile")] * 3 + [((s, LANES), F32, "tile")],
        scratch=[pltpu.VMEM((tm + SUBLANES, QKV_W), F32)])


def _dn_prep_bwd(qkv_pre, ba, conv_w8, alog_row, dtb_row, dq, dk, dv, dbg):
    s = qkv_pre.shape[0]
    tm = ROW_TILE

    def body(u_ref, halo_ref, cw_ref, ba_ref, al_ref, dtb_ref, dq_ref, dk_ref, dv_ref, dbg_ref,
             dc_ref, dba_ref, dsmall_ref, ext_ref):
        _fill_ext(ext_ref, u_ref, halo_ref, pl.program_id(0) == 0)
        for h in range(3 * DN_HEADS):
            cols = slice(h * LANES, (h + 1) * LANES)
            oc = slice((h % DN_HEADS) * LANES, (h % DN_HEADS + 1) * LANES)
            c = _conv_taps(ext_ref, cw_ref, cols, tm)
            sg = _sigmoid(c)
            a = c * sg
            if h < 2 * DN_HEADS:
                rinv = lax.rsqrt(jnp.sum(a * a, axis=-1, keepdims=True) + NORM_EPS)
                dy = dq_ref[:, oc] * DN_DK ** -0.5 if h < DN_HEADS else dk_ref[:, oc]
                da = rinv * dy - a * (rinv * rinv * rinv) * jnp.sum(dy * a, axis=-1, keepdims=True)
            else:
                da = dv_ref[:, oc]
            dc_ref[:, cols] = da * (sg * (1.0 + c * (1.0 - sg)))
        bav = ba_ref[...]
        dbgv = dbg_ref[...]
        lane = lax.broadcasted_iota(jnp.int32, bav.shape, 1)
        beta = _sigmoid(bav)
        ea = jnp.exp(al_ref[...])
        z = bav + dtb_ref[...]
        g = -ea * _softplus(z)
        is_b = lane < DN_HEADS
        is_g = jnp.logical_and(lane >= DN_HEADS, lane < 2 * DN_HEADS)
        d_aa = jnp.where(is_g, dbgv * (-ea) * _sigmoid(z), 0.0)
        dba = jnp.where(is_b, dbgv * beta * (1.0 - beta), d_aa)
        dba_ref[...] = dba.astype(dba_ref.dtype)
        r_alog = jnp.sum(jnp.where(is_g, dbgv * g, 0.0), axis=0, keepdims=True)
        r_dtb = jnp.sum(d_aa, axis=0, keepdims=True)
        _acc_add(dsmall_ref, jnp.concatenate([r_alog, r_dtb, jnp.zeros((SUBLANES - 2, LANES), F32)], axis=0))

    return _rows_call(
        body, "dn_prep_bwd", s,
        [(qkv_pre, "tile"), (qkv_pre, "prev8"), (conv_w8, "full"), (ba, "tile"), (alog_row, "full"), (dtb_row, "full"),
         (dq, "tile"), (dk, "tile"), (dv, "tile"), (dbg, "tile")],
        [((s, QKV_W), F32, "tile"), ((s, LANES), MXU, "tile"), ((SUBLANES, LANES), F32, "acc")],
        scratch=[pltpu.VMEM((tm + SUBLANES, QKV_W), F32)])


def _conv_bwd(dc, qkv_pre, conv_w8):
    s = dc.shape[0]
    tm = ROW_TILE
    steps = s // tm

    def body(dc_ref, dnext_ref, u_ref, halo_ref, cw_ref, du_ref, dcw_ref, extd_ref, ext_ref):
        i = pl.program_id(0)
        _fill_ext(ext_ref, u_ref, halo_ref, i == 0)
        extd_ref[0:tm, :] = dc_ref[...]
        extd_ref[tm:, :] = jnp.where(i == steps - 1, 0.0, dnext_ref[...])

        @pl.when(i == 0)
        def _():
            dcw_ref[...] = jnp.zeros_like(dcw_ref)

        for h in range(3 * DN_HEADS):
            cols = slice(h * LANES, (h + 1) * LANES)
            du = None
            for j in range(4):
                term = cw_ref[3 - j:4 - j, cols] * extd_ref[j:j + tm, cols]
                du = term if du is None else du + term
            du_ref[:, cols] = du.astype(du_ref.dtype)
            dcv = dc_ref[:, cols]
            for j in range(4):
                row = jnp.sum(dcv * ext_ref[SUBLANES - j:SUBLANES - j + tm, cols], axis=0, keepdims=True)
                dcw_ref[3 - j:4 - j, cols] += row

    return _rows_call(
        body, "conv_bwd", s,
        [(dc, "tile"), (dc, "next8"), (qkv_pre, "tile"), (qkv_pre, "prev8"), (conv_w8, "full")],
        [((s, QKV_W), MXU, "tile"), ((SUBLANES, QKV_W), F32, "acc")],
        scratch=[pltpu.VMEM((tm + SUBLANES, QKV_W), F32), pltpu.VMEM((tm + SUBLANES, QKV_W), F32)])


def _dn_out_fwd(o, z, dnw_row, w_o_dn):
    def body(o_ref, z_ref, w_ref, wo_ref, on_ref, y_ref):
        for h in range(DN_HEADS):
            cols = slice(h * LANES, (h + 1) * LANES)
            ov = o_ref[:, cols]
            zv = z_ref[:, cols]
            ro = lax.rsqrt(jnp.mean(ov * ov, axis=-1, keepdims=True) + NORM_EPS)
            on_ref[:, cols] = (ov * ro * w_ref[...] * (zv * _sigmoid(zv))).astype(on_ref.dtype)
        y_ref[...] = jnp.dot(on_ref[...], wo_ref[...], preferred_element_type=F32)

    return _rows_call(body, "dn_out_fwd", o.shape[0], [(o, "tile"), (z, "tile"), (dnw_row, "full"), (w_o_dn, "full")],
                      [(o.shape, MXU, "tile"), ((o.shape[0], w_o_dn.shape[1]), F32, "tile")])


def _dn_out_bwd(dy, o, z, dnw_row, w_o_dn):
    def body(dy_ref, o_ref, z_ref, w_ref, wo_ref, do_ref, dz_ref, dw_ref, d_ref):
        d_ref[...] = lax.dot_general(dy_ref[...], wo_ref[...], NT_DIMS, preferred_element_type=F32)
        acc = jnp.zeros((1, LANES), F32)
        for h in range(DN_HEADS):
            cols = slice(h * LANES, (h + 1) * LANES)
            dv, ov, zv = d_ref[:, cols], o_ref[:, cols], z_ref[:, cols]
            sg = _sigmoid(zv)
            sz = zv * sg
            ro = lax.rsqrt(jnp.mean(ov * ov, axis=-1, keepdims=True) + NORM_EPS)
            nv = ov * ro
            dn = dv * w_ref[...] * sz
            acc = acc + jnp.sum(dv * nv * sz, axis=0, keepdims=True)
            dz_ref[:, cols] = (dv * nv * w_ref[...] * (sg * (1.0 + zv * (1.0 - sg)))).astype(dz_ref.dtype)
            do_ref[:, cols] = ro * dn - ov * (ro * ro * ro) * jnp.mean(dn * ov, axis=-1, keepdims=True)
        _acc_add(dw_ref, jnp.concatenate([acc, jnp.zeros((SUBLANES - 1, LANES), F32)], axis=0))

    return _rows_call(body, "dn_out_bwd", o.shape[0],
                      [(dy, "tile"), (o, "tile"), (z, "tile"), (dnw_row, "full"), (w_o_dn, "full")],
                      [(o.shape, F32, "tile"), (o.shape, MXU, "tile"), ((SUBLANES, LANES), F32, "acc")],
                      scratch=[pltpu.VMEM((ROW_TILE, o.shape[1]), F32)])


def _attn_out_fwd(parts, lses, zb, w_o_dil):
    def body(o0, o1, o2, l0, l1, l2, z_ref, wo_ref, lse_ref, o_ref, g_ref, y_ref):
        a, b, c = l0[...], l1[...], l2[...]
        m = jnp.maximum(a, jnp.maximum(b, c))
        ea, eb, ec = jnp.exp(a - m), jnp.exp(b - m), jnp.exp(c - m)
        den = ea + eb + ec
        out = (ea * o0[...] + eb * o1[...] + ec * o2[...]) / den
        lse_ref[...] = m + jnp.log(den)
        o_ref[...] = out
        zv = z_ref[...]
        gated = (out * (zv * _sigmoid(zv))).astype(g_ref.dtype)
        g_ref[...] = gated
        y_ref[...] = jnp.dot(gated, wo_ref[...], preferred_element_type=F32)

    s = zb.shape[0]
    return _rows_call(body, "attn_out_fwd", s,
                      [(p, "tile") for p in parts] + [(l, "tile") for l in lses] + [(zb, "tile"), (w_o_dil, "full")],
                      [((s, DIL_W), F32, "tile"), ((s, DIL_W), F32, "tile"), ((s, DIL_W), MXU, "tile"),
                       ((s, w_o_dil.shape[1]), F32, "tile")])


def _attn_out_bwd(dy, o_joint, zb, w_o_dil):
    def body(dy_ref, o_ref, z_ref, wo_ref, do_ref, dz_ref, dl_ref):
        zv = z_ref[...]
        sg = _sigmoid(zv)
        dv = lax.dot_general(dy_ref[...], wo_ref[...], NT_DIMS, preferred_element_type=F32)
        ov = o_ref[...]
        do = dv * (zv * sg)
        do_ref[...] = do
        dz_ref[...] = (dv * ov * (sg * (1.0 + zv * (1.0 - sg)))).astype(dz_ref.dtype)
        for h in range(DIL_HEADS):
            cols = slice(h * LANES, (h + 1) * LANES)
            dl_ref[:, cols] = jnp.broadcast_to(jnp.sum(do[:, cols] * ov[:, cols], axis=-1, keepdims=True),
                                               (do.shape[0], LANES))

    s = zb.shape[0]
    return _rows_call(body, "attn_out_bwd", s, [(dy, "tile"), (o_joint, "tile"), (zb, "tile"), (w_o_dil, "full")],
                      [((s, DIL_W), F32, "tile"), ((s, DIL_W), MXU, "tile"), ((s, DIL_W), F32, "tile")])


def _merge_out_final(ga, gb, ya, yb, x, target, w_out, wf_row):
    s, dm = x.shape

    def body(ga_ref, gb_ref, ya_ref, yb_ref, x_ref, t_ref, wo_ref, w_ref,
             loss_ref, dw_ref, m_ref, dxb_ref, dx_ref, dya_ref, dyb_ref, dga_ref, dgb_ref):
        sa, sb = _sigmoid(ga_ref[...]), _sigmoid(gb_ref[...])
        ya, yb = ya_ref[...], yb_ref[...]
        merged = (sa * ya + sb * yb).astype(MXU)
        m_ref[...] = merged
        x2 = x_ref[...] + jnp.dot(merged, wo_ref[...], preferred_element_type=F32)
        r = lax.rsqrt(jnp.mean(x2 * x2, axis=-1, keepdims=True) + NORM_EPS)
        w = w_ref[...]
        err = x2 * r * w - t_ref[...]
        tile_loss = 0.5 * jnp.sum(jnp.mean(err * err, axis=-1, keepdims=True), axis=0, keepdims=True)
        _acc_add(loss_ref, jnp.broadcast_to(tile_loss, (SUBLANES, LANES)))
        dy = err * (1.0 / dm)
        row = jnp.sum(dy * x2 * r, axis=0, keepdims=True)
        _acc_add(dw_ref, jnp.concatenate([row, jnp.zeros((SUBLANES - 1, dm), F32)], axis=0))
        dn = dy * w
        dx2 = r * dn - x2 * (r * r * r) * jnp.mean(dn * x2, axis=-1, keepdims=True)
        dx_ref[...] = dx2
        dxb = dx2.astype(MXU)
        dxb_ref[...] = dxb
        dmv = lax.dot_general(dxb, wo_ref[...], NT_DIMS, preferred_element_type=F32)
        dya_ref[...] = (dmv * sa).astype(dya_ref.dtype)
        dyb_ref[...] = (dmv * sb).astype(dyb_ref.dtype)
        dga_ref[...] = (dmv * ya * sa * (1.0 - sa)).astype(dga_ref.dtype)
        dgb_ref[...] = (dmv * yb * sb * (1.0 - sb)).astype(dgb_ref.dtype)

    return _rows_call(body, "merge_out_final", s,
                      [(ga, "tile"), (gb, "tile"), (ya, "tile"), (yb, "tile"), (x, "tile"), (target, "tile"),
                       (w_out, "full"), (wf_row, "full")],
                      [((SUBLANES, LANES), F32, "acc"), ((SUBLANES, dm), F32, "acc"), ((s, dm), MXU, "tile"),
                       ((s, dm), MXU, "tile"), ((s, dm), F32, "tile")] + [((s, dm), MXU, "tile")] * 4)


def _lane_pick(x, idx):
    lane = lax.broadcasted_iota(jnp.int32, x.shape, 1)
    return jnp.sum(jnp.where(lane == idx, x, 0.0), axis=-1, keepdims=True)


PAIR = 2 * DN_CHUNK
SCAN_CHUNKS = 4


def _bmm(a, b):
    return lax.dot_general(a.astype(MXU), b.astype(MXU), (((2,), (1,)), ((0,), (0,))), preferred_element_type=F32)


def _bmm_nt(a, b):
    return lax.dot_general(a.astype(MXU), b.astype(MXU), (((2,), (2,)), ((0,), (0,))), preferred_element_type=F32)


def _bmm_tn(a, b):
    return lax.dot_general(a.astype(MXU), b.astype(MXU), (((1,), (1,)), ((0,), (0,))), preferred_element_type=F32)


def _bmm3(a, b):
    ah = a.astype(jnp.bfloat16)
    al = (a - ah.astype(F32)).astype(jnp.bfloat16)
    bh = b.astype(jnp.bfloat16)
    bl = (b - bh.astype(F32)).astype(jnp.bfloat16)
    f = lambda p, q: lax.dot_general(p, q, (((2,), (1,)), ((0,), (0,))), preferred_element_type=F32)
    return f(ah, bh) + (f(ah, bl) + f(al, bh))


def _pair_masks():
    row = lax.broadcasted_iota(jnp.int32, (PAIR, PAIR), 0)
    col = lax.broadcasted_iota(jnp.int32, (PAIR, PAIR), 1)
    same = (row >= DN_CHUNK) == (col >= DN_CHUNK)
    return dict(causal=same & (row >= col), strict=same & (row > col), upper=same & (row <= col), eye=row == col,
                first=row < DN_CHUNK, row=row, lane=col)


def _pair_decay(bgv, masks):
    gc_all = _dot01(masks["causal"].astype(F32), bgv)
    out = []
    for h in range(DN_HEADS):
        beta = _lane_pick(bgv, h)
        gcb = jnp.broadcast_to(_lane_pick(gc_all, DN_HEADS + h), (PAIR, PAIR))
        gam = jnp.where(masks["causal"], jnp.exp(jnp.minimum(gcb - gcb.T, 0.0)), 0.0)
        gl = jnp.where(masks["first"], gcb[DN_CHUNK - 1:DN_CHUNK, :], gcb[PAIR - 1:PAIR, :])
        out.append((beta, gcb, gam, gl))
    return out


def _pair_inverse(a_strict, eye):
    eye_f = eye.astype(F32)[None]
    m = eye_f + a_strict
    x = eye_f - a_strict
    steps = int(math.log2(DN_CHUNK)) - 1
    for i in range(steps):
        mm = _bmm3 if i == steps - 1 else _bmm
        x = x + mm(x, eye_f - mm(m, x))
    return x


def _head_cols(h):
    return slice(h * LANES, (h + 1) * LANES)


def _delta_prep(q, k, v, bg):
    s = q.shape[0]
    c = DN_CHUNK
    n_chunks = s // c

    def body(q_ref, k_ref, v_ref, bg_ref, u_ref, w_ref, qd_ref, kd_ref, aqk_ref, dl_ref, t2_ref):
        masks = _pair_masks()
        dec = _pair_decay(bg_ref[...], masks)
        kbs, ks, gams, vbs, kbes, qs, qds, kds, dls = ([] for _ in range(9))
        for h in range(DN_HEADS):
            beta, gcb, gam, gl = dec[h]
            qh, kh, vh = q_ref[:, _head_cols(h)], k_ref[:, _head_cols(h)], v_ref[:, _head_cols(h)]
            eg = jnp.exp(gcb)
            kb = kh * beta
            kbs.append(kb); ks.append(kh); gams.append(gam); vbs.append(vh * beta); kbes.append(kb * eg)
            qs.append(qh); qds.append(qh * eg); kds.append(kh * jnp.exp(gl - gcb)); dls.append(jnp.exp(gl))
        st = lambda xs: jnp.stack(xs, axis=0)
        kmat, gam = st(ks), st(gams)
        a = jnp.where(masks["strict"][None], _bmm_nt(st(kbs), kmat) * gam, 0.0)
        t = _pair_inverse(a, masks["eye"])
        u = _bmm(t, st(vbs))
        w = _bmm(t, st(kbes))
        aqk = _bmm_nt(st(qs), kmat) * gam
        t2_ref[0] = t.astype(t2_ref.dtype)
        for half in range(2):
            rows = slice(half * c, (half + 1) * c)
            u_ref[half] = u[:, rows, :]
            w_ref[half] = w[:, rows, :].astype(w_ref.dtype)
            qd_ref[half] = st(qds)[:, rows, :].astype(qd_ref.dtype)
            kd_ref[half] = st(kds)[:, rows, :].astype(kd_ref.dtype)
            aqk_ref[half] = aqk[:, rows, rows].astype(aqk_ref.dtype)
            dl_ref[half] = st(dls)[:, half * c:half * c + SUBLANES, :]

    row_spec = lambda w_: pl.BlockSpec((PAIR, w_), lambda i: (i, 0))
    hm = lambda a_, b_: pl.BlockSpec((2, DN_HEADS, a_, b_), lambda i: (i, 0, 0, 0))
    hm_shape = lambda a_, b_, dt: jax.ShapeDtypeStruct((n_chunks, DN_HEADS, a_, b_), dt)
    return _pcall(
        body, name="delta_prep", grid=(n_chunks // 2,),
        in_specs=[row_spec(D_MODEL)] * 3 + [row_spec(LANES)],
        out_specs=[hm(c, LANES)] * 4 + [hm(c, c), hm(SUBLANES, LANES),
                   pl.BlockSpec((1, DN_HEADS, PAIR, PAIR), lambda i: (i, 0, 0, 0))],
        out_shape=[hm_shape(c, LANES, F32), hm_shape(c, LANES, MXU), hm_shape(c, LANES, MXU), hm_shape(c, LANES, MXU),
                   hm_shape(c, c, MXU), hm_shape(SUBLANES, LANES, F32),
                   jax.ShapeDtypeStruct((n_chunks // 2, DN_HEADS, PAIR, PAIR), MXU)],
        compiler_params=_params("parallel"),
    )(q, k, v, bg)


def _delta_scan_fwd(u, w, qd, kd, aqk, dl):
    n_chunks = u.shape[0]
    c = DN_CHUNK
    g_n = SCAN_CHUNKS

    def body(u_ref, w_ref, qd_ref, kd_ref, aqk_ref, dl_ref, o_ref, vnew_ref, st_ref, state):
        @pl.when(pl.program_id(0) == 0)
        def _():
            state[...] = jnp.zeros_like(state)

        for g in range(g_n):
            sv = state[...]
            sb = sv.astype(MXU)
            vnew = u_ref[g] - _bmm(w_ref[g], sb)
            o = _bmm(qd_ref[g], sb) + _bmm(aqk_ref[g], vnew)
            state[...] = sv * dl_ref[g][:, 0:1, :] + _bmm_tn(kd_ref[g], vnew)
            vnew_ref[g] = vnew.astype(vnew_ref.dtype)
            st_ref[g] = sb
            for h in range(DN_HEADS):
                o_ref[g * c:(g + 1) * c, _head_cols(h)] = o[h]

    hm = lambda a_, b_: pl.BlockSpec((g_n, DN_HEADS, a_, b_), lambda i: (i, 0, 0, 0))
    return _pcall(
        body, name="delta_scan_fwd", grid=(n_chunks // g_n,),
        in_specs=[hm(c, LANES)] * 4 + [hm(c, c), hm(SUBLANES, LANES)],
        out_specs=[pl.BlockSpec((g_n * c, D_MODEL), lambda i: (i, 0)), hm(c, LANES), hm(DN_DK, DN_DK)],
        out_shape=[jax.ShapeDtypeStruct((n_chunks * c, D_MODEL), F32),
                   jax.ShapeDtypeStruct((n_chunks, DN_HEADS, c, LANES), MXU),
                   jax.ShapeDtypeStruct((n_chunks, DN_HEADS, DN_DK, DN_DK), MXU)],
        scratch_shapes=[pltpu.VMEM((DN_HEADS, DN_DK, DN_DK), F32)],
        compiler_params=_params("arbitrary"),
    )(u, w, qd, kd, aqk, dl)


def _delta_scan_bwd(w, qd, kd, aqk, dl, vnew, st, do):
    n_chunks = w.shape[0]
    c = DN_CHUNK
    g_n = SCAN_CHUNKS
    steps = n_chunks // g_n

    def body(w_ref, qd_ref, kd_ref, aqk_ref, dl_ref, vnew_ref, st_ref, do_ref, dvnew_ref, dkd_ref, ddl_ref, dstate):
        @pl.when(pl.program_id(0) == 0)
        def _():
            dstate[...] = jnp.zeros_like(dstate)

        for g in reversed(range(g_n)):
            ds = dstate[...]
            dsb = ds.astype(MXU)
            doh = jnp.stack([do_ref[g * c:(g + 1) * c, _head_cols(h)] for h in range(DN_HEADS)], axis=0)
            dvnew = _bmm_tn(aqk_ref[g], doh) + _bmm(kd_ref[g], dsb)
            dkd_ref[g] = _bmm_nt(vnew_ref[g], dsb)
            ddl = jnp.sum(jnp.sum(st_ref[g].astype(F32) * ds, axis=2, keepdims=True), axis=1, keepdims=True)
            ddl_ref[g] = jnp.broadcast_to(ddl, (DN_HEADS, SUBLANES, LANES))
            dstate[...] = ds * dl_ref[g][:, 0:1, :] + _bmm_tn(qd_ref[g], doh) - _bmm_tn(w_ref[g], dvnew)
            dvnew_ref[g] = dvnew.astype(dvnew_ref.dtype)

    rev = lambda i: steps - 1 - i
    hm = lambda a_, b_: pl.BlockSpec((g_n, DN_HEADS, a_, b_), lambda i: (rev(i), 0, 0, 0))
    return _pcall(
        body, name="delta_scan_bwd", grid=(steps,),
        in_specs=[hm(c, LANES)] * 3 + [hm(c, c), hm(SUBLANES, LANES), hm(c, LANES), hm(DN_DK, DN_DK),
                  pl.BlockSpec((g_n * c, D_MODEL), lambda i: (rev(i), 0))],
        out_specs=[hm(c, LANES), hm(c, LANES), hm(SUBLANES, LANES)],
        out_shape=[jax.ShapeDtypeStruct((n_chunks, DN_HEADS, c, LANES), MXU),
                   jax.ShapeDtypeStruct((n_chunks, DN_HEADS, c, LANES), F32),
                   jax.ShapeDtypeStruct((n_chunks, DN_HEADS, SUBLANES, LANES), F32)],
        scratch_shapes=[pltpu.VMEM((DN_HEADS, DN_DK, DN_DK), F32)],
        compiler_params=_params("arbitrary"),
    )(w, qd, kd, aqk, dl, vnew, st, do)


def _delta_post_bwd(q, k, v, bg, t2, st, vnew, do, dvnew, dkd, ddl):
    s = q.shape[0]
    c = DN_CHUNK

    def body(q_ref, k_ref, v_ref, bg_ref, t2_ref, st_ref, vnew_ref, do_ref, dvnew_ref, dkd_ref, ddl_ref,
             dq_ref, dk_ref, dv_ref, dbg_ref):
        masks = _pair_masks()
        first = masks["first"][None]
        dec = _pair_decay(bg_ref[...], masks)
        st_ = lambda xs: jnp.stack(xs, axis=0)
        heads = range(DN_HEADS)
        qm_, km_, vm_, dom = (st_([r[:, _head_cols(h)] for h in heads]) for r in (q_ref, k_ref, v_ref, do_ref))
        beta = st_([dec[h][0] for h in heads])
        gcb = st_([dec[h][1] for h in heads])
        gam = st_([dec[h][2] for h in heads])
        gl = st_([dec[h][3] for h in heads])
        pair = lambda ref: jnp.concatenate([ref[0], ref[1]], axis=1)
        vnew2, dvnew2, dkd2 = pair(vnew_ref), pair(dvnew_ref), pair(dkd_ref)
        halves = lambda x: (x[:, :c, :], x[:, c:, :])
        by_state = lambda x: jnp.concatenate([_bmm_nt(xh, st_ref[i]) for i, xh in enumerate(halves(x))], axis=1)
        dqd = by_state(dom)
        dw = -by_state(dvnew2)
        ddl2 = jnp.where(first, ddl_ref[0][:, 0:1, :], ddl_ref[1][:, 0:1, :])

        eg = jnp.exp(gcb)
        egl = jnp.exp(gl - gcb)
        dl = jnp.exp(gl)
        kb = km_ * beta
        kk = _bmm_nt(kb, km_)
        a = jnp.where(masks["strict"][None], kk * gam, 0.0)
        t = t2_ref[0]
        vb = vm_ * beta
        kbe = kb * eg
        u = _bmm(t, vb)
        w = _bmm(t, kbe)
        aqk = _bmm_nt(qm_, km_) * gam
        qd = qm_ * eg
        kd = km_ * egl

        daqk = jnp.where(masks["causal"][None], _bmm_nt(dom, vnew2), 0.0)
        dvb = _bmm_tn(t, dvnew2)
        dkbe = _bmm_tn(t, dw)
        da = jnp.where(masks["strict"][None], -(_bmm_nt(dvb, u) + _bmm_nt(dkbe, w)), 0.0)
        pm = da * gam
        qmm = daqk * gam
        dkb = _bmm(pm, km_) + dkbe * eg
        dkh = _bmm_tn(pm, kb) + _bmm_tn(qmm, qm_) + dkd2 * egl + dkb * beta
        dqh = _bmm(qmm, km_) + dqd * eg
        xm = da * a + daqk * aqk
        ones = jnp.ones((DN_HEADS, PAIR, LANES), F32)
        hi, mid, lo = _split3(xm)
        colsum = _bmm_tn(hi, ones) + (_bmm_tn(mid, ones) + _bmm_tn(lo, ones))
        tmp = jnp.sum(dkd2 * kd, axis=-1, keepdims=True)
        dgc = (jnp.sum(xm, axis=-1, keepdims=True) - colsum + jnp.sum(dkbe * kbe, axis=-1, keepdims=True)
               + jnp.sum(dqd * qd, axis=-1, keepdims=True) - tmp)
        sum0 = jnp.sum(jnp.where(first, tmp, 0.0), axis=1, keepdims=True)
        sum1 = jnp.sum(jnp.where(first, 0.0, tmp), axis=1, keepdims=True)
        dgl = jnp.where(first, sum0, sum1) + ddl2 * dl
        last = (masks["row"] == c - 1) | (masks["row"] == PAIR - 1)
        dgc = dgc + jnp.where(last[None], dgl, 0.0)
        dbeta = jnp.sum(dvb * vm_, axis=-1, keepdims=True) + jnp.sum(dkb * km_, axis=-1, keepdims=True)
        dvh = dvb * beta

        lane = masks["lane"]
        dgc_lanes = jnp.zeros((PAIR, LANES), F32)
        dbg = jnp.zeros((PAIR, LANES), F32)
        for h in heads:
            dq_ref[:, _head_cols(h)] = dqh[h]
            dk_ref[:, _head_cols(h)] = dkh[h]
            dv_ref[:, _head_cols(h)] = dvh[h]
            dgc_lanes = dgc_lanes + jnp.where(lane == DN_HEADS + h, dgc[h], 0.0)
            dbg = dbg + jnp.where(lane == h, dbeta[h], 0.0)
        dbg_ref[...] = dbg + _dot01(masks["upper"].astype(F32), dgc_lanes)

    n_pairs = s // PAIR
    row_spec = lambda w_: pl.BlockSpec((PAIR, w_), lambda i: (i, 0))
    hm = lambda a_, b_: pl.BlockSpec((2, DN_HEADS, a_, b_), lambda i: (i, 0, 0, 0))
    return _pcall(
        body, name="delta_post_bwd", grid=(n_pairs,),
        in_specs=[row_spec(D_MODEL)] * 3 + [row_spec(LANES), pl.BlockSpec((1, DN_HEADS, PAIR, PAIR), lambda i: (i, 0, 0, 0)),
                  hm(DN_DK, DN_DK), hm(c, LANES), row_spec(D_MODEL), hm(c, LANES), hm(c, LANES), hm(SUBLANES, LANES)],
        out_specs=[row_spec(D_MODEL)] * 3 + [row_spec(LANES)],
        out_shape=[jax.ShapeDtypeStruct((s, D_MODEL), F32)] * 3 + [jax.ShapeDtypeStruct((s, LANES), F32)],
        compiler_params=_params("parallel"),
    )(q, k, v, bg, t2, st, vnew, do, dvnew, dkd, ddl)


def _alibi_slope(group, head):
    n = N_DIL * DIL_HEADS
    return float(2.0 ** (-8.0 * (group * DIL_HEADS + head + 1) / n))


def _attn_plan(s, group):
    window, dil = DIL_GROUPS[group]
    assert window // dil == ATT_BLOCK
    assert (s // dil) % ATT_BLOCK == 0, "sub-sequence length must be a whole number of attention blocks"
    return dil, s // dil // ATT_BLOCK, (DIL_HEADS if dil == 1 else 1)


def _attn_specs(group, dil, nb, hp):
    rows = ATT_BLOCK * dil

    def spec(col0, shift):
        if shift < 0:
            f = lambda hb, n: (jnp.maximum(n - 1, 0), col0 + hb)
        elif shift > 0:
            f = lambda hb, n: (jnp.minimum(n + 1, nb - 1), col0 + hb)
        else:
            f = lambda hb, n: (jnp.minimum(n, nb - 1), col0 + hb)
        return pl.BlockSpec((rows, hp * LANES), f)

    return (lambda shift: spec(group * (DIL_HEADS // hp), shift)), (lambda shift: spec(0, shift))


def _sub_rows(ref, r, dil, cols):
    return ref[:, cols] if dil == 1 else ref[pl.ds(r, ATT_BLOCK, stride=dil), cols]


def _set_sub_rows(ref, r, dil, cols, value):
    if dil == 1:
        ref[:, cols] = value
    else:
        ref[pl.ds(r, ATT_BLOCK, stride=dil), cols] = value


def _step_slope(group, hp, hh):
    if hp == DIL_HEADS:
        return _alibi_slope(group, hh)
    hb = pl.program_id(0)
    slope = _alibi_slope(group, DIL_HEADS - 1)
    for h in reversed(range(DIL_HEADS - 1)):
        slope = jnp.where(hb == h, _alibi_slope(group, h), slope)
    return slope


def _window_bias(dil, n):
    a = lax.broadcasted_iota(jnp.int32, (ATT_BLOCK, 2 * ATT_BLOCK), 0)
    b = lax.broadcasted_iota(jnp.int32, (ATT_BLOCK, 2 * ATT_BLOCK), 1)
    dist = ATT_BLOCK + a - b
    valid = (dist >= 0) & (dist <= ATT_BLOCK) & ((b >= ATT_BLOCK) | (n > 0))
    return (dist * dil).astype(F32), valid


def _attn_fwd(qb, kb, vb, group):
    s = qb.shape[0]
    dil, nb, hp = _attn_plan(s, group)
    qkv, per_head = _attn_specs(group, dil, nb, hp)

    def body(q_ref, kp_ref, kc_ref, vp_ref, vc_ref, o_ref, lse_ref):
        n = pl.program_id(1)
        distd, valid = _window_bias(dil, n)
        for hh in range(hp):
            cols = _head_cols(hh)
            slope = _step_slope(group, hp, hh)
            for r in range(dil):
                sub = lambda ref: _sub_rows(ref, r, dil, cols).astype(MXU)
                kk = jnp.concatenate([sub(kp_ref), sub(kc_ref)], axis=0)
                vv = jnp.concatenate([sub(vp_ref), sub(vc_ref)], axis=0)
                sc = _dot_nt(sub(q_ref), kk) * DIL_DH ** -0.5 - slope * distd
                sc = jnp.where(valid, sc, -1e30)
                mx = jnp.max(sc, axis=-1, keepdims=True)
                p = jnp.where(valid, jnp.exp(sc - mx), 0.0)
                den = jnp.sum(p, axis=-1, keepdims=True)
                _set_sub_rows(o_ref, r, dil, cols, _dot(p, vv) / den)
                _set_sub_rows(lse_ref, r, dil, cols, jnp.broadcast_to(mx + jnp.log(den), (ATT_BLOCK, LANES)))

    return _pcall(
        body, name=f"attn_fwd_g{group}", grid=(DIL_HEADS // hp, nb),
        in_specs=[qkv(0), qkv(-1), qkv(0), qkv(-1), qkv(0)], out_specs=[per_head(0)] * 2,
        out_shape=[jax.ShapeDtypeStruct((s, DIL_W), F32)] * 2,
        compiler_params=_params("parallel", "parallel"),
    )(qb, kb, kb, vb, vb)


def _attn_bwd(qb, kb, vb, d_o, lse, delta, group):
    s = qb.shape[0]
    dil, nb, hp = _attn_plan(s, group)
    qkv, per_head = _attn_specs(group, dil, nb, hp)
    scale = DIL_DH ** -0.5

    def body(q_ref, kp_ref, kc_ref, vp_ref, vc_ref, do_ref, l_ref, dl_ref, dq_ref, dk_ref, dv_ref,
             dq_acc, dk_done, dv_done, dk_carry, dv_carry):
        n = pl.program_id(1)
        slopes = [_step_slope(group, hp, hh) for hh in range(hp)]

        @pl.when(n == 0)
        def _():
            dk_carry[...] = jnp.zeros_like(dk_carry)
            dv_carry[...] = jnp.zeros_like(dv_carry)

        @pl.when(n < nb)
        def _():
            distd, valid = _window_bias(dil, n)
            for hh in range(hp):
                cols = _head_cols(hh)
                slope = slopes[hh]
                for r in range(dil):
                    sub = lambda ref: _sub_rows(ref, r, dil, cols)
                    qc, do = sub(q_ref).astype(MXU), sub(do_ref).astype(MXU)
                    kk = jnp.concatenate([sub(kp_ref).astype(MXU), sub(kc_ref).astype(MXU)], axis=0)
                    vv = jnp.concatenate([sub(vp_ref).astype(MXU), sub(vc_ref).astype(MXU)], axis=0)
                    sc = _dot_nt(qc, kk) * scale - slope * distd
                    p = jnp.where(valid, jnp.exp(jnp.minimum(sc - jnp.concatenate([sub(l_ref)] * 2, axis=1), 0.0)), 0.0)
                    dsc = p * (_dot_nt(do, vv) - jnp.concatenate([sub(dl_ref)] * 2, axis=1))
                    _set_sub_rows(dq_acc, r, dil, cols, _dot(dsc, kk) * scale)
                    dkk = _dot_tn(dsc, qc) * scale
                    dvv = _dot_tn(p, do)
                    _set_sub_rows(dk_done, r, dil, cols, _sub_rows(dk_carry, r, dil, cols) + dkk[:ATT_BLOCK])
                    _set_sub_rows(dv_done, r, dil, cols, _sub_rows(dv_carry, r, dil, cols) + dvv[:ATT_BLOCK])
                    _set_sub_rows(dk_carry, r, dil, cols, dkk[ATT_BLOCK:])
                    _set_sub_rows(dv_carry, r, dil, cols, dvv[ATT_BLOCK:])
            dq_ref[...] = dq_acc[...].astype(dq_ref.dtype)
            dk_ref[...] = dk_done[...].astype(dk_ref.dtype)
            dv_ref[...] = dv_done[...].astype(dv_ref.dtype)

        @pl.when(n == nb)
        def _():
            dk_ref[...] = dk_carry[...].astype(dk_ref.dtype)
            dv_ref[...] = dv_carry[...].astype(dv_ref.dtype)

    return _pcall(
        body, name=f"attn_bwd_g{group}", grid=(DIL_HEADS // hp, nb + 1),
        in_specs=[qkv(0), qkv(-1), qkv(0), qkv(-1), qkv(0)] + [per_head(0)] * 3,
        out_specs=[per_head(0), per_head(-1), per_head(-1)],
        out_shape=[jax.ShapeDtypeStruct((s, DIL_W), MXU)] * 3,
        scratch_shapes=[pltpu.VMEM((ATT_BLOCK * dil, hp * LANES), F32)] * 5,
        compiler_params=_params("parallel", "arbitrary"),
    )(qb, kb, kb, vb, vb, d_o, lse, delta)


def _my_place():
    mx, my, mc = lax.axis_index("x"), lax.axis_index("y"), lax.axis_index("c")
    return mx, my, mc, 4 * mx + 2 * my + mc


N_CHIPS = 4


def _shard_row_tile(r):
    if r <= 512:
        return r
    return 128 if r % 128 == 0 else 480


def _other_chips(mx, my):
    return [(1 - mx, my), (mx, 1 - my), (1 - mx, 1 - my)]


def _all_gather(xs, name):
    n = len(xs)
    halved = [x.shape[1] % (2 * LANES) == 0 and x.size * x.dtype.itemsize >= (1 << 20) for x in xs]
    n_sems = 8

    def body(*refs):
        x_refs, o_refs = refs[:n], refs[n:2 * n]
        send_sems, recv_sems, local_sems = refs[2 * n:]
        mx, my, mc, me = _my_place()
        sibling, sibling_id = (mx, my, 1 - mc), 4 * mx + 2 * my + (1 - mc)
        x_nbr, y_nbr, diag = _other_chips(mx, my)
        slot_of = lambda chip, c: 4 * chip[0] + 2 * chip[1] + c

        def part(ref, a, half):
            if not halved[a]:
                return ref
            width = xs[a].shape[1] // 2
            return ref.at[:, pl.ds(half * width, width)]

        def copy(a, k, dst, to, src=None):
            return pltpu.make_async_remote_copy(
                src_ref=dst if src is None else src, dst_ref=dst, send_sem=send_sems.at[a, k],
                recv_sem=recv_sems.at[a, k], device_id=to, device_id_type=MESH)

        local = [pltpu.make_async_copy(x_refs[a], o_refs[a].at[me], local_sems.at[a]) for a in range(n)]
        for cp in local:
            cp.start()
        sends = []
        for a in range(n):
            mine = o_refs[a].at[me]
            sends += [copy(a, 0, mine, sibling, src=x_refs[a]), copy(a, 1, mine, (*x_nbr, mc), src=x_refs[a]),
                      copy(a, 2, mine, (*y_nbr, mc), src=x_refs[a])]
        for cp in sends:
            cp.start()
        for a in range(n):
            blk = o_refs[a].at[slot_of(x_nbr, mc)]
            copy(a, 1, blk, (*x_nbr, mc)).wait_recv()
            sends += [copy(a, 3, blk, sibling), copy(a, 5, part(blk, a, 0), (*y_nbr, mc))]
            sends[-2].start()
            sends[-1].start()
        for a in range(n):
            blk = o_refs[a].at[slot_of(y_nbr, mc)]
            copy(a, 2, blk, (*y_nbr, mc)).wait_recv()
            sends.append(copy(a, 4, blk, sibling))
            sends[-1].start()
            if halved[a]:
                sends.append(copy(a, 6, part(blk, a, 1), (*x_nbr, mc)))
                sends[-1].start()
        for a in range(n):
            blk = o_refs[a].at[slot_of(diag, mc)]
            copy(a, 5, part(blk, a, 0), (*y_nbr, mc)).wait_recv()
            if halved[a]:
                copy(a, 6, part(blk, a, 1), (*x_nbr, mc)).wait_recv()
            sends.append(copy(a, 7, blk, sibling))
            sends[-1].start()
        for a in range(n):
            copy(a, 0, o_refs[a].at[sibling_id], sibling).wait_recv()
            for k, chip in ((3, x_nbr), (4, y_nbr), (7, diag)):
                copy(a, k, o_refs[a].at[slot_of(chip, 1 - mc)], sibling).wait_recv()
        for cp in sends:
            cp.wait_send()
        for cp in local:
            cp.wait()

    any_spec = pl.BlockSpec(memory_space=pl.ANY)
    return _pcall(
        body, name=name,
        in_specs=[any_spec] * n, out_specs=[any_spec] * n,
        out_shape=[jax.ShapeDtypeStruct((N_DEV,) + x.shape, x.dtype) for x in xs],
        scratch_shapes=[pltpu.SemaphoreType.DMA((n, n_sems)), pltpu.SemaphoreType.DMA((n, n_sems)),
                        pltpu.SemaphoreType.DMA((n,))],
    )(*xs)


def _pair_exchange(gs, name):
    n = len(gs)

    def body(*refs):
        g_refs, o_refs = refs[:n], refs[n:2 * n]
        send_sems, recv_sems = refs[2 * n:]
        mx, my, mc, _ = _my_place()
        copies = [pltpu.make_async_remote_copy(
            src_ref=g_refs[a].at[p, 1 - mc], dst_ref=o_refs[a].at[p], send_sem=send_sems.at[a, p],
            recv_sem=recv_sems.at[a, p], device_id=(mx, my, 1 - mc), device_id_type=MESH)
            for a in range(n) for p in range(N_CHIPS)]
        for cp in copies:
            cp.start()
        for cp in copies:
            cp.wait()

    any_spec = pl.BlockSpec(memory_space=pl.ANY)
    return _pcall(
        body, name=name,
        in_specs=[any_spec] * n, out_specs=[any_spec] * n,
        out_shape=[jax.ShapeDtypeStruct((N_CHIPS,) + g.shape[2:], g.dtype) for g in gs],
        scratch_shapes=[pltpu.SemaphoreType.DMA((n, N_CHIPS)), pltpu.SemaphoreType.DMA((n, N_CHIPS))],
    )(*gs)


def _pair_add(g, other, name):
    chips, _, r, c = g.shape
    tr = _shard_row_tile(r)
    core = lax.axis_index("c").astype(jnp.int32).reshape(1)

    def body(core_ref, g_ref, o_ref, h_ref):
        h_ref[...] = (g_ref[...].astype(F32)[0] + o_ref[...].astype(F32)).astype(h_ref.dtype)

    blk = pl.BlockSpec((1, tr, c), lambda p, i, core_ref: (p, i, 0))
    return _pcall(
        body, name=name,
        grid_spec=pltpu.PrefetchScalarGridSpec(
            num_scalar_prefetch=1, grid=(chips, pl.cdiv(r, tr)),
            in_specs=[pl.BlockSpec((1, 1, tr, c), lambda p, i, core_ref: (p, core_ref[0], i, 0)), blk],
            out_specs=blk),
        out_shape=jax.ShapeDtypeStruct((chips, r, c), g.dtype),
        compiler_params=_params("parallel", "parallel"),
    )(core, g, other)


def _chip_exchange(hs, name):
    n = len(hs)

    def body(*refs):
        h_refs, o_refs = refs[:n], refs[n:2 * n]
        send_sems, recv_sems, local_sems = refs[2 * n:]
        mx, my, mc, _ = _my_place()
        my_chip = 2 * mx + my
        chips = _other_chips(mx, my)
        local = [pltpu.make_async_copy(h_refs[a].at[my_chip], o_refs[a].at[my_chip], local_sems.at[a]) for a in range(n)]
        for cp in local:
            cp.start()
        for j, (px, py) in enumerate(chips):
            for a in range(n):
                pltpu.make_async_remote_copy(
                    src_ref=h_refs[a].at[2 * px + py], dst_ref=o_refs[a].at[my_chip], send_sem=send_sems.at[a, j],
                    recv_sem=recv_sems.at[a, j], device_id=(px, py, mc), device_id_type=MESH).start()
        for j, (px, py) in enumerate(chips):
            for a in range(n):
                pltpu.make_async_remote_copy(
                    src_ref=h_refs[a].at[2 * px + py], dst_ref=o_refs[a].at[2 * px + py], send_sem=send_sems.at[a, j],
                    recv_sem=recv_sems.at[a, j], device_id=(px, py, mc), device_id_type=MESH).wait()
        for cp in local:
            cp.wait()

    any_spec = pl.BlockSpec(memory_space=pl.ANY)
    return _pcall(
        body, name=name,
        in_specs=[any_spec] * n, out_specs=[any_spec] * n,
        out_shape=[jax.ShapeDtypeStruct(h.shape, h.dtype) for h in hs],
        scratch_shapes=[pltpu.SemaphoreType.DMA((n, N_CHIPS - 1)), pltpu.SemaphoreType.DMA((n, N_CHIPS - 1)),
                        pltpu.SemaphoreType.DMA((n,))],
    )(*hs)


def _adamw(parts, w, m, v, name):
    r, c = w.shape
    n_parts = parts.shape[0]
    tr = _shard_row_tile(r)
    bc1 = 1.0 - ADAM_B1 ** ADAM_STEP
    bc2 = 1.0 - ADAM_B2 ** ADAM_STEP

    def body(p_ref, w_ref, m_ref, v_ref, g_ref, d_ref, nm_ref, nv_ref):
        g = p_ref[0].astype(F32)
        for j in range(1, n_parts):
            g = g + p_ref[j].astype(F32)
        nm = ADAM_B1 * m_ref[...] + (1.0 - ADAM_B1) * g
        nv = ADAM_B2 * v_ref[...] + (1.0 - ADAM_B2) * (g * g)
        g_ref[...] = g
        nm_ref[...] = nm
        nv_ref[...] = nv
        d_ref[...] = -ADAM_LR * ((nm / bc1) / (jnp.sqrt(nv / bc2) + ADAM_EPS) + ADAM_WD * w_ref[...])

    blk = pl.BlockSpec((tr, c), lambda i: (i, 0))
    return _pcall(
        body, name=name, grid=(pl.cdiv(r, tr),),
        in_specs=[pl.BlockSpec((n_parts, tr, c), lambda i: (0, i, 0)), blk, blk, blk],
        out_specs=[blk] * 4, out_shape=[jax.ShapeDtypeStruct((r, c), F32)] * 4,
        compiler_params=_params("parallel"),
    )(parts, w, m, v)


def _local_step(x, target, norm_w, w_segs, conv_w, a_log, dt_bias, dn_norm_w, w_o_dn, w_o_dil, w_out, final_norm_w):
    s = x.shape[0]
    w_qkv, w_za, w_ba, w_qb, w_kb, w_vb, w_zb, w_ga, w_gb = w_segs
    conv_w8 = jnp.concatenate([conv_w, jnp.zeros((SUBLANES - conv_w.shape[0], QKV_W), F32)], axis=0)
    pad8 = jnp.zeros((1, DN_HEADS), F32)
    alog_row = jnp.concatenate([pad8, a_log, jnp.zeros((1, LANES - 2 * DN_HEADS), F32)], axis=1)
    dtb_row = jnp.concatenate([pad8, dt_bias, jnp.zeros((1, LANES - 2 * DN_HEADS), F32)], axis=1)
    wf_row = final_norm_w.reshape(1, D_MODEL)

    hb, qkv_pre, z_a, ba, z_b = _rms_proj_fwd(x, norm_w, [w_qkv, w_za, w_ba, w_zb], "rms_proj_fwd_a")
    q_b, k_b, v_b, g_a, g_b = _mm_out(hb, [w_qb, w_kb, w_vb, w_ga, w_gb], "proj_fwd_b", w_is_out_by_in=True)

    qn, kn, vn, bg = _dn_prep_fwd(qkv_pre, ba, conv_w8, alog_row, dtb_row)
    u_d, w_d, qd_d, kd_d, aqk_d, dl_d, t2_d = _delta_prep(qn, kn, vn, bg)
    o_a, vnew_d, st_d = _delta_scan_fwd(u_d, w_d, qd_d, kd_d, aqk_d, dl_d)
    on_b, y_a = _dn_out_fwd(o_a, z_a, dn_norm_w, w_o_dn)

    parts, lses = [], []
    for gi in range(N_DIL):
        o_g, l_g = _attn_fwd(q_b, k_b, v_b, gi)
        parts.append(o_g)
        lses.append(l_g)
    lse, o_joint, ob_b, y_b = _attn_out_fwd(parts, lses, z_b, w_o_dil)

    loss8, dwf8, merged_b, dx2_b, dx2, dya_b, dyb_b, dga_b, dgb_b = _merge_out_final(
        g_a, g_b, y_a, y_b, x, target, w_out, wf_row)

    g_w_out = _mm_tn(merged_b, dx2_b, "out_wgrad")
    g_w_o_dn = _mm_tn(on_b, dya_b, "out_dn_wgrad")
    d_o_a, dza_b, ddnw8 = _dn_out_bwd(dya_b, o_a, z_a, dn_norm_w, w_o_dn)

    g_w_o_dil = _mm_tn(ob_b, dyb_b, "out_dil_wgrad")
    d_o, dzb_b, delta = _attn_out_bwd(dyb_b, o_joint, z_b, w_o_dil)
    dqs, dks, dvs = [], [], []
    for gi in range(N_DIL):
        dq_g, dk_g, dv_g = _attn_bwd(q_b, k_b, v_b, d_o, lse, delta, gi)
        dqs.append(dq_g)
        dks.append(dk_g)
        dvs.append(dv_g)

    dvnew_d, dkd_d, ddl_d = _delta_scan_bwd(w_d, qd_d, kd_d, aqk_d, dl_d, vnew_d, st_d, d_o_a)
    dqn, dkn, dvn, dbg = _delta_post_bwd(qn, kn, vn, bg, t2_d, st_d, vnew_d, d_o_a, dvnew_d, dkd_d, ddl_d)
    dc, dba_b, dsmall8 = _dn_prep_bwd(qkv_pre, ba, conv_w8, alog_row, dtb_row, dqn, dkn, dvn, dbg)
    dqkv_b, dconv8 = _conv_bwd(dc, qkv_pre, conv_w8)

    per_group = lambda w: [w[g * DIL_W:(g + 1) * DIL_W] for g in range(N_DIL)]
    dh_b = _mm_in(dqs + dks + dvs + [dga_b, dgb_b],
                  per_group(w_qb) + per_group(w_kb) + per_group(w_vb) + [w_ga, w_gb], "proj_bwd_b", w_is_out_by_in=True)
    dsegs = [dqkv_b, dza_b, dba_b] + dqs + dks + dvs + [dzb_b, dga_b, dgb_b]
    valid_rows = [d.shape[1] for d in dsegs]
    valid_rows[2] = 2 * DN_HEADS
    g_wt = _proj_wgrad_all(dsegs, valid_rows, hb)
    grad_x, dnw8 = _proj_bwd_rms_in([dqkv_b, dza_b, dba_b, dzb_b], [w_qkv, w_za, w_ba, w_zb], dh_b, x, dx2, norm_w)

    small = dict(norm_w=dnw8[0:1], final_norm_w=dwf8[0:1], dn_norm_w=ddnw8[0:1],
                 a_log=dsmall8[0:1, DN_HEADS:2 * DN_HEADS], dt_bias=dsmall8[1:2, DN_HEADS:2 * DN_HEADS])
    return loss8[0:1, 0:1], grad_x, g_wt, dconv8[0:4], g_w_o_dn, g_w_o_dil, g_w_out, small


def _proj_bwd_rms_in(ds, ws, dh_a, x, dx2, norm_w):
    n_seg = len(ds)

    def body(*refs):
        d_refs, w_refs = refs[:n_seg], refs[n_seg:2 * n_seg]
        da_ref, x_ref, dx2_ref, w_ref, dx_ref, dw_ref = refs[2 * n_seg:]
        dx_ref[...] = da_ref[...]
        for d_ref, wt_ref in zip(d_refs, w_refs):
            for c, wd in _col_chunks(d_ref.shape[1], 1024):
                dx_ref[...] += jnp.dot(d_ref[:, c:c + wd], wt_ref[c:c + wd, :], preferred_element_type=F32)
        xv = x_ref[...]
        r = lax.rsqrt(jnp.mean(xv * xv, axis=-1, keepdims=True) + NORM_EPS)
        dhv = dx_ref[...]
        dn = dhv * w_ref[...]
        dx_ref[...] = dx2_ref[...] + r * dn - xv * (r * r * r) * jnp.mean(dn * xv, axis=-1, keepdims=True)
        row = jnp.sum(dhv * xv * r, axis=0, keepdims=True)
        _acc_add(dw_ref, jnp.concatenate([row, jnp.zeros((SUBLANES - 1, row.shape[1]), F32)], axis=0))

    return _rows_call(body, "proj_bwd_b_rms_in", x.shape[0],
                      [(d, "tile") for d in ds] + [(w, "full") for w in ws]
                      + [(dh_a, "tile"), (x, "tile"), (dx2, "tile"), (norm_w, "full")],
                      [(x.shape, F32, "tile"), ((SUBLANES, x.shape[1]), F32, "acc")])


def _split_proj_rows(wt_full):
    offs = [0]
    for n in PROJ_SIZES:
        offs.append(offs[-1] + n)
    seg = lambda a, b: wt_full[offs[a]:offs[b]]
    w_ba = jnp.concatenate([seg(4, 6), jnp.zeros((LANES - 2 * DN_HEADS, wt_full.shape[1]), wt_full.dtype)], axis=0)
    return [seg(0, 3), seg(3, 4), w_ba, seg(6, 7), seg(7, 8), seg(8, 9), seg(9, 10), seg(10, 11), seg(11, 12)]


LOSS_ROW = 5


def _pack_small(norm_w, final_norm_w, dn_norm_w, a_log, dt_bias, loss=None):
    pad = lambda r: jnp.concatenate([r, jnp.zeros((1, D_MODEL - r.shape[1]), F32)], axis=1)
    rows = [pad(norm_w.reshape(1, -1)), pad(final_norm_w.reshape(1, -1)), pad(dn_norm_w.reshape(1, -1)),
            pad(a_log.reshape(1, -1)), pad(dt_bias.reshape(1, -1)),
            pad(jnp.zeros((1, 1), F32) if loss is None else loss.reshape(1, 1)),
            jnp.zeros((SUBLANES - LOSS_ROW - 1, D_MODEL), F32)]
    return jnp.concatenate(rows, axis=0)


def _unpack_small(p):
    return dict(norm_w=p[0:1], final_norm_w=p[1], dn_norm_w=p[2:3, :DN_DK], a_log=p[3:4, :DN_HEADS],
                dt_bias=p[4:5, :DN_HEADS])


def kernel(x, norm_w, w_in, conv_w, a_log, dt_bias, dn_norm_w, w_o_dn, w_o_dil, w_out, final_norm_w, loss_target, m_norm_w, m_w_in, m_conv_w, m_a_log, m_dt_bias, m_dn_norm_w, m_w_o_dn, m_w_o_dil, m_w_out, m_final_norm_w, v_norm_w, v_w_in, v_conv_w, v_a_log, v_dt_bias, v_dn_norm_w, v_w_o_dn, v_w_o_dil, v_w_out, v_final_norm_w):
    shard_w = w_in.shape[2]
    wt, m_wt, v_wt = (jnp.transpose(t[0]) for t in (w_in, m_w_in, v_w_in))
    gathered = _all_gather([wt.astype(MXU), w_o_dn[0].astype(MXU), w_o_dil[0].astype(MXU), w_out[0].astype(MXU),
                            conv_w[0]], "gather_weights")
    w_in_all, w_o_dn_all, w_o_dil_all, w_out_all, conv_all = gathered
    wt_full = w_in_all.reshape(N_DEV * shard_w, D_MODEL)
    w_o_dn_full = w_o_dn_all.reshape(D_MODEL, D_MODEL)
    w_o_dil_full = jnp.transpose(w_o_dil_all, (1, 0, 2)).reshape(DIL_W, D_MODEL)
    w_out_full = w_out_all.reshape(D_MODEL, D_MODEL)
    conv_full = jnp.transpose(conv_all, (1, 0, 2)).reshape(conv_w.shape[1], QKV_W)

    loss11, grad_x, g_wt, g_conv, g_w_o_dn, g_w_o_dil, g_w_out, small = _local_step(
        x[0], loss_target[0], norm_w, _split_proj_rows(wt_full), conv_full, a_log, dt_bias, dn_norm_w,
        w_o_dn_full, w_o_dil_full, w_out_full, final_norm_w)

    col_shards = lambda g, n: jnp.transpose(g.reshape(g.shape[0], N_DEV, n), (1, 0, 2))
    row_shards = lambda g: g.reshape(N_DEV, g.shape[0] // N_DEV, g.shape[1])
    sent = [row_shards(g_wt), row_shards(g_w_o_dn).astype(MXU),
            col_shards(g_w_o_dil, w_o_dil.shape[2]).astype(MXU), row_shards(g_w_out).astype(MXU),
            col_shards(g_conv, conv_w.shape[2])]
    sent = [g8.reshape((N_CHIPS, 2) + g8.shape[1:]) for g8 in sent]
    from_sibling = _pair_exchange(sent, "scatter_pair")
    summed = [_pair_add(g, o, f"pair_add_{i}") for i, (g, o) in enumerate(zip(sent, from_sibling))]
    p_w_in, p_w_o_dn, p_w_o_dil, p_w_out, p_conv = _chip_exchange(summed, "scatter_chips")
    p_small = _all_gather([_pack_small(small["norm_w"], small["final_norm_w"], small["dn_norm_w"], small["a_log"],
                                       small["dt_bias"], loss11)], "gather_small_grads")[0]

    res = {}
    res["w_in"] = [jnp.transpose(t) for t in _adamw(p_w_in, wt, m_wt, v_wt, "adamw_w_in")]
    res["conv_w"] = _adamw(p_conv, conv_w[0], m_conv_w[0], v_conv_w[0], "adamw_conv_w")
    res["w_o_dn"] = _adamw(p_w_o_dn, w_o_dn[0], m_w_o_dn[0], v_w_o_dn[0], "adamw_w_o_dn")
    res["w_o_dil"] = _adamw(p_w_o_dil, w_o_dil[0], m_w_o_dil[0], v_w_o_dil[0], "adamw_w_o_dil")
    res["w_out"] = _adamw(p_w_out, w_out[0], m_w_out[0], v_w_out[0], "adamw_w_out")
    small_res = _adamw(p_small, _pack_small(norm_w, final_norm_w, dn_norm_w, a_log, dt_bias),
                       _pack_small(m_norm_w, m_final_norm_w, m_dn_norm_w, m_a_log, m_dt_bias),
                       _pack_small(v_norm_w, v_final_norm_w, v_dn_norm_w, v_a_log, v_dt_bias), "adamw_small")
    loss = small_res[0][LOSS_ROW, 0]
    small_res = [_unpack_small(t) for t in small_res]

    names = ["norm_w", "w_in", "conv_w", "a_log", "dt_bias", "dn_norm_w", "w_o_dn", "w_o_dil", "w_out", "final_norm_w"]
    outs = [loss, grad_x[None]]
    for kind in range(4):
        for nm in names:
            outs.append(res[nm][kind][None] if nm in res else small_res[kind][nm])
    return tuple(outs)
```

```python
import math

import jax
import jax.numpy as jnp
from jax import lax
from jax.experimental import pallas as pl
from jax.experimental.pallas import tpu as pltpu

F32 = jnp.float32
MXU = jnp.bfloat16
MESH = pl.DeviceIdType.MESH

N_DEV = 8
D_MODEL = 1024
DN_HEADS = 8
DN_DK = 128
DN_CHUNK = 64
N_DIL = 3
DIL_HEADS = 4
DIL_DH = 128
DIL_W = DIL_HEADS * DIL_DH
DIL_GROUPS = ((128, 1), (512, 4), (2048, 16))
ATT_BLOCK = 128
NORM_EPS = 1e-6
QKV_W = 3 * D_MODEL
DILQ_W = N_DIL * DIL_W
PROJ_SIZES = (1024, 1024, 1024, 1024, 8, 8, DILQ_W, DILQ_W, DILQ_W, DIL_W, D_MODEL, D_MODEL)

ADAM_LR = 0.001
ADAM_B1 = 0.9
ADAM_B2 = 0.999
ADAM_EPS = 1e-08
ADAM_WD = 0.01
ADAM_STEP = 10

ROW_TILE = 256
LANES = 128
SUBLANES = 8
VMEM_LIMIT = 48 << 20


def _pcall(body, **kw):
    return pl.pallas_call(body, **kw)


def _params(*sem):
    return pltpu.CompilerParams(dimension_semantics=tuple(sem), vmem_limit_bytes=VMEM_LIMIT)


def _sigmoid(x):
    return 1.0 / (1.0 + jnp.exp(-x))


def _softplus(x):
    return jnp.maximum(x, 0.0) + jnp.log(1.0 + jnp.exp(-jnp.abs(x)))


def _dot(a, b):
    return jnp.dot(a.astype(MXU), b.astype(MXU), preferred_element_type=F32)


def _dot_nt(a, b):
    return lax.dot_general(a.astype(MXU), b.astype(MXU), (((1,), (1,)), ((), ())), preferred_element_type=F32)


def _dot_tn(a, b):
    return lax.dot_general(a.astype(MXU), b.astype(MXU), (((0,), (0,)), ((), ())), preferred_element_type=F32)


def _split3(x):
    hi = x.astype(jnp.bfloat16)
    r1 = x - hi.astype(F32)
    mid = r1.astype(jnp.bfloat16)
    lo = (r1 - mid.astype(F32)).astype(jnp.bfloat16)
    return hi, mid, lo


def _dot01(m01, x):
    m = m01.astype(jnp.bfloat16)
    hi, mid, lo = _split3(x)
    f = lambda p: jnp.dot(m, p, preferred_element_type=F32)
    return f(hi) + (f(mid) + f(lo))


def _rows_call(body, name, n_rows, ins, outs, scratch=(), tm=ROW_TILE):
    steps = n_rows // tm
    per8 = tm // SUBLANES
    last8 = n_rows // SUBLANES - 1
    in_specs = []
    for arr, kind in ins:
        cols = arr.shape[-1]
        if kind == "tile":
            in_specs.append(pl.BlockSpec((tm, cols), lambda i: (i, 0)))
        elif kind == "full":
            in_specs.append(pl.BlockSpec(arr.shape, lambda i, nd=arr.ndim: (0,) * nd))
        elif kind == "prev8":
            in_specs.append(pl.BlockSpec((SUBLANES, cols), lambda i: (jnp.maximum(i * per8 - 1, 0), 0)))
        elif kind == "next8":
            in_specs.append(pl.BlockSpec((SUBLANES, cols), lambda i: (jnp.minimum((i + 1) * per8, last8), 0)))
        else:
            raise ValueError(kind)
    out_specs, out_shape, has_acc = [], [], False
    for shape, dtype, kind in outs:
        out_shape.append(jax.ShapeDtypeStruct(shape, dtype))
        if kind == "tile":
            out_specs.append(pl.BlockSpec((tm, shape[-1]), lambda i: (i, 0)))
        else:
            has_acc = True
            out_specs.append(pl.BlockSpec(shape, lambda i: (0, 0)))
    return _pcall(
        body, name=name, grid=(steps,), in_specs=in_specs, out_specs=out_specs, out_shape=out_shape,
        scratch_shapes=list(scratch),
        compiler_params=_params("arbitrary" if has_acc else "parallel"),
    )(*[a for a, _ in ins])


def _acc_add(ref, value):
    @pl.when(pl.program_id(0) == 0)
    def _():
        ref[...] = jnp.zeros_like(ref)
    ref[...] += value


def _col_chunks(n, width=512):
    return [(c, min(width, n - c)) for c in range(0, n, width)]


NT_DIMS = (((1,), (1,)), ((), ()))
TN_DIMS = (((0,), (0,)), ((), ()))


def _mm_out(a, ws, name, w_is_out_by_in=False, out_dtype=F32, tm=ROW_TILE):
    m, k = a.shape
    ns = [w.shape[0] if w_is_out_by_in else w.shape[1] for w in ws]

    def body(a_ref, *refs):
        av = a_ref[...]
        for w_ref, o_ref, n in zip(refs[:len(ws)], refs[len(ws):], ns):
            for c, wd in _col_chunks(n):
                if w_is_out_by_in:
                    part = lax.dot_general(av, w_ref[c:c + wd, :], NT_DIMS, preferred_element_type=F32)
                else:
                    part = jnp.dot(av, w_ref[:, c:c + wd], preferred_element_type=F32)
                o_ref[:, c:c + wd] = part.astype(o_ref.dtype)

    return _pcall(
        body, name=name, grid=(m // tm,),
        in_specs=[pl.BlockSpec((tm, k), lambda i: (i, 0))] + [pl.BlockSpec(w.shape, lambda i: (0, 0)) for w in ws],
        out_specs=[pl.BlockSpec((tm, n), lambda i: (i, 0)) for n in ns],
        out_shape=[jax.ShapeDtypeStruct((m, n), out_dtype) for n in ns],
        compiler_params=_params("parallel"),
    )(a, *ws)


def _rms_proj_fwd(x, norm_w, wts, name, tm=ROW_TILE):
    m, k = x.shape
    ns = [w.shape[0] for w in wts]

    def body(x_ref, nw_ref, *refs):
        w_refs, h_ref, o_refs = refs[:len(wts)], refs[len(wts)], refs[len(wts) + 1:]
        xv = x_ref[...]
        r = lax.rsqrt(jnp.mean(xv * xv, axis=-1, keepdims=True) + NORM_EPS)
        hv = (xv * r * nw_ref[...]).astype(h_ref.dtype)
        h_ref[...] = hv
        for w_ref, o_ref, n in zip(w_refs, o_refs, ns):
            for c, wd in _col_chunks(n):
                o_ref[:, c:c + wd] = lax.dot_general(hv, w_ref[c:c + wd, :], NT_DIMS, preferred_element_type=F32)

    return _pcall(
        body, name=name, grid=(m // tm,),
        in_specs=[pl.BlockSpec((tm, k), lambda i: (i, 0)), pl.BlockSpec(norm_w.shape, lambda i: (0, 0))]
        + [pl.BlockSpec(w.shape, lambda i: (0, 0)) for w in wts],
        out_specs=[pl.BlockSpec((tm, k), lambda i: (i, 0))] + [pl.BlockSpec((tm, n), lambda i: (i, 0)) for n in ns],
        out_shape=[jax.ShapeDtypeStruct((m, k), MXU)] + [jax.ShapeDtypeStruct((m, n), F32) for n in ns],
        compiler_params=_params("parallel"),
    )(x, norm_w, *wts)


def _mm_in(ds, ws, name, w_is_out_by_in=False, tm=ROW_TILE):
    m = ds[0].shape[0]
    k = ws[0].shape[1] if w_is_out_by_in else ws[0].shape[0]
    ns = [d.shape[1] for d in ds]

    def body(*refs):
        d_refs, w_refs, o_ref = refs[:len(ds)], refs[len(ds):2 * len(ds)], refs[-1]
        first = True
        for d_ref, w_ref, n in zip(d_refs, w_refs, ns):
            for c, wd in _col_chunks(n, 1024):
                if w_is_out_by_in:
                    part = jnp.dot(d_ref[:, c:c + wd], w_ref[c:c + wd, :], preferred_element_type=F32)
                else:
                    part = lax.dot_general(d_ref[:, c:c + wd], w_ref[:, c:c + wd], NT_DIMS, preferred_element_type=F32)
                if first:
                    o_ref[...] = part
                    first = False
                else:
                    o_ref[...] += part

    return _pcall(
        body, name=name, grid=(m // tm,),
        in_specs=[pl.BlockSpec((tm, n), lambda i: (i, 0)) for n in ns] + [pl.BlockSpec(w.shape, lambda i: (0, 0)) for w in ws],
        out_specs=pl.BlockSpec((tm, k), lambda i: (i, 0)),
        out_shape=jax.ShapeDtypeStruct((m, k), F32),
        compiler_params=_params("parallel"),
    )(*ds, *ws)


def _mm_tn(a, d, name):
    m, k = a.shape
    n = d.shape[1]
    tk = 512 if k % 512 == 0 else k

    def body(a_ref, d_ref, o_ref):
        o_ref[...] = lax.dot_general(a_ref[...], d_ref[...], TN_DIMS, preferred_element_type=F32)

    return _pcall(
        body, name=name, grid=(k // tk,),
        in_specs=[pl.BlockSpec((m, tk), lambda p: (0, p)), pl.BlockSpec((m, n), lambda p: (0, 0))],
        out_specs=pl.BlockSpec((tk, n), lambda p: (p, 0)),
        out_shape=jax.ShapeDtypeStruct((k, n), F32),
        compiler_params=_params("parallel"),
    )(a, d)


WGRAD_TILE = 512


def _proj_wgrad_all(dsegs, valid_rows, hb):
    m, k = hb.shape
    n_seg = len(dsegs)
    tiles, row = [], 0
    for si, (d, valid) in enumerate(zip(dsegs, valid_rows)):
        for c in range(0, valid, WGRAD_TILE):
            width = min(WGRAD_TILE, d.shape[1] - c)
            tiles.append((si, c, width, row + c, min(width, valid - c)))
        row += valid
    total_rows = row

    def body(*refs):
        d_refs, hb_ref, o_ref = refs[:n_seg], refs[n_seg], refs[n_seg + 1]
        a_buf, hb_buf, o_buf, load_sems, store_sems, hb_sem = refs[n_seg + 2:]

        def load(t):
            si, c, width, _, _ = tiles[t]
            return pltpu.make_async_copy(d_refs[si].at[:, pl.ds(c, width)], a_buf.at[t % 2, :, pl.ds(0, width)],
                                         load_sems.at[t % 2])

        def stores(t):
            _, _, _, orow, valid = tiles[t]
            return [pltpu.make_async_copy(o_buf.at[t % 2, pl.ds(0, valid), :], o_ref.at[pl.ds(orow, valid), :],
                                          store_sems.at[t % 2])]

        hb_copy = pltpu.make_async_copy(hb_ref, hb_buf, hb_sem)
        hb_copy.start()
        load(0).start()
        hb_copy.wait()
        for t in range(len(tiles)):
            width = tiles[t][2]
            load(t).wait()
            if t + 1 < len(tiles):
                load(t + 1).start()
            if t >= 2:
                for cp in stores(t - 2):
                    cp.wait()
            o_buf[t % 2, 0:width, :] = lax.dot_general(a_buf[t % 2, :, 0:width], hb_buf[...], TN_DIMS,
                                                        preferred_element_type=F32).astype(o_buf.dtype)
            for cp in stores(t):
                cp.start()
        for t in range(max(len(tiles) - 2, 0), len(tiles)):
            for cp in stores(t):
                cp.wait()

    any_spec = pl.BlockSpec(memory_space=pl.ANY)
    return _pcall(
        body, name="proj_wgrad",
        in_specs=[any_spec] * (n_seg + 1), out_specs=any_spec,
        out_shape=jax.ShapeDtypeStruct((total_rows, k), hb.dtype),
        scratch_shapes=[pltpu.VMEM((2, m, WGRAD_TILE), hb.dtype), pltpu.VMEM((m, k), hb.dtype),
                        pltpu.VMEM((2, WGRAD_TILE, k), hb.dtype), pltpu.SemaphoreType.DMA((2,)),
                        pltpu.SemaphoreType.DMA((2,)), pltpu.SemaphoreType.DMA],
        compiler_params=pltpu.CompilerParams(vmem_limit_bytes=VMEM_LIMIT),
    )(*dsegs, hb)


def _conv_taps(ext_ref, cw_ref, cols, tm):
    c = None
    for j in range(4):
        term = cw_ref[3 - j:4 - j, cols] * ext_ref[SUBLANES - j:SUBLANES - j + tm, cols]
        c = term if c is None else c + term
    return c


def _fill_ext(ext_ref, u_ref, halo_ref, first):
    ext_ref[0:SUBLANES, :] = jnp.where(first, 0.0, halo_ref[...])
    ext_ref[SUBLANES:, :] = u_ref[...]


def _dn_prep_fwd(qkv_pre, ba, conv_w8, alog_row, dtb_row):
    s = qkv_pre.shape[0]
    tm = ROW_TILE

    def body(u_ref, halo_ref, cw_ref, ba_ref, al_ref, dtb_ref, q_ref, k_ref, v_ref, bg_ref, ext_ref):
        _fill_ext(ext_ref, u_ref, halo_ref, pl.program_id(0) == 0)
        for h in range(3 * DN_HEADS):
            cols = slice(h * LANES, (h + 1) * LANES)
            c = _conv_taps(ext_ref, cw_ref, cols, tm)
            a = c * _sigmoid(c)
            oc = slice((h % DN_HEADS) * LANES, (h % DN_HEADS + 1) * LANES)
            if h < 2 * DN_HEADS:
                rinv = lax.rsqrt(jnp.sum(a * a, axis=-1, keepdims=True) + NORM_EPS)
                if h < DN_HEADS:
                    q_ref[:, oc] = a * (rinv * DN_DK ** -0.5)
                else:
                    k_ref[:, oc] = a * rinv
            else:
                v_ref[:, oc] = a
        bav = ba_ref[...]
        lane = lax.broadcasted_iota(jnp.int32, bav.shape, 1)
        beta = _sigmoid(bav)
        g = -jnp.exp(al_ref[...]) * _softplus(bav + dtb_ref[...])
        bg_ref[...] = jnp.where(lane < DN_HEADS, beta, jnp.where(lane < 2 * DN_HEADS, g, 0.0))

    return _rows_call(
        body, "dn_prep_fwd", s,
        [(qkv_pre, "tile"), (qkv_pre, "prev8"), (conv_w8, "full"), (ba, "tile"), (alog_row, "full"), (dtb_row, "full")],
        [((s, D_MODEL), F32, "tile")] * 3 + [((s, LANES), F32, "tile")],
        scratch=[pltpu.VMEM((tm + SUBLANES, QKV_W), F32)])


def _dn_prep_bwd(qkv_pre, ba, conv_w8, alog_row, dtb_row, dq, dk, dv, dbg):
    s = qkv_pre.shape[0]
    tm = ROW_TILE

    def body(u_ref, halo_ref, cw_ref, ba_ref, al_ref, dtb_ref, dq_ref, dk_ref, dv_ref, dbg_ref,
             dc_ref, dba_ref, dsmall_ref, ext_ref):
        _fill_ext(ext_ref, u_ref, halo_ref, pl.program_id(0) == 0)
        for h in range(3 * DN_HEADS):
            cols = slice(h * LANES, (h + 1) * LANES)
            oc = slice((h % DN_HEADS) * LANES, (h % DN_HEADS + 1) * LANES)
            c = _conv_taps(ext_ref, cw_ref, cols, tm)
            sg = _sigmoid(c)
            a = c * sg
            if h < 2 * DN_HEADS:
                rinv = lax.rsqrt(jnp.sum(a * a, axis=-1, keepdims=True) + NORM_EPS)
                dy = dq_ref[:, oc] * DN_DK ** -0.5 if h < DN_HEADS else dk_ref[:, oc]
                da = rinv * dy - a * (rinv * rinv * rinv) * jnp.sum(dy * a, axis=-1, keepdims=True)
            else:
                da = dv_ref[:, oc]
            dc_ref[:, cols] = da * (sg * (1.0 + c * (1.0 - sg)))
        bav = ba_ref[...]
        dbgv = dbg_ref[...]
        lane = lax.broadcasted_iota(jnp.int32, bav.shape, 1)
        beta = _sigmoid(bav)
        ea = jnp.exp(al_ref[...])
        z = bav + dtb_ref[...]
        g = -ea * _softplus(z)
        is_b = lane < DN_HEADS
        is_g = jnp.logical_and(lane >= DN_HEADS, lane < 2 * DN_HEADS)
        d_aa = jnp.where(is_g, dbgv * (-ea) * _sigmoid(z), 0.0)
        dba = jnp.where(is_b, dbgv * beta * (1.0 - beta), d_aa)
        dba_ref[...] = dba.astype(dba_ref.dtype)
        r_alog = jnp.sum(jnp.where(is_g, dbgv * g, 0.0), axis=0, keepdims=True)
        r_dtb = jnp.sum(d_aa, axis=0, keepdims=True)
        _acc_add(dsmall_ref, jnp.concatenate([r_alog, r_dtb, jnp.zeros((SUBLANES - 2, LANES), F32)], axis=0))

    return _rows_call(
        body, "dn_prep_bwd", s,
        [(qkv_pre, "tile"), (qkv_pre, "prev8"), (conv_w8, "full"), (ba, "tile"), (alog_row, "full"), (dtb_row, "full"),
         (dq, "tile"), (dk, "tile"), (dv, "tile"), (dbg, "tile")],
        [((s, QKV_W), F32, "tile"), ((s, LANES), MXU, "tile"), ((SUBLANES, LANES), F32, "acc")],
        scratch=[pltpu.VMEM((tm + SUBLANES, QKV_W), F32)])


def _conv_bwd(dc, qkv_pre, conv_w8):
    s = dc.shape[0]
    tm = ROW_TILE
    steps = s // tm

    def body(dc_ref, dnext_ref, u_ref, halo_ref, cw_ref, du_ref, dcw_ref, extd_ref, ext_ref):
        i = pl.program_id(0)
        _fill_ext(ext_ref, u_ref, halo_ref, i == 0)
        extd_ref[0:tm, :] = dc_ref[...]
        extd_ref[tm:, :] = jnp.where(i == steps - 1, 0.0, dnext_ref[...])

        @pl.when(i == 0)
        def _():
            dcw_ref[...] = jnp.zeros_like(dcw_ref)

        for h in range(3 * DN_HEADS):
            cols = slice(h * LANES, (h + 1) * LANES)
            du = None
            for j in range(4):
                term = cw_ref[3 - j:4 - j, cols] * extd_ref[j:j + tm, cols]
                du = term if du is None else du + term
            du_ref[:, cols] = du.astype(du_ref.dtype)
            dcv = dc_ref[:, cols]
            for j in range(4):
                row = jnp.sum(dcv * ext_ref[SUBLANES - j:SUBLANES - j + tm, cols], axis=0, keepdims=True)
                dcw_ref[3 - j:4 - j, cols] += row

    return _rows_call(
        body, "conv_bwd", s,
        [(dc, "tile"), (dc, "next8"), (qkv_pre, "tile"), (qkv_pre, "prev8"), (conv_w8, "full")],
        [((s, QKV_W), MXU, "tile"), ((SUBLANES, QKV_W), F32, "acc")],
        scratch=[pltpu.VMEM((tm + SUBLANES, QKV_W), F32), pltpu.VMEM((tm + SUBLANES, QKV_W), F32)])


def _dn_out_fwd(o, z, dnw_row, w_o_dn):
    def body(o_ref, z_ref, w_ref, wo_ref, on_ref, y_ref):
        for h in range(DN_HEADS):
            cols = slice(h * LANES, (h + 1) * LANES)
            ov = o_ref[:, cols]
            zv = z_ref[:, cols]
            ro = lax.rsqrt(jnp.mean(ov * ov, axis=-1, keepdims=True) + NORM_EPS)
            on_ref[:, cols] = (ov * ro * w_ref[...] * (zv * _sigmoid(zv))).astype(on_ref.dtype)
        y_ref[...] = jnp.dot(on_ref[...], wo_ref[...], preferred_element_type=F32)

    return _rows_call(body, "dn_out_fwd", o.shape[0], [(o, "tile"), (z, "tile"), (dnw_row, "full"), (w_o_dn, "full")],
                      [(o.shape, MXU, "tile"), ((o.shape[0], w_o_dn.shape[1]), F32, "tile")])


def _dn_out_bwd(dy, o, z, dnw_row, w_o_dn):
    def body(dy_ref, o_ref, z_ref, w_ref, wo_ref, do_ref, dz_ref, dw_ref, d_ref):
        d_ref[...] = lax.dot_general(dy_ref[...], wo_ref[...], NT_DIMS, preferred_element_type=F32)
        acc = jnp.zeros((1, LANES), F32)
        for h in range(DN_HEADS):
            cols = slice(h * LANES, (h + 1) * LANES)
            dv, ov, zv = d_ref[:, cols], o_ref[:, cols], z_ref[:, cols]
            sg = _sigmoid(zv)
            sz = zv * sg
            ro = lax.rsqrt(jnp.mean(ov * ov, axis=-1, keepdims=True) + NORM_EPS)
            nv = ov * ro
            dn = dv * w_ref[...] * sz
            acc = acc + jnp.sum(dv * nv * sz, axis=0, keepdims=True)
            dz_ref[:, cols] = (dv * nv * w_ref[...] * (sg * (1.0 + zv * (1.0 - sg)))).astype(dz_ref.dtype)
            do_ref[:, cols] = ro * dn - ov * (ro * ro * ro) * jnp.mean(dn * ov, axis=-1, keepdims=True)
        _acc_add(dw_ref, jnp.concatenate([acc, jnp.zeros((SUBLANES - 1, LANES), F32)], axis=0))

    return _rows_call(body, "dn_out_bwd", o.shape[0],
                      [(dy, "tile"), (o, "tile"), (z, "tile"), (dnw_row, "full"), (w_o_dn, "full")],
                      [(o.shape, F32, "tile"), (o.shape, MXU, "tile"), ((SUBLANES, LANES), F32, "acc")],
                      scratch=[pltpu.VMEM((ROW_TILE, o.shape[1]), F32)])


def _attn_out_fwd(parts, lses, zb, w_o_dil):
    def body(o0, o1, o2, l0, l1, l2, z_ref, wo_ref, lse_ref, o_ref, g_ref, y_ref):
        a, b, c = l0[...], l1[...], l2[...]
        m = jnp.maximum(a, jnp.maximum(b, c))
        ea, eb, ec = jnp.exp(a - m), jnp.exp(b - m), jnp.exp(c - m)
        den = ea + eb + ec
        out = (ea * o0[...] + eb * o1[...] + ec * o2[...]) / den
        lse_ref[...] = m + jnp.log(den)
        o_ref[...] = out
        zv = z_ref[...]
        gated = (out * (zv * _sigmoid(zv))).astype(g_ref.dtype)
        g_ref[...] = gated
        y_ref[...] = jnp.dot(gated, wo_ref[...], preferred_element_type=F32)

    s = zb.shape[0]
    return _rows_call(body, "attn_out_fwd", s,
                      [(p, "tile") for p in parts] + [(l, "tile") for l in lses] + [(zb, "tile"), (w_o_dil, "full")],
                      [((s, DIL_W), F32, "tile"), ((s, DIL_W), F32, "tile"), ((s, DIL_W), MXU, "tile"),
                       ((s, w_o_dil.shape[1]), F32, "tile")])


def _attn_out_bwd(dy, o_joint, zb, w_o_dil):
    def body(dy_ref, o_ref, z_ref, wo_ref, do_ref, dz_ref, dl_ref):
        zv = z_ref[...]
        sg = _sigmoid(zv)
        dv = lax.dot_general(dy_ref[...], wo_ref[...], NT_DIMS, preferred_element_type=F32)
        ov = o_ref[...]
        do = dv * (zv * sg)
        do_ref[...] = do
        dz_ref[...] = (dv * ov * (sg * (1.0 + zv * (1.0 - sg)))).astype(dz_ref.dtype)
        for h in range(DIL_HEADS):
            cols = slice(h * LANES, (h + 1) * LANES)
            dl_ref[:, cols] = jnp.broadcast_to(jnp.sum(do[:, cols] * ov[:, cols], axis=-1, keepdims=True),
                                               (do.shape[0], LANES))

    s = zb.shape[0]
    return _rows_call(body, "attn_out_bwd", s, [(dy, "tile"), (o_joint, "tile"), (zb, "tile"), (w_o_dil, "full")],
                      [((s, DIL_W), F32, "tile"), ((s, DIL_W), MXU, "tile"), ((s, DIL_W), F32, "tile")])


def _merge_out_final(ga, gb, ya, yb, x, target, w_out, wf_row):
    s, dm = x.shape

    def body(ga_ref, gb_ref, ya_ref, yb_ref, x_ref, t_ref, wo_ref, w_ref,
             loss_ref, dw_ref, m_ref, dxb_ref, dx_ref, dya_ref, dyb_ref, dga_ref, dgb_ref):
        sa, sb = _sigmoid(ga_ref[...]), _sigmoid(gb_ref[...])
        ya, yb = ya_ref[...], yb_ref[...]
        merged = (sa * ya + sb * yb).astype(MXU)
        m_ref[...] = merged
        x2 = x_ref[...] + jnp.dot(merged, wo_ref[...], preferred_element_type=F32)
        r = lax.rsqrt(jnp.mean(x2 * x2, axis=-1, keepdims=True) + NORM_EPS)
        w = w_ref[...]
        err = x2 * r * w - t_ref[...]
        tile_loss = 0.5 * jnp.sum(jnp.mean(err * err, axis=-1, keepdims=True), axis=0, keepdims=True)
        _acc_add(loss_ref, jnp.broadcast_to(tile_loss, (SUBLANES, LANES)))
        dy = err * (1.0 / dm)
        row = jnp.sum(dy * x2 * r, axis=0, keepdims=True)
        _acc_add(dw_ref, jnp.concatenate([row, jnp.zeros((SUBLANES - 1, dm), F32)], axis=0))
        dn = dy * w
        dx2 = r * dn - x2 * (r * r * r) * jnp.mean(dn * x2, axis=-1, keepdims=True)
        dx_ref[...] = dx2
        dxb = dx2.astype(MXU)
        dxb_ref[...] = dxb
        dmv = lax.dot_general(dxb, wo_ref[...], NT_DIMS, preferred_element_type=F32)
        dya_ref[...] = (dmv * sa).astype(dya_ref.dtype)
        dyb_ref[...] = (dmv * sb).astype(dyb_ref.dtype)
        dga_ref[...] = (dmv * ya * sa * (1.0 - sa)).astype(dga_ref.dtype)
        dgb_ref[...] = (dmv * yb * sb * (1.0 - sb)).astype(dgb_ref.dtype)

    return _rows_call(body, "merge_out_final", s,
                      [(ga, "tile"), (gb, "tile"), (ya, "tile"), (yb, "tile"), (x, "tile"), (target, "tile"),
                       (w_out, "full"), (wf_row, "full")],
                      [((SUBLANES, LANES), F32, "acc"), ((SUBLANES, dm), F32, "acc"), ((s, dm), MXU, "tile"),
                       ((s, dm), MXU, "tile"), ((s, dm), F32, "tile")] + [((s, dm), MXU, "tile")] * 4)


def _lane_pick(x, idx):
    lane = lax.broadcasted_iota(jnp.int32, x.shape, 1)
    return jnp.sum(jnp.where(lane == idx, x, 0.0), axis=-1, keepdims=True)


PAIR = 2 * DN_CHUNK
SCAN_CHUNKS = 4


def _bmm(a, b):
    return lax.dot_general(a.astype(MXU), b.astype(MXU), (((2,), (1,)), ((0,), (0,))), preferred_element_type=F32)


def _bmm_nt(a, b):
    return lax.dot_general(a.astype(MXU), b.astype(MXU), (((2,), (2,)), ((0,), (0,))), preferred_element_type=F32)


def _bmm_tn(a, b):
    return lax.dot_general(a.astype(MXU), b.astype(MXU), (((1,), (1,)), ((0,), (0,))), preferred_element_type=F32)


def _bmm3(a, b):
    ah = a.astype(jnp.bfloat16)
    al = (a - ah.astype(F32)).astype(jnp.bfloat16)
    bh = b.astype(jnp.bfloat16)
    bl = (b - bh.astype(F32)).astype(jnp.bfloat16)
    f = lambda p, q: lax.dot_general(p, q, (((2,), (1,)), ((0,), (0,))), preferred_element_type=F32)
    return f(ah, bh) + (f(ah, bl) + f(al, bh))


def _pair_masks():
    row = lax.broadcasted_iota(jnp.int32, (PAIR, PAIR), 0)
    col = lax.broadcasted_iota(jnp.int32, (PAIR, PAIR), 1)
    same = (row >= DN_CHUNK) == (col >= DN_CHUNK)
    return dict(causal=same & (row >= col), strict=same & (row > col), upper=same & (row <= col), eye=row == col,
                first=row < DN_CHUNK, row=row, lane=col)


def _pair_decay(bgv, masks):
    gc_all = _dot01(masks["causal"].astype(F32), bgv)
    out = []
    for h in range(DN_HEADS):
        beta = _lane_pick(bgv, h)
        gcb = jnp.broadcast_to(_lane_pick(gc_all, DN_HEADS + h), (PAIR, PAIR))
        gam = jnp.where(masks["causal"], jnp.exp(jnp.minimum(gcb - gcb.T, 0.0)), 0.0)
        gl = jnp.where(masks["first"], gcb[DN_CHUNK - 1:DN_CHUNK, :], gcb[PAIR - 1:PAIR, :])
        out.append((beta, gcb, gam, gl))
    return out


def _pair_inverse(a_strict, eye):
    eye_f = eye.astype(F32)[None]
    m = eye_f + a_strict
    x = eye_f - a_strict
    steps = int(math.log2(DN_CHUNK)) - 1
    for i in range(steps):
        mm = _bmm3 if i == steps - 1 else _bmm
        x = x + mm(x, eye_f - mm(m, x))
    return x


def _head_cols(h):
    return slice(h * LANES, (h + 1) * LANES)


def _delta_prep(q, k, v, bg):
    s = q.shape[0]
    c = DN_CHUNK
    n_chunks = s // c

    def body(q_ref, k_ref, v_ref, bg_ref, u_ref, w_ref, qd_ref, kd_ref, aqk_ref, dl_ref, t2_ref):
        masks = _pair_masks()
        dec = _pair_decay(bg_ref[...], masks)
        kbs, ks, gams, vbs, kbes, qs, qds, kds, dls = ([] for _ in range(9))
        for h in range(DN_HEADS):
            beta, gcb, gam, gl = dec[h]
            qh, kh, vh = q_ref[:, _head_cols(h)], k_ref[:, _head_cols(h)], v_ref[:, _head_cols(h)]
            eg = jnp.exp(gcb)
            kb = kh * beta
            kbs.append(kb); ks.append(kh); gams.append(gam); vbs.append(vh * beta); kbes.append(kb * eg)
            qs.append(qh); qds.append(qh * eg); kds.append(kh * jnp.exp(gl - gcb)); dls.append(jnp.exp(gl))
        st = lambda xs: jnp.stack(xs, axis=0)
        kmat, gam = st(ks), st(gams)
        a = jnp.where(masks["strict"][None], _bmm_nt(st(kbs), kmat) * gam, 0.0)
        t = _pair_inverse(a, masks["eye"])
        u = _bmm(t, st(vbs))
        w = _bmm(t, st(kbes))
        aqk = _bmm_nt(st(qs), kmat) * gam
        t2_ref[0] = t.astype(t2_ref.dtype)
        for half in range(2):
            rows = slice(half * c, (half + 1) * c)
            u_ref[half] = u[:, rows, :]
            w_ref[half] = w[:, rows, :].astype(w_ref.dtype)
            qd_ref[half] = st(qds)[:, rows, :].astype(qd_ref.dtype)
            kd_ref[half] = st(kds)[:, rows, :].astype(kd_ref.dtype)
            aqk_ref[half] = aqk[:, rows, rows].astype(aqk_ref.dtype)
            dl_ref[half] = st(dls)[:, half * c:half * c + SUBLANES, :]

    row_spec = lambda w_: pl.BlockSpec((PAIR, w_), lambda i: (i, 0))
    hm = lambda a_, b_: pl.BlockSpec((2, DN_HEADS, a_, b_), lambda i: (i, 0, 0, 0))
    hm_shape = lambda a_, b_, dt: jax.ShapeDtypeStruct((n_chunks, DN_HEADS, a_, b_), dt)
    return _pcall(
        body, name="delta_prep", grid=(n_chunks // 2,),
        in_specs=[row_spec(D_MODEL)] * 3 + [row_spec(LANES)],
        out_specs=[hm(c, LANES)] * 4 + [hm(c, c), hm(SUBLANES, LANES),
                   pl.BlockSpec((1, DN_HEADS, PAIR, PAIR), lambda i: (i, 0, 0, 0))],
        out_shape=[hm_shape(c, LANES, F32), hm_shape(c, LANES, MXU), hm_shape(c, LANES, MXU), hm_shape(c, LANES, MXU),
                   hm_shape(c, c, MXU), hm_shape(SUBLANES, LANES, F32),
                   jax.ShapeDtypeStruct((n_chunks // 2, DN_HEADS, PAIR, PAIR), MXU)],
        compiler_params=_params("parallel"),
    )(q, k, v, bg)


def _delta_scan_fwd(u, w, qd, kd, aqk, dl):
    n_chunks = u.shape[0]
    c = DN_CHUNK
    g_n = SCAN_CHUNKS

    def body(u_ref, w_ref, qd_ref, kd_ref, aqk_ref, dl_ref, o_ref, vnew_ref, st_ref, state):
        @pl.when(pl.program_id(0) == 0)
        def _():
            state[...] = jnp.zeros_like(state)

        for g in range(g_n):
            sv = state[...]
            sb = sv.astype(MXU)
            vnew = u_ref[g] - _bmm(w_ref[g], sb)
            o = _bmm(qd_ref[g], sb) + _bmm(aqk_ref[g], vnew)
            state[...] = sv * dl_ref[g][:, 0:1, :] + _bmm_tn(kd_ref[g], vnew)
            vnew_ref[g] = vnew.astype(vnew_ref.dtype)
            st_ref[g] = sb
            for h in range(DN_HEADS):
                o_ref[g * c:(g + 1) * c, _head_cols(h)] = o[h]

    hm = lambda a_, b_: pl.BlockSpec((g_n, DN_HEADS, a_, b_), lambda i: (i, 0, 0, 0))
    return _pcall(
        body, name="delta_scan_fwd", grid=(n_chunks // g_n,),
        in_specs=[hm(c, LANES)] * 4 + [hm(c, c), hm(SUBLANES, LANES)],
        out_specs=[pl.BlockSpec((g_n * c, D_MODEL), lambda i: (i, 0)), hm(c, LANES), hm(DN_DK, DN_DK)],
        out_shape=[jax.ShapeDtypeStruct((n_chunks * c, D_MODEL), F32),
                   jax.ShapeDtypeStruct((n_chunks, DN_HEADS, c, LANES), MXU),
                   jax.ShapeDtypeStruct((n_chunks, DN_HEADS, DN_DK, DN_DK), MXU)],
        scratch_shapes=[pltpu.VMEM((DN_HEADS, DN_DK, DN_DK), F32)],
        compiler_params=_params("arbitrary"),
    )(u, w, qd, kd, aqk, dl)


def _delta_scan_bwd(w, qd, kd, aqk, dl, vnew, st, do):
    n_chunks = w.shape[0]
    c = DN_CHUNK
    g_n = SCAN_CHUNKS
    steps = n_chunks // g_n

    def body(w_ref, qd_ref, kd_ref, aqk_ref, dl_ref, vnew_ref, st_ref, do_ref, dvnew_ref, dkd_ref, ddl_ref, dstate):
        @pl.when(pl.program_id(0) == 0)
        def _():
            dstate[...] = jnp.zeros_like(dstate)

        for g in reversed(range(g_n)):
            ds = dstate[...]
            dsb = ds.astype(MXU)
            doh = jnp.stack([do_ref[g * c:(g + 1) * c, _head_cols(h)] for h in range(DN_HEADS)], axis=0)
            dvnew = _bmm_tn(aqk_ref[g], doh) + _bmm(kd_ref[g], dsb)
            dkd_ref[g] = _bmm_nt(vnew_ref[g], dsb)
            ddl = jnp.sum(jnp.sum(st_ref[g].astype(F32) * ds, axis=2, keepdims=True), axis=1, keepdims=True)
            ddl_ref[g] = jnp.broadcast_to(ddl, (DN_HEADS, SUBLANES, LANES))
            dstate[...] = ds * dl_ref[g][:, 0:1, :] + _bmm_tn(qd_ref[g], doh) - _bmm_tn(w_ref[g], dvnew)
            dvnew_ref[g] = dvnew.astype(dvnew_ref.dtype)

    rev = lambda i: steps - 1 - i
    hm = lambda a_, b_: pl.BlockSpec((g_n, DN_HEADS, a_, b_), lambda i: (rev(i), 0, 0, 0))
    return _pcall(
        body, name="delta_scan_bwd", grid=(steps,),
        in_specs=[hm(c, LANES)] * 3 + [hm(c, c), hm(SUBLANES, LANES), hm(c, LANES), hm(DN_DK, DN_DK),
                  pl.BlockSpec((g_n * c, D_MODEL), lambda i: (rev(i), 0))],
        out_specs=[hm(c, LANES), hm(c, LANES), hm(SUBLANES, LANES)],
        out_shape=[jax.ShapeDtypeStruct((n_chunks, DN_HEADS, c, LANES), MXU),
                   jax.ShapeDtypeStruct((n_chunks, DN_HEADS, c, LANES), F32),
                   jax.ShapeDtypeStruct((n_chunks, DN_HEADS, SUBLANES, LANES), F32)],
        scratch_shapes=[pltpu.VMEM((DN_HEADS, DN_DK, DN_DK), F32)],
        compiler_params=_params("arbitrary"),
    )(w, qd, kd, aqk, dl, vnew, st, do)


def _delta_post_bwd(q, k, v, bg, t2, st, vnew, do, dvnew, dkd, ddl):
    s = q.shape[0]
    c = DN_CHUNK

    def body(q_ref, k_ref, v_ref, bg_ref, t2_ref, st_ref, vnew_ref, do_ref, dvnew_ref, dkd_ref, ddl_ref,
             dq_ref, dk_ref, dv_ref, dbg_ref):
        masks = _pair_masks()
        first = masks["first"][None]
        dec = _pair_decay(bg_ref[...], masks)
        st_ = lambda xs: jnp.stack(xs, axis=0)
        heads = range(DN_HEADS)
        qm_, km_, vm_, dom = (st_([r[:, _head_cols(h)] for h in heads]) for r in (q_ref, k_ref, v_ref, do_ref))
        beta = st_([dec[h][0] for h in heads])
        gcb = st_([dec[h][1] for h in heads])
        gam = st_([dec[h][2] for h in heads])
        gl = st_([dec[h][3] for h in heads])
        pair = lambda ref: jnp.concatenate([ref[0], ref[1]], axis=1)
        vnew2, dvnew2, dkd2 = pair(vnew_ref), pair(dvnew_ref), pair(dkd_ref)
        halves = lambda x: (x[:, :c, :], x[:, c:, :])
        by_state = lambda x: jnp.concatenate([_bmm_nt(xh, st_ref[i]) for i, xh in enumerate(halves(x))], axis=1)
        dqd = by_state(dom)
        dw = -by_state(dvnew2)
        ddl2 = jnp.where(first, ddl_ref[0][:, 0:1, :], ddl_ref[1][:, 0:1, :])

        eg = jnp.exp(gcb)
        egl = jnp.exp(gl - gcb)
        dl = jnp.exp(gl)
        kb = km_ * beta
        kk = _bmm_nt(kb, km_)
        a = jnp.where(masks["strict"][None], kk * gam, 0.0)
        t = t2_ref[0]
        vb = vm_ * beta
        kbe = kb * eg
        u = _bmm(t, vb)
        w = _bmm(t, kbe)
        aqk = _bmm_nt(qm_, km_) * gam
        qd = qm_ * eg
        kd = km_ * egl

        daqk = jnp.where(masks["causal"][None], _bmm_nt(dom, vnew2), 0.0)
        dvb = _bmm_tn(t, dvnew2)
        dkbe = _bmm_tn(t, dw)
        da = jnp.where(masks["strict"][None], -(_bmm_nt(dvb, u) + _bmm_nt(dkbe, w)), 0.0)
        pm = da * gam
        qmm = daqk * gam
        dkb = _bmm(pm, km_) + dkbe * eg
        dkh = _bmm_tn(pm, kb) + _bmm_tn(qmm, qm_) + dkd2 * egl + dkb * beta
        dqh = _bmm(qmm, km_) + dqd * eg
        xm = da * a + daqk * aqk
        ones = jnp.ones((DN_HEADS, PAIR, LANES), F32)
        hi, mid, lo = _split3(xm)
        colsum = _bmm_tn(hi, ones) + (_bmm_tn(mid, ones) + _bmm_tn(lo, ones))
        tmp = jnp.sum(dkd2 * kd, axis=-1, keepdims=True)
        dgc = (jnp.sum(xm, axis=-1, keepdims=True) - colsum + jnp.sum(dkbe * kbe, axis=-1, keepdims=True)
               + jnp.sum(dqd * qd, axis=-1, keepdims=True) - tmp)
        sum0 = jnp.sum(jnp.where(first, tmp, 0.0), axis=1, keepdims=True)
        sum1 = jnp.sum(jnp.where(first, 0.0, tmp), axis=1, keepdims=True)
        dgl = jnp.where(first, sum0, sum1) + ddl2 * dl
        last = (masks["row"] == c - 1) | (masks["row"] == PAIR - 1)
        dgc = dgc + jnp.where(last[None], dgl, 0.0)
        dbeta = jnp.sum(dvb * vm_, axis=-1, keepdims=True) + jnp.sum(dkb * km_, axis=-1, keepdims=True)
        dvh = dvb * beta

        lane = masks["lane"]
        dgc_lanes = jnp.zeros((PAIR, LANES), F32)
        dbg = jnp.zeros((PAIR, LANES), F32)
        for h in heads:
            dq_ref[:, _head_cols(h)] = dqh[h]
            dk_ref[:, _head_cols(h)] = dkh[h]
            dv_ref[:, _head_cols(h)] = dvh[h]
            dgc_lanes = dgc_lanes + jnp.where(lane == DN_HEADS + h, dgc[h], 0.0)
            dbg = dbg + jnp.where(lane == h, dbeta[h], 0.0)
        dbg_ref[...] = dbg + _dot01(masks["upper"].astype(F32), dgc_lanes)

    n_pairs = s // PAIR
    row_spec = lambda w_: pl.BlockSpec((PAIR, w_), lambda i: (i, 0))
    hm = lambda a_, b_: pl.BlockSpec((2, DN_HEADS, a_, b_), lambda i: (i, 0, 0, 0))
    return _pcall(
        body, name="delta_post_bwd", grid=(n_pairs,),
        in_specs=[row_spec(D_MODEL)] * 3 + [row_spec(LANES), pl.BlockSpec((1, DN_HEADS, PAIR, PAIR), lambda i: (i, 0, 0, 0)),
                  hm(DN_DK, DN_DK), hm(c, LANES), row_spec(D_MODEL), hm(c, LANES), hm(c, LANES), hm(SUBLANES, LANES)],
        out_specs=[row_spec(D_MODEL)] * 3 + [row_spec(LANES)],
        out_shape=[jax.ShapeDtypeStruct((s, D_MODEL), F32)] * 3 + [jax.ShapeDtypeStruct((s, LANES), F32)],
        compiler_params=_params("parallel"),
    )(q, k, v, bg, t2, st, vnew, do, dvnew, dkd, ddl)


def _alibi_slope(group, head):
    n = N_DIL * DIL_HEADS
    return float(2.0 ** (-8.0 * (group * DIL_HEADS + head + 1) / n))


def _attn_plan(s, group):
    window, dil = DIL_GROUPS[group]
    assert window // dil == ATT_BLOCK
    assert (s // dil) % ATT_BLOCK == 0, "sub-sequence length must be a whole number of attention blocks"
    return dil, s // dil // ATT_BLOCK, (DIL_HEADS if dil == 1 else 1)


def _attn_specs(group, dil, nb, hp):
    rows = ATT_BLOCK * dil

    def spec(col0, shift):
        if shift < 0:
            f = lambda hb, n: (jnp.maximum(n - 1, 0), col0 + hb)
        elif shift > 0:
            f = lambda hb, n: (jnp.minimum(n + 1, nb - 1), col0 + hb)
        else:
            f = lambda hb, n: (jnp.minimum(n, nb - 1), col0 + hb)
        return pl.BlockSpec((rows, hp * LANES), f)

    return (lambda shift: spec(group * (DIL_HEADS // hp), shift)), (lambda shift: spec(0, shift))


def _sub_rows(ref, r, dil, cols):
    return ref[:, cols] if dil == 1 else ref[pl.ds(r, ATT_BLOCK, stride=dil), cols]


def _set_sub_rows(ref, r, dil, cols, value):
    if dil == 1:
        ref[:, cols] = value
    else:
        ref[pl.ds(r, ATT_BLOCK, stride=dil), cols] = value


def _step_slope(group, hp, hh):
    if hp == DIL_HEADS:
        return _alibi_slope(group, hh)
    hb = pl.program_id(0)
    slope = _alibi_slope(group, DIL_HEADS - 1)
    for h in reversed(range(DIL_HEADS - 1)):
        slope = jnp.where(hb == h, _alibi_slope(group, h), slope)
    return slope


def _attn_items(hp, dil):
    return [(hh, r) for hh in range(hp) for r in range(dil)]


def _attn_stack(ref, items, dil, dtype=MXU):
    return jnp.stack([_sub_rows(ref, r, dil, _head_cols(hh)).astype(dtype) for hh, r in items], axis=0)


def _attn_slopes(group, hp, items):
    if hp == 1:
        return _step_slope(group, hp, 0)
    return jnp.stack([jnp.full((1, 1), _alibi_slope(group, hh), F32) for hh, _ in items], axis=0)


def _window_bias(dil, n):
    a = lax.broadcasted_iota(jnp.int32, (ATT_BLOCK, 2 * ATT_BLOCK), 0)
    b = lax.broadcasted_iota(jnp.int32, (ATT_BLOCK, 2 * ATT_BLOCK), 1)
    dist = ATT_BLOCK + a - b
    valid = (dist >= 0) & (dist <= ATT_BLOCK) & ((b >= ATT_BLOCK) | (n > 0))
    return (dist * dil).astype(F32), valid


def _attn_fwd(qb, kb, vb, group):
    s = qb.shape[0]
    dil, nb, hp = _attn_plan(s, group)
    qkv, per_head = _attn_specs(group, dil, nb, hp)

    def body(q_ref, kp_ref, kc_ref, vp_ref, vc_ref, o_ref, lse_ref):
        n = pl.program_id(1)
        distd, valid = _window_bias(dil, n)
        items = _attn_items(hp, dil)
        sub = lambda ref: _attn_stack(ref, items, dil)
        kk = jnp.concatenate([sub(kp_ref), sub(kc_ref)], axis=1)
        vv = jnp.concatenate([sub(vp_ref), sub(vc_ref)], axis=1)
        sc = _bmm_nt(sub(q_ref), kk) * DIL_DH ** -0.5 - _attn_slopes(group, hp, items) * distd
        sc = jnp.where(valid, sc, -1e30)
        mx = jnp.max(sc, axis=-1, keepdims=True)
        p = jnp.where(valid, jnp.exp(sc - mx), 0.0)
        den = jnp.sum(p, axis=-1, keepdims=True)
        out = _bmm(p, vv) / den
        lse = mx + jnp.log(den)
        for b, (hh, r) in enumerate(items):
            _set_sub_rows(o_ref, r, dil, _head_cols(hh), out[b])
            _set_sub_rows(lse_ref, r, dil, _head_cols(hh), jnp.broadcast_to(lse[b], (ATT_BLOCK, LANES)))

    return _pcall(
        body, name=f"attn_fwd_g{group}", grid=(DIL_HEADS // hp, nb),
        in_specs=[qkv(0), qkv(-1), qkv(0), qkv(-1), qkv(0)], out_specs=[per_head(0)] * 2,
        out_shape=[jax.ShapeDtypeStruct((s, DIL_W), F32)] * 2,
        compiler_params=_params("parallel", "parallel"),
    )(qb, kb, kb, vb, vb)


def _attn_bwd(qb, kb, vb, d_o, lse, delta, group):
    s = qb.shape[0]
    dil, nb, hp = _attn_plan(s, group)
    qkv, per_head = _attn_specs(group, dil, nb, hp)
    scale = DIL_DH ** -0.5

    def body(q_ref, kp_ref, kc_ref, vp_ref, vc_ref, do_ref, l_ref, dl_ref, dq_ref, dk_ref, dv_ref,
             dq_acc, dk_done, dv_done, dk_carry, dv_carry):
        n = pl.program_id(1)
        items = _attn_items(hp, dil)
        slopes = _attn_slopes(group, hp, items)

        @pl.when(n == 0)
        def _():
            dk_carry[...] = jnp.zeros_like(dk_carry)
            dv_carry[...] = jnp.zeros_like(dv_carry)

        @pl.when(n < nb)
        def _():
            distd, valid = _window_bias(dil, n)
            sub = lambda ref, dtype=MXU: _attn_stack(ref, items, dil, dtype)
            qc, do = sub(q_ref), sub(do_ref)
            kk = jnp.concatenate([sub(kp_ref), sub(kc_ref)], axis=1)
            vv = jnp.concatenate([sub(vp_ref), sub(vc_ref)], axis=1)
            sc = _bmm_nt(qc, kk) * scale - slopes * distd
            p = jnp.where(valid, jnp.exp(jnp.minimum(sc - jnp.concatenate([sub(l_ref, F32)] * 2, axis=2), 0.0)), 0.0)
            dsc = p * (_bmm_nt(do, vv) - jnp.concatenate([sub(dl_ref, F32)] * 2, axis=2))
            dq = _bmm(dsc, kk) * scale
            dkk = _bmm_tn(dsc, qc) * scale
            dvv = _bmm_tn(p, do)
            for b, (hh, r) in enumerate(items):
                cols = _head_cols(hh)
                _set_sub_rows(dq_acc, r, dil, cols, dq[b])
                _set_sub_rows(dk_done, r, dil, cols, _sub_rows(dk_carry, r, dil, cols) + dkk[b, :ATT_BLOCK])
                _set_sub_rows(dv_done, r, dil, cols, _sub_rows(dv_carry, r, dil, cols) + dvv[b, :ATT_BLOCK])
                _set_sub_rows(dk_carry, r, dil, cols, dkk[b, ATT_BLOCK:])
                _set_sub_rows(dv_carry, r, dil, cols, dvv[b, ATT_BLOCK:])
            dq_ref[...] = dq_acc[...].astype(dq_ref.dtype)
            dk_ref[...] = dk_done[...].astype(dk_ref.dtype)
            dv_ref[...] = dv_done[...].astype(dv_ref.dtype)

        @pl.when(n == nb)
        def _():
            dk_ref[...] = dk_carry[...].astype(dk_ref.dtype)
            dv_ref[...] = dv_carry[...].astype(dv_ref.dtype)

    return _pcall(
        body, name=f"attn_bwd_g{group}", grid=(DIL_HEADS // hp, nb + 1),
        in_specs=[qkv(0), qkv(-1), qkv(0), qkv(-1), qkv(0)] + [per_head(0)] * 3,
        out_specs=[per_head(0), per_head(-1), per_head(-1)],
        out_shape=[jax.ShapeDtypeStruct((s, DIL_W), MXU)] * 3,
        scratch_shapes=[pltpu.VMEM((ATT_BLOCK * dil, hp * LANES), F32)] * 5,
        compiler_params=_params("parallel", "arbitrary"),
    )(qb, kb, kb, vb, vb, d_o, lse, delta)


def _my_place():
    mx, my, mc = lax.axis_index("x"), lax.axis_index("y"), lax.axis_index("c")
    return mx, my, mc, 4 * mx + 2 * my + mc


N_CHIPS = 4


def _shard_row_tile(r):
    if r <= 512:
        return r
    return 128 if r % 128 == 0 else 480


def _other_chips(mx, my):
    return [(1 - mx, my), (mx, 1 - my), (1 - mx, 1 - my)]


def _all_gather(xs, name):
    n = len(xs)
    halved = [x.shape[1] % (2 * LANES) == 0 and x.size * x.dtype.itemsize >= (1 << 20) for x in xs]
    n_sems = 8

    def body(*refs):
        x_refs, o_refs = refs[:n], refs[n:2 * n]
        send_sems, recv_sems, local_sems = refs[2 * n:]
        mx, my, mc, me = _my_place()
        sibling, sibling_id = (mx, my, 1 - mc), 4 * mx + 2 * my + (1 - mc)
        x_nbr, y_nbr, diag = _other_chips(mx, my)
        slot_of = lambda chip, c: 4 * chip[0] + 2 * chip[1] + c

        def part(ref, a, half):
            if not halved[a]:
                return ref
            width = xs[a].shape[1] // 2
            return ref.at[:, pl.ds(half * width, width)]

        def copy(a, k, dst, to, src=None):
            return pltpu.make_async_remote_copy(
                src_ref=dst if src is None else src, dst_ref=dst, send_sem=send_sems.at[a, k],
                recv_sem=recv_sems.at[a, k], device_id=to, device_id_type=MESH)

        local = [pltpu.make_async_copy(x_refs[a], o_refs[a].at[me], local_sems.at[a]) for a in range(n)]
        for cp in local:
            cp.start()
        sends = []
        for a in range(n):
            mine = o_refs[a].at[me]
            sends += [copy(a, 0, mine, sibling, src=x_refs[a]), copy(a, 1, mine, (*x_nbr, mc), src=x_refs[a]),
                      copy(a, 2, mine, (*y_nbr, mc), src=x_refs[a])]
        for cp in sends:
            cp.start()
        for a in range(n):
            blk = o_refs[a].at[slot_of(x_nbr, mc)]
            copy(a, 1, blk, (*x_nbr, mc)).wait_recv()
            sends += [copy(a, 3, blk, sibling), copy(a, 5, part(blk, a, 0), (*y_nbr, mc))]
            sends[-2].start()
            sends[-1].start()
        for a in range(n):
            blk = o_refs[a].at[slot_of(y_nbr, mc)]
            copy(a, 2, blk, (*y_nbr, mc)).wait_recv()
            sends.append(copy(a, 4, blk, sibling))
            sends[-1].start()
            if halved[a]:
                sends.append(copy(a, 6, part(blk, a, 1), (*x_nbr, mc)))
                sends[-1].start()
        for a in range(n):
            blk = o_refs[a].at[slot_of(diag, mc)]
            copy(a, 5, part(blk, a, 0), (*y_nbr, mc)).wait_recv()
            if halved[a]:
                copy(a, 6, part(blk, a, 1), (*x_nbr, mc)).wait_recv()
            sends.append(copy(a, 7, blk, sibling))
            sends[-1].start()
        for a in range(n):
            copy(a, 0, o_refs[a].at[sibling_id], sibling).wait_recv()
            for k, chip in ((3, x_nbr), (4, y_nbr), (7, diag)):
                copy(a, k, o_refs[a].at[slot_of(chip, 1 - mc)], sibling).wait_recv()
        for cp in sends:
            cp.wait_send()
        for cp in local:
            cp.wait()

    any_spec = pl.BlockSpec(memory_space=pl.ANY)
    return _pcall(
        body, name=name,
        in_specs=[any_spec] * n, out_specs=[any_spec] * n,
        out_shape=[jax.ShapeDtypeStruct((N_DEV,) + x.shape, x.dtype) for x in xs],
        scratch_shapes=[pltpu.SemaphoreType.DMA((n, n_sems)), pltpu.SemaphoreType.DMA((n, n_sems)),
                        pltpu.SemaphoreType.DMA((n,))],
    )(*xs)


def _pair_exchange(gs, name):
    n = len(gs)

    def body(*refs):
        g_refs, o_refs = refs[:n], refs[n:2 * n]
        send_sems, recv_sems = refs[2 * n:]
        mx, my, mc, _ = _my_place()
        copies = [pltpu.make_async_remote_copy(
            src_ref=g_refs[a].at[p, 1 - mc], dst_ref=o_refs[a].at[p], send_sem=send_sems.at[a, p],
            recv_sem=recv_sems.at[a, p], device_id=(mx, my, 1 - mc), device_id_type=MESH)
            for a in range(n) for p in range(N_CHIPS)]
        for cp in copies:
            cp.start()
        for cp in copies:
            cp.wait()

    any_spec = pl.BlockSpec(memory_space=pl.ANY)
    return _pcall(
        body, name=name,
        in_specs=[any_spec] * n, out_specs=[any_spec] * n,
        out_shape=[jax.ShapeDtypeStruct((N_CHIPS,) + g.shape[2:], g.dtype) for g in gs],
        scratch_shapes=[pltpu.SemaphoreType.DMA((n, N_CHIPS)), pltpu.SemaphoreType.DMA((n, N_CHIPS))],
    )(*gs)


def _pair_add(g, other, name):
    chips, _, r, c = g.shape
    tr = _shard_row_tile(r)
    core = lax.axis_index("c").astype(jnp.int32).reshape(1)

    def body(core_ref, g_ref, o_ref, h_ref):
        h_ref[...] = (g_ref[...].astype(F32)[0] + o_ref[...].astype(F32)).astype(h_ref.dtype)

    blk = pl.BlockSpec((1, tr, c), lambda p, i, core_ref: (p, i, 0))
    return _pcall(
        body, name=name,
        grid_spec=pltpu.PrefetchScalarGridSpec(
            num_scalar_prefetch=1, grid=(chips, pl.cdiv(r, tr)),
            in_specs=[pl.BlockSpec((1, 1, tr, c), lambda p, i, core_ref: (p, core_ref[0], i, 0)), blk],
            out_specs=blk),
        out_shape=jax.ShapeDtypeStruct((chips, r, c), g.dtype),
        compiler_params=_params("parallel", "parallel"),
    )(core, g, other)


def _chip_exchange(hs, name):
    n = len(hs)

    def body(*refs):
        h_refs, o_refs = refs[:n], refs[n:2 * n]
        send_sems, recv_sems, local_sems = refs[2 * n:]
        mx, my, mc, _ = _my_place()
        my_chip = 2 * mx + my
        chips = _other_chips(mx, my)
        local = [pltpu.make_async_copy(h_refs[a].at[my_chip], o_refs[a].at[my_chip], local_sems.at[a]) for a in range(n)]
        for cp in local:
            cp.start()
        for j, (px, py) in enumerate(chips):
            for a in range(n):
                pltpu.make_async_remote_copy(
                    src_ref=h_refs[a].at[2 * px + py], dst_ref=o_refs[a].at[my_chip], send_sem=send_sems.at[a, j],
                    recv_sem=recv_sems.at[a, j], device_id=(px, py, mc), device_id_type=MESH).start()
        for j, (px, py) in enumerate(chips):
            for a in range(n):
                pltpu.make_async_remote_copy(
                    src_ref=h_refs[a].at[2 * px + py], dst_ref=o_refs[a].at[2 * px + py], send_sem=send_sems.at[a, j],
                    recv_sem=recv_sems.at[a, j], device_id=(px, py, mc), device_id_type=MESH).wait()
        for cp in local:
            cp.wait()

    any_spec = pl.BlockSpec(memory_space=pl.ANY)
    return _pcall(
        body, name=name,
        in_specs=[any_spec] * n, out_specs=[any_spec] * n,
        out_shape=[jax.ShapeDtypeStruct(h.shape, h.dtype) for h in hs],
        scratch_shapes=[pltpu.SemaphoreType.DMA((n, N_CHIPS - 1)), pltpu.SemaphoreType.DMA((n, N_CHIPS - 1)),
                        pltpu.SemaphoreType.DMA((n,))],
    )(*hs)


def _adamw(parts, w, m, v, name):
    r, c = w.shape
    n_parts = parts.shape[0]
    tr = _shard_row_tile(r)
    bc1 = 1.0 - ADAM_B1 ** ADAM_STEP
    bc2 = 1.0 - ADAM_B2 ** ADAM_STEP

    def body(p_ref, w_ref, m_ref, v_ref, g_ref, d_ref, nm_ref, nv_ref):
        g = p_ref[0].astype(F32)
        for j in range(1, n_parts):
            g = g + p_ref[j].astype(F32)
        nm = ADAM_B1 * m_ref[...] + (1.0 - ADAM_B1) * g
        nv = ADAM_B2 * v_ref[...] + (1.0 - ADAM_B2) * (g * g)
        g_ref[...] = g
        nm_ref[...] = nm
        nv_ref[...] = nv
        d_ref[...] = -ADAM_LR * ((nm / bc1) / (jnp.sqrt(nv / bc2) + ADAM_EPS) + ADAM_WD * w_ref[...])

    blk = pl.BlockSpec((tr, c), lambda i: (i, 0))
    return _pcall(
        body, name=name, grid=(pl.cdiv(r, tr),),
        in_specs=[pl.BlockSpec((n_parts, tr, c), lambda i: (0, i, 0)), blk, blk, blk],
        out_specs=[blk] * 4, out_shape=[jax.ShapeDtypeStruct((r, c), F32)] * 4,
        compiler_params=_params("parallel"),
    )(parts, w, m, v)


def _local_step(x, target, norm_w, w_segs, conv_w, a_log, dt_bias, dn_norm_w, w_o_dn, w_o_dil, w_out, final_norm_w):
    s = x.shape[0]
    w_qkv, w_za, w_ba, w_qb, w_kb, w_vb, w_zb, w_ga, w_gb = w_segs
    conv_w8 = jnp.concatenate([conv_w, jnp.zeros((SUBLANES - conv_w.shape[0], QKV_W), F32)], axis=0)
    pad8 = jnp.zeros((1, DN_HEADS), F32)
    alog_row = jnp.concatenate([pad8, a_log, jnp.zeros((1, LANES - 2 * DN_HEADS), F32)], axis=1)
    dtb_row = jnp.concatenate([pad8, dt_bias, jnp.zeros((1, LANES - 2 * DN_HEADS), F32)], axis=1)
    wf_row = final_norm_w.reshape(1, D_MODEL)

    hb, qkv_pre, z_a, ba, z_b = _rms_proj_fwd(x, norm_w, [w_qkv, w_za, w_ba, w_zb], "rms_proj_fwd_a")
    q_b, k_b, v_b, g_a, g_b = _mm_out(hb, [w_qb, w_kb, w_vb, w_ga, w_gb], "proj_fwd_b", w_is_out_by_in=True)

    qn, kn, vn, bg = _dn_prep_fwd(qkv_pre, ba, conv_w8, alog_row, dtb_row)
    u_d, w_d, qd_d, kd_d, aqk_d, dl_d, t2_d = _delta_prep(qn, kn, vn, bg)
    o_a, vnew_d, st_d = _delta_scan_fwd(u_d, w_d, qd_d, kd_d, aqk_d, dl_d)
    on_b, y_a = _dn_out_fwd(o_a, z_a, dn_norm_w, w_o_dn)

    parts, lses = [], []
    for gi in range(N_DIL):
        o_g, l_g = _attn_fwd(q_b, k_b, v_b, gi)
        parts.append(o_g)
        lses.append(l_g)
    lse, o_joint, ob_b, y_b = _attn_out_fwd(parts, lses, z_b, w_o_dil)

    loss8, dwf8, merged_b, dx2_b, dx2, dya_b, dyb_b, dga_b, dgb_b = _merge_out_final(
        g_a, g_b, y_a, y_b, x, target, w_out, wf_row)

    g_w_out = _mm_tn(merged_b, dx2_b, "out_wgrad")
    g_w_o_dn = _mm_tn(on_b, dya_b, "out_dn_wgrad")
    d_o_a, dza_b, ddnw8 = _dn_out_bwd(dya_b, o_a, z_a, dn_norm_w, w_o_dn)

    g_w_o_dil = _mm_tn(ob_b, dyb_b, "out_dil_wgrad")
    d_o, dzb_b, delta = _attn_out_bwd(dyb_b, o_joint, z_b, w_o_dil)
    dqs, dks, dvs = [], [], []
    for gi in range(N_DIL):
        dq_g, dk_g, dv_g = _attn_bwd(q_b, k_b, v_b, d_o, lse, delta, gi)
        dqs.append(dq_g)
        dks.append(dk_g)
        dvs.append(dv_g)

    dvnew_d, dkd_d, ddl_d = _delta_scan_bwd(w_d, qd_d, kd_d, aqk_d, dl_d, vnew_d, st_d, d_o_a)
    dqn, dkn, dvn, dbg = _delta_post_bwd(qn, kn, vn, bg, t2_d, st_d, vnew_d, d_o_a, dvnew_d, dkd_d, ddl_d)
    dc, dba_b, dsmall8 = _dn_prep_bwd(qkv_pre, ba, conv_w8, alog_row, dtb_row, dqn, dkn, dvn, dbg)
    dqkv_b, dconv8 = _conv_bwd(dc, qkv_pre, conv_w8)

    per_group = lambda w: [w[g * DIL_W:(g + 1) * DIL_W] for g in range(N_DIL)]
    dh_b = _mm_in(dqs + dks + dvs + [dga_b, dgb_b],
                  per_group(w_qb) + per_group(w_kb) + per_group(w_vb) + [w_ga, w_gb], "proj_bwd_b", w_is_out_by_in=True)
    dsegs = [dqkv_b, dza_b, dba_b] + dqs + dks + dvs + [dzb_b, dga_b, dgb_b]
    valid_rows = [d.shape[1] for d in dsegs]
    valid_rows[2] = 2 * DN_HEADS
    g_wt = _proj_wgrad_all(dsegs, valid_rows, hb)
    grad_x, dnw8 = _proj_bwd_rms_in([dqkv_b, dza_b, dba_b, dzb_b], [w_qkv, w_za, w_ba, w_zb], dh_b, x, dx2, norm_w)

    small = dict(norm_w=dnw8[0:1], final_norm_w=dwf8[0:1], dn_norm_w=ddnw8[0:1],
                 a_log=dsmall8[0:1, DN_HEADS:2 * DN_HEADS], dt_bias=dsmall8[1:2, DN_HEADS:2 * DN_HEADS])
    return loss8[0:1, 0:1], grad_x, g_wt, dconv8[0:4], g_w_o_dn, g_w_o_dil, g_w_out, small


def _proj_bwd_rms_in(ds, ws, dh_a, x, dx2, norm_w):
    n_seg = len(ds)

    def body(*refs):
        d_refs, w_refs = refs[:n_seg], refs[n_seg:2 * n_seg]
        da_ref, x_ref, dx2_ref, w_ref, dx_ref, dw_ref = refs[2 * n_seg:]
        dx_ref[...] = da_ref[...]
        for d_ref, wt_ref in zip(d_refs, w_refs):
            for c, wd in _col_chunks(d_ref.shape[1], 1024):
                dx_ref[...] += jnp.dot(d_ref[:, c:c + wd], wt_ref[c:c + wd, :], preferred_element_type=F32)
        xv = x_ref[...]
        r = lax.rsqrt(jnp.mean(xv * xv, axis=-1, keepdims=True) + NORM_EPS)
        dhv = dx_ref[...]
        dn = dhv * w_ref[...]
        dx_ref[...] = dx2_ref[...] + r * dn - xv * (r * r * r) * jnp.mean(dn * xv, axis=-1, keepdims=True)
        row = jnp.sum(dhv * xv * r, axis=0, keepdims=True)
        _acc_add(dw_ref, jnp.concatenate([row, jnp.zeros((SUBLANES - 1, row.shape[1]), F32)], axis=0))

    return _rows_call(body, "proj_bwd_b_rms_in", x.shape[0],
                      [(d, "tile") for d in ds] + [(w, "full") for w in ws]
                      + [(dh_a, "tile"), (x, "tile"), (dx2, "tile"), (norm_w, "full")],
                      [(x.shape, F32, "tile"), ((SUBLANES, x.shape[1]), F32, "acc")])


def _split_proj_rows(wt_full):
    offs = [0]
    for n in PROJ_SIZES:
        offs.append(offs[-1] + n)
    seg = lambda a, b: wt_full[offs[a]:offs[b]]
    w_ba = jnp.concatenate([seg(4, 6), jnp.zeros((LANES - 2 * DN_HEADS, wt_full.shape[1]), wt_full.dtype)], axis=0)
    return [seg(0, 3), seg(3, 4), w_ba, seg(6, 7), seg(7, 8), seg(8, 9), seg(9, 10), seg(10, 11), seg(11, 12)]


LOSS_ROW = 5


def _pack_small(norm_w, final_norm_w, dn_norm_w, a_log, dt_bias, loss=None):
    pad = lambda r: jnp.concatenate([r, jnp.zeros((1, D_MODEL - r.shape[1]), F32)], axis=1)
    rows = [pad(norm_w.reshape(1, -1)), pad(final_norm_w.reshape(1, -1)), pad(dn_norm_w.reshape(1, -1)),
            pad(a_log.reshape(1, -1)), pad(dt_bias.reshape(1, -1)),
            pad(jnp.zeros((1, 1), F32) if loss is None else loss.reshape(1, 1)),
            jnp.zeros((SUBLANES - LOSS_ROW - 1, D_MODEL), F32)]
    return jnp.concatenate(rows, axis=0)


def _unpack_small(p):
    return dict(norm_w=p[0:1], final_norm_w=p[1], dn_norm_w=p[2:3, :DN_DK], a_log=p[3:4, :DN_HEADS],
                dt_bias=p[4:5, :DN_HEADS])


def kernel(x, norm_w, w_in, conv_w, a_log, dt_bias, dn_norm_w, w_o_dn, w_o_dil, w_out, final_norm_w, loss_target, m_norm_w, m_w_in, m_conv_w, m_a_log, m_dt_bias, m_dn_norm_w, m_w_o_dn, m_w_o_dil, m_w_out, m_final_norm_w, v_norm_w, v_w_in, v_conv_w, v_a_log, v_dt_bias, v_dn_norm_w, v_w_o_dn, v_w_o_dil, v_w_out, v_final_norm_w):
    shard_w = w_in.shape[2]
    wt, m_wt, v_wt = (jnp.transpose(t[0]) for t in (w_in, m_w_in, v_w_in))
    gathered = _all_gather([wt.astype(MXU), w_o_dn[0].astype(MXU), w_o_dil[0].astype(MXU), w_out[0].astype(MXU),
                            conv_w[0]], "gather_weights")
    w_in_all, w_o_dn_all, w_o_dil_all, w_out_all, conv_all = gathered
    wt_full = w_in_all.reshape(N_DEV * shard_w, D_MODEL)
    w_o_dn_full = w_o_dn_all.reshape(D_MODEL, D_MODEL)
    w_o_dil_full = jnp.transpose(w_o_dil_all, (1, 0, 2)).reshape(DIL_W, D_MODEL)
    w_out_full = w_out_all.reshape(D_MODEL, D_MODEL)
    conv_full = jnp.transpose(conv_all, (1, 0, 2)).reshape(conv_w.shape[1], QKV_W)

    loss11, grad_x, g_wt, g_conv, g_w_o_dn, g_w_o_dil, g_w_out, small = _local_step(
        x[0], loss_target[0], norm_w, _split_proj_rows(wt_full), conv_full, a_log, dt_bias, dn_norm_w,
        w_o_dn_full, w_o_dil_full, w_out_full, final_norm_w)

    col_shards = lambda g, n: jnp.transpose(g.reshape(g.shape[0], N_DEV, n), (1, 0, 2))
    row_shards = lambda g: g.reshape(N_DEV, g.shape[0] // N_DEV, g.shape[1])
    sent = [row_shards(g_wt), row_shards(g_w_o_dn).astype(MXU),
            col_shards(g_w_o_dil, w_o_dil.shape[2]).astype(MXU), row_shards(g_w_out).astype(MXU),
            col_shards(g_conv, conv_w.shape[2])]
    sent = [g8.reshape((N_CHIPS, 2) + g8.shape[1:]) for g8 in sent]
    from_sibling = _pair_exchange(sent, "scatter_pair")
    summed = [_pair_add(g, o, f"pair_add_{i}") for i, (g, o) in enumerate(zip(sent, from_sibling))]
    p_w_in, p_w_o_dn, p_w_o_dil, p_w_out, p_conv = _chip_exchange(summed, "scatter_chips")
    p_small = _all_gather([_pack_small(small["norm_w"], small["final_norm_w"], small["dn_norm_w"], small["a_log"],
                                       small["dt_bias"], loss11)], "gather_small_grads")[0]

    res = {}
    res["w_in"] = [jnp.transpose(t) for t in _adamw(p_w_in, wt, m_wt, v_wt, "adamw_w_in")]
    res["conv_w"] = _adamw(p_conv, conv_w[0], m_conv_w[0], v_conv_w[0], "adamw_conv_w")
    res["w_o_dn"] = _adamw(p_w_o_dn, w_o_dn[0], m_w_o_dn[0], v_w_o_dn[0], "adamw_w_o_dn")
    res["w_o_dil"] = _adamw(p_w_o_dil, w_o_dil[0], m_w_o_dil[0], v_w_o_dil[0], "adamw_w_o_dil")
    res["w_out"] = _adamw(p_w_out, w_out[0], m_w_out[0], v_w_out[0], "adamw_w_out")
    small_res = _adamw(p_small, _pack_small(norm_w, final_norm_w, dn_norm_w, a_log, dt_bias),
                       _pack_small(m_norm_w, m_final_norm_w, m_dn_norm_w, m_a_log, m_dt_bias),
                       _pack_small(v_norm_w, v_final_norm_w, v_dn_norm_w, v_a_log, v_dt_bias), "adamw_small")
    loss = small_res[0][LOSS_ROW, 0]
    small_res = [_unpack_small(t) for t in small_res]

    names = ["norm_w", "w_in", "conv_w", "a_log", "dt_bias", "dn_norm_w", "w_o_dn", "w_o_dil", "w_out", "final_norm_w"]
    outs = [loss, grad_x[None]]
    for kind in range(4):
        for nm in names:
            outs.append(res[nm][kind][None] if nm in res else small_res[kind][nm])
    return tuple(outs)
```

```python
import math

import jax
import jax.numpy as jnp
from jax import lax
from jax.experimental import pallas as pl
from jax.experimental.pallas import tpu as pltpu

F32 = jnp.float32
MXU = jnp.bfloat16
MESH = pl.DeviceIdType.MESH

N_DEV = 8
D_MODEL = 1024
DN_HEADS = 8
DN_DK = 128
DN_CHUNK = 64
N_DIL = 3
DIL_HEADS = 4
DIL_DH = 128
DIL_W = DIL_HEADS * DIL_DH
DIL_GROUPS = ((128, 1), (512, 4), (2048, 16))
ATT_BLOCK = 128
NORM_EPS = 1e-6
QKV_W = 3 * D_MODEL
DILQ_W = N_DIL * DIL_W
PROJ_SIZES = (1024, 1024, 1024, 1024, 8, 8, DILQ_W, DILQ_W, DILQ_W, DIL_W, D_MODEL, D_MODEL)

ADAM_LR = 0.001
ADAM_B1 = 0.9
ADAM_B2 = 0.999
ADAM_EPS = 1e-08
ADAM_WD = 0.01
ADAM_STEP = 10

ROW_TILE = 256
LANES = 128
SUBLANES = 8
VMEM_LIMIT = 48 << 20


def _pcall(body, **kw):
    return pl.pallas_call(body, **kw)


def _params(*sem):
    return pltpu.CompilerParams(dimension_semantics=tuple(sem), vmem_limit_bytes=VMEM_LIMIT)


def _sigmoid(x):
    return 1.0 / (1.0 + jnp.exp(-x))


def _softplus(x):
    return jnp.maximum(x, 0.0) + jnp.log(1.0 + jnp.exp(-jnp.abs(x)))


def _dot(a, b):
    return jnp.dot(a.astype(MXU), b.astype(MXU), preferred_element_type=F32)


def _dot_nt(a, b):
    return lax.dot_general(a.astype(MXU), b.astype(MXU), (((1,), (1,)), ((), ())), preferred_element_type=F32)


def _dot_tn(a, b):
    return lax.dot_general(a.astype(MXU), b.astype(MXU), (((0,), (0,)), ((), ())), preferred_element_type=F32)


def _split3(x):
    hi = x.astype(jnp.bfloat16)
    r1 = x - hi.astype(F32)
    mid = r1.astype(jnp.bfloat16)
    lo = (r1 - mid.astype(F32)).astype(jnp.bfloat16)
    return hi, mid, lo


def _dot01(m01, x):
    m = m01.astype(jnp.bfloat16)
    hi, mid, lo = _split3(x)
    f = lambda p: jnp.dot(m, p, preferred_element_type=F32)
    return f(hi) + (f(mid) + f(lo))


def _rows_call(body, name, n_rows, ins, outs, scratch=(), tm=ROW_TILE):
    steps = n_rows // tm
    per8 = tm // SUBLANES
    last8 = n_rows // SUBLANES - 1
    in_specs = []
    for arr, kind in ins:
        cols = arr.shape[-1]
        if kind == "tile":
            in_specs.append(pl.BlockSpec((tm, cols), lambda i: (i, 0)))
        elif kind == "full":
            in_specs.append(pl.BlockSpec(arr.shape, lambda i, nd=arr.ndim: (0,) * nd))
        elif kind == "prev8":
            in_specs.append(pl.BlockSpec((SUBLANES, cols), lambda i: (jnp.maximum(i * per8 - 1, 0), 0)))
        elif kind == "next8":
            in_specs.append(pl.BlockSpec((SUBLANES, cols), lambda i: (jnp.minimum((i + 1) * per8, last8), 0)))
        else:
            raise ValueError(kind)
    out_specs, out_shape, has_acc = [], [], False
    for shape, dtype, kind in outs:
        out_shape.append(jax.ShapeDtypeStruct(shape, dtype))
        if kind == "tile":
            out_specs.append(pl.BlockSpec((tm, shape[-1]), lambda i: (i, 0)))
        else:
            has_acc = True
            out_specs.append(pl.BlockSpec(shape, lambda i: (0, 0)))
    return _pcall(
        body, name=name, grid=(steps,), in_specs=in_specs, out_specs=out_specs, out_shape=out_shape,
        scratch_shapes=list(scratch),
        compiler_params=_params("arbitrary" if has_acc else "parallel"),
    )(*[a for a, _ in ins])


def _acc_add(ref, value):
    @pl.when(pl.program_id(0) == 0)
    def _():
        ref[...] = jnp.zeros_like(ref)
    ref[...] += value


def _col_chunks(n, width=512):
    return [(c, min(width, n - c)) for c in range(0, n, width)]


NT_DIMS = (((1,), (1,)), ((), ()))
TN_DIMS = (((0,), (0,)), ((), ()))


def _mm_out(a, ws, name, w_is_out_by_in=False, out_dtype=F32, tm=ROW_TILE):
    m, k = a.shape
    ns = [w.shape[0] if w_is_out_by_in else w.shape[1] for w in ws]

    def body(a_ref, *refs):
        av = a_ref[...]
        for w_ref, o_ref, n in zip(refs[:len(ws)], refs[len(ws):], ns):
            for c, wd in _col_chunks(n):
                if w_is_out_by_in:
                    part = lax.dot_general(av, w_ref[c:c + wd, :], NT_DIMS, preferred_element_type=F32)
                else:
                    part = jnp.dot(av, w_ref[:, c:c + wd], preferred_element_type=F32)
                o_ref[:, c:c + wd] = part.astype(o_ref.dtype)

    return _pcall(
        body, name=name, grid=(m // tm,),
        in_specs=[pl.BlockSpec((tm, k), lambda i: (i, 0))] + [pl.BlockSpec(w.shape, lambda i: (0, 0)) for w in ws],
        out_specs=[pl.BlockSpec((tm, n), lambda i: (i, 0)) for n in ns],
        out_shape=[jax.ShapeDtypeStruct((m, n), out_dtype) for n in ns],
        compiler_params=_params("parallel"),
    )(a, *ws)


def _rms_proj_fwd(x, norm_w, wts, name, tm=ROW_TILE):
    m, k = x.shape
    ns = [w.shape[0] for w in wts]

    def body(x_ref, nw_ref, *refs):
        w_refs, h_ref, o_refs = refs[:len(wts)], refs[len(wts)], refs[len(wts) + 1:]
        xv = x_ref[...]
        r = lax.rsqrt(jnp.mean(xv * xv, axis=-1, keepdims=True) + NORM_EPS)
        hv = (xv * r * nw_ref[...]).astype(h_ref.dtype)
        h_ref[...] = hv
        for w_ref, o_ref, n in zip(w_refs, o_refs, ns):
            for c, wd in _col_chunks(n):
                o_ref[:, c:c + wd] = lax.dot_general(hv, w_ref[c:c + wd, :], NT_DIMS, preferred_element_type=F32)

    return _pcall(
        body, name=name, grid=(m // tm,),
        in_specs=[pl.BlockSpec((tm, k), lambda i: (i, 0)), pl.BlockSpec(norm_w.shape, lambda i: (0, 0))]
        + [pl.BlockSpec(w.shape, lambda i: (0, 0)) for w in wts],
        out_specs=[pl.BlockSpec((tm, k), lambda i: (i, 0))] + [pl.BlockSpec((tm, n), lambda i: (i, 0)) for n in ns],
        out_shape=[jax.ShapeDtypeStruct((m, k), MXU)] + [jax.ShapeDtypeStruct((m, n), F32) for n in ns],
        compiler_params=_params("parallel"),
    )(x, norm_w, *wts)


def _mm_in(ds, ws, name, w_is_out_by_in=False, tm=ROW_TILE):
    m = ds[0].shape[0]
    k = ws[0].shape[1] if w_is_out_by_in else ws[0].shape[0]
    ns = [d.shape[1] for d in ds]

    def body(*refs):
        d_refs, w_refs, o_ref = refs[:len(ds)], refs[len(ds):2 * len(ds)], refs[-1]
        first = True
        for d_ref, w_ref, n in zip(d_refs, w_refs, ns):
            for c, wd in _col_chunks(n, 1024):
                if w_is_out_by_in:
                    part = jnp.dot(d_ref[:, c:c + wd], w_ref[c:c + wd, :], preferred_element_type=F32)
                else:
                    part = lax.dot_general(d_ref[:, c:c + wd], w_ref[:, c:c + wd], NT_DIMS, preferred_element_type=F32)
                if first:
                    o_ref[...] = part
                    first = False
                else:
                    o_ref[...] += part

    return _pcall(
        body, name=name, grid=(m // tm,),
        in_specs=[pl.BlockSpec((tm, n), lambda i: (i, 0)) for n in ns] + [pl.BlockSpec(w.shape, lambda i: (0, 0)) for w in ws],
        out_specs=pl.BlockSpec((tm, k), lambda i: (i, 0)),
        out_shape=jax.ShapeDtypeStruct((m, k), F32),
        compiler_params=_params("parallel"),
    )(*ds, *ws)


def _mm_tn(a, d, name):
    m, k = a.shape
    n = d.shape[1]
    tk = 512 if k % 512 == 0 else k

    def body(a_ref, d_ref, o_ref):
        o_ref[...] = lax.dot_general(a_ref[...], d_ref[...], TN_DIMS, preferred_element_type=F32)

    return _pcall(
        body, name=name, grid=(k // tk,),
        in_specs=[pl.BlockSpec((m, tk), lambda p: (0, p)), pl.BlockSpec((m, n), lambda p: (0, 0))],
        out_specs=pl.BlockSpec((tk, n), lambda p: (p, 0)),
        out_shape=jax.ShapeDtypeStruct((k, n), F32),
        compiler_params=_params("parallel"),
    )(a, d)


WGRAD_TILE = 512


def _proj_wgrad_all(dsegs, valid_rows, hb):
    m, k = hb.shape
    n_seg = len(dsegs)
    tiles, row = [], 0
    for si, (d, valid) in enumerate(zip(dsegs, valid_rows)):
        for c in range(0, valid, WGRAD_TILE):
            width = min(WGRAD_TILE, d.shape[1] - c)
            tiles.append((si, c, width, row + c, min(width, valid - c)))
        row += valid
    total_rows = row

    def body(*refs):
        d_refs, hb_ref, o_ref = refs[:n_seg], refs[n_seg], refs[n_seg + 1]
        a_buf, hb_buf, o_buf, load_sems, store_sems, hb_sem = refs[n_seg + 2:]

        def load(t):
            si, c, width, _, _ = tiles[t]
            return pltpu.make_async_copy(d_refs[si].at[:, pl.ds(c, width)], a_buf.at[t % 2, :, pl.ds(0, width)],
                                         load_sems.at[t % 2])

        def stores(t):
            _, _, _, orow, valid = tiles[t]
            return [pltpu.make_async_copy(o_buf.at[t % 2, pl.ds(0, valid), :], o_ref.at[pl.ds(orow, valid), :],
                                          store_sems.at[t % 2])]

        hb_copy = pltpu.make_async_copy(hb_ref, hb_buf, hb_sem)
        hb_copy.start()
        load(0).start()
        hb_copy.wait()
        for t in range(len(tiles)):
            width = tiles[t][2]
            load(t).wait()
            if t + 1 < len(tiles):
                load(t + 1).start()
            if t >= 2:
                for cp in stores(t - 2):
                    cp.wait()
            o_buf[t % 2, 0:width, :] = lax.dot_general(a_buf[t % 2, :, 0:width], hb_buf[...], TN_DIMS,
                                                        preferred_element_type=F32).astype(o_buf.dtype)
            for cp in stores(t):
                cp.start()
        for t in range(max(len(tiles) - 2, 0), len(tiles)):
            for cp in stores(t):
                cp.wait()

    any_spec = pl.BlockSpec(memory_space=pl.ANY)
    return _pcall(
        body, name="proj_wgrad",
        in_specs=[any_spec] * (n_seg + 1), out_specs=any_spec,
        out_shape=jax.ShapeDtypeStruct((total_rows, k), hb.dtype),
        scratch_shapes=[pltpu.VMEM((2, m, WGRAD_TILE), hb.dtype), pltpu.VMEM((m, k), hb.dtype),
                        pltpu.VMEM((2, WGRAD_TILE, k), hb.dtype), pltpu.SemaphoreType.DMA((2,)),
                        pltpu.SemaphoreType.DMA((2,)), pltpu.SemaphoreType.DMA],
        compiler_params=pltpu.CompilerParams(vmem_limit_bytes=VMEM_LIMIT),
    )(*dsegs, hb)


def _conv_taps(ext_ref, cw_ref, cols, tm):
    c = None
    for j in range(4):
        term = cw_ref[3 - j:4 - j, cols] * ext_ref[SUBLANES - j:SUBLANES - j + tm, cols]
        c = term if c is None else c + term
    return c


def _fill_ext(ext_ref, u_ref, halo_ref, first):
    ext_ref[0:SUBLANES, :] = jnp.where(first, 0.0, halo_ref[...])
    ext_ref[SUBLANES:, :] = u_ref[...]


def _dn_prep_fwd(qkv_pre, ba, conv_w8, alog_row, dtb_row):
    s = qkv_pre.shape[0]
    tm = ROW_TILE

    def body(u_ref, halo_ref, cw_ref, ba_ref, al_ref, dtb_ref, q_ref, k_ref, v_ref, bg_ref, ext_ref):
        _fill_ext(ext_ref, u_ref, halo_ref, pl.program_id(0) == 0)
        for h in range(3 * DN_HEADS):
            cols = slice(h * LANES, (h + 1) * LANES)
            c = _conv_taps(ext_ref, cw_ref, cols, tm)
            a = c * _sigmoid(c)
            oc = slice((h % DN_HEADS) * LANES, (h % DN_HEADS + 1) * LANES)
            if h < 2 * DN_HEADS:
                rinv = lax.rsqrt(jnp.sum(a * a, axis=-1, keepdims=True) + NORM_EPS)
                if h < DN_HEADS:
                    q_ref[:, oc] = a * (rinv * DN_DK ** -0.5)
                else:
                    k_ref[:, oc] = a * rinv
            else:
                v_ref[:, oc] = a
        bav = ba_ref[...]
        lane = lax.broadcasted_iota(jnp.int32, bav.shape, 1)
        beta = _sigmoid(bav)
        g = -jnp.exp(al_ref[...]) * _softplus(bav + dtb_ref[...])
        bg_ref[...] = jnp.where(lane < DN_HEADS, beta, jnp.where(lane < 2 * DN_HEADS, g, 0.0))

    return _rows_call(
        body, "dn_prep_fwd", s,
        [(qkv_pre, "tile"), (qkv_pre, "prev8"), (conv_w8, "full"), (ba, "tile"), (alog_row, "full"), (dtb_row, "full")],
        [((s, D_MODEL), F32, "tile")] * 3 + [((s, LANES), F32, "tile")],
        scratch=[pltpu.VMEM((tm + SUBLANES, QKV_W), F32)])


def _dn_prep_bwd(qkv_pre, ba, conv_w8, alog_row, dtb_row, dq, dk, dv, dbg):
    s = qkv_pre.shape[0]
    tm = ROW_TILE

    def body(u_ref, halo_ref, cw_ref, ba_ref, al_ref, dtb_ref, dq_ref, dk_ref, dv_ref, dbg_ref,
             dc_ref, dba_ref, dsmall_ref, ext_ref):
        _fill_ext(ext_ref, u_ref, halo_ref, pl.program_id(0) == 0)
        for h in range(3 * DN_HEADS):
            cols = slice(h * LANES, (h + 1) * LANES)
            oc = slice((h % DN_HEADS) * LANES, (h % DN_HEADS + 1) * LANES)
            c = _conv_taps(ext_ref, cw_ref, cols, tm)
            sg = _sigmoid(c)
            a = c * sg
            if h < 2 * DN_HEADS:
                rinv = lax.rsqrt(jnp.sum(a * a, axis=-1, keepdims=True) + NORM_EPS)
                dy = dq_ref[:, oc] * DN_DK ** -0.5 if h < DN_HEADS else dk_ref[:, oc]
                da = rinv * dy - a * (rinv * rinv * rinv) * jnp.sum(dy * a, axis=-1, keepdims=True)
            else:
                da = dv_ref[:, oc]
            dc_ref[:, cols] = da * (sg * (1.0 + c * (1.0 - sg)))
        bav = ba_ref[...]
        dbgv = dbg_ref[...]
        lane = lax.broadcasted_iota(jnp.int32, bav.shape, 1)
        beta = _sigmoid(bav)
        ea = jnp.exp(al_ref[...])
        z = bav + dtb_ref[...]
        g = -ea * _softplus(z)
        is_b = lane < DN_HEADS
        is_g = jnp.logical_and(lane >= DN_HEADS, lane < 2 * DN_HEADS)
        d_aa = jnp.where(is_g, dbgv * (-ea) * _sigmoid(z), 0.0)
        dba = jnp.where(is_b, dbgv * beta * (1.0 - beta), d_aa)
        dba_ref[...] = dba.astype(dba_ref.dtype)
        r_alog = jnp.sum(jnp.where(is_g, dbgv * g, 0.0), axis=0, keepdims=True)
        r_dtb = jnp.sum(d_aa, axis=0, keepdims=True)
        _acc_add(dsmall_ref, jnp.concatenate([r_alog, r_dtb, jnp.zeros((SUBLANES - 2, LANES), F32)], axis=0))

    return _rows_call(
        body, "dn_prep_bwd", s,
        [(qkv_pre, "tile"), (qkv_pre, "prev8"), (conv_w8, "full"), (ba, "tile"), (alog_row, "full"), (dtb_row, "full"),
         (dq, "tile"), (dk, "tile"), (dv, "tile"), (dbg, "tile")],
        [((s, QKV_W), F32, "tile"), ((s, LANES), MXU, "tile"), ((SUBLANES, LANES), F32, "acc")],
        scratch=[pltpu.VMEM((tm + SUBLANES, QKV_W), F32)])


def _conv_bwd(dc, qkv_pre, conv_w8):
    s = dc.shape[0]
    tm = ROW_TILE
    steps = s // tm

    def body(dc_ref, dnext_ref, u_ref, halo_ref, cw_ref, du_ref, dcw_ref, extd_ref, ext_ref):
        i = pl.program_id(0)
        _fill_ext(ext_ref, u_ref, halo_ref, i == 0)
        extd_ref[0:tm, :] = dc_ref[...]
        extd_ref[tm:, :] = jnp.where(i == steps - 1, 0.0, dnext_ref[...])

        @pl.when(i == 0)
        def _():
            dcw_ref[...] = jnp.zeros_like(dcw_ref)

        for h in range(3 * DN_HEADS):
            cols = slice(h * LANES, (h + 1) * LANES)
            du = None
            for j in range(4):
                term = cw_ref[3 - j:4 - j, cols] * extd_ref[j:j + tm, cols]
                du = term if du is None else du + term
            du_ref[:, cols] = du.astype(du_ref.dtype)
            dcv = dc_ref[:, cols]
            for j in range(4):
                row = jnp.sum(dcv * ext_ref[SUBLANES - j:SUBLANES - j + tm, cols], axis=0, keepdims=True)
                dcw_ref[3 - j:4 - j, cols] += row

    return _rows_call(
        body, "conv_bwd", s,
        [(dc, "tile"), (dc, "next8"), (qkv_pre, "tile"), (qkv_pre, "prev8"), (conv_w8, "full")],
        [((s, QKV_W), MXU, "tile"), ((SUBLANES, QKV_W), F32, "acc")],
        scratch=[pltpu.VMEM((tm + SUBLANES, QKV_W), F32), pltpu.VMEM((tm + SUBLANES, QKV_W), F32)])


def _dn_out_fwd(o, z, dnw_row, w_o_dn):
    def body(o_ref, z_ref, w_ref, wo_ref, on_ref, y_ref):
        for h in range(DN_HEADS):
            cols = slice(h * LANES, (h + 1) * LANES)
            ov = o_ref[:, cols]
            zv = z_ref[:, cols]
            ro = lax.rsqrt(jnp.mean(ov * ov, axis=-1, keepdims=True) + NORM_EPS)
            on_ref[:, cols] = (ov * ro * w_ref[...] * (zv * _sigmoid(zv))).astype(on_ref.dtype)
        y_ref[...] = jnp.dot(on_ref[...], wo_ref[...], preferred_element_type=F32)

    return _rows_call(body, "dn_out_fwd", o.shape[0], [(o, "tile"), (z, "tile"), (dnw_row, "full"), (w_o_dn, "full")],
                      [(o.shape, MXU, "tile"), ((o.shape[0], w_o_dn.shape[1]), F32, "tile")])


def _dn_out_bwd(dy, o, z, dnw_row, w_o_dn):
    def body(dy_ref, o_ref, z_ref, w_ref, wo_ref, do_ref, dz_ref, dw_ref, d_ref):
        d_ref[...] = lax.dot_general(dy_ref[...], wo_ref[...], NT_DIMS, preferred_element_type=F32)
        acc = jnp.zeros((1, LANES), F32)
        for h in range(DN_HEADS):
            cols = slice(h * LANES, (h + 1) * LANES)
            dv, ov, zv = d_ref[:, cols], o_ref[:, cols], z_ref[:, cols]
            sg = _sigmoid(zv)
            sz = zv * sg
            ro = lax.rsqrt(jnp.mean(ov * ov, axis=-1, keepdims=True) + NORM_EPS)
            nv = ov * ro
            dn = dv * w_ref[...] * sz
            acc = acc + jnp.sum(dv * nv * sz, axis=0, keepdims=True)
            dz_ref[:, cols] = (dv * nv * w_ref[...] * (sg * (1.0 + zv * (1.0 - sg)))).astype(dz_ref.dtype)
            do_ref[:, cols] = ro * dn - ov * (ro * ro * ro) * jnp.mean(dn * ov, axis=-1, keepdims=True)
        _acc_add(dw_ref, jnp.concatenate([acc, jnp.zeros((SUBLANES - 1, LANES), F32)], axis=0))

    return _rows_call(body, "dn_out_bwd", o.shape[0],
                      [(dy, "tile"), (o, "tile"), (z, "tile"), (dnw_row, "full"), (w_o_dn, "full")],
                      [(o.shape, F32, "tile"), (o.shape, MXU, "tile"), ((SUBLANES, LANES), F32, "acc")],
                      scratch=[pltpu.VMEM((ROW_TILE, o.shape[1]), F32)])


def _attn_out_fwd(parts, lses, zb, w_o_dil):
    def body(o0, o1, o2, l0, l1, l2, z_ref, wo_ref, lse_ref, o_ref, g_ref, y_ref):
        a, b, c = l0[...], l1[...], l2[...]
        m = jnp.maximum(a, jnp.maximum(b, c))
        ea, eb, ec = jnp.exp(a - m), jnp.exp(b - m), jnp.exp(c - m)
        den = ea + eb + ec
        out = (ea * o0[...] + eb * o1[...] + ec * o2[...]) / den
        lse_ref[...] = m + jnp.log(den)
        o_ref[...] = out
        zv = z_ref[...]
        gated = (out * (zv * _sigmoid(zv))).astype(g_ref.dtype)
        g_ref[...] = gated
        y_ref[...] = jnp.dot(gated, wo_ref[...], preferred_element_type=F32)

    s = zb.shape[0]
    return _rows_call(body, "attn_out_fwd", s,
                      [(p, "tile") for p in parts] + [(l, "tile") for l in lses] + [(zb, "tile"), (w_o_dil, "full")],
                      [((s, DIL_W), F32, "tile"), ((s, DIL_W), F32, "tile"), ((s, DIL_W), MXU, "tile"),
                       ((s, w_o_dil.shape[1]), F32, "tile")])


def _attn_out_bwd(dy, o_joint, zb, w_o_dil):
    def body(dy_ref, o_ref, z_ref, wo_ref, do_ref, dz_ref, dl_ref):
        zv = z_ref[...]
        sg = _sigmoid(zv)
        dv = lax.dot_general(dy_ref[...], wo_ref[...], NT_DIMS, preferred_element_type=F32)
        ov = o_ref[...]
        do = dv * (zv * sg)
        do_ref[...] = do
        dz_ref[...] = (dv * ov * (sg * (1.0 + zv * (1.0 - sg)))).astype(dz_ref.dtype)
        for h in range(DIL_HEADS):
            cols = slice(h * LANES, (h + 1) * LANES)
            dl_ref[:, cols] = jnp.broadcast_to(jnp.sum(do[:, cols] * ov[:, cols], axis=-1, keepdims=True),
                                               (do.shape[0], LANES))

    s = zb.shape[0]
    return _rows_call(body, "attn_out_bwd", s, [(dy, "tile"), (o_joint, "tile"), (zb, "tile"), (w_o_dil, "full")],
                      [((s, DIL_W), F32, "tile"), ((s, DIL_W), MXU, "tile"), ((s, DIL_W), F32, "tile")])


def _merge_out_final(ga, gb, ya, yb, x, target, w_out, wf_row):
    s, dm = x.shape

    def body(ga_ref, gb_ref, ya_ref, yb_ref, x_ref, t_ref, wo_ref, w_ref,
             loss_ref, dw_ref, m_ref, dxb_ref, dx_ref, dya_ref, dyb_ref, dga_ref, dgb_ref):
        sa, sb = _sigmoid(ga_ref[...]), _sigmoid(gb_ref[...])
        ya, yb = ya_ref[...], yb_ref[...]
        merged = (sa * ya + sb * yb).astype(MXU)
        m_ref[...] = merged
        x2 = x_ref[...] + jnp.dot(merged, wo_ref[...], preferred_element_type=F32)
        r = lax.rsqrt(jnp.mean(x2 * x2, axis=-1, keepdims=True) + NORM_EPS)
        w = w_ref[...]
        err = x2 * r * w - t_ref[...]
        tile_loss = 0.5 * jnp.sum(jnp.mean(err * err, axis=-1, keepdims=True), axis=0, keepdims=True)
        _acc_add(loss_ref, jnp.broadcast_to(tile_loss, (SUBLANES, LANES)))
        dy = err * (1.0 / dm)
        row = jnp.sum(dy * x2 * r, axis=0, keepdims=True)
        _acc_add(dw_ref, jnp.concatenate([row, jnp.zeros((SUBLANES - 1, dm), F32)], axis=0))
        dn = dy * w
        dx2 = r * dn - x2 * (r * r * r) * jnp.mean(dn * x2, axis=-1, keepdims=True)
        dx_ref[...] = dx2
        dxb = dx2.astype(MXU)
        dxb_ref[...] = dxb
        dmv = lax.dot_general(dxb, wo_ref[...], NT_DIMS, preferred_element_type=F32)
        dya_ref[...] = (dmv * sa).astype(dya_ref.dtype)
        dyb_ref[...] = (dmv * sb).astype(dyb_ref.dtype)
        dga_ref[...] = (dmv * ya * sa * (1.0 - sa)).astype(dga_ref.dtype)
        dgb_ref[...] = (dmv * yb * sb * (1.0 - sb)).astype(dgb_ref.dtype)

    return _rows_call(body, "merge_out_final", s,
                      [(ga, "tile"), (gb, "tile"), (ya, "tile"), (yb, "tile"), (x, "tile"), (target, "tile"),
                       (w_out, "full"), (wf_row, "full")],
                      [((SUBLANES, LANES), F32, "acc"), ((SUBLANES, dm), F32, "acc"), ((s, dm), MXU, "tile"),
                       ((s, dm), MXU, "tile"), ((s, dm), F32, "tile")] + [((s, dm), MXU, "tile")] * 4)


def _lane_pick(x, idx):
    lane = lax.broadcasted_iota(jnp.int32, x.shape, 1)
    return jnp.sum(jnp.where(lane == idx, x, 0.0), axis=-1, keepdims=True)


PAIR = 2 * DN_CHUNK
SCAN_CHUNKS = 4


def _bmm(a, b):
    return lax.dot_general(a.astype(MXU), b.astype(MXU), (((2,), (1,)), ((0,), (0,))), preferred_element_type=F32)


def _bmm_nt(a, b):
    return lax.dot_general(a.astype(MXU), b.astype(MXU), (((2,), (2,)), ((0,), (0,))), preferred_element_type=F32)


def _bmm_tn(a, b):
    return lax.dot_general(a.astype(MXU), b.astype(MXU), (((1,), (1,)), ((0,), (0,))), preferred_element_type=F32)


def _bmm3(a, b):
    ah = a.astype(jnp.bfloat16)
    al = (a - ah.astype(F32)).astype(jnp.bfloat16)
    bh = b.astype(jnp.bfloat16)
    bl = (b - bh.astype(F32)).astype(jnp.bfloat16)
    f = lambda p, q: lax.dot_general(p, q, (((2,), (1,)), ((0,), (0,))), preferred_element_type=F32)
    return f(ah, bh) + (f(ah, bl) + f(al, bh))


def _pair_masks():
    row = lax.broadcasted_iota(jnp.int32, (PAIR, PAIR), 0)
    col = lax.broadcasted_iota(jnp.int32, (PAIR, PAIR), 1)
    same = (row >= DN_CHUNK) == (col >= DN_CHUNK)
    return dict(causal=same & (row >= col), strict=same & (row > col), upper=same & (row <= col), eye=row == col,
                first=row < DN_CHUNK, row=row, lane=col)


def _pair_decay(bgv, masks):
    gc_all = _dot01(masks["causal"].astype(F32), bgv)
    out = []
    for h in range(DN_HEADS):
        beta = _lane_pick(bgv, h)
        gcb = jnp.broadcast_to(_lane_pick(gc_all, DN_HEADS + h), (PAIR, PAIR))
        gam = jnp.where(masks["causal"], jnp.exp(jnp.minimum(gcb - gcb.T, 0.0)), 0.0)
        gl = jnp.where(masks["first"], gcb[DN_CHUNK - 1:DN_CHUNK, :], gcb[PAIR - 1:PAIR, :])
        out.append((beta, gcb, gam, gl))
    return out


def _pair_inverse(a_strict, eye):
    eye_f = eye.astype(F32)[None]
    m = eye_f + a_strict
    x = eye_f - a_strict
    steps = int(math.log2(DN_CHUNK)) - 1
    for i in range(steps):
        mm = _bmm3 if i == steps - 1 else _bmm
        x = x + mm(x, eye_f - mm(m, x))
    return x


def _head_cols(h):
    return slice(h * LANES, (h + 1) * LANES)


def _delta_prep(q, k, v, bg):
    s = q.shape[0]
    c = DN_CHUNK
    n_chunks = s // c

    def body(q_ref, k_ref, v_ref, bg_ref, u_ref, w_ref, qd_ref, kd_ref, aqk_ref, dl_ref, t2_ref):
        masks = _pair_masks()
        dec = _pair_decay(bg_ref[...], masks)
        kbs, ks, gams, vbs, kbes, qs, qds, kds, dls = ([] for _ in range(9))
        for h in range(DN_HEADS):
            beta, gcb, gam, gl = dec[h]
            qh, kh, vh = q_ref[:, _head_cols(h)], k_ref[:, _head_cols(h)], v_ref[:, _head_cols(h)]
            eg = jnp.exp(gcb)
            kb = kh * beta
            kbs.append(kb); ks.append(kh); gams.append(gam); vbs.append(vh * beta); kbes.append(kb * eg)
            qs.append(qh); qds.append(qh * eg); kds.append(kh * jnp.exp(gl - gcb)); dls.append(jnp.exp(gl))
        st = lambda xs: jnp.stack(xs, axis=0)
        kmat, gam = st(ks), st(gams)
        a = jnp.where(masks["strict"][None], _bmm_nt(st(kbs), kmat) * gam, 0.0)
        t = _pair_inverse(a, masks["eye"])
        u = _bmm(t, st(vbs))
        w = _bmm(t, st(kbes))
        aqk = _bmm_nt(st(qs), kmat) * gam
        t2_ref[0] = t.astype(t2_ref.dtype)
        for half in range(2):
            rows = slice(half * c, (half + 1) * c)
            u_ref[half] = u[:, rows, :]
            w_ref[half] = w[:, rows, :].astype(w_ref.dtype)
            qd_ref[half] = st(qds)[:, rows, :].astype(qd_ref.dtype)
            kd_ref[half] = st(kds)[:, rows, :].astype(kd_ref.dtype)
            aqk_ref[half] = aqk[:, rows, rows].astype(aqk_ref.dtype)
            dl_ref[half] = st(dls)[:, half * c:half * c + SUBLANES, :]

    row_spec = lambda w_: pl.BlockSpec((PAIR, w_), lambda i: (i, 0))
    hm = lambda a_, b_: pl.BlockSpec((2, DN_HEADS, a_, b_), lambda i: (i, 0, 0, 0))
    hm_shape = lambda a_, b_, dt: jax.ShapeDtypeStruct((n_chunks, DN_HEADS, a_, b_), dt)
    return _pcall(
        body, name="delta_prep", grid=(n_chunks // 2,),
        in_specs=[row_spec(D_MODEL)] * 3 + [row_spec(LANES)],
        out_specs=[hm(c, LANES)] * 4 + [hm(c, c), hm(SUBLANES, LANES),
                   pl.BlockSpec((1, DN_HEADS, PAIR, PAIR), lambda i: (i, 0, 0, 0))],
        out_shape=[hm_shape(c, LANES, F32), hm_shape(c, LANES, MXU), hm_shape(c, LANES, MXU), hm_shape(c, LANES, MXU),
                   hm_shape(c, c, MXU), hm_shape(SUBLANES, LANES, F32),
                   jax.ShapeDtypeStruct((n_chunks // 2, DN_HEADS, PAIR, PAIR), MXU)],
        compiler_params=_params("parallel"),
    )(q, k, v, bg)


def _delta_scan_fwd(u, w, qd, kd, aqk, dl):
    n_chunks = u.shape[0]
    c = DN_CHUNK
    g_n = SCAN_CHUNKS

    def body(u_ref, w_ref, qd_ref, kd_ref, aqk_ref, dl_ref, o_ref, vnew_ref, st_ref, state):
        @pl.when(pl.program_id(0) == 0)
        def _():
            state[...] = jnp.zeros_like(state)

        for g in range(g_n):
            sv = state[...]
            sb = sv.astype(MXU)
            vnew = u_ref[g] - _bmm(w_ref[g], sb)
            o = _bmm(qd_ref[g], sb) + _bmm(aqk_ref[g], vnew)
            state[...] = sv * dl_ref[g][:, 0:1, :] + _bmm_tn(kd_ref[g], vnew)
            vnew_ref[g] = vnew.astype(vnew_ref.dtype)
            st_ref[g] = sb
            for h in range(DN_HEADS):
                o_ref[g * c:(g + 1) * c, _head_cols(h)] = o[h]

    hm = lambda a_, b_: pl.BlockSpec((g_n, DN_HEADS, a_, b_), lambda i: (i, 0, 0, 0))
    return _pcall(
        body, name="delta_scan_fwd", grid=(n_chunks // g_n,),
        in_specs=[hm(c, LANES)] * 4 + [hm(c, c), hm(SUBLANES, LANES)],
        out_specs=[pl.BlockSpec((g_n * c, D_MODEL), lambda i: (i, 0)), hm(c, LANES), hm(DN_DK, DN_DK)],
        out_shape=[jax.ShapeDtypeStruct((n_chunks * c, D_MODEL), F32),
                   jax.ShapeDtypeStruct((n_chunks, DN_HEADS, c, LANES), MXU),
                   jax.ShapeDtypeStruct((n_chunks, DN_HEADS, DN_DK, DN_DK), MXU)],
        scratch_shapes=[pltpu.VMEM((DN_HEADS, DN_DK, DN_DK), F32)],
        compiler_params=_params("arbitrary"),
    )(u, w, qd, kd, aqk, dl)


def _delta_scan_bwd(w, qd, kd, aqk, dl, vnew, st, do):
    n_chunks = w.shape[0]
    c = DN_CHUNK
    g_n = SCAN_CHUNKS
    steps = n_chunks // g_n

    def body(w_ref, qd_ref, kd_ref, aqk_ref, dl_ref, vnew_ref, st_ref, do_ref, dvnew_ref, dkd_ref, ddl_ref, dstate):
        @pl.when(pl.program_id(0) == 0)
        def _():
            dstate[...] = jnp.zeros_like(dstate)

        for g in reversed(range(g_n)):
            ds = dstate[...]
            dsb = ds.astype(MXU)
            doh = jnp.stack([do_ref[g * c:(g + 1) * c, _head_cols(h)] for h in range(DN_HEADS)], axis=0)
            dvnew = _bmm_tn(aqk_ref[g], doh) + _bmm(kd_ref[g], dsb)
            dkd_ref[g] = _bmm_nt(vnew_ref[g], dsb)
            ddl = jnp.sum(jnp.sum(st_ref[g].astype(F32) * ds, axis=2, keepdims=True), axis=1, keepdims=True)
            ddl_ref[g] = jnp.broadcast_to(ddl, (DN_HEADS, SUBLANES, LANES))
            dstate[...] = ds * dl_ref[g][:, 0:1, :] + _bmm_tn(qd_ref[g], doh) - _bmm_tn(w_ref[g], dvnew)
            dvnew_ref[g] = dvnew.astype(dvnew_ref.dtype)

    rev = lambda i: steps - 1 - i
    hm = lambda a_, b_: pl.BlockSpec((g_n, DN_HEADS, a_, b_), lambda i: (rev(i), 0, 0, 0))
    return _pcall(
        body, name="delta_scan_bwd", grid=(steps,),
        in_specs=[hm(c, LANES)] * 3 + [hm(c, c), hm(SUBLANES, LANES), hm(c, LANES), hm(DN_DK, DN_DK),
                  pl.BlockSpec((g_n * c, D_MODEL), lambda i: (rev(i), 0))],
        out_specs=[hm(c, LANES), hm(c, LANES), hm(SUBLANES, LANES)],
        out_shape=[jax.ShapeDtypeStruct((n_chunks, DN_HEADS, c, LANES), MXU),
                   jax.ShapeDtypeStruct((n_chunks, DN_HEADS, c, LANES), F32),
                   jax.ShapeDtypeStruct((n_chunks, DN_HEADS, SUBLANES, LANES), F32)],
        scratch_shapes=[pltpu.VMEM((DN_HEADS, DN_DK, DN_DK), F32)],
        compiler_params=_params("arbitrary"),
    )(w, qd, kd, aqk, dl, vnew, st, do)


def _delta_post_bwd(q, k, v, bg, t2, st, vnew, do, dvnew, dkd, ddl):
    s = q.shape[0]
    c = DN_CHUNK

    def body(q_ref, k_ref, v_ref, bg_ref, t2_ref, st_ref, vnew_ref, do_ref, dvnew_ref, dkd_ref, ddl_ref,
             dq_ref, dk_ref, dv_ref, dbg_ref):
        masks = _pair_masks()
        first = masks["first"][None]
        dec = _pair_decay(bg_ref[...], masks)
        st_ = lambda xs: jnp.stack(xs, axis=0)
        heads = range(DN_HEADS)
        qm_, km_, vm_, dom = (st_([r[:, _head_cols(h)] for h in heads]) for r in (q_ref, k_ref, v_ref, do_ref))
        beta = st_([dec[h][0] for h in heads])
        gcb = st_([dec[h][1] for h in heads])
        gam = st_([dec[h][2] for h in heads])
        gl = st_([dec[h][3] for h in heads])
        pair = lambda ref: jnp.concatenate([ref[0], ref[1]], axis=1)
        vnew2, dvnew2, dkd2 = pair(vnew_ref), pair(dvnew_ref), pair(dkd_ref)
        halves = lambda x: (x[:, :c, :], x[:, c:, :])
        by_state = lambda x: jnp.concatenate([_bmm_nt(xh, st_ref[i]) for i, xh in enumerate(halves(x))], axis=1)
        dqd = by_state(dom)
        dw = -by_state(dvnew2)
        ddl2 = jnp.where(first, ddl_ref[0][:, 0:1, :], ddl_ref[1][:, 0:1, :])

        eg = jnp.exp(gcb)
        egl = jnp.exp(gl - gcb)
        dl = jnp.exp(gl)
        kb = km_ * beta
        kk = _bmm_nt(kb, km_)
        a = jnp.where(masks["strict"][None], kk * gam, 0.0)
        t = t2_ref[0]
        vb = vm_ * beta
        kbe = kb * eg
        u = _bmm(t, vb)
        w = _bmm(t, kbe)
        aqk = _bmm_nt(qm_, km_) * gam
        qd = qm_ * eg
        kd = km_ * egl

        daqk = jnp.where(masks["causal"][None], _bmm_nt(dom, vnew2), 0.0)
        dvb = _bmm_tn(t, dvnew2)
        dkbe = _bmm_tn(t, dw)
        da = jnp.where(masks["strict"][None], -(_bmm_nt(dvb, u) + _bmm_nt(dkbe, w)), 0.0)
        pm = da * gam
        qmm = daqk * gam
        dkb = _bmm(pm, km_) + dkbe * eg
        dkh = _bmm_tn(pm, kb) + _bmm_tn(qmm, qm_) + dkd2 * egl + dkb * beta
        dqh = _bmm(qmm, km_) + dqd * eg
        xm = da * a + daqk * aqk
        col_rows = jnp.concatenate([jnp.zeros((DN_HEADS, PAIR), F32), jnp.sum(xm, axis=1),
                                    jnp.zeros((PAIR - 2 * DN_HEADS, PAIR), F32)], axis=0)
        tmp = jnp.sum(dkd2 * kd, axis=-1, keepdims=True)
        dgc = (jnp.sum(xm, axis=-1, keepdims=True) + jnp.sum(dkbe * kbe, axis=-1, keepdims=True)
               + jnp.sum(dqd * qd, axis=-1, keepdims=True) - tmp)
        sum0 = jnp.sum(jnp.where(first, tmp, 0.0), axis=1, keepdims=True)
        sum1 = jnp.sum(jnp.where(first, 0.0, tmp), axis=1, keepdims=True)
        dgl = jnp.where(first, sum0, sum1) + ddl2 * dl
        last = (masks["row"] == c - 1) | (masks["row"] == PAIR - 1)
        dgc = dgc + jnp.where(last[None], dgl, 0.0)
        dbeta = jnp.sum(dvb * vm_, axis=-1, keepdims=True) + jnp.sum(dkb * km_, axis=-1, keepdims=True)
        dvh = dvb * beta

        lane = masks["lane"]
        dgc_lanes = jnp.zeros((PAIR, LANES), F32)
        dbg = jnp.zeros((PAIR, LANES), F32)
        for h in heads:
            dq_ref[:, _head_cols(h)] = dqh[h]
            dk_ref[:, _head_cols(h)] = dkh[h]
            dv_ref[:, _head_cols(h)] = dvh[h]
            dgc_lanes = dgc_lanes + jnp.where(lane == DN_HEADS + h, dgc[h], 0.0)
            dbg = dbg + jnp.where(lane == h, dbeta[h], 0.0)
        dbg_ref[...] = dbg + _dot01(masks["upper"].astype(F32), dgc_lanes - col_rows.T)

    n_pairs = s // PAIR
    row_spec = lambda w_: pl.BlockSpec((PAIR, w_), lambda i: (i, 0))
    hm = lambda a_, b_: pl.BlockSpec((2, DN_HEADS, a_, b_), lambda i: (i, 0, 0, 0))
    return _pcall(
        body, name="delta_post_bwd", grid=(n_pairs,),
        in_specs=[row_spec(D_MODEL)] * 3 + [row_spec(LANES), pl.BlockSpec((1, DN_HEADS, PAIR, PAIR), lambda i: (i, 0, 0, 0)),
                  hm(DN_DK, DN_DK), hm(c, LANES), row_spec(D_MODEL), hm(c, LANES), hm(c, LANES), hm(SUBLANES, LANES)],
        out_specs=[row_spec(D_MODEL)] * 3 + [row_spec(LANES)],
        out_shape=[jax.ShapeDtypeStruct((s, D_MODEL), F32)] * 3 + [jax.ShapeDtypeStruct((s, LANES), F32)],
        compiler_params=_params("parallel"),
    )(q, k, v, bg, t2, st, vnew, do, dvnew, dkd, ddl)


def _alibi_slope(group, head):
    n = N_DIL * DIL_HEADS
    return float(2.0 ** (-8.0 * (group * DIL_HEADS + head + 1) / n))


def _attn_plan(s, group):
    window, dil = DIL_GROUPS[group]
    assert window // dil == ATT_BLOCK
    assert (s // dil) % ATT_BLOCK == 0, "sub-sequence length must be a whole number of attention blocks"
    return dil, s // dil // ATT_BLOCK, (DIL_HEADS if dil == 1 else 1)


def _attn_specs(group, dil, nb, hp):
    rows = ATT_BLOCK * dil

    def spec(col0, shift):
        if shift < 0:
            f = lambda hb, n: (jnp.maximum(n - 1, 0), col0 + hb)
        elif shift > 0:
            f = lambda hb, n: (jnp.minimum(n + 1, nb - 1), col0 + hb)
        else:
            f = lambda hb, n: (jnp.minimum(n, nb - 1), col0 + hb)
        return pl.BlockSpec((rows, hp * LANES), f)

    return (lambda shift: spec(group * (DIL_HEADS // hp), shift)), (lambda shift: spec(0, shift))


def _sub_rows(ref, r, dil, cols):
    return ref[:, cols] if dil == 1 else ref[pl.ds(r, ATT_BLOCK, stride=dil), cols]


def _set_sub_rows(ref, r, dil, cols, value):
    if dil == 1:
        ref[:, cols] = value
    else:
        ref[pl.ds(r, ATT_BLOCK, stride=dil), cols] = value


def _step_slope(group, hp, hh):
    if hp == DIL_HEADS:
        return _alibi_slope(group, hh)
    hb = pl.program_id(0)
    slope = _alibi_slope(group, DIL_HEADS - 1)
    for h in reversed(range(DIL_HEADS - 1)):
        slope = jnp.where(hb == h, _alibi_slope(group, h), slope)
    return slope


def _attn_items(hp, dil):
    return [(hh, r) for hh in range(hp) for r in range(dil)]


def _attn_stack(ref, items, dil, dtype=MXU):
    return jnp.stack([_sub_rows(ref, r, dil, _head_cols(hh)).astype(dtype) for hh, r in items], axis=0)


def _attn_slopes(group, hp, items):
    if hp == 1:
        return _step_slope(group, hp, 0)
    return jnp.stack([jnp.full((1, 1), _alibi_slope(group, hh), F32) for hh, _ in items], axis=0)


def _window_bias(dil, n):
    a = lax.broadcasted_iota(jnp.int32, (ATT_BLOCK, 2 * ATT_BLOCK), 0)
    b = lax.broadcasted_iota(jnp.int32, (ATT_BLOCK, 2 * ATT_BLOCK), 1)
    dist = ATT_BLOCK + a - b
    valid = (dist >= 0) & (dist <= ATT_BLOCK) & ((b >= ATT_BLOCK) | (n > 0))
    return (dist * dil).astype(F32), valid


def _attn_fwd(qb, kb, vb, group):
    s = qb.shape[0]
    dil, nb, hp = _attn_plan(s, group)
    qkv, per_head = _attn_specs(group, dil, nb, hp)

    def body(q_ref, kp_ref, kc_ref, vp_ref, vc_ref, o_ref, lse_ref):
        n = pl.program_id(1)
        distd, valid = _window_bias(dil, n)
        items = _attn_items(hp, dil)
        sub = lambda ref: _attn_stack(ref, items, dil)
        kk = jnp.concatenate([sub(kp_ref), sub(kc_ref)], axis=1)
        vv = jnp.concatenate([sub(vp_ref), sub(vc_ref)], axis=1)
        sc = _bmm_nt(sub(q_ref), kk) * DIL_DH ** -0.5 - _attn_slopes(group, hp, items) * distd
        sc = jnp.where(valid, sc, -1e30)
        mx = jnp.max(sc, axis=-1, keepdims=True)
        p = jnp.where(valid, jnp.exp(sc - mx), 0.0)
        den = jnp.sum(p, axis=-1, keepdims=True)
        out = _bmm(p, vv) / den
        lse = mx + jnp.log(den)
        for b, (hh, r) in enumerate(items):
            _set_sub_rows(o_ref, r, dil, _head_cols(hh), out[b])
            _set_sub_rows(lse_ref, r, dil, _head_cols(hh), jnp.broadcast_to(lse[b], (ATT_BLOCK, LANES)))

    return _pcall(
        body, name=f"attn_fwd_g{group}", grid=(DIL_HEADS // hp, nb),
        in_specs=[qkv(0), qkv(-1), qkv(0), qkv(-1), qkv(0)], out_specs=[per_head(0)] * 2,
        out_shape=[jax.ShapeDtypeStruct((s, DIL_W), F32)] * 2,
        compiler_params=_params("parallel", "parallel"),
    )(qb, kb, kb, vb, vb)


def _attn_bwd(qb, kb, vb, d_o, lse, delta, group):
    s = qb.shape[0]
    dil, nb, hp = _attn_plan(s, group)
    qkv, per_head = _attn_specs(group, dil, nb, hp)
    scale = DIL_DH ** -0.5

    def body(q_ref, kp_ref, kc_ref, vp_ref, vc_ref, do_ref, l_ref, dl_ref, dq_ref, dk_ref, dv_ref,
             dq_acc, dk_done, dv_done, dk_carry, dv_carry):
        n = pl.program_id(1)
        items = _attn_items(hp, dil)
        slopes = _attn_slopes(group, hp, items)

        @pl.when(n == 0)
        def _():
            dk_carry[...] = jnp.zeros_like(dk_carry)
            dv_carry[...] = jnp.zeros_like(dv_carry)

        @pl.when(n < nb)
        def _():
            distd, valid = _window_bias(dil, n)
            sub = lambda ref, dtype=MXU: _attn_stack(ref, items, dil, dtype)
            qc, do = sub(q_ref), sub(do_ref)
            kk = jnp.concatenate([sub(kp_ref), sub(kc_ref)], axis=1)
            vv = jnp.concatenate([sub(vp_ref), sub(vc_ref)], axis=1)
            sc = _bmm_nt(qc, kk) * scale - slopes * distd
            p = jnp.where(valid, jnp.exp(jnp.minimum(sc - jnp.concatenate([sub(l_ref, F32)] * 2, axis=2), 0.0)), 0.0)
            dsc = p * (_bmm_nt(do, vv) - jnp.concatenate([sub(dl_ref, F32)] * 2, axis=2))
            dq = _bmm(dsc, kk) * scale
            dkk = _bmm_tn(dsc, qc) * scale
            dvv = _bmm_tn(p, do)
            for b, (hh, r) in enumerate(items):
                cols = _head_cols(hh)
                _set_sub_rows(dq_acc, r, dil, cols, dq[b])
                _set_sub_rows(dk_done, r, dil, cols, _sub_rows(dk_carry, r, dil, cols) + dkk[b, :ATT_BLOCK])
                _set_sub_rows(dv_done, r, dil, cols, _sub_rows(dv_carry, r, dil, cols) + dvv[b, :ATT_BLOCK])
                _set_sub_rows(dk_carry, r, dil, cols, dkk[b, ATT_BLOCK:])
                _set_sub_rows(dv_carry, r, dil, cols, dvv[b, ATT_BLOCK:])
            dq_ref[...] = dq_acc[...].astype(dq_ref.dtype)
            dk_ref[...] = dk_done[...].astype(dk_ref.dtype)
            dv_ref[...] = dv_done[...].astype(dv_ref.dtype)

        @pl.when(n == nb)
        def _():
            dk_ref[...] = dk_carry[...].astype(dk_ref.dtype)
            dv_ref[...] = dv_carry[...].astype(dv_ref.dtype)

    return _pcall(
        body, name=f"attn_bwd_g{group}", grid=(DIL_HEADS // hp, nb + 1),
        in_specs=[qkv(0), qkv(-1), qkv(0), qkv(-1), qkv(0)] + [per_head(0)] * 3,
        out_specs=[per_head(0), per_head(-1), per_head(-1)],
        out_shape=[jax.ShapeDtypeStruct((s, DIL_W), MXU)] * 3,
        scratch_shapes=[pltpu.VMEM((ATT_BLOCK * dil, hp * LANES), F32)] * 5,
        compiler_params=_params("parallel", "arbitrary"),
    )(qb, kb, kb, vb, vb, d_o, lse, delta)


def _my_place():
    mx, my, mc = lax.axis_index("x"), lax.axis_index("y"), lax.axis_index("c")
    return mx, my, mc, 4 * mx + 2 * my + mc


N_CHIPS = 4


def _shard_row_tile(r):
    if r <= 512:
        return r
    return 128 if r % 128 == 0 else 480


def _other_chips(mx, my):
    return [(1 - mx, my), (mx, 1 - my), (1 - mx, 1 - my)]


def _all_gather(xs, name):
    n = len(xs)
    halved = [x.shape[1] % (2 * LANES) == 0 and x.size * x.dtype.itemsize >= (1 << 20) for x in xs]
    n_sems = 8

    def body(*refs):
        x_refs, o_refs = refs[:n], refs[n:2 * n]
        send_sems, recv_sems, local_sems = refs[2 * n:]
        mx, my, mc, me = _my_place()
        sibling, sibling_id = (mx, my, 1 - mc), 4 * mx + 2 * my + (1 - mc)
        x_nbr, y_nbr, diag = _other_chips(mx, my)
        slot_of = lambda chip, c: 4 * chip[0] + 2 * chip[1] + c

        def part(ref, a, half):
            if not halved[a]:
                return ref
            width = xs[a].shape[1] // 2
            return ref.at[:, pl.ds(half * width, width)]

        def copy(a, k, dst, to, src=None):
            return pltpu.make_async_remote_copy(
                src_ref=dst if src is None else src, dst_ref=dst, send_sem=send_sems.at[a, k],
                recv_sem=recv_sems.at[a, k], device_id=to, device_id_type=MESH)

        local = [pltpu.make_async_copy(x_refs[a], o_refs[a].at[me], local_sems.at[a]) for a in range(n)]
        for cp in local:
            cp.start()
        sends = []
        for a in range(n):
            mine = o_refs[a].at[me]
            sends += [copy(a, 0, mine, sibling, src=x_refs[a]), copy(a, 1, mine, (*x_nbr, mc), src=x_refs[a]),
                      copy(a, 2, mine, (*y_nbr, mc), src=x_refs[a])]
        for cp in sends:
            cp.start()
        for a in range(n):
            blk = o_refs[a].at[slot_of(x_nbr, mc)]
            copy(a, 1, blk, (*x_nbr, mc)).wait_recv()
            sends += [copy(a, 3, blk, sibling), copy(a, 5, part(blk, a, 0), (*y_nbr, mc))]
            sends[-2].start()
            sends[-1].start()
        for a in range(n):
            blk = o_refs[a].at[slot_of(y_nbr, mc)]
            copy(a, 2, blk, (*y_nbr, mc)).wait_recv()
            sends.append(copy(a, 4, blk, sibling))
            sends[-1].start()
            if halved[a]:
                sends.append(copy(a, 6, part(blk, a, 1), (*x_nbr, mc)))
                sends[-1].start()
        for a in range(n):
            blk = o_refs[a].at[slot_of(diag, mc)]
            copy(a, 5, part(blk, a, 0), (*y_nbr, mc)).wait_recv()
            if halved[a]:
                copy(a, 6, part(blk, a, 1), (*x_nbr, mc)).wait_recv()
            sends.append(copy(a, 7, blk, sibling))
            sends[-1].start()
        for a in range(n):
            copy(a, 0, o_refs[a].at[sibling_id], sibling).wait_recv()
            for k, chip in ((3, x_nbr), (4, y_nbr), (7, diag)):
                copy(a, k, o_refs[a].at[slot_of(chip, 1 - mc)], sibling).wait_recv()
        for cp in sends:
            cp.wait_send()
        for cp in local:
            cp.wait()

    any_spec = pl.BlockSpec(memory_space=pl.ANY)
    return _pcall(
        body, name=name,
        in_specs=[any_spec] * n, out_specs=[any_spec] * n,
        out_shape=[jax.ShapeDtypeStruct((N_DEV,) + x.shape, x.dtype) for x in xs],
        scratch_shapes=[pltpu.SemaphoreType.DMA((n, n_sems)), pltpu.SemaphoreType.DMA((n, n_sems)),
                        pltpu.SemaphoreType.DMA((n,))],
    )(*xs)


def _pair_exchange(gs, name):
    n = len(gs)

    def body(*refs):
        g_refs, o_refs = refs[:n], refs[n:2 * n]
        send_sems, recv_sems = refs[2 * n:]
        mx, my, mc, _ = _my_place()
        copies = [pltpu.make_async_remote_copy(
            src_ref=g_refs[a].at[p, 1 - mc], dst_ref=o_refs[a].at[p], send_sem=send_sems.at[a, p],
            recv_sem=recv_sems.at[a, p], device_id=(mx, my, 1 - mc), device_id_type=MESH)
            for a in range(n) for p in range(N_CHIPS)]
        for cp in copies:
            cp.start()
        for cp in copies:
            cp.wait()

    any_spec = pl.BlockSpec(memory_space=pl.ANY)
    return _pcall(
        body, name=name,
        in_specs=[any_spec] * n, out_specs=[any_spec] * n,
        out_shape=[jax.ShapeDtypeStruct((N_CHIPS,) + g.shape[2:], g.dtype) for g in gs],
        scratch_shapes=[pltpu.SemaphoreType.DMA((n, N_CHIPS)), pltpu.SemaphoreType.DMA((n, N_CHIPS))],
    )(*gs)


def _pair_add(g, other, name):
    chips, _, r, c = g.shape
    tr = _shard_row_tile(r)
    core = lax.axis_index("c").astype(jnp.int32).reshape(1)

    def body(core_ref, g_ref, o_ref, h_ref):
        h_ref[...] = (g_ref[...].astype(F32)[0] + o_ref[...].astype(F32)).astype(h_ref.dtype)

    blk = pl.BlockSpec((1, tr, c), lambda p, i, core_ref: (p, i, 0))
    return _pcall(
        body, name=name,
        grid_spec=pltpu.PrefetchScalarGridSpec(
            num_scalar_prefetch=1, grid=(chips, pl.cdiv(r, tr)),
            in_specs=[pl.BlockSpec((1, 1, tr, c), lambda p, i, core_ref: (p, core_ref[0], i, 0)), blk],
            out_specs=blk),
        out_shape=jax.ShapeDtypeStruct((chips, r, c), g.dtype),
        compiler_params=_params("parallel", "parallel"),
    )(core, g, other)


def _chip_exchange(hs, name):
    n = len(hs)

    def body(*refs):
        h_refs, o_refs = refs[:n], refs[n:2 * n]
        send_sems, recv_sems, local_sems = refs[2 * n:]
        mx, my, mc, _ = _my_place()
        my_chip = 2 * mx + my
        chips = _other_chips(mx, my)
        local = [pltpu.make_async_copy(h_refs[a].at[my_chip], o_refs[a].at[my_chip], local_sems.at[a]) for a in range(n)]
        for cp in local:
            cp.start()
        for j, (px, py) in enumerate(chips):
            for a in range(n):
                pltpu.make_async_remote_copy(
                    src_ref=h_refs[a].at[2 * px + py], dst_ref=o_refs[a].at[my_chip], send_sem=send_sems.at[a, j],
                    recv_sem=recv_sems.at[a, j], device_id=(px, py, mc), device_id_type=MESH).start()
        for j, (px, py) in enumerate(chips):
            for a in range(n):
                pltpu.make_async_remote_copy(
                    src_ref=h_refs[a].at[2 * px + py], dst_ref=o_refs[a].at[2 * px + py], send_sem=send_sems.at[a, j],
                    recv_sem=recv_sems.at[a, j], device_id=(px, py, mc), device_id_type=MESH).wait()
        for cp in local:
            cp.wait()

    any_spec = pl.BlockSpec(memory_space=pl.ANY)
    return _pcall(
        body, name=name,
        in_specs=[any_spec] * n, out_specs=[any_spec] * n,
        out_shape=[jax.ShapeDtypeStruct(h.shape, h.dtype) for h in hs],
        scratch_shapes=[pltpu.SemaphoreType.DMA((n, N_CHIPS - 1)), pltpu.SemaphoreType.DMA((n, N_CHIPS - 1)),
                        pltpu.SemaphoreType.DMA((n,))],
    )(*hs)


def _adamw(parts, w, m, v, name):
    r, c = w.shape
    n_parts = parts.shape[0]
    tr = _shard_row_tile(r)
    bc1 = 1.0 - ADAM_B1 ** ADAM_STEP
    bc2 = 1.0 - ADAM_B2 ** ADAM_STEP

    def body(p_ref, w_ref, m_ref, v_ref, g_ref, d_ref, nm_ref, nv_ref):
        g = p_ref[0].astype(F32)
        for j in range(1, n_parts):
            g = g + p_ref[j].astype(F32)
        nm = ADAM_B1 * m_ref[...] + (1.0 - ADAM_B1) * g
        nv = ADAM_B2 * v_ref[...] + (1.0 - ADAM_B2) * (g * g)
        g_ref[...] = g
        nm_ref[...] = nm
        nv_ref[...] = nv
        d_ref[...] = -ADAM_LR * ((nm / bc1) / (jnp.sqrt(nv / bc2) + ADAM_EPS) + ADAM_WD * w_ref[...])

    blk = pl.BlockSpec((tr, c), lambda i: (i, 0))
    return _pcall(
        body, name=name, grid=(pl.cdiv(r, tr),),
        in_specs=[pl.BlockSpec((n_parts, tr, c), lambda i: (0, i, 0)), blk, blk, blk],
        out_specs=[blk] * 4, out_shape=[jax.ShapeDtypeStruct((r, c), F32)] * 4,
        compiler_params=_params("parallel"),
    )(parts, w, m, v)


def _local_step(x, target, norm_w, w_segs, conv_w, a_log, dt_bias, dn_norm_w, w_o_dn, w_o_dil, w_out, final_norm_w):
    s = x.shape[0]
    w_qkv, w_za, w_ba, w_qb, w_kb, w_vb, w_zb, w_ga, w_gb = w_segs
    conv_w8 = jnp.concatenate([conv_w, jnp.zeros((SUBLANES - conv_w.shape[0], QKV_W), F32)], axis=0)
    pad8 = jnp.zeros((1, DN_HEADS), F32)
    alog_row = jnp.concatenate([pad8, a_log, jnp.zeros((1, LANES - 2 * DN_HEADS), F32)], axis=1)
    dtb_row = jnp.concatenate([pad8, dt_bias, jnp.zeros((1, LANES - 2 * DN_HEADS), F32)], axis=1)
    wf_row = final_norm_w.reshape(1, D_MODEL)

    hb, qkv_pre, z_a, ba, z_b = _rms_proj_fwd(x, norm_w, [w_qkv, w_za, w_ba, w_zb], "rms_proj_fwd_a")
    q_b, k_b, v_b, g_a, g_b = _mm_out(hb, [w_qb, w_kb, w_vb, w_ga, w_gb], "proj_fwd_b", w_is_out_by_in=True)

    qn, kn, vn, bg = _dn_prep_fwd(qkv_pre, ba, conv_w8, alog_row, dtb_row)
    u_d, w_d, qd_d, kd_d, aqk_d, dl_d, t2_d = _delta_prep(qn, kn, vn, bg)
    o_a, vnew_d, st_d = _delta_scan_fwd(u_d, w_d, qd_d, kd_d, aqk_d, dl_d)
    on_b, y_a = _dn_out_fwd(o_a, z_a, dn_norm_w, w_o_dn)

    parts, lses = [], []
    for gi in range(N_DIL):
        o_g, l_g = _attn_fwd(q_b, k_b, v_b, gi)
        parts.append(o_g)
        lses.append(l_g)
    lse, o_joint, ob_b, y_b = _attn_out_fwd(parts, lses, z_b, w_o_dil)

    loss8, dwf8, merged_b, dx2_b, dx2, dya_b, dyb_b, dga_b, dgb_b = _merge_out_final(
        g_a, g_b, y_a, y_b, x, target, w_out, wf_row)

    g_w_out = _mm_tn(merged_b, dx2_b, "out_wgrad")
    g_w_o_dn = _mm_tn(on_b, dya_b, "out_dn_wgrad")
    d_o_a, dza_b, ddnw8 = _dn_out_bwd(dya_b, o_a, z_a, dn_norm_w, w_o_dn)

    g_w_o_dil = _mm_tn(ob_b, dyb_b, "out_dil_wgrad")
    d_o, dzb_b, delta = _attn_out_bwd(dyb_b, o_joint, z_b, w_o_dil)
    dqs, dks, dvs = [], [], []
    for gi in range(N_DIL):
        dq_g, dk_g, dv_g = _attn_bwd(q_b, k_b, v_b, d_o, lse, delta, gi)
        dqs.append(dq_g)
        dks.append(dk_g)
        dvs.append(dv_g)

    dvnew_d, dkd_d, ddl_d = _delta_scan_bwd(w_d, qd_d, kd_d, aqk_d, dl_d, vnew_d, st_d, d_o_a)
    dqn, dkn, dvn, dbg = _delta_post_bwd(qn, kn, vn, bg, t2_d, st_d, vnew_d, d_o_a, dvnew_d, dkd_d, ddl_d)
    dc, dba_b, dsmall8 = _dn_prep_bwd(qkv_pre, ba, conv_w8, alog_row, dtb_row, dqn, dkn, dvn, dbg)
    dqkv_b, dconv8 = _conv_bwd(dc, qkv_pre, conv_w8)

    per_group = lambda w: [w[g * DIL_W:(g + 1) * DIL_W] for g in range(N_DIL)]
    dh_b = _mm_in(dqs + dks + dvs + [dga_b, dgb_b],
                  per_group(w_qb) + per_group(w_kb) + per_group(w_vb) + [w_ga, w_gb], "proj_bwd_b", w_is_out_by_in=True)
    dsegs = [dqkv_b, dza_b, dba_b] + dqs + dks + dvs + [dzb_b, dga_b, dgb_b]
    valid_rows = [d.shape[1] for d in dsegs]
    valid_rows[2] = 2 * DN_HEADS
    g_wt = _proj_wgrad_all(dsegs, valid_rows, hb)
    grad_x, dnw8 = _proj_bwd_rms_in([dqkv_b, dza_b, dba_b, dzb_b], [w_qkv, w_za, w_ba, w_zb], dh_b, x, dx2, norm_w)

    small = dict(norm_w=dnw8[0:1], final_norm_w=dwf8[0:1], dn_norm_w=ddnw8[0:1],
                 a_log=dsmall8[0:1, DN_HEADS:2 * DN_HEADS], dt_bias=dsmall8[1:2, DN_HEADS:2 * DN_HEADS])
    return loss8[0:1, 0:1], grad_x, g_wt, dconv8[0:4], g_w_o_dn, g_w_o_dil, g_w_out, small


def _proj_bwd_rms_in(ds, ws, dh_a, x, dx2, norm_w):
    n_seg = len(ds)

    def body(*refs):
        d_refs, w_refs = refs[:n_seg], refs[n_seg:2 * n_seg]
        da_ref, x_ref, dx2_ref, w_ref, dx_ref, dw_ref = refs[2 * n_seg:]
        dx_ref[...] = da_ref[...]
        for d_ref, wt_ref in zip(d_refs, w_refs):
            for c, wd in _col_chunks(d_ref.shape[1], 1024):
                dx_ref[...] += jnp.dot(d_ref[:, c:c + wd], wt_ref[c:c + wd, :], preferred_element_type=F32)
        xv = x_ref[...]
        r = lax.rsqrt(jnp.mean(xv * xv, axis=-1, keepdims=True) + NORM_EPS)
        dhv = dx_ref[...]
        dn = dhv * w_ref[...]
        dx_ref[...] = dx2_ref[...] + r * dn - xv * (r * r * r) * jnp.mean(dn * xv, axis=-1, keepdims=True)
        row = jnp.sum(dhv * xv * r, axis=0, keepdims=True)
        _acc_add(dw_ref, jnp.concatenate([row, jnp.zeros((SUBLANES - 1, row.shape[1]), F32)], axis=0))

    return _rows_call(body, "proj_bwd_b_rms_in", x.shape[0],
                      [(d, "tile") for d in ds] + [(w, "full") for w in ws]
                      + [(dh_a, "tile"), (x, "tile"), (dx2, "tile"), (norm_w, "full")],
                      [(x.shape, F32, "tile"), ((SUBLANES, x.shape[1]), F32, "acc")])


def _split_proj_rows(w_shards):
    n_shards, rows, k = w_shards.shape
    wt_full = w_shards.reshape(n_shards * rows, k)
    offs = [0]
    for n in PROJ_SIZES:
        offs.append(offs[-1] + n)
    seg = lambda a, b: wt_full[offs[a]:offs[b]]
    w_ba = jnp.concatenate([seg(4, 6), jnp.zeros((LANES - 2 * DN_HEADS, k), wt_full.dtype)], axis=0)
    return [seg(0, 3), seg(3, 4), w_ba, seg(6, 7), seg(7, 8), seg(8, 9), seg(9, 10), seg(10, 11), seg(11, 12)]


LOSS_ROW = 5


def _pack_small(norm_w, final_norm_w, dn_norm_w, a_log, dt_bias, loss=None):
    pad = lambda r: jnp.concatenate([r, jnp.zeros((1, D_MODEL - r.shape[1]), F32)], axis=1)
    rows = [pad(norm_w.reshape(1, -1)), pad(final_norm_w.reshape(1, -1)), pad(dn_norm_w.reshape(1, -1)),
            pad(a_log.reshape(1, -1)), pad(dt_bias.reshape(1, -1)),
            pad(jnp.zeros((1, 1), F32) if loss is None else loss.reshape(1, 1)),
            jnp.zeros((SUBLANES - LOSS_ROW - 1, D_MODEL), F32)]
    return jnp.concatenate(rows, axis=0)


def _unpack_small(p):
    return dict(norm_w=p[0:1], final_norm_w=p[1], dn_norm_w=p[2:3, :DN_DK], a_log=p[3:4, :DN_HEADS],
                dt_bias=p[4:5, :DN_HEADS])


def kernel(x, norm_w, w_in, conv_w, a_log, dt_bias, dn_norm_w, w_o_dn, w_o_dil, w_out, final_norm_w, loss_target, m_norm_w, m_w_in, m_conv_w, m_a_log, m_dt_bias, m_dn_norm_w, m_w_o_dn, m_w_o_dil, m_w_out, m_final_norm_w, v_norm_w, v_w_in, v_conv_w, v_a_log, v_dt_bias, v_dn_norm_w, v_w_o_dn, v_w_o_dil, v_w_out, v_final_norm_w):
    shard_w = w_in.shape[2]
    wt, m_wt, v_wt = (jnp.transpose(t[0]) for t in (w_in, m_w_in, v_w_in))
    gathered = _all_gather([wt.astype(MXU), w_o_dn[0].astype(MXU), w_o_dil[0].astype(MXU), w_out[0].astype(MXU),
                            conv_w[0]], "gather_weights")
    w_in_all, w_o_dn_all, w_o_dil_all, w_out_all, conv_all = gathered
    w_o_dn_full = w_o_dn_all.reshape(D_MODEL, D_MODEL)
    w_o_dil_full = jnp.transpose(w_o_dil_all, (1, 0, 2)).reshape(DIL_W, D_MODEL)
    w_out_full = w_out_all.reshape(D_MODEL, D_MODEL)
    conv_full = jnp.transpose(conv_all, (1, 0, 2)).reshape(conv_w.shape[1], QKV_W)

    loss11, grad_x, g_wt, g_conv, g_w_o_dn, g_w_o_dil, g_w_out, small = _local_step(
        x[0], loss_target[0], norm_w, _split_proj_rows(w_in_all), conv_full, a_log, dt_bias, dn_norm_w,
        w_o_dn_full, w_o_dil_full, w_out_full, final_norm_w)

    col_shards = lambda g, n: jnp.transpose(g.reshape(g.shape[0], N_DEV, n), (1, 0, 2))
    row_shards = lambda g: g.reshape(N_DEV, g.shape[0] // N_DEV, g.shape[1])
    g_wt_shards = jnp.stack([g_wt[j * shard_w:(j + 1) * shard_w] for j in range(N_DEV)], axis=0)
    sent = [g_wt_shards, row_shards(g_w_o_dn).astype(MXU),
            col_shards(g_w_o_dil, w_o_dil.shape[2]).astype(MXU), row_shards(g_w_out).astype(MXU),
            col_shards(g_conv, conv_w.shape[2])]
    sent = [g8.reshape((N_CHIPS, 2) + g8.shape[1:]) for g8 in sent]
    from_sibling = _pair_exchange(sent, "scatter_pair")
    summed = [_pair_add(g, o, f"pair_add_{i}") for i, (g, o) in enumerate(zip(sent, from_sibling))]
    p_w_in, p_w_o_dn, p_w_o_dil, p_w_out, p_conv = _chip_exchange(summed, "scatter_chips")
    p_small = _all_gather([_pack_small(small["norm_w"], small["final_norm_w"], small["dn_norm_w"], small["a_log"],
                                       small["dt_bias"], loss11)], "gather_small_grads")[0]

    res = {}
    res["w_in"] = [jnp.transpose(t) for t in _adamw(p_w_in, wt, m_wt, v_wt, "adamw_w_in")]
    res["conv_w"] = _adamw(p_conv, conv_w[0], m_conv_w[0], v_conv_w[0], "adamw_conv_w")
    res["w_o_dn"] = _adamw(p_w_o_dn, w_o_dn[0], m_w_o_dn[0], v_w_o_dn[0], "adamw_w_o_dn")
    res["w_o_dil"] = _adamw(p_w_o_dil, w_o_dil[0], m_w_o_dil[0], v_w_o_dil[0], "adamw_w_o_dil")
    res["w_out"] = _adamw(p_w_out, w_out[0], m_w_out[0], v_w_out[0], "adamw_w_out")
    small_res = _adamw(p_small, _pack_small(norm_w, final_norm_w, dn_norm_w, a_log, dt_bias),
                       _pack_small(m_norm_w, m_final_norm_w, m_dn_norm_w, m_a_log, m_dt_bias),
                       _pack_small(v_norm_w, v_final_norm_w, v_dn_norm_w, v_a_log, v_dt_bias), "adamw_small")
    loss = small_res[0][LOSS_ROW, 0]
    small_res = [_unpack_small(t) for t in small_res]

    names = ["norm_w", "w_in", "conv_w", "a_log", "dt_bias", "dn_norm_w", "w_o_dn", "w_o_dil", "w_out", "final_norm_w"]
    outs = [loss, grad_x[None]]
    for kind in range(4):
        for nm in names:
            outs.append(res[nm][kind][None] if nm in res else small_res[kind][nm])
    return tuple(outs)
```

```python
import math

import jax
import jax.numpy as jnp
from jax import lax
from jax.experimental import pallas as pl
from jax.experimental.pallas import tpu as pltpu

F32 = jnp.float32
MXU = jnp.bfloat16
MESH = pl.DeviceIdType.MESH

N_DEV = 8
D_MODEL = 1024
DN_HEADS = 8
DN_DK = 128
DN_CHUNK = 64
N_DIL = 3
DIL_HEADS = 4
DIL_DH = 128
DIL_W = DIL_HEADS * DIL_DH
DIL_GROUPS = ((128, 1), (512, 4), (2048, 16))
ATT_BLOCK = 128
NORM_EPS = 1e-6
QKV_W = 3 * D_MODEL
DILQ_W = N_DIL * DIL_W
PROJ_SIZES = (1024, 1024, 1024, 1024, 8, 8, DILQ_W, DILQ_W, DILQ_W, DIL_W, D_MODEL, D_MODEL)

ADAM_LR = 0.001
ADAM_B1 = 0.9
ADAM_B2 = 0.999
ADAM_EPS = 1e-08
ADAM_WD = 0.01
ADAM_STEP = 10

ROW_TILE = 256
LANES = 128
SUBLANES = 8
VMEM_LIMIT = 48 << 20


def _pcall(body, **kw):
    return pl.pallas_call(body, **kw)


def _params(*sem):
    return pltpu.CompilerParams(dimension_semantics=tuple(sem), vmem_limit_bytes=VMEM_LIMIT)


def _sigmoid(x):
    return 1.0 / (1.0 + jnp.exp(-x))


def _softplus(x):
    return jnp.maximum(x, 0.0) + jnp.log(1.0 + jnp.exp(-jnp.abs(x)))


def _dot(a, b):
    return jnp.dot(a.astype(MXU), b.astype(MXU), preferred_element_type=F32)


def _dot_nt(a, b):
    return lax.dot_general(a.astype(MXU), b.astype(MXU), (((1,), (1,)), ((), ())), preferred_element_type=F32)


def _dot_tn(a, b):
    return lax.dot_general(a.astype(MXU), b.astype(MXU), (((0,), (0,)), ((), ())), preferred_element_type=F32)


def _split3(x):
    hi = x.astype(jnp.bfloat16)
    r1 = x - hi.astype(F32)
    mid = r1.astype(jnp.bfloat16)
    lo = (r1 - mid.astype(F32)).astype(jnp.bfloat16)
    return hi, mid, lo


def _dot01(m01, x):
    m = m01.astype(jnp.bfloat16)
    hi, mid, lo = _split3(x)
    f = lambda p: jnp.dot(m, p, preferred_element_type=F32)
    return f(hi) + (f(mid) + f(lo))


def _rows_call(body, name, n_rows, ins, outs, scratch=(), tm=ROW_TILE):
    steps = n_rows // tm
    per8 = tm // SUBLANES
    last8 = n_rows // SUBLANES - 1
    in_specs = []
    for arr, kind in ins:
        cols = arr.shape[-1]
        if kind == "tile":
            in_specs.append(pl.BlockSpec((tm, cols), lambda i: (i, 0)))
        elif kind == "full":
            in_specs.append(pl.BlockSpec(arr.shape, lambda i, nd=arr.ndim: (0,) * nd))
        elif kind == "prev8":
            in_specs.append(pl.BlockSpec((SUBLANES, cols), lambda i: (jnp.maximum(i * per8 - 1, 0), 0)))
        elif kind == "next8":
            in_specs.append(pl.BlockSpec((SUBLANES, cols), lambda i: (jnp.minimum((i + 1) * per8, last8), 0)))
        else:
            raise ValueError(kind)
    out_specs, out_shape, has_acc = [], [], False
    for shape, dtype, kind in outs:
        out_shape.append(jax.ShapeDtypeStruct(shape, dtype))
        if kind == "tile":
            out_specs.append(pl.BlockSpec((tm, shape[-1]), lambda i: (i, 0)))
        else:
            has_acc = True
            out_specs.append(pl.BlockSpec(shape, lambda i: (0, 0)))
    return _pcall(
        body, name=name, grid=(steps,), in_specs=in_specs, out_specs=out_specs, out_shape=out_shape,
        scratch_shapes=list(scratch),
        compiler_params=_params("arbitrary" if has_acc else "parallel"),
    )(*[a for a, _ in ins])


def _acc_add(ref, value):
    @pl.when(pl.program_id(0) == 0)
    def _():
        ref[...] = jnp.zeros_like(ref)
    ref[...] += value


def _col_chunks(n, width=512):
    return [(c, min(width, n - c)) for c in range(0, n, width)]


NT_DIMS = (((1,), (1,)), ((), ()))
TN_DIMS = (((0,), (0,)), ((), ()))


def _mm_out(a, ws, name, w_is_out_by_in=False, out_dtype=F32, tm=ROW_TILE):
    m, k = a.shape
    ns = [w.shape[0] if w_is_out_by_in else w.shape[1] for w in ws]

    def body(a_ref, *refs):
        av = a_ref[...]
        for w_ref, o_ref, n in zip(refs[:len(ws)], refs[len(ws):], ns):
            for c, wd in _col_chunks(n):
                if w_is_out_by_in:
                    part = lax.dot_general(av, w_ref[c:c + wd, :], NT_DIMS, preferred_element_type=F32)
                else:
                    part = jnp.dot(av, w_ref[:, c:c + wd], preferred_element_type=F32)
                o_ref[:, c:c + wd] = part.astype(o_ref.dtype)

    return _pcall(
        body, name=name, grid=(m // tm,),
        in_specs=[pl.BlockSpec((tm, k), lambda i: (i, 0))] + [pl.BlockSpec(w.shape, lambda i: (0, 0)) for w in ws],
        out_specs=[pl.BlockSpec((tm, n), lambda i: (i, 0)) for n in ns],
        out_shape=[jax.ShapeDtypeStruct((m, n), out_dtype) for n in ns],
        compiler_params=_params("parallel"),
    )(a, *ws)


def _rms_proj_fwd(x, norm_w, wts, name, tm=ROW_TILE):
    m, k = x.shape
    ns = [w.shape[0] for w in wts]

    def body(x_ref, nw_ref, *refs):
        w_refs, h_ref, o_refs = refs[:len(wts)], refs[len(wts)], refs[len(wts) + 1:]
        xv = x_ref[...]
        r = lax.rsqrt(jnp.mean(xv * xv, axis=-1, keepdims=True) + NORM_EPS)
        hv = (xv * r * nw_ref[...]).astype(h_ref.dtype)
        h_ref[...] = hv
        for w_ref, o_ref, n in zip(w_refs, o_refs, ns):
            for c, wd in _col_chunks(n):
                o_ref[:, c:c + wd] = lax.dot_general(hv, w_ref[c:c + wd, :], NT_DIMS, preferred_element_type=F32)

    return _pcall(
        body, name=name, grid=(m // tm,),
        in_specs=[pl.BlockSpec((tm, k), lambda i: (i, 0)), pl.BlockSpec(norm_w.shape, lambda i: (0, 0))]
        + [pl.BlockSpec(w.shape, lambda i: (0, 0)) for w in wts],
        out_specs=[pl.BlockSpec((tm, k), lambda i: (i, 0))] + [pl.BlockSpec((tm, n), lambda i: (i, 0)) for n in ns],
        out_shape=[jax.ShapeDtypeStruct((m, k), MXU)] + [jax.ShapeDtypeStruct((m, n), F32) for n in ns],
        compiler_params=_params("parallel"),
    )(x, norm_w, *wts)


def _mm_in(ds, ws, name, w_is_out_by_in=False, tm=ROW_TILE):
    m = ds[0].shape[0]
    k = ws[0].shape[1] if w_is_out_by_in else ws[0].shape[0]
    ns = [d.shape[1] for d in ds]

    def body(*refs):
        d_refs, w_refs, o_ref = refs[:len(ds)], refs[len(ds):2 * len(ds)], refs[-1]
        first = True
        for d_ref, w_ref, n in zip(d_refs, w_refs, ns):
            for c, wd in _col_chunks(n, 1024):
                if w_is_out_by_in:
                    part = jnp.dot(d_ref[:, c:c + wd], w_ref[c:c + wd, :], preferred_element_type=F32)
                else:
                    part = lax.dot_general(d_ref[:, c:c + wd], w_ref[:, c:c + wd], NT_DIMS, preferred_element_type=F32)
                if first:
                    o_ref[...] = part
                    first = False
                else:
                    o_ref[...] += part

    return _pcall(
        body, name=name, grid=(m // tm,),
        in_specs=[pl.BlockSpec((tm, n), lambda i: (i, 0)) for n in ns] + [pl.BlockSpec(w.shape, lambda i: (0, 0)) for w in ws],
        out_specs=pl.BlockSpec((tm, k), lambda i: (i, 0)),
        out_shape=jax.ShapeDtypeStruct((m, k), F32),
        compiler_params=_params("parallel"),
    )(*ds, *ws)


def _mm_tn(a, d, name):
    m, k = a.shape
    n = d.shape[1]
    tk = 512 if k % 512 == 0 else k

    def body(a_ref, d_ref, o_ref):
        o_ref[...] = lax.dot_general(a_ref[...], d_ref[...], TN_DIMS, preferred_element_type=F32)

    return _pcall(
        body, name=name, grid=(k // tk,),
        in_specs=[pl.BlockSpec((m, tk), lambda p: (0, p)), pl.BlockSpec((m, n), lambda p: (0, 0))],
        out_specs=pl.BlockSpec((tk, n), lambda p: (p, 0)),
        out_shape=jax.ShapeDtypeStruct((k, n), F32),
        compiler_params=_params("parallel"),
    )(a, d)


WGRAD_TILE = 512


def _proj_wgrad_all(dsegs, valid_rows, hb):
    m, k = hb.shape
    n_seg = len(dsegs)
    tiles, row = [], 0
    for si, (d, valid) in enumerate(zip(dsegs, valid_rows)):
        for c in range(0, valid, WGRAD_TILE):
            width = min(WGRAD_TILE, d.shape[1] - c)
            tiles.append((si, c, width, row + c, min(width, valid - c)))
        row += valid
    total_rows = row

    def body(*refs):
        d_refs, hb_ref, o_ref = refs[:n_seg], refs[n_seg], refs[n_seg + 1]
        a_buf, hb_buf, o_buf, load_sems, store_sems, hb_sem = refs[n_seg + 2:]

        def load(t):
            si, c, width, _, _ = tiles[t]
            return pltpu.make_async_copy(d_refs[si].at[:, pl.ds(c, width)], a_buf.at[t % 2, :, pl.ds(0, width)],
                                         load_sems.at[t % 2])

        def stores(t):
            _, _, _, orow, valid = tiles[t]
            return [pltpu.make_async_copy(o_buf.at[t % 2, pl.ds(0, valid), :], o_ref.at[pl.ds(orow, valid), :],
                                          store_sems.at[t % 2])]

        hb_copy = pltpu.make_async_copy(hb_ref, hb_buf, hb_sem)
        hb_copy.start()
        load(0).start()
        hb_copy.wait()
        for t in range(len(tiles)):
            width = tiles[t][2]
            load(t).wait()
            if t + 1 < len(tiles):
                load(t + 1).start()
            if t >= 2:
                for cp in stores(t - 2):
                    cp.wait()
            o_buf[t % 2, 0:width, :] = lax.dot_general(a_buf[t % 2, :, 0:width], hb_buf[...], TN_DIMS,
                                                        preferred_element_type=F32).astype(o_buf.dtype)
            for cp in stores(t):
                cp.start()
        for t in range(max(len(tiles) - 2, 0), len(tiles)):
            for cp in stores(t):
                cp.wait()

    any_spec = pl.BlockSpec(memory_space=pl.ANY)
    return _pcall(
        body, name="proj_wgrad",
        in_specs=[any_spec] * (n_seg + 1), out_specs=any_spec,
        out_shape=jax.ShapeDtypeStruct((total_rows, k), hb.dtype),
        scratch_shapes=[pltpu.VMEM((2, m, WGRAD_TILE), hb.dtype), pltpu.VMEM((m, k), hb.dtype),
                        pltpu.VMEM((2, WGRAD_TILE, k), hb.dtype), pltpu.SemaphoreType.DMA((2,)),
                        pltpu.SemaphoreType.DMA((2,)), pltpu.SemaphoreType.DMA],
        compiler_params=pltpu.CompilerParams(vmem_limit_bytes=VMEM_LIMIT),
    )(*dsegs, hb)


def _conv_taps(ext_ref, cw_ref, cols, tm):
    c = None
    for j in range(4):
        term = cw_ref[3 - j:4 - j, cols] * ext_ref[SUBLANES - j:SUBLANES - j + tm, cols]
        c = term if c is None else c + term
    return c


def _fill_ext(ext_ref, u_ref, halo_ref, first):
    ext_ref[0:SUBLANES, :] = jnp.where(first, 0.0, halo_ref[...])
    ext_ref[SUBLANES:, :] = u_ref[...]


def _dn_prep_fwd(qkv_pre, ba, conv_w8, alog_row, dtb_row):
    s = qkv_pre.shape[0]
    tm = ROW_TILE

    def body(u_ref, halo_ref, cw_ref, ba_ref, al_ref, dtb_ref, q_ref, k_ref, v_ref, bg_ref, ext_ref):
        _fill_ext(ext_ref, u_ref, halo_ref, pl.program_id(0) == 0)
        for h in range(3 * DN_HEADS):
            cols = slice(h * LANES, (h + 1) * LANES)
            c = _conv_taps(ext_ref, cw_ref, cols, tm)
            a = c * _sigmoid(c)
            oc = slice((h % DN_HEADS) * LANES, (h % DN_HEADS + 1) * LANES)
            if h < 2 * DN_HEADS:
                rinv = lax.rsqrt(jnp.sum(a * a, axis=-1, keepdims=True) + NORM_EPS)
                if h < DN_HEADS:
                    q_ref[:, oc] = a * (rinv * DN_DK ** -0.5)
                else:
                    k_ref[:, oc] = a * rinv
            else:
                v_ref[:, oc] = a
        bav = ba_ref[...]
        lane = lax.broadcasted_iota(jnp.int32, bav.shape, 1)
        beta = _sigmoid(bav)
        g = -jnp.exp(al_ref[...]) * _softplus(bav + dtb_ref[...])
        bg_ref[...] = jnp.where(lane < DN_HEADS, beta, jnp.where(lane < 2 * DN_HEADS, g, 0.0))

    return _rows_call(
        body, "dn_prep_fwd", s,
        [(qkv_pre, "tile"), (qkv_pre, "prev8"), (conv_w8, "full"), (ba, "tile"), (alog_row, "full"), (dtb_row, "full")],
        [((s, D_MODEL), F32, "tile")] * 3 + [((s, LANES), F32, "tile")],
        scratch=[pltpu.VMEM((tm + SUBLANES, QKV_W), F32)])


def _dn_prep_bwd(qkv_pre, ba, conv_w8, alog_row, dtb_row, dq, dk, dv, dbg):
    s = qkv_pre.shape[0]
    tm = ROW_TILE

    def body(u_ref, halo_ref, cw_ref, ba_ref, al_ref, dtb_ref, dq_ref, dk_ref, dv_ref, dbg_ref,
             dc_ref, dba_ref, dsmall_ref, ext_ref):
        _fill_ext(ext_ref, u_ref, halo_ref, pl.program_id(0) == 0)
        for h in range(3 * DN_HEADS):
            cols = slice(h * LANES, (h + 1) * LANES)
            oc = slice((h % DN_HEADS) * LANES, (h % DN_HEADS + 1) * LANES)
            c = _conv_taps(ext_ref, cw_ref, cols, tm)
            sg = _sigmoid(c)
            a = c * sg
            if h < 2 * DN_HEADS:
                rinv = lax.rsqrt(jnp.sum(a * a, axis=-1, keepdims=True) + NORM_EPS)
                dy = dq_ref[:, oc] * DN_DK ** -0.5 if h < DN_HEADS else dk_ref[:, oc]
                da = rinv * dy - a * (rinv * rinv * rinv) * jnp.sum(dy * a, axis=-1, keepdims=True)
            else:
                da = dv_ref[:, oc]
            dc_ref[:, cols] = da * (sg * (1.0 + c * (1.0 - sg)))
        bav = ba_ref[...]
        dbgv = dbg_ref[...]
        lane = lax.broadcasted_iota(jnp.int32, bav.shape, 1)
        beta = _sigmoid(bav)
        ea = jnp.exp(al_ref[...])
        z = bav + dtb_ref[...]
        g = -ea * _softplus(z)
        is_b = lane < DN_HEADS
        is_g = jnp.logical_and(lane >= DN_HEADS, lane < 2 * DN_HEADS)
        d_aa = jnp.where(is_g, dbgv * (-ea) * _sigmoid(z), 0.0)
        dba = jnp.where(is_b, dbgv * beta * (1.0 - beta), d_aa)
        dba_ref[...] = dba.astype(dba_ref.dtype)
        r_alog = jnp.sum(jnp.where(is_g, dbgv * g, 0.0), axis=0, keepdims=True)
        r_dtb = jnp.sum(d_aa, axis=0, keepdims=True)
        _acc_add(dsmall_ref, jnp.concatenate([r_alog, r_dtb, jnp.zeros((SUBLANES - 2, LANES), F32)], axis=0))

    return _rows_call(
        body, "dn_prep_bwd", s,
        [(qkv_pre, "tile"), (qkv_pre, "prev8"), (conv_w8, "full"), (ba, "tile"), (alog_row, "full"), (dtb_row, "full"),
         (dq, "tile"), (dk, "tile"), (dv, "tile"), (dbg, "tile")],
        [((s, QKV_W), F32, "tile"), ((s, LANES), MXU, "tile"), ((SUBLANES, LANES), F32, "acc")],
        scratch=[pltpu.VMEM((tm + SUBLANES, QKV_W), F32)])


def _conv_bwd(dc, qkv_pre, conv_w8):
    s = dc.shape[0]
    tm = ROW_TILE
    steps = s // tm

    def body(dc_ref, dnext_ref, u_ref, halo_ref, cw_ref, du_ref, dcw_ref, extd_ref, ext_ref):
        i = pl.program_id(0)
        _fill_ext(ext_ref, u_ref, halo_ref, i == 0)
        extd_ref[0:tm, :] = dc_ref[...]
        extd_ref[tm:, :] = jnp.where(i == steps - 1, 0.0, dnext_ref[...])

        @pl.when(i == 0)
        def _():
            dcw_ref[...] = jnp.zeros_like(dcw_ref)

        for h in range(3 * DN_HEADS):
            cols = slice(h * LANES, (h + 1) * LANES)
            du = None
            for j in range(4):
                term = cw_ref[3 - j:4 - j, cols] * extd_ref[j:j + tm, cols]
                du = term if du is None else du + term
            du_ref[:, cols] = du.astype(du_ref.dtype)
            dcv = dc_ref[:, cols]
            for j in range(4):
                row = jnp.sum(dcv * ext_ref[SUBLANES - j:SUBLANES - j + tm, cols], axis=0, keepdims=True)
                dcw_ref[3 - j:4 - j, cols] += row

    return _rows_call(
        body, "conv_bwd", s,
        [(dc, "tile"), (dc, "next8"), (qkv_pre, "tile"), (qkv_pre, "prev8"), (conv_w8, "full")],
        [((s, QKV_W), MXU, "tile"), ((SUBLANES, QKV_W), F32, "acc")],
        scratch=[pltpu.VMEM((tm + SUBLANES, QKV_W), F32), pltpu.VMEM((tm + SUBLANES, QKV_W), F32)])


def _dn_out_fwd_tile(o_ref, z_ref, w_ref, wo_ref, on_ref, y_ref):
    for h in range(DN_HEADS):
        cols = _head_cols(h)
        ov = o_ref[:, cols]
        zv = z_ref[:, cols]
        ro = lax.rsqrt(jnp.mean(ov * ov, axis=-1, keepdims=True) + NORM_EPS)
        on_ref[:, cols] = (ov * ro * w_ref[...] * (zv * _sigmoid(zv))).astype(on_ref.dtype)
    y_ref[...] = jnp.dot(on_ref[...], wo_ref[...], preferred_element_type=F32)


def _dn_out_bwd_tile(dy_ref, o_ref, z_ref, w_ref, wo_ref, do_ref, dz_ref, d_ref):
    d_ref[...] = lax.dot_general(dy_ref[...], wo_ref[...], NT_DIMS, preferred_element_type=F32)
    acc = jnp.zeros((1, LANES), F32)
    for h in range(DN_HEADS):
        cols = _head_cols(h)
        dv, ov, zv = d_ref[:, cols], o_ref[:, cols], z_ref[:, cols]
        sg = _sigmoid(zv)
        sz = zv * sg
        ro = lax.rsqrt(jnp.mean(ov * ov, axis=-1, keepdims=True) + NORM_EPS)
        nv = ov * ro
        dn = dv * w_ref[...] * sz
        acc = acc + jnp.sum(dv * nv * sz, axis=0, keepdims=True)
        dz_ref[:, cols] = (dv * nv * w_ref[...] * (sg * (1.0 + zv * (1.0 - sg)))).astype(dz_ref.dtype)
        do_ref[:, cols] = ro * dn - ov * (ro * ro * ro) * jnp.mean(dn * ov, axis=-1, keepdims=True)
    return acc


def _attn_out_fwd(parts, lses, zb, w_o_dil):
    def body(o0, o1, o2, l0, l1, l2, z_ref, wo_ref, lse_ref, o_ref, g_ref, y_ref):
        a, b, c = l0[...], l1[...], l2[...]
        m = jnp.maximum(a, jnp.maximum(b, c))
        ea, eb, ec = jnp.exp(a - m), jnp.exp(b - m), jnp.exp(c - m)
        den = ea + eb + ec
        out = (ea * o0[...] + eb * o1[...] + ec * o2[...]) / den
        lse_ref[...] = m + jnp.log(den)
        o_ref[...] = out
        zv = z_ref[...]
        gated = (out * (zv * _sigmoid(zv))).astype(g_ref.dtype)
        g_ref[...] = gated
        y_ref[...] = jnp.dot(gated, wo_ref[...], preferred_element_type=F32)

    s = zb.shape[0]
    return _rows_call(body, "attn_out_fwd", s,
                      [(p, "tile") for p in parts] + [(l, "tile") for l in lses] + [(zb, "tile"), (w_o_dil, "full")],
                      [((s, DIL_W), F32, "tile"), ((s, DIL_W), F32, "tile"), ((s, DIL_W), MXU, "tile"),
                       ((s, w_o_dil.shape[1]), F32, "tile")])


def _attn_out_bwd(dy, o_joint, zb, w_o_dil):
    def body(dy_ref, o_ref, z_ref, wo_ref, do_ref, dz_ref, dl_ref):
        zv = z_ref[...]
        sg = _sigmoid(zv)
        dv = lax.dot_general(dy_ref[...], wo_ref[...], NT_DIMS, preferred_element_type=F32)
        ov = o_ref[...]
        do = dv * (zv * sg)
        do_ref[...] = do
        dz_ref[...] = (dv * ov * (sg * (1.0 + zv * (1.0 - sg)))).astype(dz_ref.dtype)
        for h in range(DIL_HEADS):
            cols = slice(h * LANES, (h + 1) * LANES)
            dl_ref[:, cols] = jnp.broadcast_to(jnp.sum(do[:, cols] * ov[:, cols], axis=-1, keepdims=True),
                                               (do.shape[0], LANES))

    s = zb.shape[0]
    return _rows_call(body, "attn_out_bwd", s, [(dy, "tile"), (o_joint, "tile"), (zb, "tile"), (w_o_dil, "full")],
                      [((s, DIL_W), F32, "tile"), ((s, DIL_W), MXU, "tile"), ((s, DIL_W), F32, "tile")])


def _merge_out_final(ga, gb, ya, yb, x, target, w_out, wf_row):
    s, dm = x.shape

    def body(ga_ref, gb_ref, ya_ref, yb_ref, x_ref, t_ref, wo_ref, w_ref,
             loss_ref, dw_ref, m_ref, dxb_ref, dx_ref, dya_ref, dyb_ref, dga_ref, dgb_ref):
        sa, sb = _sigmoid(ga_ref[...]), _sigmoid(gb_ref[...])
        ya, yb = ya_ref[...], yb_ref[...]
        merged = (sa * ya + sb * yb).astype(MXU)
        m_ref[...] = merged
        x2 = x_ref[...] + jnp.dot(merged, wo_ref[...], preferred_element_type=F32)
        r = lax.rsqrt(jnp.mean(x2 * x2, axis=-1, keepdims=True) + NORM_EPS)
        w = w_ref[...]
        err = x2 * r * w - t_ref[...]
        tile_loss = 0.5 * jnp.sum(jnp.mean(err * err, axis=-1, keepdims=True), axis=0, keepdims=True)
        _acc_add(loss_ref, jnp.broadcast_to(tile_loss, (SUBLANES, LANES)))
        dy = err * (1.0 / dm)
        row = jnp.sum(dy * x2 * r, axis=0, keepdims=True)
        _acc_add(dw_ref, jnp.concatenate([row, jnp.zeros((SUBLANES - 1, dm), F32)], axis=0))
        dn = dy * w
        dx2 = r * dn - x2 * (r * r * r) * jnp.mean(dn * x2, axis=-1, keepdims=True)
        dx_ref[...] = dx2
        dxb = dx2.astype(MXU)
        dxb_ref[...] = dxb
        dmv = lax.dot_general(dxb, wo_ref[...], NT_DIMS, preferred_element_type=F32)
        dya_ref[...] = (dmv * sa).astype(dya_ref.dtype)
        dyb_ref[...] = (dmv * sb).astype(dyb_ref.dtype)
        dga_ref[...] = (dmv * ya * sa * (1.0 - sa)).astype(dga_ref.dtype)
        dgb_ref[...] = (dmv * yb * sb * (1.0 - sb)).astype(dgb_ref.dtype)

    return _rows_call(body, "merge_out_final", s,
                      [(ga, "tile"), (gb, "tile"), (ya, "tile"), (yb, "tile"), (x, "tile"), (target, "tile"),
                       (w_out, "full"), (wf_row, "full")],
                      [((SUBLANES, LANES), F32, "acc"), ((SUBLANES, dm), F32, "acc"), ((s, dm), MXU, "tile"),
                       ((s, dm), MXU, "tile"), ((s, dm), F32, "tile")] + [((s, dm), MXU, "tile")] * 4)


def _lane_pick(x, idx):
    lane = lax.broadcasted_iota(jnp.int32, x.shape, 1)
    return jnp.sum(jnp.where(lane == idx, x, 0.0), axis=-1, keepdims=True)


PAIR = 2 * DN_CHUNK
SCAN_CHUNKS = 4


def _bmm(a, b):
    return lax.dot_general(a.astype(MXU), b.astype(MXU), (((2,), (1,)), ((0,), (0,))), preferred_element_type=F32)


def _bmm_nt(a, b):
    return lax.dot_general(a.astype(MXU), b.astype(MXU), (((2,), (2,)), ((0,), (0,))), preferred_element_type=F32)


def _bmm_tn(a, b):
    return lax.dot_general(a.astype(MXU), b.astype(MXU), (((1,), (1,)), ((0,), (0,))), preferred_element_type=F32)


def _bmm3(a, b):
    ah = a.astype(jnp.bfloat16)
    al = (a - ah.astype(F32)).astype(jnp.bfloat16)
    bh = b.astype(jnp.bfloat16)
    bl = (b - bh.astype(F32)).astype(jnp.bfloat16)
    f = lambda p, q: lax.dot_general(p, q, (((2,), (1,)), ((0,), (0,))), preferred_element_type=F32)
    return f(ah, bh) + (f(ah, bl) + f(al, bh))


def _pair_masks():
    row = lax.broadcasted_iota(jnp.int32, (PAIR, PAIR), 0)
    col = lax.broadcasted_iota(jnp.int32, (PAIR, PAIR), 1)
    same = (row >= DN_CHUNK) == (col >= DN_CHUNK)
    return dict(causal=same & (row >= col), strict=same & (row > col), upper=same & (row <= col), eye=row == col,
                first=row < DN_CHUNK, row=row, lane=col)


def _pair_decay(bgv, masks):
    gc_all = _dot01(masks["causal"].astype(F32), bgv)
    out = []
    for h in range(DN_HEADS):
        beta = _lane_pick(bgv, h)
        gcb = jnp.broadcast_to(_lane_pick(gc_all, DN_HEADS + h), (PAIR, PAIR))
        gam = jnp.where(masks["causal"], jnp.exp(jnp.minimum(gcb - gcb.T, 0.0)), 0.0)
        gl = jnp.where(masks["first"], gcb[DN_CHUNK - 1:DN_CHUNK, :], gcb[PAIR - 1:PAIR, :])
        out.append((beta, gcb, gam, gl))
    return out


def _pair_inverse(a_strict, eye):
    eye_f = eye.astype(F32)[None]
    m = eye_f + a_strict
    x = eye_f - a_strict
    steps = int(math.log2(DN_CHUNK)) - 1
    for i in range(steps):
        mm = _bmm3 if i == steps - 1 else _bmm
        x = x + mm(x, eye_f - mm(m, x))
    return x


def _head_cols(h):
    return slice(h * LANES, (h + 1) * LANES)


def _delta_prep(q, k, v, bg):
    s = q.shape[0]
    c = DN_CHUNK
    n_chunks = s // c

    def body(q_ref, k_ref, v_ref, bg_ref, u_ref, w_ref, qd_ref, kd_ref, aqk_ref, dl_ref, t2_ref):
        masks = _pair_masks()
        dec = _pair_decay(bg_ref[...], masks)
        kbs, ks, gams, vbs, kbes, qs, qds, kds, dls = ([] for _ in range(9))
        for h in range(DN_HEADS):
            beta, gcb, gam, gl = dec[h]
            qh, kh, vh = q_ref[:, _head_cols(h)], k_ref[:, _head_cols(h)], v_ref[:, _head_cols(h)]
            eg = jnp.exp(gcb)
            kb = kh * beta
            kbs.append(kb); ks.append(kh); gams.append(gam); vbs.append(vh * beta); kbes.append(kb * eg)
            qs.append(qh); qds.append(qh * eg); kds.append(kh * jnp.exp(gl - gcb)); dls.append(jnp.exp(gl))
        st = lambda xs: jnp.stack(xs, axis=0)
        kmat, gam = st(ks), st(gams)
        a = jnp.where(masks["strict"][None], _bmm_nt(st(kbs), kmat) * gam, 0.0)
        t = _pair_inverse(a, masks["eye"])
        u = _bmm(t, st(vbs))
        w = _bmm(t, st(kbes))
        aqk = _bmm_nt(st(qs), kmat) * gam
        t2_ref[0] = t.astype(t2_ref.dtype)
        for half in range(2):
            rows = slice(half * c, (half + 1) * c)
            u_ref[half] = u[:, rows, :]
            w_ref[half] = w[:, rows, :].astype(w_ref.dtype)
            qd_ref[half] = st(qds)[:, rows, :].astype(qd_ref.dtype)
            kd_ref[half] = st(kds)[:, rows, :].astype(kd_ref.dtype)
            aqk_ref[half] = aqk[:, rows, rows].astype(aqk_ref.dtype)
            dl_ref[half] = st(dls)[:, half * c:half * c + SUBLANES, :]

    row_spec = lambda w_: pl.BlockSpec((PAIR, w_), lambda i: (i, 0))
    hm = lambda a_, b_: pl.BlockSpec((2, DN_HEADS, a_, b_), lambda i: (i, 0, 0, 0))
    hm_shape = lambda a_, b_, dt: jax.ShapeDtypeStruct((n_chunks, DN_HEADS, a_, b_), dt)
    return _pcall(
        body, name="delta_prep", grid=(n_chunks // 2,),
        in_specs=[row_spec(D_MODEL)] * 3 + [row_spec(LANES)],
        out_specs=[hm(c, LANES)] * 4 + [hm(c, c), hm(SUBLANES, LANES),
                   pl.BlockSpec((1, DN_HEADS, PAIR, PAIR), lambda i: (i, 0, 0, 0))],
        out_shape=[hm_shape(c, LANES, F32), hm_shape(c, LANES, MXU), hm_shape(c, LANES, MXU), hm_shape(c, LANES, MXU),
                   hm_shape(c, c, MXU), hm_shape(SUBLANES, LANES, F32),
                   jax.ShapeDtypeStruct((n_chunks // 2, DN_HEADS, PAIR, PAIR), MXU)],
        compiler_params=_params("parallel"),
    )(q, k, v, bg)


def _delta_scan_fwd(u, w, qd, kd, aqk, dl, z, dnw_row, w_o_dn):
    n_chunks = u.shape[0]
    c = DN_CHUNK
    g_n = SCAN_CHUNKS

    def body(u_ref, w_ref, qd_ref, kd_ref, aqk_ref, dl_ref, z_ref, nw_ref, wo_ref,
             o_ref, vnew_ref, st_ref, on_ref, y_ref, state):
        @pl.when(pl.program_id(0) == 0)
        def _():
            state[...] = jnp.zeros_like(state)

        for g in range(g_n):
            sv = state[...]
            sb = sv.astype(MXU)
            vnew = u_ref[g] - _bmm(w_ref[g], sb)
            o = _bmm(qd_ref[g], sb) + _bmm(aqk_ref[g], vnew)
            state[...] = sv * dl_ref[g][:, 0:1, :] + _bmm_tn(kd_ref[g], vnew)
            vnew_ref[g] = vnew.astype(vnew_ref.dtype)
            st_ref[g] = sb
            for h in range(DN_HEADS):
                o_ref[g * c:(g + 1) * c, _head_cols(h)] = o[h]
        _dn_out_fwd_tile(o_ref, z_ref, nw_ref, wo_ref, on_ref, y_ref)

    hm = lambda a_, b_: pl.BlockSpec((g_n, DN_HEADS, a_, b_), lambda i: (i, 0, 0, 0))
    rows = pl.BlockSpec((g_n * c, D_MODEL), lambda i: (i, 0))
    whole = lambda t: pl.BlockSpec(t.shape, lambda i: (0, 0))
    return _pcall(
        body, name="delta_scan_fwd", grid=(n_chunks // g_n,),
        in_specs=[hm(c, LANES)] * 4 + [hm(c, c), hm(SUBLANES, LANES), rows, whole(dnw_row), whole(w_o_dn)],
        out_specs=[rows, hm(c, LANES), hm(DN_DK, DN_DK), rows, rows],
        out_shape=[jax.ShapeDtypeStruct((n_chunks * c, D_MODEL), F32),
                   jax.ShapeDtypeStruct((n_chunks, DN_HEADS, c, LANES), MXU),
                   jax.ShapeDtypeStruct((n_chunks, DN_HEADS, DN_DK, DN_DK), MXU),
                   jax.ShapeDtypeStruct((n_chunks * c, D_MODEL), MXU),
                   jax.ShapeDtypeStruct((n_chunks * c, D_MODEL), F32)],
        scratch_shapes=[pltpu.VMEM((DN_HEADS, DN_DK, DN_DK), F32)],
        compiler_params=_params("arbitrary"),
    )(u, w, qd, kd, aqk, dl, z, dnw_row, w_o_dn)


def _delta_scan_bwd(w, qd, kd, aqk, dl, vnew, st, dy, o, z, dnw_row, w_o_dn):
    n_chunks = w.shape[0]
    c = DN_CHUNK
    g_n = SCAN_CHUNKS
    steps = n_chunks // g_n

    def body(w_ref, qd_ref, kd_ref, aqk_ref, dl_ref, vnew_ref, st_ref, dy_ref, o_ref, z_ref, nw_ref, wo_ref,
             dvnew_ref, dkd_ref, ddl_ref, do_ref, dz_ref, dnw_ref, dstate, d_scratch):
        @pl.when(pl.program_id(0) == 0)
        def _():
            dstate[...] = jnp.zeros_like(dstate)

        acc = _dn_out_bwd_tile(dy_ref, o_ref, z_ref, nw_ref, wo_ref, do_ref, dz_ref, d_scratch)
        _acc_add(dnw_ref, jnp.concatenate([acc, jnp.zeros((SUBLANES - 1, LANES), F32)], axis=0))
        for g in reversed(range(g_n)):
            ds = dstate[...]
            dsb = ds.astype(MXU)
            doh = jnp.stack([do_ref[g * c:(g + 1) * c, _head_cols(h)] for h in range(DN_HEADS)], axis=0)
            dvnew = _bmm_tn(aqk_ref[g], doh) + _bmm(kd_ref[g], dsb)
            dkd_ref[g] = _bmm_nt(vnew_ref[g], dsb)
            ddl = jnp.sum(jnp.sum(st_ref[g].astype(F32) * ds, axis=2, keepdims=True), axis=1, keepdims=True)
            ddl_ref[g] = jnp.broadcast_to(ddl, (DN_HEADS, SUBLANES, LANES))
            dstate[...] = ds * dl_ref[g][:, 0:1, :] + _bmm_tn(qd_ref[g], doh) - _bmm_tn(w_ref[g], dvnew)
            dvnew_ref[g] = dvnew.astype(dvnew_ref.dtype)

    rev = lambda i: steps - 1 - i
    hm = lambda a_, b_: pl.BlockSpec((g_n, DN_HEADS, a_, b_), lambda i: (rev(i), 0, 0, 0))
    rows = pl.BlockSpec((g_n * c, D_MODEL), lambda i: (rev(i), 0))
    whole = lambda t: pl.BlockSpec(t.shape, lambda i: (0, 0))
    return _pcall(
        body, name="delta_scan_bwd", grid=(steps,),
        in_specs=[hm(c, LANES)] * 3 + [hm(c, c), hm(SUBLANES, LANES), hm(c, LANES), hm(DN_DK, DN_DK),
                  rows, rows, rows, whole(dnw_row), whole(w_o_dn)],
        out_specs=[hm(c, LANES), hm(c, LANES), hm(SUBLANES, LANES), rows, rows,
                   pl.BlockSpec((SUBLANES, LANES), lambda i: (0, 0))],
        out_shape=[jax.ShapeDtypeStruct((n_chunks, DN_HEADS, c, LANES), MXU),
                   jax.ShapeDtypeStruct((n_chunks, DN_HEADS, c, LANES), F32),
                   jax.ShapeDtypeStruct((n_chunks, DN_HEADS, SUBLANES, LANES), F32),
                   jax.ShapeDtypeStruct((n_chunks * c, D_MODEL), F32),
                   jax.ShapeDtypeStruct((n_chunks * c, D_MODEL), MXU),
                   jax.ShapeDtypeStruct((SUBLANES, LANES), F32)],
        scratch_shapes=[pltpu.VMEM((DN_HEADS, DN_DK, DN_DK), F32), pltpu.VMEM((g_n * c, D_MODEL), F32)],
        compiler_params=_params("arbitrary"),
    )(w, qd, kd, aqk, dl, vnew, st, dy, o, z, dnw_row, w_o_dn)


def _delta_post_bwd(q, k, v, bg, t2, st, vnew, do, dvnew, dkd, ddl):
    s = q.shape[0]
    c = DN_CHUNK

    def body(q_ref, k_ref, v_ref, bg_ref, t2_ref, st_ref, vnew_ref, do_ref, dvnew_ref, dkd_ref, ddl_ref,
             dq_ref, dk_ref, dv_ref, dbg_ref):
        masks = _pair_masks()
        first = masks["first"][None]
        dec = _pair_decay(bg_ref[...], masks)
        st_ = lambda xs: jnp.stack(xs, axis=0)
        heads = range(DN_HEADS)
        qm_, km_, vm_, dom = (st_([r[:, _head_cols(h)] for h in heads]) for r in (q_ref, k_ref, v_ref, do_ref))
        beta = st_([dec[h][0] for h in heads])
        gcb = st_([dec[h][1] for h in heads])
        gam = st_([dec[h][2] for h in heads])
        gl = st_([dec[h][3] for h in heads])
        pair = lambda ref: jnp.concatenate([ref[0], ref[1]], axis=1)
        vnew2, dvnew2, dkd2 = pair(vnew_ref), pair(dvnew_ref), pair(dkd_ref)
        halves = lambda x: (x[:, :c, :], x[:, c:, :])
        by_state = lambda x: jnp.concatenate([_bmm_nt(xh, st_ref[i]) for i, xh in enumerate(halves(x))], axis=1)
        dqd = by_state(dom)
        dw = -by_state(dvnew2)
        ddl2 = jnp.where(first, ddl_ref[0][:, 0:1, :], ddl_ref[1][:, 0:1, :])

        eg = jnp.exp(gcb)
        egl = jnp.exp(gl - gcb)
        dl = jnp.exp(gl)
        kb = km_ * beta
        kk = _bmm_nt(kb, km_)
        a = jnp.where(masks["strict"][None], kk * gam, 0.0)
        t = t2_ref[0]
        vb = vm_ * beta
        kbe = kb * eg
        u = _bmm(t, vb)
        w = _bmm(t, kbe)
        aqk = _bmm_nt(qm_, km_) * gam
        qd = qm_ * eg
        kd = km_ * egl

        daqk = jnp.where(masks["causal"][None], _bmm_nt(dom, vnew2), 0.0)
        dvb = _bmm_tn(t, dvnew2)
        dkbe = _bmm_tn(t, dw)
        da = jnp.where(masks["strict"][None], -(_bmm_nt(dvb, u) + _bmm_nt(dkbe, w)), 0.0)
        pm = da * gam
        qmm = daqk * gam
        dkb = _bmm(pm, km_) + dkbe * eg
        dkh = _bmm_tn(pm, kb) + _bmm_tn(qmm, qm_) + dkd2 * egl + dkb * beta
        dqh = _bmm(qmm, km_) + dqd * eg
        xm = da * a + daqk * aqk
        col_rows = jnp.concatenate([jnp.zeros((DN_HEADS, PAIR), F32), jnp.sum(xm, axis=1),
                                    jnp.zeros((PAIR - 2 * DN_HEADS, PAIR), F32)], axis=0)
        tmp = jnp.sum(dkd2 * kd, axis=-1, keepdims=True)
        dgc = (jnp.sum(xm, axis=-1, keepdims=True) + jnp.sum(dkbe * kbe, axis=-1, keepdims=True)
               + jnp.sum(dqd * qd, axis=-1, keepdims=True) - tmp)
        sum0 = jnp.sum(jnp.where(first, tmp, 0.0), axis=1, keepdims=True)
        sum1 = jnp.sum(jnp.where(first, 0.0, tmp), axis=1, keepdims=True)
        dgl = jnp.where(first, sum0, sum1) + ddl2 * dl
        last = (masks["row"] == c - 1) | (masks["row"] == PAIR - 1)
        dgc = dgc + jnp.where(last[None], dgl, 0.0)
        dbeta = jnp.sum(dvb * vm_, axis=-1, keepdims=True) + jnp.sum(dkb * km_, axis=-1, keepdims=True)
        dvh = dvb * beta

        lane = masks["lane"]
        dgc_lanes = jnp.zeros((PAIR, LANES), F32)
        dbg = jnp.zeros((PAIR, LANES), F32)
        for h in heads:
            dq_ref[:, _head_cols(h)] = dqh[h]
            dk_ref[:, _head_cols(h)] = dkh[h]
            dv_ref[:, _head_cols(h)] = dvh[h]
            dgc_lanes = dgc_lanes + jnp.where(lane == DN_HEADS + h, dgc[h], 0.0)
            dbg = dbg + jnp.where(lane == h, dbeta[h], 0.0)
        dbg_ref[...] = dbg + _dot01(masks["upper"].astype(F32), dgc_lanes - col_rows.T)

    n_pairs = s // PAIR
    row_spec = lambda w_: pl.BlockSpec((PAIR, w_), lambda i: (i, 0))
    hm = lambda a_, b_: pl.BlockSpec((2, DN_HEADS, a_, b_), lambda i: (i, 0, 0, 0))
    return _pcall(
        body, name="delta_post_bwd", grid=(n_pairs,),
        in_specs=[row_spec(D_MODEL)] * 3 + [row_spec(LANES), pl.BlockSpec((1, DN_HEADS, PAIR, PAIR), lambda i: (i, 0, 0, 0)),
                  hm(DN_DK, DN_DK), hm(c, LANES), row_spec(D_MODEL), hm(c, LANES), hm(c, LANES), hm(SUBLANES, LANES)],
        out_specs=[row_spec(D_MODEL)] * 3 + [row_spec(LANES)],
        out_shape=[jax.ShapeDtypeStruct((s, D_MODEL), F32)] * 3 + [jax.ShapeDtypeStruct((s, LANES), F32)],
        compiler_params=_params("parallel"),
    )(q, k, v, bg, t2, st, vnew, do, dvnew, dkd, ddl)


def _alibi_slope(group, head):
    n = N_DIL * DIL_HEADS
    return float(2.0 ** (-8.0 * (group * DIL_HEADS + head + 1) / n))


def _attn_plan(s, group):
    window, dil = DIL_GROUPS[group]
    assert window // dil == ATT_BLOCK
    assert (s // dil) % ATT_BLOCK == 0, "sub-sequence length must be a whole number of attention blocks"
    return dil, s // dil // ATT_BLOCK, (DIL_HEADS if dil == 1 else 1)


def _attn_specs(group, dil, nb, hp):
    rows = ATT_BLOCK * dil

    def spec(col0, shift):
        if shift < 0:
            f = lambda hb, n: (jnp.maximum(n - 1, 0), col0 + hb)
        elif shift > 0:
            f = lambda hb, n: (jnp.minimum(n + 1, nb - 1), col0 + hb)
        else:
            f = lambda hb, n: (jnp.minimum(n, nb - 1), col0 + hb)
        return pl.BlockSpec((rows, hp * LANES), f)

    return (lambda shift: spec(group * (DIL_HEADS // hp), shift)), (lambda shift: spec(0, shift))


def _sub_rows(ref, r, dil, cols):
    return ref[:, cols] if dil == 1 else ref[pl.ds(r, ATT_BLOCK, stride=dil), cols]


def _set_sub_rows(ref, r, dil, cols, value):
    if dil == 1:
        ref[:, cols] = value
    else:
        ref[pl.ds(r, ATT_BLOCK, stride=dil), cols] = value


def _step_slope(group, hp, hh):
    if hp == DIL_HEADS:
        return _alibi_slope(group, hh)
    hb = pl.program_id(0)
    slope = _alibi_slope(group, DIL_HEADS - 1)
    for h in reversed(range(DIL_HEADS - 1)):
        slope = jnp.where(hb == h, _alibi_slope(group, h), slope)
    return slope


def _attn_items(hp, dil):
    return [(hh, r) for hh in range(hp) for r in range(dil)]


def _attn_stack(ref, items, dil, dtype=MXU):
    return jnp.stack([_sub_rows(ref, r, dil, _head_cols(hh)).astype(dtype) for hh, r in items], axis=0)


def _attn_slopes(group, hp, items):
    if hp == 1:
        return _step_slope(group, hp, 0)
    return jnp.stack([jnp.full((1, 1), _alibi_slope(group, hh), F32) for hh, _ in items], axis=0)


def _window_bias(dil, n):
    a = lax.broadcasted_iota(jnp.int32, (ATT_BLOCK, 2 * ATT_BLOCK), 0)
    b = lax.broadcasted_iota(jnp.int32, (ATT_BLOCK, 2 * ATT_BLOCK), 1)
    dist = ATT_BLOCK + a - b
    valid = (dist >= 0) & (dist <= ATT_BLOCK) & ((b >= ATT_BLOCK) | (n > 0))
    return (dist * dil).astype(F32), valid


def _attn_fwd(qb, kb, vb, group):
    s = qb.shape[0]
    dil, nb, hp = _attn_plan(s, group)
    qkv, per_head = _attn_specs(group, dil, nb, hp)

    def body(q_ref, kp_ref, kc_ref, vp_ref, vc_ref, o_ref, lse_ref):
        n = pl.program_id(1)
        distd, valid = _window_bias(dil, n)
        items = _attn_items(hp, dil)
        sub = lambda ref: _attn_stack(ref, items, dil)
        kk = jnp.concatenate([sub(kp_ref), sub(kc_ref)], axis=1)
        vv = jnp.concatenate([sub(vp_ref), sub(vc_ref)], axis=1)
        sc = _bmm_nt(sub(q_ref), kk) * DIL_DH ** -0.5 - _attn_slopes(group, hp, items) * distd
        sc = jnp.where(valid, sc, -1e30)
        mx = jnp.max(sc, axis=-1, keepdims=True)
        p = jnp.where(valid, jnp.exp(sc - mx), 0.0)
        den = jnp.sum(p, axis=-1, keepdims=True)
        out = _bmm(p, vv) / den
        lse = mx + jnp.log(den)
        for b, (hh, r) in enumerate(items):
            _set_sub_rows(o_ref, r, dil, _head_cols(hh), out[b])
            _set_sub_rows(lse_ref, r, dil, _head_cols(hh), jnp.broadcast_to(lse[b], (ATT_BLOCK, LANES)))

    return _pcall(
        body, name=f"attn_fwd_g{group}", grid=(DIL_HEADS // hp, nb),
        in_specs=[qkv(0), qkv(-1), qkv(0), qkv(-1), qkv(0)], out_specs=[per_head(0)] * 2,
        out_shape=[jax.ShapeDtypeStruct((s, DIL_W), F32)] * 2,
        compiler_params=_params("parallel", "parallel"),
    )(qb, kb, kb, vb, vb)


def _attn_bwd(qb, kb, vb, d_o, lse, delta, group):
    s = qb.shape[0]
    dil, nb, hp = _attn_plan(s, group)
    qkv, per_head = _attn_specs(group, dil, nb, hp)
    scale = DIL_DH ** -0.5

    def body(q_ref, kp_ref, kc_ref, vp_ref, vc_ref, do_ref, l_ref, dl_ref, dq_ref, dk_ref, dv_ref,
             dq_acc, dk_done, dv_done, dk_carry, dv_carry):
        n = pl.program_id(1)
        items = _attn_items(hp, dil)
        slopes = _attn_slopes(group, hp, items)

        @pl.when(n == 0)
        def _():
            dk_carry[...] = jnp.zeros_like(dk_carry)
            dv_carry[...] = jnp.zeros_like(dv_carry)

        @pl.when(n < nb)
        def _():
            distd, valid = _window_bias(dil, n)
            sub = lambda ref, dtype=MXU: _attn_stack(ref, items, dil, dtype)
            qc, do = sub(q_ref), sub(do_ref)
            kk = jnp.concatenate([sub(kp_ref), sub(kc_ref)], axis=1)
            vv = jnp.concatenate([sub(vp_ref), sub(vc_ref)], axis=1)
            sc = _bmm_nt(qc, kk) * scale - slopes * distd
            p = jnp.where(valid, jnp.exp(jnp.minimum(sc - jnp.concatenate([sub(l_ref, F32)] * 2, axis=2), 0.0)), 0.0)
            dsc = p * (_bmm_nt(do, vv) - jnp.concatenate([sub(dl_ref, F32)] * 2, axis=2))
            dq = _bmm(dsc, kk) * scale
            dkk = _bmm_tn(dsc, qc) * scale
            dvv = _bmm_tn(p, do)
            for b, (hh, r) in enumerate(items):
                cols = _head_cols(hh)
                _set_sub_rows(dq_acc, r, dil, cols, dq[b])
                _set_sub_rows(dk_done, r, dil, cols, _sub_rows(dk_carry, r, dil, cols) + dkk[b, :ATT_BLOCK])
                _set_sub_rows(dv_done, r, dil, cols, _sub_rows(dv_carry, r, dil, cols) + dvv[b, :ATT_BLOCK])
                _set_sub_rows(dk_carry, r, dil, cols, dkk[b, ATT_BLOCK:])
                _set_sub_rows(dv_carry, r, dil, cols, dvv[b, ATT_BLOCK:])
            dq_ref[...] = dq_acc[...].astype(dq_ref.dtype)
            dk_ref[...] = dk_done[...].astype(dk_ref.dtype)
            dv_ref[...] = dv_done[...].astype(dv_ref.dtype)

        @pl.when(n == nb)
        def _():
            dk_ref[...] = dk_carry[...].astype(dk_ref.dtype)
            dv_ref[...] = dv_carry[...].astype(dv_ref.dtype)

    return _pcall(
        body, name=f"attn_bwd_g{group}", grid=(DIL_HEADS // hp, nb + 1),
        in_specs=[qkv(0), qkv(-1), qkv(0), qkv(-1), qkv(0)] + [per_head(0)] * 3,
        out_specs=[per_head(0), per_head(-1), per_head(-1)],
        out_shape=[jax.ShapeDtypeStruct((s, DIL_W), MXU)] * 3,
        scratch_shapes=[pltpu.VMEM((ATT_BLOCK * dil, hp * LANES), F32)] * 5,
        compiler_params=_params("parallel", "arbitrary"),
    )(qb, kb, kb, vb, vb, d_o, lse, delta)


def _my_place():
    mx, my, mc = lax.axis_index("x"), lax.axis_index("y"), lax.axis_index("c")
    return mx, my, mc, 4 * mx + 2 * my + mc


N_CHIPS = 4


def _shard_row_tile(r):
    if r <= 512:
        return r
    return 128 if r % 128 == 0 else 480


def _other_chips(mx, my):
    return [(1 - mx, my), (mx, 1 - my), (1 - mx, 1 - my)]


def _all_gather(xs, name):
    n = len(xs)
    halved = [x.shape[1] % (2 * LANES) == 0 and x.size * x.dtype.itemsize >= (1 << 20) for x in xs]
    n_sems = 8

    def body(*refs):
        x_refs, o_refs = refs[:n], refs[n:2 * n]
        send_sems, recv_sems, local_sems = refs[2 * n:]
        mx, my, mc, me = _my_place()
        sibling, sibling_id = (mx, my, 1 - mc), 4 * mx + 2 * my + (1 - mc)
        x_nbr, y_nbr, diag = _other_chips(mx, my)
        slot_of = lambda chip, c: 4 * chip[0] + 2 * chip[1] + c

        def part(ref, a, half):
            if not halved[a]:
                return ref
            width = xs[a].shape[1] // 2
            return ref.at[:, pl.ds(half * width, width)]

        def copy(a, k, dst, to, src=None):
            return pltpu.make_async_remote_copy(
                src_ref=dst if src is None else src, dst_ref=dst, send_sem=send_sems.at[a, k],
                recv_sem=recv_sems.at[a, k], device_id=to, device_id_type=MESH)

        local = [pltpu.make_async_copy(x_refs[a], o_refs[a].at[me], local_sems.at[a]) for a in range(n)]
        for cp in local:
            cp.start()
        sends = []
        for a in range(n):
            mine = o_refs[a].at[me]
            sends += [copy(a, 0, mine, sibling, src=x_refs[a]), copy(a, 1, mine, (*x_nbr, mc), src=x_refs[a]),
                      copy(a, 2, mine, (*y_nbr, mc), src=x_refs[a])]
        for cp in sends:
            cp.start()
        for a in range(n):
            blk = o_refs[a].at[slot_of(x_nbr, mc)]
            copy(a, 1, blk, (*x_nbr, mc)).wait_recv()
            sends += [copy(a, 3, blk, sibling), copy(a, 5, part(blk, a, 0), (*y_nbr, mc))]
            sends[-2].start()
            sends[-1].start()
        for a in range(n):
            blk = o_refs[a].at[slot_of(y_nbr, mc)]
            copy(a, 2, blk, (*y_nbr, mc)).wait_recv()
            sends.append(copy(a, 4, blk, sibling))
            sends[-1].start()
            if halved[a]:
                sends.append(copy(a, 6, part(blk, a, 1), (*x_nbr, mc)))
                sends[-1].start()
        for a in range(n):
            blk = o_refs[a].at[slot_of(diag, mc)]
            copy(a, 5, part(blk, a, 0), (*y_nbr, mc)).wait_recv()
            if halved[a]:
                copy(a, 6, part(blk, a, 1), (*x_nbr, mc)).wait_recv()
            sends.append(copy(a, 7, blk, sibling))
            sends[-1].start()
        for a in range(n):
            copy(a, 0, o_refs[a].at[sibling_id], sibling).wait_recv()
            for k, chip in ((3, x_nbr), (4, y_nbr), (7, diag)):
                copy(a, k, o_refs[a].at[slot_of(chip, 1 - mc)], sibling).wait_recv()
        for cp in sends:
            cp.wait_send()
        for cp in local:
            cp.wait()

    any_spec = pl.BlockSpec(memory_space=pl.ANY)
    return _pcall(
        body, name=name,
        in_specs=[any_spec] * n, out_specs=[any_spec] * n,
        out_shape=[jax.ShapeDtypeStruct((N_DEV,) + x.shape, x.dtype) for x in xs],
        scratch_shapes=[pltpu.SemaphoreType.DMA((n, n_sems)), pltpu.SemaphoreType.DMA((n, n_sems)),
                        pltpu.SemaphoreType.DMA((n,))],
    )(*xs)


def _pair_exchange(gs, name):
    n = len(gs)

    def body(*refs):
        g_refs, o_refs = refs[:n], refs[n:2 * n]
        send_sems, recv_sems = refs[2 * n:]
        mx, my, mc, _ = _my_place()
        copies = [pltpu.make_async_remote_copy(
            src_ref=g_refs[a].at[p, 1 - mc], dst_ref=o_refs[a].at[p], send_sem=send_sems.at[a, p],
            recv_sem=recv_sems.at[a, p], device_id=(mx, my, 1 - mc), device_id_type=MESH)
            for a in range(n) for p in range(N_CHIPS)]
        for cp in copies:
            cp.start()
        for cp in copies:
            cp.wait()

    any_spec = pl.BlockSpec(memory_space=pl.ANY)
    return _pcall(
        body, name=name,
        in_specs=[any_spec] * n, out_specs=[any_spec] * n,
        out_shape=[jax.ShapeDtypeStruct((N_CHIPS,) + g.shape[2:], g.dtype) for g in gs],
        scratch_shapes=[pltpu.SemaphoreType.DMA((n, N_CHIPS)), pltpu.SemaphoreType.DMA((n, N_CHIPS))],
    )(*gs)


def _pair_add(g, other, name):
    chips, _, r, c = g.shape
    tr = _shard_row_tile(r)
    core = lax.axis_index("c").astype(jnp.int32).reshape(1)

    def body(core_ref, g_ref, o_ref, h_ref):
        h_ref[...] = (g_ref[...].astype(F32)[0] + o_ref[...].astype(F32)).astype(h_ref.dtype)

    blk = pl.BlockSpec((1, tr, c), lambda p, i, core_ref: (p, i, 0))
    return _pcall(
        body, name=name,
        grid_spec=pltpu.PrefetchScalarGridSpec(
            num_scalar_prefetch=1, grid=(chips, pl.cdiv(r, tr)),
            in_specs=[pl.BlockSpec((1, 1, tr, c), lambda p, i, core_ref: (p, core_ref[0], i, 0)), blk],
            out_specs=blk),
        out_shape=jax.ShapeDtypeStruct((chips, r, c), g.dtype),
        compiler_params=_params("parallel", "parallel"),
    )(core, g, other)


def _chip_exchange(hs, name):
    n = len(hs)

    def body(*refs):
        h_refs, o_refs = refs[:n], refs[n:2 * n]
        send_sems, recv_sems, local_sems = refs[2 * n:]
        mx, my, mc, _ = _my_place()
        my_chip = 2 * mx + my
        chips = _other_chips(mx, my)
        local = [pltpu.make_async_copy(h_refs[a].at[my_chip], o_refs[a].at[my_chip], local_sems.at[a]) for a in range(n)]
        for cp in local:
            cp.start()
        for j, (px, py) in enumerate(chips):
            for a in range(n):
                pltpu.make_async_remote_copy(
                    src_ref=h_refs[a].at[2 * px + py], dst_ref=o_refs[a].at[my_chip], send_sem=send_sems.at[a, j],
                    recv_sem=recv_sems.at[a, j], device_id=(px, py, mc), device_id_type=MESH).start()
        for j, (px, py) in enumerate(chips):
            for a in range(n):
                pltpu.make_async_remote_copy(
                    src_ref=h_refs[a].at[2 * px + py], dst_ref=o_refs[a].at[2 * px + py], send_sem=send_sems.at[a, j],
                    recv_sem=recv_sems.at[a, j], device_id=(px, py, mc), device_id_type=MESH).wait()
        for cp in local:
            cp.wait()

    any_spec = pl.BlockSpec(memory_space=pl.ANY)
    return _pcall(
        body, name=name,
        in_specs=[any_spec] * n, out_specs=[any_spec] * n,
        out_shape=[jax.ShapeDtypeStruct(h.shape, h.dtype) for h in hs],
        scratch_shapes=[pltpu.SemaphoreType.DMA((n, N_CHIPS - 1)), pltpu.SemaphoreType.DMA((n, N_CHIPS - 1)),
                        pltpu.SemaphoreType.DMA((n,))],
    )(*hs)


def _adamw(parts, w, m, v, name):
    r, c = w.shape
    n_parts = parts.shape[0]
    tr = _shard_row_tile(r)
    bc1 = 1.0 - ADAM_B1 ** ADAM_STEP
    bc2 = 1.0 - ADAM_B2 ** ADAM_STEP

    def body(p_ref, w_ref, m_ref, v_ref, g_ref, d_ref, nm_ref, nv_ref):
        g = p_ref[0].astype(F32)
        for j in range(1, n_parts):
            g = g + p_ref[j].astype(F32)
        nm = ADAM_B1 * m_ref[...] + (1.0 - ADAM_B1) * g
        nv = ADAM_B2 * v_ref[...] + (1.0 - ADAM_B2) * (g * g)
        g_ref[...] = g
        nm_ref[...] = nm
        nv_ref[...] = nv
        d_ref[...] = -ADAM_LR * ((nm / bc1) / (jnp.sqrt(nv / bc2) + ADAM_EPS) + ADAM_WD * w_ref[...])

    blk = pl.BlockSpec((tr, c), lambda i: (i, 0))
    return _pcall(
        body, name=name, grid=(pl.cdiv(r, tr),),
        in_specs=[pl.BlockSpec((n_parts, tr, c), lambda i: (0, i, 0)), blk, blk, blk],
        out_specs=[blk] * 4, out_shape=[jax.ShapeDtypeStruct((r, c), F32)] * 4,
        compiler_params=_params("parallel"),
    )(parts, w, m, v)


def _local_step(x, target, norm_w, w_segs, conv_w, a_log, dt_bias, dn_norm_w, w_o_dn, w_o_dil, w_out, final_norm_w):
    s = x.shape[0]
    w_qkv, w_za, w_ba, w_qb, w_kb, w_vb, w_zb, w_ga, w_gb = w_segs
    conv_w8 = jnp.concatenate([conv_w, jnp.zeros((SUBLANES - conv_w.shape[0], QKV_W), F32)], axis=0)
    pad8 = jnp.zeros((1, DN_HEADS), F32)
    alog_row = jnp.concatenate([pad8, a_log, jnp.zeros((1, LANES - 2 * DN_HEADS), F32)], axis=1)
    dtb_row = jnp.concatenate([pad8, dt_bias, jnp.zeros((1, LANES - 2 * DN_HEADS), F32)], axis=1)
    wf_row = final_norm_w.reshape(1, D_MODEL)

    hb, qkv_pre, z_a, ba, z_b = _rms_proj_fwd(x, norm_w, [w_qkv, w_za, w_ba, w_zb], "rms_proj_fwd_a")
    q_b, k_b, v_b, g_a, g_b = _mm_out(hb, [w_qb, w_kb, w_vb, w_ga, w_gb], "proj_fwd_b", w_is_out_by_in=True)

    qn, kn, vn, bg = _dn_prep_fwd(qkv_pre, ba, conv_w8, alog_row, dtb_row)
    u_d, w_d, qd_d, kd_d, aqk_d, dl_d, t2_d = _delta_prep(qn, kn, vn, bg)
    o_a, vnew_d, st_d, on_b, y_a = _delta_scan_fwd(u_d, w_d, qd_d, kd_d, aqk_d, dl_d, z_a, dn_norm_w, w_o_dn)

    parts, lses = [], []
    for gi in range(N_DIL):
        o_g, l_g = _attn_fwd(q_b, k_b, v_b, gi)
        parts.append(o_g)
        lses.append(l_g)
    lse, o_joint, ob_b, y_b = _attn_out_fwd(parts, lses, z_b, w_o_dil)

    loss8, dwf8, merged_b, dx2_b, dx2, dya_b, dyb_b, dga_b, dgb_b = _merge_out_final(
        g_a, g_b, y_a, y_b, x, target, w_out, wf_row)

    g_w_out = _mm_tn(merged_b, dx2_b, "out_wgrad")
    g_w_o_dn = _mm_tn(on_b, dya_b, "out_dn_wgrad")

    g_w_o_dil = _mm_tn(ob_b, dyb_b, "out_dil_wgrad")
    d_o, dzb_b, delta = _attn_out_bwd(dyb_b, o_joint, z_b, w_o_dil)
    dqs, dks, dvs = [], [], []
    for gi in range(N_DIL):
        dq_g, dk_g, dv_g = _attn_bwd(q_b, k_b, v_b, d_o, lse, delta, gi)
        dqs.append(dq_g)
        dks.append(dk_g)
        dvs.append(dv_g)

    dvnew_d, dkd_d, ddl_d, d_o_a, dza_b, ddnw8 = _delta_scan_bwd(
        w_d, qd_d, kd_d, aqk_d, dl_d, vnew_d, st_d, dya_b, o_a, z_a, dn_norm_w, w_o_dn)
    dqn, dkn, dvn, dbg = _delta_post_bwd(qn, kn, vn, bg, t2_d, st_d, vnew_d, d_o_a, dvnew_d, dkd_d, ddl_d)
    dc, dba_b, dsmall8 = _dn_prep_bwd(qkv_pre, ba, conv_w8, alog_row, dtb_row, dqn, dkn, dvn, dbg)
    dqkv_b, dconv8 = _conv_bwd(dc, qkv_pre, conv_w8)

    per_group = lambda w: [w[g * DIL_W:(g + 1) * DIL_W] for g in range(N_DIL)]
    dh_b = _mm_in(dqs + dks + dvs + [dga_b, dgb_b],
                  per_group(w_qb) + per_group(w_kb) + per_group(w_vb) + [w_ga, w_gb], "proj_bwd_b", w_is_out_by_in=True)
    dsegs = [dqkv_b, dza_b, dba_b] + dqs + dks + dvs + [dzb_b, dga_b, dgb_b]
    valid_rows = [d.shape[1] for d in dsegs]
    valid_rows[2] = 2 * DN_HEADS
    g_wt = _proj_wgrad_all(dsegs, valid_rows, hb)
    grad_x, dnw8 = _proj_bwd_rms_in([dqkv_b, dza_b, dba_b, dzb_b], [w_qkv, w_za, w_ba, w_zb], dh_b, x, dx2, norm_w)

    small = dict(norm_w=dnw8[0:1], final_norm_w=dwf8[0:1], dn_norm_w=ddnw8[0:1],
                 a_log=dsmall8[0:1, DN_HEADS:2 * DN_HEADS], dt_bias=dsmall8[1:2, DN_HEADS:2 * DN_HEADS])
    return loss8[0:1, 0:1], grad_x, g_wt, dconv8[0:4], g_w_o_dn, g_w_o_dil, g_w_out, small


def _proj_bwd_rms_in(ds, ws, dh_a, x, dx2, norm_w):
    n_seg = len(ds)

    def body(*refs):
        d_refs, w_refs = refs[:n_seg], refs[n_seg:2 * n_seg]
        da_ref, x_ref, dx2_ref, w_ref, dx_ref, dw_ref = refs[2 * n_seg:]
        dx_ref[...] = da_ref[...]
        for d_ref, wt_ref in zip(d_refs, w_refs):
            for c, wd in _col_chunks(d_ref.shape[1], 1024):
                dx_ref[...] += jnp.dot(d_ref[:, c:c + wd], wt_ref[c:c + wd, :], preferred_element_type=F32)
        xv = x_ref[...]
        r = lax.rsqrt(jnp.mean(xv * xv, axis=-1, keepdims=True) + NORM_EPS)
        dhv = dx_ref[...]
        dn = dhv * w_ref[...]
        dx_ref[...] = dx2_ref[...] + r * dn - xv * (r * r * r) * jnp.mean(dn * xv, axis=-1, keepdims=True)
        row = jnp.sum(dhv * xv * r, axis=0, keepdims=True)
        _acc_add(dw_ref, jnp.concatenate([row, jnp.zeros((SUBLANES - 1, row.shape[1]), F32)], axis=0))

    return _rows_call(body, "proj_bwd_b_rms_in", x.shape[0],
                      [(d, "tile") for d in ds] + [(w, "full") for w in ws]
                      + [(dh_a, "tile"), (x, "tile"), (dx2, "tile"), (norm_w, "full")],
                      [(x.shape, F32, "tile"), ((SUBLANES, x.shape[1]), F32, "acc")])


def _split_proj_rows(w_shards):
    n_shards, rows, k = w_shards.shape
    wt_full = w_shards.reshape(n_shards * rows, k)
    offs = [0]
    for n in PROJ_SIZES:
        offs.append(offs[-1] + n)
    seg = lambda a, b: wt_full[offs[a]:offs[b]]
    w_ba = jnp.concatenate([seg(4, 6), jnp.zeros((LANES - 2 * DN_HEADS, k), wt_full.dtype)], axis=0)
    return [seg(0, 3), seg(3, 4), w_ba, seg(6, 7), seg(7, 8), seg(8, 9), seg(9, 10), seg(10, 11), seg(11, 12)]


LOSS_ROW = 5


def _pack_small(norm_w, final_norm_w, dn_norm_w, a_log, dt_bias, loss=None):
    pad = lambda r: jnp.concatenate([r, jnp.zeros((1, D_MODEL - r.shape[1]), F32)], axis=1)
    rows = [pad(norm_w.reshape(1, -1)), pad(final_norm_w.reshape(1, -1)), pad(dn_norm_w.reshape(1, -1)),
            pad(a_log.reshape(1, -1)), pad(dt_bias.reshape(1, -1)),
            pad(jnp.zeros((1, 1), F32) if loss is None else loss.reshape(1, 1)),
            jnp.zeros((SUBLANES - LOSS_ROW - 1, D_MODEL), F32)]
    return jnp.concatenate(rows, axis=0)


def _unpack_small(p):
    return dict(norm_w=p[0:1], final_norm_w=p[1], dn_norm_w=p[2:3, :DN_DK], a_log=p[3:4, :DN_HEADS],
                dt_bias=p[4:5, :DN_HEADS])


def kernel(x, norm_w, w_in, conv_w, a_log, dt_bias, dn_norm_w, w_o_dn, w_o_dil, w_out, final_norm_w, loss_target, m_norm_w, m_w_in, m_conv_w, m_a_log, m_dt_bias, m_dn_norm_w, m_w_o_dn, m_w_o_dil, m_w_out, m_final_norm_w, v_norm_w, v_w_in, v_conv_w, v_a_log, v_dt_bias, v_dn_norm_w, v_w_o_dn, v_w_o_dil, v_w_out, v_final_norm_w):
    shard_w = w_in.shape[2]
    wt, m_wt, v_wt = (jnp.transpose(t[0]) for t in (w_in, m_w_in, v_w_in))
    gathered = _all_gather([wt.astype(MXU), w_o_dn[0].astype(MXU), w_o_dil[0].astype(MXU), w_out[0].astype(MXU),
                            conv_w[0]], "gather_weights")
    w_in_all, w_o_dn_all, w_o_dil_all, w_out_all, conv_all = gathered
    w_o_dn_full = w_o_dn_all.reshape(D_MODEL, D_MODEL)
    w_o_dil_full = jnp.transpose(w_o_dil_all, (1, 0, 2)).reshape(DIL_W, D_MODEL)
    w_out_full = w_out_all.reshape(D_MODEL, D_MODEL)
    conv_full = jnp.transpose(conv_all, (1, 0, 2)).reshape(conv_w.shape[1], QKV_W)

    loss11, grad_x, g_wt, g_conv, g_w_o_dn, g_w_o_dil, g_w_out, small = _local_step(
        x[0], loss_target[0], norm_w, _split_proj_rows(w_in_all), conv_full, a_log, dt_bias, dn_norm_w,
        w_o_dn_full, w_o_dil_full, w_out_full, final_norm_w)

    col_shards = lambda g, n: jnp.transpose(g.reshape(g.shape[0], N_DEV, n), (1, 0, 2))
    row_shards = lambda g: g.reshape(N_DEV, g.shape[0] // N_DEV, g.shape[1])
    g_wt_shards = jnp.stack([g_wt[j * shard_w:(j + 1) * shard_w] for j in range(N_DEV)], axis=0)
    sent = [g_wt_shards, row_shards(g_w_o_dn).astype(MXU),
            col_shards(g_w_o_dil, w_o_dil.shape[2]).astype(MXU), row_shards(g_w_out).astype(MXU),
            col_shards(g_conv, conv_w.shape[2])]
    sent = [g8.reshape((N_CHIPS, 2) + g8.shape[1:]) for g8 in sent]
    from_sibling = _pair_exchange(sent, "scatter_pair")
    summed = [_pair_add(g, o, f"pair_add_{i}") for i, (g, o) in enumerate(zip(sent, from_sibling))]
    p_w_in, p_w_o_dn, p_w_o_dil, p_w_out, p_conv = _chip_exchange(summed, "scatter_chips")
    p_small = _all_gather([_pack_small(small["norm_w"], small["final_norm_w"], small["dn_norm_w"], small["a_log"],
                                       small["dt_bias"], loss11)], "gather_small_grads")[0]

    res = {}
    res["w_in"] = [jnp.transpose(t) for t in _adamw(p_w_in, wt, m_wt, v_wt, "adamw_w_in")]
    res["conv_w"] = _adamw(p_conv, conv_w[0], m_conv_w[0], v_conv_w[0], "adamw_conv_w")
    res["w_o_dn"] = _adamw(p_w_o_dn, w_o_dn[0], m_w_o_dn[0], v_w_o_dn[0], "adamw_w_o_dn")
    res["w_o_dil"] = _adamw(p_w_o_dil, w_o_dil[0], m_w_o_dil[0], v_w_o_dil[0], "adamw_w_o_dil")
    res["w_out"] = _adamw(p_w_out, w_out[0], m_w_out[0], v_w_out[0], "adamw_w_out")
    small_res = _adamw(p_small, _pack_small(norm_w, final_norm_w, dn_norm_w, a_log, dt_bias),
                       _pack_small(m_norm_w, m_final_norm_w, m_dn_norm_w, m_a_log, m_dt_bias),
                       _pack_small(v_norm_w, v_final_norm_w, v_dn_norm_w, v_a_log, v_dt_bias), "adamw_small")
    loss = small_res[0][LOSS_ROW, 0]
    small_res = [_unpack_small(t) for t in small_res]

    names = ["norm_w", "w_in", "conv_w", "a_log", "dt_bias", "dn_norm_w", "w_o_dn", "w_o_dil", "w_out", "final_norm_w"]
    outs = [loss, grad_x[None]]
    for kind in range(4):
        for nm in names:
            outs.append(res[nm][kind][None] if nm in res else small_res[kind][nm])
    return tuple(outs)
```

```python
import math

import jax
import jax.numpy as jnp
from jax import lax
from jax.experimental import pallas as pl
from jax.experimental.pallas import tpu as pltpu

F32 = jnp.float32
MXU = jnp.bfloat16
MESH = pl.DeviceIdType.MESH

N_DEV = 8
D_MODEL = 1024
DN_HEADS = 8
DN_DK = 128
DN_CHUNK = 64
N_DIL = 3
DIL_HEADS = 4
DIL_DH = 128
DIL_W = DIL_HEADS * DIL_DH
DIL_GROUPS = ((128, 1), (512, 4), (2048, 16))
ATT_BLOCK = 128
NORM_EPS = 1e-6
QKV_W = 3 * D_MODEL
DILQ_W = N_DIL * DIL_W
PROJ_SIZES = (1024, 1024, 1024, 1024, 8, 8, DILQ_W, DILQ_W, DILQ_W, DIL_W, D_MODEL, D_MODEL)

ADAM_LR = 0.001
ADAM_B1 = 0.9
ADAM_B2 = 0.999
ADAM_EPS = 1e-08
ADAM_WD = 0.01
ADAM_STEP = 10

ROW_TILE = 256
LANES = 128
SUBLANES = 8
VMEM_LIMIT = 48 << 20


def _pcall(body, **kw):
    return pl.pallas_call(body, **kw)


def _params(*sem):
    return pltpu.CompilerParams(dimension_semantics=tuple(sem), vmem_limit_bytes=VMEM_LIMIT)


def _sigmoid(x):
    return 1.0 / (1.0 + jnp.exp(-x))


def _softplus(x):
    return jnp.maximum(x, 0.0) + jnp.log(1.0 + jnp.exp(-jnp.abs(x)))


def _dot(a, b):
    return jnp.dot(a.astype(MXU), b.astype(MXU), preferred_element_type=F32)


def _dot_nt(a, b):
    return lax.dot_general(a.astype(MXU), b.astype(MXU), (((1,), (1,)), ((), ())), preferred_element_type=F32)


def _dot_tn(a, b):
    return lax.dot_general(a.astype(MXU), b.astype(MXU), (((0,), (0,)), ((), ())), preferred_element_type=F32)


def _split3(x):
    hi = x.astype(jnp.bfloat16)
    r1 = x - hi.astype(F32)
    mid = r1.astype(jnp.bfloat16)
    lo = (r1 - mid.astype(F32)).astype(jnp.bfloat16)
    return hi, mid, lo


def _dot01(m01, x):
    m = m01.astype(jnp.bfloat16)
    hi, mid, lo = _split3(x)
    f = lambda p: jnp.dot(m, p, preferred_element_type=F32)
    return f(hi) + (f(mid) + f(lo))


def _rows_call(body, name, n_rows, ins, outs, scratch=(), tm=ROW_TILE):
    steps = n_rows // tm
    per8 = tm // SUBLANES
    last8 = n_rows // SUBLANES - 1
    in_specs = []
    for arr, kind in ins:
        cols = arr.shape[-1]
        if kind == "tile":
            in_specs.append(pl.BlockSpec((tm, cols), lambda i: (i, 0)))
        elif kind == "full":
            in_specs.append(pl.BlockSpec(arr.shape, lambda i, nd=arr.ndim: (0,) * nd))
        elif kind == "prev8":
            in_specs.append(pl.BlockSpec((SUBLANES, cols), lambda i: (jnp.maximum(i * per8 - 1, 0), 0)))
        elif kind == "next8":
            in_specs.append(pl.BlockSpec((SUBLANES, cols), lambda i: (jnp.minimum((i + 1) * per8, last8), 0)))
        else:
            raise ValueError(kind)
    out_specs, out_shape, has_acc = [], [], False
    for shape, dtype, kind in outs:
        out_shape.append(jax.ShapeDtypeStruct(shape, dtype))
        if kind == "tile":
            out_specs.append(pl.BlockSpec((tm, shape[-1]), lambda i: (i, 0)))
        else:
            has_acc = True
            out_specs.append(pl.BlockSpec(shape, lambda i: (0, 0)))
    return _pcall(
        body, name=name, grid=(steps,), in_specs=in_specs, out_specs=out_specs, out_shape=out_shape,
        scratch_shapes=list(scratch),
        compiler_params=_params("arbitrary" if has_acc else "parallel"),
    )(*[a for a, _ in ins])


def _acc_add(ref, value):
    @pl.when(pl.program_id(0) == 0)
    def _():
        ref[...] = jnp.zeros_like(ref)
    ref[...] += value


def _col_chunks(n, width=512):
    return [(c, min(width, n - c)) for c in range(0, n, width)]


NT_DIMS = (((1,), (1,)), ((), ()))
TN_DIMS = (((0,), (0,)), ((), ()))


def _mm_out(a, ws, name, w_is_out_by_in=False, out_dtype=F32, tm=ROW_TILE):
    m, k = a.shape
    ns = [w.shape[0] if w_is_out_by_in else w.shape[1] for w in ws]

    def body(a_ref, *refs):
        av = a_ref[...]
        for w_ref, o_ref, n in zip(refs[:len(ws)], refs[len(ws):], ns):
            for c, wd in _col_chunks(n):
                if w_is_out_by_in:
                    part = lax.dot_general(av, w_ref[c:c + wd, :], NT_DIMS, preferred_element_type=F32)
                else:
                    part = jnp.dot(av, w_ref[:, c:c + wd], preferred_element_type=F32)
                o_ref[:, c:c + wd] = part.astype(o_ref.dtype)

    return _pcall(
        body, name=name, grid=(m // tm,),
        in_specs=[pl.BlockSpec((tm, k), lambda i: (i, 0))] + [pl.BlockSpec(w.shape, lambda i: (0, 0)) for w in ws],
        out_specs=[pl.BlockSpec((tm, n), lambda i: (i, 0)) for n in ns],
        out_shape=[jax.ShapeDtypeStruct((m, n), out_dtype) for n in ns],
        compiler_params=_params("parallel"),
    )(a, *ws)


def _rms_proj_fwd(x, norm_w, wts, name, tm=ROW_TILE):
    m, k = x.shape
    ns = [w.shape[0] for w in wts]

    def body(x_ref, nw_ref, *refs):
        w_refs, h_ref, o_refs = refs[:len(wts)], refs[len(wts)], refs[len(wts) + 1:]
        xv = x_ref[...]
        r = lax.rsqrt(jnp.mean(xv * xv, axis=-1, keepdims=True) + NORM_EPS)
        hv = (xv * r * nw_ref[...]).astype(h_ref.dtype)
        h_ref[...] = hv
        for w_ref, o_ref, n in zip(w_refs, o_refs, ns):
            for c, wd in _col_chunks(n):
                o_ref[:, c:c + wd] = lax.dot_general(hv, w_ref[c:c + wd, :], NT_DIMS, preferred_element_type=F32)

    return _pcall(
        body, name=name, grid=(m // tm,),
        in_specs=[pl.BlockSpec((tm, k), lambda i: (i, 0)), pl.BlockSpec(norm_w.shape, lambda i: (0, 0))]
        + [pl.BlockSpec(w.shape, lambda i: (0, 0)) for w in wts],
        out_specs=[pl.BlockSpec((tm, k), lambda i: (i, 0))] + [pl.BlockSpec((tm, n), lambda i: (i, 0)) for n in ns],
        out_shape=[jax.ShapeDtypeStruct((m, k), MXU)] + [jax.ShapeDtypeStruct((m, n), F32) for n in ns],
        compiler_params=_params("parallel"),
    )(x, norm_w, *wts)


def _mm_in(ds, ws, name, w_is_out_by_in=False, tm=ROW_TILE):
    m = ds[0].shape[0]
    k = ws[0].shape[1] if w_is_out_by_in else ws[0].shape[0]
    ns = [d.shape[1] for d in ds]

    def body(*refs):
        d_refs, w_refs, o_ref = refs[:len(ds)], refs[len(ds):2 * len(ds)], refs[-1]
        first = True
        for d_ref, w_ref, n in zip(d_refs, w_refs, ns):
            for c, wd in _col_chunks(n, 1024):
                if w_is_out_by_in:
                    part = jnp.dot(d_ref[:, c:c + wd], w_ref[c:c + wd, :], preferred_element_type=F32)
                else:
                    part = lax.dot_general(d_ref[:, c:c + wd], w_ref[:, c:c + wd], NT_DIMS, preferred_element_type=F32)
                if first:
                    o_ref[...] = part
                    first = False
                else:
                    o_ref[...] += part

    return _pcall(
        body, name=name, grid=(m // tm,),
        in_specs=[pl.BlockSpec((tm, n), lambda i: (i, 0)) for n in ns] + [pl.BlockSpec(w.shape, lambda i: (0, 0)) for w in ws],
        out_specs=pl.BlockSpec((tm, k), lambda i: (i, 0)),
        out_shape=jax.ShapeDtypeStruct((m, k), F32),
        compiler_params=_params("parallel"),
    )(*ds, *ws)


def _mm_tn(a, d, name):
    m, k = a.shape
    n = d.shape[1]
    tk = 512 if k % 512 == 0 else k

    def body(a_ref, d_ref, o_ref):
        o_ref[...] = lax.dot_general(a_ref[...], d_ref[...], TN_DIMS, preferred_element_type=F32)

    return _pcall(
        body, name=name, grid=(k // tk,),
        in_specs=[pl.BlockSpec((m, tk), lambda p: (0, p)), pl.BlockSpec((m, n), lambda p: (0, 0))],
        out_specs=pl.BlockSpec((tk, n), lambda p: (p, 0)),
        out_shape=jax.ShapeDtypeStruct((k, n), F32),
        compiler_params=_params("parallel"),
    )(a, d)


WGRAD_TILE = 512


def _proj_wgrad_all(dsegs, valid_rows, hb):
    m, k = hb.shape
    n_seg = len(dsegs)
    tiles, row = [], 0
    for si, (d, valid) in enumerate(zip(dsegs, valid_rows)):
        for c in range(0, valid, WGRAD_TILE):
            width = min(WGRAD_TILE, d.shape[1] - c)
            tiles.append((si, c, width, row + c, min(width, valid - c)))
        row += valid
    total_rows = row

    def body(*refs):
        d_refs, hb_ref, o_ref = refs[:n_seg], refs[n_seg], refs[n_seg + 1]
        a_buf, hb_buf, o_buf, load_sems, store_sems, hb_sem = refs[n_seg + 2:]

        def load(t):
            si, c, width, _, _ = tiles[t]
            return pltpu.make_async_copy(d_refs[si].at[:, pl.ds(c, width)], a_buf.at[t % 2, :, pl.ds(0, width)],
                                         load_sems.at[t % 2])

        def stores(t):
            _, _, _, orow, valid = tiles[t]
            return [pltpu.make_async_copy(o_buf.at[t % 2, pl.ds(0, valid), :], o_ref.at[pl.ds(orow, valid), :],
                                          store_sems.at[t % 2])]

        hb_copy = pltpu.make_async_copy(hb_ref, hb_buf, hb_sem)
        hb_copy.start()
        load(0).start()
        hb_copy.wait()
        for t in range(len(tiles)):
            width = tiles[t][2]
            load(t).wait()
            if t + 1 < len(tiles):
                load(t + 1).start()
            if t >= 2:
                for cp in stores(t - 2):
                    cp.wait()
            o_buf[t % 2, 0:width, :] = lax.dot_general(a_buf[t % 2, :, 0:width], hb_buf[...], TN_DIMS,
                                                        preferred_element_type=F32).astype(o_buf.dtype)
            for cp in stores(t):
                cp.start()
        for t in range(max(len(tiles) - 2, 0), len(tiles)):
            for cp in stores(t):
                cp.wait()

    any_spec = pl.BlockSpec(memory_space=pl.ANY)
    return _pcall(
        body, name="proj_wgrad",
        in_specs=[any_spec] * (n_seg + 1), out_specs=any_spec,
        out_shape=jax.ShapeDtypeStruct((total_rows, k), hb.dtype),
        scratch_shapes=[pltpu.VMEM((2, m, WGRAD_TILE), hb.dtype), pltpu.VMEM((m, k), hb.dtype),
                        pltpu.VMEM((2, WGRAD_TILE, k), hb.dtype), pltpu.SemaphoreType.DMA((2,)),
                        pltpu.SemaphoreType.DMA((2,)), pltpu.SemaphoreType.DMA],
        compiler_params=pltpu.CompilerParams(vmem_limit_bytes=VMEM_LIMIT),
    )(*dsegs, hb)


def _conv_taps(ext_ref, cw_ref, cols, tm):
    c = None
    for j in range(4):
        term = cw_ref[3 - j:4 - j, cols] * ext_ref[SUBLANES - j:SUBLANES - j + tm, cols]
        c = term if c is None else c + term
    return c


def _fill_ext(ext_ref, u_ref, halo_ref, first):
    ext_ref[0:SUBLANES, :] = jnp.where(first, 0.0, halo_ref[...])
    ext_ref[SUBLANES:, :] = u_ref[...]


def _dn_prep_fwd(qkv_pre, ba, conv_w8, alog_row, dtb_row):
    s = qkv_pre.shape[0]
    tm = ROW_TILE

    def body(u_ref, halo_ref, cw_ref, ba_ref, al_ref, dtb_ref, q_ref, k_ref, v_ref, bg_ref, ext_ref):
        _fill_ext(ext_ref, u_ref, halo_ref, pl.program_id(0) == 0)
        for h in range(3 * DN_HEADS):
            cols = slice(h * LANES, (h + 1) * LANES)
            c = _conv_taps(ext_ref, cw_ref, cols, tm)
            a = c * _sigmoid(c)
            oc = slice((h % DN_HEADS) * LANES, (h % DN_HEADS + 1) * LANES)
            if h < 2 * DN_HEADS:
                rinv = lax.rsqrt(jnp.sum(a * a, axis=-1, keepdims=True) + NORM_EPS)
                if h < DN_HEADS:
                    q_ref[:, oc] = a * (rinv * DN_DK ** -0.5)
                else:
                    k_ref[:, oc] = a * rinv
            else:
                v_ref[:, oc] = a
        bav = ba_ref[...]
        lane = lax.broadcasted_iota(jnp.int32, bav.shape, 1)
        beta = _sigmoid(bav)
        g = -jnp.exp(al_ref[...]) * _softplus(bav + dtb_ref[...])
        bg_ref[...] = jnp.where(lane < DN_HEADS, beta, jnp.where(lane < 2 * DN_HEADS, g, 0.0))

    return _rows_call(
        body, "dn_prep_fwd", s,
        [(qkv_pre, "tile"), (qkv_pre, "prev8"), (conv_w8, "full"), (ba, "tile"), (alog_row, "full"), (dtb_row, "full")],
        [((s, D_MODEL), F32, "tile")] * 3 + [((s, LANES), F32, "tile")],
        scratch=[pltpu.VMEM((tm + SUBLANES, QKV_W), F32)])


def _dn_prep_bwd(qkv_pre, ba, conv_w8, alog_row, dtb_row, dq, dk, dv, dbg):
    s = qkv_pre.shape[0]
    tm = ROW_TILE

    def body(u_ref, halo_ref, cw_ref, ba_ref, al_ref, dtb_ref, dq_ref, dk_ref, dv_ref, dbg_ref,
             dc_ref, dba_ref, dsmall_ref, ext_ref):
        _fill_ext(ext_ref, u_ref, halo_ref, pl.program_id(0) == 0)
        for h in range(3 * DN_HEADS):
            cols = slice(h * LANES, (h + 1) * LANES)
            oc = slice((h % DN_HEADS) * LANES, (h % DN_HEADS + 1) * LANES)
            c = _conv_taps(ext_ref, cw_ref, cols, tm)
            sg = _sigmoid(c)
            a = c * sg
            if h < 2 * DN_HEADS:
                rinv = lax.rsqrt(jnp.sum(a * a, axis=-1, keepdims=True) + NORM_EPS)
                dy = dq_ref[:, oc] * DN_DK ** -0.5 if h < DN_HEADS else dk_ref[:, oc]
                da = rinv * dy - a * (rinv * rinv * rinv) * jnp.sum(dy * a, axis=-1, keepdims=True)
            else:
                da = dv_ref[:, oc]
            dc_ref[:, cols] = da * (sg * (1.0 + c * (1.0 - sg)))
        bav = ba_ref[...]
        dbgv = dbg_ref[...]
        lane = lax.broadcasted_iota(jnp.int32, bav.shape, 1)
        beta = _sigmoid(bav)
        ea = jnp.exp(al_ref[...])
        z = bav + dtb_ref[...]
        g = -ea * _softplus(z)
        is_b = lane < DN_HEADS
        is_g = jnp.logical_and(lane >= DN_HEADS, lane < 2 * DN_HEADS)
        d_aa = jnp.where(is_g, dbgv * (-ea) * _sigmoid(z), 0.0)
        dba = jnp.where(is_b, dbgv * beta * (1.0 - beta), d_aa)
        dba_ref[...] = dba.astype(dba_ref.dtype)
        r_alog = jnp.sum(jnp.where(is_g, dbgv * g, 0.0), axis=0, keepdims=True)
        r_dtb = jnp.sum(d_aa, axis=0, keepdims=True)
        _acc_add(dsmall_ref, jnp.concatenate([r_alog, r_dtb, jnp.zeros((SUBLANES - 2, LANES), F32)], axis=0))

    return _rows_call(
        body, "dn_prep_bwd", s,
        [(qkv_pre, "tile"), (qkv_pre, "prev8"), (conv_w8, "full"), (ba, "tile"), (alog_row, "full"), (dtb_row, "full"),
         (dq, "tile"), (dk, "tile"), (dv, "tile"), (dbg, "tile")],
        [((s, QKV_W), F32, "tile"), ((s, LANES), MXU, "tile"), ((SUBLANES, LANES), F32, "acc")],
        scratch=[pltpu.VMEM((tm + SUBLANES, QKV_W), F32)])


def _conv_bwd(dc, qkv_pre, conv_w8):
    s = dc.shape[0]
    tm = ROW_TILE
    steps = s // tm

    def body(dc_ref, dnext_ref, u_ref, halo_ref, cw_ref, du_ref, dcw_ref, extd_ref, ext_ref):
        i = pl.program_id(0)
        _fill_ext(ext_ref, u_ref, halo_ref, i == 0)
        extd_ref[0:tm, :] = dc_ref[...]
        extd_ref[tm:, :] = jnp.where(i == steps - 1, 0.0, dnext_ref[...])

        @pl.when(i == 0)
        def _():
            dcw_ref[...] = jnp.zeros_like(dcw_ref)

        for h in range(3 * DN_HEADS):
            cols = slice(h * LANES, (h + 1) * LANES)
            du = None
            for j in range(4):
                term = cw_ref[3 - j:4 - j, cols] * extd_ref[j:j + tm, cols]
                du = term if du is None else du + term
            du_ref[:, cols] = du.astype(du_ref.dtype)
            dcv = dc_ref[:, cols]
            for j in range(4):
                row = jnp.sum(dcv * ext_ref[SUBLANES - j:SUBLANES - j + tm, cols], axis=0, keepdims=True)
                dcw_ref[3 - j:4 - j, cols] += row

    return _rows_call(
        body, "conv_bwd", s,
        [(dc, "tile"), (dc, "next8"), (qkv_pre, "tile"), (qkv_pre, "prev8"), (conv_w8, "full")],
        [((s, QKV_W), MXU, "tile"), ((SUBLANES, QKV_W), F32, "acc")],
        scratch=[pltpu.VMEM((tm + SUBLANES, QKV_W), F32), pltpu.VMEM((tm + SUBLANES, QKV_W), F32)])


def _dn_out_fwd_tile(o_ref, z_ref, w_ref, wo_ref, on_ref, y_ref):
    for h in range(DN_HEADS):
        cols = _head_cols(h)
        ov = o_ref[:, cols]
        zv = z_ref[:, cols]
        ro = lax.rsqrt(jnp.mean(ov * ov, axis=-1, keepdims=True) + NORM_EPS)
        on_ref[:, cols] = (ov * ro * w_ref[...] * (zv * _sigmoid(zv))).astype(on_ref.dtype)
    y_ref[...] = jnp.dot(on_ref[...], wo_ref[...], preferred_element_type=F32)


def _dn_out_bwd_tile(dy_ref, o_ref, z_ref, w_ref, wo_ref, do_ref, dz_ref, d_ref):
    d_ref[...] = lax.dot_general(dy_ref[...], wo_ref[...], NT_DIMS, preferred_element_type=F32)
    acc = jnp.zeros((1, LANES), F32)
    for h in range(DN_HEADS):
        cols = _head_cols(h)
        dv, ov, zv = d_ref[:, cols], o_ref[:, cols], z_ref[:, cols]
        sg = _sigmoid(zv)
        sz = zv * sg
        ro = lax.rsqrt(jnp.mean(ov * ov, axis=-1, keepdims=True) + NORM_EPS)
        nv = ov * ro
        dn = dv * w_ref[...] * sz
        acc = acc + jnp.sum(dv * nv * sz, axis=0, keepdims=True)
        dz_ref[:, cols] = (dv * nv * w_ref[...] * (sg * (1.0 + zv * (1.0 - sg)))).astype(dz_ref.dtype)
        do_ref[:, cols] = ro * dn - ov * (ro * ro * ro) * jnp.mean(dn * ov, axis=-1, keepdims=True)
    return acc


def _attn_out_fwd_tile(o0, o1, o2, l0, l1, l2, z_ref, wo_ref, lse_ref, o_ref, g_ref, y_ref):
    a, b, c = l0[...], l1[...], l2[...]
    m = jnp.maximum(a, jnp.maximum(b, c))
    ea, eb, ec = jnp.exp(a - m), jnp.exp(b - m), jnp.exp(c - m)
    den = ea + eb + ec
    out = (ea * o0[...] + eb * o1[...] + ec * o2[...]) / den
    lse_ref[...] = m + jnp.log(den)
    o_ref[...] = out
    zv = z_ref[...]
    gated = (out * (zv * _sigmoid(zv))).astype(g_ref.dtype)
    g_ref[...] = gated
    y_ref[...] = jnp.dot(gated, wo_ref[...], preferred_element_type=F32)


def _attn_out_bwd_tile(dy_ref, o_ref, z_ref, wo_ref, do_ref, dz_ref, dl_ref):
    zv = z_ref[...]
    sg = _sigmoid(zv)
    dv = lax.dot_general(dy_ref[...], wo_ref[...], NT_DIMS, preferred_element_type=F32)
    ov = o_ref[...]
    do = dv * (zv * sg)
    do_ref[...] = do
    dz_ref[...] = (dv * ov * (sg * (1.0 + zv * (1.0 - sg)))).astype(dz_ref.dtype)
    for h in range(DIL_HEADS):
        cols = _head_cols(h)
        dl_ref[:, cols] = jnp.broadcast_to(jnp.sum(do[:, cols] * ov[:, cols], axis=-1, keepdims=True),
                                           (do.shape[0], LANES))


def _merge_out_final(ga, gb, ya, yb, x, target, w_out, wf_row):
    s, dm = x.shape

    def body(ga_ref, gb_ref, ya_ref, yb_ref, x_ref, t_ref, wo_ref, w_ref,
             loss_ref, dw_ref, m_ref, dxb_ref, dx_ref, dya_ref, dyb_ref, dga_ref, dgb_ref):
        sa, sb = _sigmoid(ga_ref[...]), _sigmoid(gb_ref[...])
        ya, yb = ya_ref[...], yb_ref[...]
        merged = (sa * ya + sb * yb).astype(MXU)
        m_ref[...] = merged
        x2 = x_ref[...] + jnp.dot(merged, wo_ref[...], preferred_element_type=F32)
        r = lax.rsqrt(jnp.mean(x2 * x2, axis=-1, keepdims=True) + NORM_EPS)
        w = w_ref[...]
        err = x2 * r * w - t_ref[...]
        tile_loss = 0.5 * jnp.sum(jnp.mean(err * err, axis=-1, keepdims=True), axis=0, keepdims=True)
        _acc_add(loss_ref, jnp.broadcast_to(tile_loss, (SUBLANES, LANES)))
        dy = err * (1.0 / dm)
        row = jnp.sum(dy * x2 * r, axis=0, keepdims=True)
        _acc_add(dw_ref, jnp.concatenate([row, jnp.zeros((SUBLANES - 1, dm), F32)], axis=0))
        dn = dy * w
        dx2 = r * dn - x2 * (r * r * r) * jnp.mean(dn * x2, axis=-1, keepdims=True)
        dx_ref[...] = dx2
        dxb = dx2.astype(MXU)
        dxb_ref[...] = dxb
        dmv = lax.dot_general(dxb, wo_ref[...], NT_DIMS, preferred_element_type=F32)
        dya_ref[...] = (dmv * sa).astype(dya_ref.dtype)
        dyb_ref[...] = (dmv * sb).astype(dyb_ref.dtype)
        dga_ref[...] = (dmv * ya * sa * (1.0 - sa)).astype(dga_ref.dtype)
        dgb_ref[...] = (dmv * yb * sb * (1.0 - sb)).astype(dgb_ref.dtype)

    return _rows_call(body, "merge_out_final", s,
                      [(ga, "tile"), (gb, "tile"), (ya, "tile"), (yb, "tile"), (x, "tile"), (target, "tile"),
                       (w_out, "full"), (wf_row, "full")],
                      [((SUBLANES, LANES), F32, "acc"), ((SUBLANES, dm), F32, "acc"), ((s, dm), MXU, "tile"),
                       ((s, dm), MXU, "tile"), ((s, dm), F32, "tile")] + [((s, dm), MXU, "tile")] * 4)


def _lane_pick(x, idx):
    lane = lax.broadcasted_iota(jnp.int32, x.shape, 1)
    return jnp.sum(jnp.where(lane == idx, x, 0.0), axis=-1, keepdims=True)


PAIR = 2 * DN_CHUNK
SCAN_CHUNKS = 4


def _bmm(a, b):
    return lax.dot_general(a.astype(MXU), b.astype(MXU), (((2,), (1,)), ((0,), (0,))), preferred_element_type=F32)


def _bmm_nt(a, b):
    return lax.dot_general(a.astype(MXU), b.astype(MXU), (((2,), (2,)), ((0,), (0,))), preferred_element_type=F32)


def _bmm_tn(a, b):
    return lax.dot_general(a.astype(MXU), b.astype(MXU), (((1,), (1,)), ((0,), (0,))), preferred_element_type=F32)


def _bmm3(a, b):
    ah = a.astype(jnp.bfloat16)
    al = (a - ah.astype(F32)).astype(jnp.bfloat16)
    bh = b.astype(jnp.bfloat16)
    bl = (b - bh.astype(F32)).astype(jnp.bfloat16)
    f = lambda p, q: lax.dot_general(p, q, (((2,), (1,)), ((0,), (0,))), preferred_element_type=F32)
    return f(ah, bh) + (f(ah, bl) + f(al, bh))


def _pair_masks():
    row = lax.broadcasted_iota(jnp.int32, (PAIR, PAIR), 0)
    col = lax.broadcasted_iota(jnp.int32, (PAIR, PAIR), 1)
    same = (row >= DN_CHUNK) == (col >= DN_CHUNK)
    return dict(causal=same & (row >= col), strict=same & (row > col), upper=same & (row <= col), eye=row == col,
                first=row < DN_CHUNK, row=row, lane=col)


def _pair_decay(bgv, masks):
    gc_all = _dot01(masks["causal"].astype(F32), bgv)
    out = []
    for h in range(DN_HEADS):
        beta = _lane_pick(bgv, h)
        gcb = jnp.broadcast_to(_lane_pick(gc_all, DN_HEADS + h), (PAIR, PAIR))
        gam = jnp.where(masks["causal"], jnp.exp(jnp.minimum(gcb - gcb.T, 0.0)), 0.0)
        gl = jnp.where(masks["first"], gcb[DN_CHUNK - 1:DN_CHUNK, :], gcb[PAIR - 1:PAIR, :])
        out.append((beta, gcb, gam, gl))
    return out


def _pair_inverse(a_strict, eye):
    eye_f = eye.astype(F32)[None]
    m = eye_f + a_strict
    x = eye_f - a_strict
    steps = int(math.log2(DN_CHUNK)) - 1
    for i in range(steps):
        mm = _bmm3 if i == steps - 1 else _bmm
        x = x + mm(x, eye_f - mm(m, x))
    return x


def _head_cols(h):
    return slice(h * LANES, (h + 1) * LANES)


def _delta_prep(q, k, v, bg):
    s = q.shape[0]
    c = DN_CHUNK
    n_chunks = s // c

    def body(q_ref, k_ref, v_ref, bg_ref, u_ref, w_ref, qd_ref, kd_ref, aqk_ref, dl_ref, t2_ref):
        masks = _pair_masks()
        dec = _pair_decay(bg_ref[...], masks)
        kbs, ks, gams, vbs, kbes, qs, qds, kds, dls = ([] for _ in range(9))
        for h in range(DN_HEADS):
            beta, gcb, gam, gl = dec[h]
            qh, kh, vh = q_ref[:, _head_cols(h)], k_ref[:, _head_cols(h)], v_ref[:, _head_cols(h)]
            eg = jnp.exp(gcb)
            kb = kh * beta
            kbs.append(kb); ks.append(kh); gams.append(gam); vbs.append(vh * beta); kbes.append(kb * eg)
            qs.append(qh); qds.append(qh * eg); kds.append(kh * jnp.exp(gl - gcb)); dls.append(jnp.exp(gl))
        st = lambda xs: jnp.stack(xs, axis=0)
        kmat, gam = st(ks), st(gams)
        a = jnp.where(masks["strict"][None], _bmm_nt(st(kbs), kmat) * gam, 0.0)
        t = _pair_inverse(a, masks["eye"])
        u = _bmm(t, st(vbs))
        w = _bmm(t, st(kbes))
        aqk = _bmm_nt(st(qs), kmat) * gam
        t2_ref[0] = t.astype(t2_ref.dtype)
        for half in range(2):
            rows = slice(half * c, (half + 1) * c)
            u_ref[half] = u[:, rows, :]
            w_ref[half] = w[:, rows, :].astype(w_ref.dtype)
            qd_ref[half] = st(qds)[:, rows, :].astype(qd_ref.dtype)
            kd_ref[half] = st(kds)[:, rows, :].astype(kd_ref.dtype)
            aqk_ref[half] = aqk[:, rows, rows].astype(aqk_ref.dtype)
            dl_ref[half] = st(dls)[:, half * c:half * c + SUBLANES, :]

    row_spec = lambda w_: pl.BlockSpec((PAIR, w_), lambda i: (i, 0))
    hm = lambda a_, b_: pl.BlockSpec((2, DN_HEADS, a_, b_), lambda i: (i, 0, 0, 0))
    hm_shape = lambda a_, b_, dt: jax.ShapeDtypeStruct((n_chunks, DN_HEADS, a_, b_), dt)
    return _pcall(
        body, name="delta_prep", grid=(n_chunks // 2,),
        in_specs=[row_spec(D_MODEL)] * 3 + [row_spec(LANES)],
        out_specs=[hm(c, LANES)] * 4 + [hm(c, c), hm(SUBLANES, LANES),
                   pl.BlockSpec((1, DN_HEADS, PAIR, PAIR), lambda i: (i, 0, 0, 0))],
        out_shape=[hm_shape(c, LANES, F32), hm_shape(c, LANES, MXU), hm_shape(c, LANES, MXU), hm_shape(c, LANES, MXU),
                   hm_shape(c, c, MXU), hm_shape(SUBLANES, LANES, F32),
                   jax.ShapeDtypeStruct((n_chunks // 2, DN_HEADS, PAIR, PAIR), MXU)],
        compiler_params=_params("parallel"),
    )(q, k, v, bg)


def _delta_scan_fwd(u, w, qd, kd, aqk, dl, z, dnw_row, w_o_dn, attn_parts, attn_lses, zb, w_o_dil):
    n_chunks = u.shape[0]
    c = DN_CHUNK
    g_n = SCAN_CHUNKS

    def body(u_ref, w_ref, qd_ref, kd_ref, aqk_ref, dl_ref, z_ref, nw_ref, wo_ref,
             p0, p1, p2, l0, l1, l2, zb_ref, wod_ref,
             o_ref, vnew_ref, st_ref, on_ref, y_ref, lse_ref, oj_ref, gb_ref, yb_ref, state):
        @pl.when(pl.program_id(0) == 0)
        def _():
            state[...] = jnp.zeros_like(state)

        _attn_out_fwd_tile(p0, p1, p2, l0, l1, l2, zb_ref, wod_ref, lse_ref, oj_ref, gb_ref, yb_ref)
        for g in range(g_n):
            sv = state[...]
            sb = sv.astype(MXU)
            vnew = u_ref[g] - _bmm(w_ref[g], sb)
            o = _bmm(qd_ref[g], sb) + _bmm(aqk_ref[g], vnew)
            state[...] = sv * dl_ref[g][:, 0:1, :] + _bmm_tn(kd_ref[g], vnew)
            vnew_ref[g] = vnew.astype(vnew_ref.dtype)
            st_ref[g] = sb
            for h in range(DN_HEADS):
                o_ref[g * c:(g + 1) * c, _head_cols(h)] = o[h]
        _dn_out_fwd_tile(o_ref, z_ref, nw_ref, wo_ref, on_ref, y_ref)

    hm = lambda a_, b_: pl.BlockSpec((g_n, DN_HEADS, a_, b_), lambda i: (i, 0, 0, 0))
    rows = lambda width: pl.BlockSpec((g_n * c, width), lambda i: (i, 0))
    whole = lambda t: pl.BlockSpec(t.shape, lambda i: (0, 0))
    full = lambda width, dt: jax.ShapeDtypeStruct((n_chunks * c, width), dt)
    return _pcall(
        body, name="delta_scan_fwd", grid=(n_chunks // g_n,),
        in_specs=[hm(c, LANES)] * 4 + [hm(c, c), hm(SUBLANES, LANES), rows(D_MODEL), whole(dnw_row), whole(w_o_dn)]
        + [rows(DIL_W)] * 7 + [whole(w_o_dil)],
        out_specs=[rows(D_MODEL), hm(c, LANES), hm(DN_DK, DN_DK), rows(D_MODEL), rows(D_MODEL),
                   rows(DIL_W), rows(DIL_W), rows(DIL_W), rows(D_MODEL)],
        out_shape=[full(D_MODEL, F32),
                   jax.ShapeDtypeStruct((n_chunks, DN_HEADS, c, LANES), MXU),
                   jax.ShapeDtypeStruct((n_chunks, DN_HEADS, DN_DK, DN_DK), MXU),
                   full(D_MODEL, MXU), full(D_MODEL, F32),
                   full(DIL_W, F32), full(DIL_W, F32), full(DIL_W, MXU), full(D_MODEL, F32)],
        scratch_shapes=[pltpu.VMEM((DN_HEADS, DN_DK, DN_DK), F32)],
        compiler_params=_params("arbitrary"),
    )(u, w, qd, kd, aqk, dl, z, dnw_row, w_o_dn, *attn_parts, *attn_lses, zb, w_o_dil)


def _delta_scan_bwd(w, qd, kd, aqk, dl, vnew, st, dy, o, z, dnw_row, w_o_dn, dyb, o_joint, zb, w_o_dil):
    n_chunks = w.shape[0]
    c = DN_CHUNK
    g_n = SCAN_CHUNKS
    steps = n_chunks // g_n

    def body(w_ref, qd_ref, kd_ref, aqk_ref, dl_ref, vnew_ref, st_ref, dy_ref, o_ref, z_ref, nw_ref, wo_ref,
             dyb_ref, oj_ref, zb_ref, wod_ref,
             dvnew_ref, dkd_ref, ddl_ref, do_ref, dz_ref, dnw_ref, dob_ref, dzb_ref, delta_ref, dstate, d_scratch):
        @pl.when(pl.program_id(0) == 0)
        def _():
            dstate[...] = jnp.zeros_like(dstate)

        _attn_out_bwd_tile(dyb_ref, oj_ref, zb_ref, wod_ref, dob_ref, dzb_ref, delta_ref)
        acc = _dn_out_bwd_tile(dy_ref, o_ref, z_ref, nw_ref, wo_ref, do_ref, dz_ref, d_scratch)
        _acc_add(dnw_ref, jnp.concatenate([acc, jnp.zeros((SUBLANES - 1, LANES), F32)], axis=0))
        for g in reversed(range(g_n)):
            ds = dstate[...]
            dsb = ds.astype(MXU)
            doh = jnp.stack([do_ref[g * c:(g + 1) * c, _head_cols(h)] for h in range(DN_HEADS)], axis=0)
            dvnew = _bmm_tn(aqk_ref[g], doh) + _bmm(kd_ref[g], dsb)
            dkd_ref[g] = _bmm_nt(vnew_ref[g], dsb)
            ddl = jnp.sum(jnp.sum(st_ref[g].astype(F32) * ds, axis=2, keepdims=True), axis=1, keepdims=True)
            ddl_ref[g] = jnp.broadcast_to(ddl, (DN_HEADS, SUBLANES, LANES))
            dstate[...] = ds * dl_ref[g][:, 0:1, :] + _bmm_tn(qd_ref[g], doh) - _bmm_tn(w_ref[g], dvnew)
            dvnew_ref[g] = dvnew.astype(dvnew_ref.dtype)

    rev = lambda i: steps - 1 - i
    hm = lambda a_, b_: pl.BlockSpec((g_n, DN_HEADS, a_, b_), lambda i: (rev(i), 0, 0, 0))
    rows = lambda width: pl.BlockSpec((g_n * c, width), lambda i: (rev(i), 0))
    whole = lambda t: pl.BlockSpec(t.shape, lambda i: (0, 0))
    full = lambda width, dt: jax.ShapeDtypeStruct((n_chunks * c, width), dt)
    return _pcall(
        body, name="delta_scan_bwd", grid=(steps,),
        in_specs=[hm(c, LANES)] * 3 + [hm(c, c), hm(SUBLANES, LANES), hm(c, LANES), hm(DN_DK, DN_DK),
                  rows(D_MODEL), rows(D_MODEL), rows(D_MODEL), whole(dnw_row), whole(w_o_dn),
                  rows(D_MODEL), rows(DIL_W), rows(DIL_W), whole(w_o_dil)],
        out_specs=[hm(c, LANES), hm(c, LANES), hm(SUBLANES, LANES), rows(D_MODEL), rows(D_MODEL),
                   pl.BlockSpec((SUBLANES, LANES), lambda i: (0, 0)), rows(DIL_W), rows(DIL_W), rows(DIL_W)],
        out_shape=[jax.ShapeDtypeStruct((n_chunks, DN_HEADS, c, LANES), MXU),
                   jax.ShapeDtypeStruct((n_chunks, DN_HEADS, c, LANES), F32),
                   jax.ShapeDtypeStruct((n_chunks, DN_HEADS, SUBLANES, LANES), F32),
                   full(D_MODEL, F32), full(D_MODEL, MXU), jax.ShapeDtypeStruct((SUBLANES, LANES), F32),
                   full(DIL_W, F32), full(DIL_W, MXU), full(DIL_W, F32)],
        scratch_shapes=[pltpu.VMEM((DN_HEADS, DN_DK, DN_DK), F32), pltpu.VMEM((g_n * c, D_MODEL), F32)],
        compiler_params=_params("arbitrary"),
    )(w, qd, kd, aqk, dl, vnew, st, dy, o, z, dnw_row, w_o_dn, dyb, o_joint, zb, w_o_dil)


def _delta_post_bwd(q, k, v, bg, t2, st, vnew, do, dvnew, dkd, ddl):
    s = q.shape[0]
    c = DN_CHUNK

    def body(q_ref, k_ref, v_ref, bg_ref, t2_ref, st_ref, vnew_ref, do_ref, dvnew_ref, dkd_ref, ddl_ref,
             dq_ref, dk_ref, dv_ref, dbg_ref):
        masks = _pair_masks()
        first = masks["first"][None]
        dec = _pair_decay(bg_ref[...], masks)
        st_ = lambda xs: jnp.stack(xs, axis=0)
        heads = range(DN_HEADS)
        qm_, km_, vm_, dom = (st_([r[:, _head_cols(h)] for h in heads]) for r in (q_ref, k_ref, v_ref, do_ref))
        beta = st_([dec[h][0] for h in heads])
        gcb = st_([dec[h][1] for h in heads])
        gam = st_([dec[h][2] for h in heads])
        gl = st_([dec[h][3] for h in heads])
        pair = lambda ref: jnp.concatenate([ref[0], ref[1]], axis=1)
        vnew2, dvnew2, dkd2 = pair(vnew_ref), pair(dvnew_ref), pair(dkd_ref)
        halves = lambda x: (x[:, :c, :], x[:, c:, :])
        by_state = lambda x: jnp.concatenate([_bmm_nt(xh, st_ref[i]) for i, xh in enumerate(halves(x))], axis=1)
        dqd = by_state(dom)
        dw = -by_state(dvnew2)
        ddl2 = jnp.where(first, ddl_ref[0][:, 0:1, :], ddl_ref[1][:, 0:1, :])

        eg = jnp.exp(gcb)
        egl = jnp.exp(gl - gcb)
        dl = jnp.exp(gl)
        kb = km_ * beta
        kk = _bmm_nt(kb, km_)
        a = jnp.where(masks["strict"][None], kk * gam, 0.0)
        t = t2_ref[0]
        vb = vm_ * beta
        kbe = kb * eg
        u = _bmm(t, vb)
        w = _bmm(t, kbe)
        aqk = _bmm_nt(qm_, km_) * gam
        qd = qm_ * eg
        kd = km_ * egl

        daqk = jnp.where(masks["causal"][None], _bmm_nt(dom, vnew2), 0.0)
        dvb = _bmm_tn(t, dvnew2)
        dkbe = _bmm_tn(t, dw)
        da = jnp.where(masks["strict"][None], -(_bmm_nt(dvb, u) + _bmm_nt(dkbe, w)), 0.0)
        pm = da * gam
        qmm = daqk * gam
        dkb = _bmm(pm, km_) + dkbe * eg
        dkh = _bmm_tn(pm, kb) + _bmm_tn(qmm, qm_) + dkd2 * egl + dkb * beta
        dqh = _bmm(qmm, km_) + dqd * eg
        xm = da * a + daqk * aqk
        col_rows = jnp.concatenate([jnp.zeros((DN_HEADS, PAIR), F32), jnp.sum(xm, axis=1),
                                    jnp.zeros((PAIR - 2 * DN_HEADS, PAIR), F32)], axis=0)
        tmp = jnp.sum(dkd2 * kd, axis=-1, keepdims=True)
        dgc = (jnp.sum(xm, axis=-1, keepdims=True) + jnp.sum(dkbe * kbe, axis=-1, keepdims=True)
               + jnp.sum(dqd * qd, axis=-1, keepdims=True) - tmp)
        sum0 = jnp.sum(jnp.where(first, tmp, 0.0), axis=1, keepdims=True)
        sum1 = jnp.sum(jnp.where(first, 0.0, tmp), axis=1, keepdims=True)
        dgl = jnp.where(first, sum0, sum1) + ddl2 * dl
        last = (masks["row"] == c - 1) | (masks["row"] == PAIR - 1)
        dgc = dgc + jnp.where(last[None], dgl, 0.0)
        dbeta = jnp.sum(dvb * vm_, axis=-1, keepdims=True) + jnp.sum(dkb * km_, axis=-1, keepdims=True)
        dvh = dvb * beta

        lane = masks["lane"]
        dgc_lanes = jnp.zeros((PAIR, LANES), F32)
        dbg = jnp.zeros((PAIR, LANES), F32)
        for h in heads:
            dq_ref[:, _head_cols(h)] = dqh[h]
            dk_ref[:, _head_cols(h)] = dkh[h]
            dv_ref[:, _head_cols(h)] = dvh[h]
            dgc_lanes = dgc_lanes + jnp.where(lane == DN_HEADS + h, dgc[h], 0.0)
            dbg = dbg + jnp.where(lane == h, dbeta[h], 0.0)
        dbg_ref[...] = dbg + _dot01(masks["upper"].astype(F32), dgc_lanes - col_rows.T)

    n_pairs = s // PAIR
    row_spec = lambda w_: pl.BlockSpec((PAIR, w_), lambda i: (i, 0))
    hm = lambda a_, b_: pl.BlockSpec((2, DN_HEADS, a_, b_), lambda i: (i, 0, 0, 0))
    return _pcall(
        body, name="delta_post_bwd", grid=(n_pairs,),
        in_specs=[row_spec(D_MODEL)] * 3 + [row_spec(LANES), pl.BlockSpec((1, DN_HEADS, PAIR, PAIR), lambda i: (i, 0, 0, 0)),
                  hm(DN_DK, DN_DK), hm(c, LANES), row_spec(D_MODEL), hm(c, LANES), hm(c, LANES), hm(SUBLANES, LANES)],
        out_specs=[row_spec(D_MODEL)] * 3 + [row_spec(LANES)],
        out_shape=[jax.ShapeDtypeStruct((s, D_MODEL), F32)] * 3 + [jax.ShapeDtypeStruct((s, LANES), F32)],
        compiler_params=_params("parallel"),
    )(q, k, v, bg, t2, st, vnew, do, dvnew, dkd, ddl)


def _alibi_slope(group, head):
    n = N_DIL * DIL_HEADS
    return float(2.0 ** (-8.0 * (group * DIL_HEADS + head + 1) / n))


def _attn_plan(s, group):
    window, dil = DIL_GROUPS[group]
    assert window // dil == ATT_BLOCK
    assert (s // dil) % ATT_BLOCK == 0, "sub-sequence length must be a whole number of attention blocks"
    return dil, s // dil // ATT_BLOCK, (DIL_HEADS if dil == 1 else 1)


def _attn_specs(group, dil, nb, hp):
    rows = ATT_BLOCK * dil

    def spec(col0, shift):
        if shift < 0:
            f = lambda hb, n: (jnp.maximum(n - 1, 0), col0 + hb)
        elif shift > 0:
            f = lambda hb, n: (jnp.minimum(n + 1, nb - 1), col0 + hb)
        else:
            f = lambda hb, n: (jnp.minimum(n, nb - 1), col0 + hb)
        return pl.BlockSpec((rows, hp * LANES), f)

    return (lambda shift: spec(group * (DIL_HEADS // hp), shift)), (lambda shift: spec(0, shift))


def _sub_rows(ref, r, dil, cols):
    return ref[:, cols] if dil == 1 else ref[pl.ds(r, ATT_BLOCK, stride=dil), cols]


def _set_sub_rows(ref, r, dil, cols, value):
    if dil == 1:
        ref[:, cols] = value
    else:
        ref[pl.ds(r, ATT_BLOCK, stride=dil), cols] = value


def _step_slope(group, hp, hh):
    if hp == DIL_HEADS:
        return _alibi_slope(group, hh)
    hb = pl.program_id(0)
    slope = _alibi_slope(group, DIL_HEADS - 1)
    for h in reversed(range(DIL_HEADS - 1)):
        slope = jnp.where(hb == h, _alibi_slope(group, h), slope)
    return slope


def _attn_items(hp, dil):
    return [(hh, r) for hh in range(hp) for r in range(dil)]


def _attn_stack(ref, items, dil, dtype=MXU):
    return jnp.stack([_sub_rows(ref, r, dil, _head_cols(hh)).astype(dtype) for hh, r in items], axis=0)


def _attn_slopes(group, hp, items):
    if hp == 1:
        return _step_slope(group, hp, 0)
    return jnp.stack([jnp.full((1, 1), _alibi_slope(group, hh), F32) for hh, _ in items], axis=0)


def _window_bias(dil, n):
    a = lax.broadcasted_iota(jnp.int32, (ATT_BLOCK, 2 * ATT_BLOCK), 0)
    b = lax.broadcasted_iota(jnp.int32, (ATT_BLOCK, 2 * ATT_BLOCK), 1)
    dist = ATT_BLOCK + a - b
    valid = (dist >= 0) & (dist <= ATT_BLOCK) & ((b >= ATT_BLOCK) | (n > 0))
    return (dist * dil).astype(F32), valid


def _attn_fwd(qb, kb, vb, group):
    s = qb.shape[0]
    dil, nb, hp = _attn_plan(s, group)
    qkv, per_head = _attn_specs(group, dil, nb, hp)

    def body(q_ref, kp_ref, kc_ref, vp_ref, vc_ref, o_ref, lse_ref):
        n = pl.program_id(1)
        distd, valid = _window_bias(dil, n)
        items = _attn_items(hp, dil)
        sub = lambda ref: _attn_stack(ref, items, dil)
        kk = jnp.concatenate([sub(kp_ref), sub(kc_ref)], axis=1)
        vv = jnp.concatenate([sub(vp_ref), sub(vc_ref)], axis=1)
        sc = _bmm_nt(sub(q_ref), kk) * DIL_DH ** -0.5 - _attn_slopes(group, hp, items) * distd
        sc = jnp.where(valid, sc, -1e30)
        mx = jnp.max(sc, axis=-1, keepdims=True)
        p = jnp.where(valid, jnp.exp(sc - mx), 0.0)
        den = jnp.sum(p, axis=-1, keepdims=True)
        out = _bmm(p, vv) / den
        lse = mx + jnp.log(den)
        for b, (hh, r) in enumerate(items):
            _set_sub_rows(o_ref, r, dil, _head_cols(hh), out[b])
            _set_sub_rows(lse_ref, r, dil, _head_cols(hh), jnp.broadcast_to(lse[b], (ATT_BLOCK, LANES)))

    return _pcall(
        body, name=f"attn_fwd_g{group}", grid=(DIL_HEADS // hp, nb),
        in_specs=[qkv(0), qkv(-1), qkv(0), qkv(-1), qkv(0)], out_specs=[per_head(0)] * 2,
        out_shape=[jax.ShapeDtypeStruct((s, DIL_W), F32)] * 2,
        compiler_params=_params("parallel", "parallel"),
    )(qb, kb, kb, vb, vb)


def _attn_bwd(qb, kb, vb, d_o, lse, delta, group):
    s = qb.shape[0]
    dil, nb, hp = _attn_plan(s, group)
    qkv, per_head = _attn_specs(group, dil, nb, hp)
    scale = DIL_DH ** -0.5

    def body(q_ref, kp_ref, kc_ref, vp_ref, vc_ref, do_ref, l_ref, dl_ref, dq_ref, dk_ref, dv_ref,
             dq_acc, dk_done, dv_done, dk_carry, dv_carry):
        n = pl.program_id(1)
        items = _attn_items(hp, dil)
        slopes = _attn_slopes(group, hp, items)

        @pl.when(n == 0)
        def _():
            dk_carry[...] = jnp.zeros_like(dk_carry)
            dv_carry[...] = jnp.zeros_like(dv_carry)

        @pl.when(n < nb)
        def _():
            distd, valid = _window_bias(dil, n)
            sub = lambda ref, dtype=MXU: _attn_stack(ref, items, dil, dtype)
            qc, do = sub(q_ref), sub(do_ref)
            kk = jnp.concatenate([sub(kp_ref), sub(kc_ref)], axis=1)
            vv = jnp.concatenate([sub(vp_ref), sub(vc_ref)], axis=1)
            sc = _bmm_nt(qc, kk) * scale - slopes * distd
            p = jnp.where(valid, jnp.exp(jnp.minimum(sc - jnp.concatenate([sub(l_ref, F32)] * 2, axis=2), 0.0)), 0.0)
            dsc = p * (_bmm_nt(do, vv) - jnp.concatenate([sub(dl_ref, F32)] * 2, axis=2))
            dq = _bmm(dsc, kk) * scale
            dkk = _bmm_tn(dsc, qc) * scale
            dvv = _bmm_tn(p, do)
            for b, (hh, r) in enumerate(items):
                cols = _head_cols(hh)
                _set_sub_rows(dq_acc, r, dil, cols, dq[b])
                _set_sub_rows(dk_done, r, dil, cols, _sub_rows(dk_carry, r, dil, cols) + dkk[b, :ATT_BLOCK])
                _set_sub_rows(dv_done, r, dil, cols, _sub_rows(dv_carry, r, dil, cols) + dvv[b, :ATT_BLOCK])
                _set_sub_rows(dk_carry, r, dil, cols, dkk[b, ATT_BLOCK:])
                _set_sub_rows(dv_carry, r, dil, cols, dvv[b, ATT_BLOCK:])
            dq_ref[...] = dq_acc[...].astype(dq_ref.dtype)
            dk_ref[...] = dk_done[...].astype(dk_ref.dtype)
            dv_ref[...] = dv_done[...].astype(dv_ref.dtype)

        @pl.when(n == nb)
        def _():
            dk_ref[...] = dk_carry[...].astype(dk_ref.dtype)
            dv_ref[...] = dv_carry[...].astype(dv_ref.dtype)

    return _pcall(
        body, name=f"attn_bwd_g{group}", grid=(DIL_HEADS // hp, nb + 1),
        in_specs=[qkv(0), qkv(-1), qkv(0), qkv(-1), qkv(0)] + [per_head(0)] * 3,
        out_specs=[per_head(0), per_head(-1), per_head(-1)],
        out_shape=[jax.ShapeDtypeStruct((s, DIL_W), MXU)] * 3,
        scratch_shapes=[pltpu.VMEM((ATT_BLOCK * dil, hp * LANES), F32)] * 5,
        compiler_params=_params("parallel", "arbitrary"),
    )(qb, kb, kb, vb, vb, d_o, lse, delta)


def _my_place():
    mx, my, mc = lax.axis_index("x"), lax.axis_index("y"), lax.axis_index("c")
    return mx, my, mc, 4 * mx + 2 * my + mc


N_CHIPS = 4


def _shard_row_tile(r):
    if r <= 512:
        return r
    return 128 if r % 128 == 0 else 480


def _other_chips(mx, my):
    return [(1 - mx, my), (mx, 1 - my), (1 - mx, 1 - my)]


def _all_gather(xs, name):
    n = len(xs)
    halved = [x.shape[1] % (2 * LANES) == 0 and x.size * x.dtype.itemsize >= (1 << 20) for x in xs]
    n_sems = 8

    def body(*refs):
        x_refs, o_refs = refs[:n], refs[n:2 * n]
        send_sems, recv_sems, local_sems = refs[2 * n:]
        mx, my, mc, me = _my_place()
        sibling, sibling_id = (mx, my, 1 - mc), 4 * mx + 2 * my + (1 - mc)
        x_nbr, y_nbr, diag = _other_chips(mx, my)
        slot_of = lambda chip, c: 4 * chip[0] + 2 * chip[1] + c

        def part(ref, a, half):
            if not halved[a]:
                return ref
            width = xs[a].shape[1] // 2
            return ref.at[:, pl.ds(half * width, width)]

        def copy(a, k, dst, to, src=None):
            return pltpu.make_async_remote_copy(
                src_ref=dst if src is None else src, dst_ref=dst, send_sem=send_sems.at[a, k],
                recv_sem=recv_sems.at[a, k], device_id=to, device_id_type=MESH)

        local = [pltpu.make_async_copy(x_refs[a], o_refs[a].at[me], local_sems.at[a]) for a in range(n)]
        for cp in local:
            cp.start()
        sends = []
        for a in range(n):
            mine = o_refs[a].at[me]
            sends += [copy(a, 0, mine, sibling, src=x_refs[a]), copy(a, 1, mine, (*x_nbr, mc), src=x_refs[a]),
                      copy(a, 2, mine, (*y_nbr, mc), src=x_refs[a])]
        for cp in sends:
            cp.start()
        for a in range(n):
            blk = o_refs[a].at[slot_of(x_nbr, mc)]
            copy(a, 1, blk, (*x_nbr, mc)).wait_recv()
            sends += [copy(a, 3, blk, sibling), copy(a, 5, part(blk, a, 0), (*y_nbr, mc))]
            sends[-2].start()
            sends[-1].start()
        for a in range(n):
            blk = o_refs[a].at[slot_of(y_nbr, mc)]
            copy(a, 2, blk, (*y_nbr, mc)).wait_recv()
            sends.append(copy(a, 4, blk, sibling))
            sends[-1].start()
            if halved[a]:
                sends.append(copy(a, 6, part(blk, a, 1), (*x_nbr, mc)))
                sends[-1].start()
        for a in range(n):
            blk = o_refs[a].at[slot_of(diag, mc)]
            copy(a, 5, part(blk, a, 0), (*y_nbr, mc)).wait_recv()
            if halved[a]:
                copy(a, 6, part(blk, a, 1), (*x_nbr, mc)).wait_recv()
            sends.append(copy(a, 7, blk, sibling))
            sends[-1].start()
        for a in range(n):
            copy(a, 0, o_refs[a].at[sibling_id], sibling).wait_recv()
            for k, chip in ((3, x_nbr), (4, y_nbr), (7, diag)):
                copy(a, k, o_refs[a].at[slot_of(chip, 1 - mc)], sibling).wait_recv()
        for cp in sends:
            cp.wait_send()
        for cp in local:
            cp.wait()

    any_spec = pl.BlockSpec(memory_space=pl.ANY)
    return _pcall(
        body, name=name,
        in_specs=[any_spec] * n, out_specs=[any_spec] * n,
        out_shape=[jax.ShapeDtypeStruct((N_DEV,) + x.shape, x.dtype) for x in xs],
        scratch_shapes=[pltpu.SemaphoreType.DMA((n, n_sems)), pltpu.SemaphoreType.DMA((n, n_sems)),
                        pltpu.SemaphoreType.DMA((n,))],
    )(*xs)


def _pair_exchange(gs, name):
    n = len(gs)

    def body(*refs):
        g_refs, o_refs = refs[:n], refs[n:2 * n]
        send_sems, recv_sems = refs[2 * n:]
        mx, my, mc, _ = _my_place()
        copies = [pltpu.make_async_remote_copy(
            src_ref=g_refs[a].at[p, 1 - mc], dst_ref=o_refs[a].at[p], send_sem=send_sems.at[a, p],
            recv_sem=recv_sems.at[a, p], device_id=(mx, my, 1 - mc), device_id_type=MESH)
            for a in range(n) for p in range(N_CHIPS)]
        for cp in copies:
            cp.start()
        for cp in copies:
            cp.wait()

    any_spec = pl.BlockSpec(memory_space=pl.ANY)
    return _pcall(
        body, name=name,
        in_specs=[any_spec] * n, out_specs=[any_spec] * n,
        out_shape=[jax.ShapeDtypeStruct((N_CHIPS,) + g.shape[2:], g.dtype) for g in gs],
        scratch_shapes=[pltpu.SemaphoreType.DMA((n, N_CHIPS)), pltpu.SemaphoreType.DMA((n, N_CHIPS))],
    )(*gs)


def _pair_add(g, other, name):
    chips, _, r, c = g.shape
    tr = _shard_row_tile(r)
    core = lax.axis_index("c").astype(jnp.int32).reshape(1)

    def body(core_ref, g_ref, o_ref, h_ref):
        h_ref[...] = (g_ref[...].astype(F32)[0] + o_ref[...].astype(F32)).astype(h_ref.dtype)

    blk = pl.BlockSpec((1, tr, c), lambda p, i, core_ref: (p, i, 0))
    return _pcall(
        body, name=name,
        grid_spec=pltpu.PrefetchScalarGridSpec(
            num_scalar_prefetch=1, grid=(chips, pl.cdiv(r, tr)),
            in_specs=[pl.BlockSpec((1, 1, tr, c), lambda p, i, core_ref: (p, core_ref[0], i, 0)), blk],
            out_specs=blk),
        out_shape=jax.ShapeDtypeStruct((chips, r, c), g.dtype),
        compiler_params=_params("parallel", "parallel"),
    )(core, g, other)


def _chip_exchange(hs, name):
    n = len(hs)

    def body(*refs):
        h_refs, o_refs = refs[:n], refs[n:2 * n]
        send_sems, recv_sems, local_sems = refs[2 * n:]
        mx, my, mc, _ = _my_place()
        my_chip = 2 * mx + my
        chips = _other_chips(mx, my)
        local = [pltpu.make_async_copy(h_refs[a].at[my_chip], o_refs[a].at[my_chip], local_sems.at[a]) for a in range(n)]
        for cp in local:
            cp.start()
        for j, (px, py) in enumerate(chips):
            for a in range(n):
                pltpu.make_async_remote_copy(
                    src_ref=h_refs[a].at[2 * px + py], dst_ref=o_refs[a].at[my_chip], send_sem=send_sems.at[a, j],
                    recv_sem=recv_sems.at[a, j], device_id=(px, py, mc), device_id_type=MESH).start()
        for j, (px, py) in enumerate(chips):
            for a in range(n):
                pltpu.make_async_remote_copy(
                    src_ref=h_refs[a].at[2 * px + py], dst_ref=o_refs[a].at[2 * px + py], send_sem=send_sems.at[a, j],
                    recv_sem=recv_sems.at[a, j], device_id=(px, py, mc), device_id_type=MESH).wait()
        for cp in local:
            cp.wait()

    any_spec = pl.BlockSpec(memory_space=pl.ANY)
    return _pcall(
        body, name=name,
        in_specs=[any_spec] * n, out_specs=[any_spec] * n,
        out_shape=[jax.ShapeDtypeStruct(h.shape, h.dtype) for h in hs],
        scratch_shapes=[pltpu.SemaphoreType.DMA((n, N_CHIPS - 1)), pltpu.SemaphoreType.DMA((n, N_CHIPS - 1)),
                        pltpu.SemaphoreType.DMA((n,))],
    )(*hs)


def _adamw(parts, w, m, v, name):
    r, c = w.shape
    n_parts = parts.shape[0]
    tr = _shard_row_tile(r)
    bc1 = 1.0 - ADAM_B1 ** ADAM_STEP
    bc2 = 1.0 - ADAM_B2 ** ADAM_STEP

    def body(p_ref, w_ref, m_ref, v_ref, g_ref, d_ref, nm_ref, nv_ref):
        g = p_ref[0].astype(F32)
        for j in range(1, n_parts):
            g = g + p_ref[j].astype(F32)
        nm = ADAM_B1 * m_ref[...] + (1.0 - ADAM_B1) * g
        nv = ADAM_B2 * v_ref[...] + (1.0 - ADAM_B2) * (g * g)
        g_ref[...] = g
        nm_ref[...] = nm
        nv_ref[...] = nv
        d_ref[...] = -ADAM_LR * ((nm / bc1) / (jnp.sqrt(nv / bc2) + ADAM_EPS) + ADAM_WD * w_ref[...])

    blk = pl.BlockSpec((tr, c), lambda i: (i, 0))
    return _pcall(
        body, name=name, grid=(pl.cdiv(r, tr),),
        in_specs=[pl.BlockSpec((n_parts, tr, c), lambda i: (0, i, 0)), blk, blk, blk],
        out_specs=[blk] * 4, out_shape=[jax.ShapeDtypeStruct((r, c), F32)] * 4,
        compiler_params=_params("parallel"),
    )(parts, w, m, v)


def _local_step(x, target, norm_w, w_segs, conv_w, a_log, dt_bias, dn_norm_w, w_o_dn, w_o_dil, w_out, final_norm_w):
    s = x.shape[0]
    w_qkv, w_za, w_ba, w_qb, w_kb, w_vb, w_zb, w_ga, w_gb = w_segs
    conv_w8 = jnp.concatenate([conv_w, jnp.zeros((SUBLANES - conv_w.shape[0], QKV_W), F32)], axis=0)
    pad8 = jnp.zeros((1, DN_HEADS), F32)
    alog_row = jnp.concatenate([pad8, a_log, jnp.zeros((1, LANES - 2 * DN_HEADS), F32)], axis=1)
    dtb_row = jnp.concatenate([pad8, dt_bias, jnp.zeros((1, LANES - 2 * DN_HEADS), F32)], axis=1)
    wf_row = final_norm_w.reshape(1, D_MODEL)

    hb, qkv_pre, z_a, ba, z_b = _rms_proj_fwd(x, norm_w, [w_qkv, w_za, w_ba, w_zb], "rms_proj_fwd_a")
    q_b, k_b, v_b, g_a, g_b = _mm_out(hb, [w_qb, w_kb, w_vb, w_ga, w_gb], "proj_fwd_b", w_is_out_by_in=True)

    qn, kn, vn, bg = _dn_prep_fwd(qkv_pre, ba, conv_w8, alog_row, dtb_row)
    u_d, w_d, qd_d, kd_d, aqk_d, dl_d, t2_d = _delta_prep(qn, kn, vn, bg)
    parts, lses = [], []
    for gi in range(N_DIL):
        o_g, l_g = _attn_fwd(q_b, k_b, v_b, gi)
        parts.append(o_g)
        lses.append(l_g)
    o_a, vnew_d, st_d, on_b, y_a, lse, o_joint, ob_b, y_b = _delta_scan_fwd(
        u_d, w_d, qd_d, kd_d, aqk_d, dl_d, z_a, dn_norm_w, w_o_dn, parts, lses, z_b, w_o_dil)

    loss8, dwf8, merged_b, dx2_b, dx2, dya_b, dyb_b, dga_b, dgb_b = _merge_out_final(
        g_a, g_b, y_a, y_b, x, target, w_out, wf_row)

    g_w_out = _mm_tn(merged_b, dx2_b, "out_wgrad")
    g_w_o_dn = _mm_tn(on_b, dya_b, "out_dn_wgrad")

    g_w_o_dil = _mm_tn(ob_b, dyb_b, "out_dil_wgrad")
    dvnew_d, dkd_d, ddl_d, d_o_a, dza_b, ddnw8, d_o, dzb_b, delta = _delta_scan_bwd(
        w_d, qd_d, kd_d, aqk_d, dl_d, vnew_d, st_d, dya_b, o_a, z_a, dn_norm_w, w_o_dn, dyb_b, o_joint, z_b, w_o_dil)
    dqs, dks, dvs = [], [], []
    for gi in range(N_DIL):
        dq_g, dk_g, dv_g = _attn_bwd(q_b, k_b, v_b, d_o, lse, delta, gi)
        dqs.append(dq_g)
        dks.append(dk_g)
        dvs.append(dv_g)

    dqn, dkn, dvn, dbg = _delta_post_bwd(qn, kn, vn, bg, t2_d, st_d, vnew_d, d_o_a, dvnew_d, dkd_d, ddl_d)
    dc, dba_b, dsmall8 = _dn_prep_bwd(qkv_pre, ba, conv_w8, alog_row, dtb_row, dqn, dkn, dvn, dbg)
    dqkv_b, dconv8 = _conv_bwd(dc, qkv_pre, conv_w8)

    per_group = lambda w: [w[g * DIL_W:(g + 1) * DIL_W] for g in range(N_DIL)]
    dh_b = _mm_in(dqs + dks + dvs + [dga_b, dgb_b],
                  per_group(w_qb) + per_group(w_kb) + per_group(w_vb) + [w_ga, w_gb], "proj_bwd_b", w_is_out_by_in=True)
    dsegs = [dqkv_b, dza_b, dba_b] + dqs + dks + dvs + [dzb_b, dga_b, dgb_b]
    valid_rows = [d.shape[1] for d in dsegs]
    valid_rows[2] = 2 * DN_HEADS
    g_wt = _proj_wgrad_all(dsegs, valid_rows, hb)
    grad_x, dnw8 = _proj_bwd_rms_in([dqkv_b, dza_b, dba_b, dzb_b], [w_qkv, w_za, w_ba, w_zb], dh_b, x, dx2, norm_w)

    small = dict(norm_w=dnw8[0:1], final_norm_w=dwf8[0:1], dn_norm_w=ddnw8[0:1],
                 a_log=dsmall8[0:1, DN_HEADS:2 * DN_HEADS], dt_bias=dsmall8[1:2, DN_HEADS:2 * DN_HEADS])
    return loss8[0:1, 0:1], grad_x, g_wt, dconv8[0:4], g_w_o_dn, g_w_o_dil, g_w_out, small


def _proj_bwd_rms_in(ds, ws, dh_a, x, dx2, norm_w):
    n_seg = len(ds)

    def body(*refs):
        d_refs, w_refs = refs[:n_seg], refs[n_seg:2 * n_seg]
        da_ref, x_ref, dx2_ref, w_ref, dx_ref, dw_ref = refs[2 * n_seg:]
        dx_ref[...] = da_ref[...]
        for d_ref, wt_ref in zip(d_refs, w_refs):
            for c, wd in _col_chunks(d_ref.shape[1], 1024):
                dx_ref[...] += jnp.dot(d_ref[:, c:c + wd], wt_ref[c:c + wd, :], preferred_element_type=F32)
        xv = x_ref[...]
        r = lax.rsqrt(jnp.mean(xv * xv, axis=-1, keepdims=True) + NORM_EPS)
        dhv = dx_ref[...]
        dn = dhv * w_ref[...]
        dx_ref[...] = dx2_ref[...] + r * dn - xv * (r * r * r) * jnp.mean(dn * xv, axis=-1, keepdims=True)
        row = jnp.sum(dhv * xv * r, axis=0, keepdims=True)
        _acc_add(dw_ref, jnp.concatenate([row, jnp.zeros((SUBLANES - 1, row.shape[1]), F32)], axis=0))

    return _rows_call(body, "proj_bwd_b_rms_in", x.shape[0],
                      [(d, "tile") for d in ds] + [(w, "full") for w in ws]
                      + [(dh_a, "tile"), (x, "tile"), (dx2, "tile"), (norm_w, "full")],
                      [(x.shape, F32, "tile"), ((SUBLANES, x.shape[1]), F32, "acc")])


def _split_proj_rows(w_shards):
    n_shards, rows, k = w_shards.shape
    wt_full = w_shards.reshape(n_shards * rows, k)
    offs = [0]
    for n in PROJ_SIZES:
        offs.append(offs[-1] + n)
    seg = lambda a, b: wt_full[offs[a]:offs[b]]
    w_ba = jnp.concatenate([seg(4, 6), jnp.zeros((LANES - 2 * DN_HEADS, k), wt_full.dtype)], axis=0)
    return [seg(0, 3), seg(3, 4), w_ba, seg(6, 7), seg(7, 8), seg(8, 9), seg(9, 10), seg(10, 11), seg(11, 12)]


LOSS_ROW = 5


def _pack_small(norm_w, final_norm_w, dn_norm_w, a_log, dt_bias, loss=None):
    pad = lambda r: jnp.concatenate([r, jnp.zeros((1, D_MODEL - r.shape[1]), F32)], axis=1)
    rows = [pad(norm_w.reshape(1, -1)), pad(final_norm_w.reshape(1, -1)), pad(dn_norm_w.reshape(1, -1)),
            pad(a_log.reshape(1, -1)), pad(dt_bias.reshape(1, -1)),
            pad(jnp.zeros((1, 1), F32) if loss is None else loss.reshape(1, 1)),
            jnp.zeros((SUBLANES - LOSS_ROW - 1, D_MODEL), F32)]
    return jnp.concatenate(rows, axis=0)


def _unpack_small(p):
    return dict(norm_w=p[0:1], final_norm_w=p[1], dn_norm_w=p[2:3, :DN_DK], a_log=p[3:4, :DN_HEADS],
                dt_bias=p[4:5, :DN_HEADS])


def kernel(x, norm_w, w_in, conv_w, a_log, dt_bias, dn_norm_w, w_o_dn, w_o_dil, w_out, final_norm_w, loss_target, m_norm_w, m_w_in, m_conv_w, m_a_log, m_dt_bias, m_dn_norm_w, m_w_o_dn, m_w_o_dil, m_w_out, m_final_norm_w, v_norm_w, v_w_in, v_conv_w, v_a_log, v_dt_bias, v_dn_norm_w, v_w_o_dn, v_w_o_dil, v_w_out, v_final_norm_w):
    shard_w = w_in.shape[2]
    wt, m_wt, v_wt = (jnp.transpose(t[0]) for t in (w_in, m_w_in, v_w_in))
    gathered = _all_gather([wt.astype(MXU), w_o_dn[0].astype(MXU), w_o_dil[0].astype(MXU), w_out[0].astype(MXU),
                            conv_w[0]], "gather_weights")
    w_in_all, w_o_dn_all, w_o_dil_all, w_out_all, conv_all = gathered
    w_o_dn_full = w_o_dn_all.reshape(D_MODEL, D_MODEL)
    w_o_dil_full = jnp.transpose(w_o_dil_all, (1, 0, 2)).reshape(DIL_W, D_MODEL)
    w_out_full = w_out_all.reshape(D_MODEL, D_MODEL)
    conv_full = jnp.transpose(conv_all, (1, 0, 2)).reshape(conv_w.shape[1], QKV_W)

    loss11, grad_x, g_wt, g_conv, g_w_o_dn, g_w_o_dil, g_w_out, small = _local_step(
        x[0], loss_target[0], norm_w, _split_proj_rows(w_in_all), conv_full, a_log, dt_bias, dn_norm_w,
        w_o_dn_full, w_o_dil_full, w_out_full, final_norm_w)

    col_shards = lambda g, n: jnp.transpose(g.reshape(g.shape[0], N_DEV, n), (1, 0, 2))
    row_shards = lambda g: g.reshape(N_DEV, g.shape[0] // N_DEV, g.shape[1])
    g_wt_shards = jnp.stack([g_wt[j * shard_w:(j + 1) * shard_w] for j in range(N_DEV)], axis=0)
    sent = [g_wt_shards, row_shards(g_w_o_dn).astype(MXU),
            col_shards(g_w_o_dil, w_o_dil.shape[2]).astype(MXU), row_shards(g_w_out).astype(MXU),
            col_shards(g_conv, conv_w.shape[2])]
    sent = [g8.reshape((N_CHIPS, 2) + g8.shape[1:]) for g8 in sent]
    from_sibling = _pair_exchange(sent, "scatter_pair")
    summed = [_pair_add(g, o, f"pair_add_{i}") for i, (g, o) in enumerate(zip(sent, from_sibling))]
    p_w_in, p_w_o_dn, p_w_o_dil, p_w_out, p_conv = _chip_exchange(summed, "scatter_chips")
    p_small = _all_gather([_pack_small(small["norm_w"], small["final_norm_w"], small["dn_norm_w"], small["a_log"],
                                       small["dt_bias"], loss11)], "gather_small_grads")[0]

    res = {}
    res["w_in"] = [jnp.transpose(t) for t in _adamw(p_w_in, wt, m_wt, v_wt, "adamw_w_in")]
    res["conv_w"] = _adamw(p_conv, conv_w[0], m_conv_w[0], v_conv_w[0], "adamw_conv_w")
    res["w_o_dn"] = _adamw(p_w_o_dn, w_o_dn[0], m_w_o_dn[0], v_w_o_dn[0], "adamw_w_o_dn")
    res["w_o_dil"] = _adamw(p_w_o_dil, w_o_dil[0], m_w_o_dil[0], v_w_o_dil[0], "adamw_w_o_dil")
    res["w_out"] = _adamw(p_w_out, w_out[0], m_w_out[0], v_w_out[0], "adamw_w_out")
    small_res = _adamw(p_small, _pack_small(norm_w, final_norm_w, dn_norm_w, a_log, dt_bias),
                       _pack_small(m_norm_w, m_final_norm_w, m_dn_norm_w, m_a_log, m_dt_bias),
                       _pack_small(v_norm_w, v_final_norm_w, v_dn_norm_w, v_a_log, v_dt_bias), "adamw_small")
    loss = small_res[0][LOSS_ROW, 0]
    small_res = [_unpack_small(t) for t in small_res]

    names = ["norm_w", "w_in", "conv_w", "a_log", "dt_bias", "dn_norm_w", "w_o_dn", "w_o_dil", "w_out", "final_norm_w"]
    outs = [loss, grad_x[None]]
    for kind in range(4):
        for nm in names:
            outs.append(res[nm][kind][None] if nm in res else small_res[kind][nm])
    return tuple(outs)
```

```python
import math

import jax
import jax.numpy as jnp
from jax import lax
from jax.experimental import pallas as pl
from jax.experimental.pallas import tpu as pltpu

F32 = jnp.float32
MXU = jnp.bfloat16
MESH = pl.DeviceIdType.MESH

N_DEV = 8
D_MODEL = 1024
DN_HEADS = 8
DN_DK = 128
DN_CHUNK = 64
N_DIL = 3
DIL_HEADS = 4
DIL_DH = 128
DIL_W = DIL_HEADS * DIL_DH
DIL_GROUPS = ((128, 1), (512, 4), (2048, 16))
ATT_BLOCK = 128
NORM_EPS = 1e-6
QKV_W = 3 * D_MODEL
DILQ_W = N_DIL * DIL_W
PROJ_SIZES = (1024, 1024, 1024, 1024, 8, 8, DILQ_W, DILQ_W, DILQ_W, DIL_W, D_MODEL, D_MODEL)

ADAM_LR = 0.001
ADAM_B1 = 0.9
ADAM_B2 = 0.999
ADAM_EPS = 1e-08
ADAM_WD = 0.01
ADAM_STEP = 10

ROW_TILE = 256
LANES = 128
SUBLANES = 8
VMEM_LIMIT = 48 << 20


def _pcall(body, **kw):
    return pl.pallas_call(body, **kw)


def _params(*sem):
    return pltpu.CompilerParams(dimension_semantics=tuple(sem), vmem_limit_bytes=VMEM_LIMIT)


def _sigmoid(x):
    return 1.0 / (1.0 + jnp.exp(-x))


def _softplus(x):
    return jnp.maximum(x, 0.0) + jnp.log(1.0 + jnp.exp(-jnp.abs(x)))


def _dot(a, b):
    return jnp.dot(a.astype(MXU), b.astype(MXU), preferred_element_type=F32)


def _dot_nt(a, b):
    return lax.dot_general(a.astype(MXU), b.astype(MXU), (((1,), (1,)), ((), ())), preferred_element_type=F32)


def _dot_tn(a, b):
    return lax.dot_general(a.astype(MXU), b.astype(MXU), (((0,), (0,)), ((), ())), preferred_element_type=F32)


def _split3(x):
    hi = x.astype(jnp.bfloat16)
    r1 = x - hi.astype(F32)
    mid = r1.astype(jnp.bfloat16)
    lo = (r1 - mid.astype(F32)).astype(jnp.bfloat16)
    return hi, mid, lo


def _dot01(m01, x):
    m = m01.astype(jnp.bfloat16)
    hi, mid, lo = _split3(x)
    f = lambda p: jnp.dot(m, p, preferred_element_type=F32)
    return f(hi) + (f(mid) + f(lo))


def _rows_call(body, name, n_rows, ins, outs, scratch=(), tm=ROW_TILE):
    steps = n_rows // tm
    per8 = tm // SUBLANES
    last8 = n_rows // SUBLANES - 1
    in_specs = []
    for arr, kind in ins:
        cols = arr.shape[-1]
        if kind == "tile":
            in_specs.append(pl.BlockSpec((tm, cols), lambda i: (i, 0)))
        elif kind == "full":
            in_specs.append(pl.BlockSpec(arr.shape, lambda i, nd=arr.ndim: (0,) * nd))
        elif kind == "prev8":
            in_specs.append(pl.BlockSpec((SUBLANES, cols), lambda i: (jnp.maximum(i * per8 - 1, 0), 0)))
        elif kind == "next8":
            in_specs.append(pl.BlockSpec((SUBLANES, cols), lambda i: (jnp.minimum((i + 1) * per8, last8), 0)))
        else:
            raise ValueError(kind)
    out_specs, out_shape, has_acc = [], [], False
    for shape, dtype, kind in outs:
        out_shape.append(jax.ShapeDtypeStruct(shape, dtype))
        if kind == "tile":
            out_specs.append(pl.BlockSpec((tm, shape[-1]), lambda i: (i, 0)))
        else:
            has_acc = True
            out_specs.append(pl.BlockSpec(shape, lambda i: (0, 0)))
    return _pcall(
        body, name=name, grid=(steps,), in_specs=in_specs, out_specs=out_specs, out_shape=out_shape,
        scratch_shapes=list(scratch),
        compiler_params=_params("arbitrary" if has_acc else "parallel"),
    )(*[a for a, _ in ins])


def _acc_add(ref, value):
    @pl.when(pl.program_id(0) == 0)
    def _():
        ref[...] = jnp.zeros_like(ref)
    ref[...] += value


def _col_chunks(n, width=512):
    return [(c, min(width, n - c)) for c in range(0, n, width)]


NT_DIMS = (((1,), (1,)), ((), ()))
TN_DIMS = (((0,), (0,)), ((), ()))


def _mm_out(a, ws, name, w_is_out_by_in=False, out_dtype=F32, tm=ROW_TILE):
    m, k = a.shape
    ns = [w.shape[0] if w_is_out_by_in else w.shape[1] for w in ws]

    def body(a_ref, *refs):
        av = a_ref[...]
        for w_ref, o_ref, n in zip(refs[:len(ws)], refs[len(ws):], ns):
            for c, wd in _col_chunks(n):
                if w_is_out_by_in:
                    part = lax.dot_general(av, w_ref[c:c + wd, :], NT_DIMS, preferred_element_type=F32)
                else:
                    part = jnp.dot(av, w_ref[:, c:c + wd], preferred_element_type=F32)
                o_ref[:, c:c + wd] = part.astype(o_ref.dtype)

    return _pcall(
        body, name=name, grid=(m // tm,),
        in_specs=[pl.BlockSpec((tm, k), lambda i: (i, 0))] + [pl.BlockSpec(w.shape, lambda i: (0, 0)) for w in ws],
        out_specs=[pl.BlockSpec((tm, n), lambda i: (i, 0)) for n in ns],
        out_shape=[jax.ShapeDtypeStruct((m, n), out_dtype) for n in ns],
        compiler_params=_params("parallel"),
    )(a, *ws)


def _rms_proj_fwd(x, norm_w, wts, name, tm=ROW_TILE):
    m, k = x.shape
    ns = [w.shape[0] for w in wts]

    def body(x_ref, nw_ref, *refs):
        w_refs, h_ref, o_refs = refs[:len(wts)], refs[len(wts)], refs[len(wts) + 1:]
        xv = x_ref[...]
        r = lax.rsqrt(jnp.mean(xv * xv, axis=-1, keepdims=True) + NORM_EPS)
        hv = (xv * r * nw_ref[...]).astype(h_ref.dtype)
        h_ref[...] = hv
        for w_ref, o_ref, n in zip(w_refs, o_refs, ns):
            for c, wd in _col_chunks(n):
                o_ref[:, c:c + wd] = lax.dot_general(hv, w_ref[c:c + wd, :], NT_DIMS, preferred_element_type=F32)

    return _pcall(
        body, name=name, grid=(m // tm,),
        in_specs=[pl.BlockSpec((tm, k), lambda i: (i, 0)), pl.BlockSpec(norm_w.shape, lambda i: (0, 0))]
        + [pl.BlockSpec(w.shape, lambda i: (0, 0)) for w in wts],
        out_specs=[pl.BlockSpec((tm, k), lambda i: (i, 0))] + [pl.BlockSpec((tm, n), lambda i: (i, 0)) for n in ns],
        out_shape=[jax.ShapeDtypeStruct((m, k), MXU)] + [jax.ShapeDtypeStruct((m, n), F32) for n in ns],
        compiler_params=_params("parallel"),
    )(x, norm_w, *wts)


def _mm_in(ds, ws, name, w_is_out_by_in=False, tm=ROW_TILE):
    m = ds[0].shape[0]
    k = ws[0].shape[1] if w_is_out_by_in else ws[0].shape[0]
    ns = [d.shape[1] for d in ds]

    def body(*refs):
        d_refs, w_refs, o_ref = refs[:len(ds)], refs[len(ds):2 * len(ds)], refs[-1]
        first = True
        for d_ref, w_ref, n in zip(d_refs, w_refs, ns):
            for c, wd in _col_chunks(n, 1024):
                if w_is_out_by_in:
                    part = jnp.dot(d_ref[:, c:c + wd], w_ref[c:c + wd, :], preferred_element_type=F32)
                else:
                    part = lax.dot_general(d_ref[:, c:c + wd], w_ref[:, c:c + wd], NT_DIMS, preferred_element_type=F32)
                if first:
                    o_ref[...] = part
                    first = False
                else:
                    o_ref[...] += part

    return _pcall(
        body, name=name, grid=(m // tm,),
        in_specs=[pl.BlockSpec((tm, n), lambda i: (i, 0)) for n in ns] + [pl.BlockSpec(w.shape, lambda i: (0, 0)) for w in ws],
        out_specs=pl.BlockSpec((tm, k), lambda i: (i, 0)),
        out_shape=jax.ShapeDtypeStruct((m, k), F32),
        compiler_params=_params("parallel"),
    )(*ds, *ws)


def _mm_tn(a, d, name):
    m, k = a.shape
    n = d.shape[1]
    tk = 512 if k % 512 == 0 else k

    def body(a_ref, d_ref, o_ref):
        o_ref[...] = lax.dot_general(a_ref[...], d_ref[...], TN_DIMS, preferred_element_type=F32)

    return _pcall(
        body, name=name, grid=(k // tk,),
        in_specs=[pl.BlockSpec((m, tk), lambda p: (0, p)), pl.BlockSpec((m, n), lambda p: (0, 0))],
        out_specs=pl.BlockSpec((tk, n), lambda p: (p, 0)),
        out_shape=jax.ShapeDtypeStruct((k, n), F32),
        compiler_params=_params("parallel"),
    )(a, d)


WGRAD_TILE = 512


def _proj_wgrad_all(dsegs, valid_rows, hb):
    m, k = hb.shape
    n_seg = len(dsegs)
    tiles, row = [], 0
    for si, (d, valid) in enumerate(zip(dsegs, valid_rows)):
        for c in range(0, valid, WGRAD_TILE):
            width = min(WGRAD_TILE, d.shape[1] - c)
            tiles.append((si, c, width, row + c, min(width, valid - c)))
        row += valid
    total_rows = row

    def body(*refs):
        d_refs, hb_ref, o_ref = refs[:n_seg], refs[n_seg], refs[n_seg + 1]
        a_buf, hb_buf, o_buf, load_sems, store_sems, hb_sem = refs[n_seg + 2:]

        def load(t):
            si, c, width, _, _ = tiles[t]
            return pltpu.make_async_copy(d_refs[si].at[:, pl.ds(c, width)], a_buf.at[t % 2, :, pl.ds(0, width)],
                                         load_sems.at[t % 2])

        def stores(t):
            _, _, _, orow, valid = tiles[t]
            return [pltpu.make_async_copy(o_buf.at[t % 2, pl.ds(0, valid), :], o_ref.at[pl.ds(orow, valid), :],
                                          store_sems.at[t % 2])]

        hb_copy = pltpu.make_async_copy(hb_ref, hb_buf, hb_sem)
        hb_copy.start()
        load(0).start()
        hb_copy.wait()
        for t in range(len(tiles)):
            width = tiles[t][2]
            load(t).wait()
            if t + 1 < len(tiles):
                load(t + 1).start()
            if t >= 2:
                for cp in stores(t - 2):
                    cp.wait()
            o_buf[t % 2, 0:width, :] = lax.dot_general(a_buf[t % 2, :, 0:width], hb_buf[...], TN_DIMS,
                                                        preferred_element_type=F32).astype(o_buf.dtype)
            for cp in stores(t):
                cp.start()
        for t in range(max(len(tiles) - 2, 0), len(tiles)):
            for cp in stores(t):
                cp.wait()

    any_spec = pl.BlockSpec(memory_space=pl.ANY)
    return _pcall(
        body, name="proj_wgrad",
        in_specs=[any_spec] * (n_seg + 1), out_specs=any_spec,
        out_shape=jax.ShapeDtypeStruct((total_rows, k), hb.dtype),
        scratch_shapes=[pltpu.VMEM((2, m, WGRAD_TILE), hb.dtype), pltpu.VMEM((m, k), hb.dtype),
                        pltpu.VMEM((2, WGRAD_TILE, k), hb.dtype), pltpu.SemaphoreType.DMA((2,)),
                        pltpu.SemaphoreType.DMA((2,)), pltpu.SemaphoreType.DMA],
        compiler_params=pltpu.CompilerParams(vmem_limit_bytes=VMEM_LIMIT),
    )(*dsegs, hb)


def _conv_taps(ext_ref, cw_ref, cols, tm):
    c = None
    for j in range(4):
        term = cw_ref[3 - j:4 - j, cols] * ext_ref[SUBLANES - j:SUBLANES - j + tm, cols]
        c = term if c is None else c + term
    return c


def _fill_ext(ext_ref, u_ref, halo_ref, first):
    ext_ref[0:SUBLANES, :] = jnp.where(first, 0.0, halo_ref[...])
    ext_ref[SUBLANES:, :] = u_ref[...]


def _dn_prep_fwd_tile(u_ref, halo_ref, cw_ref, ba_ref, al_ref, dtb_ref, q_ref, k_ref, v_ref, bg_ref, ext_ref, first):
    tm = u_ref.shape[0]
    _fill_ext(ext_ref, u_ref, halo_ref, first)
    for h in range(3 * DN_HEADS):
        cols = slice(h * LANES, (h + 1) * LANES)
        c = _conv_taps(ext_ref, cw_ref, cols, tm)
        a = c * _sigmoid(c)
        oc = slice((h % DN_HEADS) * LANES, (h % DN_HEADS + 1) * LANES)
        if h < 2 * DN_HEADS:
            rinv = lax.rsqrt(jnp.sum(a * a, axis=-1, keepdims=True) + NORM_EPS)
            if h < DN_HEADS:
                q_ref[:, oc] = a * (rinv * DN_DK ** -0.5)
            else:
                k_ref[:, oc] = a * rinv
        else:
            v_ref[:, oc] = a
    bav = ba_ref[...]
    lane = lax.broadcasted_iota(jnp.int32, bav.shape, 1)
    beta = _sigmoid(bav)
    g = -jnp.exp(al_ref[...]) * _softplus(bav + dtb_ref[...])
    bg_ref[...] = jnp.where(lane < DN_HEADS, beta, jnp.where(lane < 2 * DN_HEADS, g, 0.0))


def _dn_prep_bwd_tile(u_ref, halo_ref, cw_ref, ba_ref, al_ref, dtb_ref, dq_ref, dk_ref, dv_ref, dbg_ref,
                      dc_ref, dba_ref, ext_ref, first):
    tm = u_ref.shape[0]
    _fill_ext(ext_ref, u_ref, halo_ref, first)
    for h in range(3 * DN_HEADS):
        cols = slice(h * LANES, (h + 1) * LANES)
        oc = slice((h % DN_HEADS) * LANES, (h % DN_HEADS + 1) * LANES)
        c = _conv_taps(ext_ref, cw_ref, cols, tm)
        sg = _sigmoid(c)
        a = c * sg
        if h < 2 * DN_HEADS:
            rinv = lax.rsqrt(jnp.sum(a * a, axis=-1, keepdims=True) + NORM_EPS)
            dy = dq_ref[:, oc] * DN_DK ** -0.5 if h < DN_HEADS else dk_ref[:, oc]
            da = rinv * dy - a * (rinv * rinv * rinv) * jnp.sum(dy * a, axis=-1, keepdims=True)
        else:
            da = dv_ref[:, oc]
        dc_ref[:, cols] = da * (sg * (1.0 + c * (1.0 - sg)))
    bav = ba_ref[...]
    dbgv = dbg_ref[...]
    lane = lax.broadcasted_iota(jnp.int32, bav.shape, 1)
    beta = _sigmoid(bav)
    ea = jnp.exp(al_ref[...])
    z = bav + dtb_ref[...]
    g = -ea * _softplus(z)
    is_b = lane < DN_HEADS
    is_g = jnp.logical_and(lane >= DN_HEADS, lane < 2 * DN_HEADS)
    d_aa = jnp.where(is_g, dbgv * (-ea) * _sigmoid(z), 0.0)
    dba = jnp.where(is_b, dbgv * beta * (1.0 - beta), d_aa)
    dba_ref[...] = dba.astype(dba_ref.dtype)
    r_alog = jnp.sum(jnp.where(is_g, dbgv * g, 0.0), axis=0, keepdims=True)
    r_dtb = jnp.sum(d_aa, axis=0, keepdims=True)
    return jnp.concatenate([r_alog, r_dtb, jnp.zeros((SUBLANES - 2, LANES), F32)], axis=0)


def _conv_bwd(dc, qkv_pre, conv_w8):
    s = dc.shape[0]
    tm = ROW_TILE
    steps = s // tm

    def body(dc_ref, dnext_ref, u_ref, halo_ref, cw_ref, du_ref, dcw_ref, extd_ref, ext_ref):
        i = pl.program_id(0)
        _fill_ext(ext_ref, u_ref, halo_ref, i == 0)
        extd_ref[0:tm, :] = dc_ref[...]
        extd_ref[tm:, :] = jnp.where(i == steps - 1, 0.0, dnext_ref[...])

        @pl.when(i == 0)
        def _():
            dcw_ref[...] = jnp.zeros_like(dcw_ref)

        for h in range(3 * DN_HEADS):
            cols = slice(h * LANES, (h + 1) * LANES)
            du = None
            for j in range(4):
                term = cw_ref[3 - j:4 - j, cols] * extd_ref[j:j + tm, cols]
                du = term if du is None else du + term
            du_ref[:, cols] = du.astype(du_ref.dtype)
            dcv = dc_ref[:, cols]
            for j in range(4):
                row = jnp.sum(dcv * ext_ref[SUBLANES - j:SUBLANES - j + tm, cols], axis=0, keepdims=True)
                dcw_ref[3 - j:4 - j, cols] += row

    return _rows_call(
        body, "conv_bwd", s,
        [(dc, "tile"), (dc, "next8"), (qkv_pre, "tile"), (qkv_pre, "prev8"), (conv_w8, "full")],
        [((s, QKV_W), MXU, "tile"), ((SUBLANES, QKV_W), F32, "acc")],
        scratch=[pltpu.VMEM((tm + SUBLANES, QKV_W), F32), pltpu.VMEM((tm + SUBLANES, QKV_W), F32)])


def _dn_out_fwd_tile(o_ref, z_ref, w_ref, wo_ref, on_ref, y_ref):
    for h in range(DN_HEADS):
        cols = _head_cols(h)
        ov = o_ref[:, cols]
        zv = z_ref[:, cols]
        ro = lax.rsqrt(jnp.mean(ov * ov, axis=-1, keepdims=True) + NORM_EPS)
        on_ref[:, cols] = (ov * ro * w_ref[...] * (zv * _sigmoid(zv))).astype(on_ref.dtype)
    y_ref[...] = jnp.dot(on_ref[...], wo_ref[...], preferred_element_type=F32)


def _dn_out_bwd_tile(dy_ref, o_ref, z_ref, w_ref, wo_ref, do_ref, dz_ref, d_ref):
    d_ref[...] = lax.dot_general(dy_ref[...], wo_ref[...], NT_DIMS, preferred_element_type=F32)
    acc = jnp.zeros((1, LANES), F32)
    for h in range(DN_HEADS):
        cols = _head_cols(h)
        dv, ov, zv = d_ref[:, cols], o_ref[:, cols], z_ref[:, cols]
        sg = _sigmoid(zv)
        sz = zv * sg
        ro = lax.rsqrt(jnp.mean(ov * ov, axis=-1, keepdims=True) + NORM_EPS)
        nv = ov * ro
        dn = dv * w_ref[...] * sz
        acc = acc + jnp.sum(dv * nv * sz, axis=0, keepdims=True)
        dz_ref[:, cols] = (dv * nv * w_ref[...] * (sg * (1.0 + zv * (1.0 - sg)))).astype(dz_ref.dtype)
        do_ref[:, cols] = ro * dn - ov * (ro * ro * ro) * jnp.mean(dn * ov, axis=-1, keepdims=True)
    return acc


def _attn_out_fwd_tile(o0, o1, o2, l0, l1, l2, z_ref, wo_ref, lse_ref, o_ref, g_ref, y_ref):
    a, b, c = l0[...], l1[...], l2[...]
    m = jnp.maximum(a, jnp.maximum(b, c))
    ea, eb, ec = jnp.exp(a - m), jnp.exp(b - m), jnp.exp(c - m)
    den = ea + eb + ec
    out = (ea * o0[...] + eb * o1[...] + ec * o2[...]) / den
    lse_ref[...] = m + jnp.log(den)
    o_ref[...] = out
    zv = z_ref[...]
    gated = (out * (zv * _sigmoid(zv))).astype(g_ref.dtype)
    g_ref[...] = gated
    y_ref[...] = jnp.dot(gated, wo_ref[...], preferred_element_type=F32)


def _attn_out_bwd_tile(dy_ref, o_ref, z_ref, wo_ref, do_ref, dz_ref, dl_ref):
    zv = z_ref[...]
    sg = _sigmoid(zv)
    dv = lax.dot_general(dy_ref[...], wo_ref[...], NT_DIMS, preferred_element_type=F32)
    ov = o_ref[...]
    do = dv * (zv * sg)
    do_ref[...] = do
    dz_ref[...] = (dv * ov * (sg * (1.0 + zv * (1.0 - sg)))).astype(dz_ref.dtype)
    for h in range(DIL_HEADS):
        cols = _head_cols(h)
        dl_ref[:, cols] = jnp.broadcast_to(jnp.sum(do[:, cols] * ov[:, cols], axis=-1, keepdims=True),
                                           (do.shape[0], LANES))


def _merge_out_final(ga, gb, ya, yb, x, target, w_out, wf_row):
    s, dm = x.shape

    def body(ga_ref, gb_ref, ya_ref, yb_ref, x_ref, t_ref, wo_ref, w_ref,
             loss_ref, dw_ref, m_ref, dxb_ref, dx_ref, dya_ref, dyb_ref, dga_ref, dgb_ref):
        sa, sb = _sigmoid(ga_ref[...]), _sigmoid(gb_ref[...])
        ya, yb = ya_ref[...], yb_ref[...]
        merged = (sa * ya + sb * yb).astype(MXU)
        m_ref[...] = merged
        x2 = x_ref[...] + jnp.dot(merged, wo_ref[...], preferred_element_type=F32)
        r = lax.rsqrt(jnp.mean(x2 * x2, axis=-1, keepdims=True) + NORM_EPS)
        w = w_ref[...]
        err = x2 * r * w - t_ref[...]
        tile_loss = 0.5 * jnp.sum(jnp.mean(err * err, axis=-1, keepdims=True), axis=0, keepdims=True)
        _acc_add(loss_ref, jnp.broadcast_to(tile_loss, (SUBLANES, LANES)))
        dy = err * (1.0 / dm)
        row = jnp.sum(dy * x2 * r, axis=0, keepdims=True)
        _acc_add(dw_ref, jnp.concatenate([row, jnp.zeros((SUBLANES - 1, dm), F32)], axis=0))
        dn = dy * w
        dx2 = r * dn - x2 * (r * r * r) * jnp.mean(dn * x2, axis=-1, keepdims=True)
        dx_ref[...] = dx2
        dxb = dx2.astype(MXU)
        dxb_ref[...] = dxb
        dmv = lax.dot_general(dxb, wo_ref[...], NT_DIMS, preferred_element_type=F32)
        dya_ref[...] = (dmv * sa).astype(dya_ref.dtype)
        dyb_ref[...] = (dmv * sb).astype(dyb_ref.dtype)
        dga_ref[...] = (dmv * ya * sa * (1.0 - sa)).astype(dga_ref.dtype)
        dgb_ref[...] = (dmv * yb * sb * (1.0 - sb)).astype(dgb_ref.dtype)

    return _rows_call(body, "merge_out_final", s,
                      [(ga, "tile"), (gb, "tile"), (ya, "tile"), (yb, "tile"), (x, "tile"), (target, "tile"),
                       (w_out, "full"), (wf_row, "full")],
                      [((SUBLANES, LANES), F32, "acc"), ((SUBLANES, dm), F32, "acc"), ((s, dm), MXU, "tile"),
                       ((s, dm), MXU, "tile"), ((s, dm), F32, "tile")] + [((s, dm), MXU, "tile")] * 4)


def _lane_pick(x, idx):
    lane = lax.broadcasted_iota(jnp.int32, x.shape, 1)
    return jnp.sum(jnp.where(lane == idx, x, 0.0), axis=-1, keepdims=True)


PAIR = 2 * DN_CHUNK
SCAN_CHUNKS = 4


def _bmm(a, b):
    return lax.dot_general(a.astype(MXU), b.astype(MXU), (((2,), (1,)), ((0,), (0,))), preferred_element_type=F32)


def _bmm_nt(a, b):
    return lax.dot_general(a.astype(MXU), b.astype(MXU), (((2,), (2,)), ((0,), (0,))), preferred_element_type=F32)


def _bmm_tn(a, b):
    return lax.dot_general(a.astype(MXU), b.astype(MXU), (((1,), (1,)), ((0,), (0,))), preferred_element_type=F32)


def _bmm3(a, b):
    ah = a.astype(jnp.bfloat16)
    al = (a - ah.astype(F32)).astype(jnp.bfloat16)
    bh = b.astype(jnp.bfloat16)
    bl = (b - bh.astype(F32)).astype(jnp.bfloat16)
    f = lambda p, q: lax.dot_general(p, q, (((2,), (1,)), ((0,), (0,))), preferred_element_type=F32)
    return f(ah, bh) + (f(ah, bl) + f(al, bh))


def _pair_masks():
    row = lax.broadcasted_iota(jnp.int32, (PAIR, PAIR), 0)
    col = lax.broadcasted_iota(jnp.int32, (PAIR, PAIR), 1)
    same = (row >= DN_CHUNK) == (col >= DN_CHUNK)
    return dict(causal=same & (row >= col), strict=same & (row > col), upper=same & (row <= col), eye=row == col,
                first=row < DN_CHUNK, row=row, lane=col)


def _pair_decay(bgv, masks):
    gc_all = _dot01(masks["causal"].astype(F32), bgv)
    out = []
    for h in range(DN_HEADS):
        beta = _lane_pick(bgv, h)
        gcb = jnp.broadcast_to(_lane_pick(gc_all, DN_HEADS + h), (PAIR, PAIR))
        gam = jnp.where(masks["causal"], jnp.exp(jnp.minimum(gcb - gcb.T, 0.0)), 0.0)
        gl = jnp.where(masks["first"], gcb[DN_CHUNK - 1:DN_CHUNK, :], gcb[PAIR - 1:PAIR, :])
        out.append((beta, gcb, gam, gl))
    return out


def _pair_inverse(a_strict, eye):
    eye_f = eye.astype(F32)[None]
    m = eye_f + a_strict
    x = eye_f - a_strict
    steps = int(math.log2(DN_CHUNK)) - 1
    for i in range(steps):
        mm = _bmm3 if i == steps - 1 else _bmm
        x = x + mm(x, eye_f - mm(m, x))
    return x


def _head_cols(h):
    return slice(h * LANES, (h + 1) * LANES)


def _delta_prep(qkv_pre, ba, conv_w8, alog_row, dtb_row):
    s = qkv_pre.shape[0]
    c = DN_CHUNK
    n_chunks = s // c

    def body(pre_ref, halo_ref, cw_ref, ba_ref, al_ref, dtb_ref,
             u_ref, w_ref, qd_ref, kd_ref, aqk_ref, dl_ref, t2_ref, q_ref, k_ref, v_ref, bg_ref, ext_ref):
        _dn_prep_fwd_tile(pre_ref, halo_ref, cw_ref, ba_ref, al_ref, dtb_ref, q_ref, k_ref, v_ref, bg_ref, ext_ref,
                          pl.program_id(0) == 0)
        masks = _pair_masks()
        dec = _pair_decay(bg_ref[...], masks)
        kbs, ks, gams, vbs, kbes, qs, qds, kds, dls = ([] for _ in range(9))
        for h in range(DN_HEADS):
            beta, gcb, gam, gl = dec[h]
            qh, kh, vh = q_ref[:, _head_cols(h)], k_ref[:, _head_cols(h)], v_ref[:, _head_cols(h)]
            eg = jnp.exp(gcb)
            kb = kh * beta
            kbs.append(kb); ks.append(kh); gams.append(gam); vbs.append(vh * beta); kbes.append(kb * eg)
            qs.append(qh); qds.append(qh * eg); kds.append(kh * jnp.exp(gl - gcb)); dls.append(jnp.exp(gl))
        st = lambda xs: jnp.stack(xs, axis=0)
        kmat, gam = st(ks), st(gams)
        a = jnp.where(masks["strict"][None], _bmm_nt(st(kbs), kmat) * gam, 0.0)
        t = _pair_inverse(a, masks["eye"])
        u = _bmm(t, st(vbs))
        w = _bmm(t, st(kbes))
        aqk = _bmm_nt(st(qs), kmat) * gam
        t2_ref[0] = t.astype(t2_ref.dtype)
        for half in range(2):
            rows = slice(half * c, (half + 1) * c)
            u_ref[half] = u[:, rows, :]
            w_ref[half] = w[:, rows, :].astype(w_ref.dtype)
            qd_ref[half] = st(qds)[:, rows, :].astype(qd_ref.dtype)
            kd_ref[half] = st(kds)[:, rows, :].astype(kd_ref.dtype)
            aqk_ref[half] = aqk[:, rows, rows].astype(aqk_ref.dtype)
            dl_ref[half] = st(dls)[:, half * c:half * c + SUBLANES, :]

    row_spec = lambda w_: pl.BlockSpec((PAIR, w_), lambda i: (i, 0))
    hm = lambda a_, b_: pl.BlockSpec((2, DN_HEADS, a_, b_), lambda i: (i, 0, 0, 0))
    hm_shape = lambda a_, b_, dt: jax.ShapeDtypeStruct((n_chunks, DN_HEADS, a_, b_), dt)
    whole = lambda t: pl.BlockSpec(t.shape, lambda i: (0, 0))
    halo = pl.BlockSpec((SUBLANES, QKV_W), lambda i: (jnp.maximum(i * (PAIR // SUBLANES) - 1, 0), 0))
    return _pcall(
        body, name="delta_prep", grid=(n_chunks // 2,),
        in_specs=[row_spec(QKV_W), halo, whole(conv_w8), row_spec(LANES), whole(alog_row), whole(dtb_row)],
        out_specs=[hm(c, LANES)] * 4 + [hm(c, c), hm(SUBLANES, LANES),
                   pl.BlockSpec((1, DN_HEADS, PAIR, PAIR), lambda i: (i, 0, 0, 0))]
        + [row_spec(D_MODEL)] * 3 + [row_spec(LANES)],
        out_shape=[hm_shape(c, LANES, F32), hm_shape(c, LANES, MXU), hm_shape(c, LANES, MXU), hm_shape(c, LANES, MXU),
                   hm_shape(c, c, MXU), hm_shape(SUBLANES, LANES, F32),
                   jax.ShapeDtypeStruct((n_chunks // 2, DN_HEADS, PAIR, PAIR), MXU)]
        + [jax.ShapeDtypeStruct((s, D_MODEL), F32)] * 3 + [jax.ShapeDtypeStruct((s, LANES), F32)],
        scratch_shapes=[pltpu.VMEM((PAIR + SUBLANES, QKV_W), F32)],
        compiler_params=_params("parallel"),
    )(qkv_pre, qkv_pre, conv_w8, ba, alog_row, dtb_row)


def _delta_scan_fwd(u, w, qd, kd, aqk, dl, z, dnw_row, w_o_dn, attn_parts, attn_lses, zb, w_o_dil):
    n_chunks = u.shape[0]
    c = DN_CHUNK
    g_n = SCAN_CHUNKS

    def body(u_ref, w_ref, qd_ref, kd_ref, aqk_ref, dl_ref, z_ref, nw_ref, wo_ref,
             p0, p1, p2, l0, l1, l2, zb_ref, wod_ref,
             o_ref, vnew_ref, st_ref, on_ref, y_ref, lse_ref, oj_ref, gb_ref, yb_ref, state):
        @pl.when(pl.program_id(0) == 0)
        def _():
            state[...] = jnp.zeros_like(state)

        _attn_out_fwd_tile(p0, p1, p2, l0, l1, l2, zb_ref, wod_ref, lse_ref, oj_ref, gb_ref, yb_ref)
        for g in range(g_n):
            sv = state[...]
            sb = sv.astype(MXU)
            vnew = u_ref[g] - _bmm(w_ref[g], sb)
            o = _bmm(qd_ref[g], sb) + _bmm(aqk_ref[g], vnew)
            state[...] = sv * dl_ref[g][:, 0:1, :] + _bmm_tn(kd_ref[g], vnew)
            vnew_ref[g] = vnew.astype(vnew_ref.dtype)
            st_ref[g] = sb
            for h in range(DN_HEADS):
                o_ref[g * c:(g + 1) * c, _head_cols(h)] = o[h]
        _dn_out_fwd_tile(o_ref, z_ref, nw_ref, wo_ref, on_ref, y_ref)

    hm = lambda a_, b_: pl.BlockSpec((g_n, DN_HEADS, a_, b_), lambda i: (i, 0, 0, 0))
    rows = lambda width: pl.BlockSpec((g_n * c, width), lambda i: (i, 0))
    whole = lambda t: pl.BlockSpec(t.shape, lambda i: (0, 0))
    full = lambda width, dt: jax.ShapeDtypeStruct((n_chunks * c, width), dt)
    return _pcall(
        body, name="delta_scan_fwd", grid=(n_chunks // g_n,),
        in_specs=[hm(c, LANES)] * 4 + [hm(c, c), hm(SUBLANES, LANES), rows(D_MODEL), whole(dnw_row), whole(w_o_dn)]
        + [rows(DIL_W)] * 7 + [whole(w_o_dil)],
        out_specs=[rows(D_MODEL), hm(c, LANES), hm(DN_DK, DN_DK), rows(D_MODEL), rows(D_MODEL),
                   rows(DIL_W), rows(DIL_W), rows(DIL_W), rows(D_MODEL)],
        out_shape=[full(D_MODEL, F32),
                   jax.ShapeDtypeStruct((n_chunks, DN_HEADS, c, LANES), MXU),
                   jax.ShapeDtypeStruct((n_chunks, DN_HEADS, DN_DK, DN_DK), MXU),
                   full(D_MODEL, MXU), full(D_MODEL, F32),
                   full(DIL_W, F32), full(DIL_W, F32), full(DIL_W, MXU), full(D_MODEL, F32)],
        scratch_shapes=[pltpu.VMEM((DN_HEADS, DN_DK, DN_DK), F32)],
        compiler_params=_params("arbitrary"),
    )(u, w, qd, kd, aqk, dl, z, dnw_row, w_o_dn, *attn_parts, *attn_lses, zb, w_o_dil)


def _delta_scan_bwd(w, qd, kd, aqk, dl, vnew, st, dy, o, z, dnw_row, w_o_dn, dyb, o_joint, zb, w_o_dil):
    n_chunks = w.shape[0]
    c = DN_CHUNK
    g_n = SCAN_CHUNKS
    steps = n_chunks // g_n

    def body(w_ref, qd_ref, kd_ref, aqk_ref, dl_ref, vnew_ref, st_ref, dy_ref, o_ref, z_ref, nw_ref, wo_ref,
             dyb_ref, oj_ref, zb_ref, wod_ref,
             dvnew_ref, dkd_ref, ddl_ref, do_ref, dz_ref, dnw_ref, dob_ref, dzb_ref, delta_ref, dstate, d_scratch):
        @pl.when(pl.program_id(0) == 0)
        def _():
            dstate[...] = jnp.zeros_like(dstate)

        _attn_out_bwd_tile(dyb_ref, oj_ref, zb_ref, wod_ref, dob_ref, dzb_ref, delta_ref)
        acc = _dn_out_bwd_tile(dy_ref, o_ref, z_ref, nw_ref, wo_ref, do_ref, dz_ref, d_scratch)
        _acc_add(dnw_ref, jnp.concatenate([acc, jnp.zeros((SUBLANES - 1, LANES), F32)], axis=0))
        for g in reversed(range(g_n)):
            ds = dstate[...]
            dsb = ds.astype(MXU)
            doh = jnp.stack([do_ref[g * c:(g + 1) * c, _head_cols(h)] for h in range(DN_HEADS)], axis=0)
            dvnew = _bmm_tn(aqk_ref[g], doh) + _bmm(kd_ref[g], dsb)
            dkd_ref[g] = _bmm_nt(vnew_ref[g], dsb)
            ddl = jnp.sum(jnp.sum(st_ref[g].astype(F32) * ds, axis=2, keepdims=True), axis=1, keepdims=True)
            ddl_ref[g] = jnp.broadcast_to(ddl, (DN_HEADS, SUBLANES, LANES))
            dstate[...] = ds * dl_ref[g][:, 0:1, :] + _bmm_tn(qd_ref[g], doh) - _bmm_tn(w_ref[g], dvnew)
            dvnew_ref[g] = dvnew.astype(dvnew_ref.dtype)

    rev = lambda i: steps - 1 - i
    hm = lambda a_, b_: pl.BlockSpec((g_n, DN_HEADS, a_, b_), lambda i: (rev(i), 0, 0, 0))
    rows = lambda width: pl.BlockSpec((g_n * c, width), lambda i: (rev(i), 0))
    whole = lambda t: pl.BlockSpec(t.shape, lambda i: (0, 0))
    full = lambda width, dt: jax.ShapeDtypeStruct((n_chunks * c, width), dt)
    return _pcall(
        body, name="delta_scan_bwd", grid=(steps,),
        in_specs=[hm(c, LANES)] * 3 + [hm(c, c), hm(SUBLANES, LANES), hm(c, LANES), hm(DN_DK, DN_DK),
                  rows(D_MODEL), rows(D_MODEL), rows(D_MODEL), whole(dnw_row), whole(w_o_dn),
                  rows(D_MODEL), rows(DIL_W), rows(DIL_W), whole(w_o_dil)],
        out_specs=[hm(c, LANES), hm(c, LANES), hm(SUBLANES, LANES), rows(D_MODEL), rows(D_MODEL),
                   pl.BlockSpec((SUBLANES, LANES), lambda i: (0, 0)), rows(DIL_W), rows(DIL_W), rows(DIL_W)],
        out_shape=[jax.ShapeDtypeStruct((n_chunks, DN_HEADS, c, LANES), MXU),
                   jax.ShapeDtypeStruct((n_chunks, DN_HEADS, c, LANES), F32),
                   jax.ShapeDtypeStruct((n_chunks, DN_HEADS, SUBLANES, LANES), F32),
                   full(D_MODEL, F32), full(D_MODEL, MXU), jax.ShapeDtypeStruct((SUBLANES, LANES), F32),
                   full(DIL_W, F32), full(DIL_W, MXU), full(DIL_W, F32)],
        scratch_shapes=[pltpu.VMEM((DN_HEADS, DN_DK, DN_DK), F32), pltpu.VMEM((g_n * c, D_MODEL), F32)],
        compiler_params=_params("arbitrary"),
    )(w, qd, kd, aqk, dl, vnew, st, dy, o, z, dnw_row, w_o_dn, dyb, o_joint, zb, w_o_dil)


def _delta_post_bwd(q, k, v, bg, t2, st, vnew, do, dvnew, dkd, ddl, qkv_pre, ba, conv_w8, alog_row, dtb_row):
    s = q.shape[0]
    c = DN_CHUNK

    def body(q_ref, k_ref, v_ref, bg_ref, t2_ref, st_ref, vnew_ref, do_ref, dvnew_ref, dkd_ref, ddl_ref,
             pre_ref, halo_ref, cw_ref, ba_ref, al_ref, dtb_ref,
             dc_ref, dba_ref, dsmall_ref, dq_ref, dk_ref, dv_ref, dbg_ref, ext_ref):
        masks = _pair_masks()
        first = masks["first"][None]
        dec = _pair_decay(bg_ref[...], masks)
        st_ = lambda xs: jnp.stack(xs, axis=0)
        heads = range(DN_HEADS)
        qm_, km_, vm_, dom = (st_([r[:, _head_cols(h)] for h in heads]) for r in (q_ref, k_ref, v_ref, do_ref))
        beta = st_([dec[h][0] for h in heads])
        gcb = st_([dec[h][1] for h in heads])
        gam = st_([dec[h][2] for h in heads])
        gl = st_([dec[h][3] for h in heads])
        pair = lambda ref: jnp.concatenate([ref[0], ref[1]], axis=1)
        vnew2, dvnew2, dkd2 = pair(vnew_ref), pair(dvnew_ref), pair(dkd_ref)
        halves = lambda x: (x[:, :c, :], x[:, c:, :])
        by_state = lambda x: jnp.concatenate([_bmm_nt(xh, st_ref[i]) for i, xh in enumerate(halves(x))], axis=1)
        dqd = by_state(dom)
        dw = -by_state(dvnew2)
        ddl2 = jnp.where(first, ddl_ref[0][:, 0:1, :], ddl_ref[1][:, 0:1, :])

        eg = jnp.exp(gcb)
        egl = jnp.exp(gl - gcb)
        dl = jnp.exp(gl)
        kb = km_ * beta
        kk = _bmm_nt(kb, km_)
        a = jnp.where(masks["strict"][None], kk * gam, 0.0)
        t = t2_ref[0]
        vb = vm_ * beta
        kbe = kb * eg
        u = _bmm(t, vb)
        w = _bmm(t, kbe)
        aqk = _bmm_nt(qm_, km_) * gam
        qd = qm_ * eg
        kd = km_ * egl

        daqk = jnp.where(masks["causal"][None], _bmm_nt(dom, vnew2), 0.0)
        dvb = _bmm_tn(t, dvnew2)
        dkbe = _bmm_tn(t, dw)
        da = jnp.where(masks["strict"][None], -(_bmm_nt(dvb, u) + _bmm_nt(dkbe, w)), 0.0)
        pm = da * gam
        qmm = daqk * gam
        dkb = _bmm(pm, km_) + dkbe * eg
        dkh = _bmm_tn(pm, kb) + _bmm_tn(qmm, qm_) + dkd2 * egl + dkb * beta
        dqh = _bmm(qmm, km_) + dqd * eg
        xm = da * a + daqk * aqk
        col_rows = jnp.concatenate([jnp.zeros((DN_HEADS, PAIR), F32), jnp.sum(xm, axis=1),
                                    jnp.zeros((PAIR - 2 * DN_HEADS, PAIR), F32)], axis=0)
        tmp = jnp.sum(dkd2 * kd, axis=-1, keepdims=True)
        dgc = (jnp.sum(xm, axis=-1, keepdims=True) + jnp.sum(dkbe * kbe, axis=-1, keepdims=True)
               + jnp.sum(dqd * qd, axis=-1, keepdims=True) - tmp)
        sum0 = jnp.sum(jnp.where(first, tmp, 0.0), axis=1, keepdims=True)
        sum1 = jnp.sum(jnp.where(first, 0.0, tmp), axis=1, keepdims=True)
        dgl = jnp.where(first, sum0, sum1) + ddl2 * dl
        last = (masks["row"] == c - 1) | (masks["row"] == PAIR - 1)
        dgc = dgc + jnp.where(last[None], dgl, 0.0)
        dbeta = jnp.sum(dvb * vm_, axis=-1, keepdims=True) + jnp.sum(dkb * km_, axis=-1, keepdims=True)
        dvh = dvb * beta

        lane = masks["lane"]
        dgc_lanes = jnp.zeros((PAIR, LANES), F32)
        dbg = jnp.zeros((PAIR, LANES), F32)
        for h in heads:
            dq_ref[:, _head_cols(h)] = dqh[h]
            dk_ref[:, _head_cols(h)] = dkh[h]
            dv_ref[:, _head_cols(h)] = dvh[h]
            dgc_lanes = dgc_lanes + jnp.where(lane == DN_HEADS + h, dgc[h], 0.0)
            dbg = dbg + jnp.where(lane == h, dbeta[h], 0.0)
        dbg_ref[...] = dbg + _dot01(masks["upper"].astype(F32), dgc_lanes - col_rows.T)
        small = _dn_prep_bwd_tile(pre_ref, halo_ref, cw_ref, ba_ref, al_ref, dtb_ref, dq_ref, dk_ref, dv_ref, dbg_ref,
                                  dc_ref, dba_ref, ext_ref, pl.program_id(0) == 0)
        _acc_add(dsmall_ref, small)

    n_pairs = s // PAIR
    row_spec = lambda w_: pl.BlockSpec((PAIR, w_), lambda i: (i, 0))
    hm = lambda a_, b_: pl.BlockSpec((2, DN_HEADS, a_, b_), lambda i: (i, 0, 0, 0))
    whole = lambda t: pl.BlockSpec(t.shape, lambda i: (0, 0))
    halo = pl.BlockSpec((SUBLANES, QKV_W), lambda i: (jnp.maximum(i * (PAIR // SUBLANES) - 1, 0), 0))
    return _pcall(
        body, name="delta_post_bwd", grid=(n_pairs,),
        in_specs=[row_spec(D_MODEL)] * 3 + [row_spec(LANES), pl.BlockSpec((1, DN_HEADS, PAIR, PAIR), lambda i: (i, 0, 0, 0)),
                  hm(DN_DK, DN_DK), hm(c, LANES), row_spec(D_MODEL), hm(c, LANES), hm(c, LANES), hm(SUBLANES, LANES),
                  row_spec(QKV_W), halo, whole(conv_w8), row_spec(LANES), whole(alog_row), whole(dtb_row)],
        out_specs=[row_spec(QKV_W), row_spec(LANES), pl.BlockSpec((SUBLANES, LANES), lambda i: (0, 0))],
        out_shape=[jax.ShapeDtypeStruct((s, QKV_W), F32), jax.ShapeDtypeStruct((s, LANES), MXU),
                   jax.ShapeDtypeStruct((SUBLANES, LANES), F32)],
        scratch_shapes=[pltpu.VMEM((PAIR, D_MODEL), F32)] * 3
        + [pltpu.VMEM((PAIR, LANES), F32), pltpu.VMEM((PAIR + SUBLANES, QKV_W), F32)],
        compiler_params=_params("arbitrary"),
    )(q, k, v, bg, t2, st, vnew, do, dvnew, dkd, ddl, qkv_pre, qkv_pre, conv_w8, ba, alog_row, dtb_row)


def _alibi_slope(group, head):
    n = N_DIL * DIL_HEADS
    return float(2.0 ** (-8.0 * (group * DIL_HEADS + head + 1) / n))


def _attn_plan(s, group):
    window, dil = DIL_GROUPS[group]
    assert window // dil == ATT_BLOCK
    assert (s // dil) % ATT_BLOCK == 0, "sub-sequence length must be a whole number of attention blocks"
    return dil, s // dil // ATT_BLOCK, (DIL_HEADS if dil == 1 else 1)


def _attn_specs(group, dil, nb, hp):
    rows = ATT_BLOCK * dil

    def spec(col0, shift):
        if shift < 0:
            f = lambda hb, n: (jnp.maximum(n - 1, 0), col0 + hb)
        elif shift > 0:
            f = lambda hb, n: (jnp.minimum(n + 1, nb - 1), col0 + hb)
        else:
            f = lambda hb, n: (jnp.minimum(n, nb - 1), col0 + hb)
        return pl.BlockSpec((rows, hp * LANES), f)

    return (lambda shift: spec(group * (DIL_HEADS // hp), shift)), (lambda shift: spec(0, shift))


def _sub_rows(ref, r, dil, cols):
    return ref[:, cols] if dil == 1 else ref[pl.ds(r, ATT_BLOCK, stride=dil), cols]


def _set_sub_rows(ref, r, dil, cols, value):
    if dil == 1:
        ref[:, cols] = value
    else:
        ref[pl.ds(r, ATT_BLOCK, stride=dil), cols] = value


def _step_slope(group, hp, hh):
    if hp == DIL_HEADS:
        return _alibi_slope(group, hh)
    hb = pl.program_id(0)
    slope = _alibi_slope(group, DIL_HEADS - 1)
    for h in reversed(range(DIL_HEADS - 1)):
        slope = jnp.where(hb == h, _alibi_slope(group, h), slope)
    return slope


def _attn_items(hp, dil):
    return [(hh, r) for hh in range(hp) for r in range(dil)]


def _attn_stack(ref, items, dil, dtype=MXU):
    return jnp.stack([_sub_rows(ref, r, dil, _head_cols(hh)).astype(dtype) for hh, r in items], axis=0)


def _attn_slopes(group, hp, items):
    if hp == 1:
        return _step_slope(group, hp, 0)
    return jnp.stack([jnp.full((1, 1), _alibi_slope(group, hh), F32) for hh, _ in items], axis=0)


def _window_bias(dil, n):
    a = lax.broadcasted_iota(jnp.int32, (ATT_BLOCK, 2 * ATT_BLOCK), 0)
    b = lax.broadcasted_iota(jnp.int32, (ATT_BLOCK, 2 * ATT_BLOCK), 1)
    dist = ATT_BLOCK + a - b
    valid = (dist >= 0) & (dist <= ATT_BLOCK) & ((b >= ATT_BLOCK) | (n > 0))
    return (dist * dil).astype(F32), valid


def _attn_fwd(qb, kb, vb, group):
    s = qb.shape[0]
    dil, nb, hp = _attn_plan(s, group)
    qkv, per_head = _attn_specs(group, dil, nb, hp)

    def body(q_ref, kp_ref, kc_ref, vp_ref, vc_ref, o_ref, lse_ref):
        n = pl.program_id(1)
        distd, valid = _window_bias(dil, n)
        items = _attn_items(hp, dil)
        sub = lambda ref: _attn_stack(ref, items, dil)
        kk = jnp.concatenate([sub(kp_ref), sub(kc_ref)], axis=1)
        vv = jnp.concatenate([sub(vp_ref), sub(vc_ref)], axis=1)
        sc = _bmm_nt(sub(q_ref), kk) * DIL_DH ** -0.5 - _attn_slopes(group, hp, items) * distd
        sc = jnp.where(valid, sc, -1e30)
        mx = jnp.max(sc, axis=-1, keepdims=True)
        p = jnp.where(valid, jnp.exp(sc - mx), 0.0)
        den = jnp.sum(p, axis=-1, keepdims=True)
        out = _bmm(p, vv) / den
        lse = mx + jnp.log(den)
        for b, (hh, r) in enumerate(items):
            _set_sub_rows(o_ref, r, dil, _head_cols(hh), out[b])
            _set_sub_rows(lse_ref, r, dil, _head_cols(hh), jnp.broadcast_to(lse[b], (ATT_BLOCK, LANES)))

    return _pcall(
        body, name=f"attn_fwd_g{group}", grid=(DIL_HEADS // hp, nb),
        in_specs=[qkv(0), qkv(-1), qkv(0), qkv(-1), qkv(0)], out_specs=[per_head(0)] * 2,
        out_shape=[jax.ShapeDtypeStruct((s, DIL_W), F32)] * 2,
        compiler_params=_params("parallel", "parallel"),
    )(qb, kb, kb, vb, vb)


def _attn_bwd(qb, kb, vb, d_o, lse, delta, group):
    s = qb.shape[0]
    dil, nb, hp = _attn_plan(s, group)
    qkv, per_head = _attn_specs(group, dil, nb, hp)
    scale = DIL_DH ** -0.5

    def body(q_ref, kp_ref, kc_ref, vp_ref, vc_ref, do_ref, l_ref, dl_ref, dq_ref, dk_ref, dv_ref,
             dq_acc, dk_done, dv_done, dk_carry, dv_carry):
        n = pl.program_id(1)
        items = _attn_items(hp, dil)
        slopes = _attn_slopes(group, hp, items)

        @pl.when(n == 0)
        def _():
            dk_carry[...] = jnp.zeros_like(dk_carry)
            dv_carry[...] = jnp.zeros_like(dv_carry)

        @pl.when(n < nb)
        def _():
            distd, valid = _window_bias(dil, n)
            sub = lambda ref, dtype=MXU: _attn_stack(ref, items, dil, dtype)
            qc, do = sub(q_ref), sub(do_ref)
            kk = jnp.concatenate([sub(kp_ref), sub(kc_ref)], axis=1)
            vv = jnp.concatenate([sub(vp_ref), sub(vc_ref)], axis=1)
            sc = _bmm_nt(qc, kk) * scale - slopes * distd
            p = jnp.where(valid, jnp.exp(jnp.minimum(sc - jnp.concatenate([sub(l_ref, F32)] * 2, axis=2), 0.0)), 0.0)
            dsc = p * (_bmm_nt(do, vv) - jnp.concatenate([sub(dl_ref, F32)] * 2, axis=2))
            dq = _bmm(dsc, kk) * scale
            dkk = _bmm_tn(dsc, qc) * scale
            dvv = _bmm_tn(p, do)
            for b, (hh, r) in enumerate(items):
                cols = _head_cols(hh)
                _set_sub_rows(dq_acc, r, dil, cols, dq[b])
                _set_sub_rows(dk_done, r, dil, cols, _sub_rows(dk_carry, r, dil, cols) + dkk[b, :ATT_BLOCK])
                _set_sub_rows(dv_done, r, dil, cols, _sub_rows(dv_carry, r, dil, cols) + dvv[b, :ATT_BLOCK])
                _set_sub_rows(dk_carry, r, dil, cols, dkk[b, ATT_BLOCK:])
                _set_sub_rows(dv_carry, r, dil, cols, dvv[b, ATT_BLOCK:])
            dq_ref[...] = dq_acc[...].astype(dq_ref.dtype)
            dk_ref[...] = dk_done[...].astype(dk_ref.dtype)
            dv_ref[...] = dv_done[...].astype(dv_ref.dtype)

        @pl.when(n == nb)
        def _():
            dk_ref[...] = dk_carry[...].astype(dk_ref.dtype)
            dv_ref[...] = dv_carry[...].astype(dv_ref.dtype)

    return _pcall(
        body, name=f"attn_bwd_g{group}", grid=(DIL_HEADS // hp, nb + 1),
        in_specs=[qkv(0), qkv(-1), qkv(0), qkv(-1), qkv(0)] + [per_head(0)] * 3,
        out_specs=[per_head(0), per_head(-1), per_head(-1)],
        out_shape=[jax.ShapeDtypeStruct((s, DIL_W), MXU)] * 3,
        scratch_shapes=[pltpu.VMEM((ATT_BLOCK * dil, hp * LANES), F32)] * 5,
        compiler_params=_params("parallel", "arbitrary"),
    )(qb, kb, kb, vb, vb, d_o, lse, delta)


def _my_place():
    mx, my, mc = lax.axis_index("x"), lax.axis_index("y"), lax.axis_index("c")
    return mx, my, mc, 4 * mx + 2 * my + mc


N_CHIPS = 4


def _shard_row_tile(r):
    if r <= 512:
        return r
    return 128 if r % 128 == 0 else 480


def _other_chips(mx, my):
    return [(1 - mx, my), (mx, 1 - my), (1 - mx, 1 - my)]


def _all_gather(xs, name):
    n = len(xs)
    halved = [x.shape[1] % (2 * LANES) == 0 and x.size * x.dtype.itemsize >= (1 << 20) for x in xs]
    n_sems = 8

    def body(*refs):
        x_refs, o_refs = refs[:n], refs[n:2 * n]
        send_sems, recv_sems, local_sems = refs[2 * n:]
        mx, my, mc, me = _my_place()
        sibling, sibling_id = (mx, my, 1 - mc), 4 * mx + 2 * my + (1 - mc)
        x_nbr, y_nbr, diag = _other_chips(mx, my)
        slot_of = lambda chip, c: 4 * chip[0] + 2 * chip[1] + c

        def part(ref, a, half):
            if not halved[a]:
                return ref
            width = xs[a].shape[1] // 2
            return ref.at[:, pl.ds(half * width, width)]

        def copy(a, k, dst, to, src=None):
            return pltpu.make_async_remote_copy(
                src_ref=dst if src is None else src, dst_ref=dst, send_sem=send_sems.at[a, k],
                recv_sem=recv_sems.at[a, k], device_id=to, device_id_type=MESH)

        local = [pltpu.make_async_copy(x_refs[a], o_refs[a].at[me], local_sems.at[a]) for a in range(n)]
        for cp in local:
            cp.start()
        sends = []
        for a in range(n):
            mine = o_refs[a].at[me]
            sends += [copy(a, 0, mine, sibling, src=x_refs[a]), copy(a, 1, mine, (*x_nbr, mc), src=x_refs[a]),
                      copy(a, 2, mine, (*y_nbr, mc), src=x_refs[a])]
        for cp in sends:
            cp.start()
        for a in range(n):
            blk = o_refs[a].at[slot_of(x_nbr, mc)]
            copy(a, 1, blk, (*x_nbr, mc)).wait_recv()
            sends += [copy(a, 3, blk, sibling), copy(a, 5, part(blk, a, 0), (*y_nbr, mc))]
            sends[-2].start()
            sends[-1].start()
        for a in range(n):
            blk = o_refs[a].at[slot_of(y_nbr, mc)]
            copy(a, 2, blk, (*y_nbr, mc)).wait_recv()
            sends.append(copy(a, 4, blk, sibling))
            sends[-1].start()
            if halved[a]:
                sends.append(copy(a, 6, part(blk, a, 1), (*x_nbr, mc)))
                sends[-1].start()
        for a in range(n):
            blk = o_refs[a].at[slot_of(diag, mc)]
            copy(a, 5, part(blk, a, 0), (*y_nbr, mc)).wait_recv()
            if halved[a]:
                copy(a, 6, part(blk, a, 1), (*x_nbr, mc)).wait_recv()
            sends.append(copy(a, 7, blk, sibling))
            sends[-1].start()
        for a in range(n):
            copy(a, 0, o_refs[a].at[sibling_id], sibling).wait_recv()
            for k, chip in ((3, x_nbr), (4, y_nbr), (7, diag)):
                copy(a, k, o_refs[a].at[slot_of(chip, 1 - mc)], sibling).wait_recv()
        for cp in sends:
            cp.wait_send()
        for cp in local:
            cp.wait()

    any_spec = pl.BlockSpec(memory_space=pl.ANY)
    return _pcall(
        body, name=name,
        in_specs=[any_spec] * n, out_specs=[any_spec] * n,
        out_shape=[jax.ShapeDtypeStruct((N_DEV,) + x.shape, x.dtype) for x in xs],
        scratch_shapes=[pltpu.SemaphoreType.DMA((n, n_sems)), pltpu.SemaphoreType.DMA((n, n_sems)),
                        pltpu.SemaphoreType.DMA((n,))],
    )(*xs)


def _pair_exchange(gs, name):
    n = len(gs)

    def body(*refs):
        g_refs, o_refs = refs[:n], refs[n:2 * n]
        send_sems, recv_sems = refs[2 * n:]
        mx, my, mc, _ = _my_place()
        copies = [pltpu.make_async_remote_copy(
            src_ref=g_refs[a].at[p, 1 - mc], dst_ref=o_refs[a].at[p], send_sem=send_sems.at[a, p],
            recv_sem=recv_sems.at[a, p], device_id=(mx, my, 1 - mc), device_id_type=MESH)
            for a in range(n) for p in range(N_CHIPS)]
        for cp in copies:
            cp.start()
        for cp in copies:
            cp.wait()

    any_spec = pl.BlockSpec(memory_space=pl.ANY)
    return _pcall(
        body, name=name,
        in_specs=[any_spec] * n, out_specs=[any_spec] * n,
        out_shape=[jax.ShapeDtypeStruct((N_CHIPS,) + g.shape[2:], g.dtype) for g in gs],
        scratch_shapes=[pltpu.SemaphoreType.DMA((n, N_CHIPS)), pltpu.SemaphoreType.DMA((n, N_CHIPS))],
    )(*gs)


def _pair_add(g, other, name):
    chips, _, r, c = g.shape
    tr = _shard_row_tile(r)
    core = lax.axis_index("c").astype(jnp.int32).reshape(1)

    def body(core_ref, g_ref, o_ref, h_ref):
        h_ref[...] = (g_ref[...].astype(F32)[0] + o_ref[...].astype(F32)).astype(h_ref.dtype)

    blk = pl.BlockSpec((1, tr, c), lambda p, i, core_ref: (p, i, 0))
    return _pcall(
        body, name=name,
        grid_spec=pltpu.PrefetchScalarGridSpec(
            num_scalar_prefetch=1, grid=(chips, pl.cdiv(r, tr)),
            in_specs=[pl.BlockSpec((1, 1, tr, c), lambda p, i, core_ref: (p, core_ref[0], i, 0)), blk],
            out_specs=blk),
        out_shape=jax.ShapeDtypeStruct((chips, r, c), g.dtype),
        compiler_params=_params("parallel", "parallel"),
    )(core, g, other)


def _chip_exchange(hs, name):
    n = len(hs)

    def body(*refs):
        h_refs, o_refs = refs[:n], refs[n:2 * n]
        send_sems, recv_sems, local_sems = refs[2 * n:]
        mx, my, mc, _ = _my_place()
        my_chip = 2 * mx + my
        chips = _other_chips(mx, my)
        local = [pltpu.make_async_copy(h_refs[a].at[my_chip], o_refs[a].at[my_chip], local_sems.at[a]) for a in range(n)]
        for cp in local:
            cp.start()
        for j, (px, py) in enumerate(chips):
            for a in range(n):
                pltpu.make_async_remote_copy(
                    src_ref=h_refs[a].at[2 * px + py], dst_ref=o_refs[a].at[my_chip], send_sem=send_sems.at[a, j],
                    recv_sem=recv_sems.at[a, j], device_id=(px, py, mc), device_id_type=MESH).start()
        for j, (px, py) in enumerate(chips):
            for a in range(n):
                pltpu.make_async_remote_copy(
                    src_ref=h_refs[a].at[2 * px + py], dst_ref=o_refs[a].at[2 * px + py], send_sem=send_sems.at[a, j],
                    recv_sem=recv_sems.at[a, j], device_id=(px, py, mc), device_id_type=MESH).wait()
        for cp in local:
            cp.wait()

    any_spec = pl.BlockSpec(memory_space=pl.ANY)
    return _pcall(
        body, name=name,
        in_specs=[any_spec] * n, out_specs=[any_spec] * n,
        out_shape=[jax.ShapeDtypeStruct(h.shape, h.dtype) for h in hs],
        scratch_shapes=[pltpu.SemaphoreType.DMA((n, N_CHIPS - 1)), pltpu.SemaphoreType.DMA((n, N_CHIPS - 1)),
                        pltpu.SemaphoreType.DMA((n,))],
    )(*hs)


def _adamw(parts, w, m, v, name):
    r, c = w.shape
    n_parts = parts.shape[0]
    tr = _shard_row_tile(r)
    bc1 = 1.0 - ADAM_B1 ** ADAM_STEP
    bc2 = 1.0 - ADAM_B2 ** ADAM_STEP

    def body(p_ref, w_ref, m_ref, v_ref, g_ref, d_ref, nm_ref, nv_ref):
        g = p_ref[0].astype(F32)
        for j in range(1, n_parts):
            g = g + p_ref[j].astype(F32)
        nm = ADAM_B1 * m_ref[...] + (1.0 - ADAM_B1) * g
        nv = ADAM_B2 * v_ref[...] + (1.0 - ADAM_B2) * (g * g)
        g_ref[...] = g
        nm_ref[...] = nm
        nv_ref[...] = nv
        d_ref[...] = -ADAM_LR * ((nm / bc1) / (jnp.sqrt(nv / bc2) + ADAM_EPS) + ADAM_WD * w_ref[...])

    blk = pl.BlockSpec((tr, c), lambda i: (i, 0))
    return _pcall(
        body, name=name, grid=(pl.cdiv(r, tr),),
        in_specs=[pl.BlockSpec((n_parts, tr, c), lambda i: (0, i, 0)), blk, blk, blk],
        out_specs=[blk] * 4, out_shape=[jax.ShapeDtypeStruct((r, c), F32)] * 4,
        compiler_params=_params("parallel"),
    )(parts, w, m, v)


def _local_step(x, target, norm_w, w_segs, conv_w, a_log, dt_bias, dn_norm_w, w_o_dn, w_o_dil, w_out, final_norm_w):
    s = x.shape[0]
    w_qkv, w_za, w_ba, w_qb, w_kb, w_vb, w_zb, w_ga, w_gb = w_segs
    conv_w8 = jnp.concatenate([conv_w, jnp.zeros((SUBLANES - conv_w.shape[0], QKV_W), F32)], axis=0)
    pad8 = jnp.zeros((1, DN_HEADS), F32)
    alog_row = jnp.concatenate([pad8, a_log, jnp.zeros((1, LANES - 2 * DN_HEADS), F32)], axis=1)
    dtb_row = jnp.concatenate([pad8, dt_bias, jnp.zeros((1, LANES - 2 * DN_HEADS), F32)], axis=1)
    wf_row = final_norm_w.reshape(1, D_MODEL)

    hb, qkv_pre, z_a, ba, z_b = _rms_proj_fwd(x, norm_w, [w_qkv, w_za, w_ba, w_zb], "rms_proj_fwd_a")
    q_b, k_b, v_b, g_a, g_b = _mm_out(hb, [w_qb, w_kb, w_vb, w_ga, w_gb], "proj_fwd_b", w_is_out_by_in=True)

    u_d, w_d, qd_d, kd_d, aqk_d, dl_d, t2_d, qn, kn, vn, bg = _delta_prep(qkv_pre, ba, conv_w8, alog_row, dtb_row)
    parts, lses = [], []
    for gi in range(N_DIL):
        o_g, l_g = _attn_fwd(q_b, k_b, v_b, gi)
        parts.append(o_g)
        lses.append(l_g)
    o_a, vnew_d, st_d, on_b, y_a, lse, o_joint, ob_b, y_b = _delta_scan_fwd(
        u_d, w_d, qd_d, kd_d, aqk_d, dl_d, z_a, dn_norm_w, w_o_dn, parts, lses, z_b, w_o_dil)

    loss8, dwf8, merged_b, dx2_b, dx2, dya_b, dyb_b, dga_b, dgb_b = _merge_out_final(
        g_a, g_b, y_a, y_b, x, target, w_out, wf_row)

    g_w_out = _mm_tn(merged_b, dx2_b, "out_wgrad")
    g_w_o_dn = _mm_tn(on_b, dya_b, "out_dn_wgrad")

    g_w_o_dil = _mm_tn(ob_b, dyb_b, "out_dil_wgrad")
    dvnew_d, dkd_d, ddl_d, d_o_a, dza_b, ddnw8, d_o, dzb_b, delta = _delta_scan_bwd(
        w_d, qd_d, kd_d, aqk_d, dl_d, vnew_d, st_d, dya_b, o_a, z_a, dn_norm_w, w_o_dn, dyb_b, o_joint, z_b, w_o_dil)
    dqs, dks, dvs = [], [], []
    for gi in range(N_DIL):
        dq_g, dk_g, dv_g = _attn_bwd(q_b, k_b, v_b, d_o, lse, delta, gi)
        dqs.append(dq_g)
        dks.append(dk_g)
        dvs.append(dv_g)

    dc, dba_b, dsmall8 = _delta_post_bwd(qn, kn, vn, bg, t2_d, st_d, vnew_d, d_o_a, dvnew_d, dkd_d, ddl_d,
                                         qkv_pre, ba, conv_w8, alog_row, dtb_row)
    dqkv_b, dconv8 = _conv_bwd(dc, qkv_pre, conv_w8)

    per_group = lambda w: [w[g * DIL_W:(g + 1) * DIL_W] for g in range(N_DIL)]
    dh_b = _mm_in(dqs + dks + dvs + [dga_b, dgb_b],
                  per_group(w_qb) + per_group(w_kb) + per_group(w_vb) + [w_ga, w_gb], "proj_bwd_b", w_is_out_by_in=True)
    dsegs = [dqkv_b, dza_b, dba_b] + dqs + dks + dvs + [dzb_b, dga_b, dgb_b]
    valid_rows = [d.shape[1] for d in dsegs]
    valid_rows[2] = 2 * DN_HEADS
    g_wt = _proj_wgrad_all(dsegs, valid_rows, hb)
    grad_x, dnw8 = _proj_bwd_rms_in([dqkv_b, dza_b, dba_b, dzb_b], [w_qkv, w_za, w_ba, w_zb], dh_b, x, dx2, norm_w)

    small = dict(norm_w=dnw8[0:1], final_norm_w=dwf8[0:1], dn_norm_w=ddnw8[0:1],
                 a_log=dsmall8[0:1, DN_HEADS:2 * DN_HEADS], dt_bias=dsmall8[1:2, DN_HEADS:2 * DN_HEADS])
    return loss8[0:1, 0:1], grad_x, g_wt, dconv8[0:4], g_w_o_dn, g_w_o_dil, g_w_out, small


def _proj_bwd_rms_in(ds, ws, dh_a, x, dx2, norm_w):
    n_seg = len(ds)

    def body(*refs):
        d_refs, w_refs = refs[:n_seg], refs[n_seg:2 * n_seg]
        da_ref, x_ref, dx2_ref, w_ref, dx_ref, dw_ref = refs[2 * n_seg:]
        dx_ref[...] = da_ref[...]
        for d_ref, wt_ref in zip(d_refs, w_refs):
            for c, wd in _col_chunks(d_ref.shape[1], 1024):
                dx_ref[...] += jnp.dot(d_ref[:, c:c + wd], wt_ref[c:c + wd, :], preferred_element_type=F32)
        xv = x_ref[...]
        r = lax.rsqrt(jnp.mean(xv * xv, axis=-1, keepdims=True) + NORM_EPS)
        dhv = dx_ref[...]
        dn = dhv * w_ref[...]
        dx_ref[...] = dx2_ref[...] + r * dn - xv * (r * r * r) * jnp.mean(dn * xv, axis=-1, keepdims=True)
        row = jnp.sum(dhv * xv * r, axis=0, keepdims=True)
        _acc_add(dw_ref, jnp.concatenate([row, jnp.zeros((SUBLANES - 1, row.shape[1]), F32)], axis=0))

    return _rows_call(body, "proj_bwd_b_rms_in", x.shape[0],
                      [(d, "tile") for d in ds] + [(w, "full") for w in ws]
                      + [(dh_a, "tile"), (x, "tile"), (dx2, "tile"), (norm_w, "full")],
                      [(x.shape, F32, "tile"), ((SUBLANES, x.shape[1]), F32, "acc")])


def _split_proj_rows(w_shards):
    n_shards, rows, k = w_shards.shape
    wt_full = w_shards.reshape(n_shards * rows, k)
    offs = [0]
    for n in PROJ_SIZES:
        offs.append(offs[-1] + n)
    seg = lambda a, b: wt_full[offs[a]:offs[b]]
    w_ba = jnp.concatenate([seg(4, 6), jnp.zeros((LANES - 2 * DN_HEADS, k), wt_full.dtype)], axis=0)
    return [seg(0, 3), seg(3, 4), w_ba, seg(6, 7), seg(7, 8), seg(8, 9), seg(9, 10), seg(10, 11), seg(11, 12)]


LOSS_ROW = 5


def _pack_small(norm_w, final_norm_w, dn_norm_w, a_log, dt_bias, loss=None):
    pad = lambda r: jnp.concatenate([r, jnp.zeros((1, D_MODEL - r.shape[1]), F32)], axis=1)
    rows = [pad(norm_w.reshape(1, -1)), pad(final_norm_w.reshape(1, -1)), pad(dn_norm_w.reshape(1, -1)),
            pad(a_log.reshape(1, -1)), pad(dt_bias.reshape(1, -1)),
            pad(jnp.zeros((1, 1), F32) if loss is None else loss.reshape(1, 1)),
            jnp.zeros((SUBLANES - LOSS_ROW - 1, D_MODEL), F32)]
    return jnp.concatenate(rows, axis=0)


def _unpack_small(p):
    return dict(norm_w=p[0:1], final_norm_w=p[1], dn_norm_w=p[2:3, :DN_DK], a_log=p[3:4, :DN_HEADS],
                dt_bias=p[4:5, :DN_HEADS])


def kernel(x, norm_w, w_in, conv_w, a_log, dt_bias, dn_norm_w, w_o_dn, w_o_dil, w_out, final_norm_w, loss_target, m_norm_w, m_w_in, m_conv_w, m_a_log, m_dt_bias, m_dn_norm_w, m_w_o_dn, m_w_o_dil, m_w_out, m_final_norm_w, v_norm_w, v_w_in, v_conv_w, v_a_log, v_dt_bias, v_dn_norm_w, v_w_o_dn, v_w_o_dil, v_w_out, v_final_norm_w):
    shard_w = w_in.shape[2]
    wt, m_wt, v_wt = (jnp.transpose(t[0]) for t in (w_in, m_w_in, v_w_in))
    gathered = _all_gather([wt.astype(MXU), w_o_dn[0].astype(MXU), w_o_dil[0].astype(MXU), w_out[0].astype(MXU),
                            conv_w[0]], "gather_weights")
    w_in_all, w_o_dn_all, w_o_dil_all, w_out_all, conv_all = gathered
    w_o_dn_full = w_o_dn_all.reshape(D_MODEL, D_MODEL)
    w_o_dil_full = jnp.transpose(w_o_dil_all, (1, 0, 2)).reshape(DIL_W, D_MODEL)
    w_out_full = w_out_all.reshape(D_MODEL, D_MODEL)
    conv_full = jnp.transpose(conv_all, (1, 0, 2)).reshape(conv_w.shape[1], QKV_W)

    loss11, grad_x, g_wt, g_conv, g_w_o_dn, g_w_o_dil, g_w_out, small = _local_step(
        x[0], loss_target[0], norm_w, _split_proj_rows(w_in_all), conv_full, a_log, dt_bias, dn_norm_w,
        w_o_dn_full, w_o_dil_full, w_out_full, final_norm_w)

    col_shards = lambda g, n: jnp.transpose(g.reshape(g.shape[0], N_DEV, n), (1, 0, 2))
    row_shards = lambda g: g.reshape(N_DEV, g.shape[0] // N_DEV, g.shape[1])
    g_wt_shards = jnp.stack([g_wt[j * shard_w:(j + 1) * shard_w] for j in range(N_DEV)], axis=0)
    sent = [g_wt_shards, row_shards(g_w_o_dn).astype(MXU),
            col_shards(g_w_o_dil, w_o_dil.shape[2]).astype(MXU), row_shards(g_w_out).astype(MXU),
            col_shards(g_conv, conv_w.shape[2])]
    sent = [g8.reshape((N_CHIPS, 2) + g8.shape[1:]) for g8 in sent]
    from_sibling = _pair_exchange(sent, "scatter_pair")
    summed = [_pair_add(g, o, f"pair_add_{i}") for i, (g, o) in enumerate(zip(sent, from_sibling))]
    p_w_in, p_w_o_dn, p_w_o_dil, p_w_out, p_conv = _chip_exchange(summed, "scatter_chips")
    p_small = _all_gather([_pack_small(small["norm_w"], small["final_norm_w"], small["dn_norm_w"], small["a_log"],
                                       small["dt_bias"], loss11)], "gather_small_grads")[0]

    res = {}
    res["w_in"] = [jnp.transpose(t) for t in _adamw(p_w_in, wt, m_wt, v_wt, "adamw_w_in")]
    res["conv_w"] = _adamw(p_conv, conv_w[0], m_conv_w[0], v_conv_w[0], "adamw_conv_w")
    res["w_o_dn"] = _adamw(p_w_o_dn, w_o_dn[0], m_w_o_dn[0], v_w_o_dn[0], "adamw_w_o_dn")
    res["w_o_dil"] = _adamw(p_w_o_dil, w_o_dil[0], m_w_o_dil[0], v_w_o_dil[0], "adamw_w_o_dil")
    res["w_out"] = _adamw(p_w_out, w_out[0], m_w_out[0], v_w_out[0], "adamw_w_out")
    small_res = _adamw(p_small, _pack_small(norm_w, final_norm_w, dn_norm_w, a_log, dt_bias),
                       _pack_small(m_norm_w, m_final_norm_w, m_dn_norm_w, m_a_log, m_dt_bias),
                       _pack_small(v_norm_w, v_final_norm_w, v_dn_norm_w, v_a_log, v_dt_bias), "adamw_small")
    loss = small_res[0][LOSS_ROW, 0]
    small_res = [_unpack_small(t) for t in small_res]

    names = ["norm_w", "w_in", "conv_w", "a_log", "dt_bias", "dn_norm_w", "w_o_dn", "w_o_dil", "w_out", "final_norm_w"]
    outs = [loss, grad_x[None]]
    for kind in range(4):
        for nm in names:
            outs.append(res[nm][kind][None] if nm in res else small_res[kind][nm])
    return tuple(outs)
```

```python
import math

import jax
import jax.numpy as jnp
from jax import lax
from jax.experimental import pallas as pl
from jax.experimental.pallas import tpu as pltpu

F32 = jnp.float32
MXU = jnp.bfloat16
MESH = pl.DeviceIdType.MESH

N_DEV = 8
D_MODEL = 1024
DN_HEADS = 8
DN_DK = 128
DN_CHUNK = 64
N_DIL = 3
DIL_HEADS = 4
DIL_DH = 128
DIL_W = DIL_HEADS * DIL_DH
DIL_GROUPS = ((128, 1), (512, 4), (2048, 16))
ATT_BLOCK = 128
NORM_EPS = 1e-6
QKV_W = 3 * D_MODEL
DILQ_W = N_DIL * DIL_W
PROJ_SIZES = (1024, 1024, 1024, 1024, 8, 8, DILQ_W, DILQ_W, DILQ_W, DIL_W, D_MODEL, D_MODEL)

ADAM_LR = 0.001
ADAM_B1 = 0.9
ADAM_B2 = 0.999
ADAM_EPS = 1e-08
ADAM_WD = 0.01
ADAM_STEP = 10

ROW_TILE = 256
LANES = 128
SUBLANES = 8
VMEM_LIMIT = 48 << 20


def _pcall(body, **kw):
    return pl.pallas_call(body, **kw)


def _params(*sem):
    return pltpu.CompilerParams(dimension_semantics=tuple(sem), vmem_limit_bytes=VMEM_LIMIT)


def _sigmoid(x):
    return 1.0 / (1.0 + jnp.exp(-x))


def _softplus(x):
    return jnp.maximum(x, 0.0) + jnp.log(1.0 + jnp.exp(-jnp.abs(x)))


def _dot(a, b):
    return jnp.dot(a.astype(MXU), b.astype(MXU), preferred_element_type=F32)


def _dot_nt(a, b):
    return lax.dot_general(a.astype(MXU), b.astype(MXU), (((1,), (1,)), ((), ())), preferred_element_type=F32)


def _dot_tn(a, b):
    return lax.dot_general(a.astype(MXU), b.astype(MXU), (((0,), (0,)), ((), ())), preferred_element_type=F32)


def _split3(x):
    hi = x.astype(jnp.bfloat16)
    r1 = x - hi.astype(F32)
    mid = r1.astype(jnp.bfloat16)
    lo = (r1 - mid.astype(F32)).astype(jnp.bfloat16)
    return hi, mid, lo


def _dot01(m01, x):
    m = m01.astype(jnp.bfloat16)
    hi, mid, lo = _split3(x)
    f = lambda p: jnp.dot(m, p, preferred_element_type=F32)
    return f(hi) + (f(mid) + f(lo))


def _rows_call(body, name, n_rows, ins, outs, scratch=(), tm=ROW_TILE):
    steps = n_rows // tm
    per8 = tm // SUBLANES
    last8 = n_rows // SUBLANES - 1
    in_specs = []
    for arr, kind in ins:
        cols = arr.shape[-1]
        if kind == "tile":
            in_specs.append(pl.BlockSpec((tm, cols), lambda i: (i, 0)))
        elif kind == "full":
            in_specs.append(pl.BlockSpec(arr.shape, lambda i, nd=arr.ndim: (0,) * nd))
        elif kind == "prev8":
            in_specs.append(pl.BlockSpec((SUBLANES, cols), lambda i: (jnp.maximum(i * per8 - 1, 0), 0)))
        elif kind == "next8":
            in_specs.append(pl.BlockSpec((SUBLANES, cols), lambda i: (jnp.minimum((i + 1) * per8, last8), 0)))
        else:
            raise ValueError(kind)
    out_specs, out_shape, has_acc = [], [], False
    for shape, dtype, kind in outs:
        out_shape.append(jax.ShapeDtypeStruct(shape, dtype))
        if kind == "tile":
            out_specs.append(pl.BlockSpec((tm, shape[-1]), lambda i: (i, 0)))
        else:
            has_acc = True
            out_specs.append(pl.BlockSpec(shape, lambda i: (0, 0)))
    return _pcall(
        body, name=name, grid=(steps,), in_specs=in_specs, out_specs=out_specs, out_shape=out_shape,
        scratch_shapes=list(scratch),
        compiler_params=_params("arbitrary" if has_acc else "parallel"),
    )(*[a for a, _ in ins])


def _acc_add(ref, value):
    @pl.when(pl.program_id(0) == 0)
    def _():
        ref[...] = jnp.zeros_like(ref)
    ref[...] += value


def _col_chunks(n, width=512):
    return [(c, min(width, n - c)) for c in range(0, n, width)]


NT_DIMS = (((1,), (1,)), ((), ()))
TN_DIMS = (((0,), (0,)), ((), ()))


def _mm_out(a, ws, name, w_is_out_by_in=False, out_dtype=F32, tm=ROW_TILE):
    m, k = a.shape
    ns = [w.shape[0] if w_is_out_by_in else w.shape[1] for w in ws]

    def body(a_ref, *refs):
        av = a_ref[...]
        for w_ref, o_ref, n in zip(refs[:len(ws)], refs[len(ws):], ns):
            for c, wd in _col_chunks(n):
                if w_is_out_by_in:
                    part = lax.dot_general(av, w_ref[c:c + wd, :], NT_DIMS, preferred_element_type=F32)
                else:
                    part = jnp.dot(av, w_ref[:, c:c + wd], preferred_element_type=F32)
                o_ref[:, c:c + wd] = part.astype(o_ref.dtype)

    return _pcall(
        body, name=name, grid=(m // tm,),
        in_specs=[pl.BlockSpec((tm, k), lambda i: (i, 0))] + [pl.BlockSpec(w.shape, lambda i: (0, 0)) for w in ws],
        out_specs=[pl.BlockSpec((tm, n), lambda i: (i, 0)) for n in ns],
        out_shape=[jax.ShapeDtypeStruct((m, n), out_dtype) for n in ns],
        compiler_params=_params("parallel"),
    )(a, *ws)


def _rms_proj_fwd(x, norm_w, wts, name, tm=ROW_TILE):
    m, k = x.shape
    ns = [w.shape[0] for w in wts]

    def body(x_ref, nw_ref, *refs):
        w_refs, h_ref, o_refs = refs[:len(wts)], refs[len(wts)], refs[len(wts) + 1:]
        xv = x_ref[...]
        r = lax.rsqrt(jnp.mean(xv * xv, axis=-1, keepdims=True) + NORM_EPS)
        hv = (xv * r * nw_ref[...]).astype(h_ref.dtype)
        h_ref[...] = hv
        for w_ref, o_ref, n in zip(w_refs, o_refs, ns):
            for c, wd in _col_chunks(n):
                o_ref[:, c:c + wd] = lax.dot_general(hv, w_ref[c:c + wd, :], NT_DIMS, preferred_element_type=F32)

    return _pcall(
        body, name=name, grid=(m // tm,),
        in_specs=[pl.BlockSpec((tm, k), lambda i: (i, 0)), pl.BlockSpec(norm_w.shape, lambda i: (0, 0))]
        + [pl.BlockSpec(w.shape, lambda i: (0, 0)) for w in wts],
        out_specs=[pl.BlockSpec((tm, k), lambda i: (i, 0))] + [pl.BlockSpec((tm, n), lambda i: (i, 0)) for n in ns],
        out_shape=[jax.ShapeDtypeStruct((m, k), MXU)] + [jax.ShapeDtypeStruct((m, n), F32) for n in ns],
        compiler_params=_params("parallel"),
    )(x, norm_w, *wts)


def _mm_in(ds, ws, name, w_is_out_by_in=False, tm=ROW_TILE):
    m = ds[0].shape[0]
    k = ws[0].shape[1] if w_is_out_by_in else ws[0].shape[0]
    ns = [d.shape[1] for d in ds]

    def body(*refs):
        d_refs, w_refs, o_ref = refs[:len(ds)], refs[len(ds):2 * len(ds)], refs[-1]
        first = True
        for d_ref, w_ref, n in zip(d_refs, w_refs, ns):
            for c, wd in _col_chunks(n, 1024):
                if w_is_out_by_in:
                    part = jnp.dot(d_ref[:, c:c + wd], w_ref[c:c + wd, :], preferred_element_type=F32)
                else:
                    part = lax.dot_general(d_ref[:, c:c + wd], w_ref[:, c:c + wd], NT_DIMS, preferred_element_type=F32)
                if first:
                    o_ref[...] = part
                    first = False
                else:
                    o_ref[...] += part

    return _pcall(
        body, name=name, grid=(m // tm,),
        in_specs=[pl.BlockSpec((tm, n), lambda i: (i, 0)) for n in ns] + [pl.BlockSpec(w.shape, lambda i: (0, 0)) for w in ws],
        out_specs=pl.BlockSpec((tm, k), lambda i: (i, 0)),
        out_shape=jax.ShapeDtypeStruct((m, k), F32),
        compiler_params=_params("parallel"),
    )(*ds, *ws)


def _mm_tn(a, d, name):
    m, k = a.shape
    n = d.shape[1]
    tk = 512 if k % 512 == 0 else k

    def body(a_ref, d_ref, o_ref):
        o_ref[...] = lax.dot_general(a_ref[...], d_ref[...], TN_DIMS, preferred_element_type=F32)

    return _pcall(
        body, name=name, grid=(k // tk,),
        in_specs=[pl.BlockSpec((m, tk), lambda p: (0, p)), pl.BlockSpec((m, n), lambda p: (0, 0))],
        out_specs=pl.BlockSpec((tk, n), lambda p: (p, 0)),
        out_shape=jax.ShapeDtypeStruct((k, n), F32),
        compiler_params=_params("parallel"),
    )(a, d)


WGRAD_TILE = 512


def _proj_wgrad_all(dsegs, valid_rows, hb):
    m, k = hb.shape
    n_seg = len(dsegs)
    tiles, row = [], 0
    for si, (d, valid) in enumerate(zip(dsegs, valid_rows)):
        for c in range(0, valid, WGRAD_TILE):
            width = min(WGRAD_TILE, d.shape[1] - c)
            tiles.append((si, c, width, row + c, min(width, valid - c)))
        row += valid
    total_rows = row

    def body(*refs):
        d_refs, hb_ref, o_ref = refs[:n_seg], refs[n_seg], refs[n_seg + 1]
        a_buf, hb_buf, o_buf, load_sems, store_sems, hb_sem = refs[n_seg + 2:]

        def load(t):
            si, c, width, _, _ = tiles[t]
            return pltpu.make_async_copy(d_refs[si].at[:, pl.ds(c, width)], a_buf.at[t % 2, :, pl.ds(0, width)],
                                         load_sems.at[t % 2])

        def stores(t):
            _, _, _, orow, valid = tiles[t]
            return [pltpu.make_async_copy(o_buf.at[t % 2, pl.ds(0, valid), :], o_ref.at[pl.ds(orow, valid), :],
                                          store_sems.at[t % 2])]

        hb_copy = pltpu.make_async_copy(hb_ref, hb_buf, hb_sem)
        hb_copy.start()
        load(0).start()
        hb_copy.wait()
        for t in range(len(tiles)):
            width = tiles[t][2]
            load(t).wait()
            if t + 1 < len(tiles):
                load(t + 1).start()
            if t >= 2:
                for cp in stores(t - 2):
                    cp.wait()
            o_buf[t % 2, 0:width, :] = lax.dot_general(a_buf[t % 2, :, 0:width], hb_buf[...], TN_DIMS,
                                                        preferred_element_type=F32).astype(o_buf.dtype)
            for cp in stores(t):
                cp.start()
        for t in range(max(len(tiles) - 2, 0), len(tiles)):
            for cp in stores(t):
                cp.wait()

    any_spec = pl.BlockSpec(memory_space=pl.ANY)
    return _pcall(
        body, name="proj_wgrad",
        in_specs=[any_spec] * (n_seg + 1), out_specs=any_spec,
        out_shape=jax.ShapeDtypeStruct((total_rows, k), hb.dtype),
        scratch_shapes=[pltpu.VMEM((2, m, WGRAD_TILE), hb.dtype), pltpu.VMEM((m, k), hb.dtype),
                        pltpu.VMEM((2, WGRAD_TILE, k), hb.dtype), pltpu.SemaphoreType.DMA((2,)),
                        pltpu.SemaphoreType.DMA((2,)), pltpu.SemaphoreType.DMA],
        compiler_params=pltpu.CompilerParams(vmem_limit_bytes=VMEM_LIMIT),
    )(*dsegs, hb)


def _conv_taps(ext_ref, cw_ref, cols, tm):
    c = None
    for j in range(4):
        term = cw_ref[3 - j:4 - j, cols] * ext_ref[SUBLANES - j:SUBLANES - j + tm, cols]
        c = term if c is None else c + term
    return c


def _fill_ext(ext_ref, u_ref, halo_ref, first):
    ext_ref[0:SUBLANES, :] = jnp.where(first, 0.0, halo_ref[...])
    ext_ref[SUBLANES:, :] = u_ref[...]


def _dn_prep_fwd_tile(u_ref, halo_ref, cw_ref, ba_ref, al_ref, dtb_ref, q_ref, k_ref, v_ref, bg_ref, ext_ref, first):
    tm = u_ref.shape[0]
    _fill_ext(ext_ref, u_ref, halo_ref, first)
    for h in range(3 * DN_HEADS):
        cols = slice(h * LANES, (h + 1) * LANES)
        c = _conv_taps(ext_ref, cw_ref, cols, tm)
        a = c * _sigmoid(c)
        oc = slice((h % DN_HEADS) * LANES, (h % DN_HEADS + 1) * LANES)
        if h < 2 * DN_HEADS:
            rinv = lax.rsqrt(jnp.sum(a * a, axis=-1, keepdims=True) + NORM_EPS)
            if h < DN_HEADS:
                q_ref[:, oc] = a * (rinv * DN_DK ** -0.5)
            else:
                k_ref[:, oc] = a * rinv
        else:
            v_ref[:, oc] = a
    bav = ba_ref[...]
    lane = lax.broadcasted_iota(jnp.int32, bav.shape, 1)
    beta = _sigmoid(bav)
    g = -jnp.exp(al_ref[...]) * _softplus(bav + dtb_ref[...])
    bg_ref[...] = jnp.where(lane < DN_HEADS, beta, jnp.where(lane < 2 * DN_HEADS, g, 0.0))


def _dn_prep_bwd_tile(u_ref, halo_ref, cw_ref, ba_ref, al_ref, dtb_ref, dq_ref, dk_ref, dv_ref, dbg_ref,
                      dc_ref, dba_ref, ext_ref, first):
    tm = u_ref.shape[0]
    _fill_ext(ext_ref, u_ref, halo_ref, first)
    for h in range(3 * DN_HEADS):
        cols = slice(h * LANES, (h + 1) * LANES)
        oc = slice((h % DN_HEADS) * LANES, (h % DN_HEADS + 1) * LANES)
        c = _conv_taps(ext_ref, cw_ref, cols, tm)
        sg = _sigmoid(c)
        a = c * sg
        if h < 2 * DN_HEADS:
            rinv = lax.rsqrt(jnp.sum(a * a, axis=-1, keepdims=True) + NORM_EPS)
            dy = dq_ref[:, oc] * DN_DK ** -0.5 if h < DN_HEADS else dk_ref[:, oc]
            da = rinv * dy - a * (rinv * rinv * rinv) * jnp.sum(dy * a, axis=-1, keepdims=True)
        else:
            da = dv_ref[:, oc]
        dc_ref[:, cols] = da * (sg * (1.0 + c * (1.0 - sg)))
    bav = ba_ref[...]
    dbgv = dbg_ref[...]
    lane = lax.broadcasted_iota(jnp.int32, bav.shape, 1)
    beta = _sigmoid(bav)
    ea = jnp.exp(al_ref[...])
    z = bav + dtb_ref[...]
    g = -ea * _softplus(z)
    is_b = lane < DN_HEADS
    is_g = jnp.logical_and(lane >= DN_HEADS, lane < 2 * DN_HEADS)
    d_aa = jnp.where(is_g, dbgv * (-ea) * _sigmoid(z), 0.0)
    dba = jnp.where(is_b, dbgv * beta * (1.0 - beta), d_aa)
    dba_ref[...] = dba.astype(dba_ref.dtype)
    r_alog = jnp.sum(jnp.where(is_g, dbgv * g, 0.0), axis=0, keepdims=True)
    r_dtb = jnp.sum(d_aa, axis=0, keepdims=True)
    return jnp.concatenate([r_alog, r_dtb, jnp.zeros((SUBLANES - 2, LANES), F32)], axis=0)


def _conv_bwd_tile(extd_ref, ext_ref, cw_ref, du_ref, dcw_ref, tm):
    for h in range(3 * DN_HEADS):
        cols = slice(h * LANES, (h + 1) * LANES)
        du = None
        for j in range(4):
            term = cw_ref[3 - j:4 - j, cols] * extd_ref[j:j + tm, cols]
            du = term if du is None else du + term
        du_ref[:, cols] = du.astype(du_ref.dtype)
        dcv = extd_ref[0:tm, cols]
        for j in range(4):
            row = jnp.sum(dcv * ext_ref[SUBLANES - j:SUBLANES - j + tm, cols], axis=0, keepdims=True)
            dcw_ref[3 - j:4 - j, cols] += row


def _dn_out_fwd_tile(o_ref, z_ref, w_ref, wo_ref, on_ref, y_ref):
    for h in range(DN_HEADS):
        cols = _head_cols(h)
        ov = o_ref[:, cols]
        zv = z_ref[:, cols]
        ro = lax.rsqrt(jnp.mean(ov * ov, axis=-1, keepdims=True) + NORM_EPS)
        on_ref[:, cols] = (ov * ro * w_ref[...] * (zv * _sigmoid(zv))).astype(on_ref.dtype)
    y_ref[...] = jnp.dot(on_ref[...], wo_ref[...], preferred_element_type=F32)


def _dn_out_bwd_tile(dy_ref, o_ref, z_ref, w_ref, wo_ref, do_ref, dz_ref, d_ref):
    d_ref[...] = lax.dot_general(dy_ref[...], wo_ref[...], NT_DIMS, preferred_element_type=F32)
    acc = jnp.zeros((1, LANES), F32)
    for h in range(DN_HEADS):
        cols = _head_cols(h)
        dv, ov, zv = d_ref[:, cols], o_ref[:, cols], z_ref[:, cols]
        sg = _sigmoid(zv)
        sz = zv * sg
        ro = lax.rsqrt(jnp.mean(ov * ov, axis=-1, keepdims=True) + NORM_EPS)
        nv = ov * ro
        dn = dv * w_ref[...] * sz
        acc = acc + jnp.sum(dv * nv * sz, axis=0, keepdims=True)
        dz_ref[:, cols] = (dv * nv * w_ref[...] * (sg * (1.0 + zv * (1.0 - sg)))).astype(dz_ref.dtype)
        do_ref[:, cols] = ro * dn - ov * (ro * ro * ro) * jnp.mean(dn * ov, axis=-1, keepdims=True)
    return acc


def _attn_out_fwd_tile(o0, o1, o2, l0, l1, l2, z_ref, wo_ref, lse_ref, o_ref, g_ref, y_ref):
    a, b, c = l0[...], l1[...], l2[...]
    m = jnp.maximum(a, jnp.maximum(b, c))
    ea, eb, ec = jnp.exp(a - m), jnp.exp(b - m), jnp.exp(c - m)
    den = ea + eb + ec
    out = (ea * o0[...] + eb * o1[...] + ec * o2[...]) / den
    lse_ref[...] = m + jnp.log(den)
    o_ref[...] = out
    zv = z_ref[...]
    gated = (out * (zv * _sigmoid(zv))).astype(g_ref.dtype)
    g_ref[...] = gated
    y_ref[...] = jnp.dot(gated, wo_ref[...], preferred_element_type=F32)


def _attn_out_bwd_tile(dy_ref, o_ref, z_ref, wo_ref, do_ref, dz_ref, dl_ref):
    zv = z_ref[...]
    sg = _sigmoid(zv)
    dv = lax.dot_general(dy_ref[...], wo_ref[...], NT_DIMS, preferred_element_type=F32)
    ov = o_ref[...]
    do = dv * (zv * sg)
    do_ref[...] = do
    dz_ref[...] = (dv * ov * (sg * (1.0 + zv * (1.0 - sg)))).astype(dz_ref.dtype)
    for h in range(DIL_HEADS):
        cols = _head_cols(h)
        dl_ref[:, cols] = jnp.broadcast_to(jnp.sum(do[:, cols] * ov[:, cols], axis=-1, keepdims=True),
                                           (do.shape[0], LANES))


def _merge_out_final(ga, gb, ya, yb, x, target, w_out, wf_row):
    s, dm = x.shape

    def body(ga_ref, gb_ref, ya_ref, yb_ref, x_ref, t_ref, wo_ref, w_ref,
             loss_ref, dw_ref, m_ref, dxb_ref, dx_ref, dya_ref, dyb_ref, dga_ref, dgb_ref):
        sa, sb = _sigmoid(ga_ref[...]), _sigmoid(gb_ref[...])
        ya, yb = ya_ref[...], yb_ref[...]
        merged = (sa * ya + sb * yb).astype(MXU)
        m_ref[...] = merged
        x2 = x_ref[...] + jnp.dot(merged, wo_ref[...], preferred_element_type=F32)
        r = lax.rsqrt(jnp.mean(x2 * x2, axis=-1, keepdims=True) + NORM_EPS)
        w = w_ref[...]
        err = x2 * r * w - t_ref[...]
        tile_loss = 0.5 * jnp.sum(jnp.mean(err * err, axis=-1, keepdims=True), axis=0, keepdims=True)
        _acc_add(loss_ref, jnp.broadcast_to(tile_loss, (SUBLANES, LANES)))
        dy = err * (1.0 / dm)
        row = jnp.sum(dy * x2 * r, axis=0, keepdims=True)
        _acc_add(dw_ref, jnp.concatenate([row, jnp.zeros((SUBLANES - 1, dm), F32)], axis=0))
        dn = dy * w
        dx2 = r * dn - x2 * (r * r * r) * jnp.mean(dn * x2, axis=-1, keepdims=True)
        dx_ref[...] = dx2
        dxb = dx2.astype(MXU)
        dxb_ref[...] = dxb
        dmv = lax.dot_general(dxb, wo_ref[...], NT_DIMS, preferred_element_type=F32)
        dya_ref[...] = (dmv * sa).astype(dya_ref.dtype)
        dyb_ref[...] = (dmv * sb).astype(dyb_ref.dtype)
        dga_ref[...] = (dmv * ya * sa * (1.0 - sa)).astype(dga_ref.dtype)
        dgb_ref[...] = (dmv * yb * sb * (1.0 - sb)).astype(dgb_ref.dtype)

    return _rows_call(body, "merge_out_final", s,
                      [(ga, "tile"), (gb, "tile"), (ya, "tile"), (yb, "tile"), (x, "tile"), (target, "tile"),
                       (w_out, "full"), (wf_row, "full")],
                      [((SUBLANES, LANES), F32, "acc"), ((SUBLANES, dm), F32, "acc"), ((s, dm), MXU, "tile"),
                       ((s, dm), MXU, "tile"), ((s, dm), F32, "tile")] + [((s, dm), MXU, "tile")] * 4)


def _lane_pick(x, idx):
    lane = lax.broadcasted_iota(jnp.int32, x.shape, 1)
    return jnp.sum(jnp.where(lane == idx, x, 0.0), axis=-1, keepdims=True)


PAIR = 2 * DN_CHUNK
SCAN_CHUNKS = 4


def _bmm(a, b):
    return lax.dot_general(a.astype(MXU), b.astype(MXU), (((2,), (1,)), ((0,), (0,))), preferred_element_type=F32)


def _bmm_nt(a, b):
    return lax.dot_general(a.astype(MXU), b.astype(MXU), (((2,), (2,)), ((0,), (0,))), preferred_element_type=F32)


def _bmm_tn(a, b):
    return lax.dot_general(a.astype(MXU), b.astype(MXU), (((1,), (1,)), ((0,), (0,))), preferred_element_type=F32)


def _bmm3(a, b):
    ah = a.astype(jnp.bfloat16)
    al = (a - ah.astype(F32)).astype(jnp.bfloat16)
    bh = b.astype(jnp.bfloat16)
    bl = (b - bh.astype(F32)).astype(jnp.bfloat16)
    f = lambda p, q: lax.dot_general(p, q, (((2,), (1,)), ((0,), (0,))), preferred_element_type=F32)
    return f(ah, bh) + (f(ah, bl) + f(al, bh))


def _pair_masks():
    row = lax.broadcasted_iota(jnp.int32, (PAIR, PAIR), 0)
    col = lax.broadcasted_iota(jnp.int32, (PAIR, PAIR), 1)
    same = (row >= DN_CHUNK) == (col >= DN_CHUNK)
    return dict(causal=same & (row >= col), strict=same & (row > col), upper=same & (row <= col), eye=row == col,
                first=row < DN_CHUNK, row=row, lane=col)


def _pair_decay(bgv, masks):
    gc_all = _dot01(masks["causal"].astype(F32), bgv)
    out = []
    for h in range(DN_HEADS):
        beta = _lane_pick(bgv, h)
        gcb = jnp.broadcast_to(_lane_pick(gc_all, DN_HEADS + h), (PAIR, PAIR))
        gam = jnp.where(masks["causal"], jnp.exp(jnp.minimum(gcb - gcb.T, 0.0)), 0.0)
        gl = jnp.where(masks["first"], gcb[DN_CHUNK - 1:DN_CHUNK, :], gcb[PAIR - 1:PAIR, :])
        out.append((beta, gcb, gam, gl))
    return out


def _pair_inverse(a_strict, eye):
    eye_f = eye.astype(F32)[None]
    m = eye_f + a_strict
    x = eye_f - a_strict
    steps = int(math.log2(DN_CHUNK)) - 1
    for i in range(steps):
        mm = _bmm3 if i == steps - 1 else _bmm
        x = x + mm(x, eye_f - mm(m, x))
    return x


def _head_cols(h):
    return slice(h * LANES, (h + 1) * LANES)


def _delta_prep(qkv_pre, ba, conv_w8, alog_row, dtb_row):
    s = qkv_pre.shape[0]
    c = DN_CHUNK
    n_chunks = s // c

    def body(pre_ref, halo_ref, cw_ref, ba_ref, al_ref, dtb_ref,
             u_ref, w_ref, qd_ref, kd_ref, aqk_ref, dl_ref, t2_ref, q_ref, k_ref, v_ref, bg_ref, ext_ref):
        _dn_prep_fwd_tile(pre_ref, halo_ref, cw_ref, ba_ref, al_ref, dtb_ref, q_ref, k_ref, v_ref, bg_ref, ext_ref,
                          pl.program_id(0) == 0)
        masks = _pair_masks()
        dec = _pair_decay(bg_ref[...], masks)
        kbs, ks, gams, vbs, kbes, qs, qds, kds, dls = ([] for _ in range(9))
        for h in range(DN_HEADS):
            beta, gcb, gam, gl = dec[h]
            qh, kh, vh = q_ref[:, _head_cols(h)], k_ref[:, _head_cols(h)], v_ref[:, _head_cols(h)]
            eg = jnp.exp(gcb)
            kb = kh * beta
            kbs.append(kb); ks.append(kh); gams.append(gam); vbs.append(vh * beta); kbes.append(kb * eg)
            qs.append(qh); qds.append(qh * eg); kds.append(kh * jnp.exp(gl - gcb)); dls.append(jnp.exp(gl))
        st = lambda xs: jnp.stack(xs, axis=0)
        kmat, gam = st(ks), st(gams)
        a = jnp.where(masks["strict"][None], _bmm_nt(st(kbs), kmat) * gam, 0.0)
        t = _pair_inverse(a, masks["eye"])
        u = _bmm(t, st(vbs))
        w = _bmm(t, st(kbes))
        aqk = _bmm_nt(st(qs), kmat) * gam
        t2_ref[0] = t.astype(t2_ref.dtype)
        for half in range(2):
            rows = slice(half * c, (half + 1) * c)
            u_ref[half] = u[:, rows, :]
            w_ref[half] = w[:, rows, :].astype(w_ref.dtype)
            qd_ref[half] = st(qds)[:, rows, :].astype(qd_ref.dtype)
            kd_ref[half] = st(kds)[:, rows, :].astype(kd_ref.dtype)
            aqk_ref[half] = aqk[:, rows, rows].astype(aqk_ref.dtype)
            dl_ref[half] = st(dls)[:, half * c:half * c + SUBLANES, :]

    row_spec = lambda w_: pl.BlockSpec((PAIR, w_), lambda i: (i, 0))
    hm = lambda a_, b_: pl.BlockSpec((2, DN_HEADS, a_, b_), lambda i: (i, 0, 0, 0))
    hm_shape = lambda a_, b_, dt: jax.ShapeDtypeStruct((n_chunks, DN_HEADS, a_, b_), dt)
    whole = lambda t: pl.BlockSpec(t.shape, lambda i: (0, 0))
    halo = pl.BlockSpec((SUBLANES, QKV_W), lambda i: (jnp.maximum(i * (PAIR // SUBLANES) - 1, 0), 0))
    return _pcall(
        body, name="delta_prep", grid=(n_chunks // 2,),
        in_specs=[row_spec(QKV_W), halo, whole(conv_w8), row_spec(LANES), whole(alog_row), whole(dtb_row)],
        out_specs=[hm(c, LANES)] * 4 + [hm(c, c), hm(SUBLANES, LANES),
                   pl.BlockSpec((1, DN_HEADS, PAIR, PAIR), lambda i: (i, 0, 0, 0))]
        + [row_spec(D_MODEL)] * 3 + [row_spec(LANES)],
        out_shape=[hm_shape(c, LANES, F32), hm_shape(c, LANES, MXU), hm_shape(c, LANES, MXU), hm_shape(c, LANES, MXU),
                   hm_shape(c, c, MXU), hm_shape(SUBLANES, LANES, F32),
                   jax.ShapeDtypeStruct((n_chunks // 2, DN_HEADS, PAIR, PAIR), MXU)]
        + [jax.ShapeDtypeStruct((s, D_MODEL), F32)] * 3 + [jax.ShapeDtypeStruct((s, LANES), F32)],
        scratch_shapes=[pltpu.VMEM((PAIR + SUBLANES, QKV_W), F32)],
        compiler_params=_params("parallel"),
    )(qkv_pre, qkv_pre, conv_w8, ba, alog_row, dtb_row)


def _delta_scan_fwd(u, w, qd, kd, aqk, dl, z, dnw_row, w_o_dn, attn_parts, attn_lses, zb, w_o_dil):
    n_chunks = u.shape[0]
    c = DN_CHUNK
    g_n = SCAN_CHUNKS

    def body(u_ref, w_ref, qd_ref, kd_ref, aqk_ref, dl_ref, z_ref, nw_ref, wo_ref,
             p0, p1, p2, l0, l1, l2, zb_ref, wod_ref,
             o_ref, vnew_ref, st_ref, on_ref, y_ref, lse_ref, oj_ref, gb_ref, yb_ref, state):
        @pl.when(pl.program_id(0) == 0)
        def _():
            state[...] = jnp.zeros_like(state)

        _attn_out_fwd_tile(p0, p1, p2, l0, l1, l2, zb_ref, wod_ref, lse_ref, oj_ref, gb_ref, yb_ref)
        for g in range(g_n):
            sv = state[...]
            sb = sv.astype(MXU)
            vnew = u_ref[g] - _bmm(w_ref[g], sb)
            o = _bmm(qd_ref[g], sb) + _bmm(aqk_ref[g], vnew)
            state[...] = sv * dl_ref[g][:, 0:1, :] + _bmm_tn(kd_ref[g], vnew)
            vnew_ref[g] = vnew.astype(vnew_ref.dtype)
            st_ref[g] = sb
            for h in range(DN_HEADS):
                o_ref[g * c:(g + 1) * c, _head_cols(h)] = o[h]
        _dn_out_fwd_tile(o_ref, z_ref, nw_ref, wo_ref, on_ref, y_ref)

    hm = lambda a_, b_: pl.BlockSpec((g_n, DN_HEADS, a_, b_), lambda i: (i, 0, 0, 0))
    rows = lambda width: pl.BlockSpec((g_n * c, width), lambda i: (i, 0))
    whole = lambda t: pl.BlockSpec(t.shape, lambda i: (0, 0))
    full = lambda width, dt: jax.ShapeDtypeStruct((n_chunks * c, width), dt)
    return _pcall(
        body, name="delta_scan_fwd", grid=(n_chunks // g_n,),
        in_specs=[hm(c, LANES)] * 4 + [hm(c, c), hm(SUBLANES, LANES), rows(D_MODEL), whole(dnw_row), whole(w_o_dn)]
        + [rows(DIL_W)] * 7 + [whole(w_o_dil)],
        out_specs=[rows(D_MODEL), hm(c, LANES), hm(DN_DK, DN_DK), rows(D_MODEL), rows(D_MODEL),
                   rows(DIL_W), rows(DIL_W), rows(DIL_W), rows(D_MODEL)],
        out_shape=[full(D_MODEL, F32),
                   jax.ShapeDtypeStruct((n_chunks, DN_HEADS, c, LANES), MXU),
                   jax.ShapeDtypeStruct((n_chunks, DN_HEADS, DN_DK, DN_DK), MXU),
                   full(D_MODEL, MXU), full(D_MODEL, F32),
                   full(DIL_W, F32), full(DIL_W, F32), full(DIL_W, MXU), full(D_MODEL, F32)],
        scratch_shapes=[pltpu.VMEM((DN_HEADS, DN_DK, DN_DK), F32)],
        compiler_params=_params("arbitrary"),
    )(u, w, qd, kd, aqk, dl, z, dnw_row, w_o_dn, *attn_parts, *attn_lses, zb, w_o_dil)


def _delta_scan_bwd(w, qd, kd, aqk, dl, vnew, st, dy, o, z, dnw_row, w_o_dn, dyb, o_joint, zb, w_o_dil):
    n_chunks = w.shape[0]
    c = DN_CHUNK
    g_n = SCAN_CHUNKS
    steps = n_chunks // g_n

    def body(w_ref, qd_ref, kd_ref, aqk_ref, dl_ref, vnew_ref, st_ref, dy_ref, o_ref, z_ref, nw_ref, wo_ref,
             dyb_ref, oj_ref, zb_ref, wod_ref,
             dvnew_ref, dkd_ref, ddl_ref, do_ref, dz_ref, dnw_ref, dob_ref, dzb_ref, delta_ref, dstate, d_scratch):
        @pl.when(pl.program_id(0) == 0)
        def _():
            dstate[...] = jnp.zeros_like(dstate)

        _attn_out_bwd_tile(dyb_ref, oj_ref, zb_ref, wod_ref, dob_ref, dzb_ref, delta_ref)
        acc = _dn_out_bwd_tile(dy_ref, o_ref, z_ref, nw_ref, wo_ref, do_ref, dz_ref, d_scratch)
        _acc_add(dnw_ref, jnp.concatenate([acc, jnp.zeros((SUBLANES - 1, LANES), F32)], axis=0))
        for g in reversed(range(g_n)):
            ds = dstate[...]
            dsb = ds.astype(MXU)
            doh = jnp.stack([do_ref[g * c:(g + 1) * c, _head_cols(h)] for h in range(DN_HEADS)], axis=0)
            dvnew = _bmm_tn(aqk_ref[g], doh) + _bmm(kd_ref[g], dsb)
            dkd_ref[g] = _bmm_nt(vnew_ref[g], dsb)
            ddl = jnp.sum(jnp.sum(st_ref[g].astype(F32) * ds, axis=2, keepdims=True), axis=1, keepdims=True)
            ddl_ref[g] = jnp.broadcast_to(ddl, (DN_HEADS, SUBLANES, LANES))
            dstate[...] = ds * dl_ref[g][:, 0:1, :] + _bmm_tn(qd_ref[g], doh) - _bmm_tn(w_ref[g], dvnew)
            dvnew_ref[g] = dvnew.astype(dvnew_ref.dtype)

    rev = lambda i: steps - 1 - i
    hm = lambda a_, b_: pl.BlockSpec((g_n, DN_HEADS, a_, b_), lambda i: (rev(i), 0, 0, 0))
    rows = lambda width: pl.BlockSpec((g_n * c, width), lambda i: (rev(i), 0))
    whole = lambda t: pl.BlockSpec(t.shape, lambda i: (0, 0))
    full = lambda width, dt: jax.ShapeDtypeStruct((n_chunks * c, width), dt)
    return _pcall(
        body, name="delta_scan_bwd", grid=(steps,),
        in_specs=[hm(c, LANES)] * 3 + [hm(c, c), hm(SUBLANES, LANES), hm(c, LANES), hm(DN_DK, DN_DK),
                  rows(D_MODEL), rows(D_MODEL), rows(D_MODEL), whole(dnw_row), whole(w_o_dn),
                  rows(D_MODEL), rows(DIL_W), rows(DIL_W), whole(w_o_dil)],
        out_specs=[hm(c, LANES), hm(c, LANES), hm(SUBLANES, LANES), rows(D_MODEL), rows(D_MODEL),
                   pl.BlockSpec((SUBLANES, LANES), lambda i: (0, 0)), rows(DIL_W), rows(DIL_W), rows(DIL_W)],
        out_shape=[jax.ShapeDtypeStruct((n_chunks, DN_HEADS, c, LANES), MXU),
                   jax.ShapeDtypeStruct((n_chunks, DN_HEADS, c, LANES), F32),
                   jax.ShapeDtypeStruct((n_chunks, DN_HEADS, SUBLANES, LANES), F32),
                   full(D_MODEL, F32), full(D_MODEL, MXU), jax.ShapeDtypeStruct((SUBLANES, LANES), F32),
                   full(DIL_W, F32), full(DIL_W, MXU), full(DIL_W, F32)],
        scratch_shapes=[pltpu.VMEM((DN_HEADS, DN_DK, DN_DK), F32), pltpu.VMEM((g_n * c, D_MODEL), F32)],
        compiler_params=_params("arbitrary"),
    )(w, qd, kd, aqk, dl, vnew, st, dy, o, z, dnw_row, w_o_dn, dyb, o_joint, zb, w_o_dil)


def _delta_post_bwd(q, k, v, bg, t2, st, vnew, do, dvnew, dkd, ddl, qkv_pre, ba, conv_w8, alog_row, dtb_row):
    s = q.shape[0]
    c = DN_CHUNK
    n_pairs = s // PAIR

    def body(q_ref, k_ref, v_ref, bg_ref, t2_ref, st_ref, vnew_ref, do_ref, dvnew_ref, dkd_ref, ddl_ref,
             pre_ref, halo_ref, cw_ref, ba_ref, al_ref, dtb_ref,
             du_ref, dba_ref, dsmall_ref, dcw_ref, dq_ref, dk_ref, dv_ref, dbg_ref, ext_ref, extd_ref, carry_ref):
        step = pl.program_id(0)

        @pl.when(step == 0)
        def _():
            carry_ref[...] = jnp.zeros_like(carry_ref)
            dcw_ref[...] = jnp.zeros_like(dcw_ref)

        masks = _pair_masks()
        first = masks["first"][None]
        dec = _pair_decay(bg_ref[...], masks)
        st_ = lambda xs: jnp.stack(xs, axis=0)
        heads = range(DN_HEADS)
        qm_, km_, vm_, dom = (st_([r[:, _head_cols(h)] for h in heads]) for r in (q_ref, k_ref, v_ref, do_ref))
        beta = st_([dec[h][0] for h in heads])
        gcb = st_([dec[h][1] for h in heads])
        gam = st_([dec[h][2] for h in heads])
        gl = st_([dec[h][3] for h in heads])
        pair = lambda ref: jnp.concatenate([ref[0], ref[1]], axis=1)
        vnew2, dvnew2, dkd2 = pair(vnew_ref), pair(dvnew_ref), pair(dkd_ref)
        halves = lambda x: (x[:, :c, :], x[:, c:, :])
        by_state = lambda x: jnp.concatenate([_bmm_nt(xh, st_ref[i]) for i, xh in enumerate(halves(x))], axis=1)
        dqd = by_state(dom)
        dw = -by_state(dvnew2)
        ddl2 = jnp.where(first, ddl_ref[0][:, 0:1, :], ddl_ref[1][:, 0:1, :])

        eg = jnp.exp(gcb)
        egl = jnp.exp(gl - gcb)
        dl = jnp.exp(gl)
        kb = km_ * beta
        kk = _bmm_nt(kb, km_)
        a = jnp.where(masks["strict"][None], kk * gam, 0.0)
        t = t2_ref[0]
        vb = vm_ * beta
        kbe = kb * eg
        u = _bmm(t, vb)
        w = _bmm(t, kbe)
        aqk = _bmm_nt(qm_, km_) * gam
        qd = qm_ * eg
        kd = km_ * egl

        daqk = jnp.where(masks["causal"][None], _bmm_nt(dom, vnew2), 0.0)
        dvb = _bmm_tn(t, dvnew2)
        dkbe = _bmm_tn(t, dw)
        da = jnp.where(masks["strict"][None], -(_bmm_nt(dvb, u) + _bmm_nt(dkbe, w)), 0.0)
        pm = da * gam
        qmm = daqk * gam
        dkb = _bmm(pm, km_) + dkbe * eg
        dkh = _bmm_tn(pm, kb) + _bmm_tn(qmm, qm_) + dkd2 * egl + dkb * beta
        dqh = _bmm(qmm, km_) + dqd * eg
        xm = da * a + daqk * aqk
        col_rows = jnp.concatenate([jnp.zeros((DN_HEADS, PAIR), F32), jnp.sum(xm, axis=1),
                                    jnp.zeros((PAIR - 2 * DN_HEADS, PAIR), F32)], axis=0)
        tmp = jnp.sum(dkd2 * kd, axis=-1, keepdims=True)
        dgc = (jnp.sum(xm, axis=-1, keepdims=True) + jnp.sum(dkbe * kbe, axis=-1, keepdims=True)
               + jnp.sum(dqd * qd, axis=-1, keepdims=True) - tmp)
        sum0 = jnp.sum(jnp.where(first, tmp, 0.0), axis=1, keepdims=True)
        sum1 = jnp.sum(jnp.where(first, 0.0, tmp), axis=1, keepdims=True)
        dgl = jnp.where(first, sum0, sum1) + ddl2 * dl
        last = (masks["row"] == c - 1) | (masks["row"] == PAIR - 1)
        dgc = dgc + jnp.where(last[None], dgl, 0.0)
        dbeta = jnp.sum(dvb * vm_, axis=-1, keepdims=True) + jnp.sum(dkb * km_, axis=-1, keepdims=True)
        dvh = dvb * beta

        lane = masks["lane"]
        dgc_lanes = jnp.zeros((PAIR, LANES), F32)
        dbg = jnp.zeros((PAIR, LANES), F32)
        for h in heads:
            dq_ref[:, _head_cols(h)] = dqh[h]
            dk_ref[:, _head_cols(h)] = dkh[h]
            dv_ref[:, _head_cols(h)] = dvh[h]
            dgc_lanes = dgc_lanes + jnp.where(lane == DN_HEADS + h, dgc[h], 0.0)
            dbg = dbg + jnp.where(lane == h, dbeta[h], 0.0)
        dbg_ref[...] = dbg + _dot01(masks["upper"].astype(F32), dgc_lanes - col_rows.T)
        small = _dn_prep_bwd_tile(pre_ref, halo_ref, cw_ref, ba_ref, al_ref, dtb_ref, dq_ref, dk_ref, dv_ref, dbg_ref,
                                  extd_ref.at[pl.ds(0, PAIR)], dba_ref, ext_ref, step == n_pairs - 1)
        _acc_add(dsmall_ref, small)
        extd_ref[PAIR:, :] = carry_ref[...]
        _conv_bwd_tile(extd_ref, ext_ref, cw_ref, du_ref, dcw_ref, PAIR)
        carry_ref[...] = extd_ref[0:SUBLANES, :]

    rev = lambda i: n_pairs - 1 - i
    row_spec = lambda w_: pl.BlockSpec((PAIR, w_), lambda i: (rev(i), 0))
    hm = lambda a_, b_: pl.BlockSpec((2, DN_HEADS, a_, b_), lambda i: (rev(i), 0, 0, 0))
    whole = lambda t: pl.BlockSpec(t.shape, lambda i: (0, 0))
    halo = pl.BlockSpec((SUBLANES, QKV_W), lambda i: (jnp.maximum(rev(i) * (PAIR // SUBLANES) - 1, 0), 0))
    return _pcall(
        body, name="delta_post_bwd", grid=(n_pairs,),
        in_specs=[row_spec(D_MODEL)] * 3
        + [row_spec(LANES), pl.BlockSpec((1, DN_HEADS, PAIR, PAIR), lambda i: (rev(i), 0, 0, 0)),
           hm(DN_DK, DN_DK), hm(c, LANES), row_spec(D_MODEL), hm(c, LANES), hm(c, LANES), hm(SUBLANES, LANES),
           row_spec(QKV_W), halo, whole(conv_w8), row_spec(LANES), whole(alog_row), whole(dtb_row)],
        out_specs=[row_spec(QKV_W), row_spec(LANES), pl.BlockSpec((SUBLANES, LANES), lambda i: (0, 0)),
                   pl.BlockSpec((SUBLANES, QKV_W), lambda i: (0, 0))],
        out_shape=[jax.ShapeDtypeStruct((s, QKV_W), MXU), jax.ShapeDtypeStruct((s, LANES), MXU),
                   jax.ShapeDtypeStruct((SUBLANES, LANES), F32), jax.ShapeDtypeStruct((SUBLANES, QKV_W), F32)],
        scratch_shapes=[pltpu.VMEM((PAIR, D_MODEL), F32)] * 3
        + [pltpu.VMEM((PAIR, LANES), F32), pltpu.VMEM((PAIR + SUBLANES, QKV_W), F32),
           pltpu.VMEM((PAIR + SUBLANES, QKV_W), F32), pltpu.VMEM((SUBLANES, QKV_W), F32)],
        compiler_params=_params("arbitrary"),
    )(q, k, v, bg, t2, st, vnew, do, dvnew, dkd, ddl, qkv_pre, qkv_pre, conv_w8, ba, alog_row, dtb_row)


def _alibi_slope(group, head):
    n = N_DIL * DIL_HEADS
    return float(2.0 ** (-8.0 * (group * DIL_HEADS + head + 1) / n))


def _attn_plan(s, group):
    window, dil = DIL_GROUPS[group]
    assert window // dil == ATT_BLOCK
    assert (s // dil) % ATT_BLOCK == 0, "sub-sequence length must be a whole number of attention blocks"
    return dil, s // dil // ATT_BLOCK, (DIL_HEADS if dil == 1 else 1)


def _attn_specs(group, dil, nb, hp):
    rows = ATT_BLOCK * dil

    def spec(col0, shift):
        if shift < 0:
            f = lambda hb, n: (jnp.maximum(n - 1, 0), col0 + hb)
        elif shift > 0:
            f = lambda hb, n: (jnp.minimum(n + 1, nb - 1), col0 + hb)
        else:
            f = lambda hb, n: (jnp.minimum(n, nb - 1), col0 + hb)
        return pl.BlockSpec((rows, hp * LANES), f)

    return (lambda shift: spec(group * (DIL_HEADS // hp), shift)), (lambda shift: spec(0, shift))


def _sub_rows(ref, r, dil, cols):
    return ref[:, cols] if dil == 1 else ref[pl.ds(r, ATT_BLOCK, stride=dil), cols]


def _set_sub_rows(ref, r, dil, cols, value):
    if dil == 1:
        ref[:, cols] = value
    else:
        ref[pl.ds(r, ATT_BLOCK, stride=dil), cols] = value


def _step_slope(group, hp, hh):
    if hp == DIL_HEADS:
        return _alibi_slope(group, hh)
    hb = pl.program_id(0)
    slope = _alibi_slope(group, DIL_HEADS - 1)
    for h in reversed(range(DIL_HEADS - 1)):
        slope = jnp.where(hb == h, _alibi_slope(group, h), slope)
    return slope


def _attn_items(hp, dil):
    return [(hh, r) for hh in range(hp) for r in range(dil)]


def _attn_stack(ref, items, dil, dtype=MXU):
    return jnp.stack([_sub_rows(ref, r, dil, _head_cols(hh)).astype(dtype) for hh, r in items], axis=0)


def _attn_slopes(group, hp, items):
    if hp == 1:
        return _step_slope(group, hp, 0)
    return jnp.stack([jnp.full((1, 1), _alibi_slope(group, hh), F32) for hh, _ in items], axis=0)


def _window_bias(dil, n):
    a = lax.broadcasted_iota(jnp.int32, (ATT_BLOCK, 2 * ATT_BLOCK), 0)
    b = lax.broadcasted_iota(jnp.int32, (ATT_BLOCK, 2 * ATT_BLOCK), 1)
    dist = ATT_BLOCK + a - b
    valid = (dist >= 0) & (dist <= ATT_BLOCK) & ((b >= ATT_BLOCK) | (n > 0))
    return (dist * dil).astype(F32), valid


def _attn_fwd(qb, kb, vb, group):
    s = qb.shape[0]
    dil, nb, hp = _attn_plan(s, group)
    qkv, per_head = _attn_specs(group, dil, nb, hp)

    def body(q_ref, kp_ref, kc_ref, vp_ref, vc_ref, o_ref, lse_ref):
        n = pl.program_id(1)
        distd, valid = _window_bias(dil, n)
        items = _attn_items(hp, dil)
        sub = lambda ref: _attn_stack(ref, items, dil)
        kk = jnp.concatenate([sub(kp_ref), sub(kc_ref)], axis=1)
        vv = jnp.concatenate([sub(vp_ref), sub(vc_ref)], axis=1)
        sc = _bmm_nt(sub(q_ref), kk) * DIL_DH ** -0.5 - _attn_slopes(group, hp, items) * distd
        sc = jnp.where(valid, sc, -1e30)
        mx = jnp.max(sc, axis=-1, keepdims=True)
        p = jnp.where(valid, jnp.exp(sc - mx), 0.0)
        den = jnp.sum(p, axis=-1, keepdims=True)
        out = _bmm(p, vv) / den
        lse = mx + jnp.log(den)
        for b, (hh, r) in enumerate(items):
            _set_sub_rows(o_ref, r, dil, _head_cols(hh), out[b])
            _set_sub_rows(lse_ref, r, dil, _head_cols(hh), jnp.broadcast_to(lse[b], (ATT_BLOCK, LANES)))

    return _pcall(
        body, name=f"attn_fwd_g{group}", grid=(DIL_HEADS // hp, nb),
        in_specs=[qkv(0), qkv(-1), qkv(0), qkv(-1), qkv(0)], out_specs=[per_head(0)] * 2,
        out_shape=[jax.ShapeDtypeStruct((s, DIL_W), F32)] * 2,
        compiler_params=_params("parallel", "parallel"),
    )(qb, kb, kb, vb, vb)


def _attn_bwd(qb, kb, vb, d_o, lse, delta, group):
    s = qb.shape[0]
    dil, nb, hp = _attn_plan(s, group)
    qkv, per_head = _attn_specs(group, dil, nb, hp)
    scale = DIL_DH ** -0.5

    def body(q_ref, kp_ref, kc_ref, vp_ref, vc_ref, do_ref, l_ref, dl_ref, dq_ref, dk_ref, dv_ref,
             dq_acc, dk_done, dv_done, dk_carry, dv_carry):
        n = pl.program_id(1)
        items = _attn_items(hp, dil)
        slopes = _attn_slopes(group, hp, items)

        @pl.when(n == 0)
        def _():
            dk_carry[...] = jnp.zeros_like(dk_carry)
            dv_carry[...] = jnp.zeros_like(dv_carry)

        @pl.when(n < nb)
        def _():
            distd, valid = _window_bias(dil, n)
            sub = lambda ref, dtype=MXU: _attn_stack(ref, items, dil, dtype)
            qc, do = sub(q_ref), sub(do_ref)
            kk = jnp.concatenate([sub(kp_ref), sub(kc_ref)], axis=1)
            vv = jnp.concatenate([sub(vp_ref), sub(vc_ref)], axis=1)
            sc = _bmm_nt(qc, kk) * scale - slopes * distd
            p = jnp.where(valid, jnp.exp(jnp.minimum(sc - jnp.concatenate([sub(l_ref, F32)] * 2, axis=2), 0.0)), 0.0)
            dsc = p * (_bmm_nt(do, vv) - jnp.concatenate([sub(dl_ref, F32)] * 2, axis=2))
            dq = _bmm(dsc, kk) * scale
            dkk = _bmm_tn(dsc, qc) * scale
            dvv = _bmm_tn(p, do)
            for b, (hh, r) in enumerate(items):
                cols = _head_cols(hh)
                _set_sub_rows(dq_acc, r, dil, cols, dq[b])
                _set_sub_rows(dk_done, r, dil, cols, _sub_rows(dk_carry, r, dil, cols) + dkk[b, :ATT_BLOCK])
                _set_sub_rows(dv_done, r, dil, cols, _sub_rows(dv_carry, r, dil, cols) + dvv[b, :ATT_BLOCK])
                _set_sub_rows(dk_carry, r, dil, cols, dkk[b, ATT_BLOCK:])
                _set_sub_rows(dv_carry, r, dil, cols, dvv[b, ATT_BLOCK:])
            dq_ref[...] = dq_acc[...].astype(dq_ref.dtype)
            dk_ref[...] = dk_done[...].astype(dk_ref.dtype)
            dv_ref[...] = dv_done[...].astype(dv_ref.dtype)

        @pl.when(n == nb)
        def _():
            dk_ref[...] = dk_carry[...].astype(dk_ref.dtype)
            dv_ref[...] = dv_carry[...].astype(dv_ref.dtype)

    return _pcall(
        body, name=f"attn_bwd_g{group}", grid=(DIL_HEADS // hp, nb + 1),
        in_specs=[qkv(0), qkv(-1), qkv(0), qkv(-1), qkv(0)] + [per_head(0)] * 3,
        out_specs=[per_head(0), per_head(-1), per_head(-1)],
        out_shape=[jax.ShapeDtypeStruct((s, DIL_W), MXU)] * 3,
        scratch_shapes=[pltpu.VMEM((ATT_BLOCK * dil, hp * LANES), F32)] * 5,
        compiler_params=_params("parallel", "arbitrary"),
    )(qb, kb, kb, vb, vb, d_o, lse, delta)


def _my_place():
    mx, my, mc = lax.axis_index("x"), lax.axis_index("y"), lax.axis_index("c")
    return mx, my, mc, 4 * mx + 2 * my + mc


N_CHIPS = 4


def _shard_row_tile(r):
    if r <= 512:
        return r
    return 128 if r % 128 == 0 else 480


def _other_chips(mx, my):
    return [(1 - mx, my), (mx, 1 - my), (1 - mx, 1 - my)]


def _all_gather(xs, name):
    n = len(xs)
    halved = [x.shape[1] % (2 * LANES) == 0 and x.size * x.dtype.itemsize >= (1 << 20) for x in xs]
    n_sems = 8

    def body(*refs):
        x_refs, o_refs = refs[:n], refs[n:2 * n]
        send_sems, recv_sems, local_sems = refs[2 * n:]
        mx, my, mc, me = _my_place()
        sibling, sibling_id = (mx, my, 1 - mc), 4 * mx + 2 * my + (1 - mc)
        x_nbr, y_nbr, diag = _other_chips(mx, my)
        slot_of = lambda chip, c: 4 * chip[0] + 2 * chip[1] + c

        def part(ref, a, half):
            if not halved[a]:
                return ref
            width = xs[a].shape[1] // 2
            return ref.at[:, pl.ds(half * width, width)]

        def copy(a, k, dst, to, src=None):
            return pltpu.make_async_remote_copy(
                src_ref=dst if src is None else src, dst_ref=dst, send_sem=send_sems.at[a, k],
                recv_sem=recv_sems.at[a, k], device_id=to, device_id_type=MESH)

        local = [pltpu.make_async_copy(x_refs[a], o_refs[a].at[me], local_sems.at[a]) for a in range(n)]
        for cp in local:
            cp.start()
        sends = []
        for a in range(n):
            mine = o_refs[a].at[me]
            sends += [copy(a, 0, mine, sibling, src=x_refs[a]), copy(a, 1, mine, (*x_nbr, mc), src=x_refs[a]),
                      copy(a, 2, mine, (*y_nbr, mc), src=x_refs[a])]
        for cp in sends:
            cp.start()
        for a in range(n):
            blk = o_refs[a].at[slot_of(x_nbr, mc)]
            copy(a, 1, blk, (*x_nbr, mc)).wait_recv()
            sends += [copy(a, 3, blk, sibling), copy(a, 5, part(blk, a, 0), (*y_nbr, mc))]
            sends[-2].start()
            sends[-1].start()
        for a in range(n):
            blk = o_refs[a].at[slot_of(y_nbr, mc)]
            copy(a, 2, blk, (*y_nbr, mc)).wait_recv()
            sends.append(copy(a, 4, blk, sibling))
            sends[-1].start()
            if halved[a]:
                sends.append(copy(a, 6, part(blk, a, 1), (*x_nbr, mc)))
                sends[-1].start()
        for a in range(n):
            blk = o_refs[a].at[slot_of(diag, mc)]
            copy(a, 5, part(blk, a, 0), (*y_nbr, mc)).wait_recv()
            if halved[a]:
                copy(a, 6, part(blk, a, 1), (*x_nbr, mc)).wait_recv()
            sends.append(copy(a, 7, blk, sibling))
            sends[-1].start()
        for a in range(n):
            copy(a, 0, o_refs[a].at[sibling_id], sibling).wait_recv()
            for k, chip in ((3, x_nbr), (4, y_nbr), (7, diag)):
                copy(a, k, o_refs[a].at[slot_of(chip, 1 - mc)], sibling).wait_recv()
        for cp in sends:
            cp.wait_send()
        for cp in local:
            cp.wait()

    any_spec = pl.BlockSpec(memory_space=pl.ANY)
    return _pcall(
        body, name=name,
        in_specs=[any_spec] * n, out_specs=[any_spec] * n,
        out_shape=[jax.ShapeDtypeStruct((N_DEV,) + x.shape, x.dtype) for x in xs],
        scratch_shapes=[pltpu.SemaphoreType.DMA((n, n_sems)), pltpu.SemaphoreType.DMA((n, n_sems)),
                        pltpu.SemaphoreType.DMA((n,))],
    )(*xs)


def _pair_exchange(gs, name):
    n = len(gs)

    def body(*refs):
        g_refs, o_refs = refs[:n], refs[n:2 * n]
        send_sems, recv_sems = refs[2 * n:]
        mx, my, mc, _ = _my_place()
        copies = [pltpu.make_async_remote_copy(
            src_ref=g_refs[a].at[p, 1 - mc], dst_ref=o_refs[a].at[p], send_sem=send_sems.at[a, p],
            recv_sem=recv_sems.at[a, p], device_id=(mx, my, 1 - mc), device_id_type=MESH)
            for a in range(n) for p in range(N_CHIPS)]
        for cp in copies:
            cp.start()
        for cp in copies:
            cp.wait()

    any_spec = pl.BlockSpec(memory_space=pl.ANY)
    return _pcall(
        body, name=name,
        in_specs=[any_spec] * n, out_specs=[any_spec] * n,
        out_shape=[jax.ShapeDtypeStruct((N_CHIPS,) + g.shape[2:], g.dtype) for g in gs],
        scratch_shapes=[pltpu.SemaphoreType.DMA((n, N_CHIPS)), pltpu.SemaphoreType.DMA((n, N_CHIPS))],
    )(*gs)


def _pair_add(g, other, name):
    chips, _, r, c = g.shape
    tr = _shard_row_tile(r)
    core = lax.axis_index("c").astype(jnp.int32).reshape(1)

    def body(core_ref, g_ref, o_ref, h_ref):
        h_ref[...] = (g_ref[...].astype(F32)[0] + o_ref[...].astype(F32)).astype(h_ref.dtype)

    blk = pl.BlockSpec((1, tr, c), lambda p, i, core_ref: (p, i, 0))
    return _pcall(
        body, name=name,
        grid_spec=pltpu.PrefetchScalarGridSpec(
            num_scalar_prefetch=1, grid=(chips, pl.cdiv(r, tr)),
            in_specs=[pl.BlockSpec((1, 1, tr, c), lambda p, i, core_ref: (p, core_ref[0], i, 0)), blk],
            out_specs=blk),
        out_shape=jax.ShapeDtypeStruct((chips, r, c), g.dtype),
        compiler_params=_params("parallel", "parallel"),
    )(core, g, other)


def _chip_exchange(hs, name):
    n = len(hs)

    def body(*refs):
        h_refs, o_refs = refs[:n], refs[n:2 * n]
        send_sems, recv_sems, local_sems = refs[2 * n:]
        mx, my, mc, _ = _my_place()
        my_chip = 2 * mx + my
        chips = _other_chips(mx, my)
        local = [pltpu.make_async_copy(h_refs[a].at[my_chip], o_refs[a].at[my_chip], local_sems.at[a]) for a in range(n)]
        for cp in local:
            cp.start()
        for j, (px, py) in enumerate(chips):
            for a in range(n):
                pltpu.make_async_remote_copy(
                    src_ref=h_refs[a].at[2 * px + py], dst_ref=o_refs[a].at[my_chip], send_sem=send_sems.at[a, j],
                    recv_sem=recv_sems.at[a, j], device_id=(px, py, mc), device_id_type=MESH).start()
        for j, (px, py) in enumerate(chips):
            for a in range(n):
                pltpu.make_async_remote_copy(
                    src_ref=h_refs[a].at[2 * px + py], dst_ref=o_refs[a].at[2 * px + py], send_sem=send_sems.at[a, j],
                    recv_sem=recv_sems.at[a, j], device_id=(px, py, mc), device_id_type=MESH).wait()
        for cp in local:
            cp.wait()

    any_spec = pl.BlockSpec(memory_space=pl.ANY)
    return _pcall(
        body, name=name,
        in_specs=[any_spec] * n, out_specs=[any_spec] * n,
        out_shape=[jax.ShapeDtypeStruct(h.shape, h.dtype) for h in hs],
        scratch_shapes=[pltpu.SemaphoreType.DMA((n, N_CHIPS - 1)), pltpu.SemaphoreType.DMA((n, N_CHIPS - 1)),
                        pltpu.SemaphoreType.DMA((n,))],
    )(*hs)


def _adamw(parts, w, m, v, name):
    r, c = w.shape
    n_parts = parts.shape[0]
    tr = _shard_row_tile(r)
    bc1 = 1.0 - ADAM_B1 ** ADAM_STEP
    bc2 = 1.0 - ADAM_B2 ** ADAM_STEP

    def body(p_ref, w_ref, m_ref, v_ref, g_ref, d_ref, nm_ref, nv_ref):
        g = p_ref[0].astype(F32)
        for j in range(1, n_parts):
            g = g + p_ref[j].astype(F32)
        nm = ADAM_B1 * m_ref[...] + (1.0 - ADAM_B1) * g
        nv = ADAM_B2 * v_ref[...] + (1.0 - ADAM_B2) * (g * g)
        g_ref[...] = g
        nm_ref[...] = nm
        nv_ref[...] = nv
        d_ref[...] = -ADAM_LR * ((nm / bc1) / (jnp.sqrt(nv / bc2) + ADAM_EPS) + ADAM_WD * w_ref[...])

    blk = pl.BlockSpec((tr, c), lambda i: (i, 0))
    return _pcall(
        body, name=name, grid=(pl.cdiv(r, tr),),
        in_specs=[pl.BlockSpec((n_parts, tr, c), lambda i: (0, i, 0)), blk, blk, blk],
        out_specs=[blk] * 4, out_shape=[jax.ShapeDtypeStruct((r, c), F32)] * 4,
        compiler_params=_params("parallel"),
    )(parts, w, m, v)


def _local_step(x, target, norm_w, w_segs, conv_w, a_log, dt_bias, dn_norm_w, w_o_dn, w_o_dil, w_out, final_norm_w):
    s = x.shape[0]
    w_qkv, w_za, w_ba, w_qb, w_kb, w_vb, w_zb, w_ga, w_gb = w_segs
    conv_w8 = jnp.concatenate([conv_w, jnp.zeros((SUBLANES - conv_w.shape[0], QKV_W), F32)], axis=0)
    pad8 = jnp.zeros((1, DN_HEADS), F32)
    alog_row = jnp.concatenate([pad8, a_log, jnp.zeros((1, LANES - 2 * DN_HEADS), F32)], axis=1)
    dtb_row = jnp.concatenate([pad8, dt_bias, jnp.zeros((1, LANES - 2 * DN_HEADS), F32)], axis=1)
    wf_row = final_norm_w.reshape(1, D_MODEL)

    hb, qkv_pre, z_a, ba, z_b = _rms_proj_fwd(x, norm_w, [w_qkv, w_za, w_ba, w_zb], "rms_proj_fwd_a")
    q_b, k_b, v_b, g_a, g_b = _mm_out(hb, [w_qb, w_kb, w_vb, w_ga, w_gb], "proj_fwd_b", w_is_out_by_in=True)

    u_d, w_d, qd_d, kd_d, aqk_d, dl_d, t2_d, qn, kn, vn, bg = _delta_prep(qkv_pre, ba, conv_w8, alog_row, dtb_row)
    parts, lses = [], []
    for gi in range(N_DIL):
        o_g, l_g = _attn_fwd(q_b, k_b, v_b, gi)
        parts.append(o_g)
        lses.append(l_g)
    o_a, vnew_d, st_d, on_b, y_a, lse, o_joint, ob_b, y_b = _delta_scan_fwd(
        u_d, w_d, qd_d, kd_d, aqk_d, dl_d, z_a, dn_norm_w, w_o_dn, parts, lses, z_b, w_o_dil)

    loss8, dwf8, merged_b, dx2_b, dx2, dya_b, dyb_b, dga_b, dgb_b = _merge_out_final(
        g_a, g_b, y_a, y_b, x, target, w_out, wf_row)

    g_w_out = _mm_tn(merged_b, dx2_b, "out_wgrad")
    g_w_o_dn = _mm_tn(on_b, dya_b, "out_dn_wgrad")

    g_w_o_dil = _mm_tn(ob_b, dyb_b, "out_dil_wgrad")
    dvnew_d, dkd_d, ddl_d, d_o_a, dza_b, ddnw8, d_o, dzb_b, delta = _delta_scan_bwd(
        w_d, qd_d, kd_d, aqk_d, dl_d, vnew_d, st_d, dya_b, o_a, z_a, dn_norm_w, w_o_dn, dyb_b, o_joint, z_b, w_o_dil)
    dqs, dks, dvs = [], [], []
    for gi in range(N_DIL):
        dq_g, dk_g, dv_g = _attn_bwd(q_b, k_b, v_b, d_o, lse, delta, gi)
        dqs.append(dq_g)
        dks.append(dk_g)
        dvs.append(dv_g)

    dqkv_b, dba_b, dsmall8, dconv8 = _delta_post_bwd(qn, kn, vn, bg, t2_d, st_d, vnew_d, d_o_a, dvnew_d, dkd_d, ddl_d,
                                                     qkv_pre, ba, conv_w8, alog_row, dtb_row)

    per_group = lambda w: [w[g * DIL_W:(g + 1) * DIL_W] for g in range(N_DIL)]
    dh_b = _mm_in(dqs + dks + dvs + [dga_b, dgb_b],
                  per_group(w_qb) + per_group(w_kb) + per_group(w_vb) + [w_ga, w_gb], "proj_bwd_b", w_is_out_by_in=True)
    dsegs = [dqkv_b, dza_b, dba_b] + dqs + dks + dvs + [dzb_b, dga_b, dgb_b]
    valid_rows = [d.shape[1] for d in dsegs]
    valid_rows[2] = 2 * DN_HEADS
    g_wt = _proj_wgrad_all(dsegs, valid_rows, hb)
    grad_x, dnw8 = _proj_bwd_rms_in([dqkv_b, dza_b, dba_b, dzb_b], [w_qkv, w_za, w_ba, w_zb], dh_b, x, dx2, norm_w)

    small = dict(norm_w=dnw8[0:1], final_norm_w=dwf8[0:1], dn_norm_w=ddnw8[0:1],
                 a_log=dsmall8[0:1, DN_HEADS:2 * DN_HEADS], dt_bias=dsmall8[1:2, DN_HEADS:2 * DN_HEADS])
    return loss8[0:1, 0:1], grad_x, g_wt, dconv8[0:4], g_w_o_dn, g_w_o_dil, g_w_out, small


def _proj_bwd_rms_in(ds, ws, dh_a, x, dx2, norm_w):
    n_seg = len(ds)

    def body(*refs):
        d_refs, w_refs = refs[:n_seg], refs[n_seg:2 * n_seg]
        da_ref, x_ref, dx2_ref, w_ref, dx_ref, dw_ref = refs[2 * n_seg:]
        dx_ref[...] = da_ref[...]
        for d_ref, wt_ref in zip(d_refs, w_refs):
            for c, wd in _col_chunks(d_ref.shape[1], 1024):
                dx_ref[...] += jnp.dot(d_ref[:, c:c + wd], wt_ref[c:c + wd, :], preferred_element_type=F32)
        xv = x_ref[...]
        r = lax.rsqrt(jnp.mean(xv * xv, axis=-1, keepdims=True) + NORM_EPS)
        dhv = dx_ref[...]
        dn = dhv * w_ref[...]
        dx_ref[...] = dx2_ref[...] + r * dn - xv * (r * r * r) * jnp.mean(dn * xv, axis=-1, keepdims=True)
        row = jnp.sum(dhv * xv * r, axis=0, keepdims=True)
        _acc_add(dw_ref, jnp.concatenate([row, jnp.zeros((SUBLANES - 1, row.shape[1]), F32)], axis=0))

    return _rows_call(body, "proj_bwd_b_rms_in", x.shape[0],
                      [(d, "tile") for d in ds] + [(w, "full") for w in ws]
                      + [(dh_a, "tile"), (x, "tile"), (dx2, "tile"), (norm_w, "full")],
                      [(x.shape, F32, "tile"), ((SUBLANES, x.shape[1]), F32, "acc")])


def _split_proj_rows(w_shards):
    n_shards, rows, k = w_shards.shape
    wt_full = w_shards.reshape(n_shards * rows, k)
    offs = [0]
    for n in PROJ_SIZES:
        offs.append(offs[-1] + n)
    seg = lambda a, b: wt_full[offs[a]:offs[b]]
    w_ba = jnp.concatenate([seg(4, 6), jnp.zeros((LANES - 2 * DN_HEADS, k), wt_full.dtype)], axis=0)
    return [seg(0, 3), seg(3, 4), w_ba, seg(6, 7), seg(7, 8), seg(8, 9), seg(9, 10), seg(10, 11), seg(11, 12)]


LOSS_ROW = 5


def _pack_small(norm_w, final_norm_w, dn_norm_w, a_log, dt_bias, loss=None):
    pad = lambda r: jnp.concatenate([r, jnp.zeros((1, D_MODEL - r.shape[1]), F32)], axis=1)
    rows = [pad(norm_w.reshape(1, -1)), pad(final_norm_w.reshape(1, -1)), pad(dn_norm_w.reshape(1, -1)),
            pad(a_log.reshape(1, -1)), pad(dt_bias.reshape(1, -1)),
            pad(jnp.zeros((1, 1), F32) if loss is None else loss.reshape(1, 1)),
            jnp.zeros((SUBLANES - LOSS_ROW - 1, D_MODEL), F32)]
    return jnp.concatenate(rows, axis=0)


def _unpack_small(p):
    return dict(norm_w=p[0:1], final_norm_w=p[1], dn_norm_w=p[2:3, :DN_DK], a_log=p[3:4, :DN_HEADS],
                dt_bias=p[4:5, :DN_HEADS])


def kernel(x, norm_w, w_in, conv_w, a_log, dt_bias, dn_norm_w, w_o_dn, w_o_dil, w_out, final_norm_w, loss_target, m_norm_w, m_w_in, m_conv_w, m_a_log, m_dt_bias, m_dn_norm_w, m_w_o_dn, m_w_o_dil, m_w_out, m_final_norm_w, v_norm_w, v_w_in, v_conv_w, v_a_log, v_dt_bias, v_dn_norm_w, v_w_o_dn, v_w_o_dil, v_w_out, v_final_norm_w):
    shard_w = w_in.shape[2]
    wt, m_wt, v_wt = (jnp.transpose(t[0]) for t in (w_in, m_w_in, v_w_in))
    gathered = _all_gather([wt.astype(MXU), w_o_dn[0].astype(MXU), w_o_dil[0].astype(MXU), w_out[0].astype(MXU),
                            conv_w[0]], "gather_weights")
    w_in_all, w_o_dn_all, w_o_dil_all, w_out_all, conv_all = gathered
    w_o_dn_full = w_o_dn_all.reshape(D_MODEL, D_MODEL)
    w_o_dil_full = jnp.transpose(w_o_dil_all, (1, 0, 2)).reshape(DIL_W, D_MODEL)
    w_out_full = w_out_all.reshape(D_MODEL, D_MODEL)
    conv_full = jnp.transpose(conv_all, (1, 0, 2)).reshape(conv_w.shape[1], QKV_W)

    loss11, grad_x, g_wt, g_conv, g_w_o_dn, g_w_o_dil, g_w_out, small = _local_step(
        x[0], loss_target[0], norm_w, _split_proj_rows(w_in_all), conv_full, a_log, dt_bias, dn_norm_w,
        w_o_dn_full, w_o_dil_full, w_out_full, final_norm_w)

    col_shards = lambda g, n: jnp.transpose(g.reshape(g.shape[0], N_DEV, n), (1, 0, 2))
    row_shards = lambda g: g.reshape(N_DEV, g.shape[0] // N_DEV, g.shape[1])
    g_wt_shards = jnp.stack([g_wt[j * shard_w:(j + 1) * shard_w] for j in range(N_DEV)], axis=0)
    sent = [g_wt_shards, row_shards(g_w_o_dn).astype(MXU),
            col_shards(g_w_o_dil, w_o_dil.shape[2]).astype(MXU), row_shards(g_w_out).astype(MXU),
            col_shards(g_conv, conv_w.shape[2])]
    sent = [g8.reshape((N_CHIPS, 2) + g8.shape[1:]) for g8 in sent]
    from_sibling = _pair_exchange(sent, "scatter_pair")
    summed = [_pair_add(g, o, f"pair_add_{i}") for i, (g, o) in enumerate(zip(sent, from_sibling))]
    p_w_in, p_w_o_dn, p_w_o_dil, p_w_out, p_conv = _chip_exchange(summed, "scatter_chips")
    p_small = _all_gather([_pack_small(small["norm_w"], small["final_norm_w"], small["dn_norm_w"], small["a_log"],
                                       small["dt_bias"], loss11)], "gather_small_grads")[0]

    res = {}
    res["w_in"] = [jnp.transpose(t) for t in _adamw(p_w_in, wt, m_wt, v_wt, "adamw_w_in")]
    res["conv_w"] = _adamw(p_conv, conv_w[0], m_conv_w[0], v_conv_w[0], "adamw_conv_w")
    res["w_o_dn"] = _adamw(p_w_o_dn, w_o_dn[0], m_w_o_dn[0], v_w_o_dn[0], "adamw_w_o_dn")
    res["w_o_dil"] = _adamw(p_w_o_dil, w_o_dil[0], m_w_o_dil[0], v_w_o_dil[0], "adamw_w_o_dil")
    res["w_out"] = _adamw(p_w_out, w_out[0], m_w_out[0], v_w_out[0], "adamw_w_out")
    small_res = _adamw(p_small, _pack_small(norm_w, final_norm_w, dn_norm_w, a_log, dt_bias),
                       _pack_small(m_norm_w, m_final_norm_w, m_dn_norm_w, m_a_log, m_dt_bias),
                       _pack_small(v_norm_w, v_final_norm_w, v_dn_norm_w, v_a_log, v_dt_bias), "adamw_small")
    loss = small_res[0][LOSS_ROW, 0]
    small_res = [_unpack_small(t) for t in small_res]

    names = ["norm_w", "w_in", "conv_w", "a_log", "dt_bias", "dn_norm_w", "w_o_dn", "w_o_dil", "w_out", "final_norm_w"]
    outs = [loss, grad_x[None]]
    for kind in range(4):
        for nm in names:
            outs.append(res[nm][kind][None] if nm in res else small_res[kind][nm])
    return tuple(outs)
```

```python
import math

import jax
import jax.numpy as jnp
from jax import lax
from jax.experimental import pallas as pl
from jax.experimental.pallas import tpu as pltpu

F32 = jnp.float32
MXU = jnp.bfloat16
MESH = pl.DeviceIdType.MESH

N_DEV = 8
D_MODEL = 1024
DN_HEADS = 8
DN_DK = 128
DN_CHUNK = 64
N_DIL = 3
DIL_HEADS = 4
DIL_DH = 128
DIL_W = DIL_HEADS * DIL_DH
DIL_GROUPS = ((128, 1), (512, 4), (2048, 16))
ATT_BLOCK = 128
NORM_EPS = 1e-6
QKV_W = 3 * D_MODEL
DILQ_W = N_DIL * DIL_W
PROJ_SIZES = (1024, 1024, 1024, 1024, 8, 8, DILQ_W, DILQ_W, DILQ_W, DIL_W, D_MODEL, D_MODEL)

ADAM_LR = 0.001
ADAM_B1 = 0.9
ADAM_B2 = 0.999
ADAM_EPS = 1e-08
ADAM_WD = 0.01
ADAM_STEP = 10

ROW_TILE = 256
LANES = 128
SUBLANES = 8
VMEM_LIMIT = 48 << 20


def _pcall(body, **kw):
    return pl.pallas_call(body, **kw)


def _params(*sem):
    return pltpu.CompilerParams(dimension_semantics=tuple(sem), vmem_limit_bytes=VMEM_LIMIT)


def _sigmoid(x):
    return 1.0 / (1.0 + jnp.exp(-x))


def _softplus(x):
    return jnp.maximum(x, 0.0) + jnp.log(1.0 + jnp.exp(-jnp.abs(x)))


def _dot(a, b):
    return jnp.dot(a.astype(MXU), b.astype(MXU), preferred_element_type=F32)


def _dot_nt(a, b):
    return lax.dot_general(a.astype(MXU), b.astype(MXU), (((1,), (1,)), ((), ())), preferred_element_type=F32)


def _dot_tn(a, b):
    return lax.dot_general(a.astype(MXU), b.astype(MXU), (((0,), (0,)), ((), ())), preferred_element_type=F32)


def _split3(x):
    hi = x.astype(jnp.bfloat16)
    r1 = x - hi.astype(F32)
    mid = r1.astype(jnp.bfloat16)
    lo = (r1 - mid.astype(F32)).astype(jnp.bfloat16)
    return hi, mid, lo


def _dot01(m01, x):
    m = m01.astype(jnp.bfloat16)
    hi, mid, lo = _split3(x)
    f = lambda p: jnp.dot(m, p, preferred_element_type=F32)
    return f(hi) + (f(mid) + f(lo))


def _rows_call(body, name, n_rows, ins, outs, scratch=(), tm=ROW_TILE):
    steps = n_rows // tm
    per8 = tm // SUBLANES
    last8 = n_rows // SUBLANES - 1
    in_specs = []
    for arr, kind in ins:
        cols = arr.shape[-1]
        if kind == "tile":
            in_specs.append(pl.BlockSpec((tm, cols), lambda i: (i, 0)))
        elif kind == "full":
            in_specs.append(pl.BlockSpec(arr.shape, lambda i, nd=arr.ndim: (0,) * nd))
        elif kind == "prev8":
            in_specs.append(pl.BlockSpec((SUBLANES, cols), lambda i: (jnp.maximum(i * per8 - 1, 0), 0)))
        elif kind == "next8":
            in_specs.append(pl.BlockSpec((SUBLANES, cols), lambda i: (jnp.minimum((i + 1) * per8, last8), 0)))
        else:
            raise ValueError(kind)
    out_specs, out_shape, has_acc = [], [], False
    for shape, dtype, kind in outs:
        out_shape.append(jax.ShapeDtypeStruct(shape, dtype))
        if kind == "tile":
            out_specs.append(pl.BlockSpec((tm, shape[-1]), lambda i: (i, 0)))
        else:
            has_acc = True
            out_specs.append(pl.BlockSpec(shape, lambda i: (0, 0)))
    return _pcall(
        body, name=name, grid=(steps,), in_specs=in_specs, out_specs=out_specs, out_shape=out_shape,
        scratch_shapes=list(scratch),
        compiler_params=_params("arbitrary" if has_acc else "parallel"),
    )(*[a for a, _ in ins])


def _acc_add(ref, value):
    @pl.when(pl.program_id(0) == 0)
    def _():
        ref[...] = jnp.zeros_like(ref)
    ref[...] += value


def _col_chunks(n, width=512):
    return [(c, min(width, n - c)) for c in range(0, n, width)]


NT_DIMS = (((1,), (1,)), ((), ()))
TN_DIMS = (((0,), (0,)), ((), ()))


def _mm_out(a, ws, name, w_is_out_by_in=False, out_dtype=F32, tm=ROW_TILE):
    m, k = a.shape
    ns = [w.shape[0] if w_is_out_by_in else w.shape[1] for w in ws]

    def body(a_ref, *refs):
        av = a_ref[...]
        for w_ref, o_ref, n in zip(refs[:len(ws)], refs[len(ws):], ns):
            for c, wd in _col_chunks(n):
                if w_is_out_by_in:
                    part = lax.dot_general(av, w_ref[c:c + wd, :], NT_DIMS, preferred_element_type=F32)
                else:
                    part = jnp.dot(av, w_ref[:, c:c + wd], preferred_element_type=F32)
                o_ref[:, c:c + wd] = part.astype(o_ref.dtype)

    return _pcall(
        body, name=name, grid=(m // tm,),
        in_specs=[pl.BlockSpec((tm, k), lambda i: (i, 0))] + [pl.BlockSpec(w.shape, lambda i: (0, 0)) for w in ws],
        out_specs=[pl.BlockSpec((tm, n), lambda i: (i, 0)) for n in ns],
        out_shape=[jax.ShapeDtypeStruct((m, n), out_dtype) for n in ns],
        compiler_params=_params("parallel"),
    )(a, *ws)


def _rms_proj_fwd(x, norm_w, wts, name, tm=ROW_TILE):
    m, k = x.shape
    ns = [w.shape[0] for w in wts]

    def body(x_ref, nw_ref, *refs):
        w_refs, h_ref, o_refs = refs[:len(wts)], refs[len(wts)], refs[len(wts) + 1:]
        xv = x_ref[...]
        r = lax.rsqrt(jnp.mean(xv * xv, axis=-1, keepdims=True) + NORM_EPS)
        hv = (xv * r * nw_ref[...]).astype(h_ref.dtype)
        h_ref[...] = hv
        for w_ref, o_ref, n in zip(w_refs, o_refs, ns):
            for c, wd in _col_chunks(n):
                o_ref[:, c:c + wd] = lax.dot_general(hv, w_ref[c:c + wd, :], NT_DIMS, preferred_element_type=F32)

    return _pcall(
        body, name=name, grid=(m // tm,),
        in_specs=[pl.BlockSpec((tm, k), lambda i: (i, 0)), pl.BlockSpec(norm_w.shape, lambda i: (0, 0))]
        + [pl.BlockSpec(w.shape, lambda i: (0, 0)) for w in wts],
        out_specs=[pl.BlockSpec((tm, k), lambda i: (i, 0))] + [pl.BlockSpec((tm, n), lambda i: (i, 0)) for n in ns],
        out_shape=[jax.ShapeDtypeStruct((m, k), MXU)] + [jax.ShapeDtypeStruct((m, n), F32) for n in ns],
        compiler_params=_params("parallel"),
    )(x, norm_w, *wts)


def _mm_in(ds, ws, name, w_is_out_by_in=False, tm=ROW_TILE):
    m = ds[0].shape[0]
    k = ws[0].shape[1] if w_is_out_by_in else ws[0].shape[0]
    ns = [d.shape[1] for d in ds]

    def body(*refs):
        d_refs, w_refs, o_ref = refs[:len(ds)], refs[len(ds):2 * len(ds)], refs[-1]
        first = True
        for d_ref, w_ref, n in zip(d_refs, w_refs, ns):
            for c, wd in _col_chunks(n, 1024):
                if w_is_out_by_in:
                    part = jnp.dot(d_ref[:, c:c + wd], w_ref[c:c + wd, :], preferred_element_type=F32)
                else:
                    part = lax.dot_general(d_ref[:, c:c + wd], w_ref[:, c:c + wd], NT_DIMS, preferred_element_type=F32)
                if first:
                    o_ref[...] = part
                    first = False
                else:
                    o_ref[...] += part

    return _pcall(
        body, name=name, grid=(m // tm,),
        in_specs=[pl.BlockSpec((tm, n), lambda i: (i, 0)) for n in ns] + [pl.BlockSpec(w.shape, lambda i: (0, 0)) for w in ws],
        out_specs=pl.BlockSpec((tm, k), lambda i: (i, 0)),
        out_shape=jax.ShapeDtypeStruct((m, k), F32),
        compiler_params=_params("parallel"),
    )(*ds, *ws)


def _mm_tn(a, d, name):
    m, k = a.shape
    n = d.shape[1]
    tk = 512 if k % 512 == 0 else k

    def body(a_ref, d_ref, o_ref):
        o_ref[...] = lax.dot_general(a_ref[...], d_ref[...], TN_DIMS, preferred_element_type=F32)

    return _pcall(
        body, name=name, grid=(k // tk,),
        in_specs=[pl.BlockSpec((m, tk), lambda p: (0, p)), pl.BlockSpec((m, n), lambda p: (0, 0))],
        out_specs=pl.BlockSpec((tk, n), lambda p: (p, 0)),
        out_shape=jax.ShapeDtypeStruct((k, n), F32),
        compiler_params=_params("parallel"),
    )(a, d)


WGRAD_TILE = 512


def _proj_wgrad_all(dsegs, valid_rows, hb):
    m, k = hb.shape
    n_seg = len(dsegs)
    tiles, row = [], 0
    for si, (d, valid) in enumerate(zip(dsegs, valid_rows)):
        for c in range(0, valid, WGRAD_TILE):
            width = min(WGRAD_TILE, d.shape[1] - c)
            tiles.append((si, c, width, row + c, min(width, valid - c)))
        row += valid
    total_rows = row

    def body(*refs):
        d_refs, hb_ref, o_ref = refs[:n_seg], refs[n_seg], refs[n_seg + 1]
        a_buf, hb_buf, o_buf, load_sems, store_sems, hb_sem = refs[n_seg + 2:]

        def load(t):
            si, c, width, _, _ = tiles[t]
            return pltpu.make_async_copy(d_refs[si].at[:, pl.ds(c, width)], a_buf.at[t % 2, :, pl.ds(0, width)],
                                         load_sems.at[t % 2])

        def stores(t):
            _, _, _, orow, valid = tiles[t]
            return [pltpu.make_async_copy(o_buf.at[t % 2, pl.ds(0, valid), :], o_ref.at[pl.ds(orow, valid), :],
                                          store_sems.at[t % 2])]

        hb_copy = pltpu.make_async_copy(hb_ref, hb_buf, hb_sem)
        hb_copy.start()
        load(0).start()
        hb_copy.wait()
        for t in range(len(tiles)):
            width = tiles[t][2]
            load(t).wait()
            if t + 1 < len(tiles):
                load(t + 1).start()
            if t >= 2:
                for cp in stores(t - 2):
                    cp.wait()
            o_buf[t % 2, 0:width, :] = lax.dot_general(a_buf[t % 2, :, 0:width], hb_buf[...], TN_DIMS,
                                                        preferred_element_type=F32).astype(o_buf.dtype)
            for cp in stores(t):
                cp.start()
        for t in range(max(len(tiles) - 2, 0), len(tiles)):
            for cp in stores(t):
                cp.wait()

    any_spec = pl.BlockSpec(memory_space=pl.ANY)
    return _pcall(
        body, name="proj_wgrad",
        in_specs=[any_spec] * (n_seg + 1), out_specs=any_spec,
        out_shape=jax.ShapeDtypeStruct((total_rows, k), hb.dtype),
        scratch_shapes=[pltpu.VMEM((2, m, WGRAD_TILE), hb.dtype), pltpu.VMEM((m, k), hb.dtype),
                        pltpu.VMEM((2, WGRAD_TILE, k), hb.dtype), pltpu.SemaphoreType.DMA((2,)),
                        pltpu.SemaphoreType.DMA((2,)), pltpu.SemaphoreType.DMA],
        compiler_params=pltpu.CompilerParams(vmem_limit_bytes=VMEM_LIMIT),
    )(*dsegs, hb)


def _conv_taps(ext_ref, cw_ref, cols, tm):
    c = None
    for j in range(4):
        term = cw_ref[3 - j:4 - j, cols] * ext_ref[SUBLANES - j:SUBLANES - j + tm, cols]
        c = term if c is None else c + term
    return c


def _fill_ext(ext_ref, u_ref, halo_ref, first):
    ext_ref[0:SUBLANES, :] = jnp.where(first, 0.0, halo_ref[...])
    ext_ref[SUBLANES:, :] = u_ref[...]


def _dn_prep_fwd_tile(u_ref, halo_ref, cw_ref, ba_ref, al_ref, dtb_ref, q_ref, k_ref, v_ref, bg_ref, ext_ref, first):
    tm = u_ref.shape[0]
    _fill_ext(ext_ref, u_ref, halo_ref, first)
    for h in range(3 * DN_HEADS):
        cols = slice(h * LANES, (h + 1) * LANES)
        c = _conv_taps(ext_ref, cw_ref, cols, tm)
        a = c * _sigmoid(c)
        oc = slice((h % DN_HEADS) * LANES, (h % DN_HEADS + 1) * LANES)
        if h < 2 * DN_HEADS:
            rinv = lax.rsqrt(jnp.sum(a * a, axis=-1, keepdims=True) + NORM_EPS)
            if h < DN_HEADS:
                q_ref[:, oc] = a * (rinv * DN_DK ** -0.5)
            else:
                k_ref[:, oc] = a * rinv
        else:
            v_ref[:, oc] = a
    bav = ba_ref[...]
    lane = lax.broadcasted_iota(jnp.int32, bav.shape, 1)
    beta = _sigmoid(bav)
    g = -jnp.exp(al_ref[...]) * _softplus(bav + dtb_ref[...])
    bg_ref[...] = jnp.where(lane < DN_HEADS, beta, jnp.where(lane < 2 * DN_HEADS, g, 0.0))


def _dn_prep_bwd_tile(u_ref, halo_ref, cw_ref, ba_ref, al_ref, dtb_ref, dq_ref, dk_ref, dv_ref, dbg_ref,
                      dc_ref, dba_ref, ext_ref, first):
    tm = u_ref.shape[0]
    _fill_ext(ext_ref, u_ref, halo_ref, first)
    for h in range(3 * DN_HEADS):
        cols = slice(h * LANES, (h + 1) * LANES)
        oc = slice((h % DN_HEADS) * LANES, (h % DN_HEADS + 1) * LANES)
        c = _conv_taps(ext_ref, cw_ref, cols, tm)
        sg = _sigmoid(c)
        a = c * sg
        if h < 2 * DN_HEADS:
            rinv = lax.rsqrt(jnp.sum(a * a, axis=-1, keepdims=True) + NORM_EPS)
            dy = dq_ref[:, oc] * DN_DK ** -0.5 if h < DN_HEADS else dk_ref[:, oc]
            da = rinv * dy - a * (rinv * rinv * rinv) * jnp.sum(dy * a, axis=-1, keepdims=True)
        else:
            da = dv_ref[:, oc]
        dc_ref[:, cols] = da * (sg * (1.0 + c * (1.0 - sg)))
    bav = ba_ref[...]
    dbgv = dbg_ref[...]
    lane = lax.broadcasted_iota(jnp.int32, bav.shape, 1)
    beta = _sigmoid(bav)
    ea = jnp.exp(al_ref[...])
    z = bav + dtb_ref[...]
    g = -ea * _softplus(z)
    is_b = lane < DN_HEADS
    is_g = jnp.logical_and(lane >= DN_HEADS, lane < 2 * DN_HEADS)
    d_aa = jnp.where(is_g, dbgv * (-ea) * _sigmoid(z), 0.0)
    dba = jnp.where(is_b, dbgv * beta * (1.0 - beta), d_aa)
    dba_ref[...] = dba.astype(dba_ref.dtype)
    r_alog = jnp.sum(jnp.where(is_g, dbgv * g, 0.0), axis=0, keepdims=True)
    r_dtb = jnp.sum(d_aa, axis=0, keepdims=True)
    return jnp.concatenate([r_alog, r_dtb, jnp.zeros((SUBLANES - 2, LANES), F32)], axis=0)


def _conv_bwd_tile(extd_ref, ext_ref, cw_ref, du_ref, dcw_ref, tm):
    for h in range(3 * DN_HEADS):
        cols = slice(h * LANES, (h + 1) * LANES)
        du = None
        for j in range(4):
            term = cw_ref[3 - j:4 - j, cols] * extd_ref[j:j + tm, cols]
            du = term if du is None else du + term
        du_ref[:, cols] = du.astype(du_ref.dtype)
        dcv = extd_ref[0:tm, cols]
        for j in range(4):
            row = jnp.sum(dcv * ext_ref[SUBLANES - j:SUBLANES - j + tm, cols], axis=0, keepdims=True)
            dcw_ref[3 - j:4 - j, cols] += row


def _dn_out_fwd_tile(o_ref, z_ref, w_ref, on_ref):
    for h in range(DN_HEADS):
        cols = _head_cols(h)
        ov = o_ref[:, cols]
        zv = z_ref[:, cols]
        ro = lax.rsqrt(jnp.mean(ov * ov, axis=-1, keepdims=True) + NORM_EPS)
        on_ref[:, cols] = (ov * ro * w_ref[...] * (zv * _sigmoid(zv))).astype(on_ref.dtype)


def _dn_out_bwd_tile(dy_ref, o_ref, z_ref, w_ref, wo_ref, do_ref, dz_ref, d_ref):
    d_ref[...] = lax.dot_general(dy_ref[...], wo_ref[...], NT_DIMS, preferred_element_type=F32)
    acc = jnp.zeros((1, LANES), F32)
    for h in range(DN_HEADS):
        cols = _head_cols(h)
        dv, ov, zv = d_ref[:, cols], o_ref[:, cols], z_ref[:, cols]
        sg = _sigmoid(zv)
        sz = zv * sg
        ro = lax.rsqrt(jnp.mean(ov * ov, axis=-1, keepdims=True) + NORM_EPS)
        nv = ov * ro
        dn = dv * w_ref[...] * sz
        acc = acc + jnp.sum(dv * nv * sz, axis=0, keepdims=True)
        dz_ref[:, cols] = (dv * nv * w_ref[...] * (sg * (1.0 + zv * (1.0 - sg)))).astype(dz_ref.dtype)
        do_ref[:, cols] = ro * dn - ov * (ro * ro * ro) * jnp.mean(dn * ov, axis=-1, keepdims=True)
    return acc


def _attn_out_fwd_tile(o0, o1, o2, l0, l1, l2, z_ref, lse_ref, o_ref, g_ref):
    a, b, c = l0[...], l1[...], l2[...]
    m = jnp.maximum(a, jnp.maximum(b, c))
    ea, eb, ec = jnp.exp(a - m), jnp.exp(b - m), jnp.exp(c - m)
    den = ea + eb + ec
    out = (ea * o0[...] + eb * o1[...] + ec * o2[...]) / den
    lse_ref[...] = m + jnp.log(den)
    o_ref[...] = out
    zv = z_ref[...]
    g_ref[...] = (out * (zv * _sigmoid(zv))).astype(g_ref.dtype)


def _attn_out_bwd_tile(dy_ref, o_ref, z_ref, wo_ref, do_ref, dz_ref, dl_ref):
    zv = z_ref[...]
    sg = _sigmoid(zv)
    dv = lax.dot_general(dy_ref[...], wo_ref[...], NT_DIMS, preferred_element_type=F32)
    ov = o_ref[...]
    do = dv * (zv * sg)
    do_ref[...] = do
    dz_ref[...] = (dv * ov * (sg * (1.0 + zv * (1.0 - sg)))).astype(dz_ref.dtype)
    for h in range(DIL_HEADS):
        cols = _head_cols(h)
        dl_ref[:, cols] = jnp.broadcast_to(jnp.sum(do[:, cols] * ov[:, cols], axis=-1, keepdims=True),
                                           (do.shape[0], LANES))


def _merge_out_final(ga, gb, on_a, on_b, w_o_dn, w_o_dil, x, target, w_out, wf_row):
    s, dm = x.shape

    def body(ga_ref, gb_ref, ona_ref, onb_ref, wa_ref, wb_ref, x_ref, t_ref, wo_ref, w_ref,
             loss_ref, dw_ref, m_ref, dxb_ref, dx_ref, dya_ref, dyb_ref, dga_ref, dgb_ref):
        sa, sb = _sigmoid(ga_ref[...]), _sigmoid(gb_ref[...])
        ya = jnp.dot(ona_ref[...], wa_ref[...], preferred_element_type=F32)
        yb = jnp.dot(onb_ref[...], wb_ref[...], preferred_element_type=F32)
        merged = (sa * ya + sb * yb).astype(MXU)
        m_ref[...] = merged
        x2 = x_ref[...] + jnp.dot(merged, wo_ref[...], preferred_element_type=F32)
        r = lax.rsqrt(jnp.mean(x2 * x2, axis=-1, keepdims=True) + NORM_EPS)
        w = w_ref[...]
        err = x2 * r * w - t_ref[...]
        tile_loss = 0.5 * jnp.sum(jnp.mean(err * err, axis=-1, keepdims=True), axis=0, keepdims=True)
        _acc_add(loss_ref, jnp.broadcast_to(tile_loss, (SUBLANES, LANES)))
        dy = err * (1.0 / dm)
        row = jnp.sum(dy * x2 * r, axis=0, keepdims=True)
        _acc_add(dw_ref, jnp.concatenate([row, jnp.zeros((SUBLANES - 1, dm), F32)], axis=0))
        dn = dy * w
        dx2 = r * dn - x2 * (r * r * r) * jnp.mean(dn * x2, axis=-1, keepdims=True)
        dx_ref[...] = dx2
        dxb = dx2.astype(MXU)
        dxb_ref[...] = dxb
        dmv = lax.dot_general(dxb, wo_ref[...], NT_DIMS, preferred_element_type=F32)
        dya_ref[...] = (dmv * sa).astype(dya_ref.dtype)
        dyb_ref[...] = (dmv * sb).astype(dyb_ref.dtype)
        dga_ref[...] = (dmv * ya * sa * (1.0 - sa)).astype(dga_ref.dtype)
        dgb_ref[...] = (dmv * yb * sb * (1.0 - sb)).astype(dgb_ref.dtype)

    return _rows_call(body, "merge_out_final", s,
                      [(ga, "tile"), (gb, "tile"), (on_a, "tile"), (on_b, "tile"), (w_o_dn, "full"), (w_o_dil, "full"),
                       (x, "tile"), (target, "tile"), (w_out, "full"), (wf_row, "full")],
                      [((SUBLANES, LANES), F32, "acc"), ((SUBLANES, dm), F32, "acc"), ((s, dm), MXU, "tile"),
                       ((s, dm), MXU, "tile"), ((s, dm), F32, "tile")] + [((s, dm), MXU, "tile")] * 4)


def _lane_pick(x, idx):
    lane = lax.broadcasted_iota(jnp.int32, x.shape, 1)
    return jnp.sum(jnp.where(lane == idx, x, 0.0), axis=-1, keepdims=True)


PAIR = 2 * DN_CHUNK
SCAN_CHUNKS = 4


def _bmm(a, b):
    return lax.dot_general(a.astype(MXU), b.astype(MXU), (((2,), (1,)), ((0,), (0,))), preferred_element_type=F32)


def _bmm_nt(a, b):
    return lax.dot_general(a.astype(MXU), b.astype(MXU), (((2,), (2,)), ((0,), (0,))), preferred_element_type=F32)


def _bmm_tn(a, b):
    return lax.dot_general(a.astype(MXU), b.astype(MXU), (((1,), (1,)), ((0,), (0,))), preferred_element_type=F32)


def _bmm3(a, b):
    ah = a.astype(jnp.bfloat16)
    al = (a - ah.astype(F32)).astype(jnp.bfloat16)
    bh = b.astype(jnp.bfloat16)
    bl = (b - bh.astype(F32)).astype(jnp.bfloat16)
    f = lambda p, q: lax.dot_general(p, q, (((2,), (1,)), ((0,), (0,))), preferred_element_type=F32)
    return f(ah, bh) + (f(ah, bl) + f(al, bh))


def _pair_masks():
    row = lax.broadcasted_iota(jnp.int32, (PAIR, PAIR), 0)
    col = lax.broadcasted_iota(jnp.int32, (PAIR, PAIR), 1)
    same = (row >= DN_CHUNK) == (col >= DN_CHUNK)
    return dict(causal=same & (row >= col), strict=same & (row > col), upper=same & (row <= col), eye=row == col,
                first=row < DN_CHUNK, row=row, lane=col)


def _pair_decay(bgv, masks):
    gc_all = _dot01(masks["causal"].astype(F32), bgv)
    out = []
    for h in range(DN_HEADS):
        beta = _lane_pick(bgv, h)
        gcb = jnp.broadcast_to(_lane_pick(gc_all, DN_HEADS + h), (PAIR, PAIR))
        gam = jnp.where(masks["causal"], jnp.exp(jnp.minimum(gcb - gcb.T, 0.0)), 0.0)
        gl = jnp.where(masks["first"], gcb[DN_CHUNK - 1:DN_CHUNK, :], gcb[PAIR - 1:PAIR, :])
        out.append((beta, gcb, gam, gl))
    return out


def _pair_inverse(a_strict, eye):
    eye_f = eye.astype(F32)[None]
    m = eye_f + a_strict
    x = eye_f - a_strict
    steps = int(math.log2(DN_CHUNK)) - 1
    for i in range(steps):
        mm = _bmm3 if i == steps - 1 else _bmm
        x = x + mm(x, eye_f - mm(m, x))
    return x


def _head_cols(h):
    return slice(h * LANES, (h + 1) * LANES)


def _delta_prep(qkv_pre, ba, conv_w8, alog_row, dtb_row):
    s = qkv_pre.shape[0]
    c = DN_CHUNK
    n_chunks = s // c

    def body(pre_ref, halo_ref, cw_ref, ba_ref, al_ref, dtb_ref,
             u_ref, w_ref, qd_ref, kd_ref, aqk_ref, dl_ref, t2_ref, q_ref, k_ref, v_ref, bg_ref, ext_ref):
        _dn_prep_fwd_tile(pre_ref, halo_ref, cw_ref, ba_ref, al_ref, dtb_ref, q_ref, k_ref, v_ref, bg_ref, ext_ref,
                          pl.program_id(0) == 0)
        masks = _pair_masks()
        dec = _pair_decay(bg_ref[...], masks)
        kbs, ks, gams, vbs, kbes, qs, qds, kds, dls = ([] for _ in range(9))
        for h in range(DN_HEADS):
            beta, gcb, gam, gl = dec[h]
            qh, kh, vh = q_ref[:, _head_cols(h)], k_ref[:, _head_cols(h)], v_ref[:, _head_cols(h)]
            eg = jnp.exp(gcb)
            kb = kh * beta
            kbs.append(kb); ks.append(kh); gams.append(gam); vbs.append(vh * beta); kbes.append(kb * eg)
            qs.append(qh); qds.append(qh * eg); kds.append(kh * jnp.exp(gl - gcb)); dls.append(jnp.exp(gl))
        st = lambda xs: jnp.stack(xs, axis=0)
        kmat, gam = st(ks), st(gams)
        a = jnp.where(masks["strict"][None], _bmm_nt(st(kbs), kmat) * gam, 0.0)
        t = _pair_inverse(a, masks["eye"])
        u = _bmm(t, st(vbs))
        w = _bmm(t, st(kbes))
        aqk = _bmm_nt(st(qs), kmat) * gam
        t2_ref[0] = t.astype(t2_ref.dtype)
        for half in range(2):
            rows = slice(half * c, (half + 1) * c)
            u_ref[half] = u[:, rows, :]
            w_ref[half] = w[:, rows, :].astype(w_ref.dtype)
            qd_ref[half] = st(qds)[:, rows, :].astype(qd_ref.dtype)
            kd_ref[half] = st(kds)[:, rows, :].astype(kd_ref.dtype)
            aqk_ref[half] = aqk[:, rows, rows].astype(aqk_ref.dtype)
            dl_ref[half] = st(dls)[:, half * c:half * c + SUBLANES, :]

    row_spec = lambda w_: pl.BlockSpec((PAIR, w_), lambda i: (i, 0))
    hm = lambda a_, b_: pl.BlockSpec((2, DN_HEADS, a_, b_), lambda i: (i, 0, 0, 0))
    hm_shape = lambda a_, b_, dt: jax.ShapeDtypeStruct((n_chunks, DN_HEADS, a_, b_), dt)
    whole = lambda t: pl.BlockSpec(t.shape, lambda i: (0, 0))
    halo = pl.BlockSpec((SUBLANES, QKV_W), lambda i: (jnp.maximum(i * (PAIR // SUBLANES) - 1, 0), 0))
    return _pcall(
        body, name="delta_prep", grid=(n_chunks // 2,),
        in_specs=[row_spec(QKV_W), halo, whole(conv_w8), row_spec(LANES), whole(alog_row), whole(dtb_row)],
        out_specs=[hm(c, LANES)] * 4 + [hm(c, c), hm(SUBLANES, LANES),
                   pl.BlockSpec((1, DN_HEADS, PAIR, PAIR), lambda i: (i, 0, 0, 0))]
        + [row_spec(D_MODEL)] * 3 + [row_spec(LANES)],
        out_shape=[hm_shape(c, LANES, F32), hm_shape(c, LANES, MXU), hm_shape(c, LANES, MXU), hm_shape(c, LANES, MXU),
                   hm_shape(c, c, MXU), hm_shape(SUBLANES, LANES, F32),
                   jax.ShapeDtypeStruct((n_chunks // 2, DN_HEADS, PAIR, PAIR), MXU)]
        + [jax.ShapeDtypeStruct((s, D_MODEL), F32)] * 3 + [jax.ShapeDtypeStruct((s, LANES), F32)],
        scratch_shapes=[pltpu.VMEM((PAIR + SUBLANES, QKV_W), F32)],
        compiler_params=_params("parallel"),
    )(qkv_pre, qkv_pre, conv_w8, ba, alog_row, dtb_row)


def _delta_scan_fwd(u, w, qd, kd, aqk, dl, z, dnw_row, attn_parts, attn_lses, zb):
    n_chunks = u.shape[0]
    c = DN_CHUNK
    g_n = SCAN_CHUNKS

    def body(u_ref, w_ref, qd_ref, kd_ref, aqk_ref, dl_ref, z_ref, nw_ref,
             p0, p1, p2, l0, l1, l2, zb_ref,
             o_ref, vnew_ref, st_ref, on_ref, lse_ref, oj_ref, gb_ref, state):
        @pl.when(pl.program_id(0) == 0)
        def _():
            state[...] = jnp.zeros_like(state)

        _attn_out_fwd_tile(p0, p1, p2, l0, l1, l2, zb_ref, lse_ref, oj_ref, gb_ref)
        for g in range(g_n):
            sv = state[...]
            sb = sv.astype(MXU)
            vnew = u_ref[g] - _bmm(w_ref[g], sb)
            o = _bmm(qd_ref[g], sb) + _bmm(aqk_ref[g], vnew)
            state[...] = sv * dl_ref[g][:, 0:1, :] + _bmm_tn(kd_ref[g], vnew)
            vnew_ref[g] = vnew.astype(vnew_ref.dtype)
            st_ref[g] = sb
            for h in range(DN_HEADS):
                o_ref[g * c:(g + 1) * c, _head_cols(h)] = o[h]
        _dn_out_fwd_tile(o_ref, z_ref, nw_ref, on_ref)

    hm = lambda a_, b_: pl.BlockSpec((g_n, DN_HEADS, a_, b_), lambda i: (i, 0, 0, 0))
    rows = lambda width: pl.BlockSpec((g_n * c, width), lambda i: (i, 0))
    whole = lambda t: pl.BlockSpec(t.shape, lambda i: (0, 0))
    full = lambda width, dt: jax.ShapeDtypeStruct((n_chunks * c, width), dt)
    return _pcall(
        body, name="delta_scan_fwd", grid=(n_chunks // g_n,),
        in_specs=[hm(c, LANES)] * 4 + [hm(c, c), hm(SUBLANES, LANES), rows(D_MODEL), whole(dnw_row)]
        + [rows(DIL_W)] * 7,
        out_specs=[rows(D_MODEL), hm(c, LANES), hm(DN_DK, DN_DK), rows(D_MODEL),
                   rows(DIL_W), rows(DIL_W), rows(DIL_W)],
        out_shape=[full(D_MODEL, F32),
                   jax.ShapeDtypeStruct((n_chunks, DN_HEADS, c, LANES), MXU),
                   jax.ShapeDtypeStruct((n_chunks, DN_HEADS, DN_DK, DN_DK), MXU),
                   full(D_MODEL, MXU), full(DIL_W, F32), full(DIL_W, F32), full(DIL_W, MXU)],
        scratch_shapes=[pltpu.VMEM((DN_HEADS, DN_DK, DN_DK), F32)],
        compiler_params=_params("arbitrary"),
    )(u, w, qd, kd, aqk, dl, z, dnw_row, *attn_parts, *attn_lses, zb)


def _delta_scan_bwd(w, qd, kd, aqk, dl, vnew, st, dy, o, z, dnw_row, w_o_dn, dyb, o_joint, zb, w_o_dil):
    n_chunks = w.shape[0]
    c = DN_CHUNK
    g_n = SCAN_CHUNKS
    steps = n_chunks // g_n

    def body(w_ref, qd_ref, kd_ref, aqk_ref, dl_ref, vnew_ref, st_ref, dy_ref, o_ref, z_ref, nw_ref, wo_ref,
             dyb_ref, oj_ref, zb_ref, wod_ref,
             dvnew_ref, dkd_ref, ddl_ref, do_ref, dz_ref, dnw_ref, dob_ref, dzb_ref, delta_ref, dstate, d_scratch):
        @pl.when(pl.program_id(0) == 0)
        def _():
            dstate[...] = jnp.zeros_like(dstate)

        _attn_out_bwd_tile(dyb_ref, oj_ref, zb_ref, wod_ref, dob_ref, dzb_ref, delta_ref)
        acc = _dn_out_bwd_tile(dy_ref, o_ref, z_ref, nw_ref, wo_ref, do_ref, dz_ref, d_scratch)
        _acc_add(dnw_ref, jnp.concatenate([acc, jnp.zeros((SUBLANES - 1, LANES), F32)], axis=0))
        for g in reversed(range(g_n)):
            ds = dstate[...]
            dsb = ds.astype(MXU)
            doh = jnp.stack([do_ref[g * c:(g + 1) * c, _head_cols(h)] for h in range(DN_HEADS)], axis=0)
            dvnew = _bmm_tn(aqk_ref[g], doh) + _bmm(kd_ref[g], dsb)
            dkd_ref[g] = _bmm_nt(vnew_ref[g], dsb)
            ddl = jnp.sum(jnp.sum(st_ref[g].astype(F32) * ds, axis=2, keepdims=True), axis=1, keepdims=True)
            ddl_ref[g] = jnp.broadcast_to(ddl, (DN_HEADS, SUBLANES, LANES))
            dstate[...] = ds * dl_ref[g][:, 0:1, :] + _bmm_tn(qd_ref[g], doh) - _bmm_tn(w_ref[g], dvnew)
            dvnew_ref[g] = dvnew.astype(dvnew_ref.dtype)

    rev = lambda i: steps - 1 - i
    hm = lambda a_, b_: pl.BlockSpec((g_n, DN_HEADS, a_, b_), lambda i: (rev(i), 0, 0, 0))
    rows = lambda width: pl.BlockSpec((g_n * c, width), lambda i: (rev(i), 0))
    whole = lambda t: pl.BlockSpec(t.shape, lambda i: (0, 0))
    full = lambda width, dt: jax.ShapeDtypeStruct((n_chunks * c, width), dt)
    return _pcall(
        body, name="delta_scan_bwd", grid=(steps,),
        in_specs=[hm(c, LANES)] * 3 + [hm(c, c), hm(SUBLANES, LANES), hm(c, LANES), hm(DN_DK, DN_DK),
                  rows(D_MODEL), rows(D_MODEL), rows(D_MODEL), whole(dnw_row), whole(w_o_dn),
                  rows(D_MODEL), rows(DIL_W), rows(DIL_W), whole(w_o_dil)],
        out_specs=[hm(c, LANES), hm(c, LANES), hm(SUBLANES, LANES), rows(D_MODEL), rows(D_MODEL),
                   pl.BlockSpec((SUBLANES, LANES), lambda i: (0, 0)), rows(DIL_W), rows(DIL_W), rows(DIL_W)],
        out_shape=[jax.ShapeDtypeStruct((n_chunks, DN_HEADS, c, LANES), MXU),
                   jax.ShapeDtypeStruct((n_chunks, DN_HEADS, c, LANES), F32),
                   jax.ShapeDtypeStruct((n_chunks, DN_HEADS, SUBLANES, LANES), F32),
                   full(D_MODEL, F32), full(D_MODEL, MXU), jax.ShapeDtypeStruct((SUBLANES, LANES), F32),
                   full(DIL_W, F32), full(DIL_W, MXU), full(DIL_W, F32)],
        scratch_shapes=[pltpu.VMEM((DN_HEADS, DN_DK, DN_DK), F32), pltpu.VMEM((g_n * c, D_MODEL), F32)],
        compiler_params=_params("arbitrary"),
    )(w, qd, kd, aqk, dl, vnew, st, dy, o, z, dnw_row, w_o_dn, dyb, o_joint, zb, w_o_dil)


def _delta_post_bwd(q, k, v, bg, t2, st, vnew, do, dvnew, dkd, ddl, qkv_pre, ba, conv_w8, alog_row, dtb_row):
    s = q.shape[0]
    c = DN_CHUNK
    n_pairs = s // PAIR

    def body(q_ref, k_ref, v_ref, bg_ref, t2_ref, st_ref, vnew_ref, do_ref, dvnew_ref, dkd_ref, ddl_ref,
             pre_ref, halo_ref, cw_ref, ba_ref, al_ref, dtb_ref,
             du_ref, dba_ref, dsmall_ref, dcw_ref, dq_ref, dk_ref, dv_ref, dbg_ref, ext_ref, extd_ref, carry_ref):
        step = pl.program_id(0)

        @pl.when(step == 0)
        def _():
            carry_ref[...] = jnp.zeros_like(carry_ref)
            dcw_ref[...] = jnp.zeros_like(dcw_ref)

        masks = _pair_masks()
        first = masks["first"][None]
        dec = _pair_decay(bg_ref[...], masks)
        st_ = lambda xs: jnp.stack(xs, axis=0)
        heads = range(DN_HEADS)
        qm_, km_, vm_, dom = (st_([r[:, _head_cols(h)] for h in heads]) for r in (q_ref, k_ref, v_ref, do_ref))
        beta = st_([dec[h][0] for h in heads])
        gcb = st_([dec[h][1] for h in heads])
        gam = st_([dec[h][2] for h in heads])
        gl = st_([dec[h][3] for h in heads])
        pair = lambda ref: jnp.concatenate([ref[0], ref[1]], axis=1)
        vnew2, dvnew2, dkd2 = pair(vnew_ref), pair(dvnew_ref), pair(dkd_ref)
        halves = lambda x: (x[:, :c, :], x[:, c:, :])
        by_state = lambda x: jnp.concatenate([_bmm_nt(xh, st_ref[i]) for i, xh in enumerate(halves(x))], axis=1)
        dqd = by_state(dom)
        dw = -by_state(dvnew2)
        ddl2 = jnp.where(first, ddl_ref[0][:, 0:1, :], ddl_ref[1][:, 0:1, :])

        eg = jnp.exp(gcb)
        egl = jnp.exp(gl - gcb)
        dl = jnp.exp(gl)
        kb = km_ * beta
        kk = _bmm_nt(kb, km_)
        a = jnp.where(masks["strict"][None], kk * gam, 0.0)
        t = t2_ref[0]
        vb = vm_ * beta
        kbe = kb * eg
        u = _bmm(t, vb)
        w = _bmm(t, kbe)
        aqk = _bmm_nt(qm_, km_) * gam
        qd = qm_ * eg
        kd = km_ * egl

        daqk = jnp.where(masks["causal"][None], _bmm_nt(dom, vnew2), 0.0)
        dvb = _bmm_tn(t, dvnew2)
        dkbe = _bmm_tn(t, dw)
        da = jnp.where(masks["strict"][None], -(_bmm_nt(dvb, u) + _bmm_nt(dkbe, w)), 0.0)
        pm = da * gam
        qmm = daqk * gam
        dkb = _bmm(pm, km_) + dkbe * eg
        dkh = _bmm_tn(pm, kb) + _bmm_tn(qmm, qm_) + dkd2 * egl + dkb * beta
        dqh = _bmm(qmm, km_) + dqd * eg
        xm = da * a + daqk * aqk
        col_rows = jnp.concatenate([jnp.zeros((DN_HEADS, PAIR), F32), jnp.sum(xm, axis=1),
                                    jnp.zeros((PAIR - 2 * DN_HEADS, PAIR), F32)], axis=0)
        tmp = jnp.sum(dkd2 * kd, axis=-1, keepdims=True)
        dgc = (jnp.sum(xm, axis=-1, keepdims=True) + jnp.sum(dkbe * kbe, axis=-1, keepdims=True)
               + jnp.sum(dqd * qd, axis=-1, keepdims=True) - tmp)
        sum0 = jnp.sum(jnp.where(first, tmp, 0.0), axis=1, keepdims=True)
        sum1 = jnp.sum(jnp.where(first, 0.0, tmp), axis=1, keepdims=True)
        dgl = jnp.where(first, sum0, sum1) + ddl2 * dl
        last = (masks["row"] == c - 1) | (masks["row"] == PAIR - 1)
        dgc = dgc + jnp.where(last[None], dgl, 0.0)
        dbeta = jnp.sum(dvb * vm_, axis=-1, keepdims=True) + jnp.sum(dkb * km_, axis=-1, keepdims=True)
        dvh = dvb * beta

        lane = masks["lane"]
        dgc_lanes = jnp.zeros((PAIR, LANES), F32)
        dbg = jnp.zeros((PAIR, LANES), F32)
        for h in heads:
            dq_ref[:, _head_cols(h)] = dqh[h]
            dk_ref[:, _head_cols(h)] = dkh[h]
            dv_ref[:, _head_cols(h)] = dvh[h]
            dgc_lanes = dgc_lanes + jnp.where(lane == DN_HEADS + h, dgc[h], 0.0)
            dbg = dbg + jnp.where(lane == h, dbeta[h], 0.0)
        dbg_ref[...] = dbg + _dot01(masks["upper"].astype(F32), dgc_lanes - col_rows.T)
        small = _dn_prep_bwd_tile(pre_ref, halo_ref, cw_ref, ba_ref, al_ref, dtb_ref, dq_ref, dk_ref, dv_ref, dbg_ref,
                                  extd_ref.at[pl.ds(0, PAIR)], dba_ref, ext_ref, step == n_pairs - 1)
        _acc_add(dsmall_ref, small)
        extd_ref[PAIR:, :] = carry_ref[...]
        _conv_bwd_tile(extd_ref, ext_ref, cw_ref, du_ref, dcw_ref, PAIR)
        carry_ref[...] = extd_ref[0:SUBLANES, :]

    rev = lambda i: n_pairs - 1 - i
    row_spec = lambda w_: pl.BlockSpec((PAIR, w_), lambda i: (rev(i), 0))
    hm = lambda a_, b_: pl.BlockSpec((2, DN_HEADS, a_, b_), lambda i: (rev(i), 0, 0, 0))
    whole = lambda t: pl.BlockSpec(t.shape, lambda i: (0, 0))
    halo = pl.BlockSpec((SUBLANES, QKV_W), lambda i: (jnp.maximum(rev(i) * (PAIR // SUBLANES) - 1, 0), 0))
    return _pcall(
        body, name="delta_post_bwd", grid=(n_pairs,),
        in_specs=[row_spec(D_MODEL)] * 3
        + [row_spec(LANES), pl.BlockSpec((1, DN_HEADS, PAIR, PAIR), lambda i: (rev(i), 0, 0, 0)),
           hm(DN_DK, DN_DK), hm(c, LANES), row_spec(D_MODEL), hm(c, LANES), hm(c, LANES), hm(SUBLANES, LANES),
           row_spec(QKV_W), halo, whole(conv_w8), row_spec(LANES), whole(alog_row), whole(dtb_row)],
        out_specs=[row_spec(QKV_W), row_spec(LANES), pl.BlockSpec((SUBLANES, LANES), lambda i: (0, 0)),
                   pl.BlockSpec((SUBLANES, QKV_W), lambda i: (0, 0))],
        out_shape=[jax.ShapeDtypeStruct((s, QKV_W), MXU), jax.ShapeDtypeStruct((s, LANES), MXU),
                   jax.ShapeDtypeStruct((SUBLANES, LANES), F32), jax.ShapeDtypeStruct((SUBLANES, QKV_W), F32)],
        scratch_shapes=[pltpu.VMEM((PAIR, D_MODEL), F32)] * 3
        + [pltpu.VMEM((PAIR, LANES), F32), pltpu.VMEM((PAIR + SUBLANES, QKV_W), F32),
           pltpu.VMEM((PAIR + SUBLANES, QKV_W), F32), pltpu.VMEM((SUBLANES, QKV_W), F32)],
        compiler_params=_params("arbitrary"),
    )(q, k, v, bg, t2, st, vnew, do, dvnew, dkd, ddl, qkv_pre, qkv_pre, conv_w8, ba, alog_row, dtb_row)


def _alibi_slope(group, head):
    n = N_DIL * DIL_HEADS
    return float(2.0 ** (-8.0 * (group * DIL_HEADS + head + 1) / n))


def _attn_plan(s, group):
    window, dil = DIL_GROUPS[group]
    assert window // dil == ATT_BLOCK
    assert (s // dil) % ATT_BLOCK == 0, "sub-sequence length must be a whole number of attention blocks"
    return dil, s // dil // ATT_BLOCK, (DIL_HEADS if dil == 1 else 1)


def _attn_specs(group, dil, nb, hp):
    rows = ATT_BLOCK * dil

    def spec(col0, shift):
        if shift < 0:
            f = lambda hb, n: (jnp.maximum(n - 1, 0), col0 + hb)
        elif shift > 0:
            f = lambda hb, n: (jnp.minimum(n + 1, nb - 1), col0 + hb)
        else:
            f = lambda hb, n: (jnp.minimum(n, nb - 1), col0 + hb)
        return pl.BlockSpec((rows, hp * LANES), f)

    return (lambda shift: spec(group * (DIL_HEADS // hp), shift)), (lambda shift: spec(0, shift))


def _sub_rows(ref, r, dil, cols):
    return ref[:, cols] if dil == 1 else ref[pl.ds(r, ATT_BLOCK, stride=dil), cols]


def _set_sub_rows(ref, r, dil, cols, value):
    if dil == 1:
        ref[:, cols] = value
    else:
        ref[pl.ds(r, ATT_BLOCK, stride=dil), cols] = value


def _step_slope(group, hp, hh):
    if hp == DIL_HEADS:
        return _alibi_slope(group, hh)
    hb = pl.program_id(0)
    slope = _alibi_slope(group, DIL_HEADS - 1)
    for h in reversed(range(DIL_HEADS - 1)):
        slope = jnp.where(hb == h, _alibi_slope(group, h), slope)
    return slope


def _attn_items(hp, dil):
    return [(hh, r) for hh in range(hp) for r in range(dil)]


def _attn_stack(ref, items, dil, dtype=MXU):
    return jnp.stack([_sub_rows(ref, r, dil, _head_cols(hh)).astype(dtype) for hh, r in items], axis=0)


def _attn_slopes(group, hp, items):
    if hp == 1:
        return _step_slope(group, hp, 0)
    return jnp.stack([jnp.full((1, 1), _alibi_slope(group, hh), F32) for hh, _ in items], axis=0)


def _window_bias(dil, n):
    a = lax.broadcasted_iota(jnp.int32, (ATT_BLOCK, 2 * ATT_BLOCK), 0)
    b = lax.broadcasted_iota(jnp.int32, (ATT_BLOCK, 2 * ATT_BLOCK), 1)
    dist = ATT_BLOCK + a - b
    valid = (dist >= 0) & (dist <= ATT_BLOCK) & ((b >= ATT_BLOCK) | (n > 0))
    return (dist * dil).astype(F32), valid


def _attn_fwd(qb, kb, vb, group):
    s = qb.shape[0]
    dil, nb, hp = _attn_plan(s, group)
    qkv, per_head = _attn_specs(group, dil, nb, hp)

    def body(q_ref, kp_ref, kc_ref, vp_ref, vc_ref, o_ref, lse_ref):
        n = pl.program_id(1)
        distd, valid = _window_bias(dil, n)
        items = _attn_items(hp, dil)
        sub = lambda ref: _attn_stack(ref, items, dil)
        kk = jnp.concatenate([sub(kp_ref), sub(kc_ref)], axis=1)
        vv = jnp.concatenate([sub(vp_ref), sub(vc_ref)], axis=1)
        sc = _bmm_nt(sub(q_ref), kk) * DIL_DH ** -0.5 - _attn_slopes(group, hp, items) * distd
        sc = jnp.where(valid, sc, -1e30)
        mx = jnp.max(sc, axis=-1, keepdims=True)
        p = jnp.where(valid, jnp.exp(sc - mx), 0.0)
        den = jnp.sum(p, axis=-1, keepdims=True)
        out = _bmm(p, vv) / den
        lse = mx + jnp.log(den)
        for b, (hh, r) in enumerate(items):
            _set_sub_rows(o_ref, r, dil, _head_cols(hh), out[b])
            _set_sub_rows(lse_ref, r, dil, _head_cols(hh), jnp.broadcast_to(lse[b], (ATT_BLOCK, LANES)))

    return _pcall(
        body, name=f"attn_fwd_g{group}", grid=(DIL_HEADS // hp, nb),
        in_specs=[qkv(0), qkv(-1), qkv(0), qkv(-1), qkv(0)], out_specs=[per_head(0)] * 2,
        out_shape=[jax.ShapeDtypeStruct((s, DIL_W), F32)] * 2,
        compiler_params=_params("parallel", "parallel"),
    )(qb, kb, kb, vb, vb)


def _attn_bwd(qb, kb, vb, d_o, lse, delta, group):
    s = qb.shape[0]
    dil, nb, hp = _attn_plan(s, group)
    qkv, per_head = _attn_specs(group, dil, nb, hp)
    scale = DIL_DH ** -0.5

    def body(q_ref, kp_ref, kc_ref, vp_ref, vc_ref, do_ref, l_ref, dl_ref, dq_ref, dk_ref, dv_ref,
             dq_acc, dk_done, dv_done, dk_carry, dv_carry):
        n = pl.program_id(1)
        items = _attn_items(hp, dil)
        slopes = _attn_slopes(group, hp, items)

        @pl.when(n == 0)
        def _():
            dk_carry[...] = jnp.zeros_like(dk_carry)
            dv_carry[...] = jnp.zeros_like(dv_carry)

        @pl.when(n < nb)
        def _():
            distd, valid = _window_bias(dil, n)
            sub = lambda ref, dtype=MXU: _attn_stack(ref, items, dil, dtype)
            qc, do = sub(q_ref), sub(do_ref)
            kk = jnp.concatenate([sub(kp_ref), sub(kc_ref)], axis=1)
            vv = jnp.concatenate([sub(vp_ref), sub(vc_ref)], axis=1)
            sc = _bmm_nt(qc, kk) * scale - slopes * distd
            p = jnp.where(valid, jnp.exp(jnp.minimum(sc - jnp.concatenate([sub(l_ref, F32)] * 2, axis=2), 0.0)), 0.0)
            dsc = p * (_bmm_nt(do, vv) - jnp.concatenate([sub(dl_ref, F32)] * 2, axis=2))
            dq = _bmm(dsc, kk) * scale
            dkk = _bmm_tn(dsc, qc) * scale
            dvv = _bmm_tn(p, do)
            for b, (hh, r) in enumerate(items):
                cols = _head_cols(hh)
                _set_sub_rows(dq_acc, r, dil, cols, dq[b])
                _set_sub_rows(dk_done, r, dil, cols, _sub_rows(dk_carry, r, dil, cols) + dkk[b, :ATT_BLOCK])
                _set_sub_rows(dv_done, r, dil, cols, _sub_rows(dv_carry, r, dil, cols) + dvv[b, :ATT_BLOCK])
                _set_sub_rows(dk_carry, r, dil, cols, dkk[b, ATT_BLOCK:])
                _set_sub_rows(dv_carry, r, dil, cols, dvv[b, ATT_BLOCK:])
            dq_ref[...] = dq_acc[...].astype(dq_ref.dtype)
            dk_ref[...] = dk_done[...].astype(dk_ref.dtype)
            dv_ref[...] = dv_done[...].astype(dv_ref.dtype)

        @pl.when(n == nb)
        def _():
            dk_ref[...] = dk_carry[...].astype(dk_ref.dtype)
            dv_ref[...] = dv_carry[...].astype(dv_ref.dtype)

    return _pcall(
        body, name=f"attn_bwd_g{group}", grid=(DIL_HEADS // hp, nb + 1),
        in_specs=[qkv(0), qkv(-1), qkv(0), qkv(-1), qkv(0)] + [per_head(0)] * 3,
        out_specs=[per_head(0), per_head(-1), per_head(-1)],
        out_shape=[jax.ShapeDtypeStruct((s, DIL_W), MXU)] * 3,
        scratch_shapes=[pltpu.VMEM((ATT_BLOCK * dil, hp * LANES), F32)] * 5,
        compiler_params=_params("parallel", "arbitrary"),
    )(qb, kb, kb, vb, vb, d_o, lse, delta)


def _my_place():
    mx, my, mc = lax.axis_index("x"), lax.axis_index("y"), lax.axis_index("c")
    return mx, my, mc, 4 * mx + 2 * my + mc


N_CHIPS = 4


def _shard_row_tile(r):
    if r <= 512:
        return r
    return 128 if r % 128 == 0 else 480


def _other_chips(mx, my):
    return [(1 - mx, my), (mx, 1 - my), (1 - mx, 1 - my)]


def _all_gather(xs, name):
    n = len(xs)
    halved = [x.shape[1] % (2 * LANES) == 0 and x.size * x.dtype.itemsize >= (1 << 20) for x in xs]
    n_sems = 8

    def body(*refs):
        x_refs, o_refs = refs[:n], refs[n:2 * n]
        send_sems, recv_sems, local_sems = refs[2 * n:]
        mx, my, mc, me = _my_place()
        sibling, sibling_id = (mx, my, 1 - mc), 4 * mx + 2 * my + (1 - mc)
        x_nbr, y_nbr, diag = _other_chips(mx, my)
        slot_of = lambda chip, c: 4 * chip[0] + 2 * chip[1] + c

        def part(ref, a, half):
            if not halved[a]:
                return ref
            width = xs[a].shape[1] // 2
            return ref.at[:, pl.ds(half * width, width)]

        def copy(a, k, dst, to, src=None):
            return pltpu.make_async_remote_copy(
                src_ref=dst if src is None else src, dst_ref=dst, send_sem=send_sems.at[a, k],
                recv_sem=recv_sems.at[a, k], device_id=to, device_id_type=MESH)

        local = [pltpu.make_async_copy(x_refs[a], o_refs[a].at[me], local_sems.at[a]) for a in range(n)]
        for cp in local:
            cp.start()
        sends = []
        for a in range(n):
            mine = o_refs[a].at[me]
            sends += [copy(a, 0, mine, sibling, src=x_refs[a]), copy(a, 1, mine, (*x_nbr, mc), src=x_refs[a]),
                      copy(a, 2, mine, (*y_nbr, mc), src=x_refs[a])]
        for cp in sends:
            cp.start()
        for a in range(n):
            blk = o_refs[a].at[slot_of(x_nbr, mc)]
            copy(a, 1, blk, (*x_nbr, mc)).wait_recv()
            sends += [copy(a, 3, blk, sibling), copy(a, 5, part(blk, a, 0), (*y_nbr, mc))]
            sends[-2].start()
            sends[-1].start()
        for a in range(n):
            blk = o_refs[a].at[slot_of(y_nbr, mc)]
            copy(a, 2, blk, (*y_nbr, mc)).wait_recv()
            sends.append(copy(a, 4, blk, sibling))
            sends[-1].start()
            if halved[a]:
                sends.append(copy(a, 6, part(blk, a, 1), (*x_nbr, mc)))
                sends[-1].start()
        for a in range(n):
            blk = o_refs[a].at[slot_of(diag, mc)]
            copy(a, 5, part(blk, a, 0), (*y_nbr, mc)).wait_recv()
            if halved[a]:
                copy(a, 6, part(blk, a, 1), (*x_nbr, mc)).wait_recv()
            sends.append(copy(a, 7, blk, sibling))
            sends[-1].start()
        for a in range(n):
            copy(a, 0, o_refs[a].at[sibling_id], sibling).wait_recv()
            for k, chip in ((3, x_nbr), (4, y_nbr), (7, diag)):
                copy(a, k, o_refs[a].at[slot_of(chip, 1 - mc)], sibling).wait_recv()
        for cp in sends:
            cp.wait_send()
        for cp in local:
            cp.wait()

    any_spec = pl.BlockSpec(memory_space=pl.ANY)
    return _pcall(
        body, name=name,
        in_specs=[any_spec] * n, out_specs=[any_spec] * n,
        out_shape=[jax.ShapeDtypeStruct((N_DEV,) + x.shape, x.dtype) for x in xs],
        scratch_shapes=[pltpu.SemaphoreType.DMA((n, n_sems)), pltpu.SemaphoreType.DMA((n, n_sems)),
                        pltpu.SemaphoreType.DMA((n,))],
    )(*xs)


def _pair_exchange(gs, name):
    n = len(gs)

    def body(*refs):
        g_refs, o_refs = refs[:n], refs[n:2 * n]
        send_sems, recv_sems = refs[2 * n:]
        mx, my, mc, _ = _my_place()
        copies = [pltpu.make_async_remote_copy(
            src_ref=g_refs[a].at[p, 1 - mc], dst_ref=o_refs[a].at[p], send_sem=send_sems.at[a, p],
            recv_sem=recv_sems.at[a, p], device_id=(mx, my, 1 - mc), device_id_type=MESH)
            for a in range(n) for p in range(N_CHIPS)]
        for cp in copies:
            cp.start()
        for cp in copies:
            cp.wait()

    any_spec = pl.BlockSpec(memory_space=pl.ANY)
    return _pcall(
        body, name=name,
        in_specs=[any_spec] * n, out_specs=[any_spec] * n,
        out_shape=[jax.ShapeDtypeStruct((N_CHIPS,) + g.shape[2:], g.dtype) for g in gs],
        scratch_shapes=[pltpu.SemaphoreType.DMA((n, N_CHIPS)), pltpu.SemaphoreType.DMA((n, N_CHIPS))],
    )(*gs)


def _pair_add(g, other, name):
    chips, _, r, c = g.shape
    tr = _shard_row_tile(r)
    core = lax.axis_index("c").astype(jnp.int32).reshape(1)

    def body(core_ref, g_ref, o_ref, h_ref):
        h_ref[...] = (g_ref[...].astype(F32)[0] + o_ref[...].astype(F32)).astype(h_ref.dtype)

    blk = pl.BlockSpec((1, tr, c), lambda p, i, core_ref: (p, i, 0))
    return _pcall(
        body, name=name,
        grid_spec=pltpu.PrefetchScalarGridSpec(
            num_scalar_prefetch=1, grid=(chips, pl.cdiv(r, tr)),
            in_specs=[pl.BlockSpec((1, 1, tr, c), lambda p, i, core_ref: (p, core_ref[0], i, 0)), blk],
            out_specs=blk),
        out_shape=jax.ShapeDtypeStruct((chips, r, c), g.dtype),
        compiler_params=_params("parallel", "parallel"),
    )(core, g, other)


def _chip_exchange(hs, name):
    n = len(hs)

    def body(*refs):
        h_refs, o_refs = refs[:n], refs[n:2 * n]
        send_sems, recv_sems, local_sems = refs[2 * n:]
        mx, my, mc, _ = _my_place()
        my_chip = 2 * mx + my
        chips = _other_chips(mx, my)
        local = [pltpu.make_async_copy(h_refs[a].at[my_chip], o_refs[a].at[my_chip], local_sems.at[a]) for a in range(n)]
        for cp in local:
            cp.start()
        for j, (px, py) in enumerate(chips):
            for a in range(n):
                pltpu.make_async_remote_copy(
                    src_ref=h_refs[a].at[2 * px + py], dst_ref=o_refs[a].at[my_chip], send_sem=send_sems.at[a, j],
                    recv_sem=recv_sems.at[a, j], device_id=(px, py, mc), device_id_type=MESH).start()
        for j, (px, py) in enumerate(chips):
            for a in range(n):
                pltpu.make_async_remote_copy(
                    src_ref=h_refs[a].at[2 * px + py], dst_ref=o_refs[a].at[2 * px + py], send_sem=send_sems.at[a, j],
                    recv_sem=recv_sems.at[a, j], device_id=(px, py, mc), device_id_type=MESH).wait()
        for cp in local:
            cp.wait()

    any_spec = pl.BlockSpec(memory_space=pl.ANY)
    return _pcall(
        body, name=name,
        in_specs=[any_spec] * n, out_specs=[any_spec] * n,
        out_shape=[jax.ShapeDtypeStruct(h.shape, h.dtype) for h in hs],
        scratch_shapes=[pltpu.SemaphoreType.DMA((n, N_CHIPS - 1)), pltpu.SemaphoreType.DMA((n, N_CHIPS - 1)),
                        pltpu.SemaphoreType.DMA((n,))],
    )(*hs)


def _adamw(parts, w, m, v, name):
    r, c = w.shape
    n_parts = parts.shape[0]
    tr = _shard_row_tile(r)
    bc1 = 1.0 - ADAM_B1 ** ADAM_STEP
    bc2 = 1.0 - ADAM_B2 ** ADAM_STEP

    def body(p_ref, w_ref, m_ref, v_ref, g_ref, d_ref, nm_ref, nv_ref):
        g = p_ref[0].astype(F32)
        for j in range(1, n_parts):
            g = g + p_ref[j].astype(F32)
        nm = ADAM_B1 * m_ref[...] + (1.0 - ADAM_B1) * g
        nv = ADAM_B2 * v_ref[...] + (1.0 - ADAM_B2) * (g * g)
        g_ref[...] = g
        nm_ref[...] = nm
        nv_ref[...] = nv
        d_ref[...] = -ADAM_LR * ((nm / bc1) / (jnp.sqrt(nv / bc2) + ADAM_EPS) + ADAM_WD * w_ref[...])

    blk = pl.BlockSpec((tr, c), lambda i: (i, 0))
    return _pcall(
        body, name=name, grid=(pl.cdiv(r, tr),),
        in_specs=[pl.BlockSpec((n_parts, tr, c), lambda i: (0, i, 0)), blk, blk, blk],
        out_specs=[blk] * 4, out_shape=[jax.ShapeDtypeStruct((r, c), F32)] * 4,
        compiler_params=_params("parallel"),
    )(parts, w, m, v)


def _local_step(x, target, norm_w, w_segs, conv_w, a_log, dt_bias, dn_norm_w, w_o_dn, w_o_dil, w_out, final_norm_w):
    s = x.shape[0]
    w_qkv, w_za, w_ba, w_qb, w_kb, w_vb, w_zb, w_ga, w_gb = w_segs
    conv_w8 = jnp.concatenate([conv_w, jnp.zeros((SUBLANES - conv_w.shape[0], QKV_W), F32)], axis=0)
    pad8 = jnp.zeros((1, DN_HEADS), F32)
    alog_row = jnp.concatenate([pad8, a_log, jnp.zeros((1, LANES - 2 * DN_HEADS), F32)], axis=1)
    dtb_row = jnp.concatenate([pad8, dt_bias, jnp.zeros((1, LANES - 2 * DN_HEADS), F32)], axis=1)
    wf_row = final_norm_w.reshape(1, D_MODEL)

    hb, qkv_pre, z_a, ba, z_b = _rms_proj_fwd(x, norm_w, [w_qkv, w_za, w_ba, w_zb], "rms_proj_fwd_a")
    q_b, k_b, v_b, g_a, g_b = _mm_out(hb, [w_qb, w_kb, w_vb, w_ga, w_gb], "proj_fwd_b", w_is_out_by_in=True)

    u_d, w_d, qd_d, kd_d, aqk_d, dl_d, t2_d, qn, kn, vn, bg = _delta_prep(qkv_pre, ba, conv_w8, alog_row, dtb_row)
    parts, lses = [], []
    for gi in range(N_DIL):
        o_g, l_g = _attn_fwd(q_b, k_b, v_b, gi)
        parts.append(o_g)
        lses.append(l_g)
    o_a, vnew_d, st_d, on_b, lse, o_joint, ob_b = _delta_scan_fwd(
        u_d, w_d, qd_d, kd_d, aqk_d, dl_d, z_a, dn_norm_w, parts, lses, z_b)

    loss8, dwf8, merged_b, dx2_b, dx2, dya_b, dyb_b, dga_b, dgb_b = _merge_out_final(
        g_a, g_b, on_b, ob_b, w_o_dn, w_o_dil, x, target, w_out, wf_row)

    g_w_out = _mm_tn(merged_b, dx2_b, "out_wgrad")
    g_w_o_dn = _mm_tn(on_b, dya_b, "out_dn_wgrad")

    g_w_o_dil = _mm_tn(ob_b, dyb_b, "out_dil_wgrad")
    dvnew_d, dkd_d, ddl_d, d_o_a, dza_b, ddnw8, d_o, dzb_b, delta = _delta_scan_bwd(
        w_d, qd_d, kd_d, aqk_d, dl_d, vnew_d, st_d, dya_b, o_a, z_a, dn_norm_w, w_o_dn, dyb_b, o_joint, z_b, w_o_dil)
    dqs, dks, dvs = [], [], []
    for gi in range(N_DIL):
        dq_g, dk_g, dv_g = _attn_bwd(q_b, k_b, v_b, d_o, lse, delta, gi)
        dqs.append(dq_g)
        dks.append(dk_g)
        dvs.append(dv_g)

    dqkv_b, dba_b, dsmall8, dconv8 = _delta_post_bwd(qn, kn, vn, bg, t2_d, st_d, vnew_d, d_o_a, dvnew_d, dkd_d, ddl_d,
                                                     qkv_pre, ba, conv_w8, alog_row, dtb_row)

    per_group = lambda w: [w[g * DIL_W:(g + 1) * DIL_W] for g in range(N_DIL)]
    dh_b = _mm_in(dqs + dks + dvs + [dga_b, dgb_b],
                  per_group(w_qb) + per_group(w_kb) + per_group(w_vb) + [w_ga, w_gb], "proj_bwd_b", w_is_out_by_in=True)
    dsegs = [dqkv_b, dza_b, dba_b] + dqs + dks + dvs + [dzb_b, dga_b, dgb_b]
    valid_rows = [d.shape[1] for d in dsegs]
    valid_rows[2] = 2 * DN_HEADS
    g_wt = _proj_wgrad_all(dsegs, valid_rows, hb)
    grad_x, dnw8 = _proj_bwd_rms_in([dqkv_b, dza_b, dba_b, dzb_b], [w_qkv, w_za, w_ba, w_zb], dh_b, x, dx2, norm_w)

    small = dict(norm_w=dnw8[0:1], final_norm_w=dwf8[0:1], dn_norm_w=ddnw8[0:1],
                 a_log=dsmall8[0:1, DN_HEADS:2 * DN_HEADS], dt_bias=dsmall8[1:2, DN_HEADS:2 * DN_HEADS])
    return loss8[0:1, 0:1], grad_x, g_wt, dconv8[0:4], g_w_o_dn, g_w_o_dil, g_w_out, small


def _proj_bwd_rms_in(ds, ws, dh_a, x, dx2, norm_w):
    n_seg = len(ds)

    def body(*refs):
        d_refs, w_refs = refs[:n_seg], refs[n_seg:2 * n_seg]
        da_ref, x_ref, dx2_ref, w_ref, dx_ref, dw_ref = refs[2 * n_seg:]
        dx_ref[...] = da_ref[...]
        for d_ref, wt_ref in zip(d_refs, w_refs):
            for c, wd in _col_chunks(d_ref.shape[1], 1024):
                dx_ref[...] += jnp.dot(d_ref[:, c:c + wd], wt_ref[c:c + wd, :], preferred_element_type=F32)
        xv = x_ref[...]
        r = lax.rsqrt(jnp.mean(xv * xv, axis=-1, keepdims=True) + NORM_EPS)
        dhv = dx_ref[...]
        dn = dhv * w_ref[...]
        dx_ref[...] = dx2_ref[...] + r * dn - xv * (r * r * r) * jnp.mean(dn * xv, axis=-1, keepdims=True)
        row = jnp.sum(dhv * xv * r, axis=0, keepdims=True)
        _acc_add(dw_ref, jnp.concatenate([row, jnp.zeros((SUBLANES - 1, row.shape[1]), F32)], axis=0))

    return _rows_call(body, "proj_bwd_b_rms_in", x.shape[0],
                      [(d, "tile") for d in ds] + [(w, "full") for w in ws]
                      + [(dh_a, "tile"), (x, "tile"), (dx2, "tile"), (norm_w, "full")],
                      [(x.shape, F32, "tile"), ((SUBLANES, x.shape[1]), F32, "acc")])


def _split_proj_rows(w_shards):
    n_shards, rows, k = w_shards.shape
    wt_full = w_shards.reshape(n_shards * rows, k)
    offs = [0]
    for n in PROJ_SIZES:
        offs.append(offs[-1] + n)
    seg = lambda a, b: wt_full[offs[a]:offs[b]]
    w_ba = jnp.concatenate([seg(4, 6), jnp.zeros((LANES - 2 * DN_HEADS, k), wt_full.dtype)], axis=0)
    return [seg(0, 3), seg(3, 4), w_ba, seg(6, 7), seg(7, 8), seg(8, 9), seg(9, 10), seg(10, 11), seg(11, 12)]


LOSS_ROW = 5


def _pack_small(norm_w, final_norm_w, dn_norm_w, a_log, dt_bias, loss=None):
    pad = lambda r: jnp.concatenate([r, jnp.zeros((1, D_MODEL - r.shape[1]), F32)], axis=1)
    rows = [pad(norm_w.reshape(1, -1)), pad(final_norm_w.reshape(1, -1)), pad(dn_norm_w.reshape(1, -1)),
            pad(a_log.reshape(1, -1)), pad(dt_bias.reshape(1, -1)),
            pad(jnp.zeros((1, 1), F32) if loss is None else loss.reshape(1, 1)),
            jnp.zeros((SUBLANES - LOSS_ROW - 1, D_MODEL), F32)]
    return jnp.concatenate(rows, axis=0)


def _unpack_small(p):
    return dict(norm_w=p[0:1], final_norm_w=p[1], dn_norm_w=p[2:3, :DN_DK], a_log=p[3:4, :DN_HEADS],
                dt_bias=p[4:5, :DN_HEADS])


def kernel(x, norm_w, w_in, conv_w, a_log, dt_bias, dn_norm_w, w_o_dn, w_o_dil, w_out, final_norm_w, loss_target, m_norm_w, m_w_in, m_conv_w, m_a_log, m_dt_bias, m_dn_norm_w, m_w_o_dn, m_w_o_dil, m_w_out, m_final_norm_w, v_norm_w, v_w_in, v_conv_w, v_a_log, v_dt_bias, v_dn_norm_w, v_w_o_dn, v_w_o_dil, v_w_out, v_final_norm_w):
    shard_w = w_in.shape[2]
    wt, m_wt, v_wt = (jnp.transpose(t[0]) for t in (w_in, m_w_in, v_w_in))
    gathered = _all_gather([wt.astype(MXU), w_o_dn[0].astype(MXU), w_o_dil[0].astype(MXU), w_out[0].astype(MXU),
                            conv_w[0]], "gather_weights")
    w_in_all, w_o_dn_all, w_o_dil_all, w_out_all, conv_all = gathered
    w_o_dn_full = w_o_dn_all.reshape(D_MODEL, D_MODEL)
    w_o_dil_full = jnp.transpose(w_o_dil_all, (1, 0, 2)).reshape(DIL_W, D_MODEL)
    w_out_full = w_out_all.reshape(D_MODEL, D_MODEL)
    conv_full = jnp.transpose(conv_all, (1, 0, 2)).reshape(conv_w.shape[1], QKV_W)

    loss11, grad_x, g_wt, g_conv, g_w_o_dn, g_w_o_dil, g_w_out, small = _local_step(
        x[0], loss_target[0], norm_w, _split_proj_rows(w_in_all), conv_full, a_log, dt_bias, dn_norm_w,
        w_o_dn_full, w_o_dil_full, w_out_full, final_norm_w)

    col_shards = lambda g, n: jnp.transpose(g.reshape(g.shape[0], N_DEV, n), (1, 0, 2))
    row_shards = lambda g: g.reshape(N_DEV, g.shape[0] // N_DEV, g.shape[1])
    g_wt_shards = jnp.stack([g_wt[j * shard_w:(j + 1) * shard_w] for j in range(N_DEV)], axis=0)
    sent = [g_wt_shards, row_shards(g_w_o_dn).astype(MXU),
            col_shards(g_w_o_dil, w_o_dil.shape[2]).astype(MXU), row_shards(g_w_out).astype(MXU),
            col_shards(g_conv, conv_w.shape[2])]
    sent = [g8.reshape((N_CHIPS, 2) + g8.shape[1:]) for g8 in sent]
    from_sibling = _pair_exchange(sent, "scatter_pair")
    summed = [_pair_add(g, o, f"pair_add_{i}") for i, (g, o) in enumerate(zip(sent, from_sibling))]
    p_w_in, p_w_o_dn, p_w_o_dil, p_w_out, p_conv = _chip_exchange(summed, "scatter_chips")
    p_small = _all_gather([_pack_small(small["norm_w"], small["final_norm_w"], small["dn_norm_w"], small["a_log"],
                                       small["dt_bias"], loss11)], "gather_small_grads")[0]

    res = {}
    res["w_in"] = [jnp.transpose(t) for t in _adamw(p_w_in, wt, m_wt, v_wt, "adamw_w_in")]
    res["conv_w"] = _adamw(p_conv, conv_w[0], m_conv_w[0], v_conv_w[0], "adamw_conv_w")
    res["w_o_dn"] = _adamw(p_w_o_dn, w_o_dn[0], m_w_o_dn[0], v_w_o_dn[0], "adamw_w_o_dn")
    res["w_o_dil"] = _adamw(p_w_o_dil, w_o_dil[0], m_w_o_dil[0], v_w_o_dil[0], "adamw_w_o_dil")
    res["w_out"] = _adamw(p_w_out, w_out[0], m_w_out[0], v_w_out[0], "adamw_w_out")
    small_res = _adamw(p_small, _pack_small(norm_w, final_norm_w, dn_norm_w, a_log, dt_bias),
                       _pack_small(m_norm_w, m_final_norm_w, m_dn_norm_w, m_a_log, m_dt_bias),
                       _pack_small(v_norm_w, v_final_norm_w, v_dn_norm_w, v_a_log, v_dt_bias), "adamw_small")
    loss = small_res[0][LOSS_ROW, 0]
    small_res = [_unpack_small(t) for t in small_res]

    names = ["norm_w", "w_in", "conv_w", "a_log", "dt_bias", "dn_norm_w", "w_o_dn", "w_o_dil", "w_out", "final_norm_w"]
    outs = [loss, grad_x[None]]
    for kind in range(4):
        for nm in names:
            outs.append(res[nm][kind][None] if nm in res else small_res[kind][nm])
    return tuple(outs)
```

```python
import math

import jax
import jax.numpy as jnp
from jax import lax
from jax.experimental import pallas as pl
from jax.experimental.pallas import tpu as pltpu

F32 = jnp.float32
MXU = jnp.bfloat16
MESH = pl.DeviceIdType.MESH

N_DEV = 8
D_MODEL = 1024
DN_HEADS = 8
DN_DK = 128
DN_CHUNK = 64
N_DIL = 3
DIL_HEADS = 4
DIL_DH = 128
DIL_W = DIL_HEADS * DIL_DH
DIL_GROUPS = ((128, 1), (512, 4), (2048, 16))
ATT_BLOCK = 128
NORM_EPS = 1e-6
QKV_W = 3 * D_MODEL
DILQ_W = N_DIL * DIL_W
PROJ_SIZES = (1024, 1024, 1024, 1024, 8, 8, DILQ_W, DILQ_W, DILQ_W, DIL_W, D_MODEL, D_MODEL)

ADAM_LR = 0.001
ADAM_B1 = 0.9
ADAM_B2 = 0.999
ADAM_EPS = 1e-08
ADAM_WD = 0.01
ADAM_STEP = 10

ROW_TILE = 256
LANES = 128
SUBLANES = 8
VMEM_LIMIT = 48 << 20


def _pcall(body, **kw):
    return pl.pallas_call(body, **kw)


def _params(*sem):
    return pltpu.CompilerParams(dimension_semantics=tuple(sem), vmem_limit_bytes=VMEM_LIMIT)


def _sigmoid(x):
    return 0.5 * jnp.tanh(0.5 * x) + 0.5


def _softplus(x):
    return jnp.maximum(x, 0.0) + jnp.log(1.0 + jnp.exp(-jnp.abs(x)))


def _dot(a, b):
    return jnp.dot(a.astype(MXU), b.astype(MXU), preferred_element_type=F32)


def _dot_nt(a, b):
    return lax.dot_general(a.astype(MXU), b.astype(MXU), (((1,), (1,)), ((), ())), preferred_element_type=F32)


def _dot_tn(a, b):
    return lax.dot_general(a.astype(MXU), b.astype(MXU), (((0,), (0,)), ((), ())), preferred_element_type=F32)


def _split3(x):
    hi = x.astype(jnp.bfloat16)
    r1 = x - hi.astype(F32)
    mid = r1.astype(jnp.bfloat16)
    lo = (r1 - mid.astype(F32)).astype(jnp.bfloat16)
    return hi, mid, lo


def _dot01(m01, x):
    m = m01.astype(jnp.bfloat16)
    hi, mid, lo = _split3(x)
    f = lambda p: jnp.dot(m, p, preferred_element_type=F32)
    return f(hi) + (f(mid) + f(lo))


def _rows_call(body, name, n_rows, ins, outs, scratch=(), tm=ROW_TILE):
    steps = n_rows // tm
    per8 = tm // SUBLANES
    last8 = n_rows // SUBLANES - 1
    in_specs = []
    for arr, kind in ins:
        cols = arr.shape[-1]
        if kind == "tile":
            in_specs.append(pl.BlockSpec((tm, cols), lambda i: (i, 0)))
        elif kind == "full":
            in_specs.append(pl.BlockSpec(arr.shape, lambda i, nd=arr.ndim: (0,) * nd))
        elif kind == "prev8":
            in_specs.append(pl.BlockSpec((SUBLANES, cols), lambda i: (jnp.maximum(i * per8 - 1, 0), 0)))
        elif kind == "next8":
            in_specs.append(pl.BlockSpec((SUBLANES, cols), lambda i: (jnp.minimum((i + 1) * per8, last8), 0)))
        else:
            raise ValueError(kind)
    out_specs, out_shape, has_acc = [], [], False
    for shape, dtype, kind in outs:
        out_shape.append(jax.ShapeDtypeStruct(shape, dtype))
        if kind == "tile":
            out_specs.append(pl.BlockSpec((tm, shape[-1]), lambda i: (i, 0)))
        else:
            has_acc = True
            out_specs.append(pl.BlockSpec(shape, lambda i: (0, 0)))
    return _pcall(
        body, name=name, grid=(steps,), in_specs=in_specs, out_specs=out_specs, out_shape=out_shape,
        scratch_shapes=list(scratch),
        compiler_params=_params("arbitrary" if has_acc else "parallel"),
    )(*[a for a, _ in ins])


def _acc_add(ref, value):
    @pl.when(pl.program_id(0) == 0)
    def _():
        ref[...] = jnp.zeros_like(ref)
    ref[...] += value


def _col_chunks(n, width=512):
    return [(c, min(width, n - c)) for c in range(0, n, width)]


NT_DIMS = (((1,), (1,)), ((), ()))
TN_DIMS = (((0,), (0,)), ((), ()))


def _mm_out(a, ws, name, w_is_out_by_in=False, out_dtype=F32, tm=ROW_TILE):
    m, k = a.shape
    ns = [w.shape[0] if w_is_out_by_in else w.shape[1] for w in ws]

    def body(a_ref, *refs):
        av = a_ref[...]
        for w_ref, o_ref, n in zip(refs[:len(ws)], refs[len(ws):], ns):
            for c, wd in _col_chunks(n):
                if w_is_out_by_in:
                    part = lax.dot_general(av, w_ref[c:c + wd, :], NT_DIMS, preferred_element_type=F32)
                else:
                    part = jnp.dot(av, w_ref[:, c:c + wd], preferred_element_type=F32)
                o_ref[:, c:c + wd] = part.astype(o_ref.dtype)

    return _pcall(
        body, name=name, grid=(m // tm,),
        in_specs=[pl.BlockSpec((tm, k), lambda i: (i, 0))] + [pl.BlockSpec(w.shape, lambda i: (0, 0)) for w in ws],
        out_specs=[pl.BlockSpec((tm, n), lambda i: (i, 0)) for n in ns],
        out_shape=[jax.ShapeDtypeStruct((m, n), out_dtype) for n in ns],
        compiler_params=_params("parallel"),
    )(a, *ws)


def _rms_proj_fwd(x, norm_w, wts, name, tm=ROW_TILE):
    m, k = x.shape
    ns = [w.shape[0] for w in wts]

    def body(x_ref, nw_ref, *refs):
        w_refs, h_ref, o_refs = refs[:len(wts)], refs[len(wts)], refs[len(wts) + 1:]
        xv = x_ref[...]
        r = lax.rsqrt(jnp.mean(xv * xv, axis=-1, keepdims=True) + NORM_EPS)
        hv = (xv * r * nw_ref[...]).astype(h_ref.dtype)
        h_ref[...] = hv
        for w_ref, o_ref, n in zip(w_refs, o_refs, ns):
            for c, wd in _col_chunks(n):
                o_ref[:, c:c + wd] = lax.dot_general(hv, w_ref[c:c + wd, :], NT_DIMS, preferred_element_type=F32)

    return _pcall(
        body, name=name, grid=(m // tm,),
        in_specs=[pl.BlockSpec((tm, k), lambda i: (i, 0)), pl.BlockSpec(norm_w.shape, lambda i: (0, 0))]
        + [pl.BlockSpec(w.shape, lambda i: (0, 0)) for w in wts],
        out_specs=[pl.BlockSpec((tm, k), lambda i: (i, 0))] + [pl.BlockSpec((tm, n), lambda i: (i, 0)) for n in ns],
        out_shape=[jax.ShapeDtypeStruct((m, k), MXU)] + [jax.ShapeDtypeStruct((m, n), F32) for n in ns],
        compiler_params=_params("parallel"),
    )(x, norm_w, *wts)


def _mm_in(ds, ws, name, w_is_out_by_in=False, tm=ROW_TILE):
    m = ds[0].shape[0]
    k = ws[0].shape[1] if w_is_out_by_in else ws[0].shape[0]
    ns = [d.shape[1] for d in ds]

    def body(*refs):
        d_refs, w_refs, o_ref = refs[:len(ds)], refs[len(ds):2 * len(ds)], refs[-1]
        first = True
        for d_ref, w_ref, n in zip(d_refs, w_refs, ns):
            for c, wd in _col_chunks(n, 1024):
                if w_is_out_by_in:
                    part = jnp.dot(d_ref[:, c:c + wd], w_ref[c:c + wd, :], preferred_element_type=F32)
                else:
                    part = lax.dot_general(d_ref[:, c:c + wd], w_ref[:, c:c + wd], NT_DIMS, preferred_element_type=F32)
                if first:
                    o_ref[...] = part
                    first = False
                else:
                    o_ref[...] += part

    return _pcall(
        body, name=name, grid=(m // tm,),
        in_specs=[pl.BlockSpec((tm, n), lambda i: (i, 0)) for n in ns] + [pl.BlockSpec(w.shape, lambda i: (0, 0)) for w in ws],
        out_specs=pl.BlockSpec((tm, k), lambda i: (i, 0)),
        out_shape=jax.ShapeDtypeStruct((m, k), F32),
        compiler_params=_params("parallel"),
    )(*ds, *ws)


def _mm_tn(a, d, name):
    m, k = a.shape
    n = d.shape[1]
    tk = 512 if k % 512 == 0 else k

    def body(a_ref, d_ref, o_ref):
        o_ref[...] = lax.dot_general(a_ref[...], d_ref[...], TN_DIMS, preferred_element_type=F32)

    return _pcall(
        body, name=name, grid=(k // tk,),
        in_specs=[pl.BlockSpec((m, tk), lambda p: (0, p)), pl.BlockSpec((m, n), lambda p: (0, 0))],
        out_specs=pl.BlockSpec((tk, n), lambda p: (p, 0)),
        out_shape=jax.ShapeDtypeStruct((k, n), F32),
        compiler_params=_params("parallel"),
    )(a, d)


WGRAD_TILE = 512


def _proj_wgrad_all(dsegs, valid_rows, hb):
    m, k = hb.shape
    n_seg = len(dsegs)
    tiles, row = [], 0
    for si, (d, valid) in enumerate(zip(dsegs, valid_rows)):
        for c in range(0, valid, WGRAD_TILE):
            width = min(WGRAD_TILE, d.shape[1] - c)
            tiles.append((si, c, width, row + c, min(width, valid - c)))
        row += valid
    total_rows = row

    def body(*refs):
        d_refs, hb_ref, o_ref = refs[:n_seg], refs[n_seg], refs[n_seg + 1]
        a_buf, hb_buf, o_buf, load_sems, store_sems, hb_sem = refs[n_seg + 2:]

        def load(t):
            si, c, width, _, _ = tiles[t]
            return pltpu.make_async_copy(d_refs[si].at[:, pl.ds(c, width)], a_buf.at[t % 2, :, pl.ds(0, width)],
                                         load_sems.at[t % 2])

        def stores(t):
            _, _, _, orow, valid = tiles[t]
            return [pltpu.make_async_copy(o_buf.at[t % 2, pl.ds(0, valid), :], o_ref.at[pl.ds(orow, valid), :],
                                          store_sems.at[t % 2])]

        hb_copy = pltpu.make_async_copy(hb_ref, hb_buf, hb_sem)
        hb_copy.start()
        load(0).start()
        hb_copy.wait()
        for t in range(len(tiles)):
            width = tiles[t][2]
            load(t).wait()
            if t + 1 < len(tiles):
                load(t + 1).start()
            if t >= 2:
                for cp in stores(t - 2):
                    cp.wait()
            o_buf[t % 2, 0:width, :] = lax.dot_general(a_buf[t % 2, :, 0:width], hb_buf[...], TN_DIMS,
                                                        preferred_element_type=F32).astype(o_buf.dtype)
            for cp in stores(t):
                cp.start()
        for t in range(max(len(tiles) - 2, 0), len(tiles)):
            for cp in stores(t):
                cp.wait()

    any_spec = pl.BlockSpec(memory_space=pl.ANY)
    return _pcall(
        body, name="proj_wgrad",
        in_specs=[any_spec] * (n_seg + 1), out_specs=any_spec,
        out_shape=jax.ShapeDtypeStruct((total_rows, k), hb.dtype),
        scratch_shapes=[pltpu.VMEM((2, m, WGRAD_TILE), hb.dtype), pltpu.VMEM((m, k), hb.dtype),
                        pltpu.VMEM((2, WGRAD_TILE, k), hb.dtype), pltpu.SemaphoreType.DMA((2,)),
                        pltpu.SemaphoreType.DMA((2,)), pltpu.SemaphoreType.DMA],
        compiler_params=pltpu.CompilerParams(vmem_limit_bytes=VMEM_LIMIT),
    )(*dsegs, hb)


def _conv_taps(ext_ref, cw_ref, cols, tm):
    c = None
    for j in range(4):
        term = cw_ref[3 - j:4 - j, cols] * ext_ref[SUBLANES - j:SUBLANES - j + tm, cols]
        c = term if c is None else c + term
    return c


def _fill_ext(ext_ref, u_ref, halo_ref, first):
    ext_ref[0:SUBLANES, :] = jnp.where(first, 0.0, halo_ref[...])
    ext_ref[SUBLANES:, :] = u_ref[...]


def _dn_prep_fwd_tile(u_ref, halo_ref, cw_ref, ba_ref, al_ref, dtb_ref, q_ref, k_ref, v_ref, bg_ref, ext_ref, first):
    tm = u_ref.shape[0]
    _fill_ext(ext_ref, u_ref, halo_ref, first)
    for h in range(3 * DN_HEADS):
        cols = slice(h * LANES, (h + 1) * LANES)
        c = _conv_taps(ext_ref, cw_ref, cols, tm)
        a = c * _sigmoid(c)
        oc = slice((h % DN_HEADS) * LANES, (h % DN_HEADS + 1) * LANES)
        if h < 2 * DN_HEADS:
            rinv = lax.rsqrt(jnp.sum(a * a, axis=-1, keepdims=True) + NORM_EPS)
            if h < DN_HEADS:
                q_ref[:, oc] = a * (rinv * DN_DK ** -0.5)
            else:
                k_ref[:, oc] = a * rinv
        else:
            v_ref[:, oc] = a
    bav = ba_ref[...]
    lane = lax.broadcasted_iota(jnp.int32, bav.shape, 1)
    beta = _sigmoid(bav)
    g = -jnp.exp(al_ref[...]) * _softplus(bav + dtb_ref[...])
    bg_ref[...] = jnp.where(lane < DN_HEADS, beta, jnp.where(lane < 2 * DN_HEADS, g, 0.0))


def _dn_prep_bwd_tile(u_ref, halo_ref, cw_ref, ba_ref, al_ref, dtb_ref, dq_ref, dk_ref, dv_ref, dbg_ref,
                      dc_ref, dba_ref, ext_ref, first):
    tm = u_ref.shape[0]
    _fill_ext(ext_ref, u_ref, halo_ref, first)
    for h in range(3 * DN_HEADS):
        cols = slice(h * LANES, (h + 1) * LANES)
        oc = slice((h % DN_HEADS) * LANES, (h % DN_HEADS + 1) * LANES)
        c = _conv_taps(ext_ref, cw_ref, cols, tm)
        sg = _sigmoid(c)
        a = c * sg
        if h < 2 * DN_HEADS:
            rinv = lax.rsqrt(jnp.sum(a * a, axis=-1, keepdims=True) + NORM_EPS)
            dy = dq_ref[:, oc] * DN_DK ** -0.5 if h < DN_HEADS else dk_ref[:, oc]
            da = rinv * dy - a * (rinv * rinv * rinv) * jnp.sum(dy * a, axis=-1, keepdims=True)
        else:
            da = dv_ref[:, oc]
        dc_ref[:, cols] = da * (sg * (1.0 + c * (1.0 - sg)))
    bav = ba_ref[...]
    dbgv = dbg_ref[...]
    lane = lax.broadcasted_iota(jnp.int32, bav.shape, 1)
    beta = _sigmoid(bav)
    ea = jnp.exp(al_ref[...])
    z = bav + dtb_ref[...]
    g = -ea * _softplus(z)
    is_b = lane < DN_HEADS
    is_g = jnp.logical_and(lane >= DN_HEADS, lane < 2 * DN_HEADS)
    d_aa = jnp.where(is_g, dbgv * (-ea) * _sigmoid(z), 0.0)
    dba = jnp.where(is_b, dbgv * beta * (1.0 - beta), d_aa)
    dba_ref[...] = dba.astype(dba_ref.dtype)
    r_alog = jnp.sum(jnp.where(is_g, dbgv * g, 0.0), axis=0, keepdims=True)
    r_dtb = jnp.sum(d_aa, axis=0, keepdims=True)
    return jnp.concatenate([r_alog, r_dtb, jnp.zeros((SUBLANES - 2, LANES), F32)], axis=0)


def _conv_bwd_tile(extd_ref, ext_ref, cw_ref, du_ref, dcw_ref, tm):
    for h in range(3 * DN_HEADS):
        cols = slice(h * LANES, (h + 1) * LANES)
        du = None
        for j in range(4):
            term = cw_ref[3 - j:4 - j, cols] * extd_ref[j:j + tm, cols]
            du = term if du is None else du + term
        du_ref[:, cols] = du.astype(du_ref.dtype)
        dcv = extd_ref[0:tm, cols]
        for j in range(4):
            row = jnp.sum(dcv * ext_ref[SUBLANES - j:SUBLANES - j + tm, cols], axis=0, keepdims=True)
            dcw_ref[3 - j:4 - j, cols] += row


def _dn_out_fwd_tile(o_ref, z_ref, w_ref, wo_ref, on_ref, y_ref):
    for h in range(DN_HEADS):
        cols = _head_cols(h)
        ov = o_ref[:, cols]
        zv = z_ref[:, cols]
        ro = lax.rsqrt(jnp.mean(ov * ov, axis=-1, keepdims=True) + NORM_EPS)
        on_ref[:, cols] = (ov * ro * w_ref[...] * (zv * _sigmoid(zv))).astype(on_ref.dtype)
    y_ref[...] = jnp.dot(on_ref[...], wo_ref[...], preferred_element_type=F32)


def _dn_out_bwd_tile(dy_ref, o_ref, z_ref, w_ref, wo_ref, do_ref, dz_ref, d_ref):
    d_ref[...] = lax.dot_general(dy_ref[...], wo_ref[...], NT_DIMS, preferred_element_type=F32)
    acc = jnp.zeros((1, LANES), F32)
    for h in range(DN_HEADS):
        cols = _head_cols(h)
        dv, ov, zv = d_ref[:, cols], o_ref[:, cols], z_ref[:, cols]
        sg = _sigmoid(zv)
        sz = zv * sg
        ro = lax.rsqrt(jnp.mean(ov * ov, axis=-1, keepdims=True) + NORM_EPS)
        nv = ov * ro
        dn = dv * w_ref[...] * sz
        acc = acc + jnp.sum(dv * nv * sz, axis=0, keepdims=True)
        dz_ref[:, cols] = (dv * nv * w_ref[...] * (sg * (1.0 + zv * (1.0 - sg)))).astype(dz_ref.dtype)
        do_ref[:, cols] = ro * dn - ov * (ro * ro * ro) * jnp.mean(dn * ov, axis=-1, keepdims=True)
    return acc


def _attn_out_fwd_tile(o0, o1, o2, l0, l1, l2, z_ref, wo_ref, lse_ref, o_ref, g_ref, y_ref):
    a, b, c = l0[...], l1[...], l2[...]
    m = jnp.maximum(a, jnp.maximum(b, c))
    ea, eb, ec = jnp.exp(a - m), jnp.exp(b - m), jnp.exp(c - m)
    den = ea + eb + ec
    out = (ea * o0[...] + eb * o1[...] + ec * o2[...]) / den
    lse_ref[...] = m + jnp.log(den)
    o_ref[...] = out
    zv = z_ref[...]
    gated = (out * (zv * _sigmoid(zv))).astype(g_ref.dtype)
    g_ref[...] = gated
    y_ref[...] = jnp.dot(gated, wo_ref[...], preferred_element_type=F32)


def _attn_out_bwd_tile(dy_ref, o_ref, z_ref, wo_ref, do_ref, dz_ref, dl_ref):
    zv = z_ref[...]
    sg = _sigmoid(zv)
    dv = lax.dot_general(dy_ref[...], wo_ref[...], NT_DIMS, preferred_element_type=F32)
    ov = o_ref[...]
    do = dv * (zv * sg)
    do_ref[...] = do
    dz_ref[...] = (dv * ov * (sg * (1.0 + zv * (1.0 - sg)))).astype(dz_ref.dtype)
    for h in range(DIL_HEADS):
        cols = _head_cols(h)
        dl_ref[:, cols] = jnp.broadcast_to(jnp.sum(do[:, cols] * ov[:, cols], axis=-1, keepdims=True),
                                           (do.shape[0], LANES))


def _merge_out_final(ga, gb, ya, yb, x, target, w_out, wf_row):
    s, dm = x.shape

    def body(ga_ref, gb_ref, ya_ref, yb_ref, x_ref, t_ref, wo_ref, w_ref,
             loss_ref, dw_ref, m_ref, dxb_ref, dx_ref, dya_ref, dyb_ref, dga_ref, dgb_ref):
        sa, sb = _sigmoid(ga_ref[...]), _sigmoid(gb_ref[...])
        ya, yb = ya_ref[...], yb_ref[...]
        merged = (sa * ya + sb * yb).astype(MXU)
        m_ref[...] = merged
        x2 = x_ref[...] + jnp.dot(merged, wo_ref[...], preferred_element_type=F32)
        r = lax.rsqrt(jnp.mean(x2 * x2, axis=-1, keepdims=True) + NORM_EPS)
        w = w_ref[...]
        err = x2 * r * w - t_ref[...]
        tile_loss = 0.5 * jnp.sum(jnp.mean(err * err, axis=-1, keepdims=True), axis=0, keepdims=True)
        _acc_add(loss_ref, jnp.broadcast_to(tile_loss, (SUBLANES, LANES)))
        dy = err * (1.0 / dm)
        row = jnp.sum(dy * x2 * r, axis=0, keepdims=True)
        _acc_add(dw_ref, jnp.concatenate([row, jnp.zeros((SUBLANES - 1, dm), F32)], axis=0))
        dn = dy * w
        dx2 = r * dn - x2 * (r * r * r) * jnp.mean(dn * x2, axis=-1, keepdims=True)
        dx_ref[...] = dx2
        dxb = dx2.astype(MXU)
        dxb_ref[...] = dxb
        dmv = lax.dot_general(dxb, wo_ref[...], NT_DIMS, preferred_element_type=F32)
        dya_ref[...] = (dmv * sa).astype(dya_ref.dtype)
        dyb_ref[...] = (dmv * sb).astype(dyb_ref.dtype)
        dga_ref[...] = (dmv * ya * sa * (1.0 - sa)).astype(dga_ref.dtype)
        dgb_ref[...] = (dmv * yb * sb * (1.0 - sb)).astype(dgb_ref.dtype)

    return _rows_call(body, "merge_out_final", s,
                      [(ga, "tile"), (gb, "tile"), (ya, "tile"), (yb, "tile"), (x, "tile"), (target, "tile"),
                       (w_out, "full"), (wf_row, "full")],
                      [((SUBLANES, LANES), F32, "acc"), ((SUBLANES, dm), F32, "acc"), ((s, dm), MXU, "tile"),
                       ((s, dm), MXU, "tile"), ((s, dm), F32, "tile")] + [((s, dm), MXU, "tile")] * 4)


def _lane_pick(x, idx):
    lane = lax.broadcasted_iota(jnp.int32, x.shape, 1)
    return jnp.sum(jnp.where(lane == idx, x, 0.0), axis=-1, keepdims=True)


PAIR = 2 * DN_CHUNK
SCAN_CHUNKS = 4


def _bmm(a, b):
    return lax.dot_general(a.astype(MXU), b.astype(MXU), (((2,), (1,)), ((0,), (0,))), preferred_element_type=F32)


def _bmm_nt(a, b):
    return lax.dot_general(a.astype(MXU), b.astype(MXU), (((2,), (2,)), ((0,), (0,))), preferred_element_type=F32)


def _bmm_tn(a, b):
    return lax.dot_general(a.astype(MXU), b.astype(MXU), (((1,), (1,)), ((0,), (0,))), preferred_element_type=F32)


def _bmm3(a, b):
    ah = a.astype(jnp.bfloat16)
    al = (a - ah.astype(F32)).astype(jnp.bfloat16)
    bh = b.astype(jnp.bfloat16)
    bl = (b - bh.astype(F32)).astype(jnp.bfloat16)
    f = lambda p, q: lax.dot_general(p, q, (((2,), (1,)), ((0,), (0,))), preferred_element_type=F32)
    return f(ah, bh) + (f(ah, bl) + f(al, bh))


def _pair_masks():
    row = lax.broadcasted_iota(jnp.int32, (PAIR, PAIR), 0)
    col = lax.broadcasted_iota(jnp.int32, (PAIR, PAIR), 1)
    same = (row >= DN_CHUNK) == (col >= DN_CHUNK)
    return dict(causal=same & (row >= col), strict=same & (row > col), upper=same & (row <= col), eye=row == col,
                first=row < DN_CHUNK, row=row, lane=col)


def _pair_decay(bgv, masks):
    gc_all = _dot01(masks["causal"].astype(F32), bgv)
    out = []
    for h in range(DN_HEADS):
        beta = _lane_pick(bgv, h)
        gcb = jnp.broadcast_to(_lane_pick(gc_all, DN_HEADS + h), (PAIR, PAIR))
        gam = jnp.where(masks["causal"], jnp.exp(jnp.minimum(gcb - gcb.T, 0.0)), 0.0)
        gl = jnp.where(masks["first"], gcb[DN_CHUNK - 1:DN_CHUNK, :], gcb[PAIR - 1:PAIR, :])
        out.append((beta, gcb, gam, gl))
    return out


def _pair_inverse(a_strict, eye):
    eye_f = eye.astype(F32)[None]
    m = eye_f + a_strict
    x = eye_f - a_strict
    steps = int(math.log2(DN_CHUNK)) - 1
    for i in range(steps):
        mm = _bmm3 if i == steps - 1 else _bmm
        x = x + mm(x, eye_f - mm(m, x))
    return x


def _head_cols(h):
    return slice(h * LANES, (h + 1) * LANES)


def _delta_prep(qkv_pre, ba, conv_w8, alog_row, dtb_row):
    s = qkv_pre.shape[0]
    c = DN_CHUNK
    n_chunks = s // c

    def body(pre_ref, halo_ref, cw_ref, ba_ref, al_ref, dtb_ref,
             u_ref, w_ref, qd_ref, kd_ref, aqk_ref, dl_ref, t2_ref, q_ref, k_ref, v_ref, bg_ref, ext_ref):
        _dn_prep_fwd_tile(pre_ref, halo_ref, cw_ref, ba_ref, al_ref, dtb_ref, q_ref, k_ref, v_ref, bg_ref, ext_ref,
                          pl.program_id(0) == 0)
        masks = _pair_masks()
        dec = _pair_decay(bg_ref[...], masks)
        kbs, ks, gams, vbs, kbes, qs, qds, kds, dls = ([] for _ in range(9))
        for h in range(DN_HEADS):
            beta, gcb, gam, gl = dec[h]
            qh, kh, vh = q_ref[:, _head_cols(h)], k_ref[:, _head_cols(h)], v_ref[:, _head_cols(h)]
            eg = jnp.exp(gcb)
            kb = kh * beta
            kbs.append(kb); ks.append(kh); gams.append(gam); vbs.append(vh * beta); kbes.append(kb * eg)
            qs.append(qh); qds.append(qh * eg); kds.append(kh * jnp.exp(gl - gcb)); dls.append(jnp.exp(gl))
        st = lambda xs: jnp.stack(xs, axis=0)
        kmat, gam = st(ks), st(gams)
        a = jnp.where(masks["strict"][None], _bmm_nt(st(kbs), kmat) * gam, 0.0)
        t = _pair_inverse(a, masks["eye"])
        u = _bmm(t, st(vbs))
        w = _bmm(t, st(kbes))
        aqk = _bmm_nt(st(qs), kmat) * gam
        t2_ref[0] = t.astype(t2_ref.dtype)
        for half in range(2):
            rows = slice(half * c, (half + 1) * c)
            u_ref[half] = u[:, rows, :]
            w_ref[half] = w[:, rows, :].astype(w_ref.dtype)
            qd_ref[half] = st(qds)[:, rows, :].astype(qd_ref.dtype)
            kd_ref[half] = st(kds)[:, rows, :].astype(kd_ref.dtype)
            aqk_ref[half] = aqk[:, rows, rows].astype(aqk_ref.dtype)
            dl_ref[half] = st(dls)[:, half * c:half * c + SUBLANES, :]

    row_spec = lambda w_: pl.BlockSpec((PAIR, w_), lambda i: (i, 0))
    hm = lambda a_, b_: pl.BlockSpec((2, DN_HEADS, a_, b_), lambda i: (i, 0, 0, 0))
    hm_shape = lambda a_, b_, dt: jax.ShapeDtypeStruct((n_chunks, DN_HEADS, a_, b_), dt)
    whole = lambda t: pl.BlockSpec(t.shape, lambda i: (0, 0))
    halo = pl.BlockSpec((SUBLANES, QKV_W), lambda i: (jnp.maximum(i * (PAIR // SUBLANES) - 1, 0), 0))
    return _pcall(
        body, name="delta_prep", grid=(n_chunks // 2,),
        in_specs=[row_spec(QKV_W), halo, whole(conv_w8), row_spec(LANES), whole(alog_row), whole(dtb_row)],
        out_specs=[hm(c, LANES)] * 4 + [hm(c, c), hm(SUBLANES, LANES),
                   pl.BlockSpec((1, DN_HEADS, PAIR, PAIR), lambda i: (i, 0, 0, 0))]
        + [row_spec(D_MODEL)] * 3 + [row_spec(LANES)],
        out_shape=[hm_shape(c, LANES, F32), hm_shape(c, LANES, MXU), hm_shape(c, LANES, MXU), hm_shape(c, LANES, MXU),
                   hm_shape(c, c, MXU), hm_shape(SUBLANES, LANES, F32),
                   jax.ShapeDtypeStruct((n_chunks // 2, DN_HEADS, PAIR, PAIR), MXU)]
        + [jax.ShapeDtypeStruct((s, D_MODEL), F32)] * 3 + [jax.ShapeDtypeStruct((s, LANES), F32)],
        scratch_shapes=[pltpu.VMEM((PAIR + SUBLANES, QKV_W), F32)],
        compiler_params=_params("parallel"),
    )(qkv_pre, qkv_pre, conv_w8, ba, alog_row, dtb_row)


def _delta_scan_fwd(u, w, qd, kd, aqk, dl, z, dnw_row, w_o_dn, attn_parts, attn_lses, zb, w_o_dil):
    n_chunks = u.shape[0]
    c = DN_CHUNK
    g_n = SCAN_CHUNKS

    def body(u_ref, w_ref, qd_ref, kd_ref, aqk_ref, dl_ref, z_ref, nw_ref, wo_ref,
             p0, p1, p2, l0, l1, l2, zb_ref, wod_ref,
             o_ref, vnew_ref, st_ref, on_ref, y_ref, lse_ref, oj_ref, gb_ref, yb_ref, state):
        @pl.when(pl.program_id(0) == 0)
        def _():
            state[...] = jnp.zeros_like(state)

        _attn_out_fwd_tile(p0, p1, p2, l0, l1, l2, zb_ref, wod_ref, lse_ref, oj_ref, gb_ref, yb_ref)
        for g in range(g_n):
            sv = state[...]
            sb = sv.astype(MXU)
            vnew = u_ref[g] - _bmm(w_ref[g], sb)
            o = _bmm(qd_ref[g], sb) + _bmm(aqk_ref[g], vnew)
            state[...] = sv * dl_ref[g][:, 0:1, :] + _bmm_tn(kd_ref[g], vnew)
            vnew_ref[g] = vnew.astype(vnew_ref.dtype)
            st_ref[g] = sb
            for h in range(DN_HEADS):
                o_ref[g * c:(g + 1) * c, _head_cols(h)] = o[h]
        _dn_out_fwd_tile(o_ref, z_ref, nw_ref, wo_ref, on_ref, y_ref)

    hm = lambda a_, b_: pl.BlockSpec((g_n, DN_HEADS, a_, b_), lambda i: (i, 0, 0, 0))
    rows = lambda width: pl.BlockSpec((g_n * c, width), lambda i: (i, 0))
    whole = lambda t: pl.BlockSpec(t.shape, lambda i: (0, 0))
    full = lambda width, dt: jax.ShapeDtypeStruct((n_chunks * c, width), dt)
    return _pcall(
        body, name="delta_scan_fwd", grid=(n_chunks // g_n,),
        in_specs=[hm(c, LANES)] * 4 + [hm(c, c), hm(SUBLANES, LANES), rows(D_MODEL), whole(dnw_row), whole(w_o_dn)]
        + [rows(DIL_W)] * 7 + [whole(w_o_dil)],
        out_specs=[rows(D_MODEL), hm(c, LANES), hm(DN_DK, DN_DK), rows(D_MODEL), rows(D_MODEL),
                   rows(DIL_W), rows(DIL_W), rows(DIL_W), rows(D_MODEL)],
        out_shape=[full(D_MODEL, F32),
                   jax.ShapeDtypeStruct((n_chunks, DN_HEADS, c, LANES), MXU),
                   jax.ShapeDtypeStruct((n_chunks, DN_HEADS, DN_DK, DN_DK), MXU),
                   full(D_MODEL, MXU), full(D_MODEL, F32),
                   full(DIL_W, F32), full(DIL_W, F32), full(DIL_W, MXU), full(D_MODEL, F32)],
        scratch_shapes=[pltpu.VMEM((DN_HEADS, DN_DK, DN_DK), F32)],
        compiler_params=_params("arbitrary"),
    )(u, w, qd, kd, aqk, dl, z, dnw_row, w_o_dn, *attn_parts, *attn_lses, zb, w_o_dil)


def _delta_scan_bwd(w, qd, kd, aqk, dl, vnew, st, dy, o, z, dnw_row, w_o_dn, dyb, o_joint, zb, w_o_dil):
    n_chunks = w.shape[0]
    c = DN_CHUNK
    g_n = SCAN_CHUNKS
    steps = n_chunks // g_n

    def body(w_ref, qd_ref, kd_ref, aqk_ref, dl_ref, vnew_ref, st_ref, dy_ref, o_ref, z_ref, nw_ref, wo_ref,
             dyb_ref, oj_ref, zb_ref, wod_ref,
             dvnew_ref, dkd_ref, ddl_ref, do_ref, dz_ref, dnw_ref, dob_ref, dzb_ref, delta_ref, dstate, d_scratch):
        @pl.when(pl.program_id(0) == 0)
        def _():
            dstate[...] = jnp.zeros_like(dstate)

        _attn_out_bwd_tile(dyb_ref, oj_ref, zb_ref, wod_ref, dob_ref, dzb_ref, delta_ref)
        acc = _dn_out_bwd_tile(dy_ref, o_ref, z_ref, nw_ref, wo_ref, do_ref, dz_ref, d_scratch)
        _acc_add(dnw_ref, jnp.concatenate([acc, jnp.zeros((SUBLANES - 1, LANES), F32)], axis=0))
        for g in reversed(range(g_n)):
            ds = dstate[...]
            dsb = ds.astype(MXU)
            doh = jnp.stack([do_ref[g * c:(g + 1) * c, _head_cols(h)] for h in range(DN_HEADS)], axis=0)
            dvnew = _bmm_tn(aqk_ref[g], doh) + _bmm(kd_ref[g], dsb)
            dkd_ref[g] = _bmm_nt(vnew_ref[g], dsb)
            ddl = jnp.sum(jnp.sum(st_ref[g].astype(F32) * ds, axis=2, keepdims=True), axis=1, keepdims=True)
            ddl_ref[g] = jnp.broadcast_to(ddl, (DN_HEADS, SUBLANES, LANES))
            dstate[...] = ds * dl_ref[g][:, 0:1, :] + _bmm_tn(qd_ref[g], doh) - _bmm_tn(w_ref[g], dvnew)
            dvnew_ref[g] = dvnew.astype(dvnew_ref.dtype)

    rev = lambda i: steps - 1 - i
    hm = lambda a_, b_: pl.BlockSpec((g_n, DN_HEADS, a_, b_), lambda i: (rev(i), 0, 0, 0))
    rows = lambda width: pl.BlockSpec((g_n * c, width), lambda i: (rev(i), 0))
    whole = lambda t: pl.BlockSpec(t.shape, lambda i: (0, 0))
    full = lambda width, dt: jax.ShapeDtypeStruct((n_chunks * c, width), dt)
    return _pcall(
        body, name="delta_scan_bwd", grid=(steps,),
        in_specs=[hm(c, LANES)] * 3 + [hm(c, c), hm(SUBLANES, LANES), hm(c, LANES), hm(DN_DK, DN_DK),
                  rows(D_MODEL), rows(D_MODEL), rows(D_MODEL), whole(dnw_row), whole(w_o_dn),
                  rows(D_MODEL), rows(DIL_W), rows(DIL_W), whole(w_o_dil)],
        out_specs=[hm(c, LANES), hm(c, LANES), hm(SUBLANES, LANES), rows(D_MODEL), rows(D_MODEL),
                   pl.BlockSpec((SUBLANES, LANES), lambda i: (0, 0)), rows(DIL_W), rows(DIL_W), rows(DIL_W)],
        out_shape=[jax.ShapeDtypeStruct((n_chunks, DN_HEADS, c, LANES), MXU),
                   jax.ShapeDtypeStruct((n_chunks, DN_HEADS, c, LANES), F32),
                   jax.ShapeDtypeStruct((n_chunks, DN_HEADS, SUBLANES, LANES), F32),
                   full(D_MODEL, F32), full(D_MODEL, MXU), jax.ShapeDtypeStruct((SUBLANES, LANES), F32),
                   full(DIL_W, F32), full(DIL_W, MXU), full(DIL_W, F32)],
        scratch_shapes=[pltpu.VMEM((DN_HEADS, DN_DK, DN_DK), F32), pltpu.VMEM((g_n * c, D_MODEL), F32)],
        compiler_params=_params("arbitrary"),
    )(w, qd, kd, aqk, dl, vnew, st, dy, o, z, dnw_row, w_o_dn, dyb, o_joint, zb, w_o_dil)


def _delta_post_bwd(q, k, v, bg, t2, st, vnew, do, dvnew, dkd, ddl, qkv_pre, ba, conv_w8, alog_row, dtb_row):
    s = q.shape[0]
    c = DN_CHUNK
    n_pairs = s // PAIR

    def body(q_ref, k_ref, v_ref, bg_ref, t2_ref, st_ref, vnew_ref, do_ref, dvnew_ref, dkd_ref, ddl_ref,
             pre_ref, halo_ref, cw_ref, ba_ref, al_ref, dtb_ref,
             du_ref, dba_ref, dsmall_ref, dcw_ref, dq_ref, dk_ref, dv_ref, dbg_ref, ext_ref, extd_ref, carry_ref):
        step = pl.program_id(0)

        @pl.when(step == 0)
        def _():
            carry_ref[...] = jnp.zeros_like(carry_ref)
            dcw_ref[...] = jnp.zeros_like(dcw_ref)

        masks = _pair_masks()
        first = masks["first"][None]
        dec = _pair_decay(bg_ref[...], masks)
        st_ = lambda xs: jnp.stack(xs, axis=0)
        heads = range(DN_HEADS)
        qm_, km_, vm_, dom = (st_([r[:, _head_cols(h)] for h in heads]) for r in (q_ref, k_ref, v_ref, do_ref))
        beta = st_([dec[h][0] for h in heads])
        gcb = st_([dec[h][1] for h in heads])
        gam = st_([dec[h][2] for h in heads])
        gl = st_([dec[h][3] for h in heads])
        pair = lambda ref: jnp.concatenate([ref[0], ref[1]], axis=1)
        vnew2, dvnew2, dkd2 = pair(vnew_ref), pair(dvnew_ref), pair(dkd_ref)
        halves = lambda x: (x[:, :c, :], x[:, c:, :])
        by_state = lambda x: jnp.concatenate([_bmm_nt(xh, st_ref[i]) for i, xh in enumerate(halves(x))], axis=1)
        dqd = by_state(dom)
        dw = -by_state(dvnew2)
        ddl2 = jnp.where(first, ddl_ref[0][:, 0:1, :], ddl_ref[1][:, 0:1, :])

        eg = jnp.exp(gcb)
        egl = jnp.exp(gl - gcb)
        dl = jnp.exp(gl)
        kb = km_ * beta
        kk = _bmm_nt(kb, km_)
        a = jnp.where(masks["strict"][None], kk * gam, 0.0)
        t = t2_ref[0]
        vb = vm_ * beta
        kbe = kb * eg
        u = _bmm(t, vb)
        w = _bmm(t, kbe)
        aqk = _bmm_nt(qm_, km_) * gam
        qd = qm_ * eg
        kd = km_ * egl

        daqk = jnp.where(masks["causal"][None], _bmm_nt(dom, vnew2), 0.0)
        dvb = _bmm_tn(t, dvnew2)
        dkbe = _bmm_tn(t, dw)
        da = jnp.where(masks["strict"][None], -(_bmm_nt(dvb, u) + _bmm_nt(dkbe, w)), 0.0)
        pm = da * gam
        qmm = daqk * gam
        dkb = _bmm(pm, km_) + dkbe * eg
        dkh = _bmm_tn(pm, kb) + _bmm_tn(qmm, qm_) + dkd2 * egl + dkb * beta
        dqh = _bmm(qmm, km_) + dqd * eg
        xm = da * a + daqk * aqk
        col_rows = jnp.concatenate([jnp.zeros((DN_HEADS, PAIR), F32), jnp.sum(xm, axis=1),
                                    jnp.zeros((PAIR - 2 * DN_HEADS, PAIR), F32)], axis=0)
        tmp = jnp.sum(dkd2 * kd, axis=-1, keepdims=True)
        dgc = (jnp.sum(xm, axis=-1, keepdims=True) + jnp.sum(dkbe * kbe, axis=-1, keepdims=True)
               + jnp.sum(dqd * qd, axis=-1, keepdims=True) - tmp)
        sum0 = jnp.sum(jnp.where(first, tmp, 0.0), axis=1, keepdims=True)
        sum1 = jnp.sum(jnp.where(first, 0.0, tmp), axis=1, keepdims=True)
        dgl = jnp.where(first, sum0, sum1) + ddl2 * dl
        last = (masks["row"] == c - 1) | (masks["row"] == PAIR - 1)
        dgc = dgc + jnp.where(last[None], dgl, 0.0)
        dbeta = jnp.sum(dvb * vm_, axis=-1, keepdims=True) + jnp.sum(dkb * km_, axis=-1, keepdims=True)
        dvh = dvb * beta

        lane = masks["lane"]
        dgc_lanes = jnp.zeros((PAIR, LANES), F32)
        dbg = jnp.zeros((PAIR, LANES), F32)
        for h in heads:
            dq_ref[:, _head_cols(h)] = dqh[h]
            dk_ref[:, _head_cols(h)] = dkh[h]
            dv_ref[:, _head_cols(h)] = dvh[h]
            dgc_lanes = dgc_lanes + jnp.where(lane == DN_HEADS + h, dgc[h], 0.0)
            dbg = dbg + jnp.where(lane == h, dbeta[h], 0.0)
        dbg_ref[...] = dbg + _dot01(masks["upper"].astype(F32), dgc_lanes - col_rows.T)
        small = _dn_prep_bwd_tile(pre_ref, halo_ref, cw_ref, ba_ref, al_ref, dtb_ref, dq_ref, dk_ref, dv_ref, dbg_ref,
                                  extd_ref.at[pl.ds(0, PAIR)], dba_ref, ext_ref, step == n_pairs - 1)
        _acc_add(dsmall_ref, small)
        extd_ref[PAIR:, :] = carry_ref[...]
        _conv_bwd_tile(extd_ref, ext_ref, cw_ref, du_ref, dcw_ref, PAIR)
        carry_ref[...] = extd_ref[0:SUBLANES, :]

    rev = lambda i: n_pairs - 1 - i
    row_spec = lambda w_: pl.BlockSpec((PAIR, w_), lambda i: (rev(i), 0))
    hm = lambda a_, b_: pl.BlockSpec((2, DN_HEADS, a_, b_), lambda i: (rev(i), 0, 0, 0))
    whole = lambda t: pl.BlockSpec(t.shape, lambda i: (0, 0))
    halo = pl.BlockSpec((SUBLANES, QKV_W), lambda i: (jnp.maximum(rev(i) * (PAIR // SUBLANES) - 1, 0), 0))
    return _pcall(
        body, name="delta_post_bwd", grid=(n_pairs,),
        in_specs=[row_spec(D_MODEL)] * 3
        + [row_spec(LANES), pl.BlockSpec((1, DN_HEADS, PAIR, PAIR), lambda i: (rev(i), 0, 0, 0)),
           hm(DN_DK, DN_DK), hm(c, LANES), row_spec(D_MODEL), hm(c, LANES), hm(c, LANES), hm(SUBLANES, LANES),
           row_spec(QKV_W), halo, whole(conv_w8), row_spec(LANES), whole(alog_row), whole(dtb_row)],
        out_specs=[row_spec(QKV_W), row_spec(LANES), pl.BlockSpec((SUBLANES, LANES), lambda i: (0, 0)),
                   pl.BlockSpec((SUBLANES, QKV_W), lambda i: (0, 0))],
        out_shape=[jax.ShapeDtypeStruct((s, QKV_W), MXU), jax.ShapeDtypeStruct((s, LANES), MXU),
                   jax.ShapeDtypeStruct((SUBLANES, LANES), F32), jax.ShapeDtypeStruct((SUBLANES, QKV_W), F32)],
        scratch_shapes=[pltpu.VMEM((PAIR, D_MODEL), F32)] * 3
        + [pltpu.VMEM((PAIR, LANES), F32), pltpu.VMEM((PAIR + SUBLANES, QKV_W), F32),
           pltpu.VMEM((PAIR + SUBLANES, QKV_W), F32), pltpu.VMEM((SUBLANES, QKV_W), F32)],
        compiler_params=_params("arbitrary"),
    )(q, k, v, bg, t2, st, vnew, do, dvnew, dkd, ddl, qkv_pre, qkv_pre, conv_w8, ba, alog_row, dtb_row)


def _alibi_slope(group, head):
    n = N_DIL * DIL_HEADS
    return float(2.0 ** (-8.0 * (group * DIL_HEADS + head + 1) / n))


def _attn_plan(s, group):
    window, dil = DIL_GROUPS[group]
    assert window // dil == ATT_BLOCK
    assert (s // dil) % ATT_BLOCK == 0, "sub-sequence length must be a whole number of attention blocks"
    return dil, s // dil // ATT_BLOCK, (DIL_HEADS if dil == 1 else 1)


def _attn_specs(group, dil, nb, hp):
    rows = ATT_BLOCK * dil

    def spec(col0, shift):
        if shift < 0:
            f = lambda hb, n: (jnp.maximum(n - 1, 0), col0 + hb)
        elif shift > 0:
            f = lambda hb, n: (jnp.minimum(n + 1, nb - 1), col0 + hb)
        else:
            f = lambda hb, n: (jnp.minimum(n, nb - 1), col0 + hb)
        return pl.BlockSpec((rows, hp * LANES), f)

    return (lambda shift: spec(group * (DIL_HEADS // hp), shift)), (lambda shift: spec(0, shift))


def _sub_rows(ref, r, dil, cols):
    return ref[:, cols] if dil == 1 else ref[pl.ds(r, ATT_BLOCK, stride=dil), cols]


def _set_sub_rows(ref, r, dil, cols, value):
    if dil == 1:
        ref[:, cols] = value
    else:
        ref[pl.ds(r, ATT_BLOCK, stride=dil), cols] = value


def _step_slope(group, hp, hh):
    if hp == DIL_HEADS:
        return _alibi_slope(group, hh)
    hb = pl.program_id(0)
    slope = _alibi_slope(group, DIL_HEADS - 1)
    for h in reversed(range(DIL_HEADS - 1)):
        slope = jnp.where(hb == h, _alibi_slope(group, h), slope)
    return slope


def _attn_items(hp, dil):
    return [(hh, r) for hh in range(hp) for r in range(dil)]


def _attn_stack(ref, items, dil, dtype=MXU):
    return jnp.stack([_sub_rows(ref, r, dil, _head_cols(hh)).astype(dtype) for hh, r in items], axis=0)


def _attn_slopes(group, hp, items):
    if hp == 1:
        return _step_slope(group, hp, 0)
    return jnp.stack([jnp.full((1, 1), _alibi_slope(group, hh), F32) for hh, _ in items], axis=0)


def _window_bias(dil, n):
    a = lax.broadcasted_iota(jnp.int32, (ATT_BLOCK, 2 * ATT_BLOCK), 0)
    b = lax.broadcasted_iota(jnp.int32, (ATT_BLOCK, 2 * ATT_BLOCK), 1)
    dist = ATT_BLOCK + a - b
    valid = (dist >= 0) & (dist <= ATT_BLOCK) & ((b >= ATT_BLOCK) | (n > 0))
    return (dist * dil).astype(F32), valid


def _attn_fwd(qb, kb, vb, group):
    s = qb.shape[0]
    dil, nb, hp = _attn_plan(s, group)
    qkv, per_head = _attn_specs(group, dil, nb, hp)

    def body(q_ref, kp_ref, kc_ref, vp_ref, vc_ref, o_ref, lse_ref):
        n = pl.program_id(1)
        distd, valid = _window_bias(dil, n)
        items = _attn_items(hp, dil)
        sub = lambda ref: _attn_stack(ref, items, dil)
        kk = jnp.concatenate([sub(kp_ref), sub(kc_ref)], axis=1)
        vv = jnp.concatenate([sub(vp_ref), sub(vc_ref)], axis=1)
        sc = _bmm_nt(sub(q_ref), kk) * DIL_DH ** -0.5 - _attn_slopes(group, hp, items) * distd
        sc = jnp.where(valid, sc, -1e30)
        mx = jnp.max(sc, axis=-1, keepdims=True)
        p = jnp.where(valid, jnp.exp(sc - mx), 0.0)
        den = jnp.sum(p, axis=-1, keepdims=True)
        out = _bmm(p, vv) / den
        lse = mx + jnp.log(den)
        for b, (hh, r) in enumerate(items):
            _set_sub_rows(o_ref, r, dil, _head_cols(hh), out[b])
            _set_sub_rows(lse_ref, r, dil, _head_cols(hh), jnp.broadcast_to(lse[b], (ATT_BLOCK, LANES)))

    return _pcall(
        body, name=f"attn_fwd_g{group}", grid=(DIL_HEADS // hp, nb),
        in_specs=[qkv(0), qkv(-1), qkv(0), qkv(-1), qkv(0)], out_specs=[per_head(0)] * 2,
        out_shape=[jax.ShapeDtypeStruct((s, DIL_W), F32)] * 2,
        compiler_params=_params("parallel", "parallel"),
    )(qb, kb, kb, vb, vb)


def _attn_bwd(qb, kb, vb, d_o, lse, delta, group):
    s = qb.shape[0]
    dil, nb, hp = _attn_plan(s, group)
    qkv, per_head = _attn_specs(group, dil, nb, hp)
    scale = DIL_DH ** -0.5

    def body(q_ref, kp_ref, kc_ref, vp_ref, vc_ref, do_ref, l_ref, dl_ref, dq_ref, dk_ref, dv_ref,
             dq_acc, dk_done, dv_done, dk_carry, dv_carry):
        n = pl.program_id(1)
        items = _attn_items(hp, dil)
        slopes = _attn_slopes(group, hp, items)

        @pl.when(n == 0)
        def _():
            dk_carry[...] = jnp.zeros_like(dk_carry)
            dv_carry[...] = jnp.zeros_like(dv_carry)

        @pl.when(n < nb)
        def _():
            distd, valid = _window_bias(dil, n)
            sub = lambda ref, dtype=MXU: _attn_stack(ref, items, dil, dtype)
            qc, do = sub(q_ref), sub(do_ref)
            kk = jnp.concatenate([sub(kp_ref), sub(kc_ref)], axis=1)
            vv = jnp.concatenate([sub(vp_ref), sub(vc_ref)], axis=1)
            sc = _bmm_nt(qc, kk) * scale - slopes * distd
            p = jnp.where(valid, jnp.exp(jnp.minimum(sc - jnp.concatenate([sub(l_ref, F32)] * 2, axis=2), 0.0)), 0.0)
            dsc = p * (_bmm_nt(do, vv) - jnp.concatenate([sub(dl_ref, F32)] * 2, axis=2))
            dq = _bmm(dsc, kk) * scale
            dkk = _bmm_tn(dsc, qc) * scale
            dvv = _bmm_tn(p, do)
            for b, (hh, r) in enumerate(items):
                cols = _head_cols(hh)
                _set_sub_rows(dq_acc, r, dil, cols, dq[b])
                _set_sub_rows(dk_done, r, dil, cols, _sub_rows(dk_carry, r, dil, cols) + dkk[b, :ATT_BLOCK])
                _set_sub_rows(dv_done, r, dil, cols, _sub_rows(dv_carry, r, dil, cols) + dvv[b, :ATT_BLOCK])
                _set_sub_rows(dk_carry, r, dil, cols, dkk[b, ATT_BLOCK:])
                _set_sub_rows(dv_carry, r, dil, cols, dvv[b, ATT_BLOCK:])
            dq_ref[...] = dq_acc[...].astype(dq_ref.dtype)
            dk_ref[...] = dk_done[...].astype(dk_ref.dtype)
            dv_ref[...] = dv_done[...].astype(dv_ref.dtype)

        @pl.when(n == nb)
        def _():
            dk_ref[...] = dk_carry[...].astype(dk_ref.dtype)
            dv_ref[...] = dv_carry[...].astype(dv_ref.dtype)

    return _pcall(
        body, name=f"attn_bwd_g{group}", grid=(DIL_HEADS // hp, nb + 1),
        in_specs=[qkv(0), qkv(-1), qkv(0), qkv(-1), qkv(0)] + [per_head(0)] * 3,
        out_specs=[per_head(0), per_head(-1), per_head(-1)],
        out_shape=[jax.ShapeDtypeStruct((s, DIL_W), MXU)] * 3,
        scratch_shapes=[pltpu.VMEM((ATT_BLOCK * dil, hp * LANES), F32)] * 5,
        compiler_params=_params("parallel", "arbitrary"),
    )(qb, kb, kb, vb, vb, d_o, lse, delta)


def _my_place():
    mx, my, mc = lax.axis_index("x"), lax.axis_index("y"), lax.axis_index("c")
    return mx, my, mc, 4 * mx + 2 * my + mc


N_CHIPS = 4


def _shard_row_tile(r):
    if r <= 512:
        return r
    return 128 if r % 128 == 0 else 480


def _other_chips(mx, my):
    return [(1 - mx, my), (mx, 1 - my), (1 - mx, 1 - my)]


def _all_gather(xs, name):
    n = len(xs)
    halved = [x.shape[1] % (2 * LANES) == 0 and x.size * x.dtype.itemsize >= (1 << 20) for x in xs]
    n_sems = 8

    def body(*refs):
        x_refs, o_refs = refs[:n], refs[n:2 * n]
        send_sems, recv_sems, local_sems = refs[2 * n:]
        mx, my, mc, me = _my_place()
        sibling, sibling_id = (mx, my, 1 - mc), 4 * mx + 2 * my + (1 - mc)
        x_nbr, y_nbr, diag = _other_chips(mx, my)
        slot_of = lambda chip, c: 4 * chip[0] + 2 * chip[1] + c

        def part(ref, a, half):
            if not halved[a]:
                return ref
            width = xs[a].shape[1] // 2
            return ref.at[:, pl.ds(half * width, width)]

        def copy(a, k, dst, to, src=None):
            return pltpu.make_async_remote_copy(
                src_ref=dst if src is None else src, dst_ref=dst, send_sem=send_sems.at[a, k],
                recv_sem=recv_sems.at[a, k], device_id=to, device_id_type=MESH)

        local = [pltpu.make_async_copy(x_refs[a], o_refs[a].at[me], local_sems.at[a]) for a in range(n)]
        for cp in local:
            cp.start()
        sends = []
        for a in range(n):
            mine = o_refs[a].at[me]
            sends += [copy(a, 0, mine, sibling, src=x_refs[a]), copy(a, 1, mine, (*x_nbr, mc), src=x_refs[a]),
                      copy(a, 2, mine, (*y_nbr, mc), src=x_refs[a])]
        for cp in sends:
            cp.start()
        for a in range(n):
            blk = o_refs[a].at[slot_of(x_nbr, mc)]
            copy(a, 1, blk, (*x_nbr, mc)).wait_recv()
            sends += [copy(a, 3, blk, sibling), copy(a, 5, part(blk, a, 0), (*y_nbr, mc))]
            sends[-2].start()
            sends[-1].start()
        for a in range(n):
            blk = o_refs[a].at[slot_of(y_nbr, mc)]
            copy(a, 2, blk, (*y_nbr, mc)).wait_recv()
            sends.append(copy(a, 4, blk, sibling))
            sends[-1].start()
            if halved[a]:
                sends.append(copy(a, 6, part(blk, a, 1), (*x_nbr, mc)))
                sends[-1].start()
        for a in range(n):
            blk = o_refs[a].at[slot_of(diag, mc)]
            copy(a, 5, part(blk, a, 0), (*y_nbr, mc)).wait_recv()
            if halved[a]:
                copy(a, 6, part(blk, a, 1), (*x_nbr, mc)).wait_recv()
            sends.append(copy(a, 7, blk, sibling))
            sends[-1].start()
        for a in range(n):
            copy(a, 0, o_refs[a].at[sibling_id], sibling).wait_recv()
            for k, chip in ((3, x_nbr), (4, y_nbr), (7, diag)):
                copy(a, k, o_refs[a].at[slot_of(chip, 1 - mc)], sibling).wait_recv()
        for cp in sends:
            cp.wait_send()
        for cp in local:
            cp.wait()

    any_spec = pl.BlockSpec(memory_space=pl.ANY)
    return _pcall(
        body, name=name,
        in_specs=[any_spec] * n, out_specs=[any_spec] * n,
        out_shape=[jax.ShapeDtypeStruct((N_DEV,) + x.shape, x.dtype) for x in xs],
        scratch_shapes=[pltpu.SemaphoreType.DMA((n, n_sems)), pltpu.SemaphoreType.DMA((n, n_sems)),
                        pltpu.SemaphoreType.DMA((n,))],
    )(*xs)


def _pair_exchange(gs, name):
    n = len(gs)

    def body(*refs):
        g_refs, o_refs = refs[:n], refs[n:2 * n]
        send_sems, recv_sems = refs[2 * n:]
        mx, my, mc, _ = _my_place()
        copies = [pltpu.make_async_remote_copy(
            src_ref=g_refs[a].at[p, 1 - mc], dst_ref=o_refs[a].at[p], send_sem=send_sems.at[a, p],
            recv_sem=recv_sems.at[a, p], device_id=(mx, my, 1 - mc), device_id_type=MESH)
            for a in range(n) for p in range(N_CHIPS)]
        for cp in copies:
            cp.start()
        for cp in copies:
            cp.wait()

    any_spec = pl.BlockSpec(memory_space=pl.ANY)
    return _pcall(
        body, name=name,
        in_specs=[any_spec] * n, out_specs=[any_spec] * n,
        out_shape=[jax.ShapeDtypeStruct((N_CHIPS,) + g.shape[2:], g.dtype) for g in gs],
        scratch_shapes=[pltpu.SemaphoreType.DMA((n, N_CHIPS)), pltpu.SemaphoreType.DMA((n, N_CHIPS))],
    )(*gs)


def _pair_add(g, other, name):
    chips, _, r, c = g.shape
    tr = _shard_row_tile(r)
    core = lax.axis_index("c").astype(jnp.int32).reshape(1)

    def body(core_ref, g_ref, o_ref, h_ref):
        h_ref[...] = (g_ref[...].astype(F32)[0] + o_ref[...].astype(F32)).astype(h_ref.dtype)

    blk = pl.BlockSpec((1, tr, c), lambda p, i, core_ref: (p, i, 0))
    return _pcall(
        body, name=name,
        grid_spec=pltpu.PrefetchScalarGridSpec(
            num_scalar_prefetch=1, grid=(chips, pl.cdiv(r, tr)),
            in_specs=[pl.BlockSpec((1, 1, tr, c), lambda p, i, core_ref: (p, core_ref[0], i, 0)), blk],
            out_specs=blk),
        out_shape=jax.ShapeDtypeStruct((chips, r, c), g.dtype),
        compiler_params=_params("parallel", "parallel"),
    )(core, g, other)


def _chip_exchange(hs, name):
    n = len(hs)

    def body(*refs):
        h_refs, o_refs = refs[:n], refs[n:2 * n]
        send_sems, recv_sems, local_sems = refs[2 * n:]
        mx, my, mc, _ = _my_place()
        my_chip = 2 * mx + my
        chips = _other_chips(mx, my)
        local = [pltpu.make_async_copy(h_refs[a].at[my_chip], o_refs[a].at[my_chip], local_sems.at[a]) for a in range(n)]
        for cp in local:
            cp.start()
        for j, (px, py) in enumerate(chips):
            for a in range(n):
                pltpu.make_async_remote_copy(
                    src_ref=h_refs[a].at[2 * px + py], dst_ref=o_refs[a].at[my_chip], send_sem=send_sems.at[a, j],
                    recv_sem=recv_sems.at[a, j], device_id=(px, py, mc), device_id_type=MESH).start()
        for j, (px, py) in enumerate(chips):
            for a in range(n):
                pltpu.make_async_remote_copy(
                    src_ref=h_refs[a].at[2 * px + py], dst_ref=o_refs[a].at[2 * px + py], send_sem=send_sems.at[a, j],
                    recv_sem=recv_sems.at[a, j], device_id=(px, py, mc), device_id_type=MESH).wait()
        for cp in local:
            cp.wait()

    any_spec = pl.BlockSpec(memory_space=pl.ANY)
    return _pcall(
        body, name=name,
        in_specs=[any_spec] * n, out_specs=[any_spec] * n,
        out_shape=[jax.ShapeDtypeStruct(h.shape, h.dtype) for h in hs],
        scratch_shapes=[pltpu.SemaphoreType.DMA((n, N_CHIPS - 1)), pltpu.SemaphoreType.DMA((n, N_CHIPS - 1)),
                        pltpu.SemaphoreType.DMA((n,))],
    )(*hs)


def _adamw(parts, w, m, v, name):
    r, c = w.shape
    n_parts = parts.shape[0]
    tr = _shard_row_tile(r)
    bc1 = 1.0 - ADAM_B1 ** ADAM_STEP
    bc2 = 1.0 - ADAM_B2 ** ADAM_STEP

    def body(p_ref, w_ref, m_ref, v_ref, g_ref, d_ref, nm_ref, nv_ref):
        g = p_ref[0].astype(F32)
        for j in range(1, n_parts):
            g = g + p_ref[j].astype(F32)
        nm = ADAM_B1 * m_ref[...] + (1.0 - ADAM_B1) * g
        nv = ADAM_B2 * v_ref[...] + (1.0 - ADAM_B2) * (g * g)
        g_ref[...] = g
        nm_ref[...] = nm
        nv_ref[...] = nv
        d_ref[...] = -ADAM_LR * ((nm / bc1) / (jnp.sqrt(nv / bc2) + ADAM_EPS) + ADAM_WD * w_ref[...])

    blk = pl.BlockSpec((tr, c), lambda i: (i, 0))
    return _pcall(
        body, name=name, grid=(pl.cdiv(r, tr),),
        in_specs=[pl.BlockSpec((n_parts, tr, c), lambda i: (0, i, 0)), blk, blk, blk],
        out_specs=[blk] * 4, out_shape=[jax.ShapeDtypeStruct((r, c), F32)] * 4,
        compiler_params=_params("parallel"),
    )(parts, w, m, v)


def _local_step(x, target, norm_w, w_segs, conv_w, a_log, dt_bias, dn_norm_w, w_o_dn, w_o_dil, w_out, final_norm_w):
    s = x.shape[0]
    w_qkv, w_za, w_ba, w_qb, w_kb, w_vb, w_zb, w_ga, w_gb = w_segs
    conv_w8 = jnp.concatenate([conv_w, jnp.zeros((SUBLANES - conv_w.shape[0], QKV_W), F32)], axis=0)
    pad8 = jnp.zeros((1, DN_HEADS), F32)
    alog_row = jnp.concatenate([pad8, a_log, jnp.zeros((1, LANES - 2 * DN_HEADS), F32)], axis=1)
    dtb_row = jnp.concatenate([pad8, dt_bias, jnp.zeros((1, LANES - 2 * DN_HEADS), F32)], axis=1)
    wf_row = final_norm_w.reshape(1, D_MODEL)

    hb, qkv_pre, z_a, ba, z_b = _rms_proj_fwd(x, norm_w, [w_qkv, w_za, w_ba, w_zb], "rms_proj_fwd_a")
    q_b, k_b, v_b, g_a, g_b = _mm_out(hb, [w_qb, w_kb, w_vb, w_ga, w_gb], "proj_fwd_b", w_is_out_by_in=True)

    u_d, w_d, qd_d, kd_d, aqk_d, dl_d, t2_d, qn, kn, vn, bg = _delta_prep(qkv_pre, ba, conv_w8, alog_row, dtb_row)
    parts, lses = [], []
    for gi in range(N_DIL):
        o_g, l_g = _attn_fwd(q_b, k_b, v_b, gi)
        parts.append(o_g)
        lses.append(l_g)
    o_a, vnew_d, st_d, on_b, y_a, lse, o_joint, ob_b, y_b = _delta_scan_fwd(
        u_d, w_d, qd_d, kd_d, aqk_d, dl_d, z_a, dn_norm_w, w_o_dn, parts, lses, z_b, w_o_dil)

    loss8, dwf8, merged_b, dx2_b, dx2, dya_b, dyb_b, dga_b, dgb_b = _merge_out_final(
        g_a, g_b, y_a, y_b, x, target, w_out, wf_row)

    g_w_out = _mm_tn(merged_b, dx2_b, "out_wgrad")
    g_w_o_dn = _mm_tn(on_b, dya_b, "out_dn_wgrad")

    g_w_o_dil = _mm_tn(ob_b, dyb_b, "out_dil_wgrad")
    dvnew_d, dkd_d, ddl_d, d_o_a, dza_b, ddnw8, d_o, dzb_b, delta = _delta_scan_bwd(
        w_d, qd_d, kd_d, aqk_d, dl_d, vnew_d, st_d, dya_b, o_a, z_a, dn_norm_w, w_o_dn, dyb_b, o_joint, z_b, w_o_dil)
    dqs, dks, dvs = [], [], []
    for gi in range(N_DIL):
        dq_g, dk_g, dv_g = _attn_bwd(q_b, k_b, v_b, d_o, lse, delta, gi)
        dqs.append(dq_g)
        dks.append(dk_g)
        dvs.append(dv_g)

    dqkv_b, dba_b, dsmall8, dconv8 = _delta_post_bwd(qn, kn, vn, bg, t2_d, st_d, vnew_d, d_o_a, dvnew_d, dkd_d, ddl_d,
                                                     qkv_pre, ba, conv_w8, alog_row, dtb_row)

    per_group = lambda w: [w[g * DIL_W:(g + 1) * DIL_W] for g in range(N_DIL)]
    dh_b = _mm_in(dqs + dks + dvs + [dga_b, dgb_b],
                  per_group(w_qb) + per_group(w_kb) + per_group(w_vb) + [w_ga, w_gb], "proj_bwd_b", w_is_out_by_in=True)
    dsegs = [dqkv_b, dza_b, dba_b] + dqs + dks + dvs + [dzb_b, dga_b, dgb_b]
    valid_rows = [d.shape[1] for d in dsegs]
    valid_rows[2] = 2 * DN_HEADS
    g_wt = _proj_wgrad_all(dsegs, valid_rows, hb)
    grad_x, dnw8 = _proj_bwd_rms_in([dqkv_b, dza_b, dba_b, dzb_b], [w_qkv, w_za, w_ba, w_zb], dh_b, x, dx2, norm_w)

    small = dict(norm_w=dnw8[0:1], final_norm_w=dwf8[0:1], dn_norm_w=ddnw8[0:1],
                 a_log=dsmall8[0:1, DN_HEADS:2 * DN_HEADS], dt_bias=dsmall8[1:2, DN_HEADS:2 * DN_HEADS])
    return loss8[0:1, 0:1], grad_x, g_wt, dconv8[0:4], g_w_o_dn, g_w_o_dil, g_w_out, small


def _proj_bwd_rms_in(ds, ws, dh_a, x, dx2, norm_w):
    n_seg = len(ds)

    def body(*refs):
        d_refs, w_refs = refs[:n_seg], refs[n_seg:2 * n_seg]
        da_ref, x_ref, dx2_ref, w_ref, dx_ref, dw_ref = refs[2 * n_seg:]
        dx_ref[...] = da_ref[...]
        for d_ref, wt_ref in zip(d_refs, w_refs):
            for c, wd in _col_chunks(d_ref.shape[1], 1024):
                dx_ref[...] += jnp.dot(d_ref[:, c:c + wd], wt_ref[c:c + wd, :], preferred_element_type=F32)
        xv = x_ref[...]
        r = lax.rsqrt(jnp.mean(xv * xv, axis=-1, keepdims=True) + NORM_EPS)
        dhv = dx_ref[...]
        dn = dhv * w_ref[...]
        dx_ref[...] = dx2_ref[...] + r * dn - xv * (r * r * r) * jnp.mean(dn * xv, axis=-1, keepdims=True)
        row = jnp.sum(dhv * xv * r, axis=0, keepdims=True)
        _acc_add(dw_ref, jnp.concatenate([row, jnp.zeros((SUBLANES - 1, row.shape[1]), F32)], axis=0))

    return _rows_call(body, "proj_bwd_b_rms_in", x.shape[0],
                      [(d, "tile") for d in ds] + [(w, "full") for w in ws]
                      + [(dh_a, "tile"), (x, "tile"), (dx2, "tile"), (norm_w, "full")],
                      [(x.shape, F32, "tile"), ((SUBLANES, x.shape[1]), F32, "acc")])


def _split_proj_rows(w_shards):
    n_shards, rows, k = w_shards.shape
    wt_full = w_shards.reshape(n_shards * rows, k)
    offs = [0]
    for n in PROJ_SIZES:
        offs.append(offs[-1] + n)
    seg = lambda a, b: wt_full[offs[a]:offs[b]]
    w_ba = jnp.concatenate([seg(4, 6), jnp.zeros((LANES - 2 * DN_HEADS, k), wt_full.dtype)], axis=0)
    return [seg(0, 3), seg(3, 4), w_ba, seg(6, 7), seg(7, 8), seg(8, 9), seg(9, 10), seg(10, 11), seg(11, 12)]


LOSS_ROW = 5


def _pack_small(norm_w, final_norm_w, dn_norm_w, a_log, dt_bias, loss=None):
    pad = lambda r: jnp.concatenate([r, jnp.zeros((1, D_MODEL - r.shape[1]), F32)], axis=1)
    rows = [pad(norm_w.reshape(1, -1)), pad(final_norm_w.reshape(1, -1)), pad(dn_norm_w.reshape(1, -1)),
            pad(a_log.reshape(1, -1)), pad(dt_bias.reshape(1, -1)),
            pad(jnp.zeros((1, 1), F32) if loss is None else loss.reshape(1, 1)),
            jnp.zeros((SUBLANES - LOSS_ROW - 1, D_MODEL), F32)]
    return jnp.concatenate(rows, axis=0)


def _unpack_small(p):
    return dict(norm_w=p[0:1], final_norm_w=p[1], dn_norm_w=p[2:3, :DN_DK], a_log=p[3:4, :DN_HEADS],
                dt_bias=p[4:5, :DN_HEADS])


def kernel(x, norm_w, w_in, conv_w, a_log, dt_bias, dn_norm_w, w_o_dn, w_o_dil, w_out, final_norm_w, loss_target, m_norm_w, m_w_in, m_conv_w, m_a_log, m_dt_bias, m_dn_norm_w, m_w_o_dn, m_w_o_dil, m_w_out, m_final_norm_w, v_norm_w, v_w_in, v_conv_w, v_a_log, v_dt_bias, v_dn_norm_w, v_w_o_dn, v_w_o_dil, v_w_out, v_final_norm_w):
    shard_w = w_in.shape[2]
    wt, m_wt, v_wt = (jnp.transpose(t[0]) for t in (w_in, m_w_in, v_w_in))
    gathered = _all_gather([wt.astype(MXU), w_o_dn[0].astype(MXU), w_o_dil[0].astype(MXU), w_out[0].astype(MXU),
                            conv_w[0]], "gather_weights")
    w_in_all, w_o_dn_all, w_o_dil_all, w_out_all, conv_all = gathered
    w_o_dn_full = w_o_dn_all.reshape(D_MODEL, D_MODEL)
    w_o_dil_full = jnp.transpose(w_o_dil_all, (1, 0, 2)).reshape(DIL_W, D_MODEL)
    w_out_full = w_out_all.reshape(D_MODEL, D_MODEL)
    conv_full = jnp.transpose(conv_all, (1, 0, 2)).reshape(conv_w.shape[1], QKV_W)

    loss11, grad_x, g_wt, g_conv, g_w_o_dn, g_w_o_dil, g_w_out, small = _local_step(
        x[0], loss_target[0], norm_w, _split_proj_rows(w_in_all), conv_full, a_log, dt_bias, dn_norm_w,
        w_o_dn_full, w_o_dil_full, w_out_full, final_norm_w)

    col_shards = lambda g, n: jnp.transpose(g.reshape(g.shape[0], N_DEV, n), (1, 0, 2))
    row_shards = lambda g: g.reshape(N_DEV, g.shape[0] // N_DEV, g.shape[1])
    g_wt_shards = jnp.stack([g_wt[j * shard_w:(j + 1) * shard_w] for j in range(N_DEV)], axis=0)
    sent = [g_wt_shards, row_shards(g_w_o_dn).astype(MXU),
            col_shards(g_w_o_dil, w_o_dil.shape[2]).astype(MXU), row_shards(g_w_out).astype(MXU),
            col_shards(g_conv, conv_w.shape[2])]
    sent = [g8.reshape((N_CHIPS, 2) + g8.shape[1:]) for g8 in sent]
    from_sibling = _pair_exchange(sent, "scatter_pair")
    summed = [_pair_add(g, o, f"pair_add_{i}") for i, (g, o) in enumerate(zip(sent, from_sibling))]
    p_w_in, p_w_o_dn, p_w_o_dil, p_w_out, p_conv = _chip_exchange(summed, "scatter_chips")
    p_small = _all_gather([_pack_small(small["norm_w"], small["final_norm_w"], small["dn_norm_w"], small["a_log"],
                                       small["dt_bias"], loss11)], "gather_small_grads")[0]

    res = {}
    res["w_in"] = [jnp.transpose(t) for t in _adamw(p_w_in, wt, m_wt, v_wt, "adamw_w_in")]
    res["conv_w"] = _adamw(p_conv, conv_w[0], m_conv_w[0], v_conv_w[0], "adamw_conv_w")
    res["w_o_dn"] = _adamw(p_w_o_dn, w_o_dn[0], m_w_o_dn[0], v_w_o_dn[0], "adamw_w_o_dn")
    res["w_o_dil"] = _adamw(p_w_o_dil, w_o_dil[0], m_w_o_dil[0], v_w_o_dil[0], "adamw_w_o_dil")
    res["w_out"] = _adamw(p_w_out, w_out[0], m_w_out[0], v_w_out[0], "adamw_w_out")
    small_res = _adamw(p_small, _pack_small(norm_w, final_norm_w, dn_norm_w, a_log, dt_bias),
                       _pack_small(m_norm_w, m_final_norm_w, m_dn_norm_w, m_a_log, m_dt_bias),
                       _pack_small(v_norm_w, v_final_norm_w, v_dn_norm_w, v_a_log, v_dt_bias), "adamw_small")
    loss = small_res[0][LOSS_ROW, 0]
    small_res = [_unpack_small(t) for t in small_res]

    names = ["norm_w", "w_in", "conv_w", "a_log", "dt_bias", "dn_norm_w", "w_o_dn", "w_o_dil", "w_out", "final_norm_w"]
    outs = [loss, grad_x[None]]
    for kind in range(4):
        for nm in names:
            outs.append(res[nm][kind][None] if nm in res else small_res[kind][nm])
    return tuple(outs)
```

```python
import math

import jax
import jax.numpy as jnp
from jax import lax
from jax.experimental import pallas as pl
from jax.experimental.pallas import tpu as pltpu

F32 = jnp.float32
MXU = jnp.bfloat16
MESH = pl.DeviceIdType.MESH

N_DEV = 8
D_MODEL = 1024
DN_HEADS = 8
DN_DK = 128
DN_CHUNK = 64
N_DIL = 3
DIL_HEADS = 4
DIL_DH = 128
DIL_W = DIL_HEADS * DIL_DH
DIL_GROUPS = ((128, 1), (512, 4), (2048, 16))
ATT_BLOCK = 128
NORM_EPS = 1e-6
QKV_W = 3 * D_MODEL
DILQ_W = N_DIL * DIL_W
PROJ_SIZES = (1024, 1024, 1024, 1024, 8, 8, DILQ_W, DILQ_W, DILQ_W, DIL_W, D_MODEL, D_MODEL)

ADAM_LR = 0.001
ADAM_B1 = 0.9
ADAM_B2 = 0.999
ADAM_EPS = 1e-08
ADAM_WD = 0.01
ADAM_STEP = 10

ROW_TILE = 256
LANES = 128
SUBLANES = 8
VMEM_LIMIT = 48 << 20


def _pcall(body, **kw):
    return pl.pallas_call(body, **kw)


def _params(*sem):
    return pltpu.CompilerParams(dimension_semantics=tuple(sem), vmem_limit_bytes=VMEM_LIMIT)


def _sigmoid(x):
    return 1.0 / (1.0 + jnp.exp(-x))


def _softplus(x):
    return jnp.maximum(x, 0.0) + jnp.log(1.0 + jnp.exp(-jnp.abs(x)))


def _dot(a, b):
    return jnp.dot(a.astype(MXU), b.astype(MXU), preferred_element_type=F32)


def _dot_nt(a, b):
    return lax.dot_general(a.astype(MXU), b.astype(MXU), (((1,), (1,)), ((), ())), preferred_element_type=F32)


def _dot_tn(a, b):
    return lax.dot_general(a.astype(MXU), b.astype(MXU), (((0,), (0,)), ((), ())), preferred_element_type=F32)


def _split3(x):
    hi = x.astype(jnp.bfloat16)
    r1 = x - hi.astype(F32)
    mid = r1.astype(jnp.bfloat16)
    lo = (r1 - mid.astype(F32)).astype(jnp.bfloat16)
    return hi, mid, lo


def _dot01(m01, x):
    m = m01.astype(jnp.bfloat16)
    hi, mid, lo = _split3(x)
    f = lambda p: jnp.dot(m, p, preferred_element_type=F32)
    return f(hi) + (f(mid) + f(lo))


def _rows_call(body, name, n_rows, ins, outs, scratch=(), tm=ROW_TILE):
    steps = n_rows // tm
    per8 = tm // SUBLANES
    last8 = n_rows // SUBLANES - 1
    in_specs = []
    for arr, kind in ins:
        cols = arr.shape[-1]
        if kind == "tile":
            in_specs.append(pl.BlockSpec((tm, cols), lambda i: (i, 0)))
        elif kind == "full":
            in_specs.append(pl.BlockSpec(arr.shape, lambda i, nd=arr.ndim: (0,) * nd))
        elif kind == "prev8":
            in_specs.append(pl.BlockSpec((SUBLANES, cols), lambda i: (jnp.maximum(i * per8 - 1, 0), 0)))
        elif kind == "next8":
            in_specs.append(pl.BlockSpec((SUBLANES, cols), lambda i: (jnp.minimum((i + 1) * per8, last8), 0)))
        else:
            raise ValueError(kind)
    out_specs, out_shape, has_acc = [], [], False
    for shape, dtype, kind in outs:
        out_shape.append(jax.ShapeDtypeStruct(shape, dtype))
        if kind == "tile":
            out_specs.append(pl.BlockSpec((tm, shape[-1]), lambda i: (i, 0)))
        else:
            has_acc = True
            out_specs.append(pl.BlockSpec(shape, lambda i: (0, 0)))
    return _pcall(
        body, name=name, grid=(steps,), in_specs=in_specs, out_specs=out_specs, out_shape=out_shape,
        scratch_shapes=list(scratch),
        compiler_params=_params("arbitrary" if has_acc else "parallel"),
    )(*[a for a, _ in ins])


def _acc_add(ref, value):
    @pl.when(pl.program_id(0) == 0)
    def _():
        ref[...] = jnp.zeros_like(ref)
    ref[...] += value


def _col_chunks(n, width=512):
    return [(c, min(width, n - c)) for c in range(0, n, width)]


NT_DIMS = (((1,), (1,)), ((), ()))
TN_DIMS = (((0,), (0,)), ((), ()))


def _mm_out(a, ws, name, w_is_out_by_in=False, out_dtype=F32, tm=ROW_TILE):
    m, k = a.shape
    ns = [w.shape[0] if w_is_out_by_in else w.shape[1] for w in ws]

    def body(a_ref, *refs):
        av = a_ref[...]
        for w_ref, o_ref, n in zip(refs[:len(ws)], refs[len(ws):], ns):
            for c, wd in _col_chunks(n):
                if w_is_out_by_in:
                    part = lax.dot_general(av, w_ref[c:c + wd, :], NT_DIMS, preferred_element_type=F32)
                else:
                    part = jnp.dot(av, w_ref[:, c:c + wd], preferred_element_type=F32)
                o_ref[:, c:c + wd] = part.astype(o_ref.dtype)

    return _pcall(
        body, name=name, grid=(m // tm,),
        in_specs=[pl.BlockSpec((tm, k), lambda i: (i, 0))] + [pl.BlockSpec(w.shape, lambda i: (0, 0)) for w in ws],
        out_specs=[pl.BlockSpec((tm, n), lambda i: (i, 0)) for n in ns],
        out_shape=[jax.ShapeDtypeStruct((m, n), out_dtype) for n in ns],
        compiler_params=_params("parallel"),
    )(a, *ws)


def _rms_proj_fwd(x, norm_w, wts, name, tm=ROW_TILE):
    m, k = x.shape
    ns = [w.shape[0] for w in wts]

    def body(x_ref, nw_ref, *refs):
        w_refs, h_ref, o_refs = refs[:len(wts)], refs[len(wts)], refs[len(wts) + 1:]
        xv = x_ref[...]
        r = lax.rsqrt(jnp.mean(xv * xv, axis=-1, keepdims=True) + NORM_EPS)
        hv = (xv * r * nw_ref[...]).astype(h_ref.dtype)
        h_ref[...] = hv
        for w_ref, o_ref, n in zip(w_refs, o_refs, ns):
            for c, wd in _col_chunks(n):
                o_ref[:, c:c + wd] = lax.dot_general(hv, w_ref[c:c + wd, :], NT_DIMS, preferred_element_type=F32)

    return _pcall(
        body, name=name, grid=(m // tm,),
        in_specs=[pl.BlockSpec((tm, k), lambda i: (i, 0)), pl.BlockSpec(norm_w.shape, lambda i: (0, 0))]
        + [pl.BlockSpec(w.shape, lambda i: (0, 0)) for w in wts],
        out_specs=[pl.BlockSpec((tm, k), lambda i: (i, 0))] + [pl.BlockSpec((tm, n), lambda i: (i, 0)) for n in ns],
        out_shape=[jax.ShapeDtypeStruct((m, k), MXU)] + [jax.ShapeDtypeStruct((m, n), F32) for n in ns],
        compiler_params=_params("parallel"),
    )(x, norm_w, *wts)


def _mm_in(ds, ws, name, w_is_out_by_in=False, tm=ROW_TILE):
    m = ds[0].shape[0]
    k = ws[0].shape[1] if w_is_out_by_in else ws[0].shape[0]
    ns = [d.shape[1] for d in ds]

    def body(*refs):
        d_refs, w_refs, o_ref = refs[:len(ds)], refs[len(ds):2 * len(ds)], refs[-1]
        first = True
        for d_ref, w_ref, n in zip(d_refs, w_refs, ns):
            for c, wd in _col_chunks(n, 1024):
                if w_is_out_by_in:
                    part = jnp.dot(d_ref[:, c:c + wd], w_ref[c:c + wd, :], preferred_element_type=F32)
                else:
                    part = lax.dot_general(d_ref[:, c:c + wd], w_ref[:, c:c + wd], NT_DIMS, preferred_element_type=F32)
                if first:
                    o_ref[...] = part
                    first = False
                else:
                    o_ref[...] += part

    return _pcall(
        body, name=name, grid=(m // tm,),
        in_specs=[pl.BlockSpec((tm, n), lambda i: (i, 0)) for n in ns] + [pl.BlockSpec(w.shape, lambda i: (0, 0)) for w in ws],
        out_specs=pl.BlockSpec((tm, k), lambda i: (i, 0)),
        out_shape=jax.ShapeDtypeStruct((m, k), F32),
        compiler_params=_params("parallel"),
    )(*ds, *ws)


def _mm_tn(a, d, name):
    m, k = a.shape
    n = d.shape[1]
    tk = 512 if k % 512 == 0 else k

    def body(a_ref, d_ref, o_ref):
        o_ref[...] = lax.dot_general(a_ref[...], d_ref[...], TN_DIMS, preferred_element_type=F32)

    return _pcall(
        body, name=name, grid=(k // tk,),
        in_specs=[pl.BlockSpec((m, tk), lambda p: (0, p)), pl.BlockSpec((m, n), lambda p: (0, 0))],
        out_specs=pl.BlockSpec((tk, n), lambda p: (p, 0)),
        out_shape=jax.ShapeDtypeStruct((k, n), F32),
        compiler_params=_params("parallel"),
    )(a, d)


WGRAD_TILE = 1024


def _proj_wgrad_all(dsegs, valid_rows, hb):
    m, k = hb.shape
    n_seg = len(dsegs)
    tiles, row = [], 0
    for si, (d, valid) in enumerate(zip(dsegs, valid_rows)):
        for c in range(0, valid, WGRAD_TILE):
            width = min(WGRAD_TILE, d.shape[1] - c)
            tiles.append((si, c, width, row + c, min(width, valid - c)))
        row += valid
    total_rows = row

    def body(*refs):
        d_refs, hb_ref, o_ref = refs[:n_seg], refs[n_seg], refs[n_seg + 1]
        a_buf, hb_buf, o_buf, load_sems, store_sems, hb_sem = refs[n_seg + 2:]

        def load(t):
            si, c, width, _, _ = tiles[t]
            return pltpu.make_async_copy(d_refs[si].at[:, pl.ds(c, width)], a_buf.at[t % 2, :, pl.ds(0, width)],
                                         load_sems.at[t % 2])

        def stores(t):
            _, _, _, orow, valid = tiles[t]
            return [pltpu.make_async_copy(o_buf.at[t % 2, pl.ds(0, valid), :], o_ref.at[pl.ds(orow, valid), :],
                                          store_sems.at[t % 2])]

        hb_copy = pltpu.make_async_copy(hb_ref, hb_buf, hb_sem)
        hb_copy.start()
        load(0).start()
        hb_copy.wait()
        for t in range(len(tiles)):
            width = tiles[t][2]
            load(t).wait()
            if t + 1 < len(tiles):
                load(t + 1).start()
            if t >= 2:
                for cp in stores(t - 2):
                    cp.wait()
            o_buf[t % 2, 0:width, :] = lax.dot_general(a_buf[t % 2, :, 0:width], hb_buf[...], TN_DIMS,
                                                        preferred_element_type=F32).astype(o_buf.dtype)
            for cp in stores(t):
                cp.start()
        for t in range(max(len(tiles) - 2, 0), len(tiles)):
            for cp in stores(t):
                cp.wait()

    any_spec = pl.BlockSpec(memory_space=pl.ANY)
    return _pcall(
        body, name="proj_wgrad",
        in_specs=[any_spec] * (n_seg + 1), out_specs=any_spec,
        out_shape=jax.ShapeDtypeStruct((total_rows, k), hb.dtype),
        scratch_shapes=[pltpu.VMEM((2, m, WGRAD_TILE), hb.dtype), pltpu.VMEM((m, k), hb.dtype),
                        pltpu.VMEM((2, WGRAD_TILE, k), hb.dtype), pltpu.SemaphoreType.DMA((2,)),
                        pltpu.SemaphoreType.DMA((2,)), pltpu.SemaphoreType.DMA],
        compiler_params=pltpu.CompilerParams(vmem_limit_bytes=VMEM_LIMIT),
    )(*dsegs, hb)


def _conv_taps(ext_ref, cw_ref, cols, tm):
    c = None
    for j in range(4):
        term = cw_ref[3 - j:4 - j, cols] * ext_ref[SUBLANES - j:SUBLANES - j + tm, cols]
        c = term if c is None else c + term
    return c


def _fill_ext(ext_ref, u_ref, halo_ref, first):
    ext_ref[0:SUBLANES, :] = jnp.where(first, 0.0, halo_ref[...])
    ext_ref[SUBLANES:, :] = u_ref[...]


def _dn_prep_fwd_tile(u_ref, halo_ref, cw_ref, ba_ref, al_ref, dtb_ref, q_ref, k_ref, v_ref, bg_ref, ext_ref, first):
    tm = u_ref.shape[0]
    _fill_ext(ext_ref, u_ref, halo_ref, first)
    for h in range(3 * DN_HEADS):
        cols = slice(h * LANES, (h + 1) * LANES)
        c = _conv_taps(ext_ref, cw_ref, cols, tm)
        a = c * _sigmoid(c)
        oc = slice((h % DN_HEADS) * LANES, (h % DN_HEADS + 1) * LANES)
        if h < 2 * DN_HEADS:
            rinv = lax.rsqrt(jnp.sum(a * a, axis=-1, keepdims=True) + NORM_EPS)
            if h < DN_HEADS:
                q_ref[:, oc] = a * (rinv * DN_DK ** -0.5)
            else:
                k_ref[:, oc] = a * rinv
        else:
            v_ref[:, oc] = a
    bav = ba_ref[...]
    lane = lax.broadcasted_iota(jnp.int32, bav.shape, 1)
    beta = _sigmoid(bav)
    g = -jnp.exp(al_ref[...]) * _softplus(bav + dtb_ref[...])
    bg_ref[...] = jnp.where(lane < DN_HEADS, beta, jnp.where(lane < 2 * DN_HEADS, g, 0.0))


def _dn_prep_bwd_tile(u_ref, halo_ref, cw_ref, ba_ref, al_ref, dtb_ref, dq_ref, dk_ref, dv_ref, dbg_ref,
                      dc_ref, dba_ref, ext_ref, first):
    tm = u_ref.shape[0]
    _fill_ext(ext_ref, u_ref, halo_ref, first)
    for h in range(3 * DN_HEADS):
        cols = slice(h * LANES, (h + 1) * LANES)
        oc = slice((h % DN_HEADS) * LANES, (h % DN_HEADS + 1) * LANES)
        c = _conv_taps(ext_ref, cw_ref, cols, tm)
        sg = _sigmoid(c)
        a = c * sg
        if h < 2 * DN_HEADS:
            rinv = lax.rsqrt(jnp.sum(a * a, axis=-1, keepdims=True) + NORM_EPS)
            dy = dq_ref[:, oc] * DN_DK ** -0.5 if h < DN_HEADS else dk_ref[:, oc]
            da = rinv * dy - a * (rinv * rinv * rinv) * jnp.sum(dy * a, axis=-1, keepdims=True)
        else:
            da = dv_ref[:, oc]
        dc_ref[:, cols] = da * (sg * (1.0 + c * (1.0 - sg)))
    bav = ba_ref[...]
    dbgv = dbg_ref[...]
    lane = lax.broadcasted_iota(jnp.int32, bav.shape, 1)
    beta = _sigmoid(bav)
    ea = jnp.exp(al_ref[...])
    z = bav + dtb_ref[...]
    g = -ea * _softplus(z)
    is_b = lane < DN_HEADS
    is_g = jnp.logical_and(lane >= DN_HEADS, lane < 2 * DN_HEADS)
    d_aa = jnp.where(is_g, dbgv * (-ea) * _sigmoid(z), 0.0)
    dba = jnp.where(is_b, dbgv * beta * (1.0 - beta), d_aa)
    dba_ref[...] = dba.astype(dba_ref.dtype)
    r_alog = jnp.sum(jnp.where(is_g, dbgv * g, 0.0), axis=0, keepdims=True)
    r_dtb = jnp.sum(d_aa, axis=0, keepdims=True)
    return jnp.concatenate([r_alog, r_dtb, jnp.zeros((SUBLANES - 2, LANES), F32)], axis=0)


def _conv_bwd_tile(extd_ref, ext_ref, cw_ref, du_ref, dcw_ref, tm):
    for h in range(3 * DN_HEADS):
        cols = slice(h * LANES, (h + 1) * LANES)
        du = None
        for j in range(4):
            term = cw_ref[3 - j:4 - j, cols] * extd_ref[j:j + tm, cols]
            du = term if du is None else du + term
        du_ref[:, cols] = du.astype(du_ref.dtype)
        dcv = extd_ref[0:tm, cols]
        for j in range(4):
            row = jnp.sum(dcv * ext_ref[SUBLANES - j:SUBLANES - j + tm, cols], axis=0, keepdims=True)
            dcw_ref[3 - j:4 - j, cols] += row


def _dn_out_fwd_tile(o_ref, z_ref, w_ref, wo_ref, on_ref, y_ref):
    for h in range(DN_HEADS):
        cols = _head_cols(h)
        ov = o_ref[:, cols]
        zv = z_ref[:, cols]
        ro = lax.rsqrt(jnp.mean(ov * ov, axis=-1, keepdims=True) + NORM_EPS)
        on_ref[:, cols] = (ov * ro * w_ref[...] * (zv * _sigmoid(zv))).astype(on_ref.dtype)
    y_ref[...] = jnp.dot(on_ref[...], wo_ref[...], preferred_element_type=F32)


def _dn_out_bwd_tile(dy_ref, o_ref, z_ref, w_ref, wo_ref, do_ref, dz_ref, d_ref):
    d_ref[...] = lax.dot_general(dy_ref[...], wo_ref[...], NT_DIMS, preferred_element_type=F32)
    acc = jnp.zeros((1, LANES), F32)
    for h in range(DN_HEADS):
        cols = _head_cols(h)
        dv, ov, zv = d_ref[:, cols], o_ref[:, cols], z_ref[:, cols]
        sg = _sigmoid(zv)
        sz = zv * sg
        ro = lax.rsqrt(jnp.mean(ov * ov, axis=-1, keepdims=True) + NORM_EPS)
        nv = ov * ro
        dn = dv * w_ref[...] * sz
        acc = acc + jnp.sum(dv * nv * sz, axis=0, keepdims=True)
        dz_ref[:, cols] = (dv * nv * w_ref[...] * (sg * (1.0 + zv * (1.0 - sg)))).astype(dz_ref.dtype)
        do_ref[:, cols] = ro * dn - ov * (ro * ro * ro) * jnp.mean(dn * ov, axis=-1, keepdims=True)
    return acc


def _attn_out_fwd_tile(o0, o1, o2, l0, l1, l2, z_ref, wo_ref, lse_ref, o_ref, g_ref, y_ref):
    a, b, c = l0[...], l1[...], l2[...]
    m = jnp.maximum(a, jnp.maximum(b, c))
    ea, eb, ec = jnp.exp(a - m), jnp.exp(b - m), jnp.exp(c - m)
    den = ea + eb + ec
    out = (ea * o0[...] + eb * o1[...] + ec * o2[...]) / den
    lse_ref[...] = m + jnp.log(den)
    o_ref[...] = out
    zv = z_ref[...]
    gated = (out * (zv * _sigmoid(zv))).astype(g_ref.dtype)
    g_ref[...] = gated
    y_ref[...] = jnp.dot(gated, wo_ref[...], preferred_element_type=F32)


def _attn_out_bwd_tile(dy_ref, o_ref, z_ref, wo_ref, do_ref, dz_ref, dl_ref):
    zv = z_ref[...]
    sg = _sigmoid(zv)
    dv = lax.dot_general(dy_ref[...], wo_ref[...], NT_DIMS, preferred_element_type=F32)
    ov = o_ref[...]
    do = dv * (zv * sg)
    do_ref[...] = do
    dz_ref[...] = (dv * ov * (sg * (1.0 + zv * (1.0 - sg)))).astype(dz_ref.dtype)
    for h in range(DIL_HEADS):
        cols = _head_cols(h)
        dl_ref[:, cols] = jnp.broadcast_to(jnp.sum(do[:, cols] * ov[:, cols], axis=-1, keepdims=True),
                                           (do.shape[0], LANES))


def _merge_out_final(ga, gb, ya, yb, x, target, w_out, wf_row):
    s, dm = x.shape

    def body(ga_ref, gb_ref, ya_ref, yb_ref, x_ref, t_ref, wo_ref, w_ref,
             loss_ref, dw_ref, m_ref, dxb_ref, dx_ref, dya_ref, dyb_ref, dga_ref, dgb_ref):
        sa, sb = _sigmoid(ga_ref[...]), _sigmoid(gb_ref[...])
        ya, yb = ya_ref[...], yb_ref[...]
        merged = (sa * ya + sb * yb).astype(MXU)
        m_ref[...] = merged
        x2 = x_ref[...] + jnp.dot(merged, wo_ref[...], preferred_element_type=F32)
        r = lax.rsqrt(jnp.mean(x2 * x2, axis=-1, keepdims=True) + NORM_EPS)
        w = w_ref[...]
        err = x2 * r * w - t_ref[...]
        tile_loss = 0.5 * jnp.sum(jnp.mean(err * err, axis=-1, keepdims=True), axis=0, keepdims=True)
        _acc_add(loss_ref, jnp.broadcast_to(tile_loss, (SUBLANES, LANES)))
        dy = err * (1.0 / dm)
        row = jnp.sum(dy * x2 * r, axis=0, keepdims=True)
        _acc_add(dw_ref, jnp.concatenate([row, jnp.zeros((SUBLANES - 1, dm), F32)], axis=0))
        dn = dy * w
        dx2 = r * dn - x2 * (r * r * r) * jnp.mean(dn * x2, axis=-1, keepdims=True)
        dx_ref[...] = dx2
        dxb = dx2.astype(MXU)
        dxb_ref[...] = dxb
        dmv = lax.dot_general(dxb, wo_ref[...], NT_DIMS, preferred_element_type=F32)
        dya_ref[...] = (dmv * sa).astype(dya_ref.dtype)
        dyb_ref[...] = (dmv * sb).astype(dyb_ref.dtype)
        dga_ref[...] = (dmv * ya * sa * (1.0 - sa)).astype(dga_ref.dtype)
        dgb_ref[...] = (dmv * yb * sb * (1.0 - sb)).astype(dgb_ref.dtype)

    return _rows_call(body, "merge_out_final", s,
                      [(ga, "tile"), (gb, "tile"), (ya, "tile"), (yb, "tile"), (x, "tile"), (target, "tile"),
                       (w_out, "full"), (wf_row, "full")],
                      [((SUBLANES, LANES), F32, "acc"), ((SUBLANES, dm), F32, "acc"), ((s, dm), MXU, "tile"),
                       ((s, dm), MXU, "tile"), ((s, dm), F32, "tile")] + [((s, dm), MXU, "tile")] * 4)


def _lane_pick(x, idx):
    lane = lax.broadcasted_iota(jnp.int32, x.shape, 1)
    return jnp.sum(jnp.where(lane == idx, x, 0.0), axis=-1, keepdims=True)


PAIR = 2 * DN_CHUNK
SCAN_CHUNKS = 4


def _bmm(a, b):
    return lax.dot_general(a.astype(MXU), b.astype(MXU), (((2,), (1,)), ((0,), (0,))), preferred_element_type=F32)


def _bmm_nt(a, b):
    return lax.dot_general(a.astype(MXU), b.astype(MXU), (((2,), (2,)), ((0,), (0,))), preferred_element_type=F32)


def _bmm_tn(a, b):
    return lax.dot_general(a.astype(MXU), b.astype(MXU), (((1,), (1,)), ((0,), (0,))), preferred_element_type=F32)


def _bmm3(a, b):
    ah = a.astype(jnp.bfloat16)
    al = (a - ah.astype(F32)).astype(jnp.bfloat16)
    bh = b.astype(jnp.bfloat16)
    bl = (b - bh.astype(F32)).astype(jnp.bfloat16)
    f = lambda p, q: lax.dot_general(p, q, (((2,), (1,)), ((0,), (0,))), preferred_element_type=F32)
    return f(ah, bh) + (f(ah, bl) + f(al, bh))


def _pair_masks():
    row = lax.broadcasted_iota(jnp.int32, (PAIR, PAIR), 0)
    col = lax.broadcasted_iota(jnp.int32, (PAIR, PAIR), 1)
    same = (row >= DN_CHUNK) == (col >= DN_CHUNK)
    return dict(causal=same & (row >= col), strict=same & (row > col), upper=same & (row <= col), eye=row == col,
                first=row < DN_CHUNK, row=row, lane=col)


def _pair_decay(bgv, masks):
    gc_all = _dot01(masks["causal"].astype(F32), bgv)
    out = []
    for h in range(DN_HEADS):
        beta = _lane_pick(bgv, h)
        gcb = jnp.broadcast_to(_lane_pick(gc_all, DN_HEADS + h), (PAIR, PAIR))
        gam = jnp.where(masks["causal"], jnp.exp(jnp.minimum(gcb - gcb.T, 0.0)), 0.0)
        gl = jnp.where(masks["first"], gcb[DN_CHUNK - 1:DN_CHUNK, :], gcb[PAIR - 1:PAIR, :])
        out.append((beta, gcb, gam, gl))
    return out


def _pair_inverse(a_strict, eye):
    eye_f = eye.astype(F32)[None]
    m = eye_f + a_strict
    x = eye_f - a_strict
    steps = int(math.log2(DN_CHUNK)) - 1
    for i in range(steps):
        mm = _bmm3 if i == steps - 1 else _bmm
        x = x + mm(x, eye_f - mm(m, x))
    return x


def _head_cols(h):
    return slice(h * LANES, (h + 1) * LANES)


def _delta_prep(qkv_pre, ba, conv_w8, alog_row, dtb_row):
    s = qkv_pre.shape[0]
    c = DN_CHUNK
    n_chunks = s // c

    def body(pre_ref, halo_ref, cw_ref, ba_ref, al_ref, dtb_ref,
             u_ref, w_ref, qd_ref, kd_ref, aqk_ref, dl_ref, t2_ref, q_ref, k_ref, v_ref, bg_ref, ext_ref):
        _dn_prep_fwd_tile(pre_ref, halo_ref, cw_ref, ba_ref, al_ref, dtb_ref, q_ref, k_ref, v_ref, bg_ref, ext_ref,
                          pl.program_id(0) == 0)
        masks = _pair_masks()
        dec = _pair_decay(bg_ref[...], masks)
        kbs, ks, gams, vbs, kbes, qs, qds, kds, dls = ([] for _ in range(9))
        for h in range(DN_HEADS):
            beta, gcb, gam, gl = dec[h]
            qh, kh, vh = q_ref[:, _head_cols(h)], k_ref[:, _head_cols(h)], v_ref[:, _head_cols(h)]
            eg = jnp.exp(gcb)
            kb = kh * beta
            kbs.append(kb); ks.append(kh); gams.append(gam); vbs.append(vh * beta); kbes.append(kb * eg)
            qs.append(qh); qds.append(qh * eg); kds.append(kh * jnp.exp(gl - gcb)); dls.append(jnp.exp(gl))
        st = lambda xs: jnp.stack(xs, axis=0)
        kmat, gam = st(ks), st(gams)
        a = jnp.where(masks["strict"][None], _bmm_nt(st(kbs), kmat) * gam, 0.0)
        t = _pair_inverse(a, masks["eye"])
        u = _bmm(t, st(vbs))
        w = _bmm(t, st(kbes))
        aqk = _bmm_nt(st(qs), kmat) * gam
        t2_ref[0] = t.astype(t2_ref.dtype)
        for half in range(2):
            rows = slice(half * c, (half + 1) * c)
            u_ref[half] = u[:, rows, :]
            w_ref[half] = w[:, rows, :].astype(w_ref.dtype)
            qd_ref[half] = st(qds)[:, rows, :].astype(qd_ref.dtype)
            kd_ref[half] = st(kds)[:, rows, :].astype(kd_ref.dtype)
            aqk_ref[half] = aqk[:, rows, rows].astype(aqk_ref.dtype)
            dl_ref[half] = st(dls)[:, half * c:half * c + SUBLANES, :]

    row_spec = lambda w_: pl.BlockSpec((PAIR, w_), lambda i: (i, 0))
    hm = lambda a_, b_: pl.BlockSpec((2, DN_HEADS, a_, b_), lambda i: (i, 0, 0, 0))
    hm_shape = lambda a_, b_, dt: jax.ShapeDtypeStruct((n_chunks, DN_HEADS, a_, b_), dt)
    whole = lambda t: pl.BlockSpec(t.shape, lambda i: (0, 0))
    halo = pl.BlockSpec((SUBLANES, QKV_W), lambda i: (jnp.maximum(i * (PAIR // SUBLANES) - 1, 0), 0))
    return _pcall(
        body, name="delta_prep", grid=(n_chunks // 2,),
        in_specs=[row_spec(QKV_W), halo, whole(conv_w8), row_spec(LANES), whole(alog_row), whole(dtb_row)],
        out_specs=[hm(c, LANES)] * 4 + [hm(c, c), hm(SUBLANES, LANES),
                   pl.BlockSpec((1, DN_HEADS, PAIR, PAIR), lambda i: (i, 0, 0, 0))]
        + [row_spec(D_MODEL)] * 3 + [row_spec(LANES)],
        out_shape=[hm_shape(c, LANES, F32), hm_shape(c, LANES, MXU), hm_shape(c, LANES, MXU), hm_shape(c, LANES, MXU),
                   hm_shape(c, c, MXU), hm_shape(SUBLANES, LANES, F32),
                   jax.ShapeDtypeStruct((n_chunks // 2, DN_HEADS, PAIR, PAIR), MXU)]
        + [jax.ShapeDtypeStruct((s, D_MODEL), F32)] * 3 + [jax.ShapeDtypeStruct((s, LANES), F32)],
        scratch_shapes=[pltpu.VMEM((PAIR + SUBLANES, QKV_W), F32)],
        compiler_params=_params("parallel"),
    )(qkv_pre, qkv_pre, conv_w8, ba, alog_row, dtb_row)


def _delta_scan_fwd(u, w, qd, kd, aqk, dl, z, dnw_row, w_o_dn, attn_parts, attn_lses, zb, w_o_dil):
    n_chunks = u.shape[0]
    c = DN_CHUNK
    g_n = SCAN_CHUNKS

    def body(u_ref, w_ref, qd_ref, kd_ref, aqk_ref, dl_ref, z_ref, nw_ref, wo_ref,
             p0, p1, p2, l0, l1, l2, zb_ref, wod_ref,
             o_ref, vnew_ref, st_ref, on_ref, y_ref, lse_ref, oj_ref, gb_ref, yb_ref, state):
        @pl.when(pl.program_id(0) == 0)
        def _():
            state[...] = jnp.zeros_like(state)

        _attn_out_fwd_tile(p0, p1, p2, l0, l1, l2, zb_ref, wod_ref, lse_ref, oj_ref, gb_ref, yb_ref)
        for g in range(g_n):
            sv = state[...]
            sb = sv.astype(MXU)
            vnew = u_ref[g] - _bmm(w_ref[g], sb)
            o = _bmm(qd_ref[g], sb) + _bmm(aqk_ref[g], vnew)
            state[...] = sv * dl_ref[g][:, 0:1, :] + _bmm_tn(kd_ref[g], vnew)
            vnew_ref[g] = vnew.astype(vnew_ref.dtype)
            st_ref[g] = sb
            for h in range(DN_HEADS):
                o_ref[g * c:(g + 1) * c, _head_cols(h)] = o[h]
        _dn_out_fwd_tile(o_ref, z_ref, nw_ref, wo_ref, on_ref, y_ref)

    hm = lambda a_, b_: pl.BlockSpec((g_n, DN_HEADS, a_, b_), lambda i: (i, 0, 0, 0))
    rows = lambda width: pl.BlockSpec((g_n * c, width), lambda i: (i, 0))
    whole = lambda t: pl.BlockSpec(t.shape, lambda i: (0, 0))
    full = lambda width, dt: jax.ShapeDtypeStruct((n_chunks * c, width), dt)
    return _pcall(
        body, name="delta_scan_fwd", grid=(n_chunks // g_n,),
        in_specs=[hm(c, LANES)] * 4 + [hm(c, c), hm(SUBLANES, LANES), rows(D_MODEL), whole(dnw_row), whole(w_o_dn)]
        + [rows(DIL_W)] * 7 + [whole(w_o_dil)],
        out_specs=[rows(D_MODEL), hm(c, LANES), hm(DN_DK, DN_DK), rows(D_MODEL), rows(D_MODEL),
                   rows(DIL_W), rows(DIL_W), rows(DIL_W), rows(D_MODEL)],
        out_shape=[full(D_MODEL, F32),
                   jax.ShapeDtypeStruct((n_chunks, DN_HEADS, c, LANES), MXU),
                   jax.ShapeDtypeStruct((n_chunks, DN_HEADS, DN_DK, DN_DK), MXU),
                   full(D_MODEL, MXU), full(D_MODEL, F32),
                   full(DIL_W, F32), full(DIL_W, F32), full(DIL_W, MXU), full(D_MODEL, F32)],
        scratch_shapes=[pltpu.VMEM((DN_HEADS, DN_DK, DN_DK), F32)],
        compiler_params=_params("arbitrary"),
    )(u, w, qd, kd, aqk, dl, z, dnw_row, w_o_dn, *attn_parts, *attn_lses, zb, w_o_dil)


def _delta_scan_bwd(w, qd, kd, aqk, dl, vnew, st, dy, o, z, dnw_row, w_o_dn, dyb, o_joint, zb, w_o_dil):
    n_chunks = w.shape[0]
    c = DN_CHUNK
    g_n = SCAN_CHUNKS
    steps = n_chunks // g_n

    def body(w_ref, qd_ref, kd_ref, aqk_ref, dl_ref, vnew_ref, st_ref, dy_ref, o_ref, z_ref, nw_ref, wo_ref,
             dyb_ref, oj_ref, zb_ref, wod_ref,
             dvnew_ref, dkd_ref, ddl_ref, do_ref, dz_ref, dnw_ref, dob_ref, dzb_ref, delta_ref, dstate, d_scratch):
        @pl.when(pl.program_id(0) == 0)
        def _():
            dstate[...] = jnp.zeros_like(dstate)

        _attn_out_bwd_tile(dyb_ref, oj_ref, zb_ref, wod_ref, dob_ref, dzb_ref, delta_ref)
        acc = _dn_out_bwd_tile(dy_ref, o_ref, z_ref, nw_ref, wo_ref, do_ref, dz_ref, d_scratch)
        _acc_add(dnw_ref, jnp.concatenate([acc, jnp.zeros((SUBLANES - 1, LANES), F32)], axis=0))
        for g in reversed(range(g_n)):
            ds = dstate[...]
            dsb = ds.astype(MXU)
            doh = jnp.stack([do_ref[g * c:(g + 1) * c, _head_cols(h)] for h in range(DN_HEADS)], axis=0)
            dvnew = _bmm_tn(aqk_ref[g], doh) + _bmm(kd_ref[g], dsb)
            dkd_ref[g] = _bmm_nt(vnew_ref[g], dsb)
            ddl = jnp.sum(jnp.sum(st_ref[g].astype(F32) * ds, axis=2, keepdims=True), axis=1, keepdims=True)
            ddl_ref[g] = jnp.broadcast_to(ddl, (DN_HEADS, SUBLANES, LANES))
            dstate[...] = ds * dl_ref[g][:, 0:1, :] + _bmm_tn(qd_ref[g], doh) - _bmm_tn(w_ref[g], dvnew)
            dvnew_ref[g] = dvnew.astype(dvnew_ref.dtype)

    rev = lambda i: steps - 1 - i
    hm = lambda a_, b_: pl.BlockSpec((g_n, DN_HEADS, a_, b_), lambda i: (rev(i), 0, 0, 0))
    rows = lambda width: pl.BlockSpec((g_n * c, width), lambda i: (rev(i), 0))
    whole = lambda t: pl.BlockSpec(t.shape, lambda i: (0, 0))
    full = lambda width, dt: jax.ShapeDtypeStruct((n_chunks * c, width), dt)
    return _pcall(
        body, name="delta_scan_bwd", grid=(steps,),
        in_specs=[hm(c, LANES)] * 3 + [hm(c, c), hm(SUBLANES, LANES), hm(c, LANES), hm(DN_DK, DN_DK),
                  rows(D_MODEL), rows(D_MODEL), rows(D_MODEL), whole(dnw_row), whole(w_o_dn),
                  rows(D_MODEL), rows(DIL_W), rows(DIL_W), whole(w_o_dil)],
        out_specs=[hm(c, LANES), hm(c, LANES), hm(SUBLANES, LANES), rows(D_MODEL), rows(D_MODEL),
                   pl.BlockSpec((SUBLANES, LANES), lambda i: (0, 0)), rows(DIL_W), rows(DIL_W), rows(DIL_W)],
        out_shape=[jax.ShapeDtypeStruct((n_chunks, DN_HEADS, c, LANES), MXU),
                   jax.ShapeDtypeStruct((n_chunks, DN_HEADS, c, LANES), F32),
                   jax.ShapeDtypeStruct((n_chunks, DN_HEADS, SUBLANES, LANES), F32),
                   full(D_MODEL, F32), full(D_MODEL, MXU), jax.ShapeDtypeStruct((SUBLANES, LANES), F32),
                   full(DIL_W, F32), full(DIL_W, MXU), full(DIL_W, F32)],
        scratch_shapes=[pltpu.VMEM((DN_HEADS, DN_DK, DN_DK), F32), pltpu.VMEM((g_n * c, D_MODEL), F32)],
        compiler_params=_params("arbitrary"),
    )(w, qd, kd, aqk, dl, vnew, st, dy, o, z, dnw_row, w_o_dn, dyb, o_joint, zb, w_o_dil)


def _delta_post_bwd(q, k, v, bg, t2, st, vnew, do, dvnew, dkd, ddl, qkv_pre, ba, conv_w8, alog_row, dtb_row):
    s = q.shape[0]
    c = DN_CHUNK
    n_pairs = s // PAIR

    def body(q_ref, k_ref, v_ref, bg_ref, t2_ref, st_ref, vnew_ref, do_ref, dvnew_ref, dkd_ref, ddl_ref,
             pre_ref, halo_ref, cw_ref, ba_ref, al_ref, dtb_ref,
             du_ref, dba_ref, dsmall_ref, dcw_ref, dq_ref, dk_ref, dv_ref, dbg_ref, ext_ref, extd_ref, carry_ref):
        step = pl.program_id(0)

        @pl.when(step == 0)
        def _():
            carry_ref[...] = jnp.zeros_like(carry_ref)
            dcw_ref[...] = jnp.zeros_like(dcw_ref)

        masks = _pair_masks()
        first = masks["first"][None]
        dec = _pair_decay(bg_ref[...], masks)
        st_ = lambda xs: jnp.stack(xs, axis=0)
        heads = range(DN_HEADS)
        qm_, km_, vm_, dom = (st_([r[:, _head_cols(h)] for h in heads]) for r in (q_ref, k_ref, v_ref, do_ref))
        beta = st_([dec[h][0] for h in heads])
        gcb = st_([dec[h][1] for h in heads])
        gam = st_([dec[h][2] for h in heads])
        gl = st_([dec[h][3] for h in heads])
        pair = lambda ref: jnp.concatenate([ref[0], ref[1]], axis=1)
        vnew2, dvnew2, dkd2 = pair(vnew_ref), pair(dvnew_ref), pair(dkd_ref)
        halves = lambda x: (x[:, :c, :], x[:, c:, :])
        by_state = lambda x: jnp.concatenate([_bmm_nt(xh, st_ref[i]) for i, xh in enumerate(halves(x))], axis=1)
        dqd = by_state(dom)
        dw = -by_state(dvnew2)
        ddl2 = jnp.where(first, ddl_ref[0][:, 0:1, :], ddl_ref[1][:, 0:1, :])

        eg = jnp.exp(gcb)
        egl = jnp.exp(gl - gcb)
        dl = jnp.exp(gl)
        kb = km_ * beta
        kk = _bmm_nt(kb, km_)
        a = jnp.where(masks["strict"][None], kk * gam, 0.0)
        t = t2_ref[0]
        vb = vm_ * beta
        kbe = kb * eg
        u = _bmm(t, vb)
        w = _bmm(t, kbe)
        aqk = _bmm_nt(qm_, km_) * gam
        qd = qm_ * eg
        kd = km_ * egl

        daqk = jnp.where(masks["causal"][None], _bmm_nt(dom, vnew2), 0.0)
        dvb = _bmm_tn(t, dvnew2)
        dkbe = _bmm_tn(t, dw)
        da = jnp.where(masks["strict"][None], -(_bmm_nt(dvb, u) + _bmm_nt(dkbe, w)), 0.0)
        pm = da * gam
        qmm = daqk * gam
        dkb = _bmm(pm, km_) + dkbe * eg
        dkh = _bmm_tn(pm, kb) + _bmm_tn(qmm, qm_) + dkd2 * egl + dkb * beta
        dqh = _bmm(qmm, km_) + dqd * eg
        xm = da * a + daqk * aqk
        col_rows = jnp.concatenate([jnp.zeros((DN_HEADS, PAIR), F32), jnp.sum(xm, axis=1),
                                    jnp.zeros((PAIR - 2 * DN_HEADS, PAIR), F32)], axis=0)
        tmp = jnp.sum(dkd2 * kd, axis=-1, keepdims=True)
        dgc = (jnp.sum(xm, axis=-1, keepdims=True) + jnp.sum(dkbe * kbe, axis=-1, keepdims=True)
               + jnp.sum(dqd * qd, axis=-1, keepdims=True) - tmp)
        sum0 = jnp.sum(jnp.where(first, tmp, 0.0), axis=1, keepdims=True)
        sum1 = jnp.sum(jnp.where(first, 0.0, tmp), axis=1, keepdims=True)
        dgl = jnp.where(first, sum0, sum1) + ddl2 * dl
        last = (masks["row"] == c - 1) | (masks["row"] == PAIR - 1)
        dgc = dgc + jnp.where(last[None], dgl, 0.0)
        dbeta = jnp.sum(dvb * vm_, axis=-1, keepdims=True) + jnp.sum(dkb * km_, axis=-1, keepdims=True)
        dvh = dvb * beta

        lane = masks["lane"]
        dgc_lanes = jnp.zeros((PAIR, LANES), F32)
        dbg = jnp.zeros((PAIR, LANES), F32)
        for h in heads:
            dq_ref[:, _head_cols(h)] = dqh[h]
            dk_ref[:, _head_cols(h)] = dkh[h]
            dv_ref[:, _head_cols(h)] = dvh[h]
            dgc_lanes = dgc_lanes + jnp.where(lane == DN_HEADS + h, dgc[h], 0.0)
            dbg = dbg + jnp.where(lane == h, dbeta[h], 0.0)
        dbg_ref[...] = dbg + _dot01(masks["upper"].astype(F32), dgc_lanes - col_rows.T)
        small = _dn_prep_bwd_tile(pre_ref, halo_ref, cw_ref, ba_ref, al_ref, dtb_ref, dq_ref, dk_ref, dv_ref, dbg_ref,
                                  extd_ref.at[pl.ds(0, PAIR)], dba_ref, ext_ref, step == n_pairs - 1)
        _acc_add(dsmall_ref, small)
        extd_ref[PAIR:, :] = carry_ref[...]
        _conv_bwd_tile(extd_ref, ext_ref, cw_ref, du_ref, dcw_ref, PAIR)
        carry_ref[...] = extd_ref[0:SUBLANES, :]

    rev = lambda i: n_pairs - 1 - i
    row_spec = lambda w_: pl.BlockSpec((PAIR, w_), lambda i: (rev(i), 0))
    hm = lambda a_, b_: pl.BlockSpec((2, DN_HEADS, a_, b_), lambda i: (rev(i), 0, 0, 0))
    whole = lambda t: pl.BlockSpec(t.shape, lambda i: (0, 0))
    halo = pl.BlockSpec((SUBLANES, QKV_W), lambda i: (jnp.maximum(rev(i) * (PAIR // SUBLANES) - 1, 0), 0))
    return _pcall(
        body, name="delta_post_bwd", grid=(n_pairs,),
        in_specs=[row_spec(D_MODEL)] * 3
        + [row_spec(LANES), pl.BlockSpec((1, DN_HEADS, PAIR, PAIR), lambda i: (rev(i), 0, 0, 0)),
           hm(DN_DK, DN_DK), hm(c, LANES), row_spec(D_MODEL), hm(c, LANES), hm(c, LANES), hm(SUBLANES, LANES),
           row_spec(QKV_W), halo, whole(conv_w8), row_spec(LANES), whole(alog_row), whole(dtb_row)],
        out_specs=[row_spec(QKV_W), row_spec(LANES), pl.BlockSpec((SUBLANES, LANES), lambda i: (0, 0)),
                   pl.BlockSpec((SUBLANES, QKV_W), lambda i: (0, 0))],
        out_shape=[jax.ShapeDtypeStruct((s, QKV_W), MXU), jax.ShapeDtypeStruct((s, LANES), MXU),
                   jax.ShapeDtypeStruct((SUBLANES, LANES), F32), jax.ShapeDtypeStruct((SUBLANES, QKV_W), F32)],
        scratch_shapes=[pltpu.VMEM((PAIR, D_MODEL), F32)] * 3
        + [pltpu.VMEM((PAIR, LANES), F32), pltpu.VMEM((PAIR + SUBLANES, QKV_W), F32),
           pltpu.VMEM((PAIR + SUBLANES, QKV_W), F32), pltpu.VMEM((SUBLANES, QKV_W), F32)],
        compiler_params=_params("arbitrary"),
    )(q, k, v, bg, t2, st, vnew, do, dvnew, dkd, ddl, qkv_pre, qkv_pre, conv_w8, ba, alog_row, dtb_row)


def _alibi_slope(group, head):
    n = N_DIL * DIL_HEADS
    return float(2.0 ** (-8.0 * (group * DIL_HEADS + head + 1) / n))


def _attn_plan(s, group):
    window, dil = DIL_GROUPS[group]
    assert window // dil == ATT_BLOCK
    assert (s // dil) % ATT_BLOCK == 0, "sub-sequence length must be a whole number of attention blocks"
    return dil, s // dil // ATT_BLOCK, (DIL_HEADS if dil == 1 else 1)


def _attn_specs(group, dil, nb, hp):
    rows = ATT_BLOCK * dil

    def spec(col0, shift):
        if shift < 0:
            f = lambda hb, n: (jnp.maximum(n - 1, 0), col0 + hb)
        elif shift > 0:
            f = lambda hb, n: (jnp.minimum(n + 1, nb - 1), col0 + hb)
        else:
            f = lambda hb, n: (jnp.minimum(n, nb - 1), col0 + hb)
        return pl.BlockSpec((rows, hp * LANES), f)

    return (lambda shift: spec(group * (DIL_HEADS // hp), shift)), (lambda shift: spec(0, shift))


def _sub_rows(ref, r, dil, cols):
    return ref[:, cols] if dil == 1 else ref[pl.ds(r, ATT_BLOCK, stride=dil), cols]


def _set_sub_rows(ref, r, dil, cols, value):
    if dil == 1:
        ref[:, cols] = value
    else:
        ref[pl.ds(r, ATT_BLOCK, stride=dil), cols] = value


def _step_slope(group, hp, hh):
    if hp == DIL_HEADS:
        return _alibi_slope(group, hh)
    hb = pl.program_id(0)
    slope = _alibi_slope(group, DIL_HEADS - 1)
    for h in reversed(range(DIL_HEADS - 1)):
        slope = jnp.where(hb == h, _alibi_slope(group, h), slope)
    return slope


def _attn_items(hp, dil):
    return [(hh, r) for hh in range(hp) for r in range(dil)]


def _attn_stack(ref, items, dil, dtype=MXU):
    return jnp.stack([_sub_rows(ref, r, dil, _head_cols(hh)).astype(dtype) for hh, r in items], axis=0)


def _attn_slopes(group, hp, items):
    if hp == 1:
        return _step_slope(group, hp, 0)
    return jnp.stack([jnp.full((1, 1), _alibi_slope(group, hh), F32) for hh, _ in items], axis=0)


def _window_bias(dil, n):
    a = lax.broadcasted_iota(jnp.int32, (ATT_BLOCK, 2 * ATT_BLOCK), 0)
    b = lax.broadcasted_iota(jnp.int32, (ATT_BLOCK, 2 * ATT_BLOCK), 1)
    dist = ATT_BLOCK + a - b
    valid = (dist >= 0) & (dist <= ATT_BLOCK) & ((b >= ATT_BLOCK) | (n > 0))
    return (dist * dil).astype(F32), valid


def _attn_fwd(qb, kb, vb, group):
    s = qb.shape[0]
    dil, nb, hp = _attn_plan(s, group)
    qkv, per_head = _attn_specs(group, dil, nb, hp)

    def body(q_ref, kp_ref, kc_ref, vp_ref, vc_ref, o_ref, lse_ref):
        n = pl.program_id(1)
        distd, valid = _window_bias(dil, n)
        items = _attn_items(hp, dil)
        sub = lambda ref: _attn_stack(ref, items, dil)
        kk = jnp.concatenate([sub(kp_ref), sub(kc_ref)], axis=1)
        vv = jnp.concatenate([sub(vp_ref), sub(vc_ref)], axis=1)
        sc = _bmm_nt(sub(q_ref), kk) * DIL_DH ** -0.5 - _attn_slopes(group, hp, items) * distd
        sc = jnp.where(valid, sc, -1e30)
        mx = jnp.max(sc, axis=-1, keepdims=True)
        p = jnp.where(valid, jnp.exp(sc - mx), 0.0)
        den = jnp.sum(p, axis=-1, keepdims=True)
        out = _bmm(p, vv) / den
        lse = mx + jnp.log(den)
        for b, (hh, r) in enumerate(items):
            _set_sub_rows(o_ref, r, dil, _head_cols(hh), out[b])
            _set_sub_rows(lse_ref, r, dil, _head_cols(hh), jnp.broadcast_to(lse[b], (ATT_BLOCK, LANES)))

    return _pcall(
        body, name=f"attn_fwd_g{group}", grid=(DIL_HEADS // hp, nb),
        in_specs=[qkv(0), qkv(-1), qkv(0), qkv(-1), qkv(0)], out_specs=[per_head(0)] * 2,
        out_shape=[jax.ShapeDtypeStruct((s, DIL_W), F32)] * 2,
        compiler_params=_params("parallel", "parallel"),
    )(qb, kb, kb, vb, vb)


def _attn_bwd(qb, kb, vb, d_o, lse, delta, group):
    s = qb.shape[0]
    dil, nb, hp = _attn_plan(s, group)
    qkv, per_head = _attn_specs(group, dil, nb, hp)
    scale = DIL_DH ** -0.5

    def body(q_ref, kp_ref, kc_ref, vp_ref, vc_ref, do_ref, l_ref, dl_ref, dq_ref, dk_ref, dv_ref,
             dq_acc, dk_done, dv_done, dk_carry, dv_carry):
        n = pl.program_id(1)
        items = _attn_items(hp, dil)
        slopes = _attn_slopes(group, hp, items)

        @pl.when(n == 0)
        def _():
            dk_carry[...] = jnp.zeros_like(dk_carry)
            dv_carry[...] = jnp.zeros_like(dv_carry)

        @pl.when(n < nb)
        def _():
            distd, valid = _window_bias(dil, n)
            sub = lambda ref, dtype=MXU: _attn_stack(ref, items, dil, dtype)
            qc, do = sub(q_ref), sub(do_ref)
            kk = jnp.concatenate([sub(kp_ref), sub(kc_ref)], axis=1)
            vv = jnp.concatenate([sub(vp_ref), sub(vc_ref)], axis=1)
            sc = _bmm_nt(qc, kk) * scale - slopes * distd
            p = jnp.where(valid, jnp.exp(jnp.minimum(sc - jnp.concatenate([sub(l_ref, F32)] * 2, axis=2), 0.0)), 0.0)
            dsc = p * (_bmm_nt(do, vv) - jnp.concatenate([sub(dl_ref, F32)] * 2, axis=2))
            dq = _bmm(dsc, kk) * scale
            dkk = _bmm_tn(dsc, qc) * scale
            dvv = _bmm_tn(p, do)
            for b, (hh, r) in enumerate(items):
                cols = _head_cols(hh)
                _set_sub_rows(dq_acc, r, dil, cols, dq[b])
                _set_sub_rows(dk_done, r, dil, cols, _sub_rows(dk_carry, r, dil, cols) + dkk[b, :ATT_BLOCK])
                _set_sub_rows(dv_done, r, dil, cols, _sub_rows(dv_carry, r, dil, cols) + dvv[b, :ATT_BLOCK])
                _set_sub_rows(dk_carry, r, dil, cols, dkk[b, ATT_BLOCK:])
                _set_sub_rows(dv_carry, r, dil, cols, dvv[b, ATT_BLOCK:])
            dq_ref[...] = dq_acc[...].astype(dq_ref.dtype)
            dk_ref[...] = dk_done[...].astype(dk_ref.dtype)
            dv_ref[...] = dv_done[...].astype(dv_ref.dtype)

        @pl.when(n == nb)
        def _():
            dk_ref[...] = dk_carry[...].astype(dk_ref.dtype)
            dv_ref[...] = dv_carry[...].astype(dv_ref.dtype)

    return _pcall(
        body, name=f"attn_bwd_g{group}", grid=(DIL_HEADS // hp, nb + 1),
        in_specs=[qkv(0), qkv(-1), qkv(0), qkv(-1), qkv(0)] + [per_head(0)] * 3,
        out_specs=[per_head(0), per_head(-1), per_head(-1)],
        out_shape=[jax.ShapeDtypeStruct((s, DIL_W), MXU)] * 3,
        scratch_shapes=[pltpu.VMEM((ATT_BLOCK * dil, hp * LANES), F32)] * 5,
        compiler_params=_params("parallel", "arbitrary"),
    )(qb, kb, kb, vb, vb, d_o, lse, delta)


def _my_place():
    mx, my, mc = lax.axis_index("x"), lax.axis_index("y"), lax.axis_index("c")
    return mx, my, mc, 4 * mx + 2 * my + mc


N_CHIPS = 4


def _shard_row_tile(r):
    if r <= 512:
        return r
    return 128 if r % 128 == 0 else 480


def _other_chips(mx, my):
    return [(1 - mx, my), (mx, 1 - my), (1 - mx, 1 - my)]


def _all_gather(xs, name):
    n = len(xs)
    halved = [x.shape[1] % (2 * LANES) == 0 and x.size * x.dtype.itemsize >= (1 << 20) for x in xs]
    n_sems = 8

    def body(*refs):
        x_refs, o_refs = refs[:n], refs[n:2 * n]
        send_sems, recv_sems, local_sems = refs[2 * n:]
        mx, my, mc, me = _my_place()
        sibling, sibling_id = (mx, my, 1 - mc), 4 * mx + 2 * my + (1 - mc)
        x_nbr, y_nbr, diag = _other_chips(mx, my)
        slot_of = lambda chip, c: 4 * chip[0] + 2 * chip[1] + c

        def part(ref, a, half):
            if not halved[a]:
                return ref
            width = xs[a].shape[1] // 2
            return ref.at[:, pl.ds(half * width, width)]

        def copy(a, k, dst, to, src=None):
            return pltpu.make_async_remote_copy(
                src_ref=dst if src is None else src, dst_ref=dst, send_sem=send_sems.at[a, k],
                recv_sem=recv_sems.at[a, k], device_id=to, device_id_type=MESH)

        local = [pltpu.make_async_copy(x_refs[a], o_refs[a].at[me], local_sems.at[a]) for a in range(n)]
        for cp in local:
            cp.start()
        sends = []
        for a in range(n):
            mine = o_refs[a].at[me]
            sends += [copy(a, 0, mine, sibling, src=x_refs[a]), copy(a, 1, mine, (*x_nbr, mc), src=x_refs[a]),
                      copy(a, 2, mine, (*y_nbr, mc), src=x_refs[a])]
        for cp in sends:
            cp.start()
        for a in range(n):
            blk = o_refs[a].at[slot_of(x_nbr, mc)]
            copy(a, 1, blk, (*x_nbr, mc)).wait_recv()
            sends += [copy(a, 3, blk, sibling), copy(a, 5, part(blk, a, 0), (*y_nbr, mc))]
            sends[-2].start()
            sends[-1].start()
        for a in range(n):
            blk = o_refs[a].at[slot_of(y_nbr, mc)]
            copy(a, 2, blk, (*y_nbr, mc)).wait_recv()
            sends.append(copy(a, 4, blk, sibling))
            sends[-1].start()
            if halved[a]:
                sends.append(copy(a, 6, part(blk, a, 1), (*x_nbr, mc)))
                sends[-1].start()
        for a in range(n):
            blk = o_refs[a].at[slot_of(diag, mc)]
            copy(a, 5, part(blk, a, 0), (*y_nbr, mc)).wait_recv()
            if halved[a]:
                copy(a, 6, part(blk, a, 1), (*x_nbr, mc)).wait_recv()
            sends.append(copy(a, 7, blk, sibling))
            sends[-1].start()
        for a in range(n):
            copy(a, 0, o_refs[a].at[sibling_id], sibling).wait_recv()
            for k, chip in ((3, x_nbr), (4, y_nbr), (7, diag)):
                copy(a, k, o_refs[a].at[slot_of(chip, 1 - mc)], sibling).wait_recv()
        for cp in sends:
            cp.wait_send()
        for cp in local:
            cp.wait()

    any_spec = pl.BlockSpec(memory_space=pl.ANY)
    return _pcall(
        body, name=name,
        in_specs=[any_spec] * n, out_specs=[any_spec] * n,
        out_shape=[jax.ShapeDtypeStruct((N_DEV,) + x.shape, x.dtype) for x in xs],
        scratch_shapes=[pltpu.SemaphoreType.DMA((n, n_sems)), pltpu.SemaphoreType.DMA((n, n_sems)),
                        pltpu.SemaphoreType.DMA((n,))],
    )(*xs)


def _pair_exchange(gs, name):
    n = len(gs)

    def body(*refs):
        g_refs, o_refs = refs[:n], refs[n:2 * n]
        send_sems, recv_sems = refs[2 * n:]
        mx, my, mc, _ = _my_place()
        copies = [pltpu.make_async_remote_copy(
            src_ref=g_refs[a].at[p, 1 - mc], dst_ref=o_refs[a].at[p], send_sem=send_sems.at[a, p],
            recv_sem=recv_sems.at[a, p], device_id=(mx, my, 1 - mc), device_id_type=MESH)
            for a in range(n) for p in range(N_CHIPS)]
        for cp in copies:
            cp.start()
        for cp in copies:
            cp.wait()

    any_spec = pl.BlockSpec(memory_space=pl.ANY)
    return _pcall(
        body, name=name,
        in_specs=[any_spec] * n, out_specs=[any_spec] * n,
        out_shape=[jax.ShapeDtypeStruct((N_CHIPS,) + g.shape[2:], g.dtype) for g in gs],
        scratch_shapes=[pltpu.SemaphoreType.DMA((n, N_CHIPS)), pltpu.SemaphoreType.DMA((n, N_CHIPS))],
    )(*gs)


def _pair_add(g, other, name):
    chips, _, r, c = g.shape
    tr = _shard_row_tile(r)
    core = lax.axis_index("c").astype(jnp.int32).reshape(1)

    def body(core_ref, g_ref, o_ref, h_ref):
        h_ref[...] = (g_ref[...].astype(F32)[0] + o_ref[...].astype(F32)).astype(h_ref.dtype)

    blk = pl.BlockSpec((1, tr, c), lambda p, i, core_ref: (p, i, 0))
    return _pcall(
        body, name=name,
        grid_spec=pltpu.PrefetchScalarGridSpec(
            num_scalar_prefetch=1, grid=(chips, pl.cdiv(r, tr)),
            in_specs=[pl.BlockSpec((1, 1, tr, c), lambda p, i, core_ref: (p, core_ref[0], i, 0)), blk],
            out_specs=blk),
        out_shape=jax.ShapeDtypeStruct((chips, r, c), g.dtype),
        compiler_params=_params("parallel", "parallel"),
    )(core, g, other)


def _chip_exchange(hs, name):
    n = len(hs)

    def body(*refs):
        h_refs, o_refs = refs[:n], refs[n:2 * n]
        send_sems, recv_sems, local_sems = refs[2 * n:]
        mx, my, mc, _ = _my_place()
        my_chip = 2 * mx + my
        chips = _other_chips(mx, my)
        local = [pltpu.make_async_copy(h_refs[a].at[my_chip], o_refs[a].at[my_chip], local_sems.at[a]) for a in range(n)]
        for cp in local:
            cp.start()
        for j, (px, py) in enumerate(chips):
            for a in range(n):
                pltpu.make_async_remote_copy(
                    src_ref=h_refs[a].at[2 * px + py], dst_ref=o_refs[a].at[my_chip], send_sem=send_sems.at[a, j],
                    recv_sem=recv_sems.at[a, j], device_id=(px, py, mc), device_id_type=MESH).start()
        for j, (px, py) in enumerate(chips):
            for a in range(n):
                pltpu.make_async_remote_copy(
                    src_ref=h_refs[a].at[2 * px + py], dst_ref=o_refs[a].at[2 * px + py], send_sem=send_sems.at[a, j],
                    recv_sem=recv_sems.at[a, j], device_id=(px, py, mc), device_id_type=MESH).wait()
        for cp in local:
            cp.wait()

    any_spec = pl.BlockSpec(memory_space=pl.ANY)
    return _pcall(
        body, name=name,
        in_specs=[any_spec] * n, out_specs=[any_spec] * n,
        out_shape=[jax.ShapeDtypeStruct(h.shape, h.dtype) for h in hs],
        scratch_shapes=[pltpu.SemaphoreType.DMA((n, N_CHIPS - 1)), pltpu.SemaphoreType.DMA((n, N_CHIPS - 1)),
                        pltpu.SemaphoreType.DMA((n,))],
    )(*hs)


def _adamw(parts, w, m, v, name):
    r, c = w.shape
    n_parts = parts.shape[0]
    tr = _shard_row_tile(r)
    bc1 = 1.0 - ADAM_B1 ** ADAM_STEP
    bc2 = 1.0 - ADAM_B2 ** ADAM_STEP

    def body(p_ref, w_ref, m_ref, v_ref, g_ref, d_ref, nm_ref, nv_ref):
        g = p_ref[0].astype(F32)
        for j in range(1, n_parts):
            g = g + p_ref[j].astype(F32)
        nm = ADAM_B1 * m_ref[...] + (1.0 - ADAM_B1) * g
        nv = ADAM_B2 * v_ref[...] + (1.0 - ADAM_B2) * (g * g)
        g_ref[...] = g
        nm_ref[...] = nm
        nv_ref[...] = nv
        d_ref[...] = -ADAM_LR * ((nm / bc1) / (jnp.sqrt(nv / bc2) + ADAM_EPS) + ADAM_WD * w_ref[...])

    blk = pl.BlockSpec((tr, c), lambda i: (i, 0))
    return _pcall(
        body, name=name, grid=(pl.cdiv(r, tr),),
        in_specs=[pl.BlockSpec((n_parts, tr, c), lambda i: (0, i, 0)), blk, blk, blk],
        out_specs=[blk] * 4, out_shape=[jax.ShapeDtypeStruct((r, c), F32)] * 4,
        compiler_params=_params("parallel"),
    )(parts, w, m, v)


def _local_step(x, target, norm_w, w_segs, conv_w, a_log, dt_bias, dn_norm_w, w_o_dn, w_o_dil, w_out, final_norm_w):
    s = x.shape[0]
    w_qkv, w_za, w_ba, w_qb, w_kb, w_vb, w_zb, w_ga, w_gb = w_segs
    conv_w8 = jnp.concatenate([conv_w, jnp.zeros((SUBLANES - conv_w.shape[0], QKV_W), F32)], axis=0)
    pad8 = jnp.zeros((1, DN_HEADS), F32)
    alog_row = jnp.concatenate([pad8, a_log, jnp.zeros((1, LANES - 2 * DN_HEADS), F32)], axis=1)
    dtb_row = jnp.concatenate([pad8, dt_bias, jnp.zeros((1, LANES - 2 * DN_HEADS), F32)], axis=1)
    wf_row = final_norm_w.reshape(1, D_MODEL)

    hb, qkv_pre, z_a, ba, z_b = _rms_proj_fwd(x, norm_w, [w_qkv, w_za, w_ba, w_zb], "rms_proj_fwd_a")
    q_b, k_b, v_b, g_a, g_b = _mm_out(hb, [w_qb, w_kb, w_vb, w_ga, w_gb], "proj_fwd_b", w_is_out_by_in=True)

    u_d, w_d, qd_d, kd_d, aqk_d, dl_d, t2_d, qn, kn, vn, bg = _delta_prep(qkv_pre, ba, conv_w8, alog_row, dtb_row)
    parts, lses = [], []
    for gi in range(N_DIL):
        o_g, l_g = _attn_fwd(q_b, k_b, v_b, gi)
        parts.append(o_g)
        lses.append(l_g)
    o_a, vnew_d, st_d, on_b, y_a, lse, o_joint, ob_b, y_b = _delta_scan_fwd(
        u_d, w_d, qd_d, kd_d, aqk_d, dl_d, z_a, dn_norm_w, w_o_dn, parts, lses, z_b, w_o_dil)

    loss8, dwf8, merged_b, dx2_b, dx2, dya_b, dyb_b, dga_b, dgb_b = _merge_out_final(
        g_a, g_b, y_a, y_b, x, target, w_out, wf_row)

    g_w_out = _mm_tn(merged_b, dx2_b, "out_wgrad")
    g_w_o_dn = _mm_tn(on_b, dya_b, "out_dn_wgrad")

    g_w_o_dil = _mm_tn(ob_b, dyb_b, "out_dil_wgrad")
    dvnew_d, dkd_d, ddl_d, d_o_a, dza_b, ddnw8, d_o, dzb_b, delta = _delta_scan_bwd(
        w_d, qd_d, kd_d, aqk_d, dl_d, vnew_d, st_d, dya_b, o_a, z_a, dn_norm_w, w_o_dn, dyb_b, o_joint, z_b, w_o_dil)
    dqs, dks, dvs = [], [], []
    for gi in range(N_DIL):
        dq_g, dk_g, dv_g = _attn_bwd(q_b, k_b, v_b, d_o, lse, delta, gi)
        dqs.append(dq_g)
        dks.append(dk_g)
        dvs.append(dv_g)

    dqkv_b, dba_b, dsmall8, dconv8 = _delta_post_bwd(qn, kn, vn, bg, t2_d, st_d, vnew_d, d_o_a, dvnew_d, dkd_d, ddl_d,
                                                     qkv_pre, ba, conv_w8, alog_row, dtb_row)

    per_group = lambda w: [w[g * DIL_W:(g + 1) * DIL_W] for g in range(N_DIL)]
    dh_b = _mm_in(dqs + dks + dvs + [dga_b, dgb_b],
                  per_group(w_qb) + per_group(w_kb) + per_group(w_vb) + [w_ga, w_gb], "proj_bwd_b", w_is_out_by_in=True)
    dsegs = [dqkv_b, dza_b, dba_b] + dqs + dks + dvs + [dzb_b, dga_b, dgb_b]
    valid_rows = [d.shape[1] for d in dsegs]
    valid_rows[2] = 2 * DN_HEADS
    g_wt = _proj_wgrad_all(dsegs, valid_rows, hb)
    grad_x, dnw8 = _proj_bwd_rms_in([dqkv_b, dza_b, dba_b, dzb_b], [w_qkv, w_za, w_ba, w_zb], dh_b, x, dx2, norm_w)

    small = dict(norm_w=dnw8[0:1], final_norm_w=dwf8[0:1], dn_norm_w=ddnw8[0:1],
                 a_log=dsmall8[0:1, DN_HEADS:2 * DN_HEADS], dt_bias=dsmall8[1:2, DN_HEADS:2 * DN_HEADS])
    return loss8[0:1, 0:1], grad_x, g_wt, dconv8[0:4], g_w_o_dn, g_w_o_dil, g_w_out, small


def _proj_bwd_rms_in(ds, ws, dh_a, x, dx2, norm_w):
    n_seg = len(ds)

    def body(*refs):
        d_refs, w_refs = refs[:n_seg], refs[n_seg:2 * n_seg]
        da_ref, x_ref, dx2_ref, w_ref, dx_ref, dw_ref = refs[2 * n_seg:]
        dx_ref[...] = da_ref[...]
        for d_ref, wt_ref in zip(d_refs, w_refs):
            for c, wd in _col_chunks(d_ref.shape[1], 1024):
                dx_ref[...] += jnp.dot(d_ref[:, c:c + wd], wt_ref[c:c + wd, :], preferred_element_type=F32)
        xv = x_ref[...]
        r = lax.rsqrt(jnp.mean(xv * xv, axis=-1, keepdims=True) + NORM_EPS)
        dhv = dx_ref[...]
        dn = dhv * w_ref[...]
        dx_ref[...] = dx2_ref[...] + r * dn - xv * (r * r * r) * jnp.mean(dn * xv, axis=-1, keepdims=True)
        row = jnp.sum(dhv * xv * r, axis=0, keepdims=True)
        _acc_add(dw_ref, jnp.concatenate([row, jnp.zeros((SUBLANES - 1, row.shape[1]), F32)], axis=0))

    return _rows_call(body, "proj_bwd_b_rms_in", x.shape[0],
                      [(d, "tile") for d in ds] + [(w, "full") for w in ws]
                      + [(dh_a, "tile"), (x, "tile"), (dx2, "tile"), (norm_w, "full")],
                      [(x.shape, F32, "tile"), ((SUBLANES, x.shape[1]), F32, "acc")])


def _split_proj_rows(w_shards):
    n_shards, rows, k = w_shards.shape
    wt_full = w_shards.reshape(n_shards * rows, k)
    offs = [0]
    for n in PROJ_SIZES:
        offs.append(offs[-1] + n)
    seg = lambda a, b: wt_full[offs[a]:offs[b]]
    w_ba = jnp.concatenate([seg(4, 6), jnp.zeros((LANES - 2 * DN_HEADS, k), wt_full.dtype)], axis=0)
    return [seg(0, 3), seg(3, 4), w_ba, seg(6, 7), seg(7, 8), seg(8, 9), seg(9, 10), seg(10, 11), seg(11, 12)]


LOSS_ROW = 5


def _pack_small(norm_w, final_norm_w, dn_norm_w, a_log, dt_bias, loss=None):
    pad = lambda r: jnp.concatenate([r, jnp.zeros((1, D_MODEL - r.shape[1]), F32)], axis=1)
    rows = [pad(norm_w.reshape(1, -1)), pad(final_norm_w.reshape(1, -1)), pad(dn_norm_w.reshape(1, -1)),
            pad(a_log.reshape(1, -1)), pad(dt_bias.reshape(1, -1)),
            pad(jnp.zeros((1, 1), F32) if loss is None else loss.reshape(1, 1)),
            jnp.zeros((SUBLANES - LOSS_ROW - 1, D_MODEL), F32)]
    return jnp.concatenate(rows, axis=0)


def _unpack_small(p):
    return dict(norm_w=p[0:1], final_norm_w=p[1], dn_norm_w=p[2:3, :DN_DK], a_log=p[3:4, :DN_HEADS],
                dt_bias=p[4:5, :DN_HEADS])


def kernel(x, norm_w, w_in, conv_w, a_log, dt_bias, dn_norm_w, w_o_dn, w_o_dil, w_out, final_norm_w, loss_target, m_norm_w, m_w_in, m_conv_w, m_a_log, m_dt_bias, m_dn_norm_w, m_w_o_dn, m_w_o_dil, m_w_out, m_final_norm_w, v_norm_w, v_w_in, v_conv_w, v_a_log, v_dt_bias, v_dn_norm_w, v_w_o_dn, v_w_o_dil, v_w_out, v_final_norm_w):
    shard_w = w_in.shape[2]
    wt, m_wt, v_wt = (jnp.transpose(t[0]) for t in (w_in, m_w_in, v_w_in))
    gathered = _all_gather([wt.astype(MXU), w_o_dn[0].astype(MXU), w_o_dil[0].astype(MXU), w_out[0].astype(MXU),
                            conv_w[0]], "gather_weights")
    w_in_all, w_o_dn_all, w_o_dil_all, w_out_all, conv_all = gathered
    w_o_dn_full = w_o_dn_all.reshape(D_MODEL, D_MODEL)
    w_o_dil_full = jnp.transpose(w_o_dil_all, (1, 0, 2)).reshape(DIL_W, D_MODEL)
    w_out_full = w_out_all.reshape(D_MODEL, D_MODEL)
    conv_full = jnp.transpose(conv_all, (1, 0, 2)).reshape(conv_w.shape[1], QKV_W)

    loss11, grad_x, g_wt, g_conv, g_w_o_dn, g_w_o_dil, g_w_out, small = _local_step(
        x[0], loss_target[0], norm_w, _split_proj_rows(w_in_all), conv_full, a_log, dt_bias, dn_norm_w,
        w_o_dn_full, w_o_dil_full, w_out_full, final_norm_w)

    col_shards = lambda g, n: jnp.transpose(g.reshape(g.shape[0], N_DEV, n), (1, 0, 2))
    row_shards = lambda g: g.reshape(N_DEV, g.shape[0] // N_DEV, g.shape[1])
    g_wt_shards = jnp.stack([g_wt[j * shard_w:(j + 1) * shard_w] for j in range(N_DEV)], axis=0)
    sent = [g_wt_shards, row_shards(g_w_o_dn).astype(MXU),
            col_shards(g_w_o_dil, w_o_dil.shape[2]).astype(MXU), row_shards(g_w_out).astype(MXU),
            col_shards(g_conv, conv_w.shape[2])]
    sent = [g8.reshape((N_CHIPS, 2) + g8.shape[1:]) for g8 in sent]
    from_sibling = _pair_exchange(sent, "scatter_pair")
    summed = [_pair_add(g, o, f"pair_add_{i}") for i, (g, o) in enumerate(zip(sent, from_sibling))]
    p_w_in, p_w_o_dn, p_w_o_dil, p_w_out, p_conv = _chip_exchange(summed, "scatter_chips")
    p_small = _all_gather([_pack_small(small["norm_w"], small["final_norm_w"], small["dn_norm_w"], small["a_log"],
                                       small["dt_bias"], loss11)], "gather_small_grads")[0]

    res = {}
    res["w_in"] = [jnp.transpose(t) for t in _adamw(p_w_in, wt, m_wt, v_wt, "adamw_w_in")]
    res["conv_w"] = _adamw(p_conv, conv_w[0], m_conv_w[0], v_conv_w[0], "adamw_conv_w")
    res["w_o_dn"] = _adamw(p_w_o_dn, w_o_dn[0], m_w_o_dn[0], v_w_o_dn[0], "adamw_w_o_dn")
    res["w_o_dil"] = _adamw(p_w_o_dil, w_o_dil[0], m_w_o_dil[0], v_w_o_dil[0], "adamw_w_o_dil")
    res["w_out"] = _adamw(p_w_out, w_out[0], m_w_out[0], v_w_out[0], "adamw_w_out")
    small_res = _adamw(p_small, _pack_small(norm_w, final_norm_w, dn_norm_w, a_log, dt_bias),
                       _pack_small(m_norm_w, m_final_norm_w, m_dn_norm_w, m_a_log, m_dt_bias),
                       _pack_small(v_norm_w, v_final_norm_w, v_dn_norm_w, v_a_log, v_dt_bias), "adamw_small")
    loss = small_res[0][LOSS_ROW, 0]
    small_res = [_unpack_small(t) for t in small_res]

    names = ["norm_w", "w_in", "conv_w", "a_log", "dt_bias", "dn_norm_w", "w_o_dn", "w_o_dil", "w_out", "final_norm_w"]
    outs = [loss, grad_x[None]]
    for kind in range(4):
        for nm in names:
            outs.append(res[nm][kind][None] if nm in res else small_res[kind][nm])
    return tuple(outs)
```
